```python
import math
import jax, jax.numpy as jnp
from jax import lax
import numpy as np

D_MODEL = 1024
BATCH = 8
SEQ = 4096
DEPTH = 1

S5_WIDTH = 512
S5_GROUP = 16
S5_GROUPS = S5_WIDTH // S5_GROUP
S5_STATE = 64
DT_MIN = 1e-3
DT_MAX = 1e-1
N_HEADS = 8
N_KV_HEADS = 2
GQA_GROUP = N_HEADS // N_KV_HEADS
HEAD_DIM = 64
NSA_WIDTH = N_HEADS * HEAD_DIM
KV_WIDTH = 2 * N_KV_HEADS * HEAD_DIM
CMP_BLOCK = 32
CMP_STRIDE = 16
CMP_HIDDEN = 256
SEL_BLOCK = 64
SEL_TOPK = 8
WINDOW = 256
Q_BLOCK = 128
FORCE_BONUS = 1e3
N_EXPERTS = 64
TOP_K = 8
N_EXPERT_GROUPS = 8
TOPK_EXPERT_GROUPS = 4
EXPERT_HIDDEN = 256
SHARED_HIDDEN = 256
ROUTED_SCALE = 2.5
MOE_BLOCK = 128
RMS_EPS = 1e-6
NEG_INF = -1e30
N_MOD = 6
IN_SPLITS = (S5_WIDTH, NSA_WIDTH, KV_WIDTH, KV_WIDTH, KV_WIDTH, 3 * N_HEADS, 2 * D_MODEL)
IN_WIDTH = sum(IN_SPLITS)

kernel_name = "hybrid_s5_nsa_moe_adaln_block"


def rms_norm(x, gain):
    xf = x.astype(jnp.float32)
    y = xf * lax.rsqrt(jnp.mean(xf * xf, axis=-1, keepdims=True) + RMS_EPS)
    return (y * gain.astype(jnp.float32)).astype(x.dtype)


def masked_softmax(s, mask):
    s = jnp.where(mask, s, NEG_INF)
    m = jnp.max(s, axis=-1, keepdims=True)
    p = jnp.where(mask, jnp.exp(s - m), 0.0)
    return p / jnp.maximum(jnp.sum(p, axis=-1, keepdims=True), 1e-20)


def alibi_slopes(n_heads):
    return jnp.asarray(2.0 ** (-8.0 * np.arange(1, n_heads + 1) / n_heads), jnp.float32)


def s5_mixer(u, lam_re, lam_im, log_dt, b_re, b_im, c_re, c_im, d_skip, w_glu, b_glu):
    f32 = jnp.float32
    Bsz, S, _ = u.shape
    uf = u.astype(f32).reshape(Bsz, S, S5_GROUPS, S5_GROUP)
    lr, li = lam_re.astype(f32), lam_im.astype(f32)
    dt = jnp.exp(log_dt.astype(f32))[:, None]
    mag = jnp.exp(lr * dt)
    abar_re, abar_im = mag * jnp.cos(li * dt), mag * jnp.sin(li * dt)
    num_re, num_im = abar_re - 1.0, abar_im
    den = lr * lr + li * li
    coef_re = (num_re * lr + num_im * li) / den
    coef_im = (num_im * lr - num_re * li) / den
    br, bi = b_re.astype(f32), b_im.astype(f32)
    bbar_re = coef_re[..., None] * br - coef_im[..., None] * bi
    bbar_im = coef_re[..., None] * bi + coef_im[..., None] * br
    bu_re = jnp.einsum('bsgp,gnp->bsgn', uf, bbar_re)
    bu_im = jnp.einsum('bsgp,gnp->bsgn', uf, bbar_im)
    a_re = jnp.broadcast_to(abar_re[None, None], (1, S, S5_GROUPS, S5_STATE))
    a_im = jnp.broadcast_to(abar_im[None, None], (1, S, S5_GROUPS, S5_STATE))

    def combine(e1, e2):
        a1r, a1i, b1r, b1i = e1
        a2r, a2i, b2r, b2i = e2
        return (a2r * a1r - a2i * a1i,
                a2r * a1i + a2i * a1r,
                a2r * b1r - a2i * b1i + b2r,
                a2r * b1i + a2i * b1r + b2i)

    _, _, x_re, x_im = lax.associative_scan(combine, (a_re, a_im, bu_re, bu_im), axis=1)
    y = (jnp.einsum('bsgn,gpn->bsgp', x_re, c_re.astype(f32))
         - jnp.einsum('bsgn,gpn->bsgp', x_im, c_im.astype(f32))
         + d_skip.astype(f32) * uf).reshape(Bsz, S, S5_WIDTH)
    z = jax.nn.gelu(y)
    out = z * jax.nn.sigmoid(z @ w_glu.astype(f32) + b_glu.astype(f32))
    return out.astype(u.dtype)


def nsa_mixer(q, kv_c, kv_s, kv_w, g_nsa, q_gain, k_gain, cmp_pe, cmp_w1, cmp_b1, cmp_w2, cmp_b2):
    f32 = jnp.float32
    Bsz, S, _ = q.shape
    dt = q.dtype
    scale = HEAD_DIM ** -0.5
    q = rms_norm(q.reshape(Bsz, S, N_HEADS, HEAD_DIM), q_gain).reshape(
        Bsz, S, N_KV_HEADS, GQA_GROUP, HEAD_DIM)

    def split_kv(kv):
        kv = kv.reshape(Bsz, S, 2, N_KV_HEADS, HEAD_DIM)
        return kv[:, :, 0], kv[:, :, 1]

    k_c, v_c = split_kv(kv_c)
    k_s, v_s = split_kv(kv_s)
    k_w, v_w = split_kv(kv_w)
    k_s = rms_norm(k_s, k_gain[1])
    k_w = rms_norm(k_w, k_gain[2])

    n_cmp = (S - CMP_BLOCK) // CMP_STRIDE + 1
    cmp_start = np.arange(n_cmp) * CMP_STRIDE
    cmp_idx = cmp_start[:, None] + np.arange(CMP_BLOCK)[None, :]
    cmp_pos = jnp.asarray(cmp_idx[:, -1], jnp.int32)

    def compress(t, j):
        blocks = t[:, cmp_idx] + cmp_pe[j][None, None, :, None, :]
        flat = blocks.transpose(0, 1, 3, 2, 4).reshape(Bsz, n_cmp, N_KV_HEADS, CMP_BLOCK * HEAD_DIM)
        hid = jax.nn.gelu(flat @ cmp_w1[j] + cmp_b1[j])
        return hid @ cmp_w2[j] + cmp_b2[j]

    kc = rms_norm(compress(k_c, 0), k_gain[0])
    vc = compress(v_c, 1)

    n_sel = S // SEL_BLOCK
    n_pick = min(SEL_TOPK, n_sel)
    ks_blocks = k_s.reshape(Bsz, n_sel, SEL_BLOCK, N_KV_HEADS, HEAD_DIM).transpose(0, 3, 1, 2, 4)
    vs_blocks = v_s.reshape(Bsz, n_sel, SEL_BLOCK, N_KV_HEADS, HEAD_DIM).transpose(0, 3, 1, 2, 4)
    sel_start = np.arange(n_sel) * SEL_BLOCK
    overlap = jnp.asarray(((cmp_start[:, None] <= sel_start[None, :] + SEL_BLOCK - 1)
                           & (cmp_idx[:, -1][:, None] >= sel_start[None, :])).astype(np.float32))

    pad = ((0, 0), (WINDOW, 0), (0, 0), (0, 0))
    kw_pad, vw_pad = jnp.pad(k_w, pad), jnp.pad(v_w, pad)

    slopes = alibi_slopes(N_HEADS).reshape(N_KV_HEADS, GQA_GROUP)
    gates = jax.nn.sigmoid(g_nsa.astype(f32)).astype(dt).reshape(Bsz, S, 3, N_KV_HEADS, GQA_GROUP)
    gather = jax.vmap(jax.vmap(lambda blk, ix: blk[ix]))

    def attend_block(qb):
        t0 = qb * Q_BLOCK
        qq = lax.dynamic_slice_in_dim(q, t0, Q_BLOCK, axis=1)
        gg = lax.dynamic_slice_in_dim(gates, t0, Q_BLOCK, axis=1)
        t = t0 + jnp.arange(Q_BLOCK, dtype=jnp.int32)
        dist_c = (t[:, None] - cmp_pos[None, :]).astype(f32)
        s_c = (jnp.einsum('bqhgd,bchd->bhgqc', qq, kc).astype(f32) * scale
               - slopes[:, :, None, None] * dist_c)
        p_c = masked_softmax(s_c, dist_c >= 0)
        o_c = jnp.einsum('bhgqc,bchd->bqhgd', p_c.astype(dt), vc)
        imp = jnp.einsum('bhgqc,cj->bhqj', p_c, overlap)
        cur = t // SEL_BLOCK
        jb = jnp.arange(n_sel)
        forced = (jb[None, :] == 0) | (jb[None, :] == cur[:, None]) | (jb[None, :] == cur[:, None] - 1)
        imp = jnp.where(forced, imp + FORCE_BONUS, imp)
        imp = jnp.where(jb[None, :] <= cur[:, None], imp, -1.0)
        _, sel = lax.top_k(imp, n_pick)
        ksel = gather(ks_blocks, sel).reshape(Bsz, N_KV_HEADS, Q_BLOCK, n_pick * SEL_BLOCK, HEAD_DIM)
        vsel = gather(vs_blocks, sel).reshape(Bsz, N_KV_HEADS, Q_BLOCK, n_pick * SEL_BLOCK, HEAD_DIM)
        pos_s = (sel[..., None] * SEL_BLOCK + jnp.arange(SEL_BLOCK)).reshape(
            Bsz, N_KV_HEADS, Q_BLOCK, n_pick * SEL_BLOCK)
        dist_s = (t[None, None, :, None] - pos_s)[:, :, None]
        s_s = (jnp.einsum('bqhgd,bhqkd->bhgqk', qq, ksel).astype(f32) * scale
               - slopes[None, :, :, None, None] * dist_s.astype(f32))
        p_s = masked_softmax(s_s, dist_s >= 0)
        o_s = jnp.einsum('bhgqk,bhqkd->bqhgd', p_s.astype(dt), vsel)
        kw = lax.dynamic_slice_in_dim(kw_pad, t0, WINDOW + Q_BLOCK, axis=1)
        vw = lax.dynamic_slice_in_dim(vw_pad, t0, WINDOW + Q_BLOCK, axis=1)
        pos_w = t0 - WINDOW + jnp.arange(WINDOW + Q_BLOCK, dtype=jnp.int32)
        dist_w = t[:, None] - pos_w[None, :]
        mask_w = (dist_w >= 0) & (dist_w < WINDOW) & (pos_w[None, :] >= 0)
        s_w = (jnp.einsum('bqhgd,bkhd->bhgqk', qq, kw).astype(f32) * scale
               - slopes[:, :, None, None] * dist_w.astype(f32))
        p_w = masked_softmax(s_w, mask_w)
        o_w = jnp.einsum('bhgqk,bkhd->bqhgd', p_w.astype(dt), vw)
        o = (gg[:, :, 0, :, :, None] * o_c + gg[:, :, 1, :, :, None] * o_s
             + gg[:, :, 2, :, :, None] * o_w)
        return o.reshape(Bsz, Q_BLOCK, NSA_WIDTH)

    out = lax.map(attend_block, jnp.arange(S // Q_BLOCK))
    return out.transpose(1, 0, 2, 3).reshape(Bsz, S, NSA_WIDTH)


def moe_ffn(h, w_router, router_bias, w_gate, w_up, w_down, ws_gate, ws_up, ws_down):
    Bsz, S, D = h.shape
    T = Bsz * S
    xt = h.reshape(T, D)
    scores = jax.nn.sigmoid((xt @ w_router).astype(jnp.float32))
    sel = scores + router_bias.astype(jnp.float32)
    grp = sel.reshape(T, N_EXPERT_GROUPS, N_EXPERTS // N_EXPERT_GROUPS)
    grp_score = lax.top_k(grp, 2)[0].sum(-1)
    _, top_groups = lax.top_k(grp_score, TOPK_EXPERT_GROUPS)
    group_mask = jnp.any(top_groups[..., None] == jnp.arange(N_EXPERT_GROUPS), axis=-2)
    expert_mask = jnp.repeat(group_mask, N_EXPERTS // N_EXPERT_GROUPS, axis=-1)
    _, top_e = lax.top_k(jnp.where(expert_mask, sel, NEG_INF), TOP_K)
    w = jnp.take_along_axis(scores, top_e, axis=-1)
    w = w / jnp.sum(w, axis=-1, keepdims=True) * ROUTED_SCALE

    A = T * TOP_K
    flat_e = top_e.reshape(A)
    flat_tok = jnp.repeat(jnp.arange(T, dtype=jnp.int32), TOP_K)
    flat_w = w.reshape(A)
    order = jnp.argsort(flat_e)
    e_sorted, tok_sorted, w_sorted = flat_e[order], flat_tok[order], flat_w[order]
    counts = jnp.bincount(flat_e, length=N_EXPERTS)
    padded = (counts + MOE_BLOCK - 1) // MOE_BLOCK * MOE_BLOCK
    pad_end = jnp.cumsum(padded)
    pad_start = pad_end - padded
    start = jnp.cumsum(counts) - counts
    dest = pad_start[e_sorted] + (jnp.arange(A) - start[e_sorted])
    P = A + N_EXPERTS * MOE_BLOCK
    n_blocks = P // MOE_BLOCK
    buf_tok = jnp.zeros((P,), jnp.int32).at[dest].set(tok_sorted)
    buf_w = jnp.zeros((P,), jnp.float32).at[dest].set(w_sorted)
    block_expert = jnp.minimum(
        jnp.searchsorted(pad_end, jnp.arange(n_blocks) * MOE_BLOCK, side='right'), N_EXPERTS - 1)

    def expert_block(args):
        tok, e, wt = args
        rows = xt[tok]
        hid = jax.nn.silu(rows @ w_gate[e]) * (rows @ w_up[e])
        return (hid @ w_down[e]) * wt[:, None].astype(rows.dtype)

    out = lax.map(expert_block, (buf_tok.reshape(n_blocks, MOE_BLOCK), block_expert,
                                 buf_w.reshape(n_blocks, MOE_BLOCK)))
    routed = jnp.zeros((T, D), xt.dtype).at[buf_tok].add(out.reshape(P, D))
    shared = (jax.nn.silu(xt @ ws_gate) * (xt @ ws_up)) @ ws_down
    return (routed + shared).reshape(Bsz, S, D)


def setup_inputs(seed: int = 0) -> dict:
    key = jax.random.key(seed)
    ks = iter(jax.random.split(key, 40))

    def nrm(shape, scale):
        return jax.random.normal(next(ks), shape, jnp.float32) * scale

    L, D = DEPTH, D_MODEL
    G, N, Pc = S5_GROUPS, S5_STATE, S5_GROUP
    n_idx = jnp.arange(N, dtype=jnp.float32)
    return {
        "x": nrm((BATCH, SEQ, D), 1.0),
        "c": nrm((BATCH, D), 1.0),
        "w_ada": nrm((L, D, N_MOD * D), 0.5 * D ** -0.5),
        "b_ada": nrm((L, N_MOD * D), 0.01),
        "norm_mix_gain": 1.0 + nrm((L, D), 0.01),
        "norm_ffn_gain": 1.0 + nrm((L, D), 0.01),
        "w_in": nrm((L, D, IN_WIDTH), D ** -0.5),
        "s5_lambda_re": -0.5 * (1.0 + nrm((L, G, N), 0.01)),
        "s5_lambda_im": math.pi * n_idx + nrm((L, G, N), 0.01),
        "s5_log_dt": jax.random.uniform(next(ks), (L, G), jnp.float32,
                                        math.log(DT_MIN), math.log(DT_MAX)),
        "s5_b_re": nrm((L, G, N, Pc), Pc ** -0.5),
        "s5_b_im": nrm((L, G, N, Pc), Pc ** -0.5),
        "s5_c_re": nrm((L, G, Pc, N), N ** -0.5),
        "s5_c_im": nrm((L, G, Pc, N), N ** -0.5),
        "s5_d": nrm((L, G, Pc), 0.5),
        "s5_w_glu": nrm((L, S5_WIDTH, S5_WIDTH), S5_WIDTH ** -0.5),
        "s5_b_glu": nrm((L, S5_WIDTH), 0.01),
        "q_norm_gain": 1.0 + nrm((L, HEAD_DIM), 0.01),
        "k_norm_gain": 1.0 + nrm((L, 3, HEAD_DIM), 0.01),
        "cmp_pe": nrm((L, 2, CMP_BLOCK, HEAD_DIM), 0.1),
        "cmp_w1": nrm((L, 2, CMP_BLOCK * HEAD_DIM, CMP_HIDDEN), (CMP_BLOCK * HEAD_DIM) ** -0.5),
        "cmp_b1": nrm((L, 2, CMP_HIDDEN), 0.01),
        "cmp_w2": nrm((L, 2, CMP_HIDDEN, HEAD_DIM), CMP_HIDDEN ** -0.5),
        "cmp_b2": nrm((L, 2, HEAD_DIM), 0.01),
        "w_branch_a": nrm((L, S5_WIDTH, D), S5_WIDTH ** -0.5),
        "w_branch_b": nrm((L, NSA_WIDTH, D), NSA_WIDTH ** -0.5),
        "w_out": nrm((L, D, D), D ** -0.5),
        "w_router": nrm((L, D, N_EXPERTS), D ** -0.5),
        "router_bias": nrm((L, N_EXPERTS), 0.01),
        "w_gate": nrm((L, N_EXPERTS, D, EXPERT_HIDDEN), D ** -0.5),
        "w_up": nrm((L, N_EXPERTS, D, EXPERT_HIDDEN), D ** -0.5),
        "w_down": nrm((L, N_EXPERTS, EXPERT_HIDDEN, D), EXPERT_HIDDEN ** -0.5),
        "ws_gate": nrm((L, D, SHARED_HIDDEN), D ** -0.5),
        "ws_up": nrm((L, D, SHARED_HIDDEN), D ** -0.5),
        "ws_down": nrm((L, SHARED_HIDDEN, D), SHARED_HIDDEN ** -0.5),
    }


def reference(x, c, w_ada, b_ada, norm_mix_gain, norm_ffn_gain, w_in,
              s5_lambda_re, s5_lambda_im, s5_log_dt, s5_b_re, s5_b_im, s5_c_re, s5_c_im,
              s5_d, s5_w_glu, s5_b_glu, q_norm_gain, k_norm_gain,
              cmp_pe, cmp_w1, cmp_b1, cmp_w2, cmp_b2,
              w_branch_a, w_branch_b, w_out, w_router, router_bias,
              w_gate, w_up, w_down, ws_gate, ws_up, ws_down):
    split_points = list(np.cumsum(IN_SPLITS)[:-1])
    for l in range(DEPTH):
        mod = jax.nn.silu(c) @ w_ada[l] + b_ada[l]
        shift_m, scale_m, gate_m, shift_f, scale_f, gate_f = jnp.split(mod[:, None, :], N_MOD, axis=-1)
        h = rms_norm(x, norm_mix_gain[l]) * (1.0 + scale_m) + shift_m
        u_s5, q, kv_c, kv_s, kv_w, g_nsa, g_merge = jnp.split(h @ w_in[l], split_points, axis=-1)
        y_a = s5_mixer(u_s5, s5_lambda_re[l], s5_lambda_im[l], s5_log_dt[l], s5_b_re[l], s5_b_im[l],
                       s5_c_re[l], s5_c_im[l], s5_d[l], s5_w_glu[l], s5_b_glu[l]) @ w_branch_a[l]
        y_b = nsa_mixer(q, kv_c, kv_s, kv_w, g_nsa, q_norm_gain[l], k_norm_gain[l],
                        cmp_pe[l], cmp_w1[l], cmp_b1[l], cmp_w2[l], cmp_b2[l]) @ w_branch_b[l]
        g_a, g_b = jnp.split(jax.nn.sigmoid(g_merge), 2, axis=-1)
        x = x + gate_m * ((g_a * y_a + g_b * y_b) @ w_out[l])
        h = rms_norm(x, norm_ffn_gain[l]) * (1.0 + scale_f) + shift_f
        x = x + gate_f * moe_ffn(h, w_router[l], router_bias[l], w_gate[l], w_up[l], w_down[l],
                                 ws_gate[l], ws_up[l], ws_down[l])
    return x
```

```python
import functools
import math

import numpy as np
import jax
import jax.numpy as jnp
from jax import lax
from jax.experimental import pallas as pl
from jax.experimental.pallas import tpu as pltpu

F32 = jnp.float32
BF16 = jnp.bfloat16

D_MODEL = 1024
S5_WIDTH = 512
S5_GROUP = 16
S5_GROUPS = S5_WIDTH // S5_GROUP
S5_STATE = 64
N_HEADS = 8
N_KV_HEADS = 2
GQA_GROUP = N_HEADS // N_KV_HEADS
HEAD_DIM = 64
NSA_WIDTH = N_HEADS * HEAD_DIM
KV_WIDTH = 2 * N_KV_HEADS * HEAD_DIM
CMP_BLOCK = 32
CMP_STRIDE = 16
CMP_HIDDEN = 256
SEL_BLOCK = 64
SEL_TOPK = 8
WINDOW = 256
Q_BLOCK = 128
FORCE_BONUS = 1e3
N_EXPERTS = 64
TOP_K = 8
N_EXPERT_GROUPS = 8
TOPK_EXPERT_GROUPS = 4
EXPERT_HIDDEN = 256
SHARED_HIDDEN = 256
ROUTED_SCALE = 2.5
RMS_EPS = 1e-6
NEG_INF = -1e30
N_MOD = 6

S5_CHUNK = 16
S5_CW = S5_CHUNK * S5_GROUP
ROW_TILE = 512
SEL_KEY_CHUNK = 256
ROUTE_TILE = 512
MOE_OUTER = 1024
MOE_EXPERTS_PER_STEP = 4
MOE_CAP = 128
VMEM_LIMIT = 56 * 1024 * 1024


def _dot(a, b):
    return jnp.dot(a, b, preferred_element_type=F32)


def _split(a):
    hi = a.astype(BF16)
    lo = (a - hi.astype(F32)).astype(BF16)
    return hi, lo


def _dot3(a, bh, bl):
    ah, al = _split(a)
    return _dot(ah, bh) + (_dot(al, bh) + _dot(ah, bl))


def _params(sem):
    return pltpu.CompilerParams(dimension_semantics=sem, vmem_limit_bytes=VMEM_LIMIT)


def _ada_kernel(c_ref, w_ref, b_ref, o_ref):
    cs = jax.nn.silu(c_ref[...])
    wh, wl = _split(w_ref[...])
    o_ref[...] = _dot3(cs, wh, wl) + b_ref[...]


def _ada(c, w_ada, b_ada):
    B, D = c.shape
    return pl.pallas_call(
        _ada_kernel,
        grid=(N_MOD,),
        in_specs=[pl.BlockSpec((B, D), lambda j: (0, 0)),
                  pl.BlockSpec((D, D), lambda j: (0, j)),
                  pl.BlockSpec((1, D), lambda j: (0, j))],
        out_specs=pl.BlockSpec((B, D), lambda j: (0, j)),
        out_shape=jax.ShapeDtypeStruct((B, N_MOD * D), F32),
        compiler_params=_params(("arbitrary",)),
        name="ada",
    )(c, w_ada, b_ada.reshape(1, N_MOD * D))


def _head_norm(v, bd, gain):
    sq = v * v
    sh, sl = _split(sq)
    ms = _dot(sh, bd) + _dot(sl, bd)
    return v * lax.rsqrt(ms + RMS_EPS) * gain


def _inproj_kernel(x_ref, mod_ref, gain_ref, wm_ref, wgn_ref, wgm_ref, bd_ref, qg_ref, kg_ref,
                   u_ref, q_ref, kvc_ref, kvs_ref, kvw_ref, gn_ref, gm_ref):
    x = x_ref[...]
    shift = mod_ref[0:1, :]
    scale = mod_ref[1:2, :]
    y = x * lax.rsqrt(jnp.mean(x * x, axis=-1, keepdims=True) + RMS_EPS)
    h = (y * gain_ref[...]) * (1.0 + scale) + shift
    hb = h.astype(BF16)
    main = _dot(hb, wm_ref[...])
    u_ref[...] = main[:, :S5_WIDTH]
    o = S5_WIDTH
    q = main[:, o:o + NSA_WIDTH]
    o += NSA_WIDTH
    kvc_ref[...] = main[:, o:o + KV_WIDTH]
    o += KV_WIDTH
    kvs = main[:, o:o + KV_WIDTH]
    o += KV_WIDTH
    kvw = main[:, o:o + KV_WIDTH]
    bd = bd_ref[...]
    kw = N_KV_HEADS * HEAD_DIM
    q_ref[...] = (_head_norm(q, bd, qg_ref[...]) * (HEAD_DIM ** -0.5)).astype(BF16)
    kvs_ref[:, :kw] = _head_norm(kvs[:, :kw], bd[:kw, :kw], kg_ref[1:2, :]).astype(BF16)
    kvs_ref[:, kw:] = kvs[:, kw:].astype(BF16)
    kvw_ref[:, :kw] = _head_norm(kvw[:, :kw], bd[:kw, :kw], kg_ref[2:3, :]).astype(BF16)
    kvw_ref[:, kw:] = kvw[:, kw:].astype(BF16)
    gn_ref[...] = jax.nn.sigmoid(_dot(hb, wgn_ref[...]))
    gm_ref[...] = jax.nn.sigmoid(_dot(hb, wgm_ref[...])).astype(BF16)


def _inproj(x, mod, gain, w_in, q_gain, k_gain):
    B, S, D = x.shape
    tm = min(ROW_TILE, S)
    n_main = S5_WIDTH + NSA_WIDTH + 3 * KV_WIDTH
    wm = w_in[:, :n_main].astype(BF16)
    wgn = w_in[:, n_main:n_main + 3 * N_HEADS].astype(BF16)
    wgm = w_in[:, n_main + 3 * N_HEADS:].astype(BF16)
    seg = np.arange(NSA_WIDTH) // HEAD_DIM
    bd = jnp.asarray((seg[:, None] == seg[None, :]).astype(np.float32) / HEAD_DIM, BF16)
    qg = jnp.tile(q_gain, N_HEADS).reshape(1, NSA_WIDTH)
    kg = jnp.tile(k_gain, (1, N_KV_HEADS))
    row = lambda w: pl.BlockSpec((None, tm, w), lambda b, i: (b, i, 0))
    full = lambda a: pl.BlockSpec(a.shape, lambda b, i: (0,) * a.ndim)
    return pl.pallas_call(
        _inproj_kernel,
        grid=(B, S // tm),
        in_specs=[row(D), pl.BlockSpec((None, N_MOD, D), lambda b, i: (b, 0, 0)),
                  full(gain), full(wm), full(wgn), full(wgm), full(bd), full(qg), full(kg)],
        out_specs=[row(S5_WIDTH), row(NSA_WIDTH), row(KV_WIDTH), row(KV_WIDTH), row(KV_WIDTH),
                   row(3 * N_HEADS), row(2 * D)],
        out_shape=[jax.ShapeDtypeStruct((B, S, S5_WIDTH), F32),
                   jax.ShapeDtypeStruct((B, S, NSA_WIDTH), BF16),
                   jax.ShapeDtypeStruct((B, S, KV_WIDTH), F32),
                   jax.ShapeDtypeStruct((B, S, KV_WIDTH), BF16),
                   jax.ShapeDtypeStruct((B, S, KV_WIDTH), BF16),
                   jax.ShapeDtypeStruct((B, S, 3 * N_HEADS), F32),
                   jax.ShapeDtypeStruct((B, S, 2 * D), BF16)],
        compiler_params=_params(("arbitrary", "arbitrary")),
        name="inproj",
    )(x, mod, gain, wm, wgn, wgm, bd, qg, kg)


def _s5_tables(lam_re, lam_im, log_dt, b_re, b_im, c_re, c_im, d_skip):
    L, P, N, G = S5_CHUNK, S5_GROUP, S5_STATE, S5_GROUPS
    hp = lax.Precision.HIGHEST
    lr, li = lam_re.astype(F32), lam_im.astype(F32)
    dt = jnp.exp(log_dt.astype(F32))[:, None]
    mag = jnp.exp(lr * dt)
    abar_re, abar_im = mag * jnp.cos(li * dt), mag * jnp.sin(li * dt)
    num_re, num_im = abar_re - 1.0, abar_im
    den = lr * lr + li * li
    coef_re = (num_re * lr + num_im * li) / den
    coef_im = (num_im * lr - num_re * li) / den
    br, bi = b_re.astype(F32), b_im.astype(F32)
    bbar_re = coef_re[..., None] * br - coef_im[..., None] * bi
    bbar_im = coef_re[..., None] * bi + coef_im[..., None] * br
    k = jnp.arange(L + 1, dtype=F32)[:, None, None]
    pmag = jnp.exp(lr * dt * k)
    pre, pim = pmag * jnp.cos(li * dt * k), pmag * jnp.sin(li * dt * k)
    cr, ci = c_re.astype(F32), c_im.astype(F32)
    ca_re = cr[None] * pre[:, :, None, :] - ci[None] * pim[:, :, None, :]
    ca_im = cr[None] * pim[:, :, None, :] + ci[None] * pre[:, :, None, :]
    kern = (jnp.einsum('kgpn,gnq->gkqp', ca_re[:L], bbar_re, precision=hp)
            - jnp.einsum('kgpn,gnq->gkqp', ca_im[:L], bbar_im, precision=hp))
    s_i = np.arange(L)[:, None]
    t_i = np.arange(L)[None, :]
    tau = np.clip(t_i - s_i, 0, L - 1)
    causal = jnp.asarray((t_i >= s_i).astype(np.float32))
    mt = kern[:, tau] * causal[None, :, :, None, None]
    mt = mt.transpose(0, 1, 3, 2, 4).reshape(G, L * P, L * P)
    rev = np.arange(L - 1, -1, -1)
    ab_re = pre[rev][..., None] * bbar_re[None] - pim[rev][..., None] * bbar_im[None]
    ab_im = pre[rev][..., None] * bbar_im[None] + pim[rev][..., None] * bbar_re[None]
    ws = jnp.concatenate([ab_re, ab_im], axis=2)
    ws = ws.transpose(1, 0, 3, 2).reshape(G, L * P, 2 * N)
    wo = jnp.concatenate([ca_re[1:], -ca_im[1:]], axis=3)
    wo = wo.transpose(1, 3, 0, 2).reshape(G, 2 * N, L * P)
    al = jnp.stack([jnp.concatenate([pre[L], pre[L]], axis=-1),
                    jnp.concatenate([-pim[L], pim[L]], axis=-1)], axis=1)
    dv = jnp.tile(d_skip.astype(F32), (1, L)).reshape(G, 1, L * P)
    return mt, ws, wo, al, dv


def _s5_kernel(u_ref, mth_ref, mtl_ref, wsh_ref, wsl_ref, woh_ref, wol_ref, al_ref, dv_ref,
               y_ref, v_scr, xp_scr, *, n_chunks, bsz):
    u = u_ref[...]
    uh, ul = _split(u)
    y = _dot(uh, mth_ref[...]) + (_dot(ul, mth_ref[...]) + _dot(uh, mtl_ref[...]))
    v_scr[...] = _dot(uh, wsh_ref[...]) + (_dot(ul, wsh_ref[...]) + _dot(uh, wsl_ref[...]))
    a_r = al_ref[0:1, :]
    a_i = al_ref[1:2, :]

    def step(c, x):
        r = pl.ds(pl.multiple_of(c * bsz, bsz), bsz)
        xp_scr[r, :] = x
        return a_r * x + a_i * pltpu.roll(x, S5_STATE, axis=1) + v_scr[r, :]

    lax.fori_loop(0, n_chunks, step, jnp.zeros((bsz, 2 * S5_STATE), F32), unroll=8)
    y = y + _dot3(xp_scr[...], woh_ref[...], wol_ref[...])
    y_ref[...] = y + dv_ref[...] * u


def _s5(u, tables):
    B, S, _ = u.shape
    L, P, G, N = S5_CHUNK, S5_GROUP, S5_GROUPS, S5_STATE
    nc = S // L
    mt, ws, wo, al, dv = tables
    ug = u.reshape(B, nc, L, G, P).transpose(3, 1, 0, 2, 4).reshape(G, nc * B, L * P)
    mth, mtl = _split(mt)
    wsh, wsl = _split(ws)
    woh, wol = _split(wo)
    grp = lambda a: pl.BlockSpec((None,) + a.shape[1:], lambda g: (g,) + (0,) * (a.ndim - 1))
    y = pl.pallas_call(
        functools.partial(_s5_kernel, n_chunks=nc, bsz=B),
        grid=(G,),
        in_specs=[grp(ug), grp(mth), grp(mtl), grp(wsh), grp(wsl), grp(woh), grp(wol), grp(al), grp(dv)],
        out_specs=grp(ug),
        out_shape=jax.ShapeDtypeStruct(ug.shape, F32),
        scratch_shapes=[pltpu.VMEM((nc * B, 2 * N), F32), pltpu.VMEM((nc * B, 2 * N), F32)],
        compiler_params=_params(("arbitrary",)),
        name="s5",
    )(ug, mth, mtl, wsh, wsl, woh, wol, al, dv)
    return y.reshape(G, nc, B, L, P).transpose(2, 1, 3, 0, 4).reshape(B, S, S5_WIDTH)


def _compress_kernel(x_ref, pe_ref, w1_ref, b1_ref, w2_ref, b2_ref, kg_ref, o_ref, *, n_rows):
    x = x_ref[...]
    half = CMP_STRIDE * HEAD_DIM
    a = _dot((x + pe_ref[0:1, :]).astype(BF16), w1_ref[:half, :])
    b = _dot((x + pe_ref[1:2, :]).astype(BF16), w1_ref[half:, :])
    hid = jax.nn.gelu(a + pltpu.roll(b, n_rows - 1, axis=0) + b1_ref[...])
    out = _dot(hid.astype(BF16), w2_ref[...]) + b2_ref[...]
    normed = out * lax.rsqrt(jnp.mean(out * out, axis=-1, keepdims=True) + RMS_EPS) * kg_ref[...]
    is_key = pl.program_id(1) < N_KV_HEADS
    o_ref[...] = jnp.where(is_key, normed, out).astype(BF16)


def _compress(kv_c, cmp_pe, cmp_w1, cmp_b1, cmp_w2, cmp_b2, k_gain0):
    B, S, _ = kv_c.shape
    nr = S // CMP_STRIDE
    half = CMP_STRIDE * HEAD_DIM
    nj = 2 * N_KV_HEADS
    xc = kv_c.reshape(B, nr, CMP_STRIDE, nj, HEAD_DIM).transpose(0, 3, 1, 2, 4).reshape(B, nj, nr, half)
    pe = cmp_pe.reshape(2, 2, half)
    w1 = cmp_w1.astype(BF16)
    w2 = cmp_w2.astype(BF16)
    b1 = cmp_b1.reshape(2, 1, CMP_HIDDEN)
    b2 = cmp_b2.reshape(2, 1, HEAD_DIM)
    kv = lambda a: pl.BlockSpec((None,) + a.shape[1:], lambda b, j: (j // N_KV_HEADS,) + (0,) * (a.ndim - 1))
    return pl.pallas_call(
        functools.partial(_compress_kernel, n_rows=nr),
        grid=(B, nj),
        in_specs=[pl.BlockSpec((None, None, nr, half), lambda b, j: (b, j, 0, 0)),
                  kv(pe), kv(w1), kv(b1), kv(w2), kv(b2),
                  pl.BlockSpec((1, HEAD_DIM), lambda b, j: (0, 0))],
        out_specs=pl.BlockSpec((None, None, nr, HEAD_DIM), lambda b, j: (b, j, 0, 0)),
        out_shape=jax.ShapeDtypeStruct((B, nj, nr, HEAD_DIM), BF16),
        compiler_params=_params(("arbitrary", "arbitrary")),
        name="compress",
    )(xc, pe, w1, b1, w2, b2, k_gain0.reshape(1, HEAD_DIM))


def _softmax_cols(s, mask):
    s = jnp.where(mask, s, NEG_INF)
    m = jnp.max(s, axis=0, keepdims=True)
    p = jnp.where(mask, jnp.exp(s - m), 0.0)
    return p / jnp.maximum(jnp.sum(p, axis=0, keepdims=True), 1e-20)


def _attend_kernel(qt_ref, kc_ref, vct_ref, ks_ref, vst_ref, kw_ref, vwt_ref, g_ref, slope_ref, ovt_ref,
                   o_ref, *, n_sel, n_pick, n_cmp_rows):
    QW = GQA_GROUP * Q_BLOCK
    i = pl.program_id(2)
    t0 = i * Q_BLOCK
    qt = qt_ref[...]
    slope = slope_ref[...]
    tq = t0 + (lax.broadcasted_iota(jnp.int32, (1, QW), 1) & (Q_BLOCK - 1))

    sc = _dot(kc_ref[...], qt)
    cpos = lax.broadcasted_iota(jnp.int32, (n_cmp_rows, QW), 0) * CMP_STRIDE + (CMP_BLOCK - 1)
    dist = (tq - cpos).astype(F32)
    p_c = _softmax_cols(sc - slope * dist, dist >= 0.0)
    o_c = _dot(vct_ref[...], p_c.astype(BF16))
    psum = p_c[:, 0:Q_BLOCK]
    for g in range(1, GQA_GROUP):
        psum = psum + p_c[:, g * Q_BLOCK:(g + 1) * Q_BLOCK]
    ph, pl_ = _split(psum)
    imp = _dot(ovt_ref[...], ph) + _dot(ovt_ref[...], pl_)

    jb = lax.broadcasted_iota(jnp.int32, (n_sel, Q_BLOCK), 0)
    cur = (t0 + lax.broadcasted_iota(jnp.int32, (1, Q_BLOCK), 1)) // SEL_BLOCK
    forced = (jb == 0) | (jb == cur) | (jb == cur - 1)
    imp = jnp.where(forced, imp + FORCE_BONUS, imp)
    imp = jnp.where(jb <= cur, imp, -1.0)
    sel = jnp.zeros((n_sel, Q_BLOCK), F32)
    for _ in range(n_pick):
        m = jnp.max(imp, axis=0, keepdims=True)
        first = jnp.min(jnp.where(imp == m, jb, n_sel), axis=0, keepdims=True)
        hit = jb == first
        sel = jnp.where(hit, 1.0, sel)
        imp = jnp.where(hit, -jnp.inf, imp)
    sel_b = sel.astype(BF16)

    KC = SEL_KEY_CHUNK
    n_chunks = (t0 + Q_BLOCK + KC - 1) // KC

    def sel_step(j, carry):
        m, l, acc = carry
        k0 = pl.multiple_of(j * KC, KC)
        s = _dot(ks_ref[pl.ds(k0, KC), :], qt)
        kpos = k0 + lax.broadcasted_iota(jnp.int32, (KC, QW), 0)
        dist = tq - kpos
        kblk = (k0 + lax.broadcasted_iota(jnp.int32, (KC, n_sel), 0)) // SEL_BLOCK
        et = jnp.where(kblk == lax.broadcasted_iota(jnp.int32, (KC, n_sel), 1), 1.0, 0.0).astype(BF16)
        picked = _dot(et, sel_b)
        picked = jnp.concatenate([picked] * GQA_GROUP, axis=1)
        valid = jnp.where(dist >= 0, picked, 0.0) > 0.5
        s = jnp.where(valid, s - slope * dist.astype(F32), NEG_INF)
        m_new = jnp.maximum(m, jnp.max(s, axis=0, keepdims=True))
        alpha = jnp.exp(m - m_new)
        p = jnp.where(valid, jnp.exp(s - m_new), 0.0)
        l = alpha * l + jnp.sum(p, axis=0, keepdims=True)
        acc = alpha * acc + _dot(vst_ref[:, pl.ds(k0, KC)], p.astype(BF16))
        return m_new, l, acc

    init = (jnp.full((1, QW), NEG_INF, F32), jnp.zeros((1, QW), F32), jnp.zeros((HEAD_DIM, QW), F32))
    _, l_s, acc_s = lax.fori_loop(0, n_chunks, sel_step, init)
    o_s = acc_s / jnp.maximum(l_s, 1e-20)

    WK = WINDOW + Q_BLOCK
    w0 = pl.multiple_of(jnp.maximum(t0 - WINDOW, 0), Q_BLOCK)
    sw = _dot(kw_ref[pl.ds(w0, WK), :], qt)
    dist = (tq - (w0 + lax.broadcasted_iota(jnp.int32, (WK, QW), 0))).astype(F32)
    in_win = jnp.abs(dist - (WINDOW - 1) / 2.0) < WINDOW / 2.0
    p_w = _softmax_cols(sw - slope * dist, in_win)
    o_w = _dot(vwt_ref[:, pl.ds(w0, WK)], p_w.astype(BF16))

    o = g_ref[0:1, :] * o_c + g_ref[1:2, :] * o_s + g_ref[2:3, :] * o_w
    o_ref[...] = o.astype(BF16)


def _attend(qn, kvc_c, kvs, kvw, gates):
    B, S, _ = qn.shape
    nqb = S // Q_BLOCK
    QW = GQA_GROUP * Q_BLOCK
    n_sel = S // SEL_BLOCK
    n_pick = min(SEL_TOPK, n_sel)
    nr = S // CMP_STRIDE
    n_cmp = (S - CMP_BLOCK) // CMP_STRIDE + 1
    kw = N_KV_HEADS * HEAD_DIM

    qt = qn.reshape(B, nqb, Q_BLOCK, N_KV_HEADS, GQA_GROUP, HEAD_DIM).transpose(0, 3, 5, 1, 4, 2)
    qt = qt.reshape(B, N_KV_HEADS, HEAD_DIM, nqb * QW)
    kc = kvc_c[:, :N_KV_HEADS]
    vct = kvc_c[:, N_KV_HEADS:].transpose(0, 1, 3, 2)
    heads = lambda a: a.reshape(B, S, N_KV_HEADS, HEAD_DIM)
    ks = heads(kvs[..., :kw]).transpose(0, 2, 1, 3)
    vst = heads(kvs[..., kw:]).transpose(0, 2, 3, 1)
    kwn = heads(kvw[..., :kw]).transpose(0, 2, 1, 3)
    vwt = heads(kvw[..., kw:]).transpose(0, 2, 3, 1)
    gt = gates.reshape(B, nqb, Q_BLOCK, 3, N_KV_HEADS, GQA_GROUP).transpose(0, 4, 3, 1, 5, 2)
    gt = gt.reshape(B, N_KV_HEADS, 3, nqb * QW)
    slopes = 2.0 ** (-8.0 * np.arange(1, N_HEADS + 1) / N_HEADS)
    slope_t = jnp.asarray(np.repeat(slopes.reshape(N_KV_HEADS, GQA_GROUP), Q_BLOCK, axis=1)
                          .reshape(N_KV_HEADS, 1, QW), F32)
    cmp_start = np.arange(n_cmp) * CMP_STRIDE
    sel_start = np.arange(n_sel) * SEL_BLOCK
    ov = ((cmp_start[:, None] <= sel_start[None, :] + SEL_BLOCK - 1)
          & (cmp_start[:, None] + CMP_BLOCK - 1 >= sel_start[None, :])).astype(np.float32)
    ovt = np.zeros((n_sel, nr), np.float32)
    ovt[:, :n_cmp] = ov.T
    ovt = jnp.asarray(ovt, BF16)

    per_bh = lambda r, c: pl.BlockSpec((None, None, r, c), lambda b, h, i: (b, h, 0, 0))
    per_q = lambda r: pl.BlockSpec((None, None, r, QW), lambda b, h, i: (b, h, 0, i))
    ot = pl.pallas_call(
        functools.partial(_attend_kernel, n_sel=n_sel, n_pick=n_pick, n_cmp_rows=nr),
        grid=(B, N_KV_HEADS, nqb),
        in_specs=[per_q(HEAD_DIM), per_bh(nr, HEAD_DIM), per_bh(HEAD_DIM, nr),
                  per_bh(S, HEAD_DIM), per_bh(HEAD_DIM, S), per_bh(S, HEAD_DIM), per_bh(HEAD_DIM, S),
                  per_q(3),
                  pl.BlockSpec((None, 1, QW), lambda b, h, i: (h, 0, 0)),
                  pl.BlockSpec((n_sel, nr), lambda b, h, i: (0, 0))],
        out_specs=per_q(HEAD_DIM),
        out_shape=jax.ShapeDtypeStruct((B, N_KV_HEADS, HEAD_DIM, nqb * QW), BF16),
        compiler_params=_params(("arbitrary", "arbitrary", "arbitrary")),
        name="attend",
    )(qt, kc, vct, ks, vst, kwn, vwt, gt, slope_t, ovt)
    o = ot.reshape(B, N_KV_HEADS, HEAD_DIM, nqb, GQA_GROUP, Q_BLOCK).transpose(0, 3, 5, 1, 4, 2)
    return o.reshape(B, S, NSA_WIDTH)


def _merge_kernel(x_ref, mod_ref, ys_ref, on_ref, gm_ref, wglu_ref, bglu_ref, wa_ref, wb_ref, wo_ref,
                  gain_ref, wrh_ref, wrl_ref, x1_ref, h2_ref, sc_ref):
    D = D_MODEL
    z = jax.nn.gelu(ys_ref[...])
    glu = z * jax.nn.sigmoid(_dot(z.astype(BF16), wglu_ref[...]) + bglu_ref[...])
    ya = _dot(glu.astype(BF16), wa_ref[...])
    yb = _dot(on_ref[...], wb_ref[...])
    merged = gm_ref[:, :D].astype(F32) * ya + gm_ref[:, D:].astype(F32) * yb
    x1 = x_ref[...] + mod_ref[2:3, :] * _dot(merged.astype(BF16), wo_ref[...])
    x1_ref[...] = x1
    y = x1 * lax.rsqrt(jnp.mean(x1 * x1, axis=-1, keepdims=True) + RMS_EPS)
    h2 = (y * gain_ref[...]) * (1.0 + mod_ref[4:5, :]) + mod_ref[3:4, :]
    h2_ref[...] = h2.astype(BF16)
    hh, hl = _split(h2)
    nt = (((1,), (1,)), ((), ()))
    dg = lambda a, b: lax.dot_general(a, b, nt, preferred_element_type=F32)
    logits = dg(wrh_ref[...], hh) + (dg(wrh_ref[...], hl) + dg(wrl_ref[...], hh))
    sc_ref[...] = jax.nn.sigmoid(logits)


def _merge(x, mod, ys5, o_nsa, gm, w_glu, b_glu, w_a, w_b, w_out, gain_f, w_router):
    B, S, D = x.shape
    tm = min(ROW_TILE, S)
    wrh, wrl = _split(w_router.T)
    ws = [w_glu.astype(BF16), b_glu.reshape(1, -1), w_a.astype(BF16), w_b.astype(BF16), w_out.astype(BF16),
          gain_f, wrh, wrl]
    row = lambda w: pl.BlockSpec((None, tm, w), lambda b, i: (b, i, 0))
    full = lambda a: pl.BlockSpec(a.shape, lambda b, i: (0,) * a.ndim)
    return pl.pallas_call(
        _merge_kernel,
        grid=(B, S // tm),
        in_specs=[row(D), pl.BlockSpec((None, N_MOD, D), lambda b, i: (b, 0, 0)),
                  row(S5_WIDTH), row(NSA_WIDTH), row(2 * D)] + [full(a) for a in ws],
        out_specs=[row(D), row(D), pl.BlockSpec((None, N_EXPERTS, tm), lambda b, i: (b, 0, i))],
        out_shape=[jax.ShapeDtypeStruct((B, S, D), F32), jax.ShapeDtypeStruct((B, S, D), BF16),
                   jax.ShapeDtypeStruct((B, N_EXPERTS, S), F32)],
        compiler_params=_params(("arbitrary", "arbitrary")),
        name="merge",
    )(x, mod, ys5, o_nsa, gm, *ws)


def _first_argmax_rows(v, idx, n):
    m = jnp.max(v, axis=0, keepdims=True)
    first = jnp.min(jnp.where(v == m, idx, n), axis=0, keepdims=True)
    return idx == first, m


def _route_kernel(sc_ref, bias_ref, tri_ref, rank_ref, w_ref, cnt_ref):
    E, NG = N_EXPERTS, N_EXPERT_GROUPS
    GS = E // NG
    sc = sc_ref[...]
    TM = sc.shape[1]
    sel = sc + bias_ref[...]
    i8 = lax.broadcasted_iota(jnp.int32, (GS, TM), 0)
    gscore = []
    for g in range(NG):
        blk = sel[g * GS:(g + 1) * GS, :]
        hit, m1 = _first_argmax_rows(blk, i8, GS)
        m2 = jnp.max(jnp.where(hit, -jnp.inf, blk), axis=0, keepdims=True)
        gscore.append(m1 + m2)
    gscore = jnp.concatenate(gscore, axis=0)
    ig = lax.broadcasted_iota(jnp.int32, (NG, TM), 0)
    gmask = jnp.zeros((NG, TM), F32)
    for _ in range(TOPK_EXPERT_GROUPS):
        hit, _m = _first_argmax_rows(gscore, ig, NG)
        gmask = jnp.where(hit, 1.0, gmask)
        gscore = jnp.where(hit, -jnp.inf, gscore)
    emask = jnp.concatenate([jnp.broadcast_to(gmask[g:g + 1, :], (GS, TM)) for g in range(NG)], axis=0)
    cand = jnp.where(emask > 0.5, sel, NEG_INF)
    ie = lax.broadcasted_iota(jnp.int32, (E, TM), 0)
    chosen = jnp.zeros((E, TM), F32)
    for _ in range(TOP_K):
        hit, _m = _first_argmax_rows(cand, ie, E)
        chosen = jnp.where(hit, 1.0, chosen)
        cand = jnp.where(hit, -jnp.inf, cand)
    w = chosen * sc
    w_ref[...] = w / jnp.sum(w, axis=0, keepdims=True) * ROUTED_SCALE
    cb = chosen.astype(BF16)
    prefix = _dot(cb, tri_ref[...])
    rank_ref[...] = jnp.where(chosen > 0.5, prefix, -1.0)
    cnt_ref[...] = _dot(cb, jnp.ones((TM, 128), BF16))


def _route(scores_t, router_bias):
    B, E, S = scores_t.shape
    TM = min(ROUTE_TILE, S)
    nj = S // TM
    ns = B * nj
    tri = jnp.asarray(np.triu(np.ones((TM, TM), np.float32), k=1), BF16)
    tile = lambda w: pl.BlockSpec((None, E, w), lambda b, j: (b * nj + j, 0, 0))
    return pl.pallas_call(
        _route_kernel,
        grid=(B, nj),
        in_specs=[pl.BlockSpec((None, E, TM), lambda b, j: (b, 0, j)),
                  pl.BlockSpec((E, 1), lambda b, j: (0, 0)),
                  pl.BlockSpec((TM, TM), lambda b, j: (0, 0))],
        out_specs=[tile(TM), tile(TM), tile(128)],
        out_shape=[jax.ShapeDtypeStruct((ns, E, TM), F32), jax.ShapeDtypeStruct((ns, E, TM), F32),
                   jax.ShapeDtypeStruct((ns, E, 128), F32)],
        compiler_params=_params(("arbitrary", "arbitrary")),
        name="route",
    )(scores_t, router_bias.reshape(E, 1), tri)


def _moe_kernel(cnt_ref, h_ref, x1_ref, mod_ref, rank_ref, w_ref, wg_ref, wu_ref, wd_ref,
                sg_ref, su_ref, sd_ref, o_ref, *, n_sub, tm, n_exp):
    to = pl.program_id(0)
    eb = pl.program_id(1)
    gate = mod_ref[5:6, :]

    @pl.when(eb == 0)
    def _shared():
        for s in range(n_sub):
            rows = pl.ds(s * tm, tm)
            hs = h_ref[rows, :]
            hid = jax.nn.silu(_dot(hs, sg_ref[...])) * _dot(hs, su_ref[...])
            o_ref[rows, :] = x1_ref[rows, :] + gate * _dot(hid.astype(BF16), sd_ref[...])

    for j in range(n_exp):
        e = eb * n_exp + j
        for s in range(n_sub):
            rows = pl.ds(s * tm, tm)
            cnt = cnt_ref[(to * n_sub + s) * N_EXPERTS + e]
            n_blk = (cnt + MOE_CAP - 1) // MOE_CAP

            def block(bi, carry, j=j, s=s, rows=rows, e=e):
                rk = rank_ref[s, pl.ds(e, 1), :]
                wt = w_ref[s, pl.ds(e, 1), :]
                slot = (lax.broadcasted_iota(jnp.int32, (MOE_CAP, tm), 0) + bi * MOE_CAP).astype(F32)
                match = slot == rk
                pick = jnp.where(match, 1.0, 0.0).astype(BF16)
                xg = _dot(pick, h_ref[rows, :]).astype(BF16)
                hid = jax.nn.silu(_dot(xg, wg_ref[j])) * _dot(xg, wu_ref[j])
                out = _dot(hid.astype(BF16), wd_ref[j])
                wrow = jnp.sum(jnp.where(match, wt, 0.0), axis=1, keepdims=True)
                ow = (out * wrow * gate).astype(BF16)
                o_ref[rows, :] += lax.dot_general(pick, ow, (((0,), (0,)), ((), ())),
                                                  preferred_element_type=F32)
                return carry

            lax.fori_loop(0, n_blk, block, 0)


def _moe(h2, x1, mod, rank_t, w_t, counts, w_gate, w_up, w_down, ws_gate, ws_up, ws_down, seq):
    T, D = h2.shape
    ns, E, tm = rank_t.shape
    tmo = min(MOE_OUTER, seq)
    n_sub = tmo // tm
    n_exp = MOE_EXPERTS_PER_STEP
    per_b = seq // tmo
    ws = [w_gate.astype(BF16), w_up.astype(BF16), w_down.astype(BF16)]
    sh = [ws_gate.astype(BF16), ws_up.astype(BF16), ws_down.astype(BF16)]
    tok = lambda: pl.BlockSpec((tmo, D), lambda t, e, c: (t, 0))
    sub = lambda: pl.BlockSpec((n_sub, E, tm), lambda t, e, c: (t, 0, 0))
    exp = lambda a: pl.BlockSpec((n_exp,) + a.shape[1:], lambda t, e, c: (e, 0, 0))
    full = lambda a: pl.BlockSpec(a.shape, lambda t, e, c: (0, 0))
    grid_spec = pltpu.PrefetchScalarGridSpec(
        num_scalar_prefetch=1,
        grid=(T // tmo, E // n_exp),
        in_specs=[tok(), tok(), pl.BlockSpec((None, N_MOD, D), lambda t, e, c: (t // per_b, 0, 0)),
                  sub(), sub(), exp(ws[0]), exp(ws[1]), exp(ws[2]), full(sh[0]), full(sh[1]), full(sh[2])],
        out_specs=tok(),
    )
    return pl.pallas_call(
        functools.partial(_moe_kernel, n_sub=n_sub, tm=tm, n_exp=n_exp),
        grid_spec=grid_spec,
        out_shape=jax.ShapeDtypeStruct((T, D), F32),
        compiler_params=_params(("arbitrary", "arbitrary")),
        name="moe",
    )(counts, h2, x1, mod, rank_t, w_t, *ws, *sh)


def kernel(x, c, w_ada, b_ada, norm_mix_gain, norm_ffn_gain, w_in, s5_lambda_re, s5_lambda_im, s5_log_dt, s5_b_re, s5_b_im, s5_c_re, s5_c_im, s5_d, s5_w_glu, s5_b_glu, q_norm_gain, k_norm_gain, cmp_pe, cmp_w1, cmp_b1, cmp_w2, cmp_b2, w_branch_a, w_branch_b, w_out, w_router, router_bias, w_gate, w_up, w_down, ws_gate, ws_up, ws_down):
    B, S, D = x.shape
    for l in range(w_ada.shape[0]):
        mod = _ada(c, w_ada[l], b_ada[l]).reshape(B, N_MOD, D)
        u, qn, kvc, kvs, kvw, gates, gm = _inproj(x, mod, norm_mix_gain[l:l + 1], w_in[l],
                                                  q_norm_gain[l], k_norm_gain[l])
        tables = _s5_tables(s5_lambda_re[l], s5_lambda_im[l], s5_log_dt[l], s5_b_re[l], s5_b_im[l],
                            s5_c_re[l], s5_c_im[l], s5_d[l])
        ys5 = _s5(u, tables)
        kvc_c = _compress(kvc, cmp_pe[l], cmp_w1[l], cmp_b1[l], cmp_w2[l], cmp_b2[l], k_norm_gain[l, 0])
        o_nsa = _attend(qn, kvc_c, kvs, kvw, gates)
        x1, h2, scores_t = _merge(x, mod, ys5, o_nsa, gm, s5_w_glu[l], s5_b_glu[l], w_branch_a[l],
                                  w_branch_b[l], w_out[l], norm_ffn_gain[l:l + 1], w_router[l])
        rank_t, w_t, cnt = _route(scores_t, router_bias[l])
        counts = cnt[:, :, 0].astype(jnp.int32).reshape(-1)
        x = _moe(h2.reshape(B * S, D), x1.reshape(B * S, D), mod, rank_t, w_t, counts,
                 w_gate[l], w_up[l], w_down[l], ws_gate[l], ws_up[l], ws_down[l], S).reshape(B, S, D)
    return x
```

```python
import functools
import math

import numpy as np
import jax
import jax.numpy as jnp
from jax import lax
from jax.experimental import pallas as pl
from jax.experimental.pallas import tpu as pltpu

F32 = jnp.float32
BF16 = jnp.bfloat16

D_MODEL = 1024
S5_WIDTH = 512
S5_GROUP = 16
S5_GROUPS = S5_WIDTH // S5_GROUP
S5_STATE = 64
N_HEADS = 8
N_KV_HEADS = 2
GQA_GROUP = N_HEADS // N_KV_HEADS
HEAD_DIM = 64
NSA_WIDTH = N_HEADS * HEAD_DIM
KV_WIDTH = 2 * N_KV_HEADS * HEAD_DIM
CMP_BLOCK = 32
CMP_STRIDE = 16
CMP_HIDDEN = 256
SEL_BLOCK = 64
SEL_TOPK = 8
WINDOW = 256
Q_BLOCK = 128
FORCE_BONUS = 1e3
N_EXPERTS = 64
TOP_K = 8
N_EXPERT_GROUPS = 8
TOPK_EXPERT_GROUPS = 4
EXPERT_HIDDEN = 256
SHARED_HIDDEN = 256
ROUTED_SCALE = 2.5
RMS_EPS = 1e-6
NEG_INF = -1e30
N_MOD = 6

S5_CHUNK = 16
S5_CW = S5_CHUNK * S5_GROUP
ROW_TILE = 512
SEL_KEY_CHUNK = 512
POS_BASE = 64
POS_ROWS = 16
AUG = 128
ROUTE_TILE = 512
MOE_OUTER = 1024
MOE_EXPERTS_PER_STEP = 4
MOE_CAP = 128
VMEM_LIMIT = 56 * 1024 * 1024


def _dot(a, b):
    return jnp.dot(a, b, preferred_element_type=F32)


def _split(a):
    hi = a.astype(BF16)
    lo = (a - hi.astype(F32)).astype(BF16)
    return hi, lo


def _dot3(a, bh, bl):
    ah, al = _split(a)
    return _dot(ah, bh) + (_dot(al, bh) + _dot(ah, bl))


def _params(sem):
    return pltpu.CompilerParams(dimension_semantics=sem, vmem_limit_bytes=VMEM_LIMIT)


def _ada_kernel(c_ref, w_ref, b_ref, o_ref):
    cs = jax.nn.silu(c_ref[...])
    wh, wl = _split(w_ref[...])
    o_ref[...] = _dot3(cs, wh, wl) + b_ref[...]


def _ada(c, w_ada, b_ada):
    B, D = c.shape
    return pl.pallas_call(
        _ada_kernel,
        grid=(N_MOD,),
        in_specs=[pl.BlockSpec((B, D), lambda j: (0, 0)),
                  pl.BlockSpec((D, D), lambda j: (0, j)),
                  pl.BlockSpec((1, D), lambda j: (0, j))],
        out_specs=pl.BlockSpec((B, D), lambda j: (0, j)),
        out_shape=jax.ShapeDtypeStruct((B, N_MOD * D), F32),
        compiler_params=_params(("arbitrary",)),
        name="ada",
    )(c, w_ada, b_ada.reshape(1, N_MOD * D))


def _head_norm(v, bd, gain):
    sq = v * v
    sh, sl = _split(sq)
    ms = _dot(sh, bd) + _dot(sl, bd)
    return v * lax.rsqrt(ms + RMS_EPS) * gain


def _inproj_kernel(x_ref, mod_ref, gain_ref, wm_ref, wgn_ref, wgm_ref, bd_ref, qg_ref, kg_ref,
                   u_ref, q_ref, kvc_ref, kvs_ref, kvw_ref, gn_ref, gm_ref):
    x = x_ref[...]
    shift = mod_ref[0:1, :]
    scale = mod_ref[1:2, :]
    y = x * lax.rsqrt(jnp.mean(x * x, axis=-1, keepdims=True) + RMS_EPS)
    h = (y * gain_ref[...]) * (1.0 + scale) + shift
    hb = h.astype(BF16)
    main = _dot(hb, wm_ref[...])
    u_ref[...] = main[:, :S5_WIDTH]
    o = S5_WIDTH
    q = main[:, o:o + NSA_WIDTH]
    o += NSA_WIDTH
    kvc_ref[...] = main[:, o:o + KV_WIDTH]
    o += KV_WIDTH
    kvs = main[:, o:o + KV_WIDTH]
    o += KV_WIDTH
    kvw = main[:, o:o + KV_WIDTH]
    bd = bd_ref[...]
    kw = N_KV_HEADS * HEAD_DIM
    q_ref[...] = (_head_norm(q, bd, qg_ref[...]) * (HEAD_DIM ** -0.5)).astype(BF16)
    kvs_ref[:, :kw] = _head_norm(kvs[:, :kw], bd[:kw, :kw], kg_ref[1:2, :]).astype(BF16)
    kvs_ref[:, kw:] = kvs[:, kw:].astype(BF16)
    kvw_ref[:, :kw] = _head_norm(kvw[:, :kw], bd[:kw, :kw], kg_ref[2:3, :]).astype(BF16)
    kvw_ref[:, kw:] = kvw[:, kw:].astype(BF16)
    gn_ref[...] = jax.nn.sigmoid(_dot(hb, wgn_ref[...]))
    gm_ref[...] = jax.nn.sigmoid(_dot(hb, wgm_ref[...])).astype(BF16)


def _inproj(x, mod, gain, w_in, q_gain, k_gain):
    B, S, D = x.shape
    tm = min(ROW_TILE, S)
    n_main = S5_WIDTH + NSA_WIDTH + 3 * KV_WIDTH
    wm = w_in[:, :n_main].astype(BF16)
    wgn = w_in[:, n_main:n_main + 3 * N_HEADS].astype(BF16)
    wgm = w_in[:, n_main + 3 * N_HEADS:].astype(BF16)
    seg = np.arange(NSA_WIDTH) // HEAD_DIM
    bd = jnp.asarray((seg[:, None] == seg[None, :]).astype(np.float32) / HEAD_DIM, BF16)
    qg = jnp.tile(q_gain, N_HEADS).reshape(1, NSA_WIDTH)
    kg = jnp.tile(k_gain, (1, N_KV_HEADS))
    row = lambda w: pl.BlockSpec((None, tm, w), lambda b, i: (b, i, 0))
    full = lambda a: pl.BlockSpec(a.shape, lambda b, i: (0,) * a.ndim)
    return pl.pallas_call(
        _inproj_kernel,
        grid=(B, S // tm),
        in_specs=[row(D), pl.BlockSpec((None, N_MOD, D), lambda b, i: (b, 0, 0)),
                  full(gain), full(wm), full(wgn), full(wgm), full(bd), full(qg), full(kg)],
        out_specs=[row(S5_WIDTH), row(NSA_WIDTH), row(KV_WIDTH), row(KV_WIDTH), row(KV_WIDTH),
                   row(3 * N_HEADS), row(2 * D)],
        out_shape=[jax.ShapeDtypeStruct((B, S, S5_WIDTH), F32),
                   jax.ShapeDtypeStruct((B, S, NSA_WIDTH), BF16),
                   jax.ShapeDtypeStruct((B, S, KV_WIDTH), F32),
                   jax.ShapeDtypeStruct((B, S, KV_WIDTH), BF16),
                   jax.ShapeDtypeStruct((B, S, KV_WIDTH), BF16),
                   jax.ShapeDtypeStruct((B, S, 3 * N_HEADS), F32),
                   jax.ShapeDtypeStruct((B, S, 2 * D), BF16)],
        compiler_params=_params(("arbitrary", "arbitrary")),
        name="inproj",
    )(x, mod, gain, wm, wgn, wgm, bd, qg, kg)


def _s5_tables(lam_re, lam_im, log_dt, b_re, b_im, c_re, c_im, d_skip):
    L, P, N, G = S5_CHUNK, S5_GROUP, S5_STATE, S5_GROUPS
    hp = lax.Precision.HIGHEST
    lr, li = lam_re.astype(F32), lam_im.astype(F32)
    dt = jnp.exp(log_dt.astype(F32))[:, None]
    mag = jnp.exp(lr * dt)
    abar_re, abar_im = mag * jnp.cos(li * dt), mag * jnp.sin(li * dt)
    num_re, num_im = abar_re - 1.0, abar_im
    den = lr * lr + li * li
    coef_re = (num_re * lr + num_im * li) / den
    coef_im = (num_im * lr - num_re * li) / den
    br, bi = b_re.astype(F32), b_im.astype(F32)
    bbar_re = coef_re[..., None] * br - coef_im[..., None] * bi
    bbar_im = coef_re[..., None] * bi + coef_im[..., None] * br
    k = jnp.arange(L + 1, dtype=F32)[:, None, None]
    pmag = jnp.exp(lr * dt * k)
    pre, pim = pmag * jnp.cos(li * dt * k), pmag * jnp.sin(li * dt * k)
    cr, ci = c_re.astype(F32), c_im.astype(F32)
    ca_re = cr[None] * pre[:, :, None, :] - ci[None] * pim[:, :, None, :]
    ca_im = cr[None] * pim[:, :, None, :] + ci[None] * pre[:, :, None, :]
    kern = (jnp.einsum('kgpn,gnq->gkqp', ca_re[:L], bbar_re, precision=hp)
            - jnp.einsum('kgpn,gnq->gkqp', ca_im[:L], bbar_im, precision=hp))
    s_i = np.arange(L)[:, None]
    t_i = np.arange(L)[None, :]
    tau = np.clip(t_i - s_i, 0, L - 1)
    causal = jnp.asarray((t_i >= s_i).astype(np.float32))
    mt = kern[:, tau] * causal[None, :, :, None, None]
    mt = mt.transpose(0, 1, 3, 2, 4).reshape(G, L * P, L * P)
    rev = np.arange(L - 1, -1, -1)
    ab_re = pre[rev][..., None] * bbar_re[None] - pim[rev][..., None] * bbar_im[None]
    ab_im = pre[rev][..., None] * bbar_im[None] + pim[rev][..., None] * bbar_re[None]
    ws = jnp.concatenate([ab_re, ab_im], axis=2)
    ws = ws.transpose(1, 0, 3, 2).reshape(G, L * P, 2 * N)
    wo = jnp.concatenate([ca_re[1:], -ca_im[1:]], axis=3)
    wo = wo.transpose(1, 3, 0, 2).reshape(G, 2 * N, L * P)
    al = jnp.stack([pre[L], pim[L]], axis=1)
    dv = jnp.tile(d_skip.astype(F32), (1, L)).reshape(G, 1, L * P)
    return mt, ws, wo, al, dv


def _s5_kernel(u_ref, mth_ref, mtl_ref, wsh_ref, wsl_ref, woh_ref, wol_ref, al_ref, dv_ref,
               y_ref, vr_scr, vi_scr, xr_scr, xi_scr, *, n_chunks, bsz):
    N = S5_STATE
    u = u_ref[...]
    uh, ul = _split(u)
    y = _dot(uh, mth_ref[...]) + (_dot(ul, mth_ref[...]) + _dot(uh, mtl_ref[...]))
    v = _dot(uh, wsh_ref[...]) + (_dot(ul, wsh_ref[...]) + _dot(uh, wsl_ref[...]))
    vr_scr[...] = v[:, :N]
    vi_scr[...] = v[:, N:]
    a_r = al_ref[0:1, :]
    a_i = al_ref[1:2, :]

    def step(c, x):
        xr, xi = x
        r = pl.ds(pl.multiple_of(c * bsz, bsz), bsz)
        xr_scr[r, :] = xr
        xi_scr[r, :] = xi
        return (a_r * xr - a_i * xi + vr_scr[r, :], a_r * xi + a_i * xr + vi_scr[r, :])

    zero = jnp.zeros((bsz, N), F32)
    lax.fori_loop(0, n_chunks, step, (zero, zero), unroll=8)
    xp = jnp.concatenate([xr_scr[...], xi_scr[...]], axis=1)
    y = y + _dot3(xp, woh_ref[...], wol_ref[...])
    y_ref[...] = y + dv_ref[...] * u


def _s5(u, tables):
    B, S, _ = u.shape
    L, P, G, N = S5_CHUNK, S5_GROUP, S5_GROUPS, S5_STATE
    nc = S // L
    mt, ws, wo, al, dv = tables
    ug = u.reshape(B, nc, L, G, P).transpose(3, 1, 0, 2, 4).reshape(G, nc * B, L * P)
    mth, mtl = _split(mt)
    wsh, wsl = _split(ws)
    woh, wol = _split(wo)
    grp = lambda a: pl.BlockSpec((None,) + a.shape[1:], lambda g: (g,) + (0,) * (a.ndim - 1))
    y = pl.pallas_call(
        functools.partial(_s5_kernel, n_chunks=nc, bsz=B),
        grid=(G,),
        in_specs=[grp(ug), grp(mth), grp(mtl), grp(wsh), grp(wsl), grp(woh), grp(wol), grp(al), grp(dv)],
        out_specs=grp(ug),
        out_shape=jax.ShapeDtypeStruct(ug.shape, F32),
        scratch_shapes=[pltpu.VMEM((nc * B, N), F32)] * 4,
        compiler_params=_params(("arbitrary",)),
        name="s5",
    )(ug, mth, mtl, wsh, wsl, woh, wol, al, dv)
    return y.reshape(G, nc, B, L, P).transpose(2, 1, 3, 0, 4).reshape(B, S, S5_WIDTH)


def _compress_kernel(x_ref, pe_ref, w1_ref, b1_ref, w2_ref, b2_ref, kg_ref, o_ref, *, n_rows):
    x = x_ref[...]
    half = CMP_STRIDE * HEAD_DIM
    a = _dot((x + pe_ref[0:1, :]).astype(BF16), w1_ref[:half, :])
    b = _dot((x + pe_ref[1:2, :]).astype(BF16), w1_ref[half:, :])
    hid = jax.nn.gelu(a + pltpu.roll(b, n_rows - 1, axis=0) + b1_ref[...])
    out = _dot(hid.astype(BF16), w2_ref[...]) + b2_ref[...]
    normed = out * lax.rsqrt(jnp.mean(out * out, axis=-1, keepdims=True) + RMS_EPS) * kg_ref[...]
    is_key = pl.program_id(1) < N_KV_HEADS
    o_ref[...] = jnp.where(is_key, normed, out).astype(BF16)


def _compress(kv_c, cmp_pe, cmp_w1, cmp_b1, cmp_w2, cmp_b2, k_gain0):
    B, S, _ = kv_c.shape
    nr = S // CMP_STRIDE
    half = CMP_STRIDE * HEAD_DIM
    nj = 2 * N_KV_HEADS
    xc = kv_c.reshape(B, nr, CMP_STRIDE, nj, HEAD_DIM).transpose(0, 3, 1, 2, 4).reshape(B, nj, nr, half)
    pe = cmp_pe.reshape(2, 2, half)
    w1 = cmp_w1.astype(BF16)
    w2 = cmp_w2.astype(BF16)
    b1 = cmp_b1.reshape(2, 1, CMP_HIDDEN)
    b2 = cmp_b2.reshape(2, 1, HEAD_DIM)
    kv = lambda a: pl.BlockSpec((None,) + a.shape[1:], lambda b, j: (j // N_KV_HEADS,) + (0,) * (a.ndim - 1))
    return pl.pallas_call(
        functools.partial(_compress_kernel, n_rows=nr),
        grid=(B, nj),
        in_specs=[pl.BlockSpec((None, None, nr, half), lambda b, j: (b, j, 0, 0)),
                  kv(pe), kv(w1), kv(b1), kv(w2), kv(b2),
                  pl.BlockSpec((1, HEAD_DIM), lambda b, j: (0, 0))],
        out_specs=pl.BlockSpec((None, None, nr, HEAD_DIM), lambda b, j: (b, j, 0, 0)),
        out_shape=jax.ShapeDtypeStruct((B, nj, nr, HEAD_DIM), BF16),
        compiler_params=_params(("arbitrary", "arbitrary")),
        name="compress",
    )(xc, pe, w1, b1, w2, b2, k_gain0.reshape(1, HEAD_DIM))


def _softmax_cols(s, mask):
    s = jnp.where(mask, s, NEG_INF)
    m = jnp.max(s, axis=0, keepdims=True)
    p = jnp.where(mask, jnp.exp(s - m), 0.0)
    return p / jnp.maximum(jnp.sum(p, axis=0, keepdims=True), 1e-20)


def _attend_kernel(qt_ref, kc_ref, vct_ref, ks_ref, vst_ref, kw_ref, vwt_ref, g_ref, slope_ref, ovt_ref,
                   o_ref, qa_scr, *, n_sel, n_pick, n_cmp_rows):
    QW = GQA_GROUP * Q_BLOCK
    i = pl.program_id(2)
    t0 = i * Q_BLOCK
    slope = slope_ref[...]
    tq = t0 + (lax.broadcasted_iota(jnp.int32, (1, QW), 1) & (Q_BLOCK - 1))

    r = lax.broadcasted_iota(jnp.int32, (POS_ROWS, QW), 0)
    qa_scr[0:HEAD_DIM, :] = qt_ref[...]
    qa_scr[HEAD_DIM:HEAD_DIM + POS_ROWS, :] = jnp.where(
        r == 0, slope * POS_BASE, jnp.where(r == 1, slope, 0.0)).astype(BF16)
    qa_scr[HEAD_DIM + POS_ROWS:AUG, :] = jnp.zeros((AUG - HEAD_DIM - POS_ROWS, QW), BF16)
    qa = qa_scr[0:AUG, :]

    sc = _dot(kc_ref[...], qa)
    cpos = lax.broadcasted_iota(jnp.int32, (n_cmp_rows, QW), 0) * CMP_STRIDE + (CMP_BLOCK - 1)
    p_c = _softmax_cols(sc, cpos <= tq)
    o_c = _dot(vct_ref[...], p_c.astype(BF16))
    psum = p_c[:, 0:Q_BLOCK]
    for g in range(1, GQA_GROUP):
        psum = psum + p_c[:, g * Q_BLOCK:(g + 1) * Q_BLOCK]
    ph, pl_ = _split(psum)
    imp = _dot(ovt_ref[...], ph) + _dot(ovt_ref[...], pl_)

    jb = lax.broadcasted_iota(jnp.int32, (n_sel, Q_BLOCK), 0)
    cur = (t0 + lax.broadcasted_iota(jnp.int32, (1, Q_BLOCK), 1)) // SEL_BLOCK
    forced = (jb == 0) | (jb == cur) | (jb == cur - 1)
    imp = jnp.where(forced, imp + FORCE_BONUS, imp)
    imp = jnp.where(jb <= cur, imp, -1.0)
    bias = jnp.full((n_sel, Q_BLOCK), NEG_INF, F32)
    for _ in range(n_pick):
        m = jnp.max(imp, axis=0, keepdims=True)
        first = jnp.min(jnp.where(imp == m, jb, n_sel), axis=0, keepdims=True)
        hit = jb == first
        bias = jnp.where(hit, 0.0, bias)
        imp = jnp.where(hit, -jnp.inf, imp)
    bias = jnp.concatenate([bias] * GQA_GROUP, axis=1).astype(BF16)
    qa_scr[AUG:, :] = jnp.concatenate([bias, jnp.zeros((AUG - n_sel, QW), BF16)], axis=0)

    KC = SEL_KEY_CHUNK
    n_chunks = (t0 + Q_BLOCK + KC - 1) // KC

    def scores(j):
        k0 = pl.multiple_of(j * KC, KC)
        return k0, _dot(ks_ref[pl.ds(k0, KC), :], qa_scr[...])

    def accumulate(carry, k0, s, p_of):
        m, l, acc = carry
        m_new = jnp.maximum(m, jnp.max(s, axis=0, keepdims=True))
        alpha = jnp.exp(m - m_new)
        p = p_of(jnp.exp(s - m_new))
        l = alpha * l + jnp.sum(p, axis=0, keepdims=True)
        acc = alpha * acc + _dot(vst_ref[:, pl.ds(k0, KC)], p.astype(BF16))
        return m_new, l, acc

    def past_step(j, carry):
        k0, s = scores(j)
        return accumulate(carry, k0, s, lambda p: p)

    init = (jnp.full((1, QW), NEG_INF, F32), jnp.zeros((1, QW), F32), jnp.zeros((HEAD_DIM, QW), F32))
    carry = lax.fori_loop(0, n_chunks - 1, past_step, init)
    k0, s = scores(n_chunks - 1)
    visible = (k0 + lax.broadcasted_iota(jnp.int32, (KC, QW), 0)) <= tq
    _, l_s, acc_s = accumulate(carry, k0, jnp.where(visible, s, NEG_INF),
                               lambda p: jnp.where(visible, p, 0.0))
    o_s = acc_s / jnp.maximum(l_s, 1e-20)

    WK = WINDOW + Q_BLOCK
    w0 = pl.multiple_of(jnp.maximum(t0 - WINDOW, 0), Q_BLOCK)
    sw = _dot(kw_ref[pl.ds(w0, WK), :], qa)
    dist = tq - (w0 + lax.broadcasted_iota(jnp.int32, (WK, QW), 0))
    in_win = (dist | (WINDOW - 1 - dist)) >= 0
    p_w = _softmax_cols(sw, in_win)
    o_w = _dot(vwt_ref[:, pl.ds(w0, WK)], p_w.astype(BF16))

    o = g_ref[0:1, :] * o_c + g_ref[1:2, :] * o_s + g_ref[2:3, :] * o_w
    o_ref[...] = o.astype(BF16)


def _attend(qn, kvc_c, kvs, kvw, gates):
    B, S, _ = qn.shape
    nqb = S // Q_BLOCK
    QW = GQA_GROUP * Q_BLOCK
    n_sel = S // SEL_BLOCK
    n_pick = min(SEL_TOPK, n_sel)
    nr = S // CMP_STRIDE
    n_cmp = (S - CMP_BLOCK) // CMP_STRIDE + 1
    kw = N_KV_HEADS * HEAD_DIM

    assert n_sel % 16 == 0 and n_sel <= AUG

    def augment(k, pos, onehot=None):
        cols = np.zeros((len(pos), AUG - HEAD_DIM), np.float32)
        cols[:, 0] = pos // POS_BASE
        cols[:, 1] = pos % POS_BASE
        parts = [cols] if onehot is None else [cols, onehot]
        extra = jnp.asarray(np.concatenate(parts, axis=1), BF16)
        return jnp.concatenate([k, jnp.broadcast_to(extra, k.shape[:2] + extra.shape)], axis=-1)

    qt = qn.reshape(B, nqb, Q_BLOCK, N_KV_HEADS, GQA_GROUP, HEAD_DIM).transpose(0, 3, 5, 1, 4, 2)
    qt = qt.reshape(B, N_KV_HEADS, HEAD_DIM, nqb * QW)
    kc = augment(kvc_c[:, :N_KV_HEADS], np.arange(nr) * CMP_STRIDE + CMP_BLOCK - 1)
    vct = kvc_c[:, N_KV_HEADS:].transpose(0, 1, 3, 2)
    heads = lambda a: a.reshape(B, S, N_KV_HEADS, HEAD_DIM)
    tpos = np.arange(S)
    blk_onehot = np.zeros((S, AUG), np.float32)
    blk_onehot[tpos, tpos // SEL_BLOCK] = 1.0
    ks = augment(heads(kvs[..., :kw]).transpose(0, 2, 1, 3), tpos, blk_onehot)
    vst = heads(kvs[..., kw:]).transpose(0, 2, 3, 1)
    kwn = augment(heads(kvw[..., :kw]).transpose(0, 2, 1, 3), tpos)
    vwt = heads(kvw[..., kw:]).transpose(0, 2, 3, 1)
    gt = gates.reshape(B, nqb, Q_BLOCK, 3, N_KV_HEADS, GQA_GROUP).transpose(0, 4, 3, 1, 5, 2)
    gt = gt.reshape(B, N_KV_HEADS, 3, nqb * QW)
    slopes = 2.0 ** (-8.0 * np.arange(1, N_HEADS + 1) / N_HEADS)
    slope_t = jnp.asarray(np.repeat(slopes.reshape(N_KV_HEADS, GQA_GROUP), Q_BLOCK, axis=1)
                          .reshape(N_KV_HEADS, 1, QW), F32)
    cmp_start = np.arange(n_cmp) * CMP_STRIDE
    sel_start = np.arange(n_sel) * SEL_BLOCK
    ov = ((cmp_start[:, None] <= sel_start[None, :] + SEL_BLOCK - 1)
          & (cmp_start[:, None] + CMP_BLOCK - 1 >= sel_start[None, :])).astype(np.float32)
    ovt = np.zeros((n_sel, nr), np.float32)
    ovt[:, :n_cmp] = ov.T
    ovt = jnp.asarray(ovt, BF16)

    per_bh = lambda r, c: pl.BlockSpec((None, None, r, c), lambda b, h, i: (b, h, 0, 0))
    per_q = lambda r: pl.BlockSpec((None, None, r, QW), lambda b, h, i: (b, h, 0, i))
    ot = pl.pallas_call(
        functools.partial(_attend_kernel, n_sel=n_sel, n_pick=n_pick, n_cmp_rows=nr),
        grid=(B, N_KV_HEADS, nqb),
        in_specs=[per_q(HEAD_DIM), per_bh(nr, AUG), per_bh(HEAD_DIM, nr),
                  per_bh(S, 2 * AUG), per_bh(HEAD_DIM, S), per_bh(S, AUG), per_bh(HEAD_DIM, S),
                  per_q(3),
                  pl.BlockSpec((None, 1, QW), lambda b, h, i: (h, 0, 0)),
                  pl.BlockSpec((n_sel, nr), lambda b, h, i: (0, 0))],
        out_specs=per_q(HEAD_DIM),
        out_shape=jax.ShapeDtypeStruct((B, N_KV_HEADS, HEAD_DIM, nqb * QW), BF16),
        scratch_shapes=[pltpu.VMEM((2 * AUG, QW), BF16)],
        compiler_params=_params(("arbitrary", "arbitrary", "arbitrary")),
        name="attend",
    )(qt, kc, vct, ks, vst, kwn, vwt, gt, slope_t, ovt)
    o = ot.reshape(B, N_KV_HEADS, HEAD_DIM, nqb, GQA_GROUP, Q_BLOCK).transpose(0, 3, 5, 1, 4, 2)
    return o.reshape(B, S, NSA_WIDTH)


def _merge_kernel(x_ref, mod_ref, ys_ref, on_ref, gm_ref, wglu_ref, bglu_ref, wa_ref, wb_ref, wo_ref,
                  gain_ref, wrh_ref, wrl_ref, x1_ref, h2_ref, sc_ref):
    D = D_MODEL
    z = jax.nn.gelu(ys_ref[...])
    glu = z * jax.nn.sigmoid(_dot(z.astype(BF16), wglu_ref[...]) + bglu_ref[...])
    ya = _dot(glu.astype(BF16), wa_ref[...])
    yb = _dot(on_ref[...], wb_ref[...])
    merged = gm_ref[:, :D].astype(F32) * ya + gm_ref[:, D:].astype(F32) * yb
    x1 = x_ref[...] + mod_ref[2:3, :] * _dot(merged.astype(BF16), wo_ref[...])
    x1_ref[...] = x1
    y = x1 * lax.rsqrt(jnp.mean(x1 * x1, axis=-1, keepdims=True) + RMS_EPS)
    h2 = (y * gain_ref[...]) * (1.0 + mod_ref[4:5, :]) + mod_ref[3:4, :]
    h2_ref[...] = h2.astype(BF16)
    hh, hl = _split(h2)
    nt = (((1,), (1,)), ((), ()))
    dg = lambda a, b: lax.dot_general(a, b, nt, preferred_element_type=F32)
    logits = dg(wrh_ref[...], hh) + (dg(wrh_ref[...], hl) + dg(wrl_ref[...], hh))
    sc_ref[...] = jax.nn.sigmoid(logits)


def _merge(x, mod, ys5, o_nsa, gm, w_glu, b_glu, w_a, w_b, w_out, gain_f, w_router):
    B, S, D = x.shape
    tm = min(ROW_TILE, S)
    wrh, wrl = _split(w_router.T)
    ws = [w_glu.astype(BF16), b_glu.reshape(1, -1), w_a.astype(BF16), w_b.astype(BF16), w_out.astype(BF16),
          gain_f, wrh, wrl]
    row = lambda w: pl.BlockSpec((None, tm, w), lambda b, i: (b, i, 0))
    full = lambda a: pl.BlockSpec(a.shape, lambda b, i: (0,) * a.ndim)
    return pl.pallas_call(
        _merge_kernel,
        grid=(B, S // tm),
        in_specs=[row(D), pl.BlockSpec((None, N_MOD, D), lambda b, i: (b, 0, 0)),
                  row(S5_WIDTH), row(NSA_WIDTH), row(2 * D)] + [full(a) for a in ws],
        out_specs=[row(D), row(D), pl.BlockSpec((None, N_EXPERTS, tm), lambda b, i: (b, 0, i))],
        out_shape=[jax.ShapeDtypeStruct((B, S, D), F32), jax.ShapeDtypeStruct((B, S, D), BF16),
                   jax.ShapeDtypeStruct((B, N_EXPERTS, S), F32)],
        compiler_params=_params(("arbitrary", "arbitrary")),
        name="merge",
    )(x, mod, ys5, o_nsa, gm, *ws)


def _first_argmax_rows(v, idx, n):
    m = jnp.max(v, axis=0, keepdims=True)
    first = jnp.min(jnp.where(v == m, idx, n), axis=0, keepdims=True)
    return idx == first, m


def _route_kernel(sc_ref, bias_ref, tri_ref, rank_ref, w_ref, cnt_ref):
    E, NG = N_EXPERTS, N_EXPERT_GROUPS
    GS = E // NG
    sc = sc_ref[...]
    TM = sc.shape[1]
    sel = sc + bias_ref[...]
    i8 = lax.broadcasted_iota(jnp.int32, (GS, TM), 0)
    gscore = []
    for g in range(NG):
        blk = sel[g * GS:(g + 1) * GS, :]
        hit, m1 = _first_argmax_rows(blk, i8, GS)
        m2 = jnp.max(jnp.where(hit, -jnp.inf, blk), axis=0, keepdims=True)
        gscore.append(m1 + m2)
    gscore = jnp.concatenate(gscore, axis=0)
    ig = lax.broadcasted_iota(jnp.int32, (NG, TM), 0)
    gmask = jnp.zeros((NG, TM), F32)
    for _ in range(TOPK_EXPERT_GROUPS):
        hit, _m = _first_argmax_rows(gscore, ig, NG)
        gmask = jnp.where(hit, 1.0, gmask)
        gscore = jnp.where(hit, -jnp.inf, gscore)
    emask = jnp.concatenate([jnp.broadcast_to(gmask[g:g + 1, :], (GS, TM)) for g in range(NG)], axis=0)
    cand = jnp.where(emask > 0.5, sel, NEG_INF)
    ie = lax.broadcasted_iota(jnp.int32, (E, TM), 0)
    chosen = jnp.zeros((E, TM), F32)
    for _ in range(TOP_K):
        hit, _m = _first_argmax_rows(cand, ie, E)
        chosen = jnp.where(hit, 1.0, chosen)
        cand = jnp.where(hit, -jnp.inf, cand)
    w = chosen * sc
    w_ref[...] = w / jnp.sum(w, axis=0, keepdims=True) * ROUTED_SCALE
    cb = chosen.astype(BF16)
    prefix = _dot(cb, tri_ref[...])
    rank_ref[...] = jnp.where(chosen > 0.5, prefix, -1.0)
    cnt_ref[...] = _dot(cb, jnp.ones((TM, 128), BF16))


def _route(scores_t, router_bias):
    B, E, S = scores_t.shape
    TM = min(ROUTE_TILE, S)
    nj = S // TM
    ns = B * nj
    tri = jnp.asarray(np.triu(np.ones((TM, TM), np.float32), k=1), BF16)
    tile = lambda w: pl.BlockSpec((None, E, w), lambda b, j: (b * nj + j, 0, 0))
    return pl.pallas_call(
        _route_kernel,
        grid=(B, nj),
        in_specs=[pl.BlockSpec((None, E, TM), lambda b, j: (b, 0, j)),
                  pl.BlockSpec((E, 1), lambda b, j: (0, 0)),
                  pl.BlockSpec((TM, TM), lambda b, j: (0, 0))],
        out_specs=[tile(TM), tile(TM), tile(128)],
        out_shape=[jax.ShapeDtypeStruct((ns, E, TM), F32), jax.ShapeDtypeStruct((ns, E, TM), F32),
                   jax.ShapeDtypeStruct((ns, E, 128), F32)],
        compiler_params=_params(("arbitrary", "arbitrary")),
        name="route",
    )(scores_t, router_bias.reshape(E, 1), tri)


def _moe_kernel(cnt_ref, h_ref, x1_ref, mod_ref, rank_ref, w_ref, wg_ref, wu_ref, wd_ref,
                sg_ref, su_ref, sd_ref, o_ref, *, n_sub, tm, n_exp):
    to = pl.program_id(0)
    eb = pl.program_id(1)
    gate = mod_ref[5:6, :]

    @pl.when(eb == 0)
    def _shared():
        for s in range(n_sub):
            rows = pl.ds(s * tm, tm)
            hs = h_ref[rows, :]
            hid = jax.nn.silu(_dot(hs, sg_ref[...])) * _dot(hs, su_ref[...])
            o_ref[rows, :] = x1_ref[rows, :] + gate * _dot(hid.astype(BF16), sd_ref[...])

    for j in range(n_exp):
        e = eb * n_exp + j
        for s in range(n_sub):
            rows = pl.ds(s * tm, tm)
            cnt = cnt_ref[(to * n_sub + s) * N_EXPERTS + e]
            n_blk = (cnt + MOE_CAP - 1) // MOE_CAP

            def block(bi, carry, j=j, s=s, rows=rows, e=e):
                rk = rank_ref[s, pl.ds(e, 1), :]
                wt = w_ref[s, pl.ds(e, 1), :]
                slot = (lax.broadcasted_iota(jnp.int32, (MOE_CAP, tm), 0) + bi * MOE_CAP).astype(F32)
                match = slot == rk
                pick = jnp.where(match, 1.0, 0.0).astype(BF16)
                xg = _dot(pick, h_ref[rows, :]).astype(BF16)
                hid = jax.nn.silu(_dot(xg, wg_ref[j])) * _dot(xg, wu_ref[j])
                out = _dot(hid.astype(BF16), wd_ref[j])
                wrow = jnp.sum(jnp.where(match, wt, 0.0), axis=1, keepdims=True)
                ow = (out * wrow * gate).astype(BF16)
                o_ref[rows, :] += lax.dot_general(pick, ow, (((0,), (0,)), ((), ())),
                                                  preferred_element_type=F32)
                return carry

            lax.fori_loop(0, n_blk, block, 0)


def _moe(h2, x1, mod, rank_t, w_t, counts, w_gate, w_up, w_down, ws_gate, ws_up, ws_down, seq):
    T, D = h2.shape
    ns, E, tm = rank_t.shape
    tmo = min(MOE_OUTER, seq)
    n_sub = tmo // tm
    n_exp = MOE_EXPERTS_PER_STEP
    per_b = seq // tmo
    ws = [w_gate.astype(BF16), w_up.astype(BF16), w_down.astype(BF16)]
    sh = [ws_gate.astype(BF16), ws_up.astype(BF16), ws_down.astype(BF16)]
    tok = lambda: pl.BlockSpec((tmo, D), lambda t, e, c: (t, 0))
    sub = lambda: pl.BlockSpec((n_sub, E, tm), lambda t, e, c: (t, 0, 0))
    exp = lambda a: pl.BlockSpec((n_exp,) + a.shape[1:], lambda t, e, c: (e, 0, 0))
    full = lambda a: pl.BlockSpec(a.shape, lambda t, e, c: (0, 0))
    grid_spec = pltpu.PrefetchScalarGridSpec(
        num_scalar_prefetch=1,
        grid=(T // tmo, E // n_exp),
        in_specs=[tok(), tok(), pl.BlockSpec((None, N_MOD, D), lambda t, e, c: (t // per_b, 0, 0)),
                  sub(), sub(), exp(ws[0]), exp(ws[1]), exp(ws[2]), full(sh[0]), full(sh[1]), full(sh[2])],
        out_specs=tok(),
    )
    return pl.pallas_call(
        functools.partial(_moe_kernel, n_sub=n_sub, tm=tm, n_exp=n_exp),
        grid_spec=grid_spec,
        out_shape=jax.ShapeDtypeStruct((T, D), F32),
        compiler_params=_params(("arbitrary", "arbitrary")),
        name="moe",
    )(counts, h2, x1, mod, rank_t, w_t, *ws, *sh)


def kernel(x, c, w_ada, b_ada, norm_mix_gain, norm_ffn_gain, w_in, s5_lambda_re, s5_lambda_im, s5_log_dt, s5_b_re, s5_b_im, s5_c_re, s5_c_im, s5_d, s5_w_glu, s5_b_glu, q_norm_gain, k_norm_gain, cmp_pe, cmp_w1, cmp_b1, cmp_w2, cmp_b2, w_branch_a, w_branch_b, w_out, w_router, router_bias, w_gate, w_up, w_down, ws_gate, ws_up, ws_down):
    B, S, D = x.shape
    for l in range(w_ada.shape[0]):
        mod = _ada(c, w_ada[l], b_ada[l]).reshape(B, N_MOD, D)
        u, qn, kvc, kvs, kvw, gates, gm = _inproj(x, mod, norm_mix_gain[l:l + 1], w_in[l],
                                                  q_norm_gain[l], k_norm_gain[l])
        tables = _s5_tables(s5_lambda_re[l], s5_lambda_im[l], s5_log_dt[l], s5_b_re[l], s5_b_im[l],
                            s5_c_re[l], s5_c_im[l], s5_d[l])
        ys5 = _s5(u, tables)
        kvc_c = _compress(kvc, cmp_pe[l], cmp_w1[l], cmp_b1[l], cmp_w2[l], cmp_b2[l], k_norm_gain[l, 0])
        o_nsa = _attend(qn, kvc_c, kvs, kvw, gates)
        x1, h2, scores_t = _merge(x, mod, ys5, o_nsa, gm, s5_w_glu[l], s5_b_glu[l], w_branch_a[l],
                                  w_branch_b[l], w_out[l], norm_ffn_gain[l:l + 1], w_router[l])
        rank_t, w_t, cnt = _route(scores_t, router_bias[l])
        counts = cnt[:, :, 0].astype(jnp.int32).reshape(-1)
        x = _moe(h2.reshape(B * S, D), x1.reshape(B * S, D), mod, rank_t, w_t, counts,
                 w_gate[l], w_up[l], w_down[l], ws_gate[l], ws_up[l], ws_down[l], S).reshape(B, S, D)
    return x
```

```python
import functools
import math

import numpy as np
import jax
import jax.numpy as jnp
from jax import lax
from jax.experimental import pallas as pl
from jax.experimental.pallas import tpu as pltpu

F32 = jnp.float32
BF16 = jnp.bfloat16

D_MODEL = 1024
S5_WIDTH = 512
S5_GROUP = 16
S5_GROUPS = S5_WIDTH // S5_GROUP
S5_STATE = 64
N_HEADS = 8
N_KV_HEADS = 2
GQA_GROUP = N_HEADS // N_KV_HEADS
HEAD_DIM = 64
NSA_WIDTH = N_HEADS * HEAD_DIM
KV_WIDTH = 2 * N_KV_HEADS * HEAD_DIM
CMP_BLOCK = 32
CMP_STRIDE = 16
CMP_HIDDEN = 256
SEL_BLOCK = 64
SEL_TOPK = 8
WINDOW = 256
Q_BLOCK = 128
FORCE_BONUS = 1e3
N_EXPERTS = 64
TOP_K = 8
N_EXPERT_GROUPS = 8
TOPK_EXPERT_GROUPS = 4
EXPERT_HIDDEN = 256
SHARED_HIDDEN = 256
ROUTED_SCALE = 2.5
RMS_EPS = 1e-6
NEG_INF = -1e30
N_MOD = 6

S5_CHUNK = 16
S5_CW = S5_CHUNK * S5_GROUP
ROW_TILE = 512
SEL_KEY_CHUNK = 512
POS_BASE = 64
POS_ROWS = 16
AUG = 128
ROUTE_TILE = 256
MOE_OUTER = 1024
MOE_EXPERTS_PER_STEP = 4
MOE_PIECE = 128
MOE_WIN = 48
MOE_ROWS = 512
RANK_NONE = -float(1 << 20)
VMEM_LIMIT = 56 * 1024 * 1024


def _dot(a, b):
    return jnp.dot(a, b, preferred_element_type=F32)


def _split(a):
    hi = a.astype(BF16)
    lo = (a - hi.astype(F32)).astype(BF16)
    return hi, lo


def _dot3(a, bh, bl):
    ah, al = _split(a)
    return _dot(ah, bh) + (_dot(al, bh) + _dot(ah, bl))


def _params(sem):
    return pltpu.CompilerParams(dimension_semantics=sem, vmem_limit_bytes=VMEM_LIMIT)


def _ada_kernel(c_ref, w_ref, b_ref, o_ref):
    cs = jax.nn.silu(c_ref[...])
    wh, wl = _split(w_ref[...])
    o_ref[...] = _dot3(cs, wh, wl) + b_ref[...]


def _ada(c, w_ada, b_ada):
    B, D = c.shape
    return pl.pallas_call(
        _ada_kernel,
        grid=(N_MOD,),
        in_specs=[pl.BlockSpec((B, D), lambda j: (0, 0)),
                  pl.BlockSpec((D, D), lambda j: (0, j)),
                  pl.BlockSpec((1, D), lambda j: (0, j))],
        out_specs=pl.BlockSpec((B, D), lambda j: (0, j)),
        out_shape=jax.ShapeDtypeStruct((B, N_MOD * D), F32),
        compiler_params=_params(("arbitrary",)),
        name="ada",
    )(c, w_ada, b_ada.reshape(1, N_MOD * D))


def _head_norm(v, bd, gain):
    sq = v * v
    sh, sl = _split(sq)
    ms = _dot(sh, bd) + _dot(sl, bd)
    return v * lax.rsqrt(ms + RMS_EPS) * gain


def _inproj_kernel(x_ref, mod_ref, gain_ref, wm_ref, wgn_ref, wgm_ref, bd_ref, qg_ref, kg_ref,
                   u_ref, q_ref, kvc_ref, kvs_ref, kvw_ref, gn_ref, gm_ref):
    x = x_ref[...]
    shift = mod_ref[0:1, :]
    scale = mod_ref[1:2, :]
    y = x * lax.rsqrt(jnp.mean(x * x, axis=-1, keepdims=True) + RMS_EPS)
    h = (y * gain_ref[...]) * (1.0 + scale) + shift
    hb = h.astype(BF16)
    main = _dot(hb, wm_ref[...])
    u_ref[...] = main[:, :S5_WIDTH]
    o = S5_WIDTH
    q = main[:, o:o + NSA_WIDTH]
    o += NSA_WIDTH
    kvc_ref[...] = main[:, o:o + KV_WIDTH]
    o += KV_WIDTH
    kvs = main[:, o:o + KV_WIDTH]
    o += KV_WIDTH
    kvw = main[:, o:o + KV_WIDTH]
    bd = bd_ref[...]
    kw = N_KV_HEADS * HEAD_DIM
    q_ref[...] = (_head_norm(q, bd, qg_ref[...]) * (HEAD_DIM ** -0.5)).astype(BF16)
    kvs_ref[:, :kw] = _head_norm(kvs[:, :kw], bd[:kw, :kw], kg_ref[1:2, :]).astype(BF16)
    kvs_ref[:, kw:] = kvs[:, kw:].astype(BF16)
    kvw_ref[:, :kw] = _head_norm(kvw[:, :kw], bd[:kw, :kw], kg_ref[2:3, :]).astype(BF16)
    kvw_ref[:, kw:] = kvw[:, kw:].astype(BF16)
    gn_ref[...] = jax.nn.sigmoid(_dot(hb, wgn_ref[...]))
    gm_ref[...] = jax.nn.sigmoid(_dot(hb, wgm_ref[...])).astype(BF16)


def _inproj(x, mod, gain, w_in, q_gain, k_gain):
    B, S, D = x.shape
    tm = min(ROW_TILE, S)
    n_main = S5_WIDTH + NSA_WIDTH + 3 * KV_WIDTH
    wm = w_in[:, :n_main].astype(BF16)
    wgn = w_in[:, n_main:n_main + 3 * N_HEADS].astype(BF16)
    wgm = w_in[:, n_main + 3 * N_HEADS:].astype(BF16)
    seg = np.arange(NSA_WIDTH) // HEAD_DIM
    bd = jnp.asarray((seg[:, None] == seg[None, :]).astype(np.float32) / HEAD_DIM, BF16)
    qg = jnp.tile(q_gain, N_HEADS).reshape(1, NSA_WIDTH)
    kg = jnp.tile(k_gain, (1, N_KV_HEADS))
    row = lambda w: pl.BlockSpec((None, tm, w), lambda b, i: (b, i, 0))
    full = lambda a: pl.BlockSpec(a.shape, lambda b, i: (0,) * a.ndim)
    return pl.pallas_call(
        _inproj_kernel,
        grid=(B, S // tm),
        in_specs=[row(D), pl.BlockSpec((None, N_MOD, D), lambda b, i: (b, 0, 0)),
                  full(gain), full(wm), full(wgn), full(wgm), full(bd), full(qg), full(kg)],
        out_specs=[row(S5_WIDTH), row(NSA_WIDTH), row(KV_WIDTH), row(KV_WIDTH), row(KV_WIDTH),
                   row(3 * N_HEADS), row(2 * D)],
        out_shape=[jax.ShapeDtypeStruct((B, S, S5_WIDTH), F32),
                   jax.ShapeDtypeStruct((B, S, NSA_WIDTH), BF16),
                   jax.ShapeDtypeStruct((B, S, KV_WIDTH), F32),
                   jax.ShapeDtypeStruct((B, S, KV_WIDTH), BF16),
                   jax.ShapeDtypeStruct((B, S, KV_WIDTH), BF16),
                   jax.ShapeDtypeStruct((B, S, 3 * N_HEADS), F32),
                   jax.ShapeDtypeStruct((B, S, 2 * D), BF16)],
        compiler_params=_params(("arbitrary", "arbitrary")),
        name="inproj",
    )(x, mod, gain, wm, wgn, wgm, bd, qg, kg)


def _s5_tables(lam_re, lam_im, log_dt, b_re, b_im, c_re, c_im, d_skip):
    L, P, N, G = S5_CHUNK, S5_GROUP, S5_STATE, S5_GROUPS
    hp = lax.Precision.HIGHEST
    lr, li = lam_re.astype(F32), lam_im.astype(F32)
    dt = jnp.exp(log_dt.astype(F32))[:, None]
    mag = jnp.exp(lr * dt)
    abar_re, abar_im = mag * jnp.cos(li * dt), mag * jnp.sin(li * dt)
    num_re, num_im = abar_re - 1.0, abar_im
    den = lr * lr + li * li
    coef_re = (num_re * lr + num_im * li) / den
    coef_im = (num_im * lr - num_re * li) / den
    br, bi = b_re.astype(F32), b_im.astype(F32)
    bbar_re = coef_re[..., None] * br - coef_im[..., None] * bi
    bbar_im = coef_re[..., None] * bi + coef_im[..., None] * br
    k = jnp.arange(L + 1, dtype=F32)[:, None, None]
    pmag = jnp.exp(lr * dt * k)
    pre, pim = pmag * jnp.cos(li * dt * k), pmag * jnp.sin(li * dt * k)
    cr, ci = c_re.astype(F32), c_im.astype(F32)
    ca_re = cr[None] * pre[:, :, None, :] - ci[None] * pim[:, :, None, :]
    ca_im = cr[None] * pim[:, :, None, :] + ci[None] * pre[:, :, None, :]
    kern = (jnp.einsum('kgpn,gnq->gkqp', ca_re[:L], bbar_re, precision=hp)
            - jnp.einsum('kgpn,gnq->gkqp', ca_im[:L], bbar_im, precision=hp))
    s_i = np.arange(L)[:, None]
    t_i = np.arange(L)[None, :]
    tau = np.clip(t_i - s_i, 0, L - 1)
    causal = jnp.asarray((t_i >= s_i).astype(np.float32))
    mt = kern[:, tau] * causal[None, :, :, None, None]
    mt = mt.transpose(0, 1, 3, 2, 4).reshape(G, L * P, L * P)
    rev = np.arange(L - 1, -1, -1)
    ab_re = pre[rev][..., None] * bbar_re[None] - pim[rev][..., None] * bbar_im[None]
    ab_im = pre[rev][..., None] * bbar_im[None] + pim[rev][..., None] * bbar_re[None]
    ws = jnp.concatenate([ab_re, ab_im], axis=2)
    ws = ws.transpose(1, 0, 3, 2).reshape(G, L * P, 2 * N)
    wo = jnp.concatenate([ca_re[1:], -ca_im[1:]], axis=3)
    wo = wo.transpose(1, 3, 0, 2).reshape(G, 2 * N, L * P)
    al = jnp.stack([pre[L], pim[L]], axis=1)
    dv = jnp.tile(d_skip.astype(F32), (1, L)).reshape(G, 1, L * P)
    return mt, ws, wo, al, dv


def _s5_kernel(u_ref, mth_ref, mtl_ref, wsh_ref, wsl_ref, woh_ref, wol_ref, al_ref, dv_ref,
               y_ref, vr_scr, vi_scr, xr_scr, xi_scr, *, n_chunks, bsz):
    N = S5_STATE
    u = u_ref[...]
    uh, ul = _split(u)
    y = _dot(uh, mth_ref[...]) + (_dot(ul, mth_ref[...]) + _dot(uh, mtl_ref[...]))
    v = _dot(uh, wsh_ref[...]) + (_dot(ul, wsh_ref[...]) + _dot(uh, wsl_ref[...]))
    vr_scr[...] = v[:, :N]
    vi_scr[...] = v[:, N:]
    a_r = al_ref[0:1, :]
    a_i = al_ref[1:2, :]

    def step(c, x):
        xr, xi = x
        r = pl.ds(pl.multiple_of(c * bsz, bsz), bsz)
        xr_scr[r, :] = xr
        xi_scr[r, :] = xi
        return (a_r * xr - a_i * xi + vr_scr[r, :], a_r * xi + a_i * xr + vi_scr[r, :])

    zero = jnp.zeros((bsz, N), F32)
    lax.fori_loop(0, n_chunks, step, (zero, zero), unroll=8)
    xp = jnp.concatenate([xr_scr[...], xi_scr[...]], axis=1)
    y = y + _dot3(xp, woh_ref[...], wol_ref[...])
    y_ref[...] = y + dv_ref[...] * u


def _s5(u, tables):
    B, S, _ = u.shape
    L, P, G, N = S5_CHUNK, S5_GROUP, S5_GROUPS, S5_STATE
    nc = S // L
    mt, ws, wo, al, dv = tables
    ug = u.reshape(B, nc, L, G, P).transpose(3, 1, 0, 2, 4).reshape(G, nc * B, L * P)
    mth, mtl = _split(mt)
    wsh, wsl = _split(ws)
    woh, wol = _split(wo)
    grp = lambda a: pl.BlockSpec((None,) + a.shape[1:], lambda g: (g,) + (0,) * (a.ndim - 1))
    y = pl.pallas_call(
        functools.partial(_s5_kernel, n_chunks=nc, bsz=B),
        grid=(G,),
        in_specs=[grp(ug), grp(mth), grp(mtl), grp(wsh), grp(wsl), grp(woh), grp(wol), grp(al), grp(dv)],
        out_specs=grp(ug),
        out_shape=jax.ShapeDtypeStruct(ug.shape, F32),
        scratch_shapes=[pltpu.VMEM((nc * B, N), F32)] * 4,
        compiler_params=_params(("arbitrary",)),
        name="s5",
    )(ug, mth, mtl, wsh, wsl, woh, wol, al, dv)
    return y.reshape(G, nc, B, L, P).transpose(2, 1, 3, 0, 4).reshape(B, S, S5_WIDTH)


def _compress_kernel(x_ref, pe_ref, w1_ref, b1_ref, w2_ref, b2_ref, kg_ref, o_ref, *, n_rows):
    x = x_ref[...]
    half = CMP_STRIDE * HEAD_DIM
    a = _dot((x + pe_ref[0:1, :]).astype(BF16), w1_ref[:half, :])
    b = _dot((x + pe_ref[1:2, :]).astype(BF16), w1_ref[half:, :])
    hid = jax.nn.gelu(a + pltpu.roll(b, n_rows - 1, axis=0) + b1_ref[...])
    out = _dot(hid.astype(BF16), w2_ref[...]) + b2_ref[...]
    normed = out * lax.rsqrt(jnp.mean(out * out, axis=-1, keepdims=True) + RMS_EPS) * kg_ref[...]
    is_key = pl.program_id(1) < N_KV_HEADS
    o_ref[...] = jnp.where(is_key, normed, out).astype(BF16)


def _compress(kv_c, cmp_pe, cmp_w1, cmp_b1, cmp_w2, cmp_b2, k_gain0):
    B, S, _ = kv_c.shape
    nr = S // CMP_STRIDE
    half = CMP_STRIDE * HEAD_DIM
    nj = 2 * N_KV_HEADS
    xc = kv_c.reshape(B, nr, CMP_STRIDE, nj, HEAD_DIM).transpose(0, 3, 1, 2, 4).reshape(B, nj, nr, half)
    pe = cmp_pe.reshape(2, 2, half)
    w1 = cmp_w1.astype(BF16)
    w2 = cmp_w2.astype(BF16)
    b1 = cmp_b1.reshape(2, 1, CMP_HIDDEN)
    b2 = cmp_b2.reshape(2, 1, HEAD_DIM)
    kv = lambda a: pl.BlockSpec((None,) + a.shape[1:], lambda b, j: (j // N_KV_HEADS,) + (0,) * (a.ndim - 1))
    return pl.pallas_call(
        functools.partial(_compress_kernel, n_rows=nr),
        grid=(B, nj),
        in_specs=[pl.BlockSpec((None, None, nr, half), lambda b, j: (b, j, 0, 0)),
                  kv(pe), kv(w1), kv(b1), kv(w2), kv(b2),
                  pl.BlockSpec((1, HEAD_DIM), lambda b, j: (0, 0))],
        out_specs=pl.BlockSpec((None, None, nr, HEAD_DIM), lambda b, j: (b, j, 0, 0)),
        out_shape=jax.ShapeDtypeStruct((B, nj, nr, HEAD_DIM), BF16),
        compiler_params=_params(("arbitrary", "arbitrary")),
        name="compress",
    )(xc, pe, w1, b1, w2, b2, k_gain0.reshape(1, HEAD_DIM))


def _softmax_cols(s, mask):
    s = jnp.where(mask, s, NEG_INF)
    m = jnp.max(s, axis=0, keepdims=True)
    p = jnp.where(mask, jnp.exp(s - m), 0.0)
    return p / jnp.maximum(jnp.sum(p, axis=0, keepdims=True), 1e-20)


def _attend_kernel(qt_ref, kc_ref, vct_ref, ks_ref, vst_ref, kw_ref, vwt_ref, g_ref, slope_ref, ovt_ref,
                   o_ref, qa_scr, *, n_sel, n_pick, n_cmp_rows):
    QW = GQA_GROUP * Q_BLOCK
    i = pl.program_id(2)
    t0 = i * Q_BLOCK
    slope = slope_ref[...]
    tq = t0 + (lax.broadcasted_iota(jnp.int32, (1, QW), 1) & (Q_BLOCK - 1))

    r = lax.broadcasted_iota(jnp.int32, (POS_ROWS, QW), 0)
    qa_scr[0:HEAD_DIM, :] = qt_ref[...]
    qa_scr[HEAD_DIM:HEAD_DIM + POS_ROWS, :] = jnp.where(
        r == 0, slope * POS_BASE, jnp.where(r == 1, slope, 0.0)).astype(BF16)
    qa_scr[HEAD_DIM + POS_ROWS:AUG, :] = jnp.zeros((AUG - HEAD_DIM - POS_ROWS, QW), BF16)
    qa = qa_scr[0:AUG, :]

    sc = _dot(kc_ref[...], qa)
    cpos = lax.broadcasted_iota(jnp.int32, (n_cmp_rows, QW), 0) * CMP_STRIDE + (CMP_BLOCK - 1)
    p_c = _softmax_cols(sc, cpos <= tq)
    o_c = _dot(vct_ref[...], p_c.astype(BF16))
    psum = p_c[:, 0:Q_BLOCK]
    for g in range(1, GQA_GROUP):
        psum = psum + p_c[:, g * Q_BLOCK:(g + 1) * Q_BLOCK]
    ph, pl_ = _split(psum)
    imp = _dot(ovt_ref[...], ph) + _dot(ovt_ref[...], pl_)

    jb = lax.broadcasted_iota(jnp.int32, (n_sel, Q_BLOCK), 0)
    cur = (t0 + lax.broadcasted_iota(jnp.int32, (1, Q_BLOCK), 1)) // SEL_BLOCK
    forced = (jb == 0) | (jb == cur) | (jb == cur - 1)
    imp = jnp.where(forced, imp + FORCE_BONUS, imp)
    imp = jnp.where(jb <= cur, imp, -1.0)
    bias = jnp.full((n_sel, Q_BLOCK), NEG_INF, F32)
    for _ in range(n_pick):
        m = jnp.max(imp, axis=0, keepdims=True)
        first = jnp.min(jnp.where(imp == m, jb, n_sel), axis=0, keepdims=True)
        hit = jb == first
        bias = jnp.where(hit, 0.0, bias)
        imp = jnp.where(hit, -jnp.inf, imp)
    bias = jnp.concatenate([bias] * GQA_GROUP, axis=1).astype(BF16)
    qa_scr[AUG:, :] = jnp.concatenate([bias, jnp.zeros((AUG - n_sel, QW), BF16)], axis=0)

    KC = SEL_KEY_CHUNK
    n_chunks = (t0 + Q_BLOCK + KC - 1) // KC

    def scores(j):
        k0 = pl.multiple_of(j * KC, KC)
        return _dot(ks_ref[pl.ds(k0, KC), :], qa_scr[...])

    def accumulate(carry, j, s, p_of):
        m, l, acc = carry
        k0 = pl.multiple_of(j * KC, KC)
        m_new = jnp.maximum(m, jnp.max(s, axis=0, keepdims=True))
        alpha = jnp.exp(m - m_new)
        p = p_of(jnp.exp(s - m_new))
        l = alpha * l + jnp.sum(p, axis=0, keepdims=True)
        acc = alpha * acc + _dot(vst_ref[:, pl.ds(k0, KC)], p.astype(BF16))
        return m_new, l, acc

    def past_step(j, state):
        return accumulate(state, j, scores(j), lambda p: p)

    init = (jnp.full((1, QW), NEG_INF, F32), jnp.zeros((1, QW), F32), jnp.zeros((HEAD_DIM, QW), F32))
    state = lax.fori_loop(0, n_chunks - 1, past_step, init)
    last = n_chunks - 1
    visible = (last * KC + lax.broadcasted_iota(jnp.int32, (KC, QW), 0)) <= tq
    _, l_s, acc_s = accumulate(state, last, jnp.where(visible, scores(last), NEG_INF),
                               lambda p: jnp.where(visible, p, 0.0))
    o_s = acc_s / jnp.maximum(l_s, 1e-20)

    WK = WINDOW + Q_BLOCK
    w0 = pl.multiple_of(jnp.maximum(t0 - WINDOW, 0), Q_BLOCK)
    sw = _dot(kw_ref[pl.ds(w0, WK), :], qa)
    dist = tq - (w0 + lax.broadcasted_iota(jnp.int32, (WK, QW), 0))
    in_win = (dist | (WINDOW - 1 - dist)) >= 0
    p_w = _softmax_cols(sw, in_win)
    o_w = _dot(vwt_ref[:, pl.ds(w0, WK)], p_w.astype(BF16))

    o = g_ref[0:1, :] * o_c + g_ref[1:2, :] * o_s + g_ref[2:3, :] * o_w
    o_ref[...] = o.astype(BF16)


def _attend(qn, kvc_c, kvs, kvw, gates):
    B, S, _ = qn.shape
    nqb = S // Q_BLOCK
    QW = GQA_GROUP * Q_BLOCK
    n_sel = S // SEL_BLOCK
    n_pick = min(SEL_TOPK, n_sel)
    nr = S // CMP_STRIDE
    n_cmp = (S - CMP_BLOCK) // CMP_STRIDE + 1
    kw = N_KV_HEADS * HEAD_DIM

    assert n_sel % 16 == 0 and n_sel <= AUG

    def augment(k, pos, onehot=None):
        cols = np.zeros((len(pos), AUG - HEAD_DIM), np.float32)
        cols[:, 0] = pos // POS_BASE
        cols[:, 1] = pos % POS_BASE
        parts = [cols] if onehot is None else [cols, onehot]
        extra = jnp.asarray(np.concatenate(parts, axis=1), BF16)
        return jnp.concatenate([k, jnp.broadcast_to(extra, k.shape[:2] + extra.shape)], axis=-1)

    qt = qn.reshape(B, nqb, Q_BLOCK, N_KV_HEADS, GQA_GROUP, HEAD_DIM).transpose(0, 3, 5, 1, 4, 2)
    qt = qt.reshape(B, N_KV_HEADS, HEAD_DIM, nqb * QW)
    kc = augment(kvc_c[:, :N_KV_HEADS], np.arange(nr) * CMP_STRIDE + CMP_BLOCK - 1)
    vct = kvc_c[:, N_KV_HEADS:].transpose(0, 1, 3, 2)
    heads = lambda a: a.reshape(B, S, N_KV_HEADS, HEAD_DIM)
    tpos = np.arange(S)
    blk_onehot = np.zeros((S, AUG), np.float32)
    blk_onehot[tpos, tpos // SEL_BLOCK] = 1.0
    ks = augment(heads(kvs[..., :kw]).transpose(0, 2, 1, 3), tpos, blk_onehot)
    vst = heads(kvs[..., kw:]).transpose(0, 2, 3, 1)
    kwn = augment(heads(kvw[..., :kw]).transpose(0, 2, 1, 3), tpos)
    vwt = heads(kvw[..., kw:]).transpose(0, 2, 3, 1)
    gt = gates.reshape(B, nqb, Q_BLOCK, 3, N_KV_HEADS, GQA_GROUP).transpose(0, 4, 3, 1, 5, 2)
    gt = gt.reshape(B, N_KV_HEADS, 3, nqb * QW)
    slopes = 2.0 ** (-8.0 * np.arange(1, N_HEADS + 1) / N_HEADS)
    slope_t = jnp.asarray(np.repeat(slopes.reshape(N_KV_HEADS, GQA_GROUP), Q_BLOCK, axis=1)
                          .reshape(N_KV_HEADS, 1, QW), F32)
    cmp_start = np.arange(n_cmp) * CMP_STRIDE
    sel_start = np.arange(n_sel) * SEL_BLOCK
    ov = ((cmp_start[:, None] <= sel_start[None, :] + SEL_BLOCK - 1)
          & (cmp_start[:, None] + CMP_BLOCK - 1 >= sel_start[None, :])).astype(np.float32)
    ovt = np.zeros((n_sel, nr), np.float32)
    ovt[:, :n_cmp] = ov.T
    ovt = jnp.asarray(ovt, BF16)

    per_bh = lambda r, c: pl.BlockSpec((None, None, r, c), lambda b, h, i: (b, h, 0, 0))
    per_q = lambda r: pl.BlockSpec((None, None, r, QW), lambda b, h, i: (b, h, 0, i))
    ot = pl.pallas_call(
        functools.partial(_attend_kernel, n_sel=n_sel, n_pick=n_pick, n_cmp_rows=nr),
        grid=(B, N_KV_HEADS, nqb),
        in_specs=[per_q(HEAD_DIM), per_bh(nr, AUG), per_bh(HEAD_DIM, nr),
                  per_bh(S, 2 * AUG), per_bh(HEAD_DIM, S), per_bh(S, AUG), per_bh(HEAD_DIM, S),
                  per_q(3),
                  pl.BlockSpec((None, 1, QW), lambda b, h, i: (h, 0, 0)),
                  pl.BlockSpec((n_sel, nr), lambda b, h, i: (0, 0))],
        out_specs=per_q(HEAD_DIM),
        out_shape=jax.ShapeDtypeStruct((B, N_KV_HEADS, HEAD_DIM, nqb * QW), BF16),
        scratch_shapes=[pltpu.VMEM((2 * AUG, QW), BF16)],
        compiler_params=_params(("arbitrary", "arbitrary", "arbitrary")),
        name="attend",
    )(qt, kc, vct, ks, vst, kwn, vwt, gt, slope_t, ovt)
    o = ot.reshape(B, N_KV_HEADS, HEAD_DIM, nqb, GQA_GROUP, Q_BLOCK).transpose(0, 3, 5, 1, 4, 2)
    return o.reshape(B, S, NSA_WIDTH)


def _merge_kernel(x_ref, mod_ref, ys_ref, on_ref, gm_ref, wglu_ref, bglu_ref, wa_ref, wb_ref, wo_ref,
                  gain_ref, wrh_ref, wrl_ref, x1_ref, h2_ref, sc_ref):
    D = D_MODEL
    z = jax.nn.gelu(ys_ref[...])
    glu = z * jax.nn.sigmoid(_dot(z.astype(BF16), wglu_ref[...]) + bglu_ref[...])
    ya = _dot(glu.astype(BF16), wa_ref[...])
    yb = _dot(on_ref[...], wb_ref[...])
    merged = gm_ref[:, :D].astype(F32) * ya + gm_ref[:, D:].astype(F32) * yb
    x1 = x_ref[...] + mod_ref[2:3, :] * _dot(merged.astype(BF16), wo_ref[...])
    x1_ref[...] = x1
    y = x1 * lax.rsqrt(jnp.mean(x1 * x1, axis=-1, keepdims=True) + RMS_EPS)
    h2 = (y * gain_ref[...]) * (1.0 + mod_ref[4:5, :]) + mod_ref[3:4, :]
    h2_ref[...] = h2.astype(BF16)
    hh, hl = _split(h2)
    nt = (((1,), (1,)), ((), ()))
    dg = lambda a, b: lax.dot_general(a, b, nt, preferred_element_type=F32)
    logits = dg(wrh_ref[...], hh) + (dg(wrh_ref[...], hl) + dg(wrl_ref[...], hh))
    sc_ref[...] = jax.nn.sigmoid(logits)


def _merge(x, mod, ys5, o_nsa, gm, w_glu, b_glu, w_a, w_b, w_out, gain_f, w_router):
    B, S, D = x.shape
    tm = min(ROW_TILE, S)
    wrh, wrl = _split(w_router.T)
    ws = [w_glu.astype(BF16), b_glu.reshape(1, -1), w_a.astype(BF16), w_b.astype(BF16), w_out.astype(BF16),
          gain_f, wrh, wrl]
    row = lambda w: pl.BlockSpec((None, tm, w), lambda b, i: (b, i, 0))
    full = lambda a: pl.BlockSpec(a.shape, lambda b, i: (0,) * a.ndim)
    return pl.pallas_call(
        _merge_kernel,
        grid=(B, S // tm),
        in_specs=[row(D), pl.BlockSpec((None, N_MOD, D), lambda b, i: (b, 0, 0)),
                  row(S5_WIDTH), row(NSA_WIDTH), row(2 * D)] + [full(a) for a in ws],
        out_specs=[row(D), row(D), pl.BlockSpec((None, N_EXPERTS, tm), lambda b, i: (b, 0, i))],
        out_shape=[jax.ShapeDtypeStruct((B, S, D), F32), jax.ShapeDtypeStruct((B, S, D), BF16),
                   jax.ShapeDtypeStruct((B, N_EXPERTS, S), F32)],
        compiler_params=_params(("arbitrary", "arbitrary")),
        name="merge",
    )(x, mod, ys5, o_nsa, gm, *ws)


def _first_argmax_rows(v, idx, n):
    m = jnp.max(v, axis=0, keepdims=True)
    first = jnp.min(jnp.where(v == m, idx, n), axis=0, keepdims=True)
    return idx == first, m


def _route_kernel(sc_ref, bias_ref, tri_ref, rank_ref, w_ref, cnt_ref):
    E, NG = N_EXPERTS, N_EXPERT_GROUPS
    GS = E // NG
    sc = sc_ref[...]
    TM = sc.shape[1]
    sel = sc + bias_ref[...]
    i8 = lax.broadcasted_iota(jnp.int32, (GS, TM), 0)
    gscore = []
    for g in range(NG):
        blk = sel[g * GS:(g + 1) * GS, :]
        hit, m1 = _first_argmax_rows(blk, i8, GS)
        m2 = jnp.max(jnp.where(hit, -jnp.inf, blk), axis=0, keepdims=True)
        gscore.append(m1 + m2)
    gscore = jnp.concatenate(gscore, axis=0)
    ig = lax.broadcasted_iota(jnp.int32, (NG, TM), 0)
    gmask = jnp.zeros((NG, TM), F32)
    for _ in range(TOPK_EXPERT_GROUPS):
        hit, _m = _first_argmax_rows(gscore, ig, NG)
        gmask = jnp.where(hit, 1.0, gmask)
        gscore = jnp.where(hit, -jnp.inf, gscore)
    emask = jnp.concatenate([jnp.broadcast_to(gmask[g:g + 1, :], (GS, TM)) for g in range(NG)], axis=0)
    cand = jnp.where(emask > 0.5, sel, NEG_INF)
    ie = lax.broadcasted_iota(jnp.int32, (E, TM), 0)
    chosen = jnp.zeros((E, TM), F32)
    for _ in range(TOP_K):
        hit, _m = _first_argmax_rows(cand, ie, E)
        chosen = jnp.where(hit, 1.0, chosen)
        cand = jnp.where(hit, -jnp.inf, cand)
    w = chosen * sc
    w_ref[...] = w / jnp.sum(w, axis=0, keepdims=True) * ROUTED_SCALE
    cb = chosen.astype(BF16)
    prefix = _dot(cb, tri_ref[...])
    rank_ref[...] = jnp.where(chosen > 0.5, prefix, RANK_NONE)
    cnt_ref[...] = _dot(cb, jnp.ones((TM, 128), BF16))


def _route(scores_t, router_bias):
    B, E, S = scores_t.shape
    TM = min(ROUTE_TILE, S)
    nj = S // TM
    ns = B * nj
    tri = jnp.asarray(np.triu(np.ones((TM, TM), np.float32), k=1), BF16)
    tile = lambda w: pl.BlockSpec((None, E, w), lambda b, j: (b * nj + j, 0, 0))
    return pl.pallas_call(
        _route_kernel,
        grid=(B, nj),
        in_specs=[pl.BlockSpec((None, E, TM), lambda b, j: (b, 0, j)),
                  pl.BlockSpec((E, 1), lambda b, j: (0, 0)),
                  pl.BlockSpec((TM, TM), lambda b, j: (0, 0))],
        out_specs=[tile(TM), tile(TM), tile(128)],
        out_shape=[jax.ShapeDtypeStruct((ns, E, TM), F32), jax.ShapeDtypeStruct((ns, E, TM), F32),
                   jax.ShapeDtypeStruct((ns, E, 128), F32)],
        compiler_params=_params(("arbitrary", "arbitrary")),
        name="route",
    )(scores_t, router_bias.reshape(E, 1), tri)


def _moe_kernel(cnt_ref, h_ref, x1_ref, mod_ref, rank_ref, w_ref, wg_ref, wu_ref, wd_ref,
                sg_ref, su_ref, sd_ref, o_ref, xc, yc, pc, wc, *, n_sub, tm, n_exp):
    to = pl.program_id(0)
    eb = pl.program_id(1)
    gate = mod_ref[5:6, :]

    @pl.when(eb == 0)
    def _shared():
        for s in range(n_sub):
            rows = pl.ds(s * tm, tm)
            hs = h_ref[rows, :]
            hid = jax.nn.silu(_dot(hs, sg_ref[...])) * _dot(hs, su_ref[...])
            o_ref[rows, :] = x1_ref[rows, :] + gate * _dot(hid.astype(BF16), sd_ref[...])

    PIECE, WIN = MOE_PIECE, MOE_WIN
    tn = (((0,), (0,)), ((), ()))
    experts = [eb * n_exp + j for j in range(n_exp)]
    cnt = [[cnt_ref[(to * n_sub + s) * N_EXPERTS + e] for e in experts] for s in range(n_sub)]
    off = []
    for s in range(n_sub):
        o = [jnp.int32(0)]
        for j in range(n_exp):
            o.append(o[-1] + ((cnt[s][j] + 15) // 16) * 16)
        off.append(o)
    n_pieces = [jnp.maximum((off[s][n_exp] + PIECE - 1) // PIECE, 2) for s in range(n_sub)]
    fits = n_pieces[0] < MOE_ROWS // PIECE
    for s in range(1, n_sub):
        fits = jnp.logical_and(fits, n_pieces[s] < MOE_ROWS // PIECE)

    def one_hot_rows(s, base, rows_n, js):
        slot = (lax.broadcasted_iota(jnp.int32, (rows_n, tm), 0) + base).astype(F32)
        pick = jnp.zeros((rows_n, tm), F32)
        wacc = jnp.zeros((rows_n, tm), F32)
        for j in js:
            hit = (slot - off[s][j].astype(F32)) == rank_ref[s, pl.ds(experts[j], 1), :]
            pick = jnp.where(hit, 1.0, pick)
            wacc = jnp.where(hit, w_ref[s, pl.ds(experts[j], 1), :], wacc)
        return pick.astype(BF16), jnp.sum(wacc, axis=1, keepdims=True)

    def mlp(xg, j):
        hid = jax.nn.silu(_dot(xg, wg_ref[j])) * _dot(xg, wu_ref[j])
        return _dot(hid.astype(BF16), wd_ref[j])

    @pl.when(fits)
    def _packed():
        for s in range(n_sub):
            rows = pl.ds(s * tm, tm)

            def piece(q, carry, s=s, rows=rows):
                r = pl.ds(q * PIECE if isinstance(q, int) else pl.multiple_of(q * PIECE, PIECE), PIECE)
                pick, wrow = one_hot_rows(s, q * PIECE, PIECE, range(n_exp))
                xc[s, r, :] = _dot(pick, h_ref[rows, :]).astype(BF16)
                pc[s, r, :] = pick
                wc[s, r, :] = wrow
                yc[s, r, :] = jnp.zeros((PIECE, D_MODEL), BF16)
                return carry

            piece(0, 0)
            piece(1, 0)
            lax.fori_loop(2, n_pieces[s], piece, 0)
            r = pl.ds(pl.multiple_of(n_pieces[s] * PIECE, PIECE), PIECE)
            xc[s, r, :] = jnp.zeros((PIECE, D_MODEL), BF16)
            yc[s, r, :] = jnp.zeros((PIECE, D_MODEL), BF16)
            pc[s, r, :] = jnp.zeros((PIECE, tm), BF16)
            wc[s, r, :] = jnp.zeros((PIECE, 1), F32)

        for j in range(n_exp):
            most = cnt[0][j]
            for s in range(1, n_sub):
                most = jnp.maximum(most, cnt[s][j])

            def window(p, carry, j=j):
                starts = [pl.multiple_of(jnp.minimum(off[s][j] + p * WIN, (n_pieces[s] + 1) * PIECE - WIN), 16)
                          for s in range(n_sub)]
                xg = jnp.concatenate([xc[s, pl.ds(starts[s], WIN), :] for s in range(n_sub)], axis=0)
                out = mlp(xg, j)
                for s in range(n_sub):
                    r = pl.ds(starts[s], WIN)
                    yc[s, r, :] = (out[s * WIN:(s + 1) * WIN] * wc[s, r, :] * gate).astype(BF16)
                return carry

            window(0, 0)
            lax.fori_loop(1, (most + WIN - 1) // WIN, window, 0)

        for s in range(n_sub):
            rows = pl.ds(s * tm, tm)

            def pair(q, carry, s=s, rows=rows):
                r = pl.ds(q * 2 * PIECE if isinstance(q, int) else pl.multiple_of(q * 2 * PIECE, 2 * PIECE),
                          2 * PIECE)
                o_ref[rows, :] += lax.dot_general(pc[s, r, :], yc[s, r, :], tn, preferred_element_type=F32)
                return carry

            pair(0, 0)
            lax.fori_loop(1, (n_pieces[s] + 1) // 2, pair, 0)

    @pl.when(jnp.logical_not(fits))
    def _unpacked():
        for j in range(n_exp):
            for s in range(n_sub):
                rows = pl.ds(s * tm, tm)

                def block(bi, carry, j=j, s=s, rows=rows):
                    slot0 = off[s][j] + bi * PIECE
                    pick, wrow = one_hot_rows(s, slot0, PIECE, [j])
                    out = mlp(_dot(pick, h_ref[rows, :]).astype(BF16), j)
                    ow = (out * wrow * gate).astype(BF16)
                    o_ref[rows, :] += lax.dot_general(pick, ow, tn, preferred_element_type=F32)
                    return carry

                lax.fori_loop(0, (cnt[s][j] + PIECE - 1) // PIECE, block, 0)


def _moe(h2, x1, mod, rank_t, w_t, counts, w_gate, w_up, w_down, ws_gate, ws_up, ws_down, seq):
    T, D = h2.shape
    ns, E, tm = rank_t.shape
    tmo = min(MOE_OUTER, seq)
    n_sub = tmo // tm
    n_exp = MOE_EXPERTS_PER_STEP
    per_b = seq // tmo
    ws = [w_gate.astype(BF16), w_up.astype(BF16), w_down.astype(BF16)]
    sh = [ws_gate.astype(BF16), ws_up.astype(BF16), ws_down.astype(BF16)]
    tok = lambda: pl.BlockSpec((tmo, D), lambda t, e, c: (t, 0))
    sub = lambda: pl.BlockSpec((n_sub, E, tm), lambda t, e, c: (t, 0, 0))
    exp = lambda a: pl.BlockSpec((n_exp,) + a.shape[1:], lambda t, e, c: (e, 0, 0))
    full = lambda a: pl.BlockSpec(a.shape, lambda t, e, c: (0, 0))
    grid_spec = pltpu.PrefetchScalarGridSpec(
        num_scalar_prefetch=1,
        grid=(T // tmo, E // n_exp),
        in_specs=[tok(), tok(), pl.BlockSpec((None, N_MOD, D), lambda t, e, c: (t // per_b, 0, 0)),
                  sub(), sub(), exp(ws[0]), exp(ws[1]), exp(ws[2]), full(sh[0]), full(sh[1]), full(sh[2])],
        out_specs=tok(),
        scratch_shapes=[pltpu.VMEM((n_sub, MOE_ROWS, D), BF16), pltpu.VMEM((n_sub, MOE_ROWS, D), BF16),
                        pltpu.VMEM((n_sub, MOE_ROWS, tm), BF16), pltpu.VMEM((n_sub, MOE_ROWS, 1), F32)],
    )
    return pl.pallas_call(
        functools.partial(_moe_kernel, n_sub=n_sub, tm=tm, n_exp=n_exp),
        grid_spec=grid_spec,
        out_shape=jax.ShapeDtypeStruct((T, D), F32),
        compiler_params=_params(("arbitrary", "arbitrary")),
        name="moe",
    )(counts, h2, x1, mod, rank_t, w_t, *ws, *sh)


def kernel(x, c, w_ada, b_ada, norm_mix_gain, norm_ffn_gain, w_in, s5_lambda_re, s5_lambda_im, s5_log_dt, s5_b_re, s5_b_im, s5_c_re, s5_c_im, s5_d, s5_w_glu, s5_b_glu, q_norm_gain, k_norm_gain, cmp_pe, cmp_w1, cmp_b1, cmp_w2, cmp_b2, w_branch_a, w_branch_b, w_out, w_router, router_bias, w_gate, w_up, w_down, ws_gate, ws_up, ws_down):
    B, S, D = x.shape
    for l in range(w_ada.shape[0]):
        mod = _ada(c, w_ada[l], b_ada[l]).reshape(B, N_MOD, D)
        u, qn, kvc, kvs, kvw, gates, gm = _inproj(x, mod, norm_mix_gain[l:l + 1], w_in[l],
                                                  q_norm_gain[l], k_norm_gain[l])
        tables = _s5_tables(s5_lambda_re[l], s5_lambda_im[l], s5_log_dt[l], s5_b_re[l], s5_b_im[l],
                            s5_c_re[l], s5_c_im[l], s5_d[l])
        ys5 = _s5(u, tables)
        kvc_c = _compress(kvc, cmp_pe[l], cmp_w1[l], cmp_b1[l], cmp_w2[l], cmp_b2[l], k_norm_gain[l, 0])
        o_nsa = _attend(qn, kvc_c, kvs, kvw, gates)
        x1, h2, scores_t = _merge(x, mod, ys5, o_nsa, gm, s5_w_glu[l], s5_b_glu[l], w_branch_a[l],
                                  w_branch_b[l], w_out[l], norm_ffn_gain[l:l + 1], w_router[l])
        rank_t, w_t, cnt = _route(scores_t, router_bias[l])
        counts = cnt[:, :, 0].astype(jnp.int32).reshape(-1)
        x = _moe(h2.reshape(B * S, D), x1.reshape(B * S, D), mod, rank_t, w_t, counts,
                 w_gate[l], w_up[l], w_down[l], ws_gate[l], ws_up[l], ws_down[l], S).reshape(B, S, D)
    return x
```

```python
import functools
import math

import numpy as np
import jax
import jax.numpy as jnp
from jax import lax
from jax.experimental import pallas as pl
from jax.experimental.pallas import tpu as pltpu

F32 = jnp.float32
BF16 = jnp.bfloat16

D_MODEL = 1024
S5_WIDTH = 512
S5_GROUP = 16
S5_GROUPS = S5_WIDTH // S5_GROUP
S5_STATE = 64
N_HEADS = 8
N_KV_HEADS = 2
GQA_GROUP = N_HEADS // N_KV_HEADS
HEAD_DIM = 64
NSA_WIDTH = N_HEADS * HEAD_DIM
KV_WIDTH = 2 * N_KV_HEADS * HEAD_DIM
CMP_BLOCK = 32
CMP_STRIDE = 16
CMP_HIDDEN = 256
SEL_BLOCK = 64
SEL_TOPK = 8
WINDOW = 256
Q_BLOCK = 128
FORCE_BONUS = 1e3
N_EXPERTS = 64
TOP_K = 8
N_EXPERT_GROUPS = 8
TOPK_EXPERT_GROUPS = 4
EXPERT_HIDDEN = 256
SHARED_HIDDEN = 256
ROUTED_SCALE = 2.5
RMS_EPS = 1e-6
NEG_INF = -1e30
N_MOD = 6

S5_CHUNK = 16
S5_CW = S5_CHUNK * S5_GROUP
ROW_TILE = 512
SEL_KEY_CHUNK = 512
POS_BASE = 64
POS_ROWS = 16
AUG = 128
ROUTE_TILE = 256
MOE_OUTER = 1024
MOE_EXPERTS_PER_STEP = 4
MOE_PIECE = 128
MOE_WIN = 48
MOE_ROWS = 512
RANK_NONE = -float(1 << 20)
VMEM_LIMIT = 56 * 1024 * 1024


def _dot(a, b):
    return jnp.dot(a, b, preferred_element_type=F32)


def _split(a):
    hi = a.astype(BF16)
    lo = (a - hi.astype(F32)).astype(BF16)
    return hi, lo


def _dot3(a, bh, bl):
    ah, al = _split(a)
    return _dot(ah, bh) + (_dot(al, bh) + _dot(ah, bl))


def _segment_transpose(x):
    nd = x.ndim
    i = lax.broadcasted_iota(jnp.int32, x.shape, nd - 2)
    seg = lax.broadcasted_iota(jnp.int32, x.shape, nd - 1) // S5_GROUP
    out = x
    for d in range(1, 8):
        r = pltpu.roll(pltpu.roll(x, 8 - d, axis=nd - 2), S5_GROUP * d, axis=nd - 1)
        out = jnp.where(seg == ((i + d) & 7), r, out)
    return out


def _params(sem):
    return pltpu.CompilerParams(dimension_semantics=sem, vmem_limit_bytes=VMEM_LIMIT)


def _ada_kernel(c_ref, w_ref, b_ref, o_ref):
    cs = jax.nn.silu(c_ref[...])
    wh, wl = _split(w_ref[...])
    o_ref[...] = _dot3(cs, wh, wl) + b_ref[...]


def _ada(c, w_ada, b_ada):
    B, D = c.shape
    return pl.pallas_call(
        _ada_kernel,
        grid=(N_MOD,),
        in_specs=[pl.BlockSpec((B, D), lambda j: (0, 0)),
                  pl.BlockSpec((D, D), lambda j: (0, j)),
                  pl.BlockSpec((1, D), lambda j: (0, j))],
        out_specs=pl.BlockSpec((B, D), lambda j: (0, j)),
        out_shape=jax.ShapeDtypeStruct((B, N_MOD * D), F32),
        compiler_params=_params(("arbitrary",)),
        name="ada",
    )(c, w_ada, b_ada.reshape(1, N_MOD * D))


def _head_norm(v, bd, gain):
    sq = v * v
    sh, sl = _split(sq)
    ms = _dot(sh, bd) + _dot(sl, bd)
    return v * lax.rsqrt(ms + RMS_EPS) * gain


def _inproj_kernel(x_ref, mod_ref, gain_ref, wm_ref, wt_ref, wgn_ref, wgm_ref, bd_ref, qg_ref, kg_ref,
                   u_ref, kvc_ref, qt_ref, ksa_ref, vst_ref, kwa_ref, vwt_ref, gt_ref, gm_ref):
    x = x_ref[...]
    shift = mod_ref[0:1, :]
    scale = mod_ref[1:2, :]
    y = x * lax.rsqrt(jnp.mean(x * x, axis=-1, keepdims=True) + RMS_EPS)
    h = (y * gain_ref[...]) * (1.0 + scale) + shift
    hb = h.astype(BF16)
    main = _dot(hb, wm_ref[...])
    tm = x.shape[0]
    u3 = main[:, :S5_WIDTH].reshape(tm // 8, 8, S5_WIDTH)
    for jb in range(S5_WIDTH // 128):
        t = _segment_transpose(u3[:, :, 128 * jb:128 * (jb + 1)]).reshape(tm // S5_CHUNK, 2, 8, 128)
        for hf in range(2):
            u_ref[:, 8 * jb:8 * (jb + 1), 128 * hf:128 * (hf + 1)] = t[:, hf]
    o = S5_WIDTH
    kvc_ref[...] = main[:, o:o + KV_WIDTH]
    o += KV_WIDTH
    kw = N_KV_HEADS * HEAD_DIM
    bd = bd_ref[...]
    ks = _head_norm(main[:, o:o + kw], bd[:kw, :kw], kg_ref[1:2, :]).astype(BF16)
    kwn = _head_norm(main[:, o + kw:o + 2 * kw], bd[:kw, :kw], kg_ref[2:3, :]).astype(BF16)
    pos = pl.program_id(1) * tm + lax.broadcasted_iota(jnp.int32, (tm, 1), 0)
    lane = lax.broadcasted_iota(jnp.int32, (tm, AUG), 1)
    digits = jnp.where(lane == 0, pos // POS_BASE, jnp.where(lane == 1, pos % POS_BASE, 0))
    digits = digits[:, :AUG - HEAD_DIM].astype(F32).astype(BF16)
    onehot = jnp.where(lane == pos // SEL_BLOCK, 1.0, 0.0).astype(BF16)
    for hh in range(N_KV_HEADS):
        ksa_ref[hh, :, 0:HEAD_DIM] = ks[:, hh * HEAD_DIM:(hh + 1) * HEAD_DIM]
        ksa_ref[hh, :, HEAD_DIM:AUG] = digits
        ksa_ref[hh, :, AUG:] = onehot
        kwa_ref[hh, :, 0:HEAD_DIM] = kwn[:, hh * HEAD_DIM:(hh + 1) * HEAD_DIM]
        kwa_ref[hh, :, HEAD_DIM:] = digits

    nt = (((1,), (1,)), ((), ()))
    tt = lax.dot_general(wt_ref[...], hb, nt, preferred_element_type=F32)
    qt = tt[:NSA_WIDTH]
    sq = qt * qt
    sh, sl = _split(sq)
    qn = (qt * lax.rsqrt(_dot(bd, sh) + _dot(bd, sl) + RMS_EPS) * qg_ref[...] * (HEAD_DIM ** -0.5)).astype(BF16)
    QW = GQA_GROUP * Q_BLOCK
    for qb in range(tm // Q_BLOCK):
        for hd in range(N_HEADS):
            hh, g = divmod(hd, GQA_GROUP)
            qt_ref[hh, :, qb * QW + g * Q_BLOCK:qb * QW + (g + 1) * Q_BLOCK] = (
                qn[hd * HEAD_DIM:(hd + 1) * HEAD_DIM, qb * Q_BLOCK:(qb + 1) * Q_BLOCK])
    vst_ref[...] = tt[NSA_WIDTH:NSA_WIDTH + kw].reshape(N_KV_HEADS, HEAD_DIM, tm).astype(BF16)
    vwt_ref[...] = tt[NSA_WIDTH + kw:].reshape(N_KV_HEADS, HEAD_DIM, tm).astype(BF16)
    gn = jax.nn.sigmoid(lax.dot_general(wgn_ref[...], hb, nt, preferred_element_type=F32))
    for qb in range(tm // Q_BLOCK):
        for br in range(3):
            for hd in range(N_HEADS):
                hh, g = divmod(hd, GQA_GROUP)
                r = br * N_HEADS + hd
                gt_ref[hh, br:br + 1, qb * QW + g * Q_BLOCK:qb * QW + (g + 1) * Q_BLOCK] = (
                    gn[r:r + 1, qb * Q_BLOCK:(qb + 1) * Q_BLOCK])
    gm_ref[...] = jax.nn.sigmoid(_dot(hb, wgm_ref[...])).astype(BF16)


def _s5_chunk_spec(tm):
    assert S5_CHUNK == 16 and S5_GROUP == 16, "the segment transposes assume 16 steps x 16 channels"
    return pl.BlockSpec((tm // S5_CHUNK, None, S5_GROUPS, S5_CW), lambda b, i: (i, b, 0, 0))


def _inproj(x, mod, gain, w_in, q_gain, k_gain):
    B, S, D = x.shape
    tm = min(ROW_TILE, S)
    assert S // SEL_BLOCK <= AUG and tm % Q_BLOCK == 0
    kw = N_KV_HEADS * HEAD_DIM
    cols = np.cumsum((0,) + (S5_WIDTH, NSA_WIDTH, KV_WIDTH, KV_WIDTH, KV_WIDTH, 3 * N_HEADS, 2 * D))
    c_u, c_q, c_kvc, c_kvs, c_kvw, c_gn, c_gm = cols[:7]
    sl = lambda a, n: w_in[:, a:a + n]
    wm = jnp.concatenate([sl(c_u, S5_WIDTH), sl(c_kvc, KV_WIDTH), sl(c_kvs, kw), sl(c_kvw, kw)], axis=1).astype(BF16)
    wt = jnp.concatenate([sl(c_q, NSA_WIDTH), sl(c_kvs + kw, kw), sl(c_kvw + kw, kw)], axis=1).T.astype(BF16)
    wgn = sl(c_gn, 3 * N_HEADS).T.astype(BF16)
    wgm = sl(c_gm, 2 * D).astype(BF16)
    seg = np.arange(NSA_WIDTH) // HEAD_DIM
    bd = jnp.asarray((seg[:, None] == seg[None, :]).astype(np.float32) / HEAD_DIM, BF16)
    qg = jnp.tile(q_gain, N_HEADS).reshape(NSA_WIDTH, 1)
    kg = jnp.tile(k_gain, (1, N_KV_HEADS))
    nq = tm // Q_BLOCK * GQA_GROUP * Q_BLOCK
    row = lambda w: pl.BlockSpec((None, tm, w), lambda b, i: (b, i, 0))
    full = lambda a: pl.BlockSpec(a.shape, lambda b, i: (0,) * a.ndim)
    rows4 = lambda w: pl.BlockSpec((None, N_KV_HEADS, tm, w), lambda b, i: (b, 0, i, 0))
    cols4 = lambda r, w: pl.BlockSpec((None, N_KV_HEADS, r, w), lambda b, i: (b, 0, 0, i))
    nqt = S // Q_BLOCK * GQA_GROUP * Q_BLOCK
    return pl.pallas_call(
        _inproj_kernel,
        grid=(B, S // tm),
        in_specs=[row(D), pl.BlockSpec((None, N_MOD, D), lambda b, i: (b, 0, 0)),
                  full(gain), full(wm), full(wt), full(wgn), full(wgm), full(bd), full(qg), full(kg)],
        out_specs=[_s5_chunk_spec(tm), row(KV_WIDTH), cols4(HEAD_DIM, nq), rows4(2 * AUG), cols4(HEAD_DIM, tm),
                   rows4(AUG), cols4(HEAD_DIM, tm), cols4(3, nq), row(2 * D)],
        out_shape=[jax.ShapeDtypeStruct((S // S5_CHUNK, B, S5_GROUPS, S5_CW), F32),
                   jax.ShapeDtypeStruct((B, S, KV_WIDTH), F32),
                   jax.ShapeDtypeStruct((B, N_KV_HEADS, HEAD_DIM, nqt), BF16),
                   jax.ShapeDtypeStruct((B, N_KV_HEADS, S, 2 * AUG), BF16),
                   jax.ShapeDtypeStruct((B, N_KV_HEADS, HEAD_DIM, S), BF16),
                   jax.ShapeDtypeStruct((B, N_KV_HEADS, S, AUG), BF16),
                   jax.ShapeDtypeStruct((B, N_KV_HEADS, HEAD_DIM, S), BF16),
                   jax.ShapeDtypeStruct((B, N_KV_HEADS, 3, nqt), F32),
                   jax.ShapeDtypeStruct((B, S, 2 * D), BF16)],
        compiler_params=_params(("arbitrary", "arbitrary")),
        name="inproj",
    )(x, mod, gain, wm, wt, wgn, wgm, bd, qg, kg)


def _s5_tables(lam_re, lam_im, log_dt, b_re, b_im, c_re, c_im, d_skip):
    L, P, N, G = S5_CHUNK, S5_GROUP, S5_STATE, S5_GROUPS
    hp = lax.Precision.HIGHEST
    lr, li = lam_re.astype(F32), lam_im.astype(F32)
    dt = jnp.exp(log_dt.astype(F32))[:, None]
    mag = jnp.exp(lr * dt)
    abar_re, abar_im = mag * jnp.cos(li * dt), mag * jnp.sin(li * dt)
    num_re, num_im = abar_re - 1.0, abar_im
    den = lr * lr + li * li
    coef_re = (num_re * lr + num_im * li) / den
    coef_im = (num_im * lr - num_re * li) / den
    br, bi = b_re.astype(F32), b_im.astype(F32)
    bbar_re = coef_re[..., None] * br - coef_im[..., None] * bi
    bbar_im = coef_re[..., None] * bi + coef_im[..., None] * br
    k = jnp.arange(L + 1, dtype=F32)[:, None, None]
    pmag = jnp.exp(lr * dt * k)
    pre, pim = pmag * jnp.cos(li * dt * k), pmag * jnp.sin(li * dt * k)
    cr, ci = c_re.astype(F32), c_im.astype(F32)
    ca_re = cr[None] * pre[:, :, None, :] - ci[None] * pim[:, :, None, :]
    ca_im = cr[None] * pim[:, :, None, :] + ci[None] * pre[:, :, None, :]
    kern = (jnp.einsum('kgpn,gnq->gkqp', ca_re[:L], bbar_re, precision=hp)
            - jnp.einsum('kgpn,gnq->gkqp', ca_im[:L], bbar_im, precision=hp))
    s_i = np.arange(L)[:, None]
    t_i = np.arange(L)[None, :]
    tau = np.clip(t_i - s_i, 0, L - 1)
    causal = jnp.asarray((t_i >= s_i).astype(np.float32))
    mt = kern[:, tau] * causal[None, :, :, None, None]
    mt = mt.transpose(0, 1, 3, 2, 4).reshape(G, L * P, L * P)
    rev = np.arange(L - 1, -1, -1)
    ab_re = pre[rev][..., None] * bbar_re[None] - pim[rev][..., None] * bbar_im[None]
    ab_im = pre[rev][..., None] * bbar_im[None] + pim[rev][..., None] * bbar_re[None]
    ws = jnp.concatenate([ab_re, ab_im], axis=2)
    ws = ws.transpose(1, 0, 3, 2).reshape(G, L * P, 2 * N)
    wo = jnp.concatenate([ca_re[1:], -ca_im[1:]], axis=3)
    wo = wo.transpose(1, 3, 0, 2).reshape(G, 2 * N, L * P)
    al = jnp.stack([pre[L], pim[L]], axis=1)
    dv = jnp.tile(d_skip.astype(F32), (1, L)).reshape(G, 1, L * P)
    return mt, ws, wo, al, dv


def _s5_kernel(u_ref, mth_ref, mtl_ref, wsh_ref, wsl_ref, woh_ref, wol_ref, al_ref, dv_ref,
               y_ref, vr_scr, vi_scr, xr_scr, xi_scr, *, n_chunks, bsz):
    N = S5_STATE
    u = u_ref[...]
    uh, ul = _split(u)
    y = _dot(uh, mth_ref[...]) + (_dot(ul, mth_ref[...]) + _dot(uh, mtl_ref[...]))
    v = _dot(uh, wsh_ref[...]) + (_dot(ul, wsh_ref[...]) + _dot(uh, wsl_ref[...]))
    vr_scr[...] = v[:, :N]
    vi_scr[...] = v[:, N:]
    a_r = al_ref[0:1, :]
    a_i = al_ref[1:2, :]

    def step(c, x):
        xr, xi = x
        r = pl.ds(pl.multiple_of(c * bsz, bsz), bsz)
        xr_scr[r, :] = xr
        xi_scr[r, :] = xi
        return (a_r * xr - a_i * xi + vr_scr[r, :], a_r * xi + a_i * xr + vi_scr[r, :])

    zero = jnp.zeros((bsz, N), F32)
    lax.fori_loop(0, n_chunks, step, (zero, zero), unroll=8)
    xp = jnp.concatenate([xr_scr[...], xi_scr[...]], axis=1)
    y = y + _dot3(xp, woh_ref[...], wol_ref[...])
    y_ref[...] = y + dv_ref[...] * u


def _s5(ug, tables):
    nc, B, G, cw = ug.shape
    N = S5_STATE
    mt, ws, wo, al, dv = tables
    mth, mtl = _split(mt)
    wsh, wsl = _split(ws)
    woh, wol = _split(wo)
    grp = lambda a: pl.BlockSpec((None,) + a.shape[1:], lambda g: (g,) + (0,) * (a.ndim - 1))
    ugt = ug.reshape(nc * B, G, cw).transpose(1, 0, 2)
    y = pl.pallas_call(
        functools.partial(_s5_kernel, n_chunks=nc, bsz=B),
        grid=(G,),
        in_specs=[grp(ugt), grp(mth), grp(mtl), grp(wsh), grp(wsl), grp(woh), grp(wol), grp(al), grp(dv)],
        out_specs=grp(ugt),
        out_shape=jax.ShapeDtypeStruct(ugt.shape, F32),
        scratch_shapes=[pltpu.VMEM((nc * B, N), F32)] * 4,
        compiler_params=_params(("arbitrary",)),
        name="s5",
    )(ugt, mth, mtl, wsh, wsl, woh, wol, al, dv)
    return y.transpose(1, 0, 2).reshape(nc, B, G, cw)


def _compress_kernel(x_ref, pe_ref, w1_ref, b1_ref, w2_ref, b2_ref, kg_ref, o_ref, *, n_rows):
    x = x_ref[...]
    half = CMP_STRIDE * HEAD_DIM
    a = _dot((x + pe_ref[0:1, :]).astype(BF16), w1_ref[:half, :])
    b = _dot((x + pe_ref[1:2, :]).astype(BF16), w1_ref[half:, :])
    hid = jax.nn.gelu(a + pltpu.roll(b, n_rows - 1, axis=0) + b1_ref[...])
    out = _dot(hid.astype(BF16), w2_ref[...]) + b2_ref[...]
    normed = out * lax.rsqrt(jnp.mean(out * out, axis=-1, keepdims=True) + RMS_EPS) * kg_ref[...]
    is_key = pl.program_id(1) < N_KV_HEADS
    o_ref[...] = jnp.where(is_key, normed, out).astype(BF16)


def _compress(kv_c, cmp_pe, cmp_w1, cmp_b1, cmp_w2, cmp_b2, k_gain0):
    B, S, _ = kv_c.shape
    nr = S // CMP_STRIDE
    half = CMP_STRIDE * HEAD_DIM
    nj = 2 * N_KV_HEADS
    xc = kv_c.reshape(B, nr, CMP_STRIDE, nj, HEAD_DIM).transpose(0, 3, 1, 2, 4).reshape(B, nj, nr, half)
    pe = cmp_pe.reshape(2, 2, half)
    w1 = cmp_w1.astype(BF16)
    w2 = cmp_w2.astype(BF16)
    b1 = cmp_b1.reshape(2, 1, CMP_HIDDEN)
    b2 = cmp_b2.reshape(2, 1, HEAD_DIM)
    kv = lambda a: pl.BlockSpec((None,) + a.shape[1:], lambda b, j: (j // N_KV_HEADS,) + (0,) * (a.ndim - 1))
    return pl.pallas_call(
        functools.partial(_compress_kernel, n_rows=nr),
        grid=(B, nj),
        in_specs=[pl.BlockSpec((None, None, nr, half), lambda b, j: (b, j, 0, 0)),
                  kv(pe), kv(w1), kv(b1), kv(w2), kv(b2),
                  pl.BlockSpec((1, HEAD_DIM), lambda b, j: (0, 0))],
        out_specs=pl.BlockSpec((None, None, nr, HEAD_DIM), lambda b, j: (b, j, 0, 0)),
        out_shape=jax.ShapeDtypeStruct((B, nj, nr, HEAD_DIM), BF16),
        compiler_params=_params(("arbitrary", "arbitrary")),
        name="compress",
    )(xc, pe, w1, b1, w2, b2, k_gain0.reshape(1, HEAD_DIM))


def _softmax_cols(s, mask):
    s = jnp.where(mask, s, NEG_INF)
    m = jnp.max(s, axis=0, keepdims=True)
    p = jnp.where(mask, jnp.exp(s - m), 0.0)
    return p / jnp.maximum(jnp.sum(p, axis=0, keepdims=True), 1e-20)


def _attend_kernel(qt_ref, kc_ref, vct_ref, ks_ref, vst_ref, kw_ref, vwt_ref, g_ref, slope_ref, ovt_ref,
                   o_ref, qa_scr, *, n_sel, n_pick, n_cmp_rows):
    QW = GQA_GROUP * Q_BLOCK
    i = pl.program_id(2)
    t0 = i * Q_BLOCK
    slope = slope_ref[...]
    tq = t0 + (lax.broadcasted_iota(jnp.int32, (1, QW), 1) & (Q_BLOCK - 1))

    r = lax.broadcasted_iota(jnp.int32, (POS_ROWS, QW), 0)
    qa_scr[0:HEAD_DIM, :] = qt_ref[...]
    qa_scr[HEAD_DIM:HEAD_DIM + POS_ROWS, :] = jnp.where(
        r == 0, slope * POS_BASE, jnp.where(r == 1, slope, 0.0)).astype(BF16)
    qa_scr[HEAD_DIM + POS_ROWS:AUG, :] = jnp.zeros((AUG - HEAD_DIM - POS_ROWS, QW), BF16)
    qa = qa_scr[0:AUG, :]

    sc = _dot(kc_ref[...], qa)
    cpos = lax.broadcasted_iota(jnp.int32, (n_cmp_rows, QW), 0) * CMP_STRIDE + (CMP_BLOCK - 1)
    p_c = _softmax_cols(sc, cpos <= tq)
    o_c = _dot(vct_ref[...], p_c.astype(BF16))
    psum = p_c[:, 0:Q_BLOCK]
    for g in range(1, GQA_GROUP):
        psum = psum + p_c[:, g * Q_BLOCK:(g + 1) * Q_BLOCK]
    ph, pl_ = _split(psum)
    imp = _dot(ovt_ref[...], ph) + _dot(ovt_ref[...], pl_)

    jb = lax.broadcasted_iota(jnp.int32, (n_sel, Q_BLOCK), 0)
    cur = (t0 + lax.broadcasted_iota(jnp.int32, (1, Q_BLOCK), 1)) // SEL_BLOCK
    forced = (jb == 0) | (jb == cur) | (jb == cur - 1)
    imp = jnp.where(forced, imp + FORCE_BONUS, imp)
    imp = jnp.where(jb <= cur, imp, -1.0)
    bias = jnp.full((n_sel, Q_BLOCK), NEG_INF, F32)
    for _ in range(n_pick):
        m = jnp.max(imp, axis=0, keepdims=True)
        first = jnp.min(jnp.where(imp == m, jb, n_sel), axis=0, keepdims=True)
        hit = jb == first
        bias = jnp.where(hit, 0.0, bias)
        imp = jnp.where(hit, -jnp.inf, imp)
    bias = jnp.concatenate([bias] * GQA_GROUP, axis=1).astype(BF16)
    qa_scr[AUG:, :] = jnp.concatenate([bias, jnp.zeros((AUG - n_sel, QW), BF16)], axis=0)

    KC = SEL_KEY_CHUNK
    n_chunks = (t0 + Q_BLOCK + KC - 1) // KC

    def scores(j):
        k0 = pl.multiple_of(j * KC, KC)
        return _dot(ks_ref[pl.ds(k0, KC), :], qa_scr[...])

    def accumulate(carry, j, s, p_of):
        m, l, acc = carry
        k0 = pl.multiple_of(j * KC, KC)
        m_new = jnp.maximum(m, jnp.max(s, axis=0, keepdims=True))
        alpha = jnp.exp(m - m_new)
        p = p_of(jnp.exp(s - m_new))
        l = alpha * l + jnp.sum(p, axis=0, keepdims=True)
        acc = alpha * acc + _dot(vst_ref[:, pl.ds(k0, KC)], p.astype(BF16))
        return m_new, l, acc

    def past_step(j, state):
        return accumulate(state, j, scores(j), lambda p: p)

    init = (jnp.full((1, QW), NEG_INF, F32), jnp.zeros((1, QW), F32), jnp.zeros((HEAD_DIM, QW), F32))
    state = lax.fori_loop(0, n_chunks - 1, past_step, init)
    last = n_chunks - 1
    visible = (last * KC + lax.broadcasted_iota(jnp.int32, (KC, QW), 0)) <= tq
    _, l_s, acc_s = accumulate(state, last, jnp.where(visible, scores(last), NEG_INF),
                               lambda p: jnp.where(visible, p, 0.0))
    o_s = acc_s / jnp.maximum(l_s, 1e-20)

    WK = WINDOW + Q_BLOCK
    w0 = pl.multiple_of(jnp.maximum(t0 - WINDOW, 0), Q_BLOCK)
    sw = _dot(kw_ref[pl.ds(w0, WK), :], qa)
    dist = tq - (w0 + lax.broadcasted_iota(jnp.int32, (WK, QW), 0))
    in_win = (dist | (WINDOW - 1 - dist)) >= 0
    p_w = _softmax_cols(sw, in_win)
    o_w = _dot(vwt_ref[:, pl.ds(w0, WK)], p_w.astype(BF16))

    o = g_ref[0:1, :] * o_c + g_ref[1:2, :] * o_s + g_ref[2:3, :] * o_w
    pairs = []
    for g in range(0, GQA_GROUP, 2):
        sq = jnp.concatenate([o[:, g * Q_BLOCK:(g + 1) * Q_BLOCK], o[:, (g + 1) * Q_BLOCK:(g + 2) * Q_BLOCK]], axis=0)
        pairs.append(sq.T)
    o_ref[...] = jnp.concatenate(pairs, axis=1).astype(BF16)


def _attend(qt, ks, vst, kwn, vwt, gt, kvc_c):
    B, _, S, _ = ks.shape
    nqb = S // Q_BLOCK
    QW = GQA_GROUP * Q_BLOCK
    n_sel = S // SEL_BLOCK
    n_pick = min(SEL_TOPK, n_sel)
    nr = S // CMP_STRIDE
    n_cmp = (S - CMP_BLOCK) // CMP_STRIDE + 1
    assert n_sel % 16 == 0 and n_sel <= AUG

    cpos = np.arange(nr) * CMP_STRIDE + CMP_BLOCK - 1
    digits = np.zeros((nr, AUG - HEAD_DIM), np.float32)
    digits[:, 0] = cpos // POS_BASE
    digits[:, 1] = cpos % POS_BASE
    kc = kvc_c[:, :N_KV_HEADS]
    kc = jnp.concatenate([kc, jnp.broadcast_to(jnp.asarray(digits, BF16), kc.shape[:2] + digits.shape)], axis=-1)
    vct = kvc_c[:, N_KV_HEADS:].transpose(0, 1, 3, 2)
    slopes = 2.0 ** (-8.0 * np.arange(1, N_HEADS + 1) / N_HEADS)
    slope_t = jnp.asarray(np.repeat(slopes.reshape(N_KV_HEADS, GQA_GROUP), Q_BLOCK, axis=1)
                          .reshape(N_KV_HEADS, 1, QW), F32)
    cmp_start = np.arange(n_cmp) * CMP_STRIDE
    sel_start = np.arange(n_sel) * SEL_BLOCK
    ov = ((cmp_start[:, None] <= sel_start[None, :] + SEL_BLOCK - 1)
          & (cmp_start[:, None] + CMP_BLOCK - 1 >= sel_start[None, :])).astype(np.float32)
    ovt = np.zeros((n_sel, nr), np.float32)
    ovt[:, :n_cmp] = ov.T
    ovt = jnp.asarray(ovt, BF16)

    per_bh = lambda r, c: pl.BlockSpec((None, None, r, c), lambda b, h, i: (b, h, 0, 0))
    per_q = lambda r: pl.BlockSpec((None, None, r, QW), lambda b, h, i: (b, h, 0, i))
    return pl.pallas_call(
        functools.partial(_attend_kernel, n_sel=n_sel, n_pick=n_pick, n_cmp_rows=nr),
        grid=(B, N_KV_HEADS, nqb),
        in_specs=[per_q(HEAD_DIM), per_bh(nr, AUG), per_bh(HEAD_DIM, nr),
                  per_bh(S, 2 * AUG), per_bh(HEAD_DIM, S), per_bh(S, AUG), per_bh(HEAD_DIM, S),
                  per_q(3),
                  pl.BlockSpec((None, 1, QW), lambda b, h, i: (h, 0, 0)),
                  pl.BlockSpec((n_sel, nr), lambda b, h, i: (0, 0))],
        out_specs=pl.BlockSpec((None, Q_BLOCK, GQA_GROUP * HEAD_DIM), lambda b, h, i: (b, i, h)),
        out_shape=jax.ShapeDtypeStruct((B, S, NSA_WIDTH), BF16),
        scratch_shapes=[pltpu.VMEM((2 * AUG, QW), BF16)],
        compiler_params=_params(("arbitrary", "arbitrary", "arbitrary")),
        name="attend",
    )(qt, kc, vct, ks, vst, kwn, vwt, gt, slope_t, ovt)


def _merge_kernel(x_ref, mod_ref, ys_ref, on_ref, gm_ref, wglu_ref, bglu_ref, wa_ref, wb_ref, wo_ref,
                  gain_ref, wrh_ref, wrl_ref, x1_ref, h2_ref, sc_ref):
    D = D_MODEL
    tm = x_ref.shape[0]
    cols = []
    for jb in range(S5_WIDTH // 128):
        halves = [_segment_transpose(ys_ref[:, 8 * jb:8 * (jb + 1), 128 * hf:128 * (hf + 1)]) for hf in range(2)]
        cols.append(jnp.stack(halves, axis=1).reshape(tm, 128))
    z = jax.nn.gelu(jnp.concatenate(cols, axis=1))
    glu = z * jax.nn.sigmoid(_dot(z.astype(BF16), wglu_ref[...]) + bglu_ref[...])
    ya = _dot(glu.astype(BF16), wa_ref[...])
    yb = _dot(on_ref[...], wb_ref[...])
    merged = gm_ref[:, :D].astype(F32) * ya + gm_ref[:, D:].astype(F32) * yb
    x1 = x_ref[...] + mod_ref[2:3, :] * _dot(merged.astype(BF16), wo_ref[...])
    x1_ref[...] = x1
    y = x1 * lax.rsqrt(jnp.mean(x1 * x1, axis=-1, keepdims=True) + RMS_EPS)
    h2 = (y * gain_ref[...]) * (1.0 + mod_ref[4:5, :]) + mod_ref[3:4, :]
    h2_ref[...] = h2.astype(BF16)
    hh, hl = _split(h2)
    nt = (((1,), (1,)), ((), ()))
    dg = lambda a, b: lax.dot_general(a, b, nt, preferred_element_type=F32)
    logits = dg(wrh_ref[...], hh) + (dg(wrh_ref[...], hl) + dg(wrl_ref[...], hh))
    sc_ref[...] = jax.nn.sigmoid(logits)


def _merge(x, mod, ys5, o_nsa, gm, w_glu, b_glu, w_a, w_b, w_out, gain_f, w_router):
    B, S, D = x.shape
    tm = min(ROW_TILE, S)
    wrh, wrl = _split(w_router.T)
    ws = [w_glu.astype(BF16), b_glu.reshape(1, -1), w_a.astype(BF16), w_b.astype(BF16), w_out.astype(BF16),
          gain_f, wrh, wrl]
    row = lambda w: pl.BlockSpec((None, tm, w), lambda b, i: (b, i, 0))
    full = lambda a: pl.BlockSpec(a.shape, lambda b, i: (0,) * a.ndim)
    return pl.pallas_call(
        _merge_kernel,
        grid=(B, S // tm),
        in_specs=[row(D), pl.BlockSpec((None, N_MOD, D), lambda b, i: (b, 0, 0)),
                  _s5_chunk_spec(tm), row(NSA_WIDTH), row(2 * D)] + [full(a) for a in ws],
        out_specs=[row(D), row(D), pl.BlockSpec((None, N_EXPERTS, tm), lambda b, i: (b, 0, i))],
        out_shape=[jax.ShapeDtypeStruct((B, S, D), F32), jax.ShapeDtypeStruct((B, S, D), BF16),
                   jax.ShapeDtypeStruct((B, N_EXPERTS, S), F32)],
        compiler_params=_params(("arbitrary", "arbitrary")),
        name="merge",
    )(x, mod, ys5, o_nsa, gm, *ws)


def _first_argmax_rows(v, idx, n):
    m = jnp.max(v, axis=0, keepdims=True)
    first = jnp.min(jnp.where(v == m, idx, n), axis=0, keepdims=True)
    return idx == first, m


def _route_kernel(sc_ref, bias_ref, tri_ref, rank_ref, w_ref, cnt_ref):
    E, NG = N_EXPERTS, N_EXPERT_GROUPS
    GS = E // NG
    sc = sc_ref[...]
    TM = sc.shape[1]
    sel = sc + bias_ref[...]
    i8 = lax.broadcasted_iota(jnp.int32, (GS, TM), 0)
    gscore = []
    for g in range(NG):
        blk = sel[g * GS:(g + 1) * GS, :]
        hit, m1 = _first_argmax_rows(blk, i8, GS)
        m2 = jnp.max(jnp.where(hit, -jnp.inf, blk), axis=0, keepdims=True)
        gscore.append(m1 + m2)
    gscore = jnp.concatenate(gscore, axis=0)
    ig = lax.broadcasted_iota(jnp.int32, (NG, TM), 0)
    gmask = jnp.zeros((NG, TM), F32)
    for _ in range(TOPK_EXPERT_GROUPS):
        hit, _m = _first_argmax_rows(gscore, ig, NG)
        gmask = jnp.where(hit, 1.0, gmask)
        gscore = jnp.where(hit, -jnp.inf, gscore)
    emask = jnp.concatenate([jnp.broadcast_to(gmask[g:g + 1, :], (GS, TM)) for g in range(NG)], axis=0)
    cand = jnp.where(emask > 0.5, sel, NEG_INF)
    ie = lax.broadcasted_iota(jnp.int32, (E, TM), 0)
    chosen = jnp.zeros((E, TM), F32)
    for _ in range(TOP_K):
        hit, _m = _first_argmax_rows(cand, ie, E)
        chosen = jnp.where(hit, 1.0, chosen)
        cand = jnp.where(hit, -jnp.inf, cand)
    w = chosen * sc
    w_ref[...] = w / jnp.sum(w, axis=0, keepdims=True) * ROUTED_SCALE
    cb = chosen.astype(BF16)
    prefix = _dot(cb, tri_ref[...])
    rank_ref[...] = jnp.where(chosen > 0.5, prefix, RANK_NONE)
    cnt_ref[...] = _dot(cb, jnp.ones((TM, 128), BF16))


def _route(scores_t, router_bias):
    B, E, S = scores_t.shape
    TM = min(ROUTE_TILE, S)
    nj = S // TM
    ns = B * nj
    tri = jnp.asarray(np.triu(np.ones((TM, TM), np.float32), k=1), BF16)
    tile = lambda w: pl.BlockSpec((None, E, w), lambda b, j: (b * nj + j, 0, 0))
    return pl.pallas_call(
        _route_kernel,
        grid=(B, nj),
        in_specs=[pl.BlockSpec((None, E, TM), lambda b, j: (b, 0, j)),
                  pl.BlockSpec((E, 1), lambda b, j: (0, 0)),
                  pl.BlockSpec((TM, TM), lambda b, j: (0, 0))],
        out_specs=[tile(TM), tile(TM), tile(128)],
        out_shape=[jax.ShapeDtypeStruct((ns, E, TM), F32), jax.ShapeDtypeStruct((ns, E, TM), F32),
                   jax.ShapeDtypeStruct((ns, E, 128), F32)],
        compiler_params=_params(("arbitrary", "arbitrary")),
        name="route",
    )(scores_t, router_bias.reshape(E, 1), tri)


def _moe_kernel(cnt_ref, h_ref, x1_ref, mod_ref, rank_ref, w_ref, wg_ref, wu_ref, wd_ref,
                sg_ref, su_ref, sd_ref, o_ref, xc, yc, pc, wc, *, n_sub, tm, n_exp):
    to = pl.program_id(0)
    eb = pl.program_id(1)
    gate = mod_ref[5:6, :]

    @pl.when(eb == 0)
    def _shared():
        for s in range(n_sub):
            rows = pl.ds(s * tm, tm)
            hs = h_ref[rows, :]
            hid = jax.nn.silu(_dot(hs, sg_ref[...])) * _dot(hs, su_ref[...])
            o_ref[rows, :] = x1_ref[rows, :] + gate * _dot(hid.astype(BF16), sd_ref[...])

    PIECE, WIN = MOE_PIECE, MOE_WIN
    tn = (((0,), (0,)), ((), ()))
    experts = [eb * n_exp + j for j in range(n_exp)]
    cnt = [[cnt_ref[(to * n_sub + s) * N_EXPERTS + e] for e in experts] for s in range(n_sub)]
    off = []
    for s in range(n_sub):
        o = [jnp.int32(0)]
        for j in range(n_exp):
            o.append(o[-1] + ((cnt[s][j] + 15) // 16) * 16)
        off.append(o)
    n_pieces = [jnp.maximum((off[s][n_exp] + PIECE - 1) // PIECE, 2) for s in range(n_sub)]
    fits = n_pieces[0] < MOE_ROWS // PIECE
    for s in range(1, n_sub):
        fits = jnp.logical_and(fits, n_pieces[s] < MOE_ROWS // PIECE)

    def one_hot_rows(s, base, rows_n, js):
        slot = (lax.broadcasted_iota(jnp.int32, (rows_n, tm), 0) + base).astype(F32)
        pick = jnp.zeros((rows_n, tm), F32)
        wacc = jnp.zeros((rows_n, tm), F32)
        for j in js:
            hit = (slot - off[s][j].astype(F32)) == rank_ref[s, pl.ds(experts[j], 1), :]
            pick = jnp.where(hit, 1.0, pick)
            wacc = jnp.where(hit, w_ref[s, pl.ds(experts[j], 1), :], wacc)
        return pick.astype(BF16), jnp.sum(wacc, axis=1, keepdims=True)

    def mlp(xg, j):
        hid = jax.nn.silu(_dot(xg, wg_ref[j])) * _dot(xg, wu_ref[j])
        return _dot(hid.astype(BF16), wd_ref[j])

    @pl.when(fits)
    def _packed():
        for s in range(n_sub):
            rows = pl.ds(s * tm, tm)

            def piece(q, carry, s=s, rows=rows):
                r = pl.ds(q * PIECE if isinstance(q, int) else pl.multiple_of(q * PIECE, PIECE), PIECE)
                pick, wrow = one_hot_rows(s, q * PIECE, PIECE, range(n_exp))
                xc[s, r, :] = _dot(pick, h_ref[rows, :]).astype(BF16)
                pc[s, r, :] = pick
                wc[s, r, :] = wrow
                yc[s, r, :] = jnp.zeros((PIECE, D_MODEL), BF16)
                return carry

            piece(0, 0)
            piece(1, 0)
            lax.fori_loop(2, n_pieces[s], piece, 0)
            r = pl.ds(pl.multiple_of(n_pieces[s] * PIECE, PIECE), PIECE)
            xc[s, r, :] = jnp.zeros((PIECE, D_MODEL), BF16)
            yc[s, r, :] = jnp.zeros((PIECE, D_MODEL), BF16)
            pc[s, r, :] = jnp.zeros((PIECE, tm), BF16)
            wc[s, r, :] = jnp.zeros((PIECE, 1), F32)

        for j in range(n_exp):
            most = cnt[0][j]
            for s in range(1, n_sub):
                most = jnp.maximum(most, cnt[s][j])

            def window(p, carry, j=j):
                starts = [pl.multiple_of(jnp.minimum(off[s][j] + p * WIN, (n_pieces[s] + 1) * PIECE - WIN), 16)
                          for s in range(n_sub)]
                xg = jnp.concatenate([xc[s, pl.ds(starts[s], WIN), :] for s in range(n_sub)], axis=0)
                out = mlp(xg, j)
                for s in range(n_sub):
                    r = pl.ds(starts[s], WIN)
                    yc[s, r, :] = (out[s * WIN:(s + 1) * WIN] * wc[s, r, :] * gate).astype(BF16)
                return carry

            window(0, 0)
            lax.fori_loop(1, (most + WIN - 1) // WIN, window, 0)

        for s in range(n_sub):
            rows = pl.ds(s * tm, tm)

            def pair(q, carry, s=s, rows=rows):
                r = pl.ds(q * 2 * PIECE if isinstance(q, int) else pl.multiple_of(q * 2 * PIECE, 2 * PIECE),
                          2 * PIECE)
                o_ref[rows, :] += lax.dot_general(pc[s, r, :], yc[s, r, :], tn, preferred_element_type=F32)
                return carry

            pair(0, 0)
            lax.fori_loop(1, (n_pieces[s] + 1) // 2, pair, 0)

    @pl.when(jnp.logical_not(fits))
    def _unpacked():
        for j in range(n_exp):
            for s in range(n_sub):
                rows = pl.ds(s * tm, tm)

                def block(bi, carry, j=j, s=s, rows=rows):
                    slot0 = off[s][j] + bi * PIECE
                    pick, wrow = one_hot_rows(s, slot0, PIECE, [j])
                    out = mlp(_dot(pick, h_ref[rows, :]).astype(BF16), j)
                    ow = (out * wrow * gate).astype(BF16)
                    o_ref[rows, :] += lax.dot_general(pick, ow, tn, preferred_element_type=F32)
                    return carry

                lax.fori_loop(0, (cnt[s][j] + PIECE - 1) // PIECE, block, 0)


def _moe(h2, x1, mod, rank_t, w_t, counts, w_gate, w_up, w_down, ws_gate, ws_up, ws_down, seq):
    T, D = h2.shape
    ns, E, tm = rank_t.shape
    tmo = min(MOE_OUTER, seq)
    n_sub = tmo // tm
    n_exp = MOE_EXPERTS_PER_STEP
    per_b = seq // tmo
    ws = [w_gate.astype(BF16), w_up.astype(BF16), w_down.astype(BF16)]
    sh = [ws_gate.astype(BF16), ws_up.astype(BF16), ws_down.astype(BF16)]
    tok = lambda: pl.BlockSpec((tmo, D), lambda t, e, c: (t, 0))
    sub = lambda: pl.BlockSpec((n_sub, E, tm), lambda t, e, c: (t, 0, 0))
    exp = lambda a: pl.BlockSpec((n_exp,) + a.shape[1:], lambda t, e, c: (e, 0, 0))
    full = lambda a: pl.BlockSpec(a.shape, lambda t, e, c: (0, 0))
    grid_spec = pltpu.PrefetchScalarGridSpec(
        num_scalar_prefetch=1,
        grid=(T // tmo, E // n_exp),
        in_specs=[tok(), tok(), pl.BlockSpec((None, N_MOD, D), lambda t, e, c: (t // per_b, 0, 0)),
                  sub(), sub(), exp(ws[0]), exp(ws[1]), exp(ws[2]), full(sh[0]), full(sh[1]), full(sh[2])],
        out_specs=tok(),
        scratch_shapes=[pltpu.VMEM((n_sub, MOE_ROWS, D), BF16), pltpu.VMEM((n_sub, MOE_ROWS, D), BF16),
                        pltpu.VMEM((n_sub, MOE_ROWS, tm), BF16), pltpu.VMEM((n_sub, MOE_ROWS, 1), F32)],
    )
    return pl.pallas_call(
        functools.partial(_moe_kernel, n_sub=n_sub, tm=tm, n_exp=n_exp),
        grid_spec=grid_spec,
        out_shape=jax.ShapeDtypeStruct((T, D), F32),
        compiler_params=_params(("arbitrary", "arbitrary")),
        name="moe",
    )(counts, h2, x1, mod, rank_t, w_t, *ws, *sh)


def kernel(x, c, w_ada, b_ada, norm_mix_gain, norm_ffn_gain, w_in, s5_lambda_re, s5_lambda_im, s5_log_dt, s5_b_re, s5_b_im, s5_c_re, s5_c_im, s5_d, s5_w_glu, s5_b_glu, q_norm_gain, k_norm_gain, cmp_pe, cmp_w1, cmp_b1, cmp_w2, cmp_b2, w_branch_a, w_branch_b, w_out, w_router, router_bias, w_gate, w_up, w_down, ws_gate, ws_up, ws_down):
    B, S, D = x.shape
    for l in range(w_ada.shape[0]):
        mod = _ada(c, w_ada[l], b_ada[l]).reshape(B, N_MOD, D)
        u, kvc, qt, ks, vst, kwn, vwt, gt, gm = _inproj(x, mod, norm_mix_gain[l:l + 1], w_in[l],
                                                        q_norm_gain[l], k_norm_gain[l])
        tables = _s5_tables(s5_lambda_re[l], s5_lambda_im[l], s5_log_dt[l], s5_b_re[l], s5_b_im[l],
                            s5_c_re[l], s5_c_im[l], s5_d[l])
        ys5 = _s5(u, tables)
        kvc_c = _compress(kvc, cmp_pe[l], cmp_w1[l], cmp_b1[l], cmp_w2[l], cmp_b2[l], k_norm_gain[l, 0])
        o_nsa = _attend(qt, ks, vst, kwn, vwt, gt, kvc_c)
        x1, h2, scores_t = _merge(x, mod, ys5, o_nsa, gm, s5_w_glu[l], s5_b_glu[l], w_branch_a[l],
                                  w_branch_b[l], w_out[l], norm_ffn_gain[l:l + 1], w_router[l])
        rank_t, w_t, cnt = _route(scores_t, router_bias[l])
        counts = cnt[:, :, 0].astype(jnp.int32).reshape(-1)
        x = _moe(h2.reshape(B * S, D), x1.reshape(B * S, D), mod, rank_t, w_t, counts,
                 w_gate[l], w_up[l], w_down[l], ws_gate[l], ws_up[l], ws_down[l], S).reshape(B, S, D)
    return x
```

```python
import functools
import math

import numpy as np
import jax
import jax.numpy as jnp
from jax import lax
from jax.experimental import pallas as pl
from jax.experimental.pallas import tpu as pltpu

F32 = jnp.float32
BF16 = jnp.bfloat16

D_MODEL = 1024
S5_WIDTH = 512
S5_GROUP = 16
S5_GROUPS = S5_WIDTH // S5_GROUP
S5_STATE = 64
N_HEADS = 8
N_KV_HEADS = 2
GQA_GROUP = N_HEADS // N_KV_HEADS
HEAD_DIM = 64
NSA_WIDTH = N_HEADS * HEAD_DIM
KV_WIDTH = 2 * N_KV_HEADS * HEAD_DIM
CMP_BLOCK = 32
CMP_STRIDE = 16
CMP_HIDDEN = 256
SEL_BLOCK = 64
SEL_TOPK = 8
WINDOW = 256
Q_BLOCK = 128
FORCE_BONUS = 1e3
N_EXPERTS = 64
TOP_K = 8
N_EXPERT_GROUPS = 8
TOPK_EXPERT_GROUPS = 4
EXPERT_HIDDEN = 256
SHARED_HIDDEN = 256
ROUTED_SCALE = 2.5
RMS_EPS = 1e-6
NEG_INF = -1e30
N_MOD = 6

S5_CHUNK = 16
S5_CW = S5_CHUNK * S5_GROUP
ROW_TILE = 512
SEL_KEY_CHUNK = 512
POS_BASE = 64
POS_ROWS = 16
AUG = 128
ROUTE_TILE = 256
MOE_OUTER = 1024
MOE_EXPERTS_PER_STEP = 4
MOE_PIECE = 128
MOE_WIN = 48
MOE_ROWS = 512
RANK_NONE = -float(1 << 20)
VMEM_LIMIT = 56 * 1024 * 1024


def _dot(a, b):
    return jnp.dot(a, b, preferred_element_type=F32)


def _split(a):
    hi = a.astype(BF16)
    lo = (a - hi.astype(F32)).astype(BF16)
    return hi, lo


def _dot3(a, bh, bl):
    ah, al = _split(a)
    return _dot(ah, bh) + (_dot(al, bh) + _dot(ah, bl))


def _segment_transpose(x):
    nd = x.ndim
    i = lax.broadcasted_iota(jnp.int32, x.shape, nd - 2)
    seg = lax.broadcasted_iota(jnp.int32, x.shape, nd - 1) // S5_GROUP
    out = x
    for d in range(1, 8):
        r = pltpu.roll(pltpu.roll(x, 8 - d, axis=nd - 2), S5_GROUP * d, axis=nd - 1)
        out = jnp.where(seg == ((i + d) & 7), r, out)
    return out


def _params(sem):
    return pltpu.CompilerParams(dimension_semantics=sem, vmem_limit_bytes=VMEM_LIMIT)


def _ada_kernel(c_ref, w_ref, b_ref, o_ref):
    cs = jax.nn.silu(c_ref[...])
    wh, wl = _split(w_ref[...])
    o_ref[...] = _dot3(cs, wh, wl) + b_ref[...]


def _ada(c, w_ada, b_ada):
    B, D = c.shape
    return pl.pallas_call(
        _ada_kernel,
        grid=(N_MOD,),
        in_specs=[pl.BlockSpec((B, D), lambda j: (0, 0)),
                  pl.BlockSpec((D, D), lambda j: (0, j)),
                  pl.BlockSpec((1, D), lambda j: (0, j))],
        out_specs=pl.BlockSpec((B, D), lambda j: (0, j)),
        out_shape=jax.ShapeDtypeStruct((B, N_MOD * D), F32),
        compiler_params=_params(("arbitrary",)),
        name="ada",
    )(c, w_ada, b_ada.reshape(1, N_MOD * D))


def _head_norm(v, bd, gain):
    sq = v * v
    sh, sl = _split(sq)
    ms = _dot(sh, bd) + _dot(sl, bd)
    return v * lax.rsqrt(ms + RMS_EPS) * gain


def _inproj_kernel(x_ref, mod_ref, gain_ref, wm_ref, wt_ref, wgn_ref, wgm_ref, bd_ref, qg_ref, kg_ref,
                   u_ref, kvc_ref, qt_ref, ksa_ref, vst_ref, kwa_ref, vwt_ref, gt_ref, gm_ref):
    x = x_ref[...]
    shift = mod_ref[0:1, :]
    scale = mod_ref[1:2, :]
    y = x * lax.rsqrt(jnp.mean(x * x, axis=-1, keepdims=True) + RMS_EPS)
    h = (y * gain_ref[...]) * (1.0 + scale) + shift
    hb = h.astype(BF16)
    main = _dot(hb, wm_ref[...])
    tm = x.shape[0]
    u3 = main[:, :S5_WIDTH].reshape(tm // 8, 8, S5_WIDTH)
    for jb in range(S5_WIDTH // 128):
        t = _segment_transpose(u3[:, :, 128 * jb:128 * (jb + 1)]).reshape(tm // S5_CHUNK, 2, 8, 128)
        for hf in range(2):
            u_ref[:, 8 * jb:8 * (jb + 1), 128 * hf:128 * (hf + 1)] = t[:, hf]
    o = S5_WIDTH
    kvc_ref[...] = main[:, o:o + KV_WIDTH]
    o += KV_WIDTH
    kw = N_KV_HEADS * HEAD_DIM
    bd = bd_ref[...]
    ks = _head_norm(main[:, o:o + kw], bd[:kw, :kw], kg_ref[1:2, :]).astype(BF16)
    kwn = _head_norm(main[:, o + kw:o + 2 * kw], bd[:kw, :kw], kg_ref[2:3, :]).astype(BF16)
    pos = pl.program_id(1) * tm + lax.broadcasted_iota(jnp.int32, (tm, 1), 0)
    lane = lax.broadcasted_iota(jnp.int32, (tm, AUG), 1)
    digits = jnp.where(lane == 0, pos // POS_BASE, jnp.where(lane == 1, pos % POS_BASE, 0))
    digits = digits[:, :AUG - HEAD_DIM].astype(F32).astype(BF16)
    onehot = jnp.where(lane == pos // SEL_BLOCK, 1.0, 0.0).astype(BF16)
    for hh in range(N_KV_HEADS):
        ksa_ref[hh, :, 0:HEAD_DIM] = ks[:, hh * HEAD_DIM:(hh + 1) * HEAD_DIM]
        ksa_ref[hh, :, HEAD_DIM:AUG] = digits
        ksa_ref[hh, :, AUG:] = onehot
        kwa_ref[hh, :, 0:HEAD_DIM] = kwn[:, hh * HEAD_DIM:(hh + 1) * HEAD_DIM]
        kwa_ref[hh, :, HEAD_DIM:] = digits

    nt = (((1,), (1,)), ((), ()))
    tt = lax.dot_general(wt_ref[...], hb, nt, preferred_element_type=F32)
    qt = tt[:NSA_WIDTH]
    sq = qt * qt
    sh, sl = _split(sq)
    qn = (qt * lax.rsqrt(_dot(bd, sh) + _dot(bd, sl) + RMS_EPS) * qg_ref[...] * (HEAD_DIM ** -0.5)).astype(BF16)
    QW = GQA_GROUP * Q_BLOCK
    for qb in range(tm // Q_BLOCK):
        for hd in range(N_HEADS):
            hh, g = divmod(hd, GQA_GROUP)
            qt_ref[hh, :, qb * QW + g * Q_BLOCK:qb * QW + (g + 1) * Q_BLOCK] = (
                qn[hd * HEAD_DIM:(hd + 1) * HEAD_DIM, qb * Q_BLOCK:(qb + 1) * Q_BLOCK])
    vst_ref[...] = tt[NSA_WIDTH:NSA_WIDTH + kw].reshape(N_KV_HEADS, HEAD_DIM, tm).astype(BF16)
    vwt_ref[...] = tt[NSA_WIDTH + kw:].reshape(N_KV_HEADS, HEAD_DIM, tm).astype(BF16)
    gn = jax.nn.sigmoid(lax.dot_general(wgn_ref[...], hb, nt, preferred_element_type=F32))
    for qb in range(tm // Q_BLOCK):
        for br in range(3):
            for hd in range(N_HEADS):
                hh, g = divmod(hd, GQA_GROUP)
                r = br * N_HEADS + hd
                gt_ref[hh, br:br + 1, qb * QW + g * Q_BLOCK:qb * QW + (g + 1) * Q_BLOCK] = (
                    gn[r:r + 1, qb * Q_BLOCK:(qb + 1) * Q_BLOCK])
    gm_ref[...] = jax.nn.sigmoid(_dot(hb, wgm_ref[...])).astype(BF16)


def _s5_chunk_spec(tm):
    assert S5_CHUNK == 16 and S5_GROUP == 16, "the segment transposes assume 16 steps x 16 channels"
    return pl.BlockSpec((tm // S5_CHUNK, None, S5_GROUPS, S5_CW), lambda b, i: (i, b, 0, 0))


def _inproj(x, mod, gain, w_in, q_gain, k_gain):
    B, S, D = x.shape
    tm = min(ROW_TILE, S)
    assert S // SEL_BLOCK <= AUG and tm % Q_BLOCK == 0
    kw = N_KV_HEADS * HEAD_DIM
    cols = np.cumsum((0,) + (S5_WIDTH, NSA_WIDTH, KV_WIDTH, KV_WIDTH, KV_WIDTH, 3 * N_HEADS, 2 * D))
    c_u, c_q, c_kvc, c_kvs, c_kvw, c_gn, c_gm = cols[:7]
    sl = lambda a, n: w_in[:, a:a + n]
    wm = jnp.concatenate([sl(c_u, S5_WIDTH), sl(c_kvc, KV_WIDTH), sl(c_kvs, kw), sl(c_kvw, kw)], axis=1).astype(BF16)
    wt = jnp.concatenate([sl(c_q, NSA_WIDTH), sl(c_kvs + kw, kw), sl(c_kvw + kw, kw)], axis=1).T.astype(BF16)
    wgn = sl(c_gn, 3 * N_HEADS).T.astype(BF16)
    wgm = sl(c_gm, 2 * D).astype(BF16)
    seg = np.arange(NSA_WIDTH) // HEAD_DIM
    bd = jnp.asarray((seg[:, None] == seg[None, :]).astype(np.float32) / HEAD_DIM, BF16)
    qg = jnp.tile(q_gain, N_HEADS).reshape(NSA_WIDTH, 1)
    kg = jnp.tile(k_gain, (1, N_KV_HEADS))
    nq = tm // Q_BLOCK * GQA_GROUP * Q_BLOCK
    row = lambda w: pl.BlockSpec((None, tm, w), lambda b, i: (b, i, 0))
    full = lambda a: pl.BlockSpec(a.shape, lambda b, i: (0,) * a.ndim)
    rows4 = lambda w: pl.BlockSpec((None, N_KV_HEADS, tm, w), lambda b, i: (b, 0, i, 0))
    cols4 = lambda r, w: pl.BlockSpec((None, N_KV_HEADS, r, w), lambda b, i: (b, 0, 0, i))
    nqt = S // Q_BLOCK * GQA_GROUP * Q_BLOCK
    return pl.pallas_call(
        _inproj_kernel,
        grid=(B, S // tm),
        in_specs=[row(D), pl.BlockSpec((None, N_MOD, D), lambda b, i: (b, 0, 0)),
                  full(gain), full(wm), full(wt), full(wgn), full(wgm), full(bd), full(qg), full(kg)],
        out_specs=[_s5_chunk_spec(tm), row(KV_WIDTH), cols4(HEAD_DIM, nq), rows4(2 * AUG), cols4(HEAD_DIM, tm),
                   rows4(AUG), cols4(HEAD_DIM, tm), cols4(3, nq), row(2 * D)],
        out_shape=[jax.ShapeDtypeStruct((S // S5_CHUNK, B, S5_GROUPS, S5_CW), F32),
                   jax.ShapeDtypeStruct((B, S, KV_WIDTH), F32),
                   jax.ShapeDtypeStruct((B, N_KV_HEADS, HEAD_DIM, nqt), BF16),
                   jax.ShapeDtypeStruct((B, N_KV_HEADS, S, 2 * AUG), BF16),
                   jax.ShapeDtypeStruct((B, N_KV_HEADS, HEAD_DIM, S), BF16),
                   jax.ShapeDtypeStruct((B, N_KV_HEADS, S, AUG), BF16),
                   jax.ShapeDtypeStruct((B, N_KV_HEADS, HEAD_DIM, S), BF16),
                   jax.ShapeDtypeStruct((B, N_KV_HEADS, 3, nqt), F32),
                   jax.ShapeDtypeStruct((B, S, 2 * D), BF16)],
        compiler_params=_params(("arbitrary", "arbitrary")),
        name="inproj",
    )(x, mod, gain, wm, wt, wgn, wgm, bd, qg, kg)


def _s5_tables(lam_re, lam_im, log_dt, b_re, b_im, c_re, c_im, d_skip):
    L, P, N, G = S5_CHUNK, S5_GROUP, S5_STATE, S5_GROUPS
    hp = lax.Precision.HIGHEST
    lr, li = lam_re.astype(F32), lam_im.astype(F32)
    dt = jnp.exp(log_dt.astype(F32))[:, None]
    mag = jnp.exp(lr * dt)
    abar_re, abar_im = mag * jnp.cos(li * dt), mag * jnp.sin(li * dt)
    num_re, num_im = abar_re - 1.0, abar_im
    den = lr * lr + li * li
    coef_re = (num_re * lr + num_im * li) / den
    coef_im = (num_im * lr - num_re * li) / den
    br, bi = b_re.astype(F32), b_im.astype(F32)
    bbar_re = coef_re[..., None] * br - coef_im[..., None] * bi
    bbar_im = coef_re[..., None] * bi + coef_im[..., None] * br
    k = jnp.arange(L + 1, dtype=F32)[:, None, None]
    pmag = jnp.exp(lr * dt * k)
    pre, pim = pmag * jnp.cos(li * dt * k), pmag * jnp.sin(li * dt * k)
    cr, ci = c_re.astype(F32), c_im.astype(F32)
    ca_re = cr[None] * pre[:, :, None, :] - ci[None] * pim[:, :, None, :]
    ca_im = cr[None] * pim[:, :, None, :] + ci[None] * pre[:, :, None, :]
    kern = (jnp.einsum('kgpn,gnq->gkqp', ca_re[:L], bbar_re, precision=hp)
            - jnp.einsum('kgpn,gnq->gkqp', ca_im[:L], bbar_im, precision=hp))
    s_i = np.arange(L)[:, None]
    t_i = np.arange(L)[None, :]
    tau = np.clip(t_i - s_i, 0, L - 1)
    causal = jnp.asarray((t_i >= s_i).astype(np.float32))
    mt = kern[:, tau] * causal[None, :, :, None, None]
    mt = mt.transpose(0, 1, 3, 2, 4).reshape(G, L * P, L * P)
    rev = np.arange(L - 1, -1, -1)
    ab_re = pre[rev][..., None] * bbar_re[None] - pim[rev][..., None] * bbar_im[None]
    ab_im = pre[rev][..., None] * bbar_im[None] + pim[rev][..., None] * bbar_re[None]
    ws = jnp.concatenate([ab_re, ab_im], axis=2)
    ws = ws.transpose(1, 0, 3, 2).reshape(G, L * P, 2 * N)
    wo = jnp.concatenate([ca_re[1:], -ca_im[1:]], axis=3)
    wo = wo.transpose(1, 3, 0, 2).reshape(G, 2 * N, L * P)
    al = jnp.stack([pre[L], pim[L]], axis=1)
    dv = jnp.tile(d_skip.astype(F32), (1, L)).reshape(G, 1, L * P)
    return mt, ws, wo, al, dv


def _s5_kernel(u_ref, mth_ref, mtl_ref, wsh_ref, wsl_ref, woh_ref, wol_ref, al_ref, dv_ref,
               y_ref, vr_scr, vi_scr, xr_scr, xi_scr, *, n_chunks, bsz):
    N = S5_STATE
    u = u_ref[...]
    uh, ul = _split(u)
    y = _dot(uh, mth_ref[...]) + (_dot(ul, mth_ref[...]) + _dot(uh, mtl_ref[...]))
    v = _dot(uh, wsh_ref[...]) + (_dot(ul, wsh_ref[...]) + _dot(uh, wsl_ref[...]))
    vr_scr[...] = v[:, :N]
    vi_scr[...] = v[:, N:]
    a_r = al_ref[0:1, :]
    a_i = al_ref[1:2, :]

    def step(c, x):
        xr, xi = x
        r = pl.ds(pl.multiple_of(c * bsz, bsz), bsz)
        xr_scr[r, :] = xr
        xi_scr[r, :] = xi
        return (a_r * xr - a_i * xi + vr_scr[r, :], a_r * xi + a_i * xr + vi_scr[r, :])

    zero = jnp.zeros((bsz, N), F32)
    lax.fori_loop(0, n_chunks, step, (zero, zero), unroll=8)
    xp = jnp.concatenate([xr_scr[...], xi_scr[...]], axis=1)
    y = y + _dot3(xp, woh_ref[...], wol_ref[...])
    y_ref[...] = y + dv_ref[...] * u


def _s5(ug, tables):
    nc, B, G, cw = ug.shape
    N = S5_STATE
    mt, ws, wo, al, dv = tables
    mth, mtl = _split(mt)
    wsh, wsl = _split(ws)
    woh, wol = _split(wo)
    grp = lambda a: pl.BlockSpec((None,) + a.shape[1:], lambda g: (g,) + (0,) * (a.ndim - 1))
    ugt = ug.reshape(nc * B, G, cw).transpose(1, 0, 2)
    y = pl.pallas_call(
        functools.partial(_s5_kernel, n_chunks=nc, bsz=B),
        grid=(G,),
        in_specs=[grp(ugt), grp(mth), grp(mtl), grp(wsh), grp(wsl), grp(woh), grp(wol), grp(al), grp(dv)],
        out_specs=grp(ugt),
        out_shape=jax.ShapeDtypeStruct(ugt.shape, F32),
        scratch_shapes=[pltpu.VMEM((nc * B, N), F32)] * 4,
        compiler_params=_params(("arbitrary",)),
        name="s5",
    )(ugt, mth, mtl, wsh, wsl, woh, wol, al, dv)
    return y.transpose(1, 0, 2).reshape(nc, B, G, cw)


def _compress_kernel(x_ref, pe_ref, w1_ref, b1_ref, w2_ref, b2_ref, kg_ref, o_ref, *, n_rows):
    x = x_ref[...]
    half = CMP_STRIDE * HEAD_DIM
    a = _dot((x + pe_ref[0:1, :]).astype(BF16), w1_ref[:half, :])
    b = _dot((x + pe_ref[1:2, :]).astype(BF16), w1_ref[half:, :])
    hid = jax.nn.gelu(a + pltpu.roll(b, n_rows - 1, axis=0) + b1_ref[...])
    out = _dot(hid.astype(BF16), w2_ref[...]) + b2_ref[...]
    normed = out * lax.rsqrt(jnp.mean(out * out, axis=-1, keepdims=True) + RMS_EPS) * kg_ref[...]
    is_key = pl.program_id(1) < N_KV_HEADS
    o_ref[...] = jnp.where(is_key, normed, out).astype(BF16)


def _compress(kv_c, cmp_pe, cmp_w1, cmp_b1, cmp_w2, cmp_b2, k_gain0):
    B, S, _ = kv_c.shape
    nr = S // CMP_STRIDE
    half = CMP_STRIDE * HEAD_DIM
    nj = 2 * N_KV_HEADS
    xc = kv_c.reshape(B, nr, CMP_STRIDE, nj, HEAD_DIM).transpose(0, 3, 1, 2, 4).reshape(B, nj, nr, half)
    pe = cmp_pe.reshape(2, 2, half)
    w1 = cmp_w1.astype(BF16)
    w2 = cmp_w2.astype(BF16)
    b1 = cmp_b1.reshape(2, 1, CMP_HIDDEN)
    b2 = cmp_b2.reshape(2, 1, HEAD_DIM)
    kv = lambda a: pl.BlockSpec((None,) + a.shape[1:], lambda b, j: (j // N_KV_HEADS,) + (0,) * (a.ndim - 1))
    return pl.pallas_call(
        functools.partial(_compress_kernel, n_rows=nr),
        grid=(B, nj),
        in_specs=[pl.BlockSpec((None, None, nr, half), lambda b, j: (b, j, 0, 0)),
                  kv(pe), kv(w1), kv(b1), kv(w2), kv(b2),
                  pl.BlockSpec((1, HEAD_DIM), lambda b, j: (0, 0))],
        out_specs=pl.BlockSpec((None, None, nr, HEAD_DIM), lambda b, j: (b, j, 0, 0)),
        out_shape=jax.ShapeDtypeStruct((B, nj, nr, HEAD_DIM), BF16),
        compiler_params=_params(("arbitrary", "arbitrary")),
        name="compress",
    )(xc, pe, w1, b1, w2, b2, k_gain0.reshape(1, HEAD_DIM))


def _softmax_cols(s, mask):
    s = jnp.where(mask, s, NEG_INF)
    m = jnp.max(s, axis=0, keepdims=True)
    p = jnp.where(mask, jnp.exp(s - m), 0.0)
    return p / jnp.maximum(jnp.sum(p, axis=0, keepdims=True), 1e-20)


def _attend_kernel(qt_ref, kc_ref, vct_ref, ks_ref, vst_ref, kw_ref, vwt_ref, g_ref, slope_ref, ovt_ref,
                   o_ref, qa_scr, *, n_sel, n_pick, n_cmp_rows):
    QW = GQA_GROUP * Q_BLOCK
    i = pl.program_id(2)
    t0 = i * Q_BLOCK
    slope = slope_ref[...]
    tq = t0 + (lax.broadcasted_iota(jnp.int32, (1, QW), 1) & (Q_BLOCK - 1))

    r = lax.broadcasted_iota(jnp.int32, (POS_ROWS, QW), 0)
    qa_scr[0:HEAD_DIM, :] = qt_ref[...]
    qa_scr[HEAD_DIM:HEAD_DIM + POS_ROWS, :] = jnp.where(
        r == 0, slope * POS_BASE, jnp.where(r == 1, slope, 0.0)).astype(BF16)
    qa_scr[HEAD_DIM + POS_ROWS:AUG, :] = jnp.zeros((AUG - HEAD_DIM - POS_ROWS, QW), BF16)
    qa = qa_scr[0:AUG, :]

    sc = _dot(kc_ref[...], qa)
    cpos = lax.broadcasted_iota(jnp.int32, (n_cmp_rows, QW), 0) * CMP_STRIDE + (CMP_BLOCK - 1)
    p_c = _softmax_cols(sc, cpos <= tq)
    o_c = _dot(vct_ref[...], p_c.astype(BF16))
    psum = p_c[:, 0:Q_BLOCK]
    for g in range(1, GQA_GROUP):
        psum = psum + p_c[:, g * Q_BLOCK:(g + 1) * Q_BLOCK]
    ph, pl_ = _split(psum)
    imp = _dot(ovt_ref[...], ph) + _dot(ovt_ref[...], pl_)

    jb = lax.broadcasted_iota(jnp.int32, (n_sel, Q_BLOCK), 0)
    cur = (t0 + lax.broadcasted_iota(jnp.int32, (1, Q_BLOCK), 1)) // SEL_BLOCK
    forced = (jb == 0) | (jb == cur) | (jb == cur - 1)
    imp = jnp.where(forced, imp + FORCE_BONUS, imp)
    imp = jnp.where(jb <= cur, imp, -1.0)
    bias = jnp.full((n_sel, Q_BLOCK), NEG_INF, F32)
    for _ in range(n_pick):
        m = jnp.max(imp, axis=0, keepdims=True)
        first = jnp.min(jnp.where(imp == m, jb, n_sel), axis=0, keepdims=True)
        hit = jb == first
        bias = jnp.where(hit, 0.0, bias)
        imp = jnp.where(hit, -jnp.inf, imp)
    bias = jnp.concatenate([bias] * GQA_GROUP, axis=1).astype(BF16)
    qa_scr[AUG:, :] = jnp.concatenate([bias, jnp.zeros((AUG - n_sel, QW), BF16)], axis=0)

    KC = SEL_KEY_CHUNK
    n_chunks = (t0 + Q_BLOCK + KC - 1) // KC

    def scores(j):
        k0 = pl.multiple_of(j * KC, KC)
        return _dot(ks_ref[pl.ds(k0, KC), :], qa_scr[...])

    def accumulate(carry, j, s, p_of):
        m, l, acc = carry
        k0 = pl.multiple_of(j * KC, KC)
        m_new = jnp.maximum(m, jnp.max(s, axis=0, keepdims=True))
        alpha = jnp.exp(m - m_new)
        p = p_of(jnp.exp(s - m_new))
        l = alpha * l + jnp.sum(p, axis=0, keepdims=True)
        acc = alpha * acc + _dot(vst_ref[:, pl.ds(k0, KC)], p.astype(BF16))
        return m_new, l, acc

    def past_step(j, state):
        return accumulate(state, j, scores(j), lambda p: p)

    init = (jnp.full((1, QW), NEG_INF, F32), jnp.zeros((1, QW), F32), jnp.zeros((HEAD_DIM, QW), F32))
    state = lax.fori_loop(0, n_chunks - 1, past_step, init)
    last = n_chunks - 1
    visible = (last * KC + lax.broadcasted_iota(jnp.int32, (KC, QW), 0)) <= tq
    _, l_s, acc_s = accumulate(state, last, jnp.where(visible, scores(last), NEG_INF),
                               lambda p: jnp.where(visible, p, 0.0))
    o_s = acc_s / jnp.maximum(l_s, 1e-20)

    WK = WINDOW + Q_BLOCK
    w0 = pl.multiple_of(jnp.maximum(t0 - WINDOW, 0), Q_BLOCK)
    sw = _dot(kw_ref[pl.ds(w0, WK), :], qa)
    dist = tq - (w0 + lax.broadcasted_iota(jnp.int32, (WK, QW), 0))
    in_win = (dist | (WINDOW - 1 - dist)) >= 0
    p_w = _softmax_cols(sw, in_win)
    o_w = _dot(vwt_ref[:, pl.ds(w0, WK)], p_w.astype(BF16))

    o = g_ref[0:1, :] * o_c + g_ref[1:2, :] * o_s + g_ref[2:3, :] * o_w
    pairs = []
    for g in range(0, GQA_GROUP, 2):
        sq = jnp.concatenate([o[:, g * Q_BLOCK:(g + 1) * Q_BLOCK], o[:, (g + 1) * Q_BLOCK:(g + 2) * Q_BLOCK]], axis=0)
        pairs.append(sq.T)
    o_ref[...] = jnp.concatenate(pairs, axis=1).astype(BF16)


def _attend(qt, ks, vst, kwn, vwt, gt, kvc_c):
    B, _, S, _ = ks.shape
    nqb = S // Q_BLOCK
    QW = GQA_GROUP * Q_BLOCK
    n_sel = S // SEL_BLOCK
    n_pick = min(SEL_TOPK, n_sel)
    nr = S // CMP_STRIDE
    n_cmp = (S - CMP_BLOCK) // CMP_STRIDE + 1
    assert n_sel % 16 == 0 and n_sel <= AUG

    cpos = np.arange(nr) * CMP_STRIDE + CMP_BLOCK - 1
    digits = np.zeros((nr, AUG - HEAD_DIM), np.float32)
    digits[:, 0] = cpos // POS_BASE
    digits[:, 1] = cpos % POS_BASE
    kc = kvc_c[:, :N_KV_HEADS]
    kc = jnp.concatenate([kc, jnp.broadcast_to(jnp.asarray(digits, BF16), kc.shape[:2] + digits.shape)], axis=-1)
    vct = kvc_c[:, N_KV_HEADS:].transpose(0, 1, 3, 2)
    slopes = 2.0 ** (-8.0 * np.arange(1, N_HEADS + 1) / N_HEADS)
    slope_t = jnp.asarray(np.repeat(slopes.reshape(N_KV_HEADS, GQA_GROUP), Q_BLOCK, axis=1)
                          .reshape(N_KV_HEADS, 1, QW), F32)
    cmp_start = np.arange(n_cmp) * CMP_STRIDE
    sel_start = np.arange(n_sel) * SEL_BLOCK
    ov = ((cmp_start[:, None] <= sel_start[None, :] + SEL_BLOCK - 1)
          & (cmp_start[:, None] + CMP_BLOCK - 1 >= sel_start[None, :])).astype(np.float32)
    ovt = np.zeros((n_sel, nr), np.float32)
    ovt[:, :n_cmp] = ov.T
    ovt = jnp.asarray(ovt, BF16)

    per_bh = lambda r, c: pl.BlockSpec((None, None, r, c), lambda b, h, i: (b, h, 0, 0))
    per_q = lambda r: pl.BlockSpec((None, None, r, QW), lambda b, h, i: (b, h, 0, i))
    return pl.pallas_call(
        functools.partial(_attend_kernel, n_sel=n_sel, n_pick=n_pick, n_cmp_rows=nr),
        grid=(B, N_KV_HEADS, nqb),
        in_specs=[per_q(HEAD_DIM), per_bh(nr, AUG), per_bh(HEAD_DIM, nr),
                  per_bh(S, 2 * AUG), per_bh(HEAD_DIM, S), per_bh(S, AUG), per_bh(HEAD_DIM, S),
                  per_q(3),
                  pl.BlockSpec((None, 1, QW), lambda b, h, i: (h, 0, 0)),
                  pl.BlockSpec((n_sel, nr), lambda b, h, i: (0, 0))],
        out_specs=pl.BlockSpec((None, Q_BLOCK, GQA_GROUP * HEAD_DIM), lambda b, h, i: (b, i, h)),
        out_shape=jax.ShapeDtypeStruct((B, S, NSA_WIDTH), BF16),
        scratch_shapes=[pltpu.VMEM((2 * AUG, QW), BF16)],
        compiler_params=_params(("arbitrary", "arbitrary", "arbitrary")),
        name="attend",
    )(qt, kc, vct, ks, vst, kwn, vwt, gt, slope_t, ovt)


def _merge_kernel(x_ref, mod_ref, ys_ref, on_ref, gm_ref, wglu_ref, bglu_ref, wa_ref, wb_ref, wo_ref,
                  gain_ref, wrh_ref, wrl_ref, x1_ref, h2_ref, sc_ref):
    D = D_MODEL
    tm = x_ref.shape[0]
    cols = []
    for jb in range(S5_WIDTH // 128):
        halves = [_segment_transpose(ys_ref[:, 8 * jb:8 * (jb + 1), 128 * hf:128 * (hf + 1)]) for hf in range(2)]
        cols.append(jnp.stack(halves, axis=1).reshape(tm, 128))
    z = jax.nn.gelu(jnp.concatenate(cols, axis=1))
    glu = z * jax.nn.sigmoid(_dot(z.astype(BF16), wglu_ref[...]) + bglu_ref[...])
    ya = _dot(glu.astype(BF16), wa_ref[...])
    yb = _dot(on_ref[...], wb_ref[...])
    merged = gm_ref[:, :D].astype(F32) * ya + gm_ref[:, D:].astype(F32) * yb
    x1 = x_ref[...] + mod_ref[2:3, :] * _dot(merged.astype(BF16), wo_ref[...])
    x1_ref[...] = x1
    y = x1 * lax.rsqrt(jnp.mean(x1 * x1, axis=-1, keepdims=True) + RMS_EPS)
    h2 = (y * gain_ref[...]) * (1.0 + mod_ref[4:5, :]) + mod_ref[3:4, :]
    h2_ref[...] = h2.astype(BF16)
    hh, hl = _split(h2)
    nt = (((1,), (1,)), ((), ()))
    dg = lambda a, b: lax.dot_general(a, b, nt, preferred_element_type=F32)
    logits = dg(wrh_ref[...], hh) + (dg(wrh_ref[...], hl) + dg(wrl_ref[...], hh))
    sc_ref[...] = jax.nn.sigmoid(logits)


def _merge(x, mod, ys5, o_nsa, gm, w_glu, b_glu, w_a, w_b, w_out, gain_f, w_router):
    B, S, D = x.shape
    tm = min(ROW_TILE, S)
    wrh, wrl = _split(w_router.T)
    ws = [w_glu.astype(BF16), b_glu.reshape(1, -1), w_a.astype(BF16), w_b.astype(BF16), w_out.astype(BF16),
          gain_f, wrh, wrl]
    row = lambda w: pl.BlockSpec((None, tm, w), lambda b, i: (b, i, 0))
    full = lambda a: pl.BlockSpec(a.shape, lambda b, i: (0,) * a.ndim)
    return pl.pallas_call(
        _merge_kernel,
        grid=(B, S // tm),
        in_specs=[row(D), pl.BlockSpec((None, N_MOD, D), lambda b, i: (b, 0, 0)),
                  _s5_chunk_spec(tm), row(NSA_WIDTH), row(2 * D)] + [full(a) for a in ws],
        out_specs=[row(D), row(D), pl.BlockSpec((None, N_EXPERTS, tm), lambda b, i: (b, 0, i))],
        out_shape=[jax.ShapeDtypeStruct((B, S, D), F32), jax.ShapeDtypeStruct((B, S, D), BF16),
                   jax.ShapeDtypeStruct((B, N_EXPERTS, S), F32)],
        compiler_params=_params(("arbitrary", "arbitrary")),
        name="merge",
    )(x, mod, ys5, o_nsa, gm, *ws)


def _first_argmax_rows(v, idx, n):
    m = jnp.max(v, axis=0, keepdims=True)
    first = jnp.min(jnp.where(v == m, idx, n), axis=0, keepdims=True)
    return idx == first, m


def _route_kernel(sc_ref, bias_ref, tri_ref, rank_ref, w_ref, cnt_ref):
    E, NG = N_EXPERTS, N_EXPERT_GROUPS
    GS = E // NG
    sc = sc_ref[...]
    TM = sc.shape[1]
    sel = sc + bias_ref[...]
    i8 = lax.broadcasted_iota(jnp.int32, (GS, TM), 0)
    gscore = []
    for g in range(NG):
        blk = sel[g * GS:(g + 1) * GS, :]
        hit, m1 = _first_argmax_rows(blk, i8, GS)
        m2 = jnp.max(jnp.where(hit, -jnp.inf, blk), axis=0, keepdims=True)
        gscore.append(m1 + m2)
    gscore = jnp.concatenate(gscore, axis=0)
    ig = lax.broadcasted_iota(jnp.int32, (NG, TM), 0)
    gmask = jnp.zeros((NG, TM), F32)
    for _ in range(TOPK_EXPERT_GROUPS):
        hit, _m = _first_argmax_rows(gscore, ig, NG)
        gmask = jnp.where(hit, 1.0, gmask)
        gscore = jnp.where(hit, -jnp.inf, gscore)
    emask = jnp.concatenate([jnp.broadcast_to(gmask[g:g + 1, :], (GS, TM)) for g in range(NG)], axis=0)
    cand = jnp.where(emask > 0.5, sel, NEG_INF)
    ie = lax.broadcasted_iota(jnp.int32, (E, TM), 0)
    chosen = jnp.zeros((E, TM), F32)
    for _ in range(TOP_K):
        hit, _m = _first_argmax_rows(cand, ie, E)
        chosen = jnp.where(hit, 1.0, chosen)
        cand = jnp.where(hit, -jnp.inf, cand)
    w = chosen * sc
    w_ref[...] = w / jnp.sum(w, axis=0, keepdims=True) * ROUTED_SCALE
    cb = chosen.astype(BF16)
    prefix = _dot(cb, tri_ref[...])
    rank_ref[...] = jnp.where(chosen > 0.5, prefix, RANK_NONE)
    cnt_ref[...] = _dot(cb, jnp.ones((TM, 128), BF16))


def _route(scores_t, router_bias):
    B, E, S = scores_t.shape
    TM = min(ROUTE_TILE, S)
    nj = S // TM
    ns = B * nj
    tri = jnp.asarray(np.triu(np.ones((TM, TM), np.float32), k=1), BF16)
    tile = lambda w: pl.BlockSpec((None, E, w), lambda b, j: (b * nj + j, 0, 0))
    return pl.pallas_call(
        _route_kernel,
        grid=(B, nj),
        in_specs=[pl.BlockSpec((None, E, TM), lambda b, j: (b, 0, j)),
                  pl.BlockSpec((E, 1), lambda b, j: (0, 0)),
                  pl.BlockSpec((TM, TM), lambda b, j: (0, 0))],
        out_specs=[tile(TM), tile(TM), tile(128)],
        out_shape=[jax.ShapeDtypeStruct((ns, E, TM), F32), jax.ShapeDtypeStruct((ns, E, TM), F32),
                   jax.ShapeDtypeStruct((ns, E, 128), F32)],
        compiler_params=_params(("arbitrary", "arbitrary")),
        name="route",
    )(scores_t, router_bias.reshape(E, 1), tri)


def _moe_kernel(cnt_ref, h_ref, x1_ref, mod_ref, rank_ref, w_ref, wg_ref, wu_ref, wd_ref,
                sg_ref, su_ref, sd_ref, o_ref, xc, yc, pc, wc, *, n_sub, tm, n_exp):
    to = pl.program_id(0)
    eb = pl.program_id(1)
    gate = mod_ref[5:6, :]

    @pl.when(eb == 0)
    def _shared():
        for s in range(n_sub):
            rows = pl.ds(s * tm, tm)
            hs = h_ref[rows, :]
            hid = jax.nn.silu(_dot(hs, sg_ref[...])) * _dot(hs, su_ref[...])
            o_ref[rows, :] = x1_ref[rows, :] + gate * _dot(hid.astype(BF16), sd_ref[...])

    PIECE, WIN = MOE_PIECE, MOE_WIN
    tn = (((0,), (0,)), ((), ()))
    experts = [eb * n_exp + j for j in range(n_exp)]
    cnt = [[cnt_ref[(to * n_sub + s) * N_EXPERTS + e] for e in experts] for s in range(n_sub)]
    off = []
    for s in range(n_sub):
        o = [jnp.int32(0)]
        for j in range(n_exp):
            o.append(o[-1] + ((cnt[s][j] + 15) // 16) * 16)
        off.append(o)
    n_pieces = [(off[s][n_exp] + PIECE - 1) // PIECE for s in range(n_sub)]
    fits = n_pieces[0] < MOE_ROWS // PIECE
    for s in range(1, n_sub):
        fits = jnp.logical_and(fits, n_pieces[s] < MOE_ROWS // PIECE)

    def one_hot_rows(s, base, rows_n, js):
        slot = (lax.broadcasted_iota(jnp.int32, (rows_n, tm), 0) + base).astype(F32)
        pick = jnp.zeros((rows_n, tm), F32)
        wacc = jnp.zeros((rows_n, tm), F32)
        for j in js:
            hit = (slot - off[s][j].astype(F32)) == rank_ref[s, pl.ds(experts[j], 1), :]
            pick = jnp.where(hit, 1.0, pick)
            wacc = jnp.where(hit, w_ref[s, pl.ds(experts[j], 1), :], wacc)
        return pick.astype(BF16), jnp.sum(wacc, axis=1, keepdims=True)

    def mlp(xg, j):
        hid = jax.nn.silu(_dot(xg, wg_ref[j])) * _dot(xg, wu_ref[j])
        return _dot(hid.astype(BF16), wd_ref[j])

    def aligned(q, n):
        return q * n if isinstance(q, int) else pl.multiple_of(q * n, n)

    def piece(s, q):
        r = pl.ds(aligned(q, PIECE), PIECE)
        pick, wrow = one_hot_rows(s, q * PIECE, PIECE, range(n_exp))
        xc[s, r, :] = _dot(pick, h_ref[pl.ds(s * tm, tm), :]).astype(BF16)
        pc[s, r, :] = pick
        wc[s, r, :] = wrow
        yc[s, r, :] = jnp.zeros((PIECE, D_MODEL), BF16)

    def window(j, p, last_start):
        starts = [off[s][j] + p * WIN for s in range(n_sub)]
        if last_start is not None:
            starts = [jnp.minimum(st, ls) for st, ls in zip(starts, last_start)]
        starts = [pl.multiple_of(st, 16) for st in starts]
        xg = jnp.concatenate([xc[s, pl.ds(starts[s], WIN), :] for s in range(n_sub)], axis=0)
        out = mlp(xg, j)
        for s in range(n_sub):
            r = pl.ds(starts[s], WIN)
            yc[s, r, :] = (out[s * WIN:(s + 1) * WIN] * wc[s, r, :] * gate).astype(BF16)

    def pair(s, q):
        r = pl.ds(aligned(q, 2 * PIECE), 2 * PIECE)
        o_ref[pl.ds(s * tm, tm), :] += lax.dot_general(pc[s, r, :], yc[s, r, :], tn, preferred_element_type=F32)

    straight = off[0][n_exp] + WIN <= 2 * PIECE
    for s in range(n_sub):
        straight = jnp.logical_and(straight, off[s][n_exp] + WIN <= 2 * PIECE)
        for j in range(n_exp):
            straight = jnp.logical_and(straight, cnt[s][j] <= WIN)

    @pl.when(straight)
    def _straight():
        for s in range(n_sub):
            piece(s, 0)
            piece(s, 1)
        for j in range(n_exp):
            window(j, 0, None)
        for s in range(n_sub):
            pair(s, 0)

    @pl.when(jnp.logical_and(fits, jnp.logical_not(straight)))
    def _packed():
        for s in range(n_sub):
            lax.fori_loop(0, n_pieces[s], lambda q, c, s=s: (piece(s, q), c)[1], 0)
            r = pl.ds(pl.multiple_of(n_pieces[s] * PIECE, PIECE), PIECE)
            xc[s, r, :] = jnp.zeros((PIECE, D_MODEL), BF16)
            yc[s, r, :] = jnp.zeros((PIECE, D_MODEL), BF16)
            pc[s, r, :] = jnp.zeros((PIECE, tm), BF16)
            wc[s, r, :] = jnp.zeros((PIECE, 1), F32)
        last_start = [(n_pieces[s] + 1) * PIECE - WIN for s in range(n_sub)]
        for j in range(n_exp):
            most = cnt[0][j]
            for s in range(1, n_sub):
                most = jnp.maximum(most, cnt[s][j])
            lax.fori_loop(0, (most + WIN - 1) // WIN, lambda p, c, j=j: (window(j, p, last_start), c)[1], 0)
        for s in range(n_sub):
            lax.fori_loop(0, (n_pieces[s] + 1) // 2, lambda q, c, s=s: (pair(s, q), c)[1], 0)

    @pl.when(jnp.logical_not(fits))
    def _unpacked():
        for j in range(n_exp):
            for s in range(n_sub):
                rows = pl.ds(s * tm, tm)

                def block(bi, carry, j=j, s=s, rows=rows):
                    slot0 = off[s][j] + bi * PIECE
                    pick, wrow = one_hot_rows(s, slot0, PIECE, [j])
                    out = mlp(_dot(pick, h_ref[rows, :]).astype(BF16), j)
                    ow = (out * wrow * gate).astype(BF16)
                    o_ref[rows, :] += lax.dot_general(pick, ow, tn, preferred_element_type=F32)
                    return carry

                lax.fori_loop(0, (cnt[s][j] + PIECE - 1) // PIECE, block, 0)


def _moe(h2, x1, mod, rank_t, w_t, counts, w_gate, w_up, w_down, ws_gate, ws_up, ws_down, seq):
    T, D = h2.shape
    ns, E, tm = rank_t.shape
    tmo = min(MOE_OUTER, seq)
    n_sub = tmo // tm
    n_exp = MOE_EXPERTS_PER_STEP
    per_b = seq // tmo
    ws = [w_gate.astype(BF16), w_up.astype(BF16), w_down.astype(BF16)]
    sh = [ws_gate.astype(BF16), ws_up.astype(BF16), ws_down.astype(BF16)]
    tok = lambda: pl.BlockSpec((tmo, D), lambda t, e, c: (t, 0))
    sub = lambda: pl.BlockSpec((n_sub, E, tm), lambda t, e, c: (t, 0, 0))
    exp = lambda a: pl.BlockSpec((n_exp,) + a.shape[1:], lambda t, e, c: (e, 0, 0))
    full = lambda a: pl.BlockSpec(a.shape, lambda t, e, c: (0, 0))
    grid_spec = pltpu.PrefetchScalarGridSpec(
        num_scalar_prefetch=1,
        grid=(T // tmo, E // n_exp),
        in_specs=[tok(), tok(), pl.BlockSpec((None, N_MOD, D), lambda t, e, c: (t // per_b, 0, 0)),
                  sub(), sub(), exp(ws[0]), exp(ws[1]), exp(ws[2]), full(sh[0]), full(sh[1]), full(sh[2])],
        out_specs=tok(),
        scratch_shapes=[pltpu.VMEM((n_sub, MOE_ROWS, D), BF16), pltpu.VMEM((n_sub, MOE_ROWS, D), BF16),
                        pltpu.VMEM((n_sub, MOE_ROWS, tm), BF16), pltpu.VMEM((n_sub, MOE_ROWS, 1), F32)],
    )
    return pl.pallas_call(
        functools.partial(_moe_kernel, n_sub=n_sub, tm=tm, n_exp=n_exp),
        grid_spec=grid_spec,
        out_shape=jax.ShapeDtypeStruct((T, D), F32),
        compiler_params=_params(("arbitrary", "arbitrary")),
        name="moe",
    )(counts, h2, x1, mod, rank_t, w_t, *ws, *sh)


def kernel(x, c, w_ada, b_ada, norm_mix_gain, norm_ffn_gain, w_in, s5_lambda_re, s5_lambda_im, s5_log_dt, s5_b_re, s5_b_im, s5_c_re, s5_c_im, s5_d, s5_w_glu, s5_b_glu, q_norm_gain, k_norm_gain, cmp_pe, cmp_w1, cmp_b1, cmp_w2, cmp_b2, w_branch_a, w_branch_b, w_out, w_router, router_bias, w_gate, w_up, w_down, ws_gate, ws_up, ws_down):
    B, S, D = x.shape
    for l in range(w_ada.shape[0]):
        mod = _ada(c, w_ada[l], b_ada[l]).reshape(B, N_MOD, D)
        u, kvc, qt, ks, vst, kwn, vwt, gt, gm = _inproj(x, mod, norm_mix_gain[l:l + 1], w_in[l],
                                                        q_norm_gain[l], k_norm_gain[l])
        tables = _s5_tables(s5_lambda_re[l], s5_lambda_im[l], s5_log_dt[l], s5_b_re[l], s5_b_im[l],
                            s5_c_re[l], s5_c_im[l], s5_d[l])
        ys5 = _s5(u, tables)
        kvc_c = _compress(kvc, cmp_pe[l], cmp_w1[l], cmp_b1[l], cmp_w2[l], cmp_b2[l], k_norm_gain[l, 0])
        o_nsa = _attend(qt, ks, vst, kwn, vwt, gt, kvc_c)
        x1, h2, scores_t = _merge(x, mod, ys5, o_nsa, gm, s5_w_glu[l], s5_b_glu[l], w_branch_a[l],
                                  w_branch_b[l], w_out[l], norm_ffn_gain[l:l + 1], w_router[l])
        rank_t, w_t, cnt = _route(scores_t, router_bias[l])
        counts = cnt[:, :, 0].astype(jnp.int32).reshape(-1)
        x = _moe(h2.reshape(B * S, D), x1.reshape(B * S, D), mod, rank_t, w_t, counts,
                 w_gate[l], w_up[l], w_down[l], ws_gate[l], ws_up[l], ws_down[l], S).reshape(B, S, D)
    return x
```

```python
import functools
import math

import numpy as np
import jax
import jax.numpy as jnp
from jax import lax
from jax.experimental import pallas as pl
from jax.experimental.pallas import tpu as pltpu

F32 = jnp.float32
BF16 = jnp.bfloat16

D_MODEL = 1024
S5_WIDTH = 512
S5_GROUP = 16
S5_GROUPS = S5_WIDTH // S5_GROUP
S5_STATE = 64
N_HEADS = 8
N_KV_HEADS = 2
GQA_GROUP = N_HEADS // N_KV_HEADS
HEAD_DIM = 64
NSA_WIDTH = N_HEADS * HEAD_DIM
KV_WIDTH = 2 * N_KV_HEADS * HEAD_DIM
CMP_BLOCK = 32
CMP_STRIDE = 16
CMP_HIDDEN = 256
SEL_BLOCK = 64
SEL_TOPK = 8
WINDOW = 256
Q_BLOCK = 128
FORCE_BONUS = 1e3
N_EXPERTS = 64
TOP_K = 8
N_EXPERT_GROUPS = 8
TOPK_EXPERT_GROUPS = 4
EXPERT_HIDDEN = 256
SHARED_HIDDEN = 256
ROUTED_SCALE = 2.5
RMS_EPS = 1e-6
NEG_INF = -1e30
N_MOD = 6

S5_CHUNK = 16
S5_CW = S5_CHUNK * S5_GROUP
ROW_TILE = 512
SEL_KEY_CHUNK = 512
POS_BASE = 64
POS_ROWS = 16
AUG = 128
ROUTE_TILE = 256
MOE_OUTER = 1024
MOE_EXPERTS_PER_STEP = 4
MOE_PIECE = 128
MOE_WIN = 48
MOE_ROWS = 512
RANK_NONE = -float(1 << 20)
VMEM_LIMIT = 56 * 1024 * 1024


def _dot(a, b):
    return jnp.dot(a, b, preferred_element_type=F32)


def _split(a):
    hi = a.astype(BF16)
    lo = (a - hi.astype(F32)).astype(BF16)
    return hi, lo


def _dot3(a, bh, bl):
    ah, al = _split(a)
    return _dot(ah, bh) + (_dot(al, bh) + _dot(ah, bl))


def _segment_transpose(x):
    nd = x.ndim
    i = lax.broadcasted_iota(jnp.int32, x.shape, nd - 2)
    seg = lax.broadcasted_iota(jnp.int32, x.shape, nd - 1) // S5_GROUP
    out = x
    for d in range(1, 8):
        r = pltpu.roll(pltpu.roll(x, 8 - d, axis=nd - 2), S5_GROUP * d, axis=nd - 1)
        out = jnp.where(seg == ((i + d) & 7), r, out)
    return out


def _params(sem):
    return pltpu.CompilerParams(dimension_semantics=sem, vmem_limit_bytes=VMEM_LIMIT)


def _ada_kernel(c_ref, w_ref, b_ref, o_ref):
    cs = jax.nn.silu(c_ref[...])
    wh, wl = _split(w_ref[...])
    o_ref[...] = _dot3(cs, wh, wl) + b_ref[...]


def _ada(c, w_ada, b_ada):
    B, D = c.shape
    return pl.pallas_call(
        _ada_kernel,
        grid=(N_MOD,),
        in_specs=[pl.BlockSpec((B, D), lambda j: (0, 0)),
                  pl.BlockSpec((D, D), lambda j: (0, j)),
                  pl.BlockSpec((1, D), lambda j: (0, j))],
        out_specs=pl.BlockSpec((B, D), lambda j: (0, j)),
        out_shape=jax.ShapeDtypeStruct((B, N_MOD * D), F32),
        compiler_params=_params(("arbitrary",)),
        name="ada",
    )(c, w_ada, b_ada.reshape(1, N_MOD * D))


def _head_norm(v, bd, gain):
    sq = v * v
    sh, sl = _split(sq)
    ms = _dot(sh, bd) + _dot(sl, bd)
    return v * lax.rsqrt(ms + RMS_EPS) * gain


def _inproj_kernel(x_ref, mod_ref, gain_ref, wm_ref, wt_ref, wgn_ref, wgm_ref, bd_ref, qg_ref, kg_ref,
                   u_ref, kvc_ref, qt_ref, ksa_ref, vst_ref, kwa_ref, vwt_ref, gt_ref, gm_ref):
    x = x_ref[...]
    shift = mod_ref[0:1, :]
    scale = mod_ref[1:2, :]
    y = x * lax.rsqrt(jnp.mean(x * x, axis=-1, keepdims=True) + RMS_EPS)
    h = (y * gain_ref[...]) * (1.0 + scale) + shift
    hb = h.astype(BF16)
    main = _dot(hb, wm_ref[...])
    tm = x.shape[0]
    u3 = main[:, :S5_WIDTH].reshape(tm // 8, 8, S5_WIDTH)
    for jb in range(S5_WIDTH // 128):
        t = _segment_transpose(u3[:, :, 128 * jb:128 * (jb + 1)]).reshape(tm // S5_CHUNK, 2, 8, 128)
        for hf in range(2):
            u_ref[:, 8 * jb:8 * (jb + 1), 128 * hf:128 * (hf + 1)] = t[:, hf]
    o = S5_WIDTH
    kvc_ref[...] = main[:, o:o + KV_WIDTH]
    o += KV_WIDTH
    kw = N_KV_HEADS * HEAD_DIM
    bd = bd_ref[...]
    ks = _head_norm(main[:, o:o + kw], bd[:kw, :kw], kg_ref[1:2, :]).astype(BF16)
    kwn = _head_norm(main[:, o + kw:o + 2 * kw], bd[:kw, :kw], kg_ref[2:3, :]).astype(BF16)
    pos = pl.program_id(1) * tm + lax.broadcasted_iota(jnp.int32, (tm, 1), 0)
    lane = lax.broadcasted_iota(jnp.int32, (tm, AUG), 1)
    digits = jnp.where(lane == 0, pos // POS_BASE, jnp.where(lane == 1, pos % POS_BASE, 0))
    digits = digits[:, :AUG - HEAD_DIM].astype(F32).astype(BF16)
    onehot = jnp.where(lane == pos // SEL_BLOCK, 1.0, 0.0).astype(BF16)
    for hh in range(N_KV_HEADS):
        ksa_ref[hh, :, 0:HEAD_DIM] = ks[:, hh * HEAD_DIM:(hh + 1) * HEAD_DIM]
        ksa_ref[hh, :, HEAD_DIM:AUG] = digits
        ksa_ref[hh, :, AUG:] = onehot
        kwa_ref[hh, :, 0:HEAD_DIM] = kwn[:, hh * HEAD_DIM:(hh + 1) * HEAD_DIM]
        kwa_ref[hh, :, HEAD_DIM:] = digits

    nt = (((1,), (1,)), ((), ()))
    tt = lax.dot_general(wt_ref[...], hb, nt, preferred_element_type=F32)
    qt = tt[:NSA_WIDTH]
    sq = qt * qt
    sh, sl = _split(sq)
    qn = (qt * lax.rsqrt(_dot(bd, sh) + _dot(bd, sl) + RMS_EPS) * qg_ref[...] * (HEAD_DIM ** -0.5)).astype(BF16)
    QW = GQA_GROUP * Q_BLOCK
    for qb in range(tm // Q_BLOCK):
        for hd in range(N_HEADS):
            hh, g = divmod(hd, GQA_GROUP)
            qt_ref[hh, :, qb * QW + g * Q_BLOCK:qb * QW + (g + 1) * Q_BLOCK] = (
                qn[hd * HEAD_DIM:(hd + 1) * HEAD_DIM, qb * Q_BLOCK:(qb + 1) * Q_BLOCK])
    vst_ref[...] = tt[NSA_WIDTH:NSA_WIDTH + kw].reshape(N_KV_HEADS, HEAD_DIM, tm).astype(BF16)
    vwt_ref[...] = tt[NSA_WIDTH + kw:].reshape(N_KV_HEADS, HEAD_DIM, tm).astype(BF16)
    gn = jax.nn.sigmoid(lax.dot_general(wgn_ref[...], hb, nt, preferred_element_type=F32))
    for qb in range(tm // Q_BLOCK):
        for br in range(3):
            for hd in range(N_HEADS):
                hh, g = divmod(hd, GQA_GROUP)
                r = br * N_HEADS + hd
                gt_ref[hh, br:br + 1, qb * QW + g * Q_BLOCK:qb * QW + (g + 1) * Q_BLOCK] = (
                    gn[r:r + 1, qb * Q_BLOCK:(qb + 1) * Q_BLOCK])
    gm_ref[...] = jax.nn.sigmoid(_dot(hb, wgm_ref[...])).astype(BF16)


def _s5_chunk_spec(tm):
    assert S5_CHUNK == 16 and S5_GROUP == 16, "the segment transposes assume 16 steps x 16 channels"
    return pl.BlockSpec((tm // S5_CHUNK, None, S5_GROUPS, S5_CW), lambda b, i: (i, b, 0, 0))


def _inproj(x, mod, gain, w_in, q_gain, k_gain):
    B, S, D = x.shape
    tm = min(ROW_TILE, S)
    assert S // SEL_BLOCK <= AUG and tm % Q_BLOCK == 0
    kw = N_KV_HEADS * HEAD_DIM
    cols = np.cumsum((0,) + (S5_WIDTH, NSA_WIDTH, KV_WIDTH, KV_WIDTH, KV_WIDTH, 3 * N_HEADS, 2 * D))
    c_u, c_q, c_kvc, c_kvs, c_kvw, c_gn, c_gm = cols[:7]
    sl = lambda a, n: w_in[:, a:a + n]
    wm = jnp.concatenate([sl(c_u, S5_WIDTH), sl(c_kvc, KV_WIDTH), sl(c_kvs, kw), sl(c_kvw, kw)], axis=1).astype(BF16)
    wt = jnp.concatenate([sl(c_q, NSA_WIDTH), sl(c_kvs + kw, kw), sl(c_kvw + kw, kw)], axis=1).T.astype(BF16)
    wgn = sl(c_gn, 3 * N_HEADS).T.astype(BF16)
    wgm = sl(c_gm, 2 * D).astype(BF16)
    seg = np.arange(NSA_WIDTH) // HEAD_DIM
    bd = jnp.asarray((seg[:, None] == seg[None, :]).astype(np.float32) / HEAD_DIM, BF16)
    qg = jnp.tile(q_gain, N_HEADS).reshape(NSA_WIDTH, 1)
    kg = jnp.tile(k_gain, (1, N_KV_HEADS))
    nq = tm // Q_BLOCK * GQA_GROUP * Q_BLOCK
    row = lambda w: pl.BlockSpec((None, tm, w), lambda b, i: (b, i, 0))
    full = lambda a: pl.BlockSpec(a.shape, lambda b, i: (0,) * a.ndim)
    rows4 = lambda w: pl.BlockSpec((None, N_KV_HEADS, tm, w), lambda b, i: (b, 0, i, 0))
    cols4 = lambda r, w: pl.BlockSpec((None, N_KV_HEADS, r, w), lambda b, i: (b, 0, 0, i))
    nqt = S // Q_BLOCK * GQA_GROUP * Q_BLOCK
    return pl.pallas_call(
        _inproj_kernel,
        grid=(B, S // tm),
        in_specs=[row(D), pl.BlockSpec((None, N_MOD, D), lambda b, i: (b, 0, 0)),
                  full(gain), full(wm), full(wt), full(wgn), full(wgm), full(bd), full(qg), full(kg)],
        out_specs=[_s5_chunk_spec(tm), row(KV_WIDTH), cols4(HEAD_DIM, nq), rows4(2 * AUG), cols4(HEAD_DIM, tm),
                   rows4(AUG), cols4(HEAD_DIM, tm), cols4(3, nq), row(2 * D)],
        out_shape=[jax.ShapeDtypeStruct((S // S5_CHUNK, B, S5_GROUPS, S5_CW), F32),
                   jax.ShapeDtypeStruct((B, S, KV_WIDTH), F32),
                   jax.ShapeDtypeStruct((B, N_KV_HEADS, HEAD_DIM, nqt), BF16),
                   jax.ShapeDtypeStruct((B, N_KV_HEADS, S, 2 * AUG), BF16),
                   jax.ShapeDtypeStruct((B, N_KV_HEADS, HEAD_DIM, S), BF16),
                   jax.ShapeDtypeStruct((B, N_KV_HEADS, S, AUG), BF16),
                   jax.ShapeDtypeStruct((B, N_KV_HEADS, HEAD_DIM, S), BF16),
                   jax.ShapeDtypeStruct((B, N_KV_HEADS, 3, nqt), F32),
                   jax.ShapeDtypeStruct((B, S, 2 * D), BF16)],
        compiler_params=_params(("arbitrary", "arbitrary")),
        name="inproj",
    )(x, mod, gain, wm, wt, wgn, wgm, bd, qg, kg)


def _s5_tables(lam_re, lam_im, log_dt, b_re, b_im, c_re, c_im, d_skip):
    L, P, N, G = S5_CHUNK, S5_GROUP, S5_STATE, S5_GROUPS
    hp = lax.Precision.HIGHEST
    lr, li = lam_re.astype(F32), lam_im.astype(F32)
    dt = jnp.exp(log_dt.astype(F32))[:, None]
    mag = jnp.exp(lr * dt)
    abar_re, abar_im = mag * jnp.cos(li * dt), mag * jnp.sin(li * dt)
    num_re, num_im = abar_re - 1.0, abar_im
    den = lr * lr + li * li
    coef_re = (num_re * lr + num_im * li) / den
    coef_im = (num_im * lr - num_re * li) / den
    br, bi = b_re.astype(F32), b_im.astype(F32)
    bbar_re = coef_re[..., None] * br - coef_im[..., None] * bi
    bbar_im = coef_re[..., None] * bi + coef_im[..., None] * br
    k = jnp.arange(L + 1, dtype=F32)[:, None, None]
    pmag = jnp.exp(lr * dt * k)
    pre, pim = pmag * jnp.cos(li * dt * k), pmag * jnp.sin(li * dt * k)
    cr, ci = c_re.astype(F32), c_im.astype(F32)
    ca_re = cr[None] * pre[:, :, None, :] - ci[None] * pim[:, :, None, :]
    ca_im = cr[None] * pim[:, :, None, :] + ci[None] * pre[:, :, None, :]
    kern = (jnp.einsum('kgpn,gnq->gkqp', ca_re[:L], bbar_re, precision=hp)
            - jnp.einsum('kgpn,gnq->gkqp', ca_im[:L], bbar_im, precision=hp))
    s_i = np.arange(L)[:, None]
    t_i = np.arange(L)[None, :]
    tau = np.clip(t_i - s_i, 0, L - 1)
    causal = jnp.asarray((t_i >= s_i).astype(np.float32))
    mt = kern[:, tau] * causal[None, :, :, None, None]
    mt = mt.transpose(0, 1, 3, 2, 4).reshape(G, L * P, L * P)
    rev = np.arange(L - 1, -1, -1)
    ab_re = pre[rev][..., None] * bbar_re[None] - pim[rev][..., None] * bbar_im[None]
    ab_im = pre[rev][..., None] * bbar_im[None] + pim[rev][..., None] * bbar_re[None]
    ws = jnp.concatenate([ab_re, ab_im], axis=2)
    ws = ws.transpose(1, 0, 3, 2).reshape(G, L * P, 2 * N)
    wo = jnp.concatenate([ca_re[1:], -ca_im[1:]], axis=3)
    wo = wo.transpose(1, 3, 0, 2).reshape(G, 2 * N, L * P)
    al = jnp.stack([pre[L], pim[L]], axis=1)
    dv = jnp.tile(d_skip.astype(F32), (1, L)).reshape(G, 1, L * P)
    return mt, ws, wo, al, dv


def _s5_kernel(u_ref, mth_ref, mtl_ref, wsh_ref, wsl_ref, woh_ref, wol_ref, al_ref, dv_ref,
               y_ref, vr_scr, vi_scr, xr_scr, xi_scr, *, n_chunks, bsz):
    N = S5_STATE
    u = u_ref[...]
    uh, ul = _split(u)
    y = _dot(uh, mth_ref[...]) + (_dot(ul, mth_ref[...]) + _dot(uh, mtl_ref[...]))
    v = _dot(uh, wsh_ref[...]) + (_dot(ul, wsh_ref[...]) + _dot(uh, wsl_ref[...]))
    vr_scr[...] = v[:, :N]
    vi_scr[...] = v[:, N:]
    a_r = al_ref[0:1, :]
    a_i = al_ref[1:2, :]

    def step(c, x):
        xr, xi = x
        r = pl.ds(pl.multiple_of(c * bsz, bsz), bsz)
        xr_scr[r, :] = xr
        xi_scr[r, :] = xi
        return (a_r * xr - a_i * xi + vr_scr[r, :], a_r * xi + a_i * xr + vi_scr[r, :])

    zero = jnp.zeros((bsz, N), F32)
    lax.fori_loop(0, n_chunks, step, (zero, zero), unroll=8)
    xp = jnp.concatenate([xr_scr[...], xi_scr[...]], axis=1)
    y = y + _dot3(xp, woh_ref[...], wol_ref[...])
    y_ref[...] = y + dv_ref[...] * u


def _s5(ug, tables):
    nc, B, G, cw = ug.shape
    N = S5_STATE
    mt, ws, wo, al, dv = tables
    mth, mtl = _split(mt)
    wsh, wsl = _split(ws)
    woh, wol = _split(wo)
    grp = lambda a: pl.BlockSpec((None,) + a.shape[1:], lambda g: (g,) + (0,) * (a.ndim - 1))
    ugt = ug.reshape(nc * B, G, cw).transpose(1, 0, 2)
    y = pl.pallas_call(
        functools.partial(_s5_kernel, n_chunks=nc, bsz=B),
        grid=(G,),
        in_specs=[grp(ugt), grp(mth), grp(mtl), grp(wsh), grp(wsl), grp(woh), grp(wol), grp(al), grp(dv)],
        out_specs=grp(ugt),
        out_shape=jax.ShapeDtypeStruct(ugt.shape, F32),
        scratch_shapes=[pltpu.VMEM((nc * B, N), F32)] * 4,
        compiler_params=_params(("arbitrary",)),
        name="s5",
    )(ugt, mth, mtl, wsh, wsl, woh, wol, al, dv)
    return y.transpose(1, 0, 2).reshape(nc, B, G, cw)


def _compress_kernel(x_ref, pe_ref, w1_ref, b1_ref, w2_ref, b2_ref, kg_ref, o_ref, *, n_rows):
    x = x_ref[...]
    half = CMP_STRIDE * HEAD_DIM
    a = _dot((x + pe_ref[0:1, :]).astype(BF16), w1_ref[:half, :])
    b = _dot((x + pe_ref[1:2, :]).astype(BF16), w1_ref[half:, :])
    hid = jax.nn.gelu(a + pltpu.roll(b, n_rows - 1, axis=0) + b1_ref[...])
    out = _dot(hid.astype(BF16), w2_ref[...]) + b2_ref[...]
    normed = out * lax.rsqrt(jnp.mean(out * out, axis=-1, keepdims=True) + RMS_EPS) * kg_ref[...]
    is_key = pl.program_id(1) < N_KV_HEADS
    o_ref[...] = jnp.where(is_key, normed, out).astype(BF16)


def _compress(kv_c, cmp_pe, cmp_w1, cmp_b1, cmp_w2, cmp_b2, k_gain0):
    B, S, _ = kv_c.shape
    nr = S // CMP_STRIDE
    half = CMP_STRIDE * HEAD_DIM
    nj = 2 * N_KV_HEADS
    xc = kv_c.reshape(B, nr, CMP_STRIDE, nj, HEAD_DIM).transpose(0, 3, 1, 2, 4).reshape(B, nj, nr, half)
    pe = cmp_pe.reshape(2, 2, half)
    w1 = cmp_w1.astype(BF16)
    w2 = cmp_w2.astype(BF16)
    b1 = cmp_b1.reshape(2, 1, CMP_HIDDEN)
    b2 = cmp_b2.reshape(2, 1, HEAD_DIM)
    kv = lambda a: pl.BlockSpec((None,) + a.shape[1:], lambda b, j: (j // N_KV_HEADS,) + (0,) * (a.ndim - 1))
    return pl.pallas_call(
        functools.partial(_compress_kernel, n_rows=nr),
        grid=(B, nj),
        in_specs=[pl.BlockSpec((None, None, nr, half), lambda b, j: (b, j, 0, 0)),
                  kv(pe), kv(w1), kv(b1), kv(w2), kv(b2),
                  pl.BlockSpec((1, HEAD_DIM), lambda b, j: (0, 0))],
        out_specs=pl.BlockSpec((None, None, nr, HEAD_DIM), lambda b, j: (b, j, 0, 0)),
        out_shape=jax.ShapeDtypeStruct((B, nj, nr, HEAD_DIM), BF16),
        compiler_params=_params(("arbitrary", "arbitrary")),
        name="compress",
    )(xc, pe, w1, b1, w2, b2, k_gain0.reshape(1, HEAD_DIM))


def _softmax_cols(s, mask):
    s = jnp.where(mask, s, NEG_INF)
    m = jnp.max(s, axis=0, keepdims=True)
    p = jnp.where(mask, jnp.exp(s - m), 0.0)
    return p / jnp.maximum(jnp.sum(p, axis=0, keepdims=True), 1e-20)


def _attend_kernel(qt_ref, kc_ref, vct_ref, ks_ref, vst_ref, kw_ref, vwt_ref, g_ref, slope_ref, ovt_ref,
                   o_ref, qa_scr, *, n_sel, n_pick, n_cmp_rows):
    QW = GQA_GROUP * Q_BLOCK
    i = pl.program_id(2)
    t0 = i * Q_BLOCK
    slope = slope_ref[...]
    tq = t0 + (lax.broadcasted_iota(jnp.int32, (1, QW), 1) & (Q_BLOCK - 1))

    r = lax.broadcasted_iota(jnp.int32, (POS_ROWS, QW), 0)
    qa_scr[0:HEAD_DIM, :] = qt_ref[...]
    qa_scr[HEAD_DIM:HEAD_DIM + POS_ROWS, :] = jnp.where(
        r == 0, slope * POS_BASE, jnp.where(r == 1, slope, 0.0)).astype(BF16)
    qa_scr[HEAD_DIM + POS_ROWS:AUG, :] = jnp.zeros((AUG - HEAD_DIM - POS_ROWS, QW), BF16)
    qa = qa_scr[0:AUG, :]

    sc = _dot(kc_ref[...], qa)
    cpos = lax.broadcasted_iota(jnp.int32, (n_cmp_rows, QW), 0) * CMP_STRIDE + (CMP_BLOCK - 1)
    p_c = _softmax_cols(sc, cpos <= tq)
    o_c = _dot(vct_ref[...], p_c.astype(BF16))
    psum = p_c[:, 0:Q_BLOCK]
    for g in range(1, GQA_GROUP):
        psum = psum + p_c[:, g * Q_BLOCK:(g + 1) * Q_BLOCK]
    ph, pl_ = _split(psum)
    imp = _dot(ovt_ref[...], ph) + _dot(ovt_ref[...], pl_)

    jb = lax.broadcasted_iota(jnp.int32, (n_sel, Q_BLOCK), 0)
    cur = (t0 + lax.broadcasted_iota(jnp.int32, (1, Q_BLOCK), 1)) // SEL_BLOCK
    forced = (jb == 0) | (jb == cur) | (jb == cur - 1)
    imp = jnp.where(forced, imp + FORCE_BONUS, imp)
    imp = jnp.where(jb <= cur, imp, -1.0)
    bias = jnp.full((n_sel, Q_BLOCK), NEG_INF, F32)
    for _ in range(n_pick):
        m = jnp.max(imp, axis=0, keepdims=True)
        first = jnp.min(jnp.where(imp == m, jb, n_sel), axis=0, keepdims=True)
        hit = jb == first
        bias = jnp.where(hit, 0.0, bias)
        imp = jnp.where(hit, -jnp.inf, imp)
    bias = jnp.concatenate([bias] * GQA_GROUP, axis=1).astype(BF16)
    qa_scr[AUG:, :] = jnp.concatenate([bias, jnp.zeros((AUG - n_sel, QW), BF16)], axis=0)

    KC = SEL_KEY_CHUNK
    n_chunks = (t0 + Q_BLOCK + KC - 1) // KC

    def scores(j):
        k0 = pl.multiple_of(j * KC, KC)
        return _dot(ks_ref[pl.ds(k0, KC), :], qa_scr[...])

    def accumulate(carry, j, s, p_of):
        m, l, acc = carry
        k0 = pl.multiple_of(j * KC, KC)
        m_new = jnp.maximum(m, jnp.max(s, axis=0, keepdims=True))
        alpha = jnp.exp(m - m_new)
        p = p_of(jnp.exp(s - m_new))
        l = alpha * l + jnp.sum(p, axis=0, keepdims=True)
        acc = alpha * acc + _dot(vst_ref[:, pl.ds(k0, KC)], p.astype(BF16))
        return m_new, l, acc

    def past_step(j, state):
        return accumulate(state, j, scores(j), lambda p: p)

    init = (jnp.full((1, QW), NEG_INF, F32), jnp.zeros((1, QW), F32), jnp.zeros((HEAD_DIM, QW), F32))
    state = lax.fori_loop(0, n_chunks - 1, past_step, init)
    last = n_chunks - 1
    visible = (last * KC + lax.broadcasted_iota(jnp.int32, (KC, QW), 0)) <= tq
    _, l_s, acc_s = accumulate(state, last, jnp.where(visible, scores(last), NEG_INF),
                               lambda p: jnp.where(visible, p, 0.0))
    o_s = acc_s / jnp.maximum(l_s, 1e-20)

    WK = WINDOW + Q_BLOCK
    w0 = pl.multiple_of(jnp.maximum(t0 - WINDOW, 0), Q_BLOCK)
    sw = _dot(kw_ref[pl.ds(w0, WK), :], qa)
    dist = tq - (w0 + lax.broadcasted_iota(jnp.int32, (WK, QW), 0))
    in_win = (dist | (WINDOW - 1 - dist)) >= 0
    p_w = _softmax_cols(sw, in_win)
    o_w = _dot(vwt_ref[:, pl.ds(w0, WK)], p_w.astype(BF16))

    o = g_ref[0:1, :] * o_c + g_ref[1:2, :] * o_s + g_ref[2:3, :] * o_w
    pairs = []
    for g in range(0, GQA_GROUP, 2):
        sq = jnp.concatenate([o[:, g * Q_BLOCK:(g + 1) * Q_BLOCK], o[:, (g + 1) * Q_BLOCK:(g + 2) * Q_BLOCK]], axis=0)
        pairs.append(sq.T)
    o_ref[...] = jnp.concatenate(pairs, axis=1).astype(BF16)


def _attend(qt, ks, vst, kwn, vwt, gt, kvc_c):
    B, _, S, _ = ks.shape
    nqb = S // Q_BLOCK
    QW = GQA_GROUP * Q_BLOCK
    n_sel = S // SEL_BLOCK
    n_pick = min(SEL_TOPK, n_sel)
    nr = S // CMP_STRIDE
    n_cmp = (S - CMP_BLOCK) // CMP_STRIDE + 1
    assert n_sel % 16 == 0 and n_sel <= AUG

    cpos = np.arange(nr) * CMP_STRIDE + CMP_BLOCK - 1
    digits = np.zeros((nr, AUG - HEAD_DIM), np.float32)
    digits[:, 0] = cpos // POS_BASE
    digits[:, 1] = cpos % POS_BASE
    kc = kvc_c[:, :N_KV_HEADS]
    kc = jnp.concatenate([kc, jnp.broadcast_to(jnp.asarray(digits, BF16), kc.shape[:2] + digits.shape)], axis=-1)
    vct = kvc_c[:, N_KV_HEADS:].transpose(0, 1, 3, 2)
    slopes = 2.0 ** (-8.0 * np.arange(1, N_HEADS + 1) / N_HEADS)
    slope_t = jnp.asarray(np.repeat(slopes.reshape(N_KV_HEADS, GQA_GROUP), Q_BLOCK, axis=1)
                          .reshape(N_KV_HEADS, 1, QW), F32)
    cmp_start = np.arange(n_cmp) * CMP_STRIDE
    sel_start = np.arange(n_sel) * SEL_BLOCK
    ov = ((cmp_start[:, None] <= sel_start[None, :] + SEL_BLOCK - 1)
          & (cmp_start[:, None] + CMP_BLOCK - 1 >= sel_start[None, :])).astype(np.float32)
    ovt = np.zeros((n_sel, nr), np.float32)
    ovt[:, :n_cmp] = ov.T
    ovt = jnp.asarray(ovt, BF16)

    per_bh = lambda r, c: pl.BlockSpec((None, None, r, c), lambda b, h, i: (b, h, 0, 0))
    per_q = lambda r: pl.BlockSpec((None, None, r, QW), lambda b, h, i: (b, h, 0, i))
    return pl.pallas_call(
        functools.partial(_attend_kernel, n_sel=n_sel, n_pick=n_pick, n_cmp_rows=nr),
        grid=(B, N_KV_HEADS, nqb),
        in_specs=[per_q(HEAD_DIM), per_bh(nr, AUG), per_bh(HEAD_DIM, nr),
                  per_bh(S, 2 * AUG), per_bh(HEAD_DIM, S), per_bh(S, AUG), per_bh(HEAD_DIM, S),
                  per_q(3),
                  pl.BlockSpec((None, 1, QW), lambda b, h, i: (h, 0, 0)),
                  pl.BlockSpec((n_sel, nr), lambda b, h, i: (0, 0))],
        out_specs=pl.BlockSpec((None, Q_BLOCK, GQA_GROUP * HEAD_DIM), lambda b, h, i: (b, i, h)),
        out_shape=jax.ShapeDtypeStruct((B, S, NSA_WIDTH), BF16),
        scratch_shapes=[pltpu.VMEM((2 * AUG, QW), BF16)],
        compiler_params=_params(("arbitrary", "arbitrary", "arbitrary")),
        name="attend",
    )(qt, kc, vct, ks, vst, kwn, vwt, gt, slope_t, ovt)


def _merge_kernel(x_ref, mod_ref, ys_ref, on_ref, gm_ref, wglu_ref, bglu_ref, wa_ref, wb_ref, wo_ref,
                  gain_ref, wrh_ref, wrl_ref, x1_ref, h2_ref, sc_ref):
    D = D_MODEL
    tm = x_ref.shape[0]
    cols = []
    for jb in range(S5_WIDTH // 128):
        halves = [_segment_transpose(ys_ref[:, 8 * jb:8 * (jb + 1), 128 * hf:128 * (hf + 1)]) for hf in range(2)]
        cols.append(jnp.stack(halves, axis=1).reshape(tm, 128))
    z = jax.nn.gelu(jnp.concatenate(cols, axis=1))
    glu = z * jax.nn.sigmoid(_dot(z.astype(BF16), wglu_ref[...]) + bglu_ref[...])
    ya = _dot(glu.astype(BF16), wa_ref[...])
    yb = _dot(on_ref[...], wb_ref[...])
    merged = gm_ref[:, :D].astype(F32) * ya + gm_ref[:, D:].astype(F32) * yb
    x1 = x_ref[...] + mod_ref[2:3, :] * _dot(merged.astype(BF16), wo_ref[...])
    x1_ref[...] = x1
    y = x1 * lax.rsqrt(jnp.mean(x1 * x1, axis=-1, keepdims=True) + RMS_EPS)
    h2 = (y * gain_ref[...]) * (1.0 + mod_ref[4:5, :]) + mod_ref[3:4, :]
    h2_ref[...] = h2.astype(BF16)
    hh, hl = _split(h2)
    nt = (((1,), (1,)), ((), ()))
    dg = lambda a, b: lax.dot_general(a, b, nt, preferred_element_type=F32)
    logits = dg(wrh_ref[...], hh) + (dg(wrh_ref[...], hl) + dg(wrl_ref[...], hh))
    sc_ref[...] = jax.nn.sigmoid(logits)


def _merge(x, mod, ys5, o_nsa, gm, w_glu, b_glu, w_a, w_b, w_out, gain_f, w_router):
    B, S, D = x.shape
    tm = min(ROW_TILE, S)
    wrh, wrl = _split(w_router.T)
    ws = [w_glu.astype(BF16), b_glu.reshape(1, -1), w_a.astype(BF16), w_b.astype(BF16), w_out.astype(BF16),
          gain_f, wrh, wrl]
    row = lambda w: pl.BlockSpec((None, tm, w), lambda b, i: (b, i, 0))
    full = lambda a: pl.BlockSpec(a.shape, lambda b, i: (0,) * a.ndim)
    return pl.pallas_call(
        _merge_kernel,
        grid=(B, S // tm),
        in_specs=[row(D), pl.BlockSpec((None, N_MOD, D), lambda b, i: (b, 0, 0)),
                  _s5_chunk_spec(tm), row(NSA_WIDTH), row(2 * D)] + [full(a) for a in ws],
        out_specs=[row(D), row(D), pl.BlockSpec((None, N_EXPERTS, tm), lambda b, i: (b, 0, i))],
        out_shape=[jax.ShapeDtypeStruct((B, S, D), F32), jax.ShapeDtypeStruct((B, S, D), BF16),
                   jax.ShapeDtypeStruct((B, N_EXPERTS, S), F32)],
        compiler_params=_params(("arbitrary", "arbitrary")),
        name="merge",
    )(x, mod, ys5, o_nsa, gm, *ws)


def _first_argmax_rows(v, idx, n):
    m = jnp.max(v, axis=0, keepdims=True)
    first = jnp.min(jnp.where(v == m, idx, n), axis=0, keepdims=True)
    return idx == first, m


def _route_kernel(sc_ref, bias_ref, tri_ref, rank_ref, w_ref, cnt_ref):
    E, NG = N_EXPERTS, N_EXPERT_GROUPS
    GS = E // NG
    sc = sc_ref[...]
    TM = sc.shape[1]
    sel = sc + bias_ref[...]
    i8 = lax.broadcasted_iota(jnp.int32, (GS, TM), 0)
    gscore = []
    for g in range(NG):
        blk = sel[g * GS:(g + 1) * GS, :]
        hit, m1 = _first_argmax_rows(blk, i8, GS)
        m2 = jnp.max(jnp.where(hit, -jnp.inf, blk), axis=0, keepdims=True)
        gscore.append(m1 + m2)
    gscore = jnp.concatenate(gscore, axis=0)
    ig = lax.broadcasted_iota(jnp.int32, (NG, TM), 0)
    gmask = jnp.zeros((NG, TM), F32)
    for _ in range(TOPK_EXPERT_GROUPS):
        hit, _m = _first_argmax_rows(gscore, ig, NG)
        gmask = jnp.where(hit, 1.0, gmask)
        gscore = jnp.where(hit, -jnp.inf, gscore)
    emask = jnp.concatenate([jnp.broadcast_to(gmask[g:g + 1, :], (GS, TM)) for g in range(NG)], axis=0)
    cand = jnp.where(emask > 0.5, sel, NEG_INF)
    ie = lax.broadcasted_iota(jnp.int32, (E, TM), 0)
    chosen = jnp.zeros((E, TM), F32)
    for _ in range(TOP_K):
        hit, _m = _first_argmax_rows(cand, ie, E)
        chosen = jnp.where(hit, 1.0, chosen)
        cand = jnp.where(hit, -jnp.inf, cand)
    w = chosen * sc
    w_ref[...] = w / jnp.sum(w, axis=0, keepdims=True) * ROUTED_SCALE
    cb = chosen.astype(BF16)
    prefix = _dot(cb, tri_ref[...])
    rank_ref[...] = jnp.where(chosen > 0.5, prefix, RANK_NONE)
    cnt_ref[...] = _dot(cb, jnp.ones((TM, 128), BF16))


def _route(scores_t, router_bias):
    B, E, S = scores_t.shape
    TM = min(ROUTE_TILE, S)
    nj = S // TM
    ns = B * nj
    tri = jnp.asarray(np.triu(np.ones((TM, TM), np.float32), k=1), BF16)
    tile = lambda w: pl.BlockSpec((None, E, w), lambda b, j: (b * nj + j, 0, 0))
    return pl.pallas_call(
        _route_kernel,
        grid=(B, nj),
        in_specs=[pl.BlockSpec((None, E, TM), lambda b, j: (b, 0, j)),
                  pl.BlockSpec((E, 1), lambda b, j: (0, 0)),
                  pl.BlockSpec((TM, TM), lambda b, j: (0, 0))],
        out_specs=[tile(TM), tile(TM), tile(128)],
        out_shape=[jax.ShapeDtypeStruct((ns, E, TM), F32), jax.ShapeDtypeStruct((ns, E, TM), F32),
                   jax.ShapeDtypeStruct((ns, E, 128), F32)],
        compiler_params=_params(("arbitrary", "arbitrary")),
        name="route",
    )(scores_t, router_bias.reshape(E, 1), tri)


def _moe_kernel(cnt_ref, h_ref, x1_ref, mod_ref, rank_ref, w_ref, wg_ref, wu_ref, wd_ref,
                sg_ref, su_ref, sd_ref, o_ref, xc, yc, pc, wc, *, n_sub, tm, n_exp):
    to = pl.program_id(0)
    eb = pl.program_id(1)
    gate = mod_ref[5:6, :]

    @pl.when(eb == 0)
    def _shared():
        for s in range(n_sub):
            rows = pl.ds(s * tm, tm)
            hs = h_ref[rows, :]
            hid = jax.nn.silu(_dot(hs, sg_ref[...])) * _dot(hs, su_ref[...])
            o_ref[rows, :] = x1_ref[rows, :] + gate * _dot(hid.astype(BF16), sd_ref[...])

    PIECE, WIN = MOE_PIECE, MOE_WIN
    tn = (((0,), (0,)), ((), ()))
    experts = [eb * n_exp + j for j in range(n_exp)]
    cnt = [[cnt_ref[(to * n_sub + s) * N_EXPERTS + e] for e in experts] for s in range(n_sub)]
    off = []
    for s in range(n_sub):
        o = [jnp.int32(0)]
        for j in range(n_exp):
            o.append(o[-1] + ((cnt[s][j] + 15) // 16) * 16)
        off.append(o)
    n_pieces = [(off[s][n_exp] + PIECE - 1) // PIECE for s in range(n_sub)]
    fits = n_pieces[0] < MOE_ROWS // PIECE
    for s in range(1, n_sub):
        fits = jnp.logical_and(fits, n_pieces[s] < MOE_ROWS // PIECE)

    def one_hot_rows(s, base, rows_n, js):
        slot = (lax.broadcasted_iota(jnp.int32, (rows_n, tm), 0) + base).astype(F32)
        pick = jnp.zeros((rows_n, tm), F32)
        wacc = jnp.zeros((rows_n, tm), F32)
        for j in js:
            hit = (slot - off[s][j].astype(F32)) == rank_ref[s, pl.ds(experts[j], 1), :]
            pick = jnp.where(hit, 1.0, pick)
            wacc = jnp.where(hit, w_ref[s, pl.ds(experts[j], 1), :], wacc)
        return pick.astype(BF16), jnp.sum(wacc, axis=1, keepdims=True)

    def mlp(xg, j):
        hid = jax.nn.silu(_dot(xg, wg_ref[j])) * _dot(xg, wu_ref[j])
        return _dot(hid.astype(BF16), wd_ref[j])

    def aligned(q, n):
        return q * n if isinstance(q, int) else pl.multiple_of(q * n, n)

    def piece(s, q):
        r = pl.ds(aligned(q, PIECE), PIECE)
        pick, wrow = one_hot_rows(s, q * PIECE, PIECE, range(n_exp))
        xc[s, r, :] = _dot(pick, h_ref[pl.ds(s * tm, tm), :]).astype(BF16)
        pc[s, r, :] = pick
        wc[s, r, :] = wrow
        yc[s, r, :] = jnp.zeros((PIECE, D_MODEL), BF16)

    def zero_piece(s, q):
        r = pl.ds(q * PIECE, PIECE)
        xc[s, r, :] = jnp.zeros((PIECE, D_MODEL), BF16)
        yc[s, r, :] = jnp.zeros((PIECE, D_MODEL), BF16)
        pc[s, r, :] = jnp.zeros((PIECE, tm), BF16)
        wc[s, r, :] = jnp.zeros((PIECE, 1), F32)

    def window(j, p, first):
        starts = [pl.multiple_of(jnp.minimum(off[s][j] + p * WIN, (n_pieces[s] + 1) * PIECE - WIN), 16)
                  for s in range(n_sub)]
        xg = jnp.concatenate([xc[s, pl.ds(starts[s], WIN), :] for s in range(n_sub)], axis=0)
        out = mlp(xg, j)
        for s in range(n_sub):
            r = pl.ds(starts[s], WIN)
            new = (out[s * WIN:(s + 1) * WIN] * wc[s, r, :] * gate).astype(BF16)
            if not first:
                rank = starts[s] - off[s][j] + lax.broadcasted_iota(jnp.int32, (WIN, 1), 0)
                new = jnp.where(rank < cnt[s][j], new, yc[s, r, :])
            yc[s, r, :] = new

    def scatter(s, start, n):
        r = pl.ds(start, n)
        o_ref[pl.ds(s * tm, tm), :] += lax.dot_general(pc[s, r, :], yc[s, r, :], tn, preferred_element_type=F32)

    @pl.when(fits)
    def _packed():
        for s in range(n_sub):
            piece(s, 0)
            piece(s, 1)
            zero_piece(s, 2)
        for s in range(n_sub):
            @pl.when(n_pieces[s] > 2)
            def _third(s=s):
                piece(s, 2)
                zero_piece(s, 3)
        for j in range(n_exp):
            window(j, 0, True)
        for j in range(n_exp):
            most = cnt[0][j]
            for s in range(1, n_sub):
                most = jnp.maximum(most, cnt[s][j])

            @pl.when(most > WIN)
            def _more(j=j, most=most):
                lax.fori_loop(1, (most + WIN - 1) // WIN, lambda p, c: (window(j, p, False), c)[1], 0)
        for s in range(n_sub):
            scatter(s, 0, 2 * PIECE)
        for s in range(n_sub):
            @pl.when(n_pieces[s] > 2)
            def _third_out(s=s):
                scatter(s, 2 * PIECE, PIECE)

    @pl.when(jnp.logical_not(fits))
    def _unpacked():
        for j in range(n_exp):
            for s in range(n_sub):
                rows = pl.ds(s * tm, tm)

                def block(bi, carry, j=j, s=s, rows=rows):
                    slot0 = off[s][j] + bi * PIECE
                    pick, wrow = one_hot_rows(s, slot0, PIECE, [j])
                    out = mlp(_dot(pick, h_ref[rows, :]).astype(BF16), j)
                    ow = (out * wrow * gate).astype(BF16)
                    o_ref[rows, :] += lax.dot_general(pick, ow, tn, preferred_element_type=F32)
                    return carry

                lax.fori_loop(0, (cnt[s][j] + PIECE - 1) // PIECE, block, 0)


def _moe(h2, x1, mod, rank_t, w_t, counts, w_gate, w_up, w_down, ws_gate, ws_up, ws_down, seq):
    T, D = h2.shape
    ns, E, tm = rank_t.shape
    tmo = min(MOE_OUTER, seq)
    n_sub = tmo // tm
    n_exp = MOE_EXPERTS_PER_STEP
    per_b = seq // tmo
    ws = [w_gate.astype(BF16), w_up.astype(BF16), w_down.astype(BF16)]
    sh = [ws_gate.astype(BF16), ws_up.astype(BF16), ws_down.astype(BF16)]
    tok = lambda: pl.BlockSpec((tmo, D), lambda t, e, c: (t, 0))
    sub = lambda: pl.BlockSpec((n_sub, E, tm), lambda t, e, c: (t, 0, 0))
    exp = lambda a: pl.BlockSpec((n_exp,) + a.shape[1:], lambda t, e, c: (e, 0, 0))
    full = lambda a: pl.BlockSpec(a.shape, lambda t, e, c: (0, 0))
    grid_spec = pltpu.PrefetchScalarGridSpec(
        num_scalar_prefetch=1,
        grid=(T // tmo, E // n_exp),
        in_specs=[tok(), tok(), pl.BlockSpec((None, N_MOD, D), lambda t, e, c: (t // per_b, 0, 0)),
                  sub(), sub(), exp(ws[0]), exp(ws[1]), exp(ws[2]), full(sh[0]), full(sh[1]), full(sh[2])],
        out_specs=tok(),
        scratch_shapes=[pltpu.VMEM((n_sub, MOE_ROWS, D), BF16), pltpu.VMEM((n_sub, MOE_ROWS, D), BF16),
                        pltpu.VMEM((n_sub, MOE_ROWS, tm), BF16), pltpu.VMEM((n_sub, MOE_ROWS, 1), F32)],
    )
    return pl.pallas_call(
        functools.partial(_moe_kernel, n_sub=n_sub, tm=tm, n_exp=n_exp),
        grid_spec=grid_spec,
        out_shape=jax.ShapeDtypeStruct((T, D), F32),
        compiler_params=_params(("arbitrary", "arbitrary")),
        name="moe",
    )(counts, h2, x1, mod, rank_t, w_t, *ws, *sh)


def kernel(x, c, w_ada, b_ada, norm_mix_gain, norm_ffn_gain, w_in, s5_lambda_re, s5_lambda_im, s5_log_dt, s5_b_re, s5_b_im, s5_c_re, s5_c_im, s5_d, s5_w_glu, s5_b_glu, q_norm_gain, k_norm_gain, cmp_pe, cmp_w1, cmp_b1, cmp_w2, cmp_b2, w_branch_a, w_branch_b, w_out, w_router, router_bias, w_gate, w_up, w_down, ws_gate, ws_up, ws_down):
    B, S, D = x.shape
    for l in range(w_ada.shape[0]):
        mod = _ada(c, w_ada[l], b_ada[l]).reshape(B, N_MOD, D)
        u, kvc, qt, ks, vst, kwn, vwt, gt, gm = _inproj(x, mod, norm_mix_gain[l:l + 1], w_in[l],
                                                        q_norm_gain[l], k_norm_gain[l])
        tables = _s5_tables(s5_lambda_re[l], s5_lambda_im[l], s5_log_dt[l], s5_b_re[l], s5_b_im[l],
                            s5_c_re[l], s5_c_im[l], s5_d[l])
        ys5 = _s5(u, tables)
        kvc_c = _compress(kvc, cmp_pe[l], cmp_w1[l], cmp_b1[l], cmp_w2[l], cmp_b2[l], k_norm_gain[l, 0])
        o_nsa = _attend(qt, ks, vst, kwn, vwt, gt, kvc_c)
        x1, h2, scores_t = _merge(x, mod, ys5, o_nsa, gm, s5_w_glu[l], s5_b_glu[l], w_branch_a[l],
                                  w_branch_b[l], w_out[l], norm_ffn_gain[l:l + 1], w_router[l])
        rank_t, w_t, cnt = _route(scores_t, router_bias[l])
        counts = cnt[:, :, 0].astype(jnp.int32).reshape(-1)
        x = _moe(h2.reshape(B * S, D), x1.reshape(B * S, D), mod, rank_t, w_t, counts,
                 w_gate[l], w_up[l], w_down[l], ws_gate[l], ws_up[l], ws_down[l], S).reshape(B, S, D)
    return x
```

```python
import functools
import math

import numpy as np
import jax
import jax.numpy as jnp
from jax import lax
from jax.experimental import pallas as pl
from jax.experimental.pallas import tpu as pltpu

F32 = jnp.float32
BF16 = jnp.bfloat16

D_MODEL = 1024
S5_WIDTH = 512
S5_GROUP = 16
S5_GROUPS = S5_WIDTH // S5_GROUP
S5_STATE = 64
N_HEADS = 8
N_KV_HEADS = 2
GQA_GROUP = N_HEADS // N_KV_HEADS
HEAD_DIM = 64
NSA_WIDTH = N_HEADS * HEAD_DIM
KV_WIDTH = 2 * N_KV_HEADS * HEAD_DIM
CMP_BLOCK = 32
CMP_STRIDE = 16
CMP_HIDDEN = 256
SEL_BLOCK = 64
SEL_TOPK = 8
WINDOW = 256
Q_BLOCK = 128
FORCE_BONUS = 1e3
N_EXPERTS = 64
TOP_K = 8
N_EXPERT_GROUPS = 8
TOPK_EXPERT_GROUPS = 4
EXPERT_HIDDEN = 256
SHARED_HIDDEN = 256
ROUTED_SCALE = 2.5
RMS_EPS = 1e-6
NEG_INF = -1e30
N_MOD = 6

S5_CHUNK = 16
S5_CW = S5_CHUNK * S5_GROUP
ROW_TILE = 512
SEL_KEY_CHUNK = 512
POS_BASE = 64
POS_ROWS = 16
AUG = 128
ROUTE_TILE = 256
MOE_OUTER = 1024
MOE_EXPERTS_PER_STEP = 4
MOE_PIECE = 128
MOE_WIN = 48
MOE_ROWS = 512
RANK_NONE = -float(1 << 20)
VMEM_LIMIT = 56 * 1024 * 1024


def _dot(a, b):
    return jnp.dot(a, b, preferred_element_type=F32)


def _split(a):
    hi = a.astype(BF16)
    lo = (a - hi.astype(F32)).astype(BF16)
    return hi, lo


def _dot3(a, bh, bl):
    ah, al = _split(a)
    return _dot(ah, bh) + (_dot(al, bh) + _dot(ah, bl))


def _segment_transpose(x):
    nd = x.ndim
    i = lax.broadcasted_iota(jnp.int32, x.shape, nd - 2)
    seg = lax.broadcasted_iota(jnp.int32, x.shape, nd - 1) // S5_GROUP
    out = x
    for d in range(1, 8):
        r = pltpu.roll(pltpu.roll(x, 8 - d, axis=nd - 2), S5_GROUP * d, axis=nd - 1)
        out = jnp.where(seg == ((i + d) & 7), r, out)
    return out


def _params(sem):
    return pltpu.CompilerParams(dimension_semantics=sem, vmem_limit_bytes=VMEM_LIMIT)


def _ada_kernel(c_ref, w_ref, b_ref, o_ref):
    cs = jax.nn.silu(c_ref[...])
    wh, wl = _split(w_ref[...])
    o_ref[...] = _dot3(cs, wh, wl) + b_ref[...]


def _ada(c, w_ada, b_ada):
    B, D = c.shape
    return pl.pallas_call(
        _ada_kernel,
        grid=(N_MOD,),
        in_specs=[pl.BlockSpec((B, D), lambda j: (0, 0)),
                  pl.BlockSpec((D, D), lambda j: (0, j)),
                  pl.BlockSpec((1, D), lambda j: (0, j))],
        out_specs=pl.BlockSpec((B, D), lambda j: (0, j)),
        out_shape=jax.ShapeDtypeStruct((B, N_MOD * D), F32),
        compiler_params=_params(("arbitrary",)),
        name="ada",
    )(c, w_ada, b_ada.reshape(1, N_MOD * D))


def _head_norm(v, bd, gain):
    sq = v * v
    sh, sl = _split(sq)
    ms = _dot(sh, bd) + _dot(sl, bd)
    return v * lax.rsqrt(ms + RMS_EPS) * gain


def _inproj_kernel(x_ref, mod_ref, gain_ref, wm_ref, wt_ref, wgn_ref, wgm_ref, bd_ref, qg_ref, kg_ref,
                   u_ref, kvc_ref, qt_ref, ksa_ref, vst_ref, kwa_ref, vwt_ref, gt_ref, gm_ref):
    x = x_ref[...]
    shift = mod_ref[0:1, :]
    scale = mod_ref[1:2, :]
    y = x * lax.rsqrt(jnp.mean(x * x, axis=-1, keepdims=True) + RMS_EPS)
    h = (y * gain_ref[...]) * (1.0 + scale) + shift
    hb = h.astype(BF16)
    main = _dot(hb, wm_ref[...])
    tm = x.shape[0]
    u3 = main[:, :S5_WIDTH].reshape(tm // 8, 8, S5_WIDTH)
    for jb in range(S5_WIDTH // 128):
        t = _segment_transpose(u3[:, :, 128 * jb:128 * (jb + 1)]).reshape(tm // S5_CHUNK, 2, 8, 128)
        for hf in range(2):
            u_ref[:, 8 * jb:8 * (jb + 1), 128 * hf:128 * (hf + 1)] = t[:, hf]
    o = S5_WIDTH
    kvc_ref[...] = main[:, o:o + KV_WIDTH]
    o += KV_WIDTH
    kw = N_KV_HEADS * HEAD_DIM
    bd = bd_ref[...]
    ks = _head_norm(main[:, o:o + kw], bd[:kw, :kw], kg_ref[1:2, :]).astype(BF16)
    kwn = _head_norm(main[:, o + kw:o + 2 * kw], bd[:kw, :kw], kg_ref[2:3, :]).astype(BF16)
    pos = pl.program_id(1) * tm + lax.broadcasted_iota(jnp.int32, (tm, 1), 0)
    lane = lax.broadcasted_iota(jnp.int32, (tm, AUG), 1)
    digits = jnp.where(lane == 0, pos // POS_BASE, jnp.where(lane == 1, pos % POS_BASE, 0))
    digits = digits[:, :AUG - HEAD_DIM].astype(F32).astype(BF16)
    onehot = jnp.where(lane == pos // SEL_BLOCK, 1.0, 0.0).astype(BF16)
    for hh in range(N_KV_HEADS):
        ksa_ref[hh, :, 0:HEAD_DIM] = ks[:, hh * HEAD_DIM:(hh + 1) * HEAD_DIM]
        ksa_ref[hh, :, HEAD_DIM:AUG] = digits
        ksa_ref[hh, :, AUG:] = onehot
        kwa_ref[hh, :, 0:HEAD_DIM] = kwn[:, hh * HEAD_DIM:(hh + 1) * HEAD_DIM]
        kwa_ref[hh, :, HEAD_DIM:] = digits

    nt = (((1,), (1,)), ((), ()))
    tt = lax.dot_general(wt_ref[...], hb, nt, preferred_element_type=F32)
    qt = tt[:NSA_WIDTH]
    sq = qt * qt
    sh, sl = _split(sq)
    qn = (qt * lax.rsqrt(_dot(bd, sh) + _dot(bd, sl) + RMS_EPS) * qg_ref[...] * (HEAD_DIM ** -0.5)).astype(BF16)
    QW = GQA_GROUP * Q_BLOCK
    for qb in range(tm // Q_BLOCK):
        for hd in range(N_HEADS):
            hh, g = divmod(hd, GQA_GROUP)
            qt_ref[hh, :, qb * QW + g * Q_BLOCK:qb * QW + (g + 1) * Q_BLOCK] = (
                qn[hd * HEAD_DIM:(hd + 1) * HEAD_DIM, qb * Q_BLOCK:(qb + 1) * Q_BLOCK])
    vst_ref[...] = tt[NSA_WIDTH:NSA_WIDTH + kw].reshape(N_KV_HEADS, HEAD_DIM, tm).astype(BF16)
    vwt_ref[...] = tt[NSA_WIDTH + kw:].reshape(N_KV_HEADS, HEAD_DIM, tm).astype(BF16)
    gn = jax.nn.sigmoid(lax.dot_general(wgn_ref[...], hb, nt, preferred_element_type=F32))
    for qb in range(tm // Q_BLOCK):
        for br in range(3):
            for hd in range(N_HEADS):
                hh, g = divmod(hd, GQA_GROUP)
                r = br * N_HEADS + hd
                gt_ref[hh, br:br + 1, qb * QW + g * Q_BLOCK:qb * QW + (g + 1) * Q_BLOCK] = (
                    gn[r:r + 1, qb * Q_BLOCK:(qb + 1) * Q_BLOCK])
    gm_ref[...] = jax.nn.sigmoid(_dot(hb, wgm_ref[...])).astype(BF16)


def _s5_chunk_spec(tm):
    assert S5_CHUNK == 16 and S5_GROUP == 16, "the segment transposes assume 16 steps x 16 channels"
    return pl.BlockSpec((tm // S5_CHUNK, None, S5_GROUPS, S5_CW), lambda b, i: (i, b, 0, 0))


def _inproj(x, mod, gain, w_in, q_gain, k_gain):
    B, S, D = x.shape
    tm = min(ROW_TILE, S)
    assert S // SEL_BLOCK <= AUG and tm % Q_BLOCK == 0
    kw = N_KV_HEADS * HEAD_DIM
    cols = np.cumsum((0,) + (S5_WIDTH, NSA_WIDTH, KV_WIDTH, KV_WIDTH, KV_WIDTH, 3 * N_HEADS, 2 * D))
    c_u, c_q, c_kvc, c_kvs, c_kvw, c_gn, c_gm = cols[:7]
    sl = lambda a, n: w_in[:, a:a + n]
    wm = jnp.concatenate([sl(c_u, S5_WIDTH), sl(c_kvc, KV_WIDTH), sl(c_kvs, kw), sl(c_kvw, kw)], axis=1).astype(BF16)
    wt = jnp.concatenate([sl(c_q, NSA_WIDTH), sl(c_kvs + kw, kw), sl(c_kvw + kw, kw)], axis=1).T.astype(BF16)
    wgn = sl(c_gn, 3 * N_HEADS).T.astype(BF16)
    wgm = sl(c_gm, 2 * D).astype(BF16)
    seg = np.arange(NSA_WIDTH) // HEAD_DIM
    bd = jnp.asarray((seg[:, None] == seg[None, :]).astype(np.float32) / HEAD_DIM, BF16)
    qg = jnp.tile(q_gain, N_HEADS).reshape(NSA_WIDTH, 1)
    kg = jnp.tile(k_gain, (1, N_KV_HEADS))
    nq = tm // Q_BLOCK * GQA_GROUP * Q_BLOCK
    row = lambda w: pl.BlockSpec((None, tm, w), lambda b, i: (b, i, 0))
    full = lambda a: pl.BlockSpec(a.shape, lambda b, i: (0,) * a.ndim)
    rows4 = lambda w: pl.BlockSpec((None, N_KV_HEADS, tm, w), lambda b, i: (b, 0, i, 0))
    cols4 = lambda r, w: pl.BlockSpec((None, N_KV_HEADS, r, w), lambda b, i: (b, 0, 0, i))
    nqt = S // Q_BLOCK * GQA_GROUP * Q_BLOCK
    return pl.pallas_call(
        _inproj_kernel,
        grid=(B, S // tm),
        in_specs=[row(D), pl.BlockSpec((None, N_MOD, D), lambda b, i: (b, 0, 0)),
                  full(gain), full(wm), full(wt), full(wgn), full(wgm), full(bd), full(qg), full(kg)],
        out_specs=[_s5_chunk_spec(tm), row(KV_WIDTH), cols4(HEAD_DIM, nq), rows4(2 * AUG), cols4(HEAD_DIM, tm),
                   rows4(AUG), cols4(HEAD_DIM, tm), cols4(3, nq), row(2 * D)],
        out_shape=[jax.ShapeDtypeStruct((S // S5_CHUNK, B, S5_GROUPS, S5_CW), F32),
                   jax.ShapeDtypeStruct((B, S, KV_WIDTH), F32),
                   jax.ShapeDtypeStruct((B, N_KV_HEADS, HEAD_DIM, nqt), BF16),
                   jax.ShapeDtypeStruct((B, N_KV_HEADS, S, 2 * AUG), BF16),
                   jax.ShapeDtypeStruct((B, N_KV_HEADS, HEAD_DIM, S), BF16),
                   jax.ShapeDtypeStruct((B, N_KV_HEADS, S, AUG), BF16),
                   jax.ShapeDtypeStruct((B, N_KV_HEADS, HEAD_DIM, S), BF16),
                   jax.ShapeDtypeStruct((B, N_KV_HEADS, 3, nqt), F32),
                   jax.ShapeDtypeStruct((B, S, 2 * D), BF16)],
        compiler_params=_params(("arbitrary", "arbitrary")),
        name="inproj",
    )(x, mod, gain, wm, wt, wgn, wgm, bd, qg, kg)


def _s5_tables(lam_re, lam_im, log_dt, b_re, b_im, c_re, c_im, d_skip):
    L, P, N, G = S5_CHUNK, S5_GROUP, S5_STATE, S5_GROUPS
    hp = lax.Precision.HIGHEST
    lr, li = lam_re.astype(F32), lam_im.astype(F32)
    dt = jnp.exp(log_dt.astype(F32))[:, None]
    mag = jnp.exp(lr * dt)
    abar_re, abar_im = mag * jnp.cos(li * dt), mag * jnp.sin(li * dt)
    num_re, num_im = abar_re - 1.0, abar_im
    den = lr * lr + li * li
    coef_re = (num_re * lr + num_im * li) / den
    coef_im = (num_im * lr - num_re * li) / den
    br, bi = b_re.astype(F32), b_im.astype(F32)
    bbar_re = coef_re[..., None] * br - coef_im[..., None] * bi
    bbar_im = coef_re[..., None] * bi + coef_im[..., None] * br
    k = jnp.arange(L + 1, dtype=F32)[:, None, None]
    pmag = jnp.exp(lr * dt * k)
    pre, pim = pmag * jnp.cos(li * dt * k), pmag * jnp.sin(li * dt * k)
    cr, ci = c_re.astype(F32), c_im.astype(F32)
    ca_re = cr[None] * pre[:, :, None, :] - ci[None] * pim[:, :, None, :]
    ca_im = cr[None] * pim[:, :, None, :] + ci[None] * pre[:, :, None, :]
    kern = (jnp.einsum('kgpn,gnq->gkqp', ca_re[:L], bbar_re, precision=hp)
            - jnp.einsum('kgpn,gnq->gkqp', ca_im[:L], bbar_im, precision=hp))
    s_i = np.arange(L)[:, None]
    t_i = np.arange(L)[None, :]
    tau = np.clip(t_i - s_i, 0, L - 1)
    causal = jnp.asarray((t_i >= s_i).astype(np.float32))
    mt = kern[:, tau] * causal[None, :, :, None, None]
    mt = mt.transpose(0, 1, 3, 2, 4).reshape(G, L * P, L * P)
    rev = np.arange(L - 1, -1, -1)
    ab_re = pre[rev][..., None] * bbar_re[None] - pim[rev][..., None] * bbar_im[None]
    ab_im = pre[rev][..., None] * bbar_im[None] + pim[rev][..., None] * bbar_re[None]
    ws = jnp.concatenate([ab_re, ab_im], axis=2)
    ws = ws.transpose(1, 0, 3, 2).reshape(G, L * P, 2 * N)
    wo = jnp.concatenate([ca_re[1:], -ca_im[1:]], axis=3)
    wo = wo.transpose(1, 3, 0, 2).reshape(G, 2 * N, L * P)
    al = jnp.stack([pre[L], pim[L]], axis=1)
    dv = jnp.tile(d_skip.astype(F32), (1, L)).reshape(G, 1, L * P)
    return mt, ws, wo, al, dv


def _s5_kernel(u_ref, mth_ref, mtl_ref, wsh_ref, wsl_ref, woh_ref, wol_ref, al_ref, dv_ref,
               y_ref, vr_scr, vi_scr, xr_scr, xi_scr, *, n_chunks, bsz):
    N = S5_STATE
    u = u_ref[...]
    uh, ul = _split(u)
    y = _dot(uh, mth_ref[...]) + (_dot(ul, mth_ref[...]) + _dot(uh, mtl_ref[...]))
    v = _dot(uh, wsh_ref[...]) + (_dot(ul, wsh_ref[...]) + _dot(uh, wsl_ref[...]))
    vr_scr[...] = v[:, :N]
    vi_scr[...] = v[:, N:]
    a_r = al_ref[0:1, :]
    a_i = al_ref[1:2, :]

    def step(c, x):
        xr, xi = x
        r = pl.ds(pl.multiple_of(c * bsz, bsz), bsz)
        xr_scr[r, :] = xr
        xi_scr[r, :] = xi
        return (a_r * xr - a_i * xi + vr_scr[r, :], a_r * xi + a_i * xr + vi_scr[r, :])

    zero = jnp.zeros((bsz, N), F32)
    lax.fori_loop(0, n_chunks, step, (zero, zero), unroll=8)
    xp = jnp.concatenate([xr_scr[...], xi_scr[...]], axis=1)
    y = y + _dot3(xp, woh_ref[...], wol_ref[...])
    y_ref[...] = y + dv_ref[...] * u


def _s5(ug, tables):
    nc, B, G, cw = ug.shape
    N = S5_STATE
    mt, ws, wo, al, dv = tables
    mth, mtl = _split(mt)
    wsh, wsl = _split(ws)
    woh, wol = _split(wo)
    grp = lambda a: pl.BlockSpec((None,) + a.shape[1:], lambda g: (g,) + (0,) * (a.ndim - 1))
    ugt = ug.reshape(nc * B, G, cw).transpose(1, 0, 2)
    y = pl.pallas_call(
        functools.partial(_s5_kernel, n_chunks=nc, bsz=B),
        grid=(G,),
        in_specs=[grp(ugt), grp(mth), grp(mtl), grp(wsh), grp(wsl), grp(woh), grp(wol), grp(al), grp(dv)],
        out_specs=grp(ugt),
        out_shape=jax.ShapeDtypeStruct(ugt.shape, F32),
        scratch_shapes=[pltpu.VMEM((nc * B, N), F32)] * 4,
        compiler_params=_params(("arbitrary",)),
        name="s5",
    )(ugt, mth, mtl, wsh, wsl, woh, wol, al, dv)
    return y.transpose(1, 0, 2).reshape(nc, B, G, cw)


def _compress_kernel(x_ref, pe_ref, w1_ref, b1_ref, w2_ref, b2_ref, kg_ref, o_ref, *, n_rows):
    x = x_ref[...]
    half = CMP_STRIDE * HEAD_DIM
    a = _dot((x + pe_ref[0:1, :]).astype(BF16), w1_ref[:half, :])
    b = _dot((x + pe_ref[1:2, :]).astype(BF16), w1_ref[half:, :])
    hid = jax.nn.gelu(a + pltpu.roll(b, n_rows - 1, axis=0) + b1_ref[...])
    out = _dot(hid.astype(BF16), w2_ref[...]) + b2_ref[...]
    normed = out * lax.rsqrt(jnp.mean(out * out, axis=-1, keepdims=True) + RMS_EPS) * kg_ref[...]
    is_key = pl.program_id(1) < N_KV_HEADS
    o_ref[...] = jnp.where(is_key, normed, out).astype(BF16)


def _compress(kv_c, cmp_pe, cmp_w1, cmp_b1, cmp_w2, cmp_b2, k_gain0):
    B, S, _ = kv_c.shape
    nr = S // CMP_STRIDE
    half = CMP_STRIDE * HEAD_DIM
    nj = 2 * N_KV_HEADS
    xc = kv_c.reshape(B, nr, CMP_STRIDE, nj, HEAD_DIM).transpose(0, 3, 1, 2, 4).reshape(B, nj, nr, half)
    pe = cmp_pe.reshape(2, 2, half)
    w1 = cmp_w1.astype(BF16)
    w2 = cmp_w2.astype(BF16)
    b1 = cmp_b1.reshape(2, 1, CMP_HIDDEN)
    b2 = cmp_b2.reshape(2, 1, HEAD_DIM)
    kv = lambda a: pl.BlockSpec((None,) + a.shape[1:], lambda b, j: (j // N_KV_HEADS,) + (0,) * (a.ndim - 1))
    return pl.pallas_call(
        functools.partial(_compress_kernel, n_rows=nr),
        grid=(B, nj),
        in_specs=[pl.BlockSpec((None, None, nr, half), lambda b, j: (b, j, 0, 0)),
                  kv(pe), kv(w1), kv(b1), kv(w2), kv(b2),
                  pl.BlockSpec((1, HEAD_DIM), lambda b, j: (0, 0))],
        out_specs=pl.BlockSpec((None, None, nr, HEAD_DIM), lambda b, j: (b, j, 0, 0)),
        out_shape=jax.ShapeDtypeStruct((B, nj, nr, HEAD_DIM), BF16),
        compiler_params=_params(("arbitrary", "arbitrary")),
        name="compress",
    )(xc, pe, w1, b1, w2, b2, k_gain0.reshape(1, HEAD_DIM))


def _softmax_cols(s, mask):
    s = jnp.where(mask, s, NEG_INF)
    m = jnp.max(s, axis=0, keepdims=True)
    p = jnp.where(mask, jnp.exp(s - m), 0.0)
    return p / jnp.maximum(jnp.sum(p, axis=0, keepdims=True), 1e-20)


def _attend_kernel(qt_ref, kc_ref, vct_ref, ks_ref, vst_ref, kw_ref, vwt_ref, g_ref, slope_ref, ovt_ref,
                   o_ref, qa_scr, sa_scr, sb_scr, *, n_sel, n_pick, n_cmp_rows):
    QW = GQA_GROUP * Q_BLOCK
    i = pl.program_id(2)
    t0 = i * Q_BLOCK
    slope = slope_ref[...]
    tq = t0 + (lax.broadcasted_iota(jnp.int32, (1, QW), 1) & (Q_BLOCK - 1))

    r = lax.broadcasted_iota(jnp.int32, (POS_ROWS, QW), 0)
    qa_scr[0:HEAD_DIM, :] = qt_ref[...]
    qa_scr[HEAD_DIM:HEAD_DIM + POS_ROWS, :] = jnp.where(
        r == 0, slope * POS_BASE, jnp.where(r == 1, slope, 0.0)).astype(BF16)
    qa_scr[HEAD_DIM + POS_ROWS:AUG, :] = jnp.zeros((AUG - HEAD_DIM - POS_ROWS, QW), BF16)
    qa = qa_scr[0:AUG, :]

    sc = _dot(kc_ref[...], qa)
    cpos = lax.broadcasted_iota(jnp.int32, (n_cmp_rows, QW), 0) * CMP_STRIDE + (CMP_BLOCK - 1)
    p_c = _softmax_cols(sc, cpos <= tq)
    o_c = _dot(vct_ref[...], p_c.astype(BF16))
    psum = p_c[:, 0:Q_BLOCK]
    for g in range(1, GQA_GROUP):
        psum = psum + p_c[:, g * Q_BLOCK:(g + 1) * Q_BLOCK]
    ph, pl_ = _split(psum)
    imp = _dot(ovt_ref[...], ph) + _dot(ovt_ref[...], pl_)

    jb = lax.broadcasted_iota(jnp.int32, (n_sel, Q_BLOCK), 0)
    cur = (t0 + lax.broadcasted_iota(jnp.int32, (1, Q_BLOCK), 1)) // SEL_BLOCK
    forced = (jb == 0) | (jb == cur) | (jb == cur - 1)
    imp = jnp.where(forced, imp + FORCE_BONUS, imp)
    imp = jnp.where(jb <= cur, imp, -1.0)
    bias = jnp.full((n_sel, Q_BLOCK), NEG_INF, F32)
    for _ in range(n_pick):
        m = jnp.max(imp, axis=0, keepdims=True)
        first = jnp.min(jnp.where(imp == m, jb, n_sel), axis=0, keepdims=True)
        hit = jb == first
        bias = jnp.where(hit, 0.0, bias)
        imp = jnp.where(hit, -jnp.inf, imp)
    bias = jnp.concatenate([bias] * GQA_GROUP, axis=1).astype(BF16)
    qa_scr[AUG:, :] = jnp.concatenate([bias, jnp.zeros((AUG - n_sel, QW), BF16)], axis=0)

    KC = SEL_KEY_CHUNK
    n_chunks = (t0 + Q_BLOCK + KC - 1) // KC

    def scores_to(ref, j):
        k0 = pl.multiple_of(j * KC, KC)
        ref[...] = _dot(ks_ref[pl.ds(k0, KC), :], qa_scr[...])

    def accumulate(carry, j, s, p_of):
        m, l, acc = carry
        k0 = pl.multiple_of(j * KC, KC)
        m_new = jnp.maximum(m, jnp.max(s, axis=0, keepdims=True))
        alpha = jnp.exp(m - m_new)
        p = p_of(jnp.exp(s - m_new))
        l = alpha * l + jnp.sum(p, axis=0, keepdims=True)
        acc = alpha * acc + _dot(vst_ref[:, pl.ds(k0, KC)], p.astype(BF16))
        return m_new, l, acc

    keep = lambda p: p
    last = n_chunks - 1
    scores_to(sa_scr, 0)

    def two_chunks(i, state):
        j = 2 * i
        scores_to(sb_scr, j + 1)
        state = accumulate(state, j, sa_scr[...], keep)
        scores_to(sa_scr, j + 2)
        return accumulate(state, j + 1, sb_scr[...], keep)

    def odd_chunk(state):
        state = accumulate(state, last - 1, sa_scr[...], keep)
        scores_to(sa_scr, last)
        return state

    init = (jnp.full((1, QW), NEG_INF, F32), jnp.zeros((1, QW), F32), jnp.zeros((HEAD_DIM, QW), F32))
    state = lax.fori_loop(0, last // 2, two_chunks, init)
    state = lax.cond(last % 2 == 1, odd_chunk, lambda st: st, state)
    visible = (last * KC + lax.broadcasted_iota(jnp.int32, (KC, QW), 0)) <= tq
    _, l_s, acc_s = accumulate(state, last, jnp.where(visible, sa_scr[...], NEG_INF),
                               lambda p: jnp.where(visible, p, 0.0))
    o_s = acc_s / jnp.maximum(l_s, 1e-20)

    WK = WINDOW + Q_BLOCK
    w0 = pl.multiple_of(jnp.maximum(t0 - WINDOW, 0), Q_BLOCK)
    sw = _dot(kw_ref[pl.ds(w0, WK), :], qa)
    dist = tq - (w0 + lax.broadcasted_iota(jnp.int32, (WK, QW), 0))
    in_win = (dist | (WINDOW - 1 - dist)) >= 0
    p_w = _softmax_cols(sw, in_win)
    o_w = _dot(vwt_ref[:, pl.ds(w0, WK)], p_w.astype(BF16))

    o = g_ref[0:1, :] * o_c + g_ref[1:2, :] * o_s + g_ref[2:3, :] * o_w
    pairs = []
    for g in range(0, GQA_GROUP, 2):
        sq = jnp.concatenate([o[:, g * Q_BLOCK:(g + 1) * Q_BLOCK], o[:, (g + 1) * Q_BLOCK:(g + 2) * Q_BLOCK]], axis=0)
        pairs.append(sq.T)
    o_ref[...] = jnp.concatenate(pairs, axis=1).astype(BF16)


def _attend(qt, ks, vst, kwn, vwt, gt, kvc_c):
    B, _, S, _ = ks.shape
    nqb = S // Q_BLOCK
    QW = GQA_GROUP * Q_BLOCK
    n_sel = S // SEL_BLOCK
    n_pick = min(SEL_TOPK, n_sel)
    nr = S // CMP_STRIDE
    n_cmp = (S - CMP_BLOCK) // CMP_STRIDE + 1
    assert n_sel % 16 == 0 and n_sel <= AUG

    cpos = np.arange(nr) * CMP_STRIDE + CMP_BLOCK - 1
    digits = np.zeros((nr, AUG - HEAD_DIM), np.float32)
    digits[:, 0] = cpos // POS_BASE
    digits[:, 1] = cpos % POS_BASE
    kc = kvc_c[:, :N_KV_HEADS]
    kc = jnp.concatenate([kc, jnp.broadcast_to(jnp.asarray(digits, BF16), kc.shape[:2] + digits.shape)], axis=-1)
    vct = kvc_c[:, N_KV_HEADS:].transpose(0, 1, 3, 2)
    slopes = 2.0 ** (-8.0 * np.arange(1, N_HEADS + 1) / N_HEADS)
    slope_t = jnp.asarray(np.repeat(slopes.reshape(N_KV_HEADS, GQA_GROUP), Q_BLOCK, axis=1)
                          .reshape(N_KV_HEADS, 1, QW), F32)
    cmp_start = np.arange(n_cmp) * CMP_STRIDE
    sel_start = np.arange(n_sel) * SEL_BLOCK
    ov = ((cmp_start[:, None] <= sel_start[None, :] + SEL_BLOCK - 1)
          & (cmp_start[:, None] + CMP_BLOCK - 1 >= sel_start[None, :])).astype(np.float32)
    ovt = np.zeros((n_sel, nr), np.float32)
    ovt[:, :n_cmp] = ov.T
    ovt = jnp.asarray(ovt, BF16)

    per_bh = lambda r, c: pl.BlockSpec((None, None, r, c), lambda b, h, i: (b, h, 0, 0))
    per_q = lambda r: pl.BlockSpec((None, None, r, QW), lambda b, h, i: (b, h, 0, i))
    return pl.pallas_call(
        functools.partial(_attend_kernel, n_sel=n_sel, n_pick=n_pick, n_cmp_rows=nr),
        grid=(B, N_KV_HEADS, nqb),
        in_specs=[per_q(HEAD_DIM), per_bh(nr, AUG), per_bh(HEAD_DIM, nr),
                  per_bh(S, 2 * AUG), per_bh(HEAD_DIM, S), per_bh(S, AUG), per_bh(HEAD_DIM, S),
                  per_q(3),
                  pl.BlockSpec((None, 1, QW), lambda b, h, i: (h, 0, 0)),
                  pl.BlockSpec((n_sel, nr), lambda b, h, i: (0, 0))],
        out_specs=pl.BlockSpec((None, Q_BLOCK, GQA_GROUP * HEAD_DIM), lambda b, h, i: (b, i, h)),
        out_shape=jax.ShapeDtypeStruct((B, S, NSA_WIDTH), BF16),
        scratch_shapes=[pltpu.VMEM((2 * AUG, QW), BF16), pltpu.VMEM((SEL_KEY_CHUNK, QW), F32),
                        pltpu.VMEM((SEL_KEY_CHUNK, QW), F32)],
        compiler_params=_params(("arbitrary", "arbitrary", "arbitrary")),
        name="attend",
    )(qt, kc, vct, ks, vst, kwn, vwt, gt, slope_t, ovt)


def _merge_kernel(x_ref, mod_ref, ys_ref, on_ref, gm_ref, wglu_ref, bglu_ref, wa_ref, wb_ref, wo_ref,
                  gain_ref, wrh_ref, wrl_ref, x1_ref, h2_ref, sc_ref):
    D = D_MODEL
    tm = x_ref.shape[0]
    cols = []
    for jb in range(S5_WIDTH // 128):
        halves = [_segment_transpose(ys_ref[:, 8 * jb:8 * (jb + 1), 128 * hf:128 * (hf + 1)]) for hf in range(2)]
        cols.append(jnp.stack(halves, axis=1).reshape(tm, 128))
    z = jax.nn.gelu(jnp.concatenate(cols, axis=1))
    glu = z * jax.nn.sigmoid(_dot(z.astype(BF16), wglu_ref[...]) + bglu_ref[...])
    ya = _dot(glu.astype(BF16), wa_ref[...])
    yb = _dot(on_ref[...], wb_ref[...])
    merged = gm_ref[:, :D].astype(F32) * ya + gm_ref[:, D:].astype(F32) * yb
    x1 = x_ref[...] + mod_ref[2:3, :] * _dot(merged.astype(BF16), wo_ref[...])
    x1_ref[...] = x1
    y = x1 * lax.rsqrt(jnp.mean(x1 * x1, axis=-1, keepdims=True) + RMS_EPS)
    h2 = (y * gain_ref[...]) * (1.0 + mod_ref[4:5, :]) + mod_ref[3:4, :]
    h2_ref[...] = h2.astype(BF16)
    hh, hl = _split(h2)
    nt = (((1,), (1,)), ((), ()))
    dg = lambda a, b: lax.dot_general(a, b, nt, preferred_element_type=F32)
    logits = dg(wrh_ref[...], hh) + (dg(wrh_ref[...], hl) + dg(wrl_ref[...], hh))
    sc_ref[...] = jax.nn.sigmoid(logits)


def _merge(x, mod, ys5, o_nsa, gm, w_glu, b_glu, w_a, w_b, w_out, gain_f, w_router):
    B, S, D = x.shape
    tm = min(ROW_TILE, S)
    wrh, wrl = _split(w_router.T)
    ws = [w_glu.astype(BF16), b_glu.reshape(1, -1), w_a.astype(BF16), w_b.astype(BF16), w_out.astype(BF16),
          gain_f, wrh, wrl]
    row = lambda w: pl.BlockSpec((None, tm, w), lambda b, i: (b, i, 0))
    full = lambda a: pl.BlockSpec(a.shape, lambda b, i: (0,) * a.ndim)
    return pl.pallas_call(
        _merge_kernel,
        grid=(B, S // tm),
        in_specs=[row(D), pl.BlockSpec((None, N_MOD, D), lambda b, i: (b, 0, 0)),
                  _s5_chunk_spec(tm), row(NSA_WIDTH), row(2 * D)] + [full(a) for a in ws],
        out_specs=[row(D), row(D), pl.BlockSpec((None, N_EXPERTS, tm), lambda b, i: (b, 0, i))],
        out_shape=[jax.ShapeDtypeStruct((B, S, D), F32), jax.ShapeDtypeStruct((B, S, D), BF16),
                   jax.ShapeDtypeStruct((B, N_EXPERTS, S), F32)],
        compiler_params=_params(("arbitrary", "arbitrary")),
        name="merge",
    )(x, mod, ys5, o_nsa, gm, *ws)


def _first_argmax_rows(v, idx, n):
    m = jnp.max(v, axis=0, keepdims=True)
    first = jnp.min(jnp.where(v == m, idx, n), axis=0, keepdims=True)
    return idx == first, m


def _route_kernel(sc_ref, bias_ref, tri_ref, rank_ref, w_ref, cnt_ref):
    E, NG = N_EXPERTS, N_EXPERT_GROUPS
    GS = E // NG
    sc = sc_ref[...]
    TM = sc.shape[1]
    sel = sc + bias_ref[...]
    i8 = lax.broadcasted_iota(jnp.int32, (GS, TM), 0)
    gscore = []
    for g in range(NG):
        blk = sel[g * GS:(g + 1) * GS, :]
        hit, m1 = _first_argmax_rows(blk, i8, GS)
        m2 = jnp.max(jnp.where(hit, -jnp.inf, blk), axis=0, keepdims=True)
        gscore.append(m1 + m2)
    gscore = jnp.concatenate(gscore, axis=0)
    ig = lax.broadcasted_iota(jnp.int32, (NG, TM), 0)
    gmask = jnp.zeros((NG, TM), F32)
    for _ in range(TOPK_EXPERT_GROUPS):
        hit, _m = _first_argmax_rows(gscore, ig, NG)
        gmask = jnp.where(hit, 1.0, gmask)
        gscore = jnp.where(hit, -jnp.inf, gscore)
    emask = jnp.concatenate([jnp.broadcast_to(gmask[g:g + 1, :], (GS, TM)) for g in range(NG)], axis=0)
    cand = jnp.where(emask > 0.5, sel, NEG_INF)
    ie = lax.broadcasted_iota(jnp.int32, (E, TM), 0)
    chosen = jnp.zeros((E, TM), F32)
    for _ in range(TOP_K):
        hit, _m = _first_argmax_rows(cand, ie, E)
        chosen = jnp.where(hit, 1.0, chosen)
        cand = jnp.where(hit, -jnp.inf, cand)
    w = chosen * sc
    w_ref[...] = w / jnp.sum(w, axis=0, keepdims=True) * ROUTED_SCALE
    cb = chosen.astype(BF16)
    prefix = _dot(cb, tri_ref[...])
    rank_ref[...] = jnp.where(chosen > 0.5, prefix, RANK_NONE)
    cnt_ref[...] = _dot(cb, jnp.ones((TM, 128), BF16))


def _route(scores_t, router_bias):
    B, E, S = scores_t.shape
    TM = min(ROUTE_TILE, S)
    nj = S // TM
    ns = B * nj
    tri = jnp.asarray(np.triu(np.ones((TM, TM), np.float32), k=1), BF16)
    tile = lambda w: pl.BlockSpec((None, E, w), lambda b, j: (b * nj + j, 0, 0))
    return pl.pallas_call(
        _route_kernel,
        grid=(B, nj),
        in_specs=[pl.BlockSpec((None, E, TM), lambda b, j: (b, 0, j)),
                  pl.BlockSpec((E, 1), lambda b, j: (0, 0)),
                  pl.BlockSpec((TM, TM), lambda b, j: (0, 0))],
        out_specs=[tile(TM), tile(TM), tile(128)],
        out_shape=[jax.ShapeDtypeStruct((ns, E, TM), F32), jax.ShapeDtypeStruct((ns, E, TM), F32),
                   jax.ShapeDtypeStruct((ns, E, 128), F32)],
        compiler_params=_params(("arbitrary", "arbitrary")),
        name="route",
    )(scores_t, router_bias.reshape(E, 1), tri)


def _moe_kernel(cnt_ref, h_ref, x1_ref, mod_ref, rank_ref, w_ref, wg_ref, wu_ref, wd_ref,
                sg_ref, su_ref, sd_ref, o_ref, xc, yc, pc, wc, *, n_sub, tm, n_exp):
    to = pl.program_id(0)
    eb = pl.program_id(1)
    gate = mod_ref[5:6, :]

    @pl.when(eb == 0)
    def _shared():
        for s in range(n_sub):
            rows = pl.ds(s * tm, tm)
            hs = h_ref[rows, :]
            hid = jax.nn.silu(_dot(hs, sg_ref[...])) * _dot(hs, su_ref[...])
            o_ref[rows, :] = x1_ref[rows, :] + gate * _dot(hid.astype(BF16), sd_ref[...])

    PIECE, WIN = MOE_PIECE, MOE_WIN
    tn = (((0,), (0,)), ((), ()))
    experts = [eb * n_exp + j for j in range(n_exp)]
    cnt = [[cnt_ref[(to * n_sub + s) * N_EXPERTS + e] for e in experts] for s in range(n_sub)]
    off = []
    for s in range(n_sub):
        o = [jnp.int32(0)]
        for j in range(n_exp):
            o.append(o[-1] + ((cnt[s][j] + 15) // 16) * 16)
        off.append(o)
    n_pieces = [(off[s][n_exp] + PIECE - 1) // PIECE for s in range(n_sub)]
    fits = n_pieces[0] < MOE_ROWS // PIECE
    for s in range(1, n_sub):
        fits = jnp.logical_and(fits, n_pieces[s] < MOE_ROWS // PIECE)

    def one_hot_rows(s, base, rows_n, js):
        slot = (lax.broadcasted_iota(jnp.int32, (rows_n, tm), 0) + base).astype(F32)
        pick = jnp.zeros((rows_n, tm), F32)
        wacc = jnp.zeros((rows_n, tm), F32)
        for j in js:
            hit = (slot - off[s][j].astype(F32)) == rank_ref[s, pl.ds(experts[j], 1), :]
            pick = jnp.where(hit, 1.0, pick)
            wacc = jnp.where(hit, w_ref[s, pl.ds(experts[j], 1), :], wacc)
        return pick.astype(BF16), jnp.sum(wacc, axis=1, keepdims=True)

    def mlp(xg, j):
        hid = jax.nn.silu(_dot(xg, wg_ref[j])) * _dot(xg, wu_ref[j])
        return _dot(hid.astype(BF16), wd_ref[j])

    def aligned(q, n):
        return q * n if isinstance(q, int) else pl.multiple_of(q * n, n)

    def piece(s, q):
        r = pl.ds(aligned(q, PIECE), PIECE)
        pick, wrow = one_hot_rows(s, q * PIECE, PIECE, range(n_exp))
        xc[s, r, :] = _dot(pick, h_ref[pl.ds(s * tm, tm), :]).astype(BF16)
        pc[s, r, :] = pick
        wc[s, r, :] = wrow
        yc[s, r, :] = jnp.zeros((PIECE, D_MODEL), BF16)

    def zero_piece(s, q):
        r = pl.ds(q * PIECE, PIECE)
        xc[s, r, :] = jnp.zeros((PIECE, D_MODEL), BF16)
        yc[s, r, :] = jnp.zeros((PIECE, D_MODEL), BF16)
        pc[s, r, :] = jnp.zeros((PIECE, tm), BF16)
        wc[s, r, :] = jnp.zeros((PIECE, 1), F32)

    def window(j, p, first):
        starts = [pl.multiple_of(jnp.minimum(off[s][j] + p * WIN, (n_pieces[s] + 1) * PIECE - WIN), 16)
                  for s in range(n_sub)]
        xg = jnp.concatenate([xc[s, pl.ds(starts[s], WIN), :] for s in range(n_sub)], axis=0)
        out = mlp(xg, j)
        for s in range(n_sub):
            r = pl.ds(starts[s], WIN)
            new = (out[s * WIN:(s + 1) * WIN] * wc[s, r, :] * gate).astype(BF16)
            if not first:
                rank = starts[s] - off[s][j] + lax.broadcasted_iota(jnp.int32, (WIN, 1), 0)
                new = jnp.where(rank < cnt[s][j], new, yc[s, r, :])
            yc[s, r, :] = new

    def scatter(s, start, n):
        r = pl.ds(start, n)
        o_ref[pl.ds(s * tm, tm), :] += lax.dot_general(pc[s, r, :], yc[s, r, :], tn, preferred_element_type=F32)

    @pl.when(fits)
    def _packed():
        for s in range(n_sub):
            piece(s, 0)
            piece(s, 1)
            zero_piece(s, 2)
        for s in range(n_sub):
            @pl.when(n_pieces[s] > 2)
            def _third(s=s):
                piece(s, 2)
                zero_piece(s, 3)
        for j in range(n_exp):
            window(j, 0, True)
        for j in range(n_exp):
            most = cnt[0][j]
            for s in range(1, n_sub):
                most = jnp.maximum(most, cnt[s][j])

            @pl.when(most > WIN)
            def _more(j=j, most=most):
                lax.fori_loop(1, (most + WIN - 1) // WIN, lambda p, c: (window(j, p, False), c)[1], 0)
        for s in range(n_sub):
            scatter(s, 0, 2 * PIECE)
        for s in range(n_sub):
            @pl.when(n_pieces[s] > 2)
            def _third_out(s=s):
                scatter(s, 2 * PIECE, PIECE)

    @pl.when(jnp.logical_not(fits))
    def _unpacked():
        for j in range(n_exp):
            for s in range(n_sub):
                rows = pl.ds(s * tm, tm)

                def block(bi, carry, j=j, s=s, rows=rows):
                    slot0 = off[s][j] + bi * PIECE
                    pick, wrow = one_hot_rows(s, slot0, PIECE, [j])
                    out = mlp(_dot(pick, h_ref[rows, :]).astype(BF16), j)
                    ow = (out * wrow * gate).astype(BF16)
                    o_ref[rows, :] += lax.dot_general(pick, ow, tn, preferred_element_type=F32)
                    return carry

                lax.fori_loop(0, (cnt[s][j] + PIECE - 1) // PIECE, block, 0)


def _moe(h2, x1, mod, rank_t, w_t, counts, w_gate, w_up, w_down, ws_gate, ws_up, ws_down, seq):
    T, D = h2.shape
    ns, E, tm = rank_t.shape
    tmo = min(MOE_OUTER, seq)
    n_sub = tmo // tm
    n_exp = MOE_EXPERTS_PER_STEP
    per_b = seq // tmo
    ws = [w_gate.astype(BF16), w_up.astype(BF16), w_down.astype(BF16)]
    sh = [ws_gate.astype(BF16), ws_up.astype(BF16), ws_down.astype(BF16)]
    tok = lambda: pl.BlockSpec((tmo, D), lambda t, e, c: (t, 0))
    sub = lambda: pl.BlockSpec((n_sub, E, tm), lambda t, e, c: (t, 0, 0))
    exp = lambda a: pl.BlockSpec((n_exp,) + a.shape[1:], lambda t, e, c: (e, 0, 0))
    full = lambda a: pl.BlockSpec(a.shape, lambda t, e, c: (0, 0))
    grid_spec = pltpu.PrefetchScalarGridSpec(
        num_scalar_prefetch=1,
        grid=(T // tmo, E // n_exp),
        in_specs=[tok(), tok(), pl.BlockSpec((None, N_MOD, D), lambda t, e, c: (t // per_b, 0, 0)),
                  sub(), sub(), exp(ws[0]), exp(ws[1]), exp(ws[2]), full(sh[0]), full(sh[1]), full(sh[2])],
        out_specs=tok(),
        scratch_shapes=[pltpu.VMEM((n_sub, MOE_ROWS, D), BF16), pltpu.VMEM((n_sub, MOE_ROWS, D), BF16),
                        pltpu.VMEM((n_sub, MOE_ROWS, tm), BF16), pltpu.VMEM((n_sub, MOE_ROWS, 1), F32)],
    )
    return pl.pallas_call(
        functools.partial(_moe_kernel, n_sub=n_sub, tm=tm, n_exp=n_exp),
        grid_spec=grid_spec,
        out_shape=jax.ShapeDtypeStruct((T, D), F32),
        compiler_params=_params(("arbitrary", "arbitrary")),
        name="moe",
    )(counts, h2, x1, mod, rank_t, w_t, *ws, *sh)


def kernel(x, c, w_ada, b_ada, norm_mix_gain, norm_ffn_gain, w_in, s5_lambda_re, s5_lambda_im, s5_log_dt, s5_b_re, s5_b_im, s5_c_re, s5_c_im, s5_d, s5_w_glu, s5_b_glu, q_norm_gain, k_norm_gain, cmp_pe, cmp_w1, cmp_b1, cmp_w2, cmp_b2, w_branch_a, w_branch_b, w_out, w_router, router_bias, w_gate, w_up, w_down, ws_gate, ws_up, ws_down):
    B, S, D = x.shape
    for l in range(w_ada.shape[0]):
        mod = _ada(c, w_ada[l], b_ada[l]).reshape(B, N_MOD, D)
        u, kvc, qt, ks, vst, kwn, vwt, gt, gm = _inproj(x, mod, norm_mix_gain[l:l + 1], w_in[l],
                                                        q_norm_gain[l], k_norm_gain[l])
        tables = _s5_tables(s5_lambda_re[l], s5_lambda_im[l], s5_log_dt[l], s5_b_re[l], s5_b_im[l],
                            s5_c_re[l], s5_c_im[l], s5_d[l])
        ys5 = _s5(u, tables)
        kvc_c = _compress(kvc, cmp_pe[l], cmp_w1[l], cmp_b1[l], cmp_w2[l], cmp_b2[l], k_norm_gain[l, 0])
        o_nsa = _attend(qt, ks, vst, kwn, vwt, gt, kvc_c)
        x1, h2, scores_t = _merge(x, mod, ys5, o_nsa, gm, s5_w_glu[l], s5_b_glu[l], w_branch_a[l],
                                  w_branch_b[l], w_out[l], norm_ffn_gain[l:l + 1], w_router[l])
        rank_t, w_t, cnt = _route(scores_t, router_bias[l])
        counts = cnt[:, :, 0].astype(jnp.int32).reshape(-1)
        x = _moe(h2.reshape(B * S, D), x1.reshape(B * S, D), mod, rank_t, w_t, counts,
                 w_gate[l], w_up[l], w_down[l], ws_gate[l], ws_up[l], ws_down[l], S).reshape(B, S, D)
    return x
```

```python
import functools
import math

import numpy as np
import jax
import jax.numpy as jnp
from jax import lax
from jax.experimental import pallas as pl
from jax.experimental.pallas import tpu as pltpu

F32 = jnp.float32
BF16 = jnp.bfloat16

D_MODEL = 1024
S5_WIDTH = 512
S5_GROUP = 16
S5_GROUPS = S5_WIDTH // S5_GROUP
S5_STATE = 64
N_HEADS = 8
N_KV_HEADS = 2
GQA_GROUP = N_HEADS // N_KV_HEADS
HEAD_DIM = 64
NSA_WIDTH = N_HEADS * HEAD_DIM
KV_WIDTH = 2 * N_KV_HEADS * HEAD_DIM
CMP_BLOCK = 32
CMP_STRIDE = 16
CMP_HIDDEN = 256
SEL_BLOCK = 64
SEL_TOPK = 8
WINDOW = 256
Q_BLOCK = 128
FORCE_BONUS = 1e3
N_EXPERTS = 64
TOP_K = 8
N_EXPERT_GROUPS = 8
TOPK_EXPERT_GROUPS = 4
EXPERT_HIDDEN = 256
SHARED_HIDDEN = 256
ROUTED_SCALE = 2.5
RMS_EPS = 1e-6
NEG_INF = -1e30
N_MOD = 6

LANES = 128
SUBLANES = 8
S5_CHUNK = 16
S5_CW = S5_CHUNK * S5_GROUP
ROW_TILE = 512
SEL_KEY_CHUNK = 512
POS_BASE = 64
POS_ROWS = 16
AUG = 128
ROUTE_TILE = 256
MOE_OUTER = 1024
MOE_EXPERTS_PER_STEP = 4
MOE_PIECE = 128
MOE_WIN = 48
MOE_ROWS = 512
RANK_NONE = -float(1 << 20)
VMEM_LIMIT = 56 * 1024 * 1024


def _dot(a, b):
    return jnp.dot(a, b, preferred_element_type=F32)


def _split(a):
    hi = a.astype(BF16)
    lo = (a - hi.astype(F32)).astype(BF16)
    return hi, lo


def _dot3(a, bh, bl):
    ah, al = _split(a)
    return _dot(ah, bh) + (_dot(al, bh) + _dot(ah, bl))


def _segment_transpose(x):
    assert x.shape[-2:] == (SUBLANES, LANES) and LANES // S5_GROUP == SUBLANES
    nd = x.ndim
    i = lax.broadcasted_iota(jnp.int32, x.shape, nd - 2)
    seg = lax.broadcasted_iota(jnp.int32, x.shape, nd - 1) // S5_GROUP
    out = x
    for d in range(1, SUBLANES):
        r = pltpu.roll(pltpu.roll(x, SUBLANES - d, axis=nd - 2), S5_GROUP * d, axis=nd - 1)
        out = jnp.where(seg == ((i + d) & (SUBLANES - 1)), r, out)
    return out


def _to_chunk_major(u, put):
    rows = u.shape[0]
    u3 = u.reshape(rows // SUBLANES, SUBLANES, S5_WIDTH)
    halves = S5_CHUNK // SUBLANES
    for jb in range(S5_WIDTH // LANES):
        t = _segment_transpose(u3[:, :, LANES * jb:LANES * (jb + 1)])
        t = t.reshape(rows // S5_CHUNK, halves, SUBLANES, LANES)
        for hf in range(halves):
            put(slice(SUBLANES * jb, SUBLANES * (jb + 1)), slice(LANES * hf, LANES * (hf + 1)), t[:, hf])


def _from_chunk_major(get, rows):
    halves = S5_CHUNK // SUBLANES
    cols = []
    for jb in range(S5_WIDTH // LANES):
        parts = [_segment_transpose(get(slice(SUBLANES * jb, SUBLANES * (jb + 1)), slice(LANES * hf, LANES * (hf + 1))))
                 for hf in range(halves)]
        cols.append(jnp.stack(parts, axis=1).reshape(rows, LANES))
    return jnp.concatenate(cols, axis=1)


def _params(sem):
    return pltpu.CompilerParams(dimension_semantics=sem, vmem_limit_bytes=VMEM_LIMIT)


def _ada_kernel(c_ref, w_ref, b_ref, o_ref):
    cs = jax.nn.silu(c_ref[...])
    wh, wl = _split(w_ref[...])
    o_ref[...] = _dot3(cs, wh, wl) + b_ref[...]


def _ada(c, w_ada, b_ada):
    B, D = c.shape
    return pl.pallas_call(
        _ada_kernel,
        grid=(N_MOD,),
        in_specs=[pl.BlockSpec((B, D), lambda j: (0, 0)),
                  pl.BlockSpec((D, D), lambda j: (0, j)),
                  pl.BlockSpec((1, D), lambda j: (0, j))],
        out_specs=pl.BlockSpec((B, D), lambda j: (0, j)),
        out_shape=jax.ShapeDtypeStruct((B, N_MOD * D), F32),
        compiler_params=_params(("arbitrary",)),
        name="ada",
    )(c, w_ada, b_ada.reshape(1, N_MOD * D))


def _head_norm(v, bd, gain):
    sq = v * v
    sh, sl = _split(sq)
    ms = _dot(sh, bd) + _dot(sl, bd)
    return v * lax.rsqrt(ms + RMS_EPS) * gain


def _inproj_kernel(x_ref, mod_ref, gain_ref, wm_ref, wt_ref, wgn_ref, wgm_ref, bd_ref, qg_ref, kg_ref,
                   u_ref, kvc_ref, qt_ref, ksa_ref, vst_ref, kwa_ref, vwt_ref, gt_ref, gm_ref):
    x = x_ref[...]
    shift = mod_ref[0:1, :]
    scale = mod_ref[1:2, :]
    y = x * lax.rsqrt(jnp.mean(x * x, axis=-1, keepdims=True) + RMS_EPS)
    h = (y * gain_ref[...]) * (1.0 + scale) + shift
    hb = h.astype(BF16)
    main = _dot(hb, wm_ref[...])
    tm = x.shape[0]

    def put(groups, lanes, block):
        u_ref[:, groups, lanes] = block

    _to_chunk_major(main[:, :S5_WIDTH], put)
    o = S5_WIDTH
    kvc_ref[...] = main[:, o:o + KV_WIDTH]
    o += KV_WIDTH
    kw = N_KV_HEADS * HEAD_DIM
    bd = bd_ref[...]
    ks = _head_norm(main[:, o:o + kw], bd[:kw, :kw], kg_ref[1:2, :]).astype(BF16)
    kwn = _head_norm(main[:, o + kw:o + 2 * kw], bd[:kw, :kw], kg_ref[2:3, :]).astype(BF16)
    pos = pl.program_id(1) * tm + lax.broadcasted_iota(jnp.int32, (tm, 1), 0)
    lane = lax.broadcasted_iota(jnp.int32, (tm, AUG), 1)
    digits = jnp.where(lane == 0, pos // POS_BASE, jnp.where(lane == 1, pos % POS_BASE, 0))
    digits = digits[:, :AUG - HEAD_DIM].astype(F32).astype(BF16)
    onehot = jnp.where(lane == pos // SEL_BLOCK, 1.0, 0.0).astype(BF16)
    for hh in range(N_KV_HEADS):
        ksa_ref[hh, :, 0:HEAD_DIM] = ks[:, hh * HEAD_DIM:(hh + 1) * HEAD_DIM]
        ksa_ref[hh, :, HEAD_DIM:AUG] = digits
        ksa_ref[hh, :, AUG:] = onehot
        kwa_ref[hh, :, 0:HEAD_DIM] = kwn[:, hh * HEAD_DIM:(hh + 1) * HEAD_DIM]
        kwa_ref[hh, :, HEAD_DIM:] = digits

    nt = (((1,), (1,)), ((), ()))
    tt = lax.dot_general(wt_ref[...], hb, nt, preferred_element_type=F32)
    qt = tt[:NSA_WIDTH]
    sq = qt * qt
    sh, sl = _split(sq)
    qn = (qt * lax.rsqrt(_dot(bd, sh) + _dot(bd, sl) + RMS_EPS) * qg_ref[...] * (HEAD_DIM ** -0.5)).astype(BF16)
    QW = GQA_GROUP * Q_BLOCK
    for qb in range(tm // Q_BLOCK):
        for hd in range(N_HEADS):
            hh, g = divmod(hd, GQA_GROUP)
            qt_ref[hh, :, qb * QW + g * Q_BLOCK:qb * QW + (g + 1) * Q_BLOCK] = (
                qn[hd * HEAD_DIM:(hd + 1) * HEAD_DIM, qb * Q_BLOCK:(qb + 1) * Q_BLOCK])
    vst_ref[...] = tt[NSA_WIDTH:NSA_WIDTH + kw].reshape(N_KV_HEADS, HEAD_DIM, tm).astype(BF16)
    vwt_ref[...] = tt[NSA_WIDTH + kw:].reshape(N_KV_HEADS, HEAD_DIM, tm).astype(BF16)
    gn = jax.nn.sigmoid(lax.dot_general(wgn_ref[...], hb, nt, preferred_element_type=F32))
    for qb in range(tm // Q_BLOCK):
        for br in range(3):
            for hd in range(N_HEADS):
                hh, g = divmod(hd, GQA_GROUP)
                r = br * N_HEADS + hd
                gt_ref[hh, br:br + 1, qb * QW + g * Q_BLOCK:qb * QW + (g + 1) * Q_BLOCK] = (
                    gn[r:r + 1, qb * Q_BLOCK:(qb + 1) * Q_BLOCK])
    gm_ref[...] = jax.nn.sigmoid(_dot(hb, wgm_ref[...])).astype(BF16)


def _s5_chunk_spec(tm):
    assert S5_CHUNK == 16 and S5_GROUP == 16, "the segment transposes assume 16 steps x 16 channels"
    return pl.BlockSpec((tm // S5_CHUNK, None, S5_GROUPS, S5_CW), lambda b, i: (i, b, 0, 0))


def _inproj(x, mod, gain, w_in, q_gain, k_gain):
    B, S, D = x.shape
    tm = min(ROW_TILE, S)
    assert S // SEL_BLOCK <= AUG and tm % Q_BLOCK == 0
    kw = N_KV_HEADS * HEAD_DIM
    cols = np.cumsum((0,) + (S5_WIDTH, NSA_WIDTH, KV_WIDTH, KV_WIDTH, KV_WIDTH, 3 * N_HEADS, 2 * D))
    c_u, c_q, c_kvc, c_kvs, c_kvw, c_gn, c_gm = cols[:7]
    sl = lambda a, n: w_in[:, a:a + n]
    wm = jnp.concatenate([sl(c_u, S5_WIDTH), sl(c_kvc, KV_WIDTH), sl(c_kvs, kw), sl(c_kvw, kw)], axis=1).astype(BF16)
    wt = jnp.concatenate([sl(c_q, NSA_WIDTH), sl(c_kvs + kw, kw), sl(c_kvw + kw, kw)], axis=1).T.astype(BF16)
    wgn = sl(c_gn, 3 * N_HEADS).T.astype(BF16)
    wgm = sl(c_gm, 2 * D).astype(BF16)
    seg = np.arange(NSA_WIDTH) // HEAD_DIM
    bd = jnp.asarray((seg[:, None] == seg[None, :]).astype(np.float32) / HEAD_DIM, BF16)
    qg = jnp.tile(q_gain, N_HEADS).reshape(NSA_WIDTH, 1)
    kg = jnp.tile(k_gain, (1, N_KV_HEADS))
    nq = tm // Q_BLOCK * GQA_GROUP * Q_BLOCK
    row = lambda w: pl.BlockSpec((None, tm, w), lambda b, i: (b, i, 0))
    full = lambda a: pl.BlockSpec(a.shape, lambda b, i: (0,) * a.ndim)
    rows4 = lambda w: pl.BlockSpec((None, N_KV_HEADS, tm, w), lambda b, i: (b, 0, i, 0))
    cols4 = lambda r, w: pl.BlockSpec((None, N_KV_HEADS, r, w), lambda b, i: (b, 0, 0, i))
    nqt = S // Q_BLOCK * GQA_GROUP * Q_BLOCK
    return pl.pallas_call(
        _inproj_kernel,
        grid=(B, S // tm),
        in_specs=[row(D), pl.BlockSpec((None, N_MOD, D), lambda b, i: (b, 0, 0)),
                  full(gain), full(wm), full(wt), full(wgn), full(wgm), full(bd), full(qg), full(kg)],
        out_specs=[_s5_chunk_spec(tm), row(KV_WIDTH), cols4(HEAD_DIM, nq), rows4(2 * AUG), cols4(HEAD_DIM, tm),
                   rows4(AUG), cols4(HEAD_DIM, tm), cols4(3, nq), row(2 * D)],
        out_shape=[jax.ShapeDtypeStruct((S // S5_CHUNK, B, S5_GROUPS, S5_CW), F32),
                   jax.ShapeDtypeStruct((B, S, KV_WIDTH), F32),
                   jax.ShapeDtypeStruct((B, N_KV_HEADS, HEAD_DIM, nqt), BF16),
                   jax.ShapeDtypeStruct((B, N_KV_HEADS, S, 2 * AUG), BF16),
                   jax.ShapeDtypeStruct((B, N_KV_HEADS, HEAD_DIM, S), BF16),
                   jax.ShapeDtypeStruct((B, N_KV_HEADS, S, AUG), BF16),
                   jax.ShapeDtypeStruct((B, N_KV_HEADS, HEAD_DIM, S), BF16),
                   jax.ShapeDtypeStruct((B, N_KV_HEADS, 3, nqt), F32),
                   jax.ShapeDtypeStruct((B, S, 2 * D), BF16)],
        compiler_params=_params(("arbitrary", "arbitrary")),
        name="inproj",
    )(x, mod, gain, wm, wt, wgn, wgm, bd, qg, kg)


def _s5_tables(lam_re, lam_im, log_dt, b_re, b_im, c_re, c_im, d_skip):
    L, P, N, G = S5_CHUNK, S5_GROUP, S5_STATE, S5_GROUPS
    hp = lax.Precision.HIGHEST
    lr, li = lam_re.astype(F32), lam_im.astype(F32)
    dt = jnp.exp(log_dt.astype(F32))[:, None]
    mag = jnp.exp(lr * dt)
    abar_re, abar_im = mag * jnp.cos(li * dt), mag * jnp.sin(li * dt)
    num_re, num_im = abar_re - 1.0, abar_im
    den = lr * lr + li * li
    coef_re = (num_re * lr + num_im * li) / den
    coef_im = (num_im * lr - num_re * li) / den
    br, bi = b_re.astype(F32), b_im.astype(F32)
    bbar_re = coef_re[..., None] * br - coef_im[..., None] * bi
    bbar_im = coef_re[..., None] * bi + coef_im[..., None] * br
    k = jnp.arange(L + 1, dtype=F32)[:, None, None]
    pmag = jnp.exp(lr * dt * k)
    pre, pim = pmag * jnp.cos(li * dt * k), pmag * jnp.sin(li * dt * k)
    cr, ci = c_re.astype(F32), c_im.astype(F32)
    ca_re = cr[None] * pre[:, :, None, :] - ci[None] * pim[:, :, None, :]
    ca_im = cr[None] * pim[:, :, None, :] + ci[None] * pre[:, :, None, :]
    kern = (jnp.einsum('kgpn,gnq->gkqp', ca_re[:L], bbar_re, precision=hp)
            - jnp.einsum('kgpn,gnq->gkqp', ca_im[:L], bbar_im, precision=hp))
    s_i = np.arange(L)[:, None]
    t_i = np.arange(L)[None, :]
    tau = np.clip(t_i - s_i, 0, L - 1)
    causal = jnp.asarray((t_i >= s_i).astype(np.float32))
    mt = kern[:, tau] * causal[None, :, :, None, None]
    mt = mt.transpose(0, 1, 3, 2, 4).reshape(G, L * P, L * P)
    rev = np.arange(L - 1, -1, -1)
    ab_re = pre[rev][..., None] * bbar_re[None] - pim[rev][..., None] * bbar_im[None]
    ab_im = pre[rev][..., None] * bbar_im[None] + pim[rev][..., None] * bbar_re[None]
    ws = jnp.concatenate([ab_re, ab_im], axis=2)
    ws = ws.transpose(1, 0, 3, 2).reshape(G, L * P, 2 * N)
    wo = jnp.concatenate([ca_re[1:], -ca_im[1:]], axis=3)
    wo = wo.transpose(1, 3, 0, 2).reshape(G, 2 * N, L * P)
    al = jnp.stack([pre[L], pim[L]], axis=1)
    dv = jnp.tile(d_skip.astype(F32), (1, L)).reshape(G, 1, L * P)
    return mt, ws, wo, al, dv


def _s5_kernel(u_ref, mth_ref, mtl_ref, wsh_ref, wsl_ref, woh_ref, wol_ref, al_ref, dv_ref,
               y_ref, vr_scr, vi_scr, xr_scr, xi_scr, *, n_chunks, bsz):
    N = S5_STATE
    u = u_ref[...]
    uh, ul = _split(u)
    y = _dot(uh, mth_ref[...]) + (_dot(ul, mth_ref[...]) + _dot(uh, mtl_ref[...]))
    v = _dot(uh, wsh_ref[...]) + (_dot(ul, wsh_ref[...]) + _dot(uh, wsl_ref[...]))
    vr_scr[...] = v[:, :N]
    vi_scr[...] = v[:, N:]
    a_r = al_ref[0:1, :]
    a_i = al_ref[1:2, :]

    def step(c, x):
        xr, xi = x
        r = pl.ds(pl.multiple_of(c * bsz, bsz), bsz)
        xr_scr[r, :] = xr
        xi_scr[r, :] = xi
        return (a_r * xr - a_i * xi + vr_scr[r, :], a_r * xi + a_i * xr + vi_scr[r, :])

    zero = jnp.zeros((bsz, N), F32)
    lax.fori_loop(0, n_chunks, step, (zero, zero), unroll=8)
    xp = jnp.concatenate([xr_scr[...], xi_scr[...]], axis=1)
    y = y + _dot3(xp, woh_ref[...], wol_ref[...])
    y_ref[...] = y + dv_ref[...] * u


def _s5(ug, tables):
    nc, B, G, cw = ug.shape
    N = S5_STATE
    mt, ws, wo, al, dv = tables
    mth, mtl = _split(mt)
    wsh, wsl = _split(ws)
    woh, wol = _split(wo)
    grp = lambda a: pl.BlockSpec((None,) + a.shape[1:], lambda g: (g,) + (0,) * (a.ndim - 1))
    ugt = ug.reshape(nc * B, G, cw).transpose(1, 0, 2)
    y = pl.pallas_call(
        functools.partial(_s5_kernel, n_chunks=nc, bsz=B),
        grid=(G,),
        in_specs=[grp(ugt), grp(mth), grp(mtl), grp(wsh), grp(wsl), grp(woh), grp(wol), grp(al), grp(dv)],
        out_specs=grp(ugt),
        out_shape=jax.ShapeDtypeStruct(ugt.shape, F32),
        scratch_shapes=[pltpu.VMEM((nc * B, N), F32)] * 4,
        compiler_params=_params(("arbitrary",)),
        name="s5",
    )(ugt, mth, mtl, wsh, wsl, woh, wol, al, dv)
    return y.transpose(1, 0, 2).reshape(nc, B, G, cw)


def _compress_kernel(x_ref, pe_ref, w1_ref, b1_ref, w2_ref, b2_ref, kg_ref, o_ref, *, n_rows):
    x = x_ref[...]
    half = CMP_STRIDE * HEAD_DIM
    a = _dot((x + pe_ref[0:1, :]).astype(BF16), w1_ref[:half, :])
    b = _dot((x + pe_ref[1:2, :]).astype(BF16), w1_ref[half:, :])
    hid = jax.nn.gelu(a + pltpu.roll(b, n_rows - 1, axis=0) + b1_ref[...])
    out = _dot(hid.astype(BF16), w2_ref[...]) + b2_ref[...]
    normed = out * lax.rsqrt(jnp.mean(out * out, axis=-1, keepdims=True) + RMS_EPS) * kg_ref[...]
    is_key = pl.program_id(1) < N_KV_HEADS
    o_ref[...] = jnp.where(is_key, normed, out).astype(BF16)


def _compress(kv_c, cmp_pe, cmp_w1, cmp_b1, cmp_w2, cmp_b2, k_gain0):
    B, S, _ = kv_c.shape
    nr = S // CMP_STRIDE
    half = CMP_STRIDE * HEAD_DIM
    nj = 2 * N_KV_HEADS
    xc = kv_c.reshape(B, nr, CMP_STRIDE, nj, HEAD_DIM).transpose(0, 3, 1, 2, 4).reshape(B, nj, nr, half)
    pe = cmp_pe.reshape(2, 2, half)
    w1 = cmp_w1.astype(BF16)
    w2 = cmp_w2.astype(BF16)
    b1 = cmp_b1.reshape(2, 1, CMP_HIDDEN)
    b2 = cmp_b2.reshape(2, 1, HEAD_DIM)
    kv = lambda a: pl.BlockSpec((None,) + a.shape[1:], lambda b, j: (j // N_KV_HEADS,) + (0,) * (a.ndim - 1))
    return pl.pallas_call(
        functools.partial(_compress_kernel, n_rows=nr),
        grid=(B, nj),
        in_specs=[pl.BlockSpec((None, None, nr, half), lambda b, j: (b, j, 0, 0)),
                  kv(pe), kv(w1), kv(b1), kv(w2), kv(b2),
                  pl.BlockSpec((1, HEAD_DIM), lambda b, j: (0, 0))],
        out_specs=pl.BlockSpec((None, None, nr, HEAD_DIM), lambda b, j: (b, j, 0, 0)),
        out_shape=jax.ShapeDtypeStruct((B, nj, nr, HEAD_DIM), BF16),
        compiler_params=_params(("arbitrary", "arbitrary")),
        name="compress",
    )(xc, pe, w1, b1, w2, b2, k_gain0.reshape(1, HEAD_DIM))


def _softmax_cols(s, mask):
    s = jnp.where(mask, s, NEG_INF)
    m = jnp.max(s, axis=0, keepdims=True)
    p = jnp.where(mask, jnp.exp(s - m), 0.0)
    return p / jnp.maximum(jnp.sum(p, axis=0, keepdims=True), 1e-20)


def _attend_kernel(qt_ref, kc_ref, vct_ref, ks_ref, vst_ref, kw_ref, vwt_ref, g_ref, slope_ref, ovt_ref,
                   o_ref, qa_scr, sa_scr, sb_scr, *, n_sel, n_pick, n_cmp_rows):
    QW = GQA_GROUP * Q_BLOCK
    i = pl.program_id(2)
    t0 = i * Q_BLOCK
    slope = slope_ref[...]
    tq = t0 + (lax.broadcasted_iota(jnp.int32, (1, QW), 1) & (Q_BLOCK - 1))

    r = lax.broadcasted_iota(jnp.int32, (POS_ROWS, QW), 0)
    qa_scr[0:HEAD_DIM, :] = qt_ref[...]
    qa_scr[HEAD_DIM:HEAD_DIM + POS_ROWS, :] = jnp.where(
        r == 0, slope * POS_BASE, jnp.where(r == 1, slope, 0.0)).astype(BF16)
    qa_scr[HEAD_DIM + POS_ROWS:AUG, :] = jnp.zeros((AUG - HEAD_DIM - POS_ROWS, QW), BF16)
    qa = qa_scr[0:AUG, :]

    sc = _dot(kc_ref[...], qa)
    cpos = lax.broadcasted_iota(jnp.int32, (n_cmp_rows, QW), 0) * CMP_STRIDE + (CMP_BLOCK - 1)
    p_c = _softmax_cols(sc, cpos <= tq)
    o_c = _dot(vct_ref[...], p_c.astype(BF16))
    psum = p_c[:, 0:Q_BLOCK]
    for g in range(1, GQA_GROUP):
        psum = psum + p_c[:, g * Q_BLOCK:(g + 1) * Q_BLOCK]
    ph, pl_ = _split(psum)
    imp = _dot(ovt_ref[...], ph) + _dot(ovt_ref[...], pl_)

    jb = lax.broadcasted_iota(jnp.int32, (n_sel, Q_BLOCK), 0)
    cur = (t0 + lax.broadcasted_iota(jnp.int32, (1, Q_BLOCK), 1)) // SEL_BLOCK
    forced = (jb == 0) | (jb == cur) | (jb == cur - 1)
    imp = jnp.where(forced, imp + FORCE_BONUS, imp)
    imp = jnp.where(jb <= cur, imp, -1.0)
    bias = jnp.full((n_sel, Q_BLOCK), NEG_INF, F32)
    for _ in range(n_pick):
        m = jnp.max(imp, axis=0, keepdims=True)
        first = jnp.min(jnp.where(imp == m, jb, n_sel), axis=0, keepdims=True)
        hit = jb == first
        bias = jnp.where(hit, 0.0, bias)
        imp = jnp.where(hit, -jnp.inf, imp)
    bias = jnp.concatenate([bias] * GQA_GROUP, axis=1).astype(BF16)
    qa_scr[AUG:, :] = jnp.concatenate([bias, jnp.zeros((AUG - n_sel, QW), BF16)], axis=0)

    KC = SEL_KEY_CHUNK
    n_chunks = (t0 + Q_BLOCK + KC - 1) // KC

    def scores_to(ref, j):
        k0 = pl.multiple_of(j * KC, KC)
        ref[...] = _dot(ks_ref[pl.ds(k0, KC), :], qa_scr[...])

    def accumulate(carry, j, s, p_of):
        m, l, acc = carry
        k0 = pl.multiple_of(j * KC, KC)
        m_new = jnp.maximum(m, jnp.max(s, axis=0, keepdims=True))
        alpha = jnp.exp(m - m_new)
        p = p_of(jnp.exp(s - m_new))
        l = alpha * l + jnp.sum(p, axis=0, keepdims=True)
        acc = alpha * acc + _dot(vst_ref[:, pl.ds(k0, KC)], p.astype(BF16))
        return m_new, l, acc

    keep = lambda p: p
    last = n_chunks - 1
    scores_to(sa_scr, 0)

    def two_chunks(i, state):
        j = 2 * i
        scores_to(sb_scr, j + 1)
        state = accumulate(state, j, sa_scr[...], keep)
        scores_to(sa_scr, j + 2)
        return accumulate(state, j + 1, sb_scr[...], keep)

    def odd_chunk(state):
        state = accumulate(state, last - 1, sa_scr[...], keep)
        scores_to(sa_scr, last)
        return state

    init = (jnp.full((1, QW), NEG_INF, F32), jnp.zeros((1, QW), F32), jnp.zeros((HEAD_DIM, QW), F32))
    state = lax.fori_loop(0, last // 2, two_chunks, init)
    state = lax.cond(last % 2 == 1, odd_chunk, lambda st: st, state)
    visible = (last * KC + lax.broadcasted_iota(jnp.int32, (KC, QW), 0)) <= tq
    _, l_s, acc_s = accumulate(state, last, jnp.where(visible, sa_scr[...], NEG_INF),
                               lambda p: jnp.where(visible, p, 0.0))
    o_s = acc_s / jnp.maximum(l_s, 1e-20)

    WK = WINDOW + Q_BLOCK
    w0 = pl.multiple_of(jnp.maximum(t0 - WINDOW, 0), Q_BLOCK)
    sw = _dot(kw_ref[pl.ds(w0, WK), :], qa)
    dist = tq - (w0 + lax.broadcasted_iota(jnp.int32, (WK, QW), 0))
    in_win = (dist | (WINDOW - 1 - dist)) >= 0
    p_w = _softmax_cols(sw, in_win)
    o_w = _dot(vwt_ref[:, pl.ds(w0, WK)], p_w.astype(BF16))

    o = g_ref[0:1, :] * o_c + g_ref[1:2, :] * o_s + g_ref[2:3, :] * o_w
    pairs = []
    for g in range(0, GQA_GROUP, 2):
        sq = jnp.concatenate([o[:, g * Q_BLOCK:(g + 1) * Q_BLOCK], o[:, (g + 1) * Q_BLOCK:(g + 2) * Q_BLOCK]], axis=0)
        pairs.append(sq.T)
    o_ref[...] = jnp.concatenate(pairs, axis=1).astype(BF16)


def _attend(qt, ks, vst, kwn, vwt, gt, kvc_c):
    B, _, S, _ = ks.shape
    nqb = S // Q_BLOCK
    QW = GQA_GROUP * Q_BLOCK
    n_sel = S // SEL_BLOCK
    n_pick = min(SEL_TOPK, n_sel)
    nr = S // CMP_STRIDE
    n_cmp = (S - CMP_BLOCK) // CMP_STRIDE + 1
    assert n_sel % 16 == 0 and n_sel <= AUG

    cpos = np.arange(nr) * CMP_STRIDE + CMP_BLOCK - 1
    digits = np.zeros((nr, AUG - HEAD_DIM), np.float32)
    digits[:, 0] = cpos // POS_BASE
    digits[:, 1] = cpos % POS_BASE
    kc = kvc_c[:, :N_KV_HEADS]
    kc = jnp.concatenate([kc, jnp.broadcast_to(jnp.asarray(digits, BF16), kc.shape[:2] + digits.shape)], axis=-1)
    vct = kvc_c[:, N_KV_HEADS:].transpose(0, 1, 3, 2)
    slopes = 2.0 ** (-8.0 * np.arange(1, N_HEADS + 1) / N_HEADS)
    slope_t = jnp.asarray(np.repeat(slopes.reshape(N_KV_HEADS, GQA_GROUP), Q_BLOCK, axis=1)
                          .reshape(N_KV_HEADS, 1, QW), F32)
    cmp_start = np.arange(n_cmp) * CMP_STRIDE
    sel_start = np.arange(n_sel) * SEL_BLOCK
    ov = ((cmp_start[:, None] <= sel_start[None, :] + SEL_BLOCK - 1)
          & (cmp_start[:, None] + CMP_BLOCK - 1 >= sel_start[None, :])).astype(np.float32)
    ovt = np.zeros((n_sel, nr), np.float32)
    ovt[:, :n_cmp] = ov.T
    ovt = jnp.asarray(ovt, BF16)

    per_bh = lambda r, c: pl.BlockSpec((None, None, r, c), lambda b, h, i: (b, h, 0, 0))
    per_q = lambda r: pl.BlockSpec((None, None, r, QW), lambda b, h, i: (b, h, 0, i))
    return pl.pallas_call(
        functools.partial(_attend_kernel, n_sel=n_sel, n_pick=n_pick, n_cmp_rows=nr),
        grid=(B, N_KV_HEADS, nqb),
        in_specs=[per_q(HEAD_DIM), per_bh(nr, AUG), per_bh(HEAD_DIM, nr),
                  per_bh(S, 2 * AUG), per_bh(HEAD_DIM, S), per_bh(S, AUG), per_bh(HEAD_DIM, S),
                  per_q(3),
                  pl.BlockSpec((None, 1, QW), lambda b, h, i: (h, 0, 0)),
                  pl.BlockSpec((n_sel, nr), lambda b, h, i: (0, 0))],
        out_specs=pl.BlockSpec((None, Q_BLOCK, GQA_GROUP * HEAD_DIM), lambda b, h, i: (b, i, h)),
        out_shape=jax.ShapeDtypeStruct((B, S, NSA_WIDTH), BF16),
        scratch_shapes=[pltpu.VMEM((2 * AUG, QW), BF16), pltpu.VMEM((SEL_KEY_CHUNK, QW), F32),
                        pltpu.VMEM((SEL_KEY_CHUNK, QW), F32)],
        compiler_params=_params(("arbitrary", "arbitrary", "arbitrary")),
        name="attend",
    )(qt, kc, vct, ks, vst, kwn, vwt, gt, slope_t, ovt)


def _merge_kernel(x_ref, mod_ref, ys_ref, on_ref, gm_ref, wglu_ref, bglu_ref, wa_ref, wb_ref, wo_ref,
                  gain_ref, wrh_ref, wrl_ref, x1_ref, h2_ref, sc_ref):
    D = D_MODEL
    z = jax.nn.gelu(_from_chunk_major(lambda groups, lanes: ys_ref[:, groups, lanes], x_ref.shape[0]))
    glu = z * jax.nn.sigmoid(_dot(z.astype(BF16), wglu_ref[...]) + bglu_ref[...])
    ya = _dot(glu.astype(BF16), wa_ref[...])
    yb = _dot(on_ref[...], wb_ref[...])
    merged = gm_ref[:, :D].astype(F32) * ya + gm_ref[:, D:].astype(F32) * yb
    x1 = x_ref[...] + mod_ref[2:3, :] * _dot(merged.astype(BF16), wo_ref[...])
    x1_ref[...] = x1
    y = x1 * lax.rsqrt(jnp.mean(x1 * x1, axis=-1, keepdims=True) + RMS_EPS)
    h2 = (y * gain_ref[...]) * (1.0 + mod_ref[4:5, :]) + mod_ref[3:4, :]
    h2_ref[...] = h2.astype(BF16)
    hh, hl = _split(h2)
    nt = (((1,), (1,)), ((), ()))
    dg = lambda a, b: lax.dot_general(a, b, nt, preferred_element_type=F32)
    logits = dg(wrh_ref[...], hh) + (dg(wrh_ref[...], hl) + dg(wrl_ref[...], hh))
    sc_ref[...] = jax.nn.sigmoid(logits)


def _merge(x, mod, ys5, o_nsa, gm, w_glu, b_glu, w_a, w_b, w_out, gain_f, w_router):
    B, S, D = x.shape
    tm = min(ROW_TILE, S)
    wrh, wrl = _split(w_router.T)
    ws = [w_glu.astype(BF16), b_glu.reshape(1, -1), w_a.astype(BF16), w_b.astype(BF16), w_out.astype(BF16),
          gain_f, wrh, wrl]
    row = lambda w: pl.BlockSpec((None, tm, w), lambda b, i: (b, i, 0))
    full = lambda a: pl.BlockSpec(a.shape, lambda b, i: (0,) * a.ndim)
    return pl.pallas_call(
        _merge_kernel,
        grid=(B, S // tm),
        in_specs=[row(D), pl.BlockSpec((None, N_MOD, D), lambda b, i: (b, 0, 0)),
                  _s5_chunk_spec(tm), row(NSA_WIDTH), row(2 * D)] + [full(a) for a in ws],
        out_specs=[row(D), row(D), pl.BlockSpec((None, N_EXPERTS, tm), lambda b, i: (b, 0, i))],
        out_shape=[jax.ShapeDtypeStruct((B, S, D), F32), jax.ShapeDtypeStruct((B, S, D), BF16),
                   jax.ShapeDtypeStruct((B, N_EXPERTS, S), F32)],
        compiler_params=_params(("arbitrary", "arbitrary")),
        name="merge",
    )(x, mod, ys5, o_nsa, gm, *ws)


def _first_argmax_rows(v, idx, n):
    m = jnp.max(v, axis=0, keepdims=True)
    first = jnp.min(jnp.where(v == m, idx, n), axis=0, keepdims=True)
    return idx == first, m


def _route_kernel(sc_ref, bias_ref, tri_ref, rank_ref, w_ref, cnt_ref):
    E, NG = N_EXPERTS, N_EXPERT_GROUPS
    GS = E // NG
    sc = sc_ref[...]
    TM = sc.shape[1]
    sel = sc + bias_ref[...]
    i8 = lax.broadcasted_iota(jnp.int32, (GS, TM), 0)
    gscore = []
    for g in range(NG):
        blk = sel[g * GS:(g + 1) * GS, :]
        hit, m1 = _first_argmax_rows(blk, i8, GS)
        m2 = jnp.max(jnp.where(hit, -jnp.inf, blk), axis=0, keepdims=True)
        gscore.append(m1 + m2)
    gscore = jnp.concatenate(gscore, axis=0)
    ig = lax.broadcasted_iota(jnp.int32, (NG, TM), 0)
    gmask = jnp.zeros((NG, TM), F32)
    for _ in range(TOPK_EXPERT_GROUPS):
        hit, _m = _first_argmax_rows(gscore, ig, NG)
        gmask = jnp.where(hit, 1.0, gmask)
        gscore = jnp.where(hit, -jnp.inf, gscore)
    emask = jnp.concatenate([jnp.broadcast_to(gmask[g:g + 1, :], (GS, TM)) for g in range(NG)], axis=0)
    cand = jnp.where(emask > 0.5, sel, NEG_INF)
    ie = lax.broadcasted_iota(jnp.int32, (E, TM), 0)
    chosen = jnp.zeros((E, TM), F32)
    for _ in range(TOP_K):
        hit, _m = _first_argmax_rows(cand, ie, E)
        chosen = jnp.where(hit, 1.0, chosen)
        cand = jnp.where(hit, -jnp.inf, cand)
    w = chosen * sc
    w = w / jnp.sum(w, axis=0, keepdims=True) * ROUTED_SCALE
    sub = rank_ref.shape[-1]
    for q in range(TM // sub):
        cols = slice(q * sub, (q + 1) * sub)
        cb = chosen[:, cols].astype(BF16)
        prefix = _dot(cb, tri_ref[...])
        rank_ref[q] = jnp.where(chosen[:, cols] > 0.5, prefix, RANK_NONE)
        w_ref[q] = w[:, cols]
        cnt_ref[q] = _dot(cb, jnp.ones((sub, LANES), BF16))


def _route(scores_t, router_bias):
    B, E, S = scores_t.shape
    TM = min(ROUTE_TILE, S)
    per_step = min(MOE_OUTER, S) // TM
    nj = S // (TM * per_step)
    ns = B * nj * per_step
    tri = jnp.asarray(np.triu(np.ones((TM, TM), np.float32), k=1), BF16)
    tile = lambda w: pl.BlockSpec((per_step, E, w), lambda b, j: (b * nj + j, 0, 0))
    return pl.pallas_call(
        _route_kernel,
        grid=(B, nj),
        in_specs=[pl.BlockSpec((None, E, TM * per_step), lambda b, j: (b, 0, j)),
                  pl.BlockSpec((E, 1), lambda b, j: (0, 0)),
                  pl.BlockSpec((TM, TM), lambda b, j: (0, 0))],
        out_specs=[tile(TM), tile(TM), tile(LANES)],
        out_shape=[jax.ShapeDtypeStruct((ns, E, TM), F32), jax.ShapeDtypeStruct((ns, E, TM), F32),
                   jax.ShapeDtypeStruct((ns, E, LANES), F32)],
        compiler_params=_params(("arbitrary", "arbitrary")),
        name="route",
    )(scores_t, router_bias.reshape(E, 1), tri)


def _moe_kernel(cnt_ref, h_ref, x1_ref, mod_ref, rank_ref, w_ref, wg_ref, wu_ref, wd_ref,
                sg_ref, su_ref, sd_ref, o_ref, xc, yc, pc, wc, *, n_sub, tm, n_exp):
    to = pl.program_id(0)
    eb = pl.program_id(1)
    gate = mod_ref[5:6, :]

    @pl.when(eb == 0)
    def _shared():
        for s in range(n_sub):
            rows = pl.ds(s * tm, tm)
            hs = h_ref[rows, :]
            hid = jax.nn.silu(_dot(hs, sg_ref[...])) * _dot(hs, su_ref[...])
            o_ref[rows, :] = x1_ref[rows, :] + gate * _dot(hid.astype(BF16), sd_ref[...])

    PIECE, WIN = MOE_PIECE, MOE_WIN
    tn = (((0,), (0,)), ((), ()))
    experts = [eb * n_exp + j for j in range(n_exp)]
    cnt = [[cnt_ref[(to * n_sub + s) * N_EXPERTS + e] for e in experts] for s in range(n_sub)]
    off = []
    for s in range(n_sub):
        o = [jnp.int32(0)]
        for j in range(n_exp):
            o.append(o[-1] + ((cnt[s][j] + 15) // 16) * 16)
        off.append(o)
    n_pieces = [(off[s][n_exp] + PIECE - 1) // PIECE for s in range(n_sub)]
    fits = n_pieces[0] < MOE_ROWS // PIECE
    for s in range(1, n_sub):
        fits = jnp.logical_and(fits, n_pieces[s] < MOE_ROWS // PIECE)

    def one_hot_rows(s, base, rows_n, js):
        slot = (lax.broadcasted_iota(jnp.int32, (rows_n, tm), 0) + base).astype(F32)
        pick = jnp.zeros((rows_n, tm), F32)
        wacc = jnp.zeros((rows_n, tm), F32)
        for j in js:
            hit = (slot - off[s][j].astype(F32)) == rank_ref[s, pl.ds(experts[j], 1), :]
            pick = jnp.where(hit, 1.0, pick)
            wacc = jnp.where(hit, w_ref[s, pl.ds(experts[j], 1), :], wacc)
        return pick.astype(BF16), jnp.sum(wacc, axis=1, keepdims=True)

    def mlp(xg, j):
        hid = jax.nn.silu(_dot(xg, wg_ref[j])) * _dot(xg, wu_ref[j])
        return _dot(hid.astype(BF16), wd_ref[j])

    def aligned(q, n):
        return q * n if isinstance(q, int) else pl.multiple_of(q * n, n)

    def piece(s, q):
        r = pl.ds(aligned(q, PIECE), PIECE)
        pick, wrow = one_hot_rows(s, q * PIECE, PIECE, range(n_exp))
        xc[s, r, :] = _dot(pick, h_ref[pl.ds(s * tm, tm), :]).astype(BF16)
        pc[s, :, r] = pick.T
        wc[s, r, :] = wrow
        yc[s, r, :] = jnp.zeros((PIECE, D_MODEL), BF16)

    def zero_piece(s, q):
        r = pl.ds(q * PIECE, PIECE)
        xc[s, r, :] = jnp.zeros((PIECE, D_MODEL), BF16)
        yc[s, r, :] = jnp.zeros((PIECE, D_MODEL), BF16)
        pc[s, :, r] = jnp.zeros((tm, PIECE), BF16)
        wc[s, r, :] = jnp.zeros((PIECE, 1), F32)

    def window(j, p, first):
        starts = [pl.multiple_of(jnp.minimum(off[s][j] + p * WIN, (n_pieces[s] + 1) * PIECE - WIN), 16)
                  for s in range(n_sub)]
        xg = jnp.concatenate([xc[s, pl.ds(starts[s], WIN), :] for s in range(n_sub)], axis=0)
        out = mlp(xg, j)
        for s in range(n_sub):
            r = pl.ds(starts[s], WIN)
            new = (out[s * WIN:(s + 1) * WIN] * wc[s, r, :] * gate).astype(BF16)
            if not first:
                rank = starts[s] - off[s][j] + lax.broadcasted_iota(jnp.int32, (WIN, 1), 0)
                new = jnp.where(rank < cnt[s][j], new, yc[s, r, :])
            yc[s, r, :] = new

    def scatter(s, start, n):
        r = pl.ds(start, n)
        o_ref[pl.ds(s * tm, tm), :] += _dot(pc[s, :, r], yc[s, r, :])

    @pl.when(fits)
    def _packed():
        for s in range(n_sub):
            piece(s, 0)
            piece(s, 1)
            zero_piece(s, 2)
        for s in range(n_sub):
            @pl.when(n_pieces[s] > 2)
            def _third(s=s):
                piece(s, 2)
                zero_piece(s, 3)
        for j in range(n_exp):
            window(j, 0, True)
        for j in range(n_exp):
            most = cnt[0][j]
            for s in range(1, n_sub):
                most = jnp.maximum(most, cnt[s][j])

            @pl.when(most > WIN)
            def _more(j=j, most=most):
                lax.fori_loop(1, (most + WIN - 1) // WIN, lambda p, c: (window(j, p, False), c)[1], 0)
        for s in range(n_sub):
            scatter(s, 0, 2 * PIECE)
        for s in range(n_sub):
            @pl.when(n_pieces[s] > 2)
            def _third_out(s=s):
                scatter(s, 2 * PIECE, PIECE)

    @pl.when(jnp.logical_not(fits))
    def _unpacked():
        for j in range(n_exp):
            for s in range(n_sub):
                rows = pl.ds(s * tm, tm)

                def block(bi, carry, j=j, s=s, rows=rows):
                    slot0 = off[s][j] + bi * PIECE
                    pick, wrow = one_hot_rows(s, slot0, PIECE, [j])
                    out = mlp(_dot(pick, h_ref[rows, :]).astype(BF16), j)
                    ow = (out * wrow * gate).astype(BF16)
                    o_ref[rows, :] += lax.dot_general(pick, ow, tn, preferred_element_type=F32)
                    return carry

                lax.fori_loop(0, (cnt[s][j] + PIECE - 1) // PIECE, block, 0)


def _moe(h2, x1, mod, rank_t, w_t, counts, w_gate, w_up, w_down, ws_gate, ws_up, ws_down, seq):
    T, D = h2.shape
    ns, E, tm = rank_t.shape
    tmo = min(MOE_OUTER, seq)
    n_sub = tmo // tm
    n_exp = MOE_EXPERTS_PER_STEP
    per_b = seq // tmo
    ws = [w_gate.astype(BF16), w_up.astype(BF16), w_down.astype(BF16)]
    sh = [ws_gate.astype(BF16), ws_up.astype(BF16), ws_down.astype(BF16)]
    tok = lambda: pl.BlockSpec((tmo, D), lambda t, e, c: (t, 0))
    sub = lambda: pl.BlockSpec((n_sub, E, tm), lambda t, e, c: (t, 0, 0))
    exp = lambda a: pl.BlockSpec((n_exp,) + a.shape[1:], lambda t, e, c: (e, 0, 0))
    full = lambda a: pl.BlockSpec(a.shape, lambda t, e, c: (0, 0))
    grid_spec = pltpu.PrefetchScalarGridSpec(
        num_scalar_prefetch=1,
        grid=(T // tmo, E // n_exp),
        in_specs=[tok(), tok(), pl.BlockSpec((None, N_MOD, D), lambda t, e, c: (t // per_b, 0, 0)),
                  sub(), sub(), exp(ws[0]), exp(ws[1]), exp(ws[2]), full(sh[0]), full(sh[1]), full(sh[2])],
        out_specs=tok(),
        scratch_shapes=[pltpu.VMEM((n_sub, MOE_ROWS, D), BF16), pltpu.VMEM((n_sub, MOE_ROWS, D), BF16),
                        pltpu.VMEM((n_sub, tm, MOE_ROWS), BF16), pltpu.VMEM((n_sub, MOE_ROWS, 1), F32)],
    )
    return pl.pallas_call(
        functools.partial(_moe_kernel, n_sub=n_sub, tm=tm, n_exp=n_exp),
        grid_spec=grid_spec,
        out_shape=jax.ShapeDtypeStruct((T, D), F32),
        compiler_params=_params(("arbitrary", "arbitrary")),
        name="moe",
    )(counts, h2, x1, mod, rank_t, w_t, *ws, *sh)


def kernel(x, c, w_ada, b_ada, norm_mix_gain, norm_ffn_gain, w_in, s5_lambda_re, s5_lambda_im, s5_log_dt, s5_b_re, s5_b_im, s5_c_re, s5_c_im, s5_d, s5_w_glu, s5_b_glu, q_norm_gain, k_norm_gain, cmp_pe, cmp_w1, cmp_b1, cmp_w2, cmp_b2, w_branch_a, w_branch_b, w_out, w_router, router_bias, w_gate, w_up, w_down, ws_gate, ws_up, ws_down):
    B, S, D = x.shape
    for l in range(w_ada.shape[0]):
        mod = _ada(c, w_ada[l], b_ada[l]).reshape(B, N_MOD, D)
        u, kvc, qt, ks, vst, kwn, vwt, gt, gm = _inproj(x, mod, norm_mix_gain[l:l + 1], w_in[l],
                                                        q_norm_gain[l], k_norm_gain[l])
        tables = _s5_tables(s5_lambda_re[l], s5_lambda_im[l], s5_log_dt[l], s5_b_re[l], s5_b_im[l],
                            s5_c_re[l], s5_c_im[l], s5_d[l])
        ys5 = _s5(u, tables)
        kvc_c = _compress(kvc, cmp_pe[l], cmp_w1[l], cmp_b1[l], cmp_w2[l], cmp_b2[l], k_norm_gain[l, 0])
        o_nsa = _attend(qt, ks, vst, kwn, vwt, gt, kvc_c)
        x1, h2, scores_t = _merge(x, mod, ys5, o_nsa, gm, s5_w_glu[l], s5_b_glu[l], w_branch_a[l],
                                  w_branch_b[l], w_out[l], norm_ffn_gain[l:l + 1], w_router[l])
        rank_t, w_t, cnt = _route(scores_t, router_bias[l])
        counts = cnt[:, :, 0].astype(jnp.int32).reshape(-1)
        x = _moe(h2.reshape(B * S, D), x1.reshape(B * S, D), mod, rank_t, w_t, counts,
                 w_gate[l], w_up[l], w_down[l], ws_gate[l], ws_up[l], ws_down[l], S).reshape(B, S, D)
    return x
```

```python
import functools
import math

import numpy as np
import jax
import jax.numpy as jnp
from jax import lax
from jax.experimental import pallas as pl
from jax.experimental.pallas import tpu as pltpu

F32 = jnp.float32
BF16 = jnp.bfloat16

D_MODEL = 1024
S5_WIDTH = 512
S5_GROUP = 16
S5_GROUPS = S5_WIDTH // S5_GROUP
S5_STATE = 64
N_HEADS = 8
N_KV_HEADS = 2
GQA_GROUP = N_HEADS // N_KV_HEADS
HEAD_DIM = 64
NSA_WIDTH = N_HEADS * HEAD_DIM
KV_WIDTH = 2 * N_KV_HEADS * HEAD_DIM
CMP_BLOCK = 32
CMP_STRIDE = 16
CMP_HIDDEN = 256
SEL_BLOCK = 64
SEL_TOPK = 8
WINDOW = 256
Q_BLOCK = 128
FORCE_BONUS = 1e3
N_EXPERTS = 64
TOP_K = 8
N_EXPERT_GROUPS = 8
TOPK_EXPERT_GROUPS = 4
EXPERT_HIDDEN = 256
SHARED_HIDDEN = 256
ROUTED_SCALE = 2.5
RMS_EPS = 1e-6
NEG_INF = -1e30
N_MOD = 6

LANES = 128
SUBLANES = 8
S5_CHUNK = 16
S5_CW = S5_CHUNK * S5_GROUP
ROW_TILE = 512
SEL_KEY_CHUNK = 512
SEL_UNIT = 128
SEL_SLOTS = 8
POS_BASE = 64
POS_ROWS = 16
AUG = 128
ROUTE_TILE = 256
MOE_OUTER = 1024
MOE_EXPERTS_PER_STEP = 4
MOE_PIECE = 128
MOE_WIN = 48
MOE_ROWS = 512
RANK_NONE = -float(1 << 20)
VMEM_LIMIT = 56 * 1024 * 1024


def _dot(a, b):
    return jnp.dot(a, b, preferred_element_type=F32)


def _split(a):
    hi = a.astype(BF16)
    lo = (a - hi.astype(F32)).astype(BF16)
    return hi, lo


def _dot3(a, bh, bl):
    ah, al = _split(a)
    return _dot(ah, bh) + (_dot(al, bh) + _dot(ah, bl))


def _segment_transpose(x):
    assert x.shape[-2:] == (SUBLANES, LANES) and LANES // S5_GROUP == SUBLANES
    nd = x.ndim
    i = lax.broadcasted_iota(jnp.int32, x.shape, nd - 2)
    seg = lax.broadcasted_iota(jnp.int32, x.shape, nd - 1) // S5_GROUP
    out = x
    for d in range(1, SUBLANES):
        r = pltpu.roll(pltpu.roll(x, SUBLANES - d, axis=nd - 2), S5_GROUP * d, axis=nd - 1)
        out = jnp.where(seg == ((i + d) & (SUBLANES - 1)), r, out)
    return out


def _to_chunk_major(u, put):
    rows = u.shape[0]
    u3 = u.reshape(rows // SUBLANES, SUBLANES, S5_WIDTH)
    halves = S5_CHUNK // SUBLANES
    for jb in range(S5_WIDTH // LANES):
        t = _segment_transpose(u3[:, :, LANES * jb:LANES * (jb + 1)])
        t = t.reshape(rows // S5_CHUNK, halves, SUBLANES, LANES)
        for hf in range(halves):
            put(slice(SUBLANES * jb, SUBLANES * (jb + 1)), slice(LANES * hf, LANES * (hf + 1)), t[:, hf])


def _from_chunk_major(get, rows):
    halves = S5_CHUNK // SUBLANES
    cols = []
    for jb in range(S5_WIDTH // LANES):
        parts = [_segment_transpose(get(slice(SUBLANES * jb, SUBLANES * (jb + 1)), slice(LANES * hf, LANES * (hf + 1))))
                 for hf in range(halves)]
        cols.append(jnp.stack(parts, axis=1).reshape(rows, LANES))
    return jnp.concatenate(cols, axis=1)


def _params(sem):
    return pltpu.CompilerParams(dimension_semantics=sem, vmem_limit_bytes=VMEM_LIMIT)


def _ada_kernel(c_ref, w_ref, b_ref, o_ref):
    cs = jax.nn.silu(c_ref[...])
    wh, wl = _split(w_ref[...])
    o_ref[...] = _dot3(cs, wh, wl) + b_ref[...]


def _ada(c, w_ada, b_ada):
    B, D = c.shape
    return pl.pallas_call(
        _ada_kernel,
        grid=(N_MOD,),
        in_specs=[pl.BlockSpec((B, D), lambda j: (0, 0)),
                  pl.BlockSpec((D, D), lambda j: (0, j)),
                  pl.BlockSpec((1, D), lambda j: (0, j))],
        out_specs=pl.BlockSpec((B, D), lambda j: (0, j)),
        out_shape=jax.ShapeDtypeStruct((B, N_MOD * D), F32),
        compiler_params=_params(("arbitrary",)),
        name="ada",
    )(c, w_ada, b_ada.reshape(1, N_MOD * D))


def _head_norm(v, bd, gain):
    sq = v * v
    sh, sl = _split(sq)
    ms = _dot(sh, bd) + _dot(sl, bd)
    return v * lax.rsqrt(ms + RMS_EPS) * gain


def _inproj_kernel(x_ref, mod_ref, gain_ref, wm_ref, wt_ref, wgn_ref, wgm_ref, bd_ref, qg_ref, kg_ref,
                   u_ref, kvc_ref, qt_ref, ksa_ref, vst_ref, kwa_ref, vwt_ref, gt_ref, gm_ref):
    x = x_ref[...]
    shift = mod_ref[0:1, :]
    scale = mod_ref[1:2, :]
    y = x * lax.rsqrt(jnp.mean(x * x, axis=-1, keepdims=True) + RMS_EPS)
    h = (y * gain_ref[...]) * (1.0 + scale) + shift
    hb = h.astype(BF16)
    main = _dot(hb, wm_ref[...])
    tm = x.shape[0]

    def put(groups, lanes, block):
        u_ref[:, groups, lanes] = block

    _to_chunk_major(main[:, :S5_WIDTH], put)
    o = S5_WIDTH
    kvc_ref[...] = main[:, o:o + KV_WIDTH]
    o += KV_WIDTH
    kw = N_KV_HEADS * HEAD_DIM
    bd = bd_ref[...]
    ks = _head_norm(main[:, o:o + kw], bd[:kw, :kw], kg_ref[1:2, :]).astype(BF16)
    kwn = _head_norm(main[:, o + kw:o + 2 * kw], bd[:kw, :kw], kg_ref[2:3, :]).astype(BF16)
    pos = pl.program_id(1) * tm + lax.broadcasted_iota(jnp.int32, (tm, 1), 0)
    lane = lax.broadcasted_iota(jnp.int32, (tm, AUG), 1)
    digits = jnp.where(lane == 0, pos // POS_BASE, jnp.where(lane == 1, pos % POS_BASE, 0))
    digits = digits[:, :AUG - HEAD_DIM].astype(F32).astype(BF16)
    onehot = jnp.where(lane == pos // SEL_BLOCK, 1.0, 0.0).astype(BF16)
    for hh in range(N_KV_HEADS):
        ksa_ref[hh, :, 0:HEAD_DIM] = ks[:, hh * HEAD_DIM:(hh + 1) * HEAD_DIM]
        ksa_ref[hh, :, HEAD_DIM:AUG] = digits
        ksa_ref[hh, :, AUG:] = onehot
        kwa_ref[hh, :, 0:HEAD_DIM] = kwn[:, hh * HEAD_DIM:(hh + 1) * HEAD_DIM]
        kwa_ref[hh, :, HEAD_DIM:] = digits

    nt = (((1,), (1,)), ((), ()))
    tt = lax.dot_general(wt_ref[...], hb, nt, preferred_element_type=F32)
    qt = tt[:NSA_WIDTH]
    sq = qt * qt
    sh, sl = _split(sq)
    qn = (qt * lax.rsqrt(_dot(bd, sh) + _dot(bd, sl) + RMS_EPS) * qg_ref[...] * (HEAD_DIM ** -0.5)).astype(BF16)
    QW = GQA_GROUP * Q_BLOCK
    for qb in range(tm // Q_BLOCK):
        for hd in range(N_HEADS):
            hh, g = divmod(hd, GQA_GROUP)
            qt_ref[hh, :, qb * QW + g * Q_BLOCK:qb * QW + (g + 1) * Q_BLOCK] = (
                qn[hd * HEAD_DIM:(hd + 1) * HEAD_DIM, qb * Q_BLOCK:(qb + 1) * Q_BLOCK])
    vst_ref[...] = tt[NSA_WIDTH:NSA_WIDTH + kw].reshape(N_KV_HEADS, HEAD_DIM, tm).astype(BF16)
    vwt_ref[...] = tt[NSA_WIDTH + kw:].reshape(N_KV_HEADS, HEAD_DIM, tm).astype(BF16)
    gn = jax.nn.sigmoid(lax.dot_general(wgn_ref[...], hb, nt, preferred_element_type=F32))
    for qb in range(tm // Q_BLOCK):
        for br in range(3):
            for hd in range(N_HEADS):
                hh, g = divmod(hd, GQA_GROUP)
                r = br * N_HEADS + hd
                gt_ref[hh, br:br + 1, qb * QW + g * Q_BLOCK:qb * QW + (g + 1) * Q_BLOCK] = (
                    gn[r:r + 1, qb * Q_BLOCK:(qb + 1) * Q_BLOCK])
    gm_ref[...] = jax.nn.sigmoid(_dot(hb, wgm_ref[...])).astype(BF16)


def _s5_chunk_spec(tm):
    assert S5_CHUNK == 16 and S5_GROUP == 16, "the segment transposes assume 16 steps x 16 channels"
    return pl.BlockSpec((tm // S5_CHUNK, None, S5_GROUPS, S5_CW), lambda b, i: (i, b, 0, 0))


def _inproj(x, mod, gain, w_in, q_gain, k_gain):
    B, S, D = x.shape
    tm = min(ROW_TILE, S)
    assert S // SEL_BLOCK <= AUG and tm % Q_BLOCK == 0
    kw = N_KV_HEADS * HEAD_DIM
    cols = np.cumsum((0,) + (S5_WIDTH, NSA_WIDTH, KV_WIDTH, KV_WIDTH, KV_WIDTH, 3 * N_HEADS, 2 * D))
    c_u, c_q, c_kvc, c_kvs, c_kvw, c_gn, c_gm = cols[:7]
    sl = lambda a, n: w_in[:, a:a + n]
    wm = jnp.concatenate([sl(c_u, S5_WIDTH), sl(c_kvc, KV_WIDTH), sl(c_kvs, kw), sl(c_kvw, kw)], axis=1).astype(BF16)
    wt = jnp.concatenate([sl(c_q, NSA_WIDTH), sl(c_kvs + kw, kw), sl(c_kvw + kw, kw)], axis=1).T.astype(BF16)
    wgn = sl(c_gn, 3 * N_HEADS).T.astype(BF16)
    wgm = sl(c_gm, 2 * D).astype(BF16)
    seg = np.arange(NSA_WIDTH) // HEAD_DIM
    bd = jnp.asarray((seg[:, None] == seg[None, :]).astype(np.float32) / HEAD_DIM, BF16)
    qg = jnp.tile(q_gain, N_HEADS).reshape(NSA_WIDTH, 1)
    kg = jnp.tile(k_gain, (1, N_KV_HEADS))
    nq = tm // Q_BLOCK * GQA_GROUP * Q_BLOCK
    row = lambda w: pl.BlockSpec((None, tm, w), lambda b, i: (b, i, 0))
    full = lambda a: pl.BlockSpec(a.shape, lambda b, i: (0,) * a.ndim)
    rows4 = lambda w: pl.BlockSpec((None, N_KV_HEADS, tm, w), lambda b, i: (b, 0, i, 0))
    cols4 = lambda r, w: pl.BlockSpec((None, N_KV_HEADS, r, w), lambda b, i: (b, 0, 0, i))
    nqt = S // Q_BLOCK * GQA_GROUP * Q_BLOCK
    return pl.pallas_call(
        _inproj_kernel,
        grid=(B, S // tm),
        in_specs=[row(D), pl.BlockSpec((None, N_MOD, D), lambda b, i: (b, 0, 0)),
                  full(gain), full(wm), full(wt), full(wgn), full(wgm), full(bd), full(qg), full(kg)],
        out_specs=[_s5_chunk_spec(tm), row(KV_WIDTH), cols4(HEAD_DIM, nq), rows4(2 * AUG), cols4(HEAD_DIM, tm),
                   rows4(AUG), cols4(HEAD_DIM, tm), cols4(3, nq), row(2 * D)],
        out_shape=[jax.ShapeDtypeStruct((S // S5_CHUNK, B, S5_GROUPS, S5_CW), F32),
                   jax.ShapeDtypeStruct((B, S, KV_WIDTH), F32),
                   jax.ShapeDtypeStruct((B, N_KV_HEADS, HEAD_DIM, nqt), BF16),
                   jax.ShapeDtypeStruct((B, N_KV_HEADS, S, 2 * AUG), BF16),
                   jax.ShapeDtypeStruct((B, N_KV_HEADS, HEAD_DIM, S), BF16),
                   jax.ShapeDtypeStruct((B, N_KV_HEADS, S, AUG), BF16),
                   jax.ShapeDtypeStruct((B, N_KV_HEADS, HEAD_DIM, S), BF16),
                   jax.ShapeDtypeStruct((B, N_KV_HEADS, 3, nqt), F32),
                   jax.ShapeDtypeStruct((B, S, 2 * D), BF16)],
        compiler_params=_params(("arbitrary", "arbitrary")),
        name="inproj",
    )(x, mod, gain, wm, wt, wgn, wgm, bd, qg, kg)


def _s5_tables(lam_re, lam_im, log_dt, b_re, b_im, c_re, c_im, d_skip):
    L, P, N, G = S5_CHUNK, S5_GROUP, S5_STATE, S5_GROUPS
    hp = lax.Precision.HIGHEST
    lr, li = lam_re.astype(F32), lam_im.astype(F32)
    dt = jnp.exp(log_dt.astype(F32))[:, None]
    mag = jnp.exp(lr * dt)
    abar_re, abar_im = mag * jnp.cos(li * dt), mag * jnp.sin(li * dt)
    num_re, num_im = abar_re - 1.0, abar_im
    den = lr * lr + li * li
    coef_re = (num_re * lr + num_im * li) / den
    coef_im = (num_im * lr - num_re * li) / den
    br, bi = b_re.astype(F32), b_im.astype(F32)
    bbar_re = coef_re[..., None] * br - coef_im[..., None] * bi
    bbar_im = coef_re[..., None] * bi + coef_im[..., None] * br
    k = jnp.arange(L + 1, dtype=F32)[:, None, None]
    pmag = jnp.exp(lr * dt * k)
    pre, pim = pmag * jnp.cos(li * dt * k), pmag * jnp.sin(li * dt * k)
    cr, ci = c_re.astype(F32), c_im.astype(F32)
    ca_re = cr[None] * pre[:, :, None, :] - ci[None] * pim[:, :, None, :]
    ca_im = cr[None] * pim[:, :, None, :] + ci[None] * pre[:, :, None, :]
    kern = (jnp.einsum('kgpn,gnq->gkqp', ca_re[:L], bbar_re, precision=hp)
            - jnp.einsum('kgpn,gnq->gkqp', ca_im[:L], bbar_im, precision=hp))
    s_i = np.arange(L)[:, None]
    t_i = np.arange(L)[None, :]
    tau = np.clip(t_i - s_i, 0, L - 1)
    causal = jnp.asarray((t_i >= s_i).astype(np.float32))
    mt = kern[:, tau] * causal[None, :, :, None, None]
    mt = mt.transpose(0, 1, 3, 2, 4).reshape(G, L * P, L * P)
    rev = np.arange(L - 1, -1, -1)
    ab_re = pre[rev][..., None] * bbar_re[None] - pim[rev][..., None] * bbar_im[None]
    ab_im = pre[rev][..., None] * bbar_im[None] + pim[rev][..., None] * bbar_re[None]
    ws = jnp.concatenate([ab_re, ab_im], axis=2)
    ws = ws.transpose(1, 0, 3, 2).reshape(G, L * P, 2 * N)
    wo = jnp.concatenate([ca_re[1:], -ca_im[1:]], axis=3)
    wo = wo.transpose(1, 3, 0, 2).reshape(G, 2 * N, L * P)
    al = jnp.stack([pre[L], pim[L]], axis=1)
    dv = jnp.tile(d_skip.astype(F32), (1, L)).reshape(G, 1, L * P)
    return mt, ws, wo, al, dv


def _s5_kernel(u_ref, mth_ref, mtl_ref, wsh_ref, wsl_ref, woh_ref, wol_ref, al_ref, dv_ref,
               y_ref, vr_scr, vi_scr, xr_scr, xi_scr, *, n_chunks, bsz):
    N = S5_STATE
    u = u_ref[...]
    uh, ul = _split(u)
    y = _dot(uh, mth_ref[...]) + (_dot(ul, mth_ref[...]) + _dot(uh, mtl_ref[...]))
    v = _dot(uh, wsh_ref[...]) + (_dot(ul, wsh_ref[...]) + _dot(uh, wsl_ref[...]))
    vr_scr[...] = v[:, :N]
    vi_scr[...] = v[:, N:]
    a_r = al_ref[0:1, :]
    a_i = al_ref[1:2, :]

    def step(c, x):
        xr, xi = x
        r = pl.ds(pl.multiple_of(c * bsz, bsz), bsz)
        xr_scr[r, :] = xr
        xi_scr[r, :] = xi
        return (a_r * xr - a_i * xi + vr_scr[r, :], a_r * xi + a_i * xr + vi_scr[r, :])

    zero = jnp.zeros((bsz, N), F32)
    lax.fori_loop(0, n_chunks, step, (zero, zero), unroll=8)
    xp = jnp.concatenate([xr_scr[...], xi_scr[...]], axis=1)
    y = y + _dot3(xp, woh_ref[...], wol_ref[...])
    y_ref[...] = y + dv_ref[...] * u


def _s5(ug, tables):
    nc, B, G, cw = ug.shape
    N = S5_STATE
    mt, ws, wo, al, dv = tables
    mth, mtl = _split(mt)
    wsh, wsl = _split(ws)
    woh, wol = _split(wo)
    grp = lambda a: pl.BlockSpec((None,) + a.shape[1:], lambda g: (g,) + (0,) * (a.ndim - 1))
    ugt = ug.reshape(nc * B, G, cw).transpose(1, 0, 2)
    y = pl.pallas_call(
        functools.partial(_s5_kernel, n_chunks=nc, bsz=B),
        grid=(G,),
        in_specs=[grp(ugt), grp(mth), grp(mtl), grp(wsh), grp(wsl), grp(woh), grp(wol), grp(al), grp(dv)],
        out_specs=grp(ugt),
        out_shape=jax.ShapeDtypeStruct(ugt.shape, F32),
        scratch_shapes=[pltpu.VMEM((nc * B, N), F32)] * 4,
        compiler_params=_params(("arbitrary",)),
        name="s5",
    )(ugt, mth, mtl, wsh, wsl, woh, wol, al, dv)
    return y.transpose(1, 0, 2).reshape(nc, B, G, cw)


def _compress_kernel(x_ref, pe_ref, w1_ref, b1_ref, w2_ref, b2_ref, kg_ref, o_ref, *, n_rows):
    x = x_ref[...]
    half = CMP_STRIDE * HEAD_DIM
    a = _dot((x + pe_ref[0:1, :]).astype(BF16), w1_ref[:half, :])
    b = _dot((x + pe_ref[1:2, :]).astype(BF16), w1_ref[half:, :])
    hid = jax.nn.gelu(a + pltpu.roll(b, n_rows - 1, axis=0) + b1_ref[...])
    out = _dot(hid.astype(BF16), w2_ref[...]) + b2_ref[...]
    normed = out * lax.rsqrt(jnp.mean(out * out, axis=-1, keepdims=True) + RMS_EPS) * kg_ref[...]
    is_key = pl.program_id(1) < N_KV_HEADS
    o_ref[...] = jnp.where(is_key, normed, out).astype(BF16)


def _compress(kv_c, cmp_pe, cmp_w1, cmp_b1, cmp_w2, cmp_b2, k_gain0):
    B, S, _ = kv_c.shape
    nr = S // CMP_STRIDE
    half = CMP_STRIDE * HEAD_DIM
    nj = 2 * N_KV_HEADS
    xc = kv_c.reshape(B, nr, CMP_STRIDE, nj, HEAD_DIM).transpose(0, 3, 1, 2, 4).reshape(B, nj, nr, half)
    pe = cmp_pe.reshape(2, 2, half)
    w1 = cmp_w1.astype(BF16)
    w2 = cmp_w2.astype(BF16)
    b1 = cmp_b1.reshape(2, 1, CMP_HIDDEN)
    b2 = cmp_b2.reshape(2, 1, HEAD_DIM)
    kv = lambda a: pl.BlockSpec((None,) + a.shape[1:], lambda b, j: (j // N_KV_HEADS,) + (0,) * (a.ndim - 1))
    return pl.pallas_call(
        functools.partial(_compress_kernel, n_rows=nr),
        grid=(B, nj),
        in_specs=[pl.BlockSpec((None, None, nr, half), lambda b, j: (b, j, 0, 0)),
                  kv(pe), kv(w1), kv(b1), kv(w2), kv(b2),
                  pl.BlockSpec((1, HEAD_DIM), lambda b, j: (0, 0))],
        out_specs=pl.BlockSpec((None, None, nr, HEAD_DIM), lambda b, j: (b, j, 0, 0)),
        out_shape=jax.ShapeDtypeStruct((B, nj, nr, HEAD_DIM), BF16),
        compiler_params=_params(("arbitrary", "arbitrary")),
        name="compress",
    )(xc, pe, w1, b1, w2, b2, k_gain0.reshape(1, HEAD_DIM))


def _softmax_cols(s, mask):
    s = jnp.where(mask, s, NEG_INF)
    m = jnp.max(s, axis=0, keepdims=True)
    p = jnp.where(mask, jnp.exp(s - m), 0.0)
    return p / jnp.maximum(jnp.sum(p, axis=0, keepdims=True), 1e-20)


def _attend_kernel(qt_ref, kc_ref, vct_ref, ks_ref, vst_ref, kw_ref, vwt_ref, g_ref, slope_ref, ovt_ref,
                   o_ref, qa_scr, sa_scr, sb_scr, *, n_sel, n_pick, n_cmp_rows):
    QW = GQA_GROUP * Q_BLOCK
    i = pl.program_id(2)
    t0 = i * Q_BLOCK
    slope = slope_ref[...]
    tq = t0 + (lax.broadcasted_iota(jnp.int32, (1, QW), 1) & (Q_BLOCK - 1))

    r = lax.broadcasted_iota(jnp.int32, (POS_ROWS, QW), 0)
    qa_scr[0:HEAD_DIM, :] = qt_ref[...]
    qa_scr[HEAD_DIM:HEAD_DIM + POS_ROWS, :] = jnp.where(
        r == 0, slope * POS_BASE, jnp.where(r == 1, slope, 0.0)).astype(BF16)
    qa_scr[HEAD_DIM + POS_ROWS:AUG, :] = jnp.zeros((AUG - HEAD_DIM - POS_ROWS, QW), BF16)
    qa = qa_scr[0:AUG, :]

    sc = _dot(kc_ref[...], qa)
    cpos = lax.broadcasted_iota(jnp.int32, (n_cmp_rows, QW), 0) * CMP_STRIDE + (CMP_BLOCK - 1)
    p_c = _softmax_cols(sc, cpos <= tq)
    o_c = _dot(vct_ref[...], p_c.astype(BF16))
    psum = p_c[:, 0:Q_BLOCK]
    for g in range(1, GQA_GROUP):
        psum = psum + p_c[:, g * Q_BLOCK:(g + 1) * Q_BLOCK]
    ph, pl_ = _split(psum)
    imp = _dot(ovt_ref[...], ph) + _dot(ovt_ref[...], pl_)

    jb = lax.broadcasted_iota(jnp.int32, (n_sel, Q_BLOCK), 0)
    cur = (t0 + lax.broadcasted_iota(jnp.int32, (1, Q_BLOCK), 1)) // SEL_BLOCK
    forced = (jb == 0) | (jb == cur) | (jb == cur - 1)
    imp = jnp.where(forced, imp + FORCE_BONUS, imp)
    imp = jnp.where(jb <= cur, imp, -1.0)
    bias = jnp.full((n_sel, Q_BLOCK), NEG_INF, F32)
    for _ in range(n_pick):
        m = jnp.max(imp, axis=0, keepdims=True)
        first = jnp.min(jnp.where(imp == m, jb, n_sel), axis=0, keepdims=True)
        hit = jb == first
        bias = jnp.where(hit, 0.0, bias)
        imp = jnp.where(hit, -jnp.inf, imp)
    per_unit = SEL_UNIT // SEL_BLOCK
    n_units = n_sel // per_unit
    picked = jnp.where(bias == 0.0, 1.0, 0.0).astype(BF16)
    in_unit = (lax.broadcasted_iota(jnp.int32, (n_units, n_sel), 1) // per_unit
               == lax.broadcasted_iota(jnp.int32, (n_units, n_sel), 0))
    hits = _dot(jnp.where(in_unit, 1.0, 0.0).astype(BF16), picked)
    live = jnp.where(jnp.max(hits, axis=1, keepdims=True) > 0.5, 1.0, 0.0)
    live = jnp.broadcast_to(live, (n_units, LANES))
    earlier = (lax.broadcasted_iota(jnp.int32, (n_units, n_units), 1)
               < lax.broadcasted_iota(jnp.int32, (n_units, n_units), 0))
    before = _dot(jnp.where(earlier, 1.0, 0.0).astype(BF16), live.astype(BF16))
    slot = lax.broadcasted_iota(jnp.int32, (n_units, LANES), 1).astype(F32)
    in_slot = jnp.where(before == slot, live, 0.0).astype(BF16)
    r8 = lax.broadcasted_iota(jnp.int32, (SUBLANES, n_units), 0)
    c8 = lax.broadcasted_iota(jnp.int32, (SUBLANES, n_units), 1)
    pick_rows = jnp.where(r8 == 0, c8, jnp.where(r8 == 1, 1, 0)).astype(F32).astype(BF16)
    ids = _dot(pick_rows, in_slot)
    n_live = (before[n_units - 1:n_units, 0:1] + live[n_units - 1:n_units, 0:1])[0, 0]

    bias = jnp.concatenate([bias] * GQA_GROUP, axis=1).astype(BF16)
    qa_scr[AUG:, :] = jnp.concatenate([bias, jnp.zeros((AUG - n_sel, QW), BF16)], axis=0)

    def gathered(_):
        units = [ids[0, s].astype(jnp.int32) for s in range(SEL_SLOTS)]
        starts = [pl.multiple_of(u * SEL_UNIT, SEL_UNIT) for u in units]
        keys = jnp.concatenate([ks_ref[pl.ds(st, SEL_UNIT), :] for st in starts], axis=0)
        s = _dot(keys, qa_scr[...])
        krow = lax.broadcasted_iota(jnp.int32, (SEL_UNIT, QW), 0)
        seen = jnp.concatenate(
            [st + krow <= jnp.where(ids[1, si] > 0.5, tq, -1) for si, st in enumerate(starts)], axis=0)
        s = jnp.where(seen, s, NEG_INF)
        p = jnp.exp(s - jnp.max(s, axis=0, keepdims=True))
        vals = jnp.concatenate([vst_ref[:, pl.ds(st, SEL_UNIT)] for st in starts], axis=1)
        return _dot(vals, p.astype(BF16)) / jnp.maximum(jnp.sum(p, axis=0, keepdims=True), 1e-20)

    KC = SEL_KEY_CHUNK
    n_chunks = (t0 + Q_BLOCK + KC - 1) // KC

    def scores_to(ref, j):
        k0 = pl.multiple_of(j * KC, KC)
        ref[...] = _dot(ks_ref[pl.ds(k0, KC), :], qa_scr[...])

    def accumulate(carry, j, s, p_of):
        m, l, acc = carry
        k0 = pl.multiple_of(j * KC, KC)
        m_new = jnp.maximum(m, jnp.max(s, axis=0, keepdims=True))
        alpha = jnp.exp(m - m_new)
        p = p_of(jnp.exp(s - m_new))
        l = alpha * l + jnp.sum(p, axis=0, keepdims=True)
        acc = alpha * acc + _dot(vst_ref[:, pl.ds(k0, KC)], p.astype(BF16))
        return m_new, l, acc

    keep = lambda p: p
    last = n_chunks - 1

    def two_chunks(i, state):
        j = 2 * i
        scores_to(sb_scr, j + 1)
        state = accumulate(state, j, sa_scr[...], keep)
        scores_to(sa_scr, j + 2)
        return accumulate(state, j + 1, sb_scr[...], keep)

    def odd_chunk(state):
        state = accumulate(state, last - 1, sa_scr[...], keep)
        scores_to(sa_scr, last)
        return state

    def swept(_):
        scores_to(sa_scr, 0)
        init = (jnp.full((1, QW), NEG_INF, F32), jnp.zeros((1, QW), F32), jnp.zeros((HEAD_DIM, QW), F32))
        state = lax.fori_loop(0, last // 2, two_chunks, init)
        state = lax.cond(last % 2 == 1, odd_chunk, lambda st: st, state)
        visible = (last * KC + lax.broadcasted_iota(jnp.int32, (KC, QW), 0)) <= tq
        _, l_s, acc_s = accumulate(state, last, jnp.where(visible, sa_scr[...], NEG_INF),
                                   lambda p: jnp.where(visible, p, 0.0))
        return acc_s / jnp.maximum(l_s, 1e-20)

    WK = WINDOW + Q_BLOCK
    w0 = pl.multiple_of(jnp.maximum(t0 - WINDOW, 0), Q_BLOCK)
    sw = _dot(kw_ref[pl.ds(w0, WK), :], qa)
    dist = tq - (w0 + lax.broadcasted_iota(jnp.int32, (WK, QW), 0))
    in_win = (dist | (WINDOW - 1 - dist)) >= 0
    p_w = _softmax_cols(sw, in_win)
    o_w = _dot(vwt_ref[:, pl.ds(w0, WK)], p_w.astype(BF16))

    o_s = lax.cond(n_live <= SEL_SLOTS, gathered, swept, 0)

    o = g_ref[0:1, :] * o_c + g_ref[1:2, :] * o_s + g_ref[2:3, :] * o_w
    pairs = []
    for g in range(0, GQA_GROUP, 2):
        sq = jnp.concatenate([o[:, g * Q_BLOCK:(g + 1) * Q_BLOCK], o[:, (g + 1) * Q_BLOCK:(g + 2) * Q_BLOCK]], axis=0)
        pairs.append(sq.T)
    o_ref[...] = jnp.concatenate(pairs, axis=1).astype(BF16)


def _attend(qt, ks, vst, kwn, vwt, gt, kvc_c):
    B, _, S, _ = ks.shape
    nqb = S // Q_BLOCK
    QW = GQA_GROUP * Q_BLOCK
    n_sel = S // SEL_BLOCK
    n_pick = min(SEL_TOPK, n_sel)
    nr = S // CMP_STRIDE
    n_cmp = (S - CMP_BLOCK) // CMP_STRIDE + 1
    assert n_sel % 16 == 0 and n_sel <= AUG

    cpos = np.arange(nr) * CMP_STRIDE + CMP_BLOCK - 1
    digits = np.zeros((nr, AUG - HEAD_DIM), np.float32)
    digits[:, 0] = cpos // POS_BASE
    digits[:, 1] = cpos % POS_BASE
    kc = kvc_c[:, :N_KV_HEADS]
    kc = jnp.concatenate([kc, jnp.broadcast_to(jnp.asarray(digits, BF16), kc.shape[:2] + digits.shape)], axis=-1)
    vct = kvc_c[:, N_KV_HEADS:].transpose(0, 1, 3, 2)
    slopes = 2.0 ** (-8.0 * np.arange(1, N_HEADS + 1) / N_HEADS)
    slope_t = jnp.asarray(np.repeat(slopes.reshape(N_KV_HEADS, GQA_GROUP), Q_BLOCK, axis=1)
                          .reshape(N_KV_HEADS, 1, QW), F32)
    cmp_start = np.arange(n_cmp) * CMP_STRIDE
    sel_start = np.arange(n_sel) * SEL_BLOCK
    ov = ((cmp_start[:, None] <= sel_start[None, :] + SEL_BLOCK - 1)
          & (cmp_start[:, None] + CMP_BLOCK - 1 >= sel_start[None, :])).astype(np.float32)
    ovt = np.zeros((n_sel, nr), np.float32)
    ovt[:, :n_cmp] = ov.T
    ovt = jnp.asarray(ovt, BF16)

    per_bh = lambda r, c: pl.BlockSpec((None, None, r, c), lambda b, h, i: (b, h, 0, 0))
    per_q = lambda r: pl.BlockSpec((None, None, r, QW), lambda b, h, i: (b, h, 0, i))
    return pl.pallas_call(
        functools.partial(_attend_kernel, n_sel=n_sel, n_pick=n_pick, n_cmp_rows=nr),
        grid=(B, N_KV_HEADS, nqb),
        in_specs=[per_q(HEAD_DIM), per_bh(nr, AUG), per_bh(HEAD_DIM, nr),
                  per_bh(S, 2 * AUG), per_bh(HEAD_DIM, S), per_bh(S, AUG), per_bh(HEAD_DIM, S),
                  per_q(3),
                  pl.BlockSpec((None, 1, QW), lambda b, h, i: (h, 0, 0)),
                  pl.BlockSpec((n_sel, nr), lambda b, h, i: (0, 0))],
        out_specs=pl.BlockSpec((None, Q_BLOCK, GQA_GROUP * HEAD_DIM), lambda b, h, i: (b, i, h)),
        out_shape=jax.ShapeDtypeStruct((B, S, NSA_WIDTH), BF16),
        scratch_shapes=[pltpu.VMEM((2 * AUG, QW), BF16), pltpu.VMEM((SEL_KEY_CHUNK, QW), F32),
                        pltpu.VMEM((SEL_KEY_CHUNK, QW), F32)],
        compiler_params=_params(("arbitrary", "arbitrary", "arbitrary")),
        name="attend",
    )(qt, kc, vct, ks, vst, kwn, vwt, gt, slope_t, ovt)


def _merge_kernel(x_ref, mod_ref, ys_ref, on_ref, gm_ref, wglu_ref, bglu_ref, wa_ref, wb_ref, wo_ref,
                  gain_ref, wrh_ref, wrl_ref, x1_ref, h2_ref, sc_ref):
    D = D_MODEL
    z = jax.nn.gelu(_from_chunk_major(lambda groups, lanes: ys_ref[:, groups, lanes], x_ref.shape[0]))
    glu = z * jax.nn.sigmoid(_dot(z.astype(BF16), wglu_ref[...]) + bglu_ref[...])
    ya = _dot(glu.astype(BF16), wa_ref[...])
    yb = _dot(on_ref[...], wb_ref[...])
    merged = gm_ref[:, :D].astype(F32) * ya + gm_ref[:, D:].astype(F32) * yb
    x1 = x_ref[...] + mod_ref[2:3, :] * _dot(merged.astype(BF16), wo_ref[...])
    x1_ref[...] = x1
    y = x1 * lax.rsqrt(jnp.mean(x1 * x1, axis=-1, keepdims=True) + RMS_EPS)
    h2 = (y * gain_ref[...]) * (1.0 + mod_ref[4:5, :]) + mod_ref[3:4, :]
    h2_ref[...] = h2.astype(BF16)
    hh, hl = _split(h2)
    nt = (((1,), (1,)), ((), ()))
    dg = lambda a, b: lax.dot_general(a, b, nt, preferred_element_type=F32)
    logits = dg(wrh_ref[...], hh) + (dg(wrh_ref[...], hl) + dg(wrl_ref[...], hh))
    sc_ref[...] = jax.nn.sigmoid(logits)


def _merge(x, mod, ys5, o_nsa, gm, w_glu, b_glu, w_a, w_b, w_out, gain_f, w_router):
    B, S, D = x.shape
    tm = min(ROW_TILE, S)
    wrh, wrl = _split(w_router.T)
    ws = [w_glu.astype(BF16), b_glu.reshape(1, -1), w_a.astype(BF16), w_b.astype(BF16), w_out.astype(BF16),
          gain_f, wrh, wrl]
    row = lambda w: pl.BlockSpec((None, tm, w), lambda b, i: (b, i, 0))
    full = lambda a: pl.BlockSpec(a.shape, lambda b, i: (0,) * a.ndim)
    return pl.pallas_call(
        _merge_kernel,
        grid=(B, S // tm),
        in_specs=[row(D), pl.BlockSpec((None, N_MOD, D), lambda b, i: (b, 0, 0)),
                  _s5_chunk_spec(tm), row(NSA_WIDTH), row(2 * D)] + [full(a) for a in ws],
        out_specs=[row(D), row(D), pl.BlockSpec((None, N_EXPERTS, tm), lambda b, i: (b, 0, i))],
        out_shape=[jax.ShapeDtypeStruct((B, S, D), F32), jax.ShapeDtypeStruct((B, S, D), BF16),
                   jax.ShapeDtypeStruct((B, N_EXPERTS, S), F32)],
        compiler_params=_params(("arbitrary", "arbitrary")),
        name="merge",
    )(x, mod, ys5, o_nsa, gm, *ws)


def _first_argmax_rows(v, idx, n):
    m = jnp.max(v, axis=0, keepdims=True)
    first = jnp.min(jnp.where(v == m, idx, n), axis=0, keepdims=True)
    return idx == first, m


def _route_kernel(sc_ref, bias_ref, tri_ref, rank_ref, w_ref, cnt_ref):
    E, NG = N_EXPERTS, N_EXPERT_GROUPS
    GS = E // NG
    sc = sc_ref[...]
    TM = sc.shape[1]
    sel = sc + bias_ref[...]
    i8 = lax.broadcasted_iota(jnp.int32, (GS, TM), 0)
    gscore = []
    for g in range(NG):
        blk = sel[g * GS:(g + 1) * GS, :]
        hit, m1 = _first_argmax_rows(blk, i8, GS)
        m2 = jnp.max(jnp.where(hit, -jnp.inf, blk), axis=0, keepdims=True)
        gscore.append(m1 + m2)
    gscore = jnp.concatenate(gscore, axis=0)
    ig = lax.broadcasted_iota(jnp.int32, (NG, TM), 0)
    gmask = jnp.zeros((NG, TM), F32)
    for _ in range(TOPK_EXPERT_GROUPS):
        hit, _m = _first_argmax_rows(gscore, ig, NG)
        gmask = jnp.where(hit, 1.0, gmask)
        gscore = jnp.where(hit, -jnp.inf, gscore)
    emask = jnp.concatenate([jnp.broadcast_to(gmask[g:g + 1, :], (GS, TM)) for g in range(NG)], axis=0)
    cand = jnp.where(emask > 0.5, sel, NEG_INF)
    ie = lax.broadcasted_iota(jnp.int32, (E, TM), 0)
    chosen = jnp.zeros((E, TM), F32)
    for _ in range(TOP_K):
        hit, _m = _first_argmax_rows(cand, ie, E)
        chosen = jnp.where(hit, 1.0, chosen)
        cand = jnp.where(hit, -jnp.inf, cand)
    w = chosen * sc
    w = w / jnp.sum(w, axis=0, keepdims=True) * ROUTED_SCALE
    sub = rank_ref.shape[-1]
    for q in range(TM // sub):
        cols = slice(q * sub, (q + 1) * sub)
        cb = chosen[:, cols].astype(BF16)
        prefix = _dot(cb, tri_ref[...])
        rank_ref[q] = jnp.where(chosen[:, cols] > 0.5, prefix, RANK_NONE)
        w_ref[q] = w[:, cols]
        cnt_ref[q] = _dot(cb, jnp.ones((sub, LANES), BF16))


def _route(scores_t, router_bias):
    B, E, S = scores_t.shape
    TM = min(ROUTE_TILE, S)
    per_step = min(MOE_OUTER, S) // TM
    nj = S // (TM * per_step)
    ns = B * nj * per_step
    tri = jnp.asarray(np.triu(np.ones((TM, TM), np.float32), k=1), BF16)
    tile = lambda w: pl.BlockSpec((per_step, E, w), lambda b, j: (b * nj + j, 0, 0))
    return pl.pallas_call(
        _route_kernel,
        grid=(B, nj),
        in_specs=[pl.BlockSpec((None, E, TM * per_step), lambda b, j: (b, 0, j)),
                  pl.BlockSpec((E, 1), lambda b, j: (0, 0)),
                  pl.BlockSpec((TM, TM), lambda b, j: (0, 0))],
        out_specs=[tile(TM), tile(TM), tile(LANES)],
        out_shape=[jax.ShapeDtypeStruct((ns, E, TM), F32), jax.ShapeDtypeStruct((ns, E, TM), F32),
                   jax.ShapeDtypeStruct((ns, E, LANES), F32)],
        compiler_params=_params(("arbitrary", "arbitrary")),
        name="route",
    )(scores_t, router_bias.reshape(E, 1), tri)


def _moe_kernel(cnt_ref, h_ref, x1_ref, mod_ref, rank_ref, w_ref, wg_ref, wu_ref, wd_ref,
                sg_ref, su_ref, sd_ref, o_ref, xc, yc, pc, wc, *, n_sub, tm, n_exp):
    to = pl.program_id(0)
    eb = pl.program_id(1)
    gate = mod_ref[5:6, :]

    @pl.when(eb == 0)
    def _shared():
        for s in range(n_sub):
            rows = pl.ds(s * tm, tm)
            hs = h_ref[rows, :]
            hid = jax.nn.silu(_dot(hs, sg_ref[...])) * _dot(hs, su_ref[...])
            o_ref[rows, :] = x1_ref[rows, :] + gate * _dot(hid.astype(BF16), sd_ref[...])

    PIECE, WIN = MOE_PIECE, MOE_WIN
    tn = (((0,), (0,)), ((), ()))
    experts = [eb * n_exp + j for j in range(n_exp)]
    cnt = [[cnt_ref[(to * n_sub + s) * N_EXPERTS + e] for e in experts] for s in range(n_sub)]
    off = []
    for s in range(n_sub):
        o = [jnp.int32(0)]
        for j in range(n_exp):
            o.append(o[-1] + ((cnt[s][j] + 15) // 16) * 16)
        off.append(o)
    n_pieces = [(off[s][n_exp] + PIECE - 1) // PIECE for s in range(n_sub)]
    fits = n_pieces[0] < MOE_ROWS // PIECE
    for s in range(1, n_sub):
        fits = jnp.logical_and(fits, n_pieces[s] < MOE_ROWS // PIECE)

    def one_hot_rows(s, base, rows_n, js):
        slot = (lax.broadcasted_iota(jnp.int32, (rows_n, tm), 0) + base).astype(F32)
        pick = jnp.zeros((rows_n, tm), F32)
        wacc = jnp.zeros((rows_n, tm), F32)
        for j in js:
            hit = (slot - off[s][j].astype(F32)) == rank_ref[s, pl.ds(experts[j], 1), :]
            pick = jnp.where(hit, 1.0, pick)
            wacc = jnp.where(hit, w_ref[s, pl.ds(experts[j], 1), :], wacc)
        return pick.astype(BF16), jnp.sum(wacc, axis=1, keepdims=True)

    def mlp(xg, j):
        hid = jax.nn.silu(_dot(xg, wg_ref[j])) * _dot(xg, wu_ref[j])
        return _dot(hid.astype(BF16), wd_ref[j])

    def aligned(q, n):
        return q * n if isinstance(q, int) else pl.multiple_of(q * n, n)

    def piece(s, q):
        r = pl.ds(aligned(q, PIECE), PIECE)
        pick, wrow = one_hot_rows(s, q * PIECE, PIECE, range(n_exp))
        xc[s, r, :] = _dot(pick, h_ref[pl.ds(s * tm, tm), :]).astype(BF16)
        pc[s, :, r] = pick.T
        wc[s, r, :] = wrow
        yc[s, r, :] = jnp.zeros((PIECE, D_MODEL), BF16)

    def zero_piece(s, q):
        r = pl.ds(q * PIECE, PIECE)
        xc[s, r, :] = jnp.zeros((PIECE, D_MODEL), BF16)
        yc[s, r, :] = jnp.zeros((PIECE, D_MODEL), BF16)
        pc[s, :, r] = jnp.zeros((tm, PIECE), BF16)
        wc[s, r, :] = jnp.zeros((PIECE, 1), F32)

    def window(j, p, first):
        starts = [pl.multiple_of(jnp.minimum(off[s][j] + p * WIN, (n_pieces[s] + 1) * PIECE - WIN), 16)
                  for s in range(n_sub)]
        xg = jnp.concatenate([xc[s, pl.ds(starts[s], WIN), :] for s in range(n_sub)], axis=0)
        out = mlp(xg, j)
        for s in range(n_sub):
            r = pl.ds(starts[s], WIN)
            new = (out[s * WIN:(s + 1) * WIN] * wc[s, r, :] * gate).astype(BF16)
            if not first:
                rank = starts[s] - off[s][j] + lax.broadcasted_iota(jnp.int32, (WIN, 1), 0)
                new = jnp.where(rank < cnt[s][j], new, yc[s, r, :])
            yc[s, r, :] = new

    def scatter(s, start, n):
        r = pl.ds(start, n)
        o_ref[pl.ds(s * tm, tm), :] += _dot(pc[s, :, r], yc[s, r, :])

    @pl.when(fits)
    def _packed():
        for s in range(n_sub):
            piece(s, 0)
            piece(s, 1)
            zero_piece(s, 2)
        for s in range(n_sub):
            @pl.when(n_pieces[s] > 2)
            def _third(s=s):
                piece(s, 2)
                zero_piece(s, 3)
        for j in range(n_exp):
            window(j, 0, True)
        for j in range(n_exp):
            most = cnt[0][j]
            for s in range(1, n_sub):
                most = jnp.maximum(most, cnt[s][j])

            @pl.when(most > WIN)
            def _more(j=j, most=most):
                lax.fori_loop(1, (most + WIN - 1) // WIN, lambda p, c: (window(j, p, False), c)[1], 0)
        for s in range(n_sub):
            scatter(s, 0, 2 * PIECE)
        for s in range(n_sub):
            @pl.when(n_pieces[s] > 2)
            def _third_out(s=s):
                scatter(s, 2 * PIECE, PIECE)

    @pl.when(jnp.logical_not(fits))
    def _unpacked():
        for j in range(n_exp):
            for s in range(n_sub):
                rows = pl.ds(s * tm, tm)

                def block(bi, carry, j=j, s=s, rows=rows):
                    slot0 = off[s][j] + bi * PIECE
                    pick, wrow = one_hot_rows(s, slot0, PIECE, [j])
                    out = mlp(_dot(pick, h_ref[rows, :]).astype(BF16), j)
                    ow = (out * wrow * gate).astype(BF16)
                    o_ref[rows, :] += lax.dot_general(pick, ow, tn, preferred_element_type=F32)
                    return carry

                lax.fori_loop(0, (cnt[s][j] + PIECE - 1) // PIECE, block, 0)


def _moe(h2, x1, mod, rank_t, w_t, counts, w_gate, w_up, w_down, ws_gate, ws_up, ws_down, seq):
    T, D = h2.shape
    ns, E, tm = rank_t.shape
    tmo = min(MOE_OUTER, seq)
    n_sub = tmo // tm
    n_exp = MOE_EXPERTS_PER_STEP
    per_b = seq // tmo
    ws = [w_gate.astype(BF16), w_up.astype(BF16), w_down.astype(BF16)]
    sh = [ws_gate.astype(BF16), ws_up.astype(BF16), ws_down.astype(BF16)]
    tok = lambda: pl.BlockSpec((tmo, D), lambda t, e, c: (t, 0))
    sub = lambda: pl.BlockSpec((n_sub, E, tm), lambda t, e, c: (t, 0, 0))
    exp = lambda a: pl.BlockSpec((n_exp,) + a.shape[1:], lambda t, e, c: (e, 0, 0))
    full = lambda a: pl.BlockSpec(a.shape, lambda t, e, c: (0, 0))
    grid_spec = pltpu.PrefetchScalarGridSpec(
        num_scalar_prefetch=1,
        grid=(T // tmo, E // n_exp),
        in_specs=[tok(), tok(), pl.BlockSpec((None, N_MOD, D), lambda t, e, c: (t // per_b, 0, 0)),
                  sub(), sub(), exp(ws[0]), exp(ws[1]), exp(ws[2]), full(sh[0]), full(sh[1]), full(sh[2])],
        out_specs=tok(),
        scratch_shapes=[pltpu.VMEM((n_sub, MOE_ROWS, D), BF16), pltpu.VMEM((n_sub, MOE_ROWS, D), BF16),
                        pltpu.VMEM((n_sub, tm, MOE_ROWS), BF16), pltpu.VMEM((n_sub, MOE_ROWS, 1), F32)],
    )
    return pl.pallas_call(
        functools.partial(_moe_kernel, n_sub=n_sub, tm=tm, n_exp=n_exp),
        grid_spec=grid_spec,
        out_shape=jax.ShapeDtypeStruct((T, D), F32),
        compiler_params=_params(("arbitrary", "arbitrary")),
        name="moe",
    )(counts, h2, x1, mod, rank_t, w_t, *ws, *sh)


def kernel(x, c, w_ada, b_ada, norm_mix_gain, norm_ffn_gain, w_in, s5_lambda_re, s5_lambda_im, s5_log_dt, s5_b_re, s5_b_im, s5_c_re, s5_c_im, s5_d, s5_w_glu, s5_b_glu, q_norm_gain, k_norm_gain, cmp_pe, cmp_w1, cmp_b1, cmp_w2, cmp_b2, w_branch_a, w_branch_b, w_out, w_router, router_bias, w_gate, w_up, w_down, ws_gate, ws_up, ws_down):
    B, S, D = x.shape
    for l in range(w_ada.shape[0]):
        mod = _ada(c, w_ada[l], b_ada[l]).reshape(B, N_MOD, D)
        u, kvc, qt, ks, vst, kwn, vwt, gt, gm = _inproj(x, mod, norm_mix_gain[l:l + 1], w_in[l],
                                                        q_norm_gain[l], k_norm_gain[l])
        tables = _s5_tables(s5_lambda_re[l], s5_lambda_im[l], s5_log_dt[l], s5_b_re[l], s5_b_im[l],
                            s5_c_re[l], s5_c_im[l], s5_d[l])
        ys5 = _s5(u, tables)
        kvc_c = _compress(kvc, cmp_pe[l], cmp_w1[l], cmp_b1[l], cmp_w2[l], cmp_b2[l], k_norm_gain[l, 0])
        o_nsa = _attend(qt, ks, vst, kwn, vwt, gt, kvc_c)
        x1, h2, scores_t = _merge(x, mod, ys5, o_nsa, gm, s5_w_glu[l], s5_b_glu[l], w_branch_a[l],
                                  w_branch_b[l], w_out[l], norm_ffn_gain[l:l + 1], w_router[l])
        rank_t, w_t, cnt = _route(scores_t, router_bias[l])
        counts = cnt[:, :, 0].astype(jnp.int32).reshape(-1)
        x = _moe(h2.reshape(B * S, D), x1.reshape(B * S, D), mod, rank_t, w_t, counts,
                 w_gate[l], w_up[l], w_down[l], ws_gate[l], ws_up[l], ws_down[l], S).reshape(B, S, D)
    return x
```

```python
import functools
import math

import numpy as np
import jax
import jax.numpy as jnp
from jax import lax
from jax.experimental import pallas as pl
from jax.experimental.pallas import tpu as pltpu

F32 = jnp.float32
BF16 = jnp.bfloat16

D_MODEL = 1024
S5_WIDTH = 512
S5_GROUP = 16
S5_GROUPS = S5_WIDTH // S5_GROUP
S5_STATE = 64
N_HEADS = 8
N_KV_HEADS = 2
GQA_GROUP = N_HEADS // N_KV_HEADS
HEAD_DIM = 64
NSA_WIDTH = N_HEADS * HEAD_DIM
KV_WIDTH = 2 * N_KV_HEADS * HEAD_DIM
CMP_BLOCK = 32
CMP_STRIDE = 16
CMP_HIDDEN = 256
SEL_BLOCK = 64
SEL_TOPK = 8
WINDOW = 256
Q_BLOCK = 128
FORCE_BONUS = 1e3
N_EXPERTS = 64
TOP_K = 8
N_EXPERT_GROUPS = 8
TOPK_EXPERT_GROUPS = 4
EXPERT_HIDDEN = 256
SHARED_HIDDEN = 256
ROUTED_SCALE = 2.5
RMS_EPS = 1e-6
NEG_INF = -1e30
N_MOD = 6

LANES = 128
SUBLANES = 8
S5_CHUNK = 16
S5_CW = S5_CHUNK * S5_GROUP
ROW_TILE = 512
SEL_KEY_CHUNK = 512
SEL_UNIT = 128
SEL_SLOTS = (5, 8)
POS_BASE = 64
POS_ROWS = 16
AUG = 128
ROUTE_TILE = 256
MOE_OUTER = 1024
MOE_EXPERTS_PER_STEP = 4
MOE_PIECE = 128
MOE_WIN = 48
MOE_ROWS = 512
RANK_NONE = -float(1 << 20)
VMEM_LIMIT = 56 * 1024 * 1024


def _dot(a, b):
    return jnp.dot(a, b, preferred_element_type=F32)


def _split(a):
    hi = a.astype(BF16)
    lo = (a - hi.astype(F32)).astype(BF16)
    return hi, lo


def _dot3(a, bh, bl):
    ah, al = _split(a)
    return _dot(ah, bh) + (_dot(al, bh) + _dot(ah, bl))


def _segment_transpose(x):
    assert x.shape[-2:] == (SUBLANES, LANES) and LANES // S5_GROUP == SUBLANES
    nd = x.ndim
    i = lax.broadcasted_iota(jnp.int32, x.shape, nd - 2)
    seg = lax.broadcasted_iota(jnp.int32, x.shape, nd - 1) // S5_GROUP
    out = x
    for d in range(1, SUBLANES):
        r = pltpu.roll(pltpu.roll(x, SUBLANES - d, axis=nd - 2), S5_GROUP * d, axis=nd - 1)
        out = jnp.where(seg == ((i + d) & (SUBLANES - 1)), r, out)
    return out


def _to_chunk_major(u, put):
    rows = u.shape[0]
    u3 = u.reshape(rows // SUBLANES, SUBLANES, S5_WIDTH)
    halves = S5_CHUNK // SUBLANES
    for jb in range(S5_WIDTH // LANES):
        t = _segment_transpose(u3[:, :, LANES * jb:LANES * (jb + 1)])
        t = t.reshape(rows // S5_CHUNK, halves, SUBLANES, LANES)
        for hf in range(halves):
            put(slice(SUBLANES * jb, SUBLANES * (jb + 1)), slice(LANES * hf, LANES * (hf + 1)), t[:, hf])


def _from_chunk_major(get, rows):
    halves = S5_CHUNK // SUBLANES
    cols = []
    for jb in range(S5_WIDTH // LANES):
        parts = [_segment_transpose(get(slice(SUBLANES * jb, SUBLANES * (jb + 1)), slice(LANES * hf, LANES * (hf + 1))))
                 for hf in range(halves)]
        cols.append(jnp.stack(parts, axis=1).reshape(rows, LANES))
    return jnp.concatenate(cols, axis=1)


def _params(sem):
    return pltpu.CompilerParams(dimension_semantics=sem, vmem_limit_bytes=VMEM_LIMIT)


def _ada_kernel(c_ref, w_ref, b_ref, o_ref):
    cs = jax.nn.silu(c_ref[...])
    wh, wl = _split(w_ref[...])
    o_ref[...] = _dot3(cs, wh, wl) + b_ref[...]


def _ada(c, w_ada, b_ada):
    B, D = c.shape
    return pl.pallas_call(
        _ada_kernel,
        grid=(N_MOD,),
        in_specs=[pl.BlockSpec((B, D), lambda j: (0, 0)),
                  pl.BlockSpec((D, D), lambda j: (0, j)),
                  pl.BlockSpec((1, D), lambda j: (0, j))],
        out_specs=pl.BlockSpec((B, D), lambda j: (0, j)),
        out_shape=jax.ShapeDtypeStruct((B, N_MOD * D), F32),
        compiler_params=_params(("arbitrary",)),
        name="ada",
    )(c, w_ada, b_ada.reshape(1, N_MOD * D))


def _head_norm(v, bd, gain):
    sq = v * v
    sh, sl = _split(sq)
    ms = _dot(sh, bd) + _dot(sl, bd)
    return v * lax.rsqrt(ms + RMS_EPS) * gain


def _inproj_kernel(x_ref, mod_ref, gain_ref, wm_ref, wt_ref, wgn_ref, wgm_ref, bd_ref, qg_ref, kg_ref,
                   u_ref, kvc_ref, qt_ref, ksa_ref, vst_ref, kwa_ref, vwt_ref, gt_ref, gm_ref):
    x = x_ref[...]
    shift = mod_ref[0:1, :]
    scale = mod_ref[1:2, :]
    y = x * lax.rsqrt(jnp.mean(x * x, axis=-1, keepdims=True) + RMS_EPS)
    h = (y * gain_ref[...]) * (1.0 + scale) + shift
    hb = h.astype(BF16)
    main = _dot(hb, wm_ref[...])
    tm = x.shape[0]

    def put(groups, lanes, block):
        u_ref[:, groups, lanes] = block

    _to_chunk_major(main[:, :S5_WIDTH], put)
    o = S5_WIDTH
    kvc_ref[...] = main[:, o:o + KV_WIDTH]
    o += KV_WIDTH
    kw = N_KV_HEADS * HEAD_DIM
    bd = bd_ref[...]
    ks = _head_norm(main[:, o:o + kw], bd[:kw, :kw], kg_ref[1:2, :]).astype(BF16)
    kwn = _head_norm(main[:, o + kw:o + 2 * kw], bd[:kw, :kw], kg_ref[2:3, :]).astype(BF16)
    pos = pl.program_id(1) * tm + lax.broadcasted_iota(jnp.int32, (tm, 1), 0)
    lane = lax.broadcasted_iota(jnp.int32, (tm, AUG), 1)
    digits = jnp.where(lane == 0, pos // POS_BASE, jnp.where(lane == 1, pos % POS_BASE, 0))
    digits = digits[:, :AUG - HEAD_DIM].astype(F32).astype(BF16)
    onehot = jnp.where(lane == pos // SEL_BLOCK, 1.0, 0.0).astype(BF16)
    for hh in range(N_KV_HEADS):
        ksa_ref[hh, :, 0:HEAD_DIM] = ks[:, hh * HEAD_DIM:(hh + 1) * HEAD_DIM]
        ksa_ref[hh, :, HEAD_DIM:AUG] = digits
        ksa_ref[hh, :, AUG:] = onehot
        kwa_ref[hh, :, 0:HEAD_DIM] = kwn[:, hh * HEAD_DIM:(hh + 1) * HEAD_DIM]
        kwa_ref[hh, :, HEAD_DIM:] = digits

    nt = (((1,), (1,)), ((), ()))
    tt = lax.dot_general(wt_ref[...], hb, nt, preferred_element_type=F32)
    qt = tt[:NSA_WIDTH]
    sq = qt * qt
    sh, sl = _split(sq)
    qn = (qt * lax.rsqrt(_dot(bd, sh) + _dot(bd, sl) + RMS_EPS) * qg_ref[...] * (HEAD_DIM ** -0.5)).astype(BF16)
    QW = GQA_GROUP * Q_BLOCK
    for qb in range(tm // Q_BLOCK):
        for hd in range(N_HEADS):
            hh, g = divmod(hd, GQA_GROUP)
            qt_ref[hh, :, qb * QW + g * Q_BLOCK:qb * QW + (g + 1) * Q_BLOCK] = (
                qn[hd * HEAD_DIM:(hd + 1) * HEAD_DIM, qb * Q_BLOCK:(qb + 1) * Q_BLOCK])
    vst_ref[...] = tt[NSA_WIDTH:NSA_WIDTH + kw].reshape(N_KV_HEADS, HEAD_DIM, tm).astype(BF16)
    vwt_ref[...] = tt[NSA_WIDTH + kw:].reshape(N_KV_HEADS, HEAD_DIM, tm).astype(BF16)
    gn = jax.nn.sigmoid(lax.dot_general(wgn_ref[...], hb, nt, preferred_element_type=F32))
    for qb in range(tm // Q_BLOCK):
        for br in range(3):
            for hd in range(N_HEADS):
                hh, g = divmod(hd, GQA_GROUP)
                r = br * N_HEADS + hd
                gt_ref[hh, br:br + 1, qb * QW + g * Q_BLOCK:qb * QW + (g + 1) * Q_BLOCK] = (
                    gn[r:r + 1, qb * Q_BLOCK:(qb + 1) * Q_BLOCK])
    gm_ref[...] = jax.nn.sigmoid(_dot(hb, wgm_ref[...])).astype(BF16)


def _s5_chunk_spec(tm):
    assert S5_CHUNK == 16 and S5_GROUP == 16, "the segment transposes assume 16 steps x 16 channels"
    return pl.BlockSpec((tm // S5_CHUNK, None, S5_GROUPS, S5_CW), lambda b, i: (i, b, 0, 0))


def _inproj(x, mod, gain, w_in, q_gain, k_gain):
    B, S, D = x.shape
    tm = min(ROW_TILE, S)
    assert S // SEL_BLOCK <= AUG and tm % Q_BLOCK == 0
    kw = N_KV_HEADS * HEAD_DIM
    cols = np.cumsum((0,) + (S5_WIDTH, NSA_WIDTH, KV_WIDTH, KV_WIDTH, KV_WIDTH, 3 * N_HEADS, 2 * D))
    c_u, c_q, c_kvc, c_kvs, c_kvw, c_gn, c_gm = cols[:7]
    sl = lambda a, n: w_in[:, a:a + n]
    wm = jnp.concatenate([sl(c_u, S5_WIDTH), sl(c_kvc, KV_WIDTH), sl(c_kvs, kw), sl(c_kvw, kw)], axis=1).astype(BF16)
    wt = jnp.concatenate([sl(c_q, NSA_WIDTH), sl(c_kvs + kw, kw), sl(c_kvw + kw, kw)], axis=1).T.astype(BF16)
    wgn = sl(c_gn, 3 * N_HEADS).T.astype(BF16)
    wgm = sl(c_gm, 2 * D).astype(BF16)
    seg = np.arange(NSA_WIDTH) // HEAD_DIM
    bd = jnp.asarray((seg[:, None] == seg[None, :]).astype(np.float32) / HEAD_DIM, BF16)
    qg = jnp.tile(q_gain, N_HEADS).reshape(NSA_WIDTH, 1)
    kg = jnp.tile(k_gain, (1, N_KV_HEADS))
    nq = tm // Q_BLOCK * GQA_GROUP * Q_BLOCK
    row = lambda w: pl.BlockSpec((None, tm, w), lambda b, i: (b, i, 0))
    full = lambda a: pl.BlockSpec(a.shape, lambda b, i: (0,) * a.ndim)
    rows4 = lambda w: pl.BlockSpec((None, N_KV_HEADS, tm, w), lambda b, i: (b, 0, i, 0))
    cols4 = lambda r, w: pl.BlockSpec((None, N_KV_HEADS, r, w), lambda b, i: (b, 0, 0, i))
    nqt = S // Q_BLOCK * GQA_GROUP * Q_BLOCK
    return pl.pallas_call(
        _inproj_kernel,
        grid=(B, S // tm),
        in_specs=[row(D), pl.BlockSpec((None, N_MOD, D), lambda b, i: (b, 0, 0)),
                  full(gain), full(wm), full(wt), full(wgn), full(wgm), full(bd), full(qg), full(kg)],
        out_specs=[_s5_chunk_spec(tm), row(KV_WIDTH), cols4(HEAD_DIM, nq), rows4(2 * AUG), cols4(HEAD_DIM, tm),
                   rows4(AUG), cols4(HEAD_DIM, tm), cols4(3, nq), row(2 * D)],
        out_shape=[jax.ShapeDtypeStruct((S // S5_CHUNK, B, S5_GROUPS, S5_CW), F32),
                   jax.ShapeDtypeStruct((B, S, KV_WIDTH), F32),
                   jax.ShapeDtypeStruct((B, N_KV_HEADS, HEAD_DIM, nqt), BF16),
                   jax.ShapeDtypeStruct((B, N_KV_HEADS, S, 2 * AUG), BF16),
                   jax.ShapeDtypeStruct((B, N_KV_HEADS, HEAD_DIM, S), BF16),
                   jax.ShapeDtypeStruct((B, N_KV_HEADS, S, AUG), BF16),
                   jax.ShapeDtypeStruct((B, N_KV_HEADS, HEAD_DIM, S), BF16),
                   jax.ShapeDtypeStruct((B, N_KV_HEADS, 3, nqt), F32),
                   jax.ShapeDtypeStruct((B, S, 2 * D), BF16)],
        compiler_params=_params(("arbitrary", "arbitrary")),
        name="inproj",
    )(x, mod, gain, wm, wt, wgn, wgm, bd, qg, kg)


def _s5_tables(lam_re, lam_im, log_dt, b_re, b_im, c_re, c_im, d_skip):
    L, P, N, G = S5_CHUNK, S5_GROUP, S5_STATE, S5_GROUPS
    hp = lax.Precision.HIGHEST
    lr, li = lam_re.astype(F32), lam_im.astype(F32)
    dt = jnp.exp(log_dt.astype(F32))[:, None]
    mag = jnp.exp(lr * dt)
    abar_re, abar_im = mag * jnp.cos(li * dt), mag * jnp.sin(li * dt)
    num_re, num_im = abar_re - 1.0, abar_im
    den = lr * lr + li * li
    coef_re = (num_re * lr + num_im * li) / den
    coef_im = (num_im * lr - num_re * li) / den
    br, bi = b_re.astype(F32), b_im.astype(F32)
    bbar_re = coef_re[..., None] * br - coef_im[..., None] * bi
    bbar_im = coef_re[..., None] * bi + coef_im[..., None] * br
    k = jnp.arange(L + 1, dtype=F32)[:, None, None]
    pmag = jnp.exp(lr * dt * k)
    pre, pim = pmag * jnp.cos(li * dt * k), pmag * jnp.sin(li * dt * k)
    cr, ci = c_re.astype(F32), c_im.astype(F32)
    ca_re = cr[None] * pre[:, :, None, :] - ci[None] * pim[:, :, None, :]
    ca_im = cr[None] * pim[:, :, None, :] + ci[None] * pre[:, :, None, :]
    kern = (jnp.einsum('kgpn,gnq->gkqp', ca_re[:L], bbar_re, precision=hp)
            - jnp.einsum('kgpn,gnq->gkqp', ca_im[:L], bbar_im, precision=hp))
    s_i = np.arange(L)[:, None]
    t_i = np.arange(L)[None, :]
    tau = np.clip(t_i - s_i, 0, L - 1)
    causal = jnp.asarray((t_i >= s_i).astype(np.float32))
    mt = kern[:, tau] * causal[None, :, :, None, None]
    mt = mt.transpose(0, 1, 3, 2, 4).reshape(G, L * P, L * P)
    rev = np.arange(L - 1, -1, -1)
    ab_re = pre[rev][..., None] * bbar_re[None] - pim[rev][..., None] * bbar_im[None]
    ab_im = pre[rev][..., None] * bbar_im[None] + pim[rev][..., None] * bbar_re[None]
    ws = jnp.concatenate([ab_re, ab_im], axis=2)
    ws = ws.transpose(1, 0, 3, 2).reshape(G, L * P, 2 * N)
    wo = jnp.concatenate([ca_re[1:], -ca_im[1:]], axis=3)
    wo = wo.transpose(1, 3, 0, 2).reshape(G, 2 * N, L * P)
    al = jnp.stack([pre[L], pim[L]], axis=1)
    dv = jnp.tile(d_skip.astype(F32), (1, L)).reshape(G, 1, L * P)
    return mt, ws, wo, al, dv


def _s5_kernel(u_ref, mth_ref, mtl_ref, wsh_ref, wsl_ref, woh_ref, wol_ref, al_ref, dv_ref,
               y_ref, vr_scr, vi_scr, xr_scr, xi_scr, *, n_chunks, bsz):
    N = S5_STATE
    u = u_ref[...]
    uh, ul = _split(u)
    y = _dot(uh, mth_ref[...]) + (_dot(ul, mth_ref[...]) + _dot(uh, mtl_ref[...]))
    v = _dot(uh, wsh_ref[...]) + (_dot(ul, wsh_ref[...]) + _dot(uh, wsl_ref[...]))
    vr_scr[...] = v[:, :N]
    vi_scr[...] = v[:, N:]
    a_r = al_ref[0:1, :]
    a_i = al_ref[1:2, :]

    def step(c, x):
        xr, xi = x
        r = pl.ds(pl.multiple_of(c * bsz, bsz), bsz)
        xr_scr[r, :] = xr
        xi_scr[r, :] = xi
        return (a_r * xr - a_i * xi + vr_scr[r, :], a_r * xi + a_i * xr + vi_scr[r, :])

    zero = jnp.zeros((bsz, N), F32)
    lax.fori_loop(0, n_chunks, step, (zero, zero), unroll=8)
    xp = jnp.concatenate([xr_scr[...], xi_scr[...]], axis=1)
    y = y + _dot3(xp, woh_ref[...], wol_ref[...])
    y_ref[...] = y + dv_ref[...] * u


def _s5(ug, tables):
    nc, B, G, cw = ug.shape
    N = S5_STATE
    mt, ws, wo, al, dv = tables
    mth, mtl = _split(mt)
    wsh, wsl = _split(ws)
    woh, wol = _split(wo)
    grp = lambda a: pl.BlockSpec((None,) + a.shape[1:], lambda g: (g,) + (0,) * (a.ndim - 1))
    ugt = ug.reshape(nc * B, G, cw).transpose(1, 0, 2)
    y = pl.pallas_call(
        functools.partial(_s5_kernel, n_chunks=nc, bsz=B),
        grid=(G,),
        in_specs=[grp(ugt), grp(mth), grp(mtl), grp(wsh), grp(wsl), grp(woh), grp(wol), grp(al), grp(dv)],
        out_specs=grp(ugt),
        out_shape=jax.ShapeDtypeStruct(ugt.shape, F32),
        scratch_shapes=[pltpu.VMEM((nc * B, N), F32)] * 4,
        compiler_params=_params(("arbitrary",)),
        name="s5",
    )(ugt, mth, mtl, wsh, wsl, woh, wol, al, dv)
    return y.transpose(1, 0, 2).reshape(nc, B, G, cw)


def _compress_kernel(x_ref, pe_ref, w1_ref, b1_ref, w2_ref, b2_ref, kg_ref, o_ref, *, n_rows):
    x = x_ref[...]
    half = CMP_STRIDE * HEAD_DIM
    a = _dot((x + pe_ref[0:1, :]).astype(BF16), w1_ref[:half, :])
    b = _dot((x + pe_ref[1:2, :]).astype(BF16), w1_ref[half:, :])
    hid = jax.nn.gelu(a + pltpu.roll(b, n_rows - 1, axis=0) + b1_ref[...])
    out = _dot(hid.astype(BF16), w2_ref[...]) + b2_ref[...]
    normed = out * lax.rsqrt(jnp.mean(out * out, axis=-1, keepdims=True) + RMS_EPS) * kg_ref[...]
    is_key = pl.program_id(1) < N_KV_HEADS
    o_ref[...] = jnp.where(is_key, normed, out).astype(BF16)


def _compress(kv_c, cmp_pe, cmp_w1, cmp_b1, cmp_w2, cmp_b2, k_gain0):
    B, S, _ = kv_c.shape
    nr = S // CMP_STRIDE
    half = CMP_STRIDE * HEAD_DIM
    nj = 2 * N_KV_HEADS
    xc = kv_c.reshape(B, nr, CMP_STRIDE, nj, HEAD_DIM).transpose(0, 3, 1, 2, 4).reshape(B, nj, nr, half)
    pe = cmp_pe.reshape(2, 2, half)
    w1 = cmp_w1.astype(BF16)
    w2 = cmp_w2.astype(BF16)
    b1 = cmp_b1.reshape(2, 1, CMP_HIDDEN)
    b2 = cmp_b2.reshape(2, 1, HEAD_DIM)
    kv = lambda a: pl.BlockSpec((None,) + a.shape[1:], lambda b, j: (j // N_KV_HEADS,) + (0,) * (a.ndim - 1))
    return pl.pallas_call(
        functools.partial(_compress_kernel, n_rows=nr),
        grid=(B, nj),
        in_specs=[pl.BlockSpec((None, None, nr, half), lambda b, j: (b, j, 0, 0)),
                  kv(pe), kv(w1), kv(b1), kv(w2), kv(b2),
                  pl.BlockSpec((1, HEAD_DIM), lambda b, j: (0, 0))],
        out_specs=pl.BlockSpec((None, None, nr, HEAD_DIM), lambda b, j: (b, j, 0, 0)),
        out_shape=jax.ShapeDtypeStruct((B, nj, nr, HEAD_DIM), BF16),
        compiler_params=_params(("arbitrary", "arbitrary")),
        name="compress",
    )(xc, pe, w1, b1, w2, b2, k_gain0.reshape(1, HEAD_DIM))


def _softmax_cols(s, mask):
    s = jnp.where(mask, s, NEG_INF)
    m = jnp.max(s, axis=0, keepdims=True)
    p = jnp.where(mask, jnp.exp(s - m), 0.0)
    return p / jnp.maximum(jnp.sum(p, axis=0, keepdims=True), 1e-20)


def _attend_kernel(qt_ref, kc_ref, vct_ref, ks_ref, vst_ref, kw_ref, vwt_ref, g_ref, slope_ref, ovt_ref,
                   o_ref, qa_scr, sa_scr, sb_scr, *, n_sel, n_pick, n_cmp_rows):
    QW = GQA_GROUP * Q_BLOCK
    i = pl.program_id(2)
    t0 = i * Q_BLOCK
    slope = slope_ref[...]
    tq = t0 + (lax.broadcasted_iota(jnp.int32, (1, QW), 1) & (Q_BLOCK - 1))

    r = lax.broadcasted_iota(jnp.int32, (POS_ROWS, QW), 0)
    qa_scr[0:HEAD_DIM, :] = qt_ref[...]
    qa_scr[HEAD_DIM:HEAD_DIM + POS_ROWS, :] = jnp.where(
        r == 0, slope * POS_BASE, jnp.where(r == 1, slope, 0.0)).astype(BF16)
    qa_scr[HEAD_DIM + POS_ROWS:AUG, :] = jnp.zeros((AUG - HEAD_DIM - POS_ROWS, QW), BF16)
    qa = qa_scr[0:AUG, :]

    def compressed(rows):
        sc = _dot(kc_ref[0:rows, :], qa)
        cpos = lax.broadcasted_iota(jnp.int32, (rows, QW), 0) * CMP_STRIDE + (CMP_BLOCK - 1)
        p_c = _softmax_cols(sc, cpos <= tq)
        o_c = _dot(vct_ref[:, 0:rows], p_c.astype(BF16))
        psum = p_c[:, 0:Q_BLOCK]
        for g in range(1, GQA_GROUP):
            psum = psum + p_c[:, g * Q_BLOCK:(g + 1) * Q_BLOCK]
        ph, pl_ = _split(psum)
        return o_c, _dot(ovt_ref[:, 0:rows], ph) + _dot(ovt_ref[:, 0:rows], pl_)

    half = n_cmp_rows // 2
    if half % LANES == 0:
        early = (t0 + Q_BLOCK - CMP_BLOCK) // CMP_STRIDE < half
        o_c, imp = lax.cond(early, lambda _: compressed(half), lambda _: compressed(n_cmp_rows), 0)
    else:
        o_c, imp = compressed(n_cmp_rows)

    jb = lax.broadcasted_iota(jnp.int32, (n_sel, Q_BLOCK), 0)
    cur = (t0 + lax.broadcasted_iota(jnp.int32, (1, Q_BLOCK), 1)) // SEL_BLOCK
    forced = (jb == 0) | (jb == cur) | (jb == cur - 1)
    imp = jnp.where(forced, imp + FORCE_BONUS, imp)
    imp = jnp.where(jb <= cur, imp, -1.0)
    bias = jnp.full((n_sel, Q_BLOCK), NEG_INF, F32)
    for _ in range(n_pick):
        m = jnp.max(imp, axis=0, keepdims=True)
        first = jnp.min(jnp.where(imp == m, jb, n_sel), axis=0, keepdims=True)
        hit = jb == first
        bias = jnp.where(hit, 0.0, bias)
        imp = jnp.where(hit, -jnp.inf, imp)
    per_unit = SEL_UNIT // SEL_BLOCK
    n_units = n_sel // per_unit
    picked = jnp.where(bias == 0.0, 1.0, 0.0).astype(BF16)
    in_unit = (lax.broadcasted_iota(jnp.int32, (n_units, n_sel), 1) // per_unit
               == lax.broadcasted_iota(jnp.int32, (n_units, n_sel), 0))
    hits = _dot(jnp.where(in_unit, 1.0, 0.0).astype(BF16), picked)
    live = jnp.where(jnp.max(hits, axis=1, keepdims=True) > 0.5, 1.0, 0.0)
    live = jnp.broadcast_to(live, (n_units, LANES))
    earlier = (lax.broadcasted_iota(jnp.int32, (n_units, n_units), 1)
               < lax.broadcasted_iota(jnp.int32, (n_units, n_units), 0))
    before = _dot(jnp.where(earlier, 1.0, 0.0).astype(BF16), live.astype(BF16))
    slot = lax.broadcasted_iota(jnp.int32, (n_units, LANES), 1).astype(F32)
    in_slot = jnp.where(before == slot, live, 0.0).astype(BF16)
    r8 = lax.broadcasted_iota(jnp.int32, (SUBLANES, n_units), 0)
    c8 = lax.broadcasted_iota(jnp.int32, (SUBLANES, n_units), 1)
    pick_rows = jnp.where(r8 == 0, c8, jnp.where(r8 == 1, 1, 0)).astype(F32).astype(BF16)
    ids = _dot(pick_rows, in_slot)
    n_live = (before[n_units - 1:n_units, 0:1] + live[n_units - 1:n_units, 0:1])[0, 0]

    bias = jnp.concatenate([bias] * GQA_GROUP, axis=1).astype(BF16)
    qa_scr[AUG:, :] = jnp.concatenate([bias, jnp.zeros((AUG - n_sel, QW), BF16)], axis=0)

    def gathered(n_slots):
        units = [ids[0, s].astype(jnp.int32) for s in range(n_slots)]
        starts = [pl.multiple_of(u * SEL_UNIT, SEL_UNIT) for u in units]
        keys = jnp.concatenate([ks_ref[pl.ds(st, SEL_UNIT), :] for st in starts], axis=0)
        s = _dot(keys, qa_scr[...])
        krow = lax.broadcasted_iota(jnp.int32, (SEL_UNIT, QW), 0)
        seen = jnp.concatenate(
            [st + krow <= jnp.where(ids[1, si] > 0.5, tq, -1) for si, st in enumerate(starts)], axis=0)
        s = jnp.where(seen, s, NEG_INF)
        p = jnp.exp(s - jnp.max(s, axis=0, keepdims=True))
        vals = jnp.concatenate([vst_ref[:, pl.ds(st, SEL_UNIT)] for st in starts], axis=1)
        return _dot(vals, p.astype(BF16)) / jnp.maximum(jnp.sum(p, axis=0, keepdims=True), 1e-20)

    KC = SEL_KEY_CHUNK
    n_chunks = (t0 + Q_BLOCK + KC - 1) // KC

    def scores_to(ref, j):
        k0 = pl.multiple_of(j * KC, KC)
        ref[...] = _dot(ks_ref[pl.ds(k0, KC), :], qa_scr[...])

    def accumulate(carry, j, s, p_of):
        m, l, acc = carry
        k0 = pl.multiple_of(j * KC, KC)
        m_new = jnp.maximum(m, jnp.max(s, axis=0, keepdims=True))
        alpha = jnp.exp(m - m_new)
        p = p_of(jnp.exp(s - m_new))
        l = alpha * l + jnp.sum(p, axis=0, keepdims=True)
        acc = alpha * acc + _dot(vst_ref[:, pl.ds(k0, KC)], p.astype(BF16))
        return m_new, l, acc

    keep = lambda p: p
    last = n_chunks - 1

    def two_chunks(i, state):
        j = 2 * i
        scores_to(sb_scr, j + 1)
        state = accumulate(state, j, sa_scr[...], keep)
        scores_to(sa_scr, j + 2)
        return accumulate(state, j + 1, sb_scr[...], keep)

    def odd_chunk(state):
        state = accumulate(state, last - 1, sa_scr[...], keep)
        scores_to(sa_scr, last)
        return state

    def swept(_):
        scores_to(sa_scr, 0)
        init = (jnp.full((1, QW), NEG_INF, F32), jnp.zeros((1, QW), F32), jnp.zeros((HEAD_DIM, QW), F32))
        state = lax.fori_loop(0, last // 2, two_chunks, init)
        state = lax.cond(last % 2 == 1, odd_chunk, lambda st: st, state)
        visible = (last * KC + lax.broadcasted_iota(jnp.int32, (KC, QW), 0)) <= tq
        _, l_s, acc_s = accumulate(state, last, jnp.where(visible, sa_scr[...], NEG_INF),
                                   lambda p: jnp.where(visible, p, 0.0))
        return acc_s / jnp.maximum(l_s, 1e-20)

    WK = WINDOW + Q_BLOCK
    w0 = pl.multiple_of(jnp.maximum(t0 - WINDOW, 0), Q_BLOCK)
    sw = _dot(kw_ref[pl.ds(w0, WK), :], qa)
    dist = tq - (w0 + lax.broadcasted_iota(jnp.int32, (WK, QW), 0))
    in_win = (dist | (WINDOW - 1 - dist)) >= 0
    p_w = _softmax_cols(sw, in_win)
    o_w = _dot(vwt_ref[:, pl.ds(w0, WK)], p_w.astype(BF16))

    few, some = SEL_SLOTS
    o_s = lax.cond(n_live <= few, lambda _: gathered(few),
                   lambda _: lax.cond(n_live <= some, lambda _: gathered(some), swept, 0), 0)

    o = g_ref[0:1, :] * o_c + g_ref[1:2, :] * o_s + g_ref[2:3, :] * o_w
    pairs = []
    for g in range(0, GQA_GROUP, 2):
        sq = jnp.concatenate([o[:, g * Q_BLOCK:(g + 1) * Q_BLOCK], o[:, (g + 1) * Q_BLOCK:(g + 2) * Q_BLOCK]], axis=0)
        pairs.append(sq.T)
    o_ref[...] = jnp.concatenate(pairs, axis=1).astype(BF16)


def _attend(qt, ks, vst, kwn, vwt, gt, kvc_c):
    B, _, S, _ = ks.shape
    nqb = S // Q_BLOCK
    QW = GQA_GROUP * Q_BLOCK
    n_sel = S // SEL_BLOCK
    n_pick = min(SEL_TOPK, n_sel)
    nr = S // CMP_STRIDE
    n_cmp = (S - CMP_BLOCK) // CMP_STRIDE + 1
    assert n_sel % 16 == 0 and n_sel <= AUG

    cpos = np.arange(nr) * CMP_STRIDE + CMP_BLOCK - 1
    digits = np.zeros((nr, AUG - HEAD_DIM), np.float32)
    digits[:, 0] = cpos // POS_BASE
    digits[:, 1] = cpos % POS_BASE
    kc = kvc_c[:, :N_KV_HEADS]
    kc = jnp.concatenate([kc, jnp.broadcast_to(jnp.asarray(digits, BF16), kc.shape[:2] + digits.shape)], axis=-1)
    vct = kvc_c[:, N_KV_HEADS:].transpose(0, 1, 3, 2)
    slopes = 2.0 ** (-8.0 * np.arange(1, N_HEADS + 1) / N_HEADS)
    slope_t = jnp.asarray(np.repeat(slopes.reshape(N_KV_HEADS, GQA_GROUP), Q_BLOCK, axis=1)
                          .reshape(N_KV_HEADS, 1, QW), F32)
    cmp_start = np.arange(n_cmp) * CMP_STRIDE
    sel_start = np.arange(n_sel) * SEL_BLOCK
    ov = ((cmp_start[:, None] <= sel_start[None, :] + SEL_BLOCK - 1)
          & (cmp_start[:, None] + CMP_BLOCK - 1 >= sel_start[None, :])).astype(np.float32)
    ovt = np.zeros((n_sel, nr), np.float32)
    ovt[:, :n_cmp] = ov.T
    ovt = jnp.asarray(ovt, BF16)

    per_bh = lambda r, c: pl.BlockSpec((None, None, r, c), lambda b, h, i: (b, h, 0, 0))
    per_q = lambda r: pl.BlockSpec((None, None, r, QW), lambda b, h, i: (b, h, 0, i))
    return pl.pallas_call(
        functools.partial(_attend_kernel, n_sel=n_sel, n_pick=n_pick, n_cmp_rows=nr),
        grid=(B, N_KV_HEADS, nqb),
        in_specs=[per_q(HEAD_DIM), per_bh(nr, AUG), per_bh(HEAD_DIM, nr),
                  per_bh(S, 2 * AUG), per_bh(HEAD_DIM, S), per_bh(S, AUG), per_bh(HEAD_DIM, S),
                  per_q(3),
                  pl.BlockSpec((None, 1, QW), lambda b, h, i: (h, 0, 0)),
                  pl.BlockSpec((n_sel, nr), lambda b, h, i: (0, 0))],
        out_specs=pl.BlockSpec((None, Q_BLOCK, GQA_GROUP * HEAD_DIM), lambda b, h, i: (b, i, h)),
        out_shape=jax.ShapeDtypeStruct((B, S, NSA_WIDTH), BF16),
        scratch_shapes=[pltpu.VMEM((2 * AUG, QW), BF16), pltpu.VMEM((SEL_KEY_CHUNK, QW), F32),
                        pltpu.VMEM((SEL_KEY_CHUNK, QW), F32)],
        compiler_params=_params(("arbitrary", "arbitrary", "arbitrary")),
        name="attend",
    )(qt, kc, vct, ks, vst, kwn, vwt, gt, slope_t, ovt)


def _merge_kernel(x_ref, mod_ref, ys_ref, on_ref, gm_ref, wglu_ref, bglu_ref, wa_ref, wb_ref, wo_ref,
                  gain_ref, wrh_ref, wrl_ref, x1_ref, h2_ref, sc_ref):
    D = D_MODEL
    z = jax.nn.gelu(_from_chunk_major(lambda groups, lanes: ys_ref[:, groups, lanes], x_ref.shape[0]))
    glu = z * jax.nn.sigmoid(_dot(z.astype(BF16), wglu_ref[...]) + bglu_ref[...])
    ya = _dot(glu.astype(BF16), wa_ref[...])
    yb = _dot(on_ref[...], wb_ref[...])
    merged = gm_ref[:, :D].astype(F32) * ya + gm_ref[:, D:].astype(F32) * yb
    x1 = x_ref[...] + mod_ref[2:3, :] * _dot(merged.astype(BF16), wo_ref[...])
    x1_ref[...] = x1
    y = x1 * lax.rsqrt(jnp.mean(x1 * x1, axis=-1, keepdims=True) + RMS_EPS)
    h2 = (y * gain_ref[...]) * (1.0 + mod_ref[4:5, :]) + mod_ref[3:4, :]
    h2_ref[...] = h2.astype(BF16)
    hh, hl = _split(h2)
    nt = (((1,), (1,)), ((), ()))
    dg = lambda a, b: lax.dot_general(a, b, nt, preferred_element_type=F32)
    logits = dg(wrh_ref[...], hh) + (dg(wrh_ref[...], hl) + dg(wrl_ref[...], hh))
    sc_ref[...] = jax.nn.sigmoid(logits)


def _merge(x, mod, ys5, o_nsa, gm, w_glu, b_glu, w_a, w_b, w_out, gain_f, w_router):
    B, S, D = x.shape
    tm = min(ROW_TILE, S)
    wrh, wrl = _split(w_router.T)
    ws = [w_glu.astype(BF16), b_glu.reshape(1, -1), w_a.astype(BF16), w_b.astype(BF16), w_out.astype(BF16),
          gain_f, wrh, wrl]
    row = lambda w: pl.BlockSpec((None, tm, w), lambda b, i: (b, i, 0))
    full = lambda a: pl.BlockSpec(a.shape, lambda b, i: (0,) * a.ndim)
    return pl.pallas_call(
        _merge_kernel,
        grid=(B, S // tm),
        in_specs=[row(D), pl.BlockSpec((None, N_MOD, D), lambda b, i: (b, 0, 0)),
                  _s5_chunk_spec(tm), row(NSA_WIDTH), row(2 * D)] + [full(a) for a in ws],
        out_specs=[row(D), row(D), pl.BlockSpec((None, N_EXPERTS, tm), lambda b, i: (b, 0, i))],
        out_shape=[jax.ShapeDtypeStruct((B, S, D), F32), jax.ShapeDtypeStruct((B, S, D), BF16),
                   jax.ShapeDtypeStruct((B, N_EXPERTS, S), F32)],
        compiler_params=_params(("arbitrary", "arbitrary")),
        name="merge",
    )(x, mod, ys5, o_nsa, gm, *ws)


def _first_argmax_rows(v, idx, n):
    m = jnp.max(v, axis=0, keepdims=True)
    first = jnp.min(jnp.where(v == m, idx, n), axis=0, keepdims=True)
    return idx == first, m


def _route_kernel(sc_ref, bias_ref, tri_ref, rank_ref, w_ref, cnt_ref):
    E, NG = N_EXPERTS, N_EXPERT_GROUPS
    GS = E // NG
    sc = sc_ref[...]
    TM = sc.shape[1]
    sel = sc + bias_ref[...]
    i8 = lax.broadcasted_iota(jnp.int32, (GS, TM), 0)
    gscore = []
    for g in range(NG):
        blk = sel[g * GS:(g + 1) * GS, :]
        hit, m1 = _first_argmax_rows(blk, i8, GS)
        m2 = jnp.max(jnp.where(hit, -jnp.inf, blk), axis=0, keepdims=True)
        gscore.append(m1 + m2)
    gscore = jnp.concatenate(gscore, axis=0)
    ig = lax.broadcasted_iota(jnp.int32, (NG, TM), 0)
    gmask = jnp.zeros((NG, TM), F32)
    for _ in range(TOPK_EXPERT_GROUPS):
        hit, _m = _first_argmax_rows(gscore, ig, NG)
        gmask = jnp.where(hit, 1.0, gmask)
        gscore = jnp.where(hit, -jnp.inf, gscore)
    emask = jnp.concatenate([jnp.broadcast_to(gmask[g:g + 1, :], (GS, TM)) for g in range(NG)], axis=0)
    cand = jnp.where(emask > 0.5, sel, NEG_INF)
    ie = lax.broadcasted_iota(jnp.int32, (E, TM), 0)
    chosen = jnp.zeros((E, TM), F32)
    for _ in range(TOP_K):
        hit, _m = _first_argmax_rows(cand, ie, E)
        chosen = jnp.where(hit, 1.0, chosen)
        cand = jnp.where(hit, -jnp.inf, cand)
    w = chosen * sc
    w = w / jnp.sum(w, axis=0, keepdims=True) * ROUTED_SCALE
    sub = rank_ref.shape[-1]
    for q in range(TM // sub):
        cols = slice(q * sub, (q + 1) * sub)
        cb = chosen[:, cols].astype(BF16)
        prefix = _dot(cb, tri_ref[...])
        rank_ref[q] = jnp.where(chosen[:, cols] > 0.5, prefix, RANK_NONE)
        w_ref[q] = w[:, cols]
        cnt_ref[q] = _dot(cb, jnp.ones((sub, LANES), BF16))


def _route(scores_t, router_bias):
    B, E, S = scores_t.shape
    TM = min(ROUTE_TILE, S)
    per_step = min(MOE_OUTER, S) // TM
    nj = S // (TM * per_step)
    ns = B * nj * per_step
    tri = jnp.asarray(np.triu(np.ones((TM, TM), np.float32), k=1), BF16)
    tile = lambda w: pl.BlockSpec((per_step, E, w), lambda b, j: (b * nj + j, 0, 0))
    return pl.pallas_call(
        _route_kernel,
        grid=(B, nj),
        in_specs=[pl.BlockSpec((None, E, TM * per_step), lambda b, j: (b, 0, j)),
                  pl.BlockSpec((E, 1), lambda b, j: (0, 0)),
                  pl.BlockSpec((TM, TM), lambda b, j: (0, 0))],
        out_specs=[tile(TM), tile(TM), tile(LANES)],
        out_shape=[jax.ShapeDtypeStruct((ns, E, TM), F32), jax.ShapeDtypeStruct((ns, E, TM), F32),
                   jax.ShapeDtypeStruct((ns, E, LANES), F32)],
        compiler_params=_params(("arbitrary", "arbitrary")),
        name="route",
    )(scores_t, router_bias.reshape(E, 1), tri)


def _moe_kernel(cnt_ref, h_ref, x1_ref, mod_ref, rank_ref, w_ref, wg_ref, wu_ref, wd_ref,
                sg_ref, su_ref, sd_ref, o_ref, xc, yc, pc, wc, *, n_sub, tm, n_exp):
    to = pl.program_id(0)
    eb = pl.program_id(1)
    gate = mod_ref[5:6, :]

    @pl.when(eb == 0)
    def _shared():
        for s in range(n_sub):
            rows = pl.ds(s * tm, tm)
            hs = h_ref[rows, :]
            hid = jax.nn.silu(_dot(hs, sg_ref[...])) * _dot(hs, su_ref[...])
            o_ref[rows, :] = x1_ref[rows, :] + gate * _dot(hid.astype(BF16), sd_ref[...])

    PIECE, WIN = MOE_PIECE, MOE_WIN
    tn = (((0,), (0,)), ((), ()))
    experts = [eb * n_exp + j for j in range(n_exp)]
    cnt = [[cnt_ref[(to * n_sub + s) * N_EXPERTS + e] for e in experts] for s in range(n_sub)]
    off = []
    for s in range(n_sub):
        o = [jnp.int32(0)]
        for j in range(n_exp):
            o.append(o[-1] + ((cnt[s][j] + 15) // 16) * 16)
        off.append(o)
    n_pieces = [(off[s][n_exp] + PIECE - 1) // PIECE for s in range(n_sub)]
    fits = n_pieces[0] < MOE_ROWS // PIECE
    for s in range(1, n_sub):
        fits = jnp.logical_and(fits, n_pieces[s] < MOE_ROWS // PIECE)

    def one_hot_rows(s, base, rows_n, js):
        slot = (lax.broadcasted_iota(jnp.int32, (rows_n, tm), 0) + base).astype(F32)
        pick = jnp.zeros((rows_n, tm), F32)
        wacc = jnp.zeros((rows_n, tm), F32)
        for j in js:
            hit = (slot - off[s][j].astype(F32)) == rank_ref[s, pl.ds(experts[j], 1), :]
            pick = jnp.where(hit, 1.0, pick)
            wacc = jnp.where(hit, w_ref[s, pl.ds(experts[j], 1), :], wacc)
        return pick.astype(BF16), jnp.sum(wacc, axis=1, keepdims=True)

    def mlp(xg, j):
        hid = jax.nn.silu(_dot(xg, wg_ref[j])) * _dot(xg, wu_ref[j])
        return _dot(hid.astype(BF16), wd_ref[j])

    def aligned(q, n):
        return q * n if isinstance(q, int) else pl.multiple_of(q * n, n)

    def piece(s, q):
        r = pl.ds(aligned(q, PIECE), PIECE)
        pick, wrow = one_hot_rows(s, q * PIECE, PIECE, range(n_exp))
        xc[s, r, :] = _dot(pick, h_ref[pl.ds(s * tm, tm), :]).astype(BF16)
        pc[s, :, r] = pick.T
        wc[s, r, :] = wrow
        yc[s, r, :] = jnp.zeros((PIECE, D_MODEL), BF16)

    def zero_piece(s, q):
        r = pl.ds(q * PIECE, PIECE)
        xc[s, r, :] = jnp.zeros((PIECE, D_MODEL), BF16)
        yc[s, r, :] = jnp.zeros((PIECE, D_MODEL), BF16)
        pc[s, :, r] = jnp.zeros((tm, PIECE), BF16)
        wc[s, r, :] = jnp.zeros((PIECE, 1), F32)

    def window(j, p, first):
        starts = [pl.multiple_of(jnp.minimum(off[s][j] + p * WIN, (n_pieces[s] + 1) * PIECE - WIN), 16)
                  for s in range(n_sub)]
        xg = jnp.concatenate([xc[s, pl.ds(starts[s], WIN), :] for s in range(n_sub)], axis=0)
        out = mlp(xg, j)
        for s in range(n_sub):
            r = pl.ds(starts[s], WIN)
            new = (out[s * WIN:(s + 1) * WIN] * wc[s, r, :] * gate).astype(BF16)
            if not first:
                rank = starts[s] - off[s][j] + lax.broadcasted_iota(jnp.int32, (WIN, 1), 0)
                new = jnp.where(rank < cnt[s][j], new, yc[s, r, :])
            yc[s, r, :] = new

    def scatter(s, start, n):
        r = pl.ds(start, n)
        o_ref[pl.ds(s * tm, tm), :] += _dot(pc[s, :, r], yc[s, r, :])

    @pl.when(fits)
    def _packed():
        for s in range(n_sub):
            piece(s, 0)
            piece(s, 1)
            zero_piece(s, 2)
        for s in range(n_sub):
            @pl.when(n_pieces[s] > 2)
            def _third(s=s):
                piece(s, 2)
                zero_piece(s, 3)
        for j in range(n_exp):
            window(j, 0, True)
        for j in range(n_exp):
            most = cnt[0][j]
            for s in range(1, n_sub):
                most = jnp.maximum(most, cnt[s][j])

            @pl.when(most > WIN)
            def _more(j=j, most=most):
                lax.fori_loop(1, (most + WIN - 1) // WIN, lambda p, c: (window(j, p, False), c)[1], 0)
        for s in range(n_sub):
            scatter(s, 0, 2 * PIECE)
        for s in range(n_sub):
            @pl.when(n_pieces[s] > 2)
            def _third_out(s=s):
                scatter(s, 2 * PIECE, PIECE)

    @pl.when(jnp.logical_not(fits))
    def _unpacked():
        for j in range(n_exp):
            for s in range(n_sub):
                rows = pl.ds(s * tm, tm)

                def block(bi, carry, j=j, s=s, rows=rows):
                    slot0 = off[s][j] + bi * PIECE
                    pick, wrow = one_hot_rows(s, slot0, PIECE, [j])
                    out = mlp(_dot(pick, h_ref[rows, :]).astype(BF16), j)
                    ow = (out * wrow * gate).astype(BF16)
                    o_ref[rows, :] += lax.dot_general(pick, ow, tn, preferred_element_type=F32)
                    return carry

                lax.fori_loop(0, (cnt[s][j] + PIECE - 1) // PIECE, block, 0)


def _moe(h2, x1, mod, rank_t, w_t, counts, w_gate, w_up, w_down, ws_gate, ws_up, ws_down, seq):
    T, D = h2.shape
    ns, E, tm = rank_t.shape
    tmo = min(MOE_OUTER, seq)
    n_sub = tmo // tm
    n_exp = MOE_EXPERTS_PER_STEP
    per_b = seq // tmo
    ws = [w_gate.astype(BF16), w_up.astype(BF16), w_down.astype(BF16)]
    sh = [ws_gate.astype(BF16), ws_up.astype(BF16), ws_down.astype(BF16)]
    tok = lambda: pl.BlockSpec((tmo, D), lambda t, e, c: (t, 0))
    sub = lambda: pl.BlockSpec((n_sub, E, tm), lambda t, e, c: (t, 0, 0))
    exp = lambda a: pl.BlockSpec((n_exp,) + a.shape[1:], lambda t, e, c: (e, 0, 0))
    full = lambda a: pl.BlockSpec(a.shape, lambda t, e, c: (0, 0))
    grid_spec = pltpu.PrefetchScalarGridSpec(
        num_scalar_prefetch=1,
        grid=(T // tmo, E // n_exp),
        in_specs=[tok(), tok(), pl.BlockSpec((None, N_MOD, D), lambda t, e, c: (t // per_b, 0, 0)),
                  sub(), sub(), exp(ws[0]), exp(ws[1]), exp(ws[2]), full(sh[0]), full(sh[1]), full(sh[2])],
        out_specs=tok(),
        scratch_shapes=[pltpu.VMEM((n_sub, MOE_ROWS, D), BF16), pltpu.VMEM((n_sub, MOE_ROWS, D), BF16),
                        pltpu.VMEM((n_sub, tm, MOE_ROWS), BF16), pltpu.VMEM((n_sub, MOE_ROWS, 1), F32)],
    )
    return pl.pallas_call(
        functools.partial(_moe_kernel, n_sub=n_sub, tm=tm, n_exp=n_exp),
        grid_spec=grid_spec,
        out_shape=jax.ShapeDtypeStruct((T, D), F32),
        compiler_params=_params(("arbitrary", "arbitrary")),
        name="moe",
    )(counts, h2, x1, mod, rank_t, w_t, *ws, *sh)


def kernel(x, c, w_ada, b_ada, norm_mix_gain, norm_ffn_gain, w_in, s5_lambda_re, s5_lambda_im, s5_log_dt, s5_b_re, s5_b_im, s5_c_re, s5_c_im, s5_d, s5_w_glu, s5_b_glu, q_norm_gain, k_norm_gain, cmp_pe, cmp_w1, cmp_b1, cmp_w2, cmp_b2, w_branch_a, w_branch_b, w_out, w_router, router_bias, w_gate, w_up, w_down, ws_gate, ws_up, ws_down):
    B, S, D = x.shape
    for l in range(w_ada.shape[0]):
        mod = _ada(c, w_ada[l], b_ada[l]).reshape(B, N_MOD, D)
        u, kvc, qt, ks, vst, kwn, vwt, gt, gm = _inproj(x, mod, norm_mix_gain[l:l + 1], w_in[l],
                                                        q_norm_gain[l], k_norm_gain[l])
        tables = _s5_tables(s5_lambda_re[l], s5_lambda_im[l], s5_log_dt[l], s5_b_re[l], s5_b_im[l],
                            s5_c_re[l], s5_c_im[l], s5_d[l])
        ys5 = _s5(u, tables)
        kvc_c = _compress(kvc, cmp_pe[l], cmp_w1[l], cmp_b1[l], cmp_w2[l], cmp_b2[l], k_norm_gain[l, 0])
        o_nsa = _attend(qt, ks, vst, kwn, vwt, gt, kvc_c)
        x1, h2, scores_t = _merge(x, mod, ys5, o_nsa, gm, s5_w_glu[l], s5_b_glu[l], w_branch_a[l],
                                  w_branch_b[l], w_out[l], norm_ffn_gain[l:l + 1], w_router[l])
        rank_t, w_t, cnt = _route(scores_t, router_bias[l])
        counts = cnt[:, :, 0].astype(jnp.int32).reshape(-1)
        x = _moe(h2.reshape(B * S, D), x1.reshape(B * S, D), mod, rank_t, w_t, counts,
                 w_gate[l], w_up[l], w_down[l], ws_gate[l], ws_up[l], ws_down[l], S).reshape(B, S, D)
    return x
```

```python
import functools
import math

import numpy as np
import jax
import jax.numpy as jnp
from jax import lax
from jax.experimental import pallas as pl
from jax.experimental.pallas import tpu as pltpu

F32 = jnp.float32
BF16 = jnp.bfloat16

D_MODEL = 1024
S5_WIDTH = 512
S5_GROUP = 16
S5_GROUPS = S5_WIDTH // S5_GROUP
S5_STATE = 64
N_HEADS = 8
N_KV_HEADS = 2
GQA_GROUP = N_HEADS // N_KV_HEADS
HEAD_DIM = 64
NSA_WIDTH = N_HEADS * HEAD_DIM
KV_WIDTH = 2 * N_KV_HEADS * HEAD_DIM
CMP_BLOCK = 32
CMP_STRIDE = 16
CMP_HIDDEN = 256
SEL_BLOCK = 64
SEL_TOPK = 8
WINDOW = 256
Q_BLOCK = 128
FORCE_BONUS = 1e3
N_EXPERTS = 64
TOP_K = 8
N_EXPERT_GROUPS = 8
TOPK_EXPERT_GROUPS = 4
EXPERT_HIDDEN = 256
SHARED_HIDDEN = 256
ROUTED_SCALE = 2.5
RMS_EPS = 1e-6
NEG_INF = -1e30
N_MOD = 6

LANES = 128
SUBLANES = 8
S5_CHUNK = 16
S5_CW = S5_CHUNK * S5_GROUP
ROW_TILE = 512
SEL_KEY_CHUNK = 512
SEL_UNIT = 128
SEL_SLOTS = (5, 8)
POS_BASE = 64
POS_ROWS = 16
AUG = 128
ROUTE_TILE = 256
MOE_OUTER = 1024
MOE_EXPERTS_PER_STEP = 4
MOE_PIECE = 128
MOE_WIN = 48
MOE_ROWS = 512
RANK_NONE = -float(1 << 20)
VMEM_LIMIT = 56 * 1024 * 1024


def _dot(a, b):
    return jnp.dot(a, b, preferred_element_type=F32)


def _split(a):
    hi = a.astype(BF16)
    lo = (a - hi.astype(F32)).astype(BF16)
    return hi, lo


def _dot3(a, bh, bl):
    ah, al = _split(a)
    return _dot(ah, bh) + (_dot(al, bh) + _dot(ah, bl))


def _segment_transpose(x):
    assert x.shape[-2:] == (SUBLANES, LANES) and LANES // S5_GROUP == SUBLANES
    nd = x.ndim
    i = lax.broadcasted_iota(jnp.int32, x.shape, nd - 2)
    seg = lax.broadcasted_iota(jnp.int32, x.shape, nd - 1) // S5_GROUP
    out = x
    for d in range(1, SUBLANES):
        r = pltpu.roll(pltpu.roll(x, SUBLANES - d, axis=nd - 2), S5_GROUP * d, axis=nd - 1)
        out = jnp.where(seg == ((i + d) & (SUBLANES - 1)), r, out)
    return out


def _to_chunk_major(u, put):
    rows = u.shape[0]
    u3 = u.reshape(rows // SUBLANES, SUBLANES, S5_WIDTH)
    halves = S5_CHUNK // SUBLANES
    for jb in range(S5_WIDTH // LANES):
        t = _segment_transpose(u3[:, :, LANES * jb:LANES * (jb + 1)])
        t = t.reshape(rows // S5_CHUNK, halves, SUBLANES, LANES)
        for hf in range(halves):
            put(slice(SUBLANES * jb, SUBLANES * (jb + 1)), slice(LANES * hf, LANES * (hf + 1)), t[:, hf])


def _from_chunk_major(get, rows):
    halves = S5_CHUNK // SUBLANES
    cols = []
    for jb in range(S5_WIDTH // LANES):
        parts = [_segment_transpose(get(slice(SUBLANES * jb, SUBLANES * (jb + 1)), slice(LANES * hf, LANES * (hf + 1))))
                 for hf in range(halves)]
        cols.append(jnp.stack(parts, axis=1).reshape(rows, LANES))
    return jnp.concatenate(cols, axis=1)


def _params(sem):
    return pltpu.CompilerParams(dimension_semantics=sem, vmem_limit_bytes=VMEM_LIMIT)


def _ada_kernel(c_ref, w_ref, b_ref, o_ref):
    cs = jax.nn.silu(c_ref[...])
    wh, wl = _split(w_ref[...])
    o_ref[...] = _dot3(cs, wh, wl) + b_ref[...]


def _ada(c, w_ada, b_ada):
    B, D = c.shape
    return pl.pallas_call(
        _ada_kernel,
        grid=(N_MOD,),
        in_specs=[pl.BlockSpec((B, D), lambda j: (0, 0)),
                  pl.BlockSpec((D, D), lambda j: (0, j)),
                  pl.BlockSpec((1, D), lambda j: (0, j))],
        out_specs=pl.BlockSpec((B, D), lambda j: (0, j)),
        out_shape=jax.ShapeDtypeStruct((B, N_MOD * D), F32),
        compiler_params=_params(("arbitrary",)),
        name="ada",
    )(c, w_ada, b_ada.reshape(1, N_MOD * D))


def _head_norm(v, bd, gain):
    sq = v * v
    sh, sl = _split(sq)
    ms = _dot(sh, bd) + _dot(sl, bd)
    return v * lax.rsqrt(ms + RMS_EPS) * gain


def _inproj_kernel(x_ref, mod_ref, gain_ref, wm_ref, wt_ref, wgn_ref, wgm_ref, bd_ref, qg_ref, kg_ref,
                   u_ref, kvc_ref, qt_ref, ksa_ref, vst_ref, kwa_ref, vwt_ref, gt_ref, gm_ref):
    x = x_ref[...]
    shift = mod_ref[0:1, :]
    scale = mod_ref[1:2, :]
    y = x * lax.rsqrt(jnp.mean(x * x, axis=-1, keepdims=True) + RMS_EPS)
    h = (y * gain_ref[...]) * (1.0 + scale) + shift
    hb = h.astype(BF16)
    main = _dot(hb, wm_ref[...])
    tm = x.shape[0]

    def put(groups, lanes, block):
        u_ref[:, groups, lanes] = block

    _to_chunk_major(main[:, :S5_WIDTH], put)
    o = S5_WIDTH
    kvc_ref[...] = main[:, o:o + KV_WIDTH]
    o += KV_WIDTH
    kw = N_KV_HEADS * HEAD_DIM
    bd = bd_ref[...]
    ks = _head_norm(main[:, o:o + kw], bd[:kw, :kw], kg_ref[1:2, :]).astype(BF16)
    kwn = _head_norm(main[:, o + kw:o + 2 * kw], bd[:kw, :kw], kg_ref[2:3, :]).astype(BF16)
    pos = pl.program_id(1) * tm + lax.broadcasted_iota(jnp.int32, (tm, 1), 0)
    lane = lax.broadcasted_iota(jnp.int32, (tm, AUG), 1)
    digits = jnp.where(lane == 0, pos // POS_BASE, jnp.where(lane == 1, pos % POS_BASE, 0))
    digits = digits[:, :AUG - HEAD_DIM].astype(F32).astype(BF16)
    onehot = jnp.where(lane == pos // SEL_BLOCK, 1.0, 0.0).astype(BF16)
    for hh in range(N_KV_HEADS):
        ksa_ref[hh, :, 0:HEAD_DIM] = ks[:, hh * HEAD_DIM:(hh + 1) * HEAD_DIM]
        ksa_ref[hh, :, HEAD_DIM:AUG] = digits
        ksa_ref[hh, :, AUG:] = onehot
        kwa_ref[hh, :, 0:HEAD_DIM] = kwn[:, hh * HEAD_DIM:(hh + 1) * HEAD_DIM]
        kwa_ref[hh, :, HEAD_DIM:] = digits

    nt = (((1,), (1,)), ((), ()))
    tt = lax.dot_general(wt_ref[...], hb, nt, preferred_element_type=F32)
    qt = tt[:NSA_WIDTH]
    sq = qt * qt
    sh, sl = _split(sq)
    qn = (qt * lax.rsqrt(_dot(bd, sh) + _dot(bd, sl) + RMS_EPS) * qg_ref[...] * (HEAD_DIM ** -0.5)).astype(BF16)
    QW = GQA_GROUP * Q_BLOCK
    for qb in range(tm // Q_BLOCK):
        for hd in range(N_HEADS):
            hh, g = divmod(hd, GQA_GROUP)
            qt_ref[hh, :, qb * QW + g * Q_BLOCK:qb * QW + (g + 1) * Q_BLOCK] = (
                qn[hd * HEAD_DIM:(hd + 1) * HEAD_DIM, qb * Q_BLOCK:(qb + 1) * Q_BLOCK])
    vst_ref[...] = tt[NSA_WIDTH:NSA_WIDTH + kw].reshape(N_KV_HEADS, HEAD_DIM, tm).astype(BF16)
    vwt_ref[...] = tt[NSA_WIDTH + kw:].reshape(N_KV_HEADS, HEAD_DIM, tm).astype(BF16)
    gn = jax.nn.sigmoid(lax.dot_general(wgn_ref[...], hb, nt, preferred_element_type=F32))
    for qb in range(tm // Q_BLOCK):
        for br in range(3):
            for hd in range(N_HEADS):
                hh, g = divmod(hd, GQA_GROUP)
                r = br * N_HEADS + hd
                gt_ref[hh, br:br + 1, qb * QW + g * Q_BLOCK:qb * QW + (g + 1) * Q_BLOCK] = (
                    gn[r:r + 1, qb * Q_BLOCK:(qb + 1) * Q_BLOCK])
    gm_ref[...] = jax.nn.sigmoid(_dot(hb, wgm_ref[...])).astype(BF16)


def _s5_chunk_spec(tm):
    assert S5_CHUNK == 16 and S5_GROUP == 16, "the segment transposes assume 16 steps x 16 channels"
    return pl.BlockSpec((tm // S5_CHUNK, None, S5_GROUPS, S5_CW), lambda b, i: (i, b, 0, 0))


def _inproj(x, mod, gain, w_in, q_gain, k_gain):
    B, S, D = x.shape
    tm = min(ROW_TILE, S)
    assert S // SEL_BLOCK <= AUG and tm % Q_BLOCK == 0
    kw = N_KV_HEADS * HEAD_DIM
    cols = np.cumsum((0,) + (S5_WIDTH, NSA_WIDTH, KV_WIDTH, KV_WIDTH, KV_WIDTH, 3 * N_HEADS, 2 * D))
    c_u, c_q, c_kvc, c_kvs, c_kvw, c_gn, c_gm = cols[:7]
    sl = lambda a, n: w_in[:, a:a + n]
    wm = jnp.concatenate([sl(c_u, S5_WIDTH), sl(c_kvc, KV_WIDTH), sl(c_kvs, kw), sl(c_kvw, kw)], axis=1).astype(BF16)
    wt = jnp.concatenate([sl(c_q, NSA_WIDTH), sl(c_kvs + kw, kw), sl(c_kvw + kw, kw)], axis=1).T.astype(BF16)
    wgn = sl(c_gn, 3 * N_HEADS).T.astype(BF16)
    wgm = sl(c_gm, 2 * D).astype(BF16)
    seg = np.arange(NSA_WIDTH) // HEAD_DIM
    bd = jnp.asarray((seg[:, None] == seg[None, :]).astype(np.float32) / HEAD_DIM, BF16)
    qg = jnp.tile(q_gain, N_HEADS).reshape(NSA_WIDTH, 1)
    kg = jnp.tile(k_gain, (1, N_KV_HEADS))
    nq = tm // Q_BLOCK * GQA_GROUP * Q_BLOCK
    row = lambda w: pl.BlockSpec((None, tm, w), lambda b, i: (b, i, 0))
    full = lambda a: pl.BlockSpec(a.shape, lambda b, i: (0,) * a.ndim)
    rows4 = lambda w: pl.BlockSpec((None, N_KV_HEADS, tm, w), lambda b, i: (b, 0, i, 0))
    cols4 = lambda r, w: pl.BlockSpec((None, N_KV_HEADS, r, w), lambda b, i: (b, 0, 0, i))
    nqt = S // Q_BLOCK * GQA_GROUP * Q_BLOCK
    return pl.pallas_call(
        _inproj_kernel,
        grid=(B, S // tm),
        in_specs=[row(D), pl.BlockSpec((None, N_MOD, D), lambda b, i: (b, 0, 0)),
                  full(gain), full(wm), full(wt), full(wgn), full(wgm), full(bd), full(qg), full(kg)],
        out_specs=[_s5_chunk_spec(tm), row(KV_WIDTH), cols4(HEAD_DIM, nq), rows4(2 * AUG), cols4(HEAD_DIM, tm),
                   rows4(AUG), cols4(HEAD_DIM, tm), cols4(3, nq), row(2 * D)],
        out_shape=[jax.ShapeDtypeStruct((S // S5_CHUNK, B, S5_GROUPS, S5_CW), F32),
                   jax.ShapeDtypeStruct((B, S, KV_WIDTH), F32),
                   jax.ShapeDtypeStruct((B, N_KV_HEADS, HEAD_DIM, nqt), BF16),
                   jax.ShapeDtypeStruct((B, N_KV_HEADS, S, 2 * AUG), BF16),
                   jax.ShapeDtypeStruct((B, N_KV_HEADS, HEAD_DIM, S), BF16),
                   jax.ShapeDtypeStruct((B, N_KV_HEADS, S, AUG), BF16),
                   jax.ShapeDtypeStruct((B, N_KV_HEADS, HEAD_DIM, S), BF16),
                   jax.ShapeDtypeStruct((B, N_KV_HEADS, 3, nqt), F32),
                   jax.ShapeDtypeStruct((B, S, 2 * D), BF16)],
        compiler_params=_params(("arbitrary", "arbitrary")),
        name="inproj",
    )(x, mod, gain, wm, wt, wgn, wgm, bd, qg, kg)


def _s5_tables(lam_re, lam_im, log_dt, b_re, b_im, c_re, c_im, d_skip):
    L, P, N, G = S5_CHUNK, S5_GROUP, S5_STATE, S5_GROUPS
    hp = lax.Precision.HIGHEST
    lr, li = lam_re.astype(F32), lam_im.astype(F32)
    dt = jnp.exp(log_dt.astype(F32))[:, None]
    mag = jnp.exp(lr * dt)
    abar_re, abar_im = mag * jnp.cos(li * dt), mag * jnp.sin(li * dt)
    num_re, num_im = abar_re - 1.0, abar_im
    den = lr * lr + li * li
    coef_re = (num_re * lr + num_im * li) / den
    coef_im = (num_im * lr - num_re * li) / den
    br, bi = b_re.astype(F32), b_im.astype(F32)
    bbar_re = coef_re[..., None] * br - coef_im[..., None] * bi
    bbar_im = coef_re[..., None] * bi + coef_im[..., None] * br
    k = jnp.arange(L + 1, dtype=F32)[:, None, None]
    pmag = jnp.exp(lr * dt * k)
    pre, pim = pmag * jnp.cos(li * dt * k), pmag * jnp.sin(li * dt * k)
    cr, ci = c_re.astype(F32), c_im.astype(F32)
    ca_re = cr[None] * pre[:, :, None, :] - ci[None] * pim[:, :, None, :]
    ca_im = cr[None] * pim[:, :, None, :] + ci[None] * pre[:, :, None, :]
    kern = (jnp.einsum('kgpn,gnq->gkqp', ca_re[:L], bbar_re, precision=hp)
            - jnp.einsum('kgpn,gnq->gkqp', ca_im[:L], bbar_im, precision=hp))
    s_i = np.arange(L)[:, None]
    t_i = np.arange(L)[None, :]
    tau = np.clip(t_i - s_i, 0, L - 1)
    causal = jnp.asarray((t_i >= s_i).astype(np.float32))
    mt = kern[:, tau] * causal[None, :, :, None, None]
    mt = mt.transpose(0, 1, 3, 2, 4).reshape(G, L * P, L * P)
    rev = np.arange(L - 1, -1, -1)
    ab_re = pre[rev][..., None] * bbar_re[None] - pim[rev][..., None] * bbar_im[None]
    ab_im = pre[rev][..., None] * bbar_im[None] + pim[rev][..., None] * bbar_re[None]
    ws = jnp.concatenate([ab_re, ab_im], axis=2)
    ws = ws.transpose(1, 0, 3, 2).reshape(G, L * P, 2 * N)
    wo = jnp.concatenate([ca_re[1:], -ca_im[1:]], axis=3)
    wo = wo.transpose(1, 3, 0, 2).reshape(G, 2 * N, L * P)
    al = jnp.stack([pre[L], pim[L]], axis=1)
    dv = jnp.tile(d_skip.astype(F32), (1, L)).reshape(G, 1, L * P)
    return mt, ws, wo, al, dv


def _s5_kernel(u_ref, mth_ref, mtl_ref, wsh_ref, wsl_ref, woh_ref, wol_ref, al_ref, dv_ref,
               y_ref, vr_scr, vi_scr, xr_scr, xi_scr, *, n_chunks, bsz):
    N = S5_STATE
    u = u_ref[...]
    uh, ul = _split(u)
    y = _dot(uh, mth_ref[...]) + (_dot(ul, mth_ref[...]) + _dot(uh, mtl_ref[...]))
    v = _dot(uh, wsh_ref[...]) + (_dot(ul, wsh_ref[...]) + _dot(uh, wsl_ref[...]))
    vr_scr[...] = v[:, :N]
    vi_scr[...] = v[:, N:]
    a_r = al_ref[0:1, :]
    a_i = al_ref[1:2, :]

    def step(c, x):
        xr, xi = x
        r = pl.ds(pl.multiple_of(c * bsz, bsz), bsz)
        xr_scr[r, :] = xr
        xi_scr[r, :] = xi
        return (a_r * xr - a_i * xi + vr_scr[r, :], a_r * xi + a_i * xr + vi_scr[r, :])

    zero = jnp.zeros((bsz, N), F32)
    lax.fori_loop(0, n_chunks, step, (zero, zero), unroll=8)
    xp = jnp.concatenate([xr_scr[...], xi_scr[...]], axis=1)
    y = y + _dot3(xp, woh_ref[...], wol_ref[...])
    y_ref[...] = y + dv_ref[...] * u


def _s5(ug, tables):
    nc, B, G, cw = ug.shape
    N = S5_STATE
    mt, ws, wo, al, dv = tables
    mth, mtl = _split(mt)
    wsh, wsl = _split(ws)
    woh, wol = _split(wo)
    grp = lambda a: pl.BlockSpec((None,) + a.shape[1:], lambda g: (g,) + (0,) * (a.ndim - 1))
    ugt = ug.reshape(nc * B, G, cw).transpose(1, 0, 2)
    y = pl.pallas_call(
        functools.partial(_s5_kernel, n_chunks=nc, bsz=B),
        grid=(G,),
        in_specs=[grp(ugt), grp(mth), grp(mtl), grp(wsh), grp(wsl), grp(woh), grp(wol), grp(al), grp(dv)],
        out_specs=grp(ugt),
        out_shape=jax.ShapeDtypeStruct(ugt.shape, F32),
        scratch_shapes=[pltpu.VMEM((nc * B, N), F32)] * 4,
        compiler_params=_params(("arbitrary",)),
        name="s5",
    )(ugt, mth, mtl, wsh, wsl, woh, wol, al, dv)
    return y.transpose(1, 0, 2).reshape(nc, B, G, cw)


def _compress_kernel(x_ref, pe_ref, w1_ref, b1_ref, w2_ref, b2_ref, kg_ref, o_ref, *, n_rows):
    x = x_ref[...]
    half = CMP_STRIDE * HEAD_DIM
    a = _dot((x + pe_ref[0:1, :]).astype(BF16), w1_ref[:half, :])
    b = _dot((x + pe_ref[1:2, :]).astype(BF16), w1_ref[half:, :])
    hid = jax.nn.gelu(a + pltpu.roll(b, n_rows - 1, axis=0) + b1_ref[...])
    out = _dot(hid.astype(BF16), w2_ref[...]) + b2_ref[...]
    normed = out * lax.rsqrt(jnp.mean(out * out, axis=-1, keepdims=True) + RMS_EPS) * kg_ref[...]
    is_key = pl.program_id(1) < N_KV_HEADS
    o_ref[...] = jnp.where(is_key, normed, out).astype(BF16)


def _compress(kv_c, cmp_pe, cmp_w1, cmp_b1, cmp_w2, cmp_b2, k_gain0):
    B, S, _ = kv_c.shape
    nr = S // CMP_STRIDE
    half = CMP_STRIDE * HEAD_DIM
    nj = 2 * N_KV_HEADS
    xc = kv_c.reshape(B, nr, CMP_STRIDE, nj, HEAD_DIM).transpose(0, 3, 1, 2, 4).reshape(B, nj, nr, half)
    pe = cmp_pe.reshape(2, 2, half)
    w1 = cmp_w1.astype(BF16)
    w2 = cmp_w2.astype(BF16)
    b1 = cmp_b1.reshape(2, 1, CMP_HIDDEN)
    b2 = cmp_b2.reshape(2, 1, HEAD_DIM)
    kv = lambda a: pl.BlockSpec((None,) + a.shape[1:], lambda b, j: (j // N_KV_HEADS,) + (0,) * (a.ndim - 1))
    return pl.pallas_call(
        functools.partial(_compress_kernel, n_rows=nr),
        grid=(B, nj),
        in_specs=[pl.BlockSpec((None, None, nr, half), lambda b, j: (b, j, 0, 0)),
                  kv(pe), kv(w1), kv(b1), kv(w2), kv(b2),
                  pl.BlockSpec((1, HEAD_DIM), lambda b, j: (0, 0))],
        out_specs=pl.BlockSpec((None, None, nr, HEAD_DIM), lambda b, j: (b, j, 0, 0)),
        out_shape=jax.ShapeDtypeStruct((B, nj, nr, HEAD_DIM), BF16),
        compiler_params=_params(("arbitrary", "arbitrary")),
        name="compress",
    )(xc, pe, w1, b1, w2, b2, k_gain0.reshape(1, HEAD_DIM))


def _softmax_cols(s, mask):
    s = jnp.where(mask, s, NEG_INF)
    m = jnp.max(s, axis=0, keepdims=True)
    p = jnp.where(mask, jnp.exp(s - m), 0.0)
    return p / jnp.maximum(jnp.sum(p, axis=0, keepdims=True), 1e-20)


def _attend_kernel(qt_ref, kc_ref, vct_ref, ks_ref, vst_ref, kw_ref, vwt_ref, g_ref, slope_ref, ovt_ref,
                   o_ref, qa_scr, sa_scr, sb_scr, *, n_sel, n_pick, n_cmp_rows):
    QW = GQA_GROUP * Q_BLOCK
    heads = range(N_KV_HEADS)
    i = pl.program_id(1)
    t0 = i * Q_BLOCK
    tq = t0 + (lax.broadcasted_iota(jnp.int32, (1, QW), 1) & (Q_BLOCK - 1))

    r = lax.broadcasted_iota(jnp.int32, (POS_ROWS, QW), 0)
    qa = []
    for h in heads:
        slope = slope_ref[h]
        qa_scr[h, 0:HEAD_DIM, :] = qt_ref[h]
        qa_scr[h, HEAD_DIM:HEAD_DIM + POS_ROWS, :] = jnp.where(
            r == 0, slope * POS_BASE, jnp.where(r == 1, slope, 0.0)).astype(BF16)
        qa_scr[h, HEAD_DIM + POS_ROWS:AUG, :] = jnp.zeros((AUG - HEAD_DIM - POS_ROWS, QW), BF16)
        qa.append(qa_scr[h, 0:AUG, :])

    def compressed(h, rows):
        sc = _dot(kc_ref[h, 0:rows, :], qa[h])
        cpos = lax.broadcasted_iota(jnp.int32, (rows, QW), 0) * CMP_STRIDE + (CMP_BLOCK - 1)
        p_c = _softmax_cols(sc, cpos <= tq)
        o_c = _dot(vct_ref[h, :, 0:rows], p_c.astype(BF16))
        psum = p_c[:, 0:Q_BLOCK]
        for g in range(1, GQA_GROUP):
            psum = psum + p_c[:, g * Q_BLOCK:(g + 1) * Q_BLOCK]
        ph, pl_ = _split(psum)
        return o_c, _dot(ovt_ref[:, 0:rows], ph) + _dot(ovt_ref[:, 0:rows], pl_)

    half = n_cmp_rows // 2
    if half % LANES == 0:
        early = (t0 + Q_BLOCK - CMP_BLOCK) // CMP_STRIDE < half
        cmp_out = lax.cond(early, lambda _: [compressed(h, half) for h in heads],
                           lambda _: [compressed(h, n_cmp_rows) for h in heads], 0)
    else:
        cmp_out = [compressed(h, n_cmp_rows) for h in heads]

    def select(h, imp):
        jb = lax.broadcasted_iota(jnp.int32, (n_sel, Q_BLOCK), 0)
        cur = (t0 + lax.broadcasted_iota(jnp.int32, (1, Q_BLOCK), 1)) // SEL_BLOCK
        forced = (jb == 0) | (jb == cur) | (jb == cur - 1)
        imp = jnp.where(forced, imp + FORCE_BONUS, imp)
        imp = jnp.where(jb <= cur, imp, -1.0)
        bias = jnp.full((n_sel, Q_BLOCK), NEG_INF, F32)
        for _ in range(n_pick):
            m = jnp.max(imp, axis=0, keepdims=True)
            first = jnp.min(jnp.where(imp == m, jb, n_sel), axis=0, keepdims=True)
            hit = jb == first
            bias = jnp.where(hit, 0.0, bias)
            imp = jnp.where(hit, -jnp.inf, imp)
        per_unit = SEL_UNIT // SEL_BLOCK
        n_units = n_sel // per_unit
        picked = jnp.where(bias == 0.0, 1.0, 0.0).astype(BF16)
        in_unit = (lax.broadcasted_iota(jnp.int32, (n_units, n_sel), 1) // per_unit
                   == lax.broadcasted_iota(jnp.int32, (n_units, n_sel), 0))
        hits = _dot(jnp.where(in_unit, 1.0, 0.0).astype(BF16), picked)
        live = jnp.where(jnp.max(hits, axis=1, keepdims=True) > 0.5, 1.0, 0.0)
        live = jnp.broadcast_to(live, (n_units, LANES))
        earlier = (lax.broadcasted_iota(jnp.int32, (n_units, n_units), 1)
                   < lax.broadcasted_iota(jnp.int32, (n_units, n_units), 0))
        before = _dot(jnp.where(earlier, 1.0, 0.0).astype(BF16), live.astype(BF16))
        slot = lax.broadcasted_iota(jnp.int32, (n_units, LANES), 1).astype(F32)
        in_slot = jnp.where(before == slot, live, 0.0).astype(BF16)
        r8 = lax.broadcasted_iota(jnp.int32, (SUBLANES, n_units), 0)
        c8 = lax.broadcasted_iota(jnp.int32, (SUBLANES, n_units), 1)
        pick_rows = jnp.where(r8 == 0, c8, jnp.where(r8 == 1, 1, 0)).astype(F32).astype(BF16)
        ids = _dot(pick_rows, in_slot)
        n_live = (before[n_units - 1:n_units, 0:1] + live[n_units - 1:n_units, 0:1])[0, 0]
        bias = jnp.concatenate([bias] * GQA_GROUP, axis=1).astype(BF16)
        qa_scr[h, AUG:, :] = jnp.concatenate([bias, jnp.zeros((AUG - n_sel, QW), BF16)], axis=0)
        return ids, n_live

    def window(h):
        WK = WINDOW + Q_BLOCK
        w0 = pl.multiple_of(jnp.maximum(t0 - WINDOW, 0), Q_BLOCK)
        sw = _dot(kw_ref[h, pl.ds(w0, WK), :], qa[h])
        dist = tq - (w0 + lax.broadcasted_iota(jnp.int32, (WK, QW), 0))
        in_win = (dist | (WINDOW - 1 - dist)) >= 0
        p_w = _softmax_cols(sw, in_win)
        return _dot(vwt_ref[h, :, pl.ds(w0, WK)], p_w.astype(BF16))

    lists = [select(h, cmp_out[h][1]) for h in heads]
    o_win = [window(h) for h in heads]

    def gathered(h, ids, n_slots):
        units = [ids[0, s].astype(jnp.int32) for s in range(n_slots)]
        starts = [pl.multiple_of(u * SEL_UNIT, SEL_UNIT) for u in units]
        keys = jnp.concatenate([ks_ref[h, pl.ds(st, SEL_UNIT), :] for st in starts], axis=0)
        s = _dot(keys, qa_scr[h])
        krow = lax.broadcasted_iota(jnp.int32, (SEL_UNIT, QW), 0)
        seen = jnp.concatenate(
            [st + krow <= jnp.where(ids[1, si] > 0.5, tq, -1) for si, st in enumerate(starts)], axis=0)
        s = jnp.where(seen, s, NEG_INF)
        p = jnp.exp(s - jnp.max(s, axis=0, keepdims=True))
        vals = jnp.concatenate([vst_ref[h, :, pl.ds(st, SEL_UNIT)] for st in starts], axis=1)
        return _dot(vals, p.astype(BF16)) / jnp.maximum(jnp.sum(p, axis=0, keepdims=True), 1e-20)

    KC = SEL_KEY_CHUNK
    n_chunks = (t0 + Q_BLOCK + KC - 1) // KC
    last = n_chunks - 1

    def swept(h):
        def scores_to(ref, j):
            k0 = pl.multiple_of(j * KC, KC)
            ref[...] = _dot(ks_ref[h, pl.ds(k0, KC), :], qa_scr[h])

        def accumulate(carry, j, s, p_of):
            m, l, acc = carry
            k0 = pl.multiple_of(j * KC, KC)
            m_new = jnp.maximum(m, jnp.max(s, axis=0, keepdims=True))
            alpha = jnp.exp(m - m_new)
            p = p_of(jnp.exp(s - m_new))
            l = alpha * l + jnp.sum(p, axis=0, keepdims=True)
            acc = alpha * acc + _dot(vst_ref[h, :, pl.ds(k0, KC)], p.astype(BF16))
            return m_new, l, acc

        keep = lambda p: p

        def two_chunks(i, state):
            j = 2 * i
            scores_to(sb_scr, j + 1)
            state = accumulate(state, j, sa_scr[...], keep)
            scores_to(sa_scr, j + 2)
            return accumulate(state, j + 1, sb_scr[...], keep)

        def odd_chunk(state):
            state = accumulate(state, last - 1, sa_scr[...], keep)
            scores_to(sa_scr, last)
            return state

        scores_to(sa_scr, 0)
        init = (jnp.full((1, QW), NEG_INF, F32), jnp.zeros((1, QW), F32), jnp.zeros((HEAD_DIM, QW), F32))
        state = lax.fori_loop(0, last // 2, two_chunks, init)
        state = lax.cond(last % 2 == 1, odd_chunk, lambda st: st, state)
        visible = (last * KC + lax.broadcasted_iota(jnp.int32, (KC, QW), 0)) <= tq
        _, l_s, acc_s = accumulate(state, last, jnp.where(visible, sa_scr[...], NEG_INF),
                                   lambda p: jnp.where(visible, p, 0.0))
        return acc_s / jnp.maximum(l_s, 1e-20)

    few, some = SEL_SLOTS
    outs = []
    for h in heads:
        ids, n_live = lists[h]
        o_s = lax.cond(n_live <= few, lambda _: gathered(h, ids, few),
                       lambda _: lax.cond(n_live <= some, lambda _: gathered(h, ids, some), lambda _: swept(h), 0), 0)
        o = g_ref[h, 0:1, :] * cmp_out[h][0] + g_ref[h, 1:2, :] * o_s + g_ref[h, 2:3, :] * o_win[h]
        for g in range(0, GQA_GROUP, 2):
            sq = jnp.concatenate([o[:, g * Q_BLOCK:(g + 1) * Q_BLOCK], o[:, (g + 1) * Q_BLOCK:(g + 2) * Q_BLOCK]],
                                 axis=0)
            outs.append(sq.T)
    o_ref[...] = jnp.concatenate(outs, axis=1).astype(BF16)


def _attend(qt, ks, vst, kwn, vwt, gt, kvc_c):
    B, _, S, _ = ks.shape
    nqb = S // Q_BLOCK
    QW = GQA_GROUP * Q_BLOCK
    n_sel = S // SEL_BLOCK
    n_pick = min(SEL_TOPK, n_sel)
    nr = S // CMP_STRIDE
    n_cmp = (S - CMP_BLOCK) // CMP_STRIDE + 1
    assert n_sel % 16 == 0 and n_sel <= AUG

    cpos = np.arange(nr) * CMP_STRIDE + CMP_BLOCK - 1
    digits = np.zeros((nr, AUG - HEAD_DIM), np.float32)
    digits[:, 0] = cpos // POS_BASE
    digits[:, 1] = cpos % POS_BASE
    kc = kvc_c[:, :N_KV_HEADS]
    kc = jnp.concatenate([kc, jnp.broadcast_to(jnp.asarray(digits, BF16), kc.shape[:2] + digits.shape)], axis=-1)
    vct = kvc_c[:, N_KV_HEADS:].transpose(0, 1, 3, 2)
    slopes = 2.0 ** (-8.0 * np.arange(1, N_HEADS + 1) / N_HEADS)
    slope_t = jnp.asarray(np.repeat(slopes.reshape(N_KV_HEADS, GQA_GROUP), Q_BLOCK, axis=1)
                          .reshape(N_KV_HEADS, 1, QW), F32)
    cmp_start = np.arange(n_cmp) * CMP_STRIDE
    sel_start = np.arange(n_sel) * SEL_BLOCK
    ov = ((cmp_start[:, None] <= sel_start[None, :] + SEL_BLOCK - 1)
          & (cmp_start[:, None] + CMP_BLOCK - 1 >= sel_start[None, :])).astype(np.float32)
    ovt = np.zeros((n_sel, nr), np.float32)
    ovt[:, :n_cmp] = ov.T
    ovt = jnp.asarray(ovt, BF16)

    H = N_KV_HEADS
    per_bh = lambda r, c: pl.BlockSpec((None, H, r, c), lambda b, i: (b, 0, 0, 0))
    per_q = lambda r: pl.BlockSpec((None, H, r, QW), lambda b, i: (b, 0, 0, i))
    return pl.pallas_call(
        functools.partial(_attend_kernel, n_sel=n_sel, n_pick=n_pick, n_cmp_rows=nr),
        grid=(B, nqb),
        in_specs=[per_q(HEAD_DIM), per_bh(nr, AUG), per_bh(HEAD_DIM, nr),
                  per_bh(S, 2 * AUG), per_bh(HEAD_DIM, S), per_bh(S, AUG), per_bh(HEAD_DIM, S),
                  per_q(3),
                  pl.BlockSpec((H, 1, QW), lambda b, i: (0, 0, 0)),
                  pl.BlockSpec((n_sel, nr), lambda b, i: (0, 0))],
        out_specs=pl.BlockSpec((None, Q_BLOCK, NSA_WIDTH), lambda b, i: (b, i, 0)),
        out_shape=jax.ShapeDtypeStruct((B, S, NSA_WIDTH), BF16),
        scratch_shapes=[pltpu.VMEM((H, 2 * AUG, QW), BF16), pltpu.VMEM((SEL_KEY_CHUNK, QW), F32),
                        pltpu.VMEM((SEL_KEY_CHUNK, QW), F32)],
        compiler_params=_params(("arbitrary", "arbitrary")),
        name="attend",
    )(qt, kc, vct, ks, vst, kwn, vwt, gt, slope_t, ovt)


def _merge_kernel(x_ref, mod_ref, ys_ref, on_ref, gm_ref, wglu_ref, bglu_ref, wa_ref, wb_ref, wo_ref,
                  gain_ref, wrh_ref, wrl_ref, x1_ref, h2_ref, sc_ref):
    D = D_MODEL
    z = jax.nn.gelu(_from_chunk_major(lambda groups, lanes: ys_ref[:, groups, lanes], x_ref.shape[0]))
    glu = z * jax.nn.sigmoid(_dot(z.astype(BF16), wglu_ref[...]) + bglu_ref[...])
    ya = _dot(glu.astype(BF16), wa_ref[...])
    yb = _dot(on_ref[...], wb_ref[...])
    merged = gm_ref[:, :D].astype(F32) * ya + gm_ref[:, D:].astype(F32) * yb
    x1 = x_ref[...] + mod_ref[2:3, :] * _dot(merged.astype(BF16), wo_ref[...])
    x1_ref[...] = x1
    y = x1 * lax.rsqrt(jnp.mean(x1 * x1, axis=-1, keepdims=True) + RMS_EPS)
    h2 = (y * gain_ref[...]) * (1.0 + mod_ref[4:5, :]) + mod_ref[3:4, :]
    h2_ref[...] = h2.astype(BF16)
    hh, hl = _split(h2)
    nt = (((1,), (1,)), ((), ()))
    dg = lambda a, b: lax.dot_general(a, b, nt, preferred_element_type=F32)
    logits = dg(wrh_ref[...], hh) + (dg(wrh_ref[...], hl) + dg(wrl_ref[...], hh))
    sc_ref[...] = jax.nn.sigmoid(logits)


def _merge(x, mod, ys5, o_nsa, gm, w_glu, b_glu, w_a, w_b, w_out, gain_f, w_router):
    B, S, D = x.shape
    tm = min(ROW_TILE, S)
    wrh, wrl = _split(w_router.T)
    ws = [w_glu.astype(BF16), b_glu.reshape(1, -1), w_a.astype(BF16), w_b.astype(BF16), w_out.astype(BF16),
          gain_f, wrh, wrl]
    row = lambda w: pl.BlockSpec((None, tm, w), lambda b, i: (b, i, 0))
    full = lambda a: pl.BlockSpec(a.shape, lambda b, i: (0,) * a.ndim)
    return pl.pallas_call(
        _merge_kernel,
        grid=(B, S // tm),
        in_specs=[row(D), pl.BlockSpec((None, N_MOD, D), lambda b, i: (b, 0, 0)),
                  _s5_chunk_spec(tm), row(NSA_WIDTH), row(2 * D)] + [full(a) for a in ws],
        out_specs=[row(D), row(D), pl.BlockSpec((None, N_EXPERTS, tm), lambda b, i: (b, 0, i))],
        out_shape=[jax.ShapeDtypeStruct((B, S, D), F32), jax.ShapeDtypeStruct((B, S, D), BF16),
                   jax.ShapeDtypeStruct((B, N_EXPERTS, S), F32)],
        compiler_params=_params(("arbitrary", "arbitrary")),
        name="merge",
    )(x, mod, ys5, o_nsa, gm, *ws)


def _first_argmax_rows(v, idx, n):
    m = jnp.max(v, axis=0, keepdims=True)
    first = jnp.min(jnp.where(v == m, idx, n), axis=0, keepdims=True)
    return idx == first, m


def _route_kernel(sc_ref, bias_ref, tri_ref, rank_ref, w_ref, cnt_ref):
    E, NG = N_EXPERTS, N_EXPERT_GROUPS
    GS = E // NG
    sc = sc_ref[...]
    TM = sc.shape[1]
    sel = sc + bias_ref[...]
    i8 = lax.broadcasted_iota(jnp.int32, (GS, TM), 0)
    gscore = []
    for g in range(NG):
        blk = sel[g * GS:(g + 1) * GS, :]
        hit, m1 = _first_argmax_rows(blk, i8, GS)
        m2 = jnp.max(jnp.where(hit, -jnp.inf, blk), axis=0, keepdims=True)
        gscore.append(m1 + m2)
    gscore = jnp.concatenate(gscore, axis=0)
    ig = lax.broadcasted_iota(jnp.int32, (NG, TM), 0)
    gmask = jnp.zeros((NG, TM), F32)
    for _ in range(TOPK_EXPERT_GROUPS):
        hit, _m = _first_argmax_rows(gscore, ig, NG)
        gmask = jnp.where(hit, 1.0, gmask)
        gscore = jnp.where(hit, -jnp.inf, gscore)
    emask = jnp.concatenate([jnp.broadcast_to(gmask[g:g + 1, :], (GS, TM)) for g in range(NG)], axis=0)
    cand = jnp.where(emask > 0.5, sel, NEG_INF)
    ie = lax.broadcasted_iota(jnp.int32, (E, TM), 0)
    chosen = jnp.zeros((E, TM), F32)
    for _ in range(TOP_K):
        hit, _m = _first_argmax_rows(cand, ie, E)
        chosen = jnp.where(hit, 1.0, chosen)
        cand = jnp.where(hit, -jnp.inf, cand)
    w = chosen * sc
    w = w / jnp.sum(w, axis=0, keepdims=True) * ROUTED_SCALE
    sub = rank_ref.shape[-1]
    for q in range(TM // sub):
        cols = slice(q * sub, (q + 1) * sub)
        cb = chosen[:, cols].astype(BF16)
        prefix = _dot(cb, tri_ref[...])
        rank_ref[q] = jnp.where(chosen[:, cols] > 0.5, prefix, RANK_NONE)
        w_ref[q] = w[:, cols]
        cnt_ref[q] = _dot(cb, jnp.ones((sub, LANES), BF16))


def _route(scores_t, router_bias):
    B, E, S = scores_t.shape
    TM = min(ROUTE_TILE, S)
    per_step = min(MOE_OUTER, S) // TM
    nj = S // (TM * per_step)
    ns = B * nj * per_step
    tri = jnp.asarray(np.triu(np.ones((TM, TM), np.float32), k=1), BF16)
    tile = lambda w: pl.BlockSpec((per_step, E, w), lambda b, j: (b * nj + j, 0, 0))
    return pl.pallas_call(
        _route_kernel,
        grid=(B, nj),
        in_specs=[pl.BlockSpec((None, E, TM * per_step), lambda b, j: (b, 0, j)),
                  pl.BlockSpec((E, 1), lambda b, j: (0, 0)),
                  pl.BlockSpec((TM, TM), lambda b, j: (0, 0))],
        out_specs=[tile(TM), tile(TM), tile(LANES)],
        out_shape=[jax.ShapeDtypeStruct((ns, E, TM), F32), jax.ShapeDtypeStruct((ns, E, TM), F32),
                   jax.ShapeDtypeStruct((ns, E, LANES), F32)],
        compiler_params=_params(("arbitrary", "arbitrary")),
        name="route",
    )(scores_t, router_bias.reshape(E, 1), tri)


def _moe_kernel(cnt_ref, h_ref, x1_ref, mod_ref, rank_ref, w_ref, wg_ref, wu_ref, wd_ref,
                sg_ref, su_ref, sd_ref, o_ref, xc, yc, pc, wc, *, n_sub, tm, n_exp):
    to = pl.program_id(0)
    eb = pl.program_id(1)
    gate = mod_ref[5:6, :]

    @pl.when(eb == 0)
    def _shared():
        for s in range(n_sub):
            rows = pl.ds(s * tm, tm)
            hs = h_ref[rows, :]
            hid = jax.nn.silu(_dot(hs, sg_ref[...])) * _dot(hs, su_ref[...])
            o_ref[rows, :] = x1_ref[rows, :] + gate * _dot(hid.astype(BF16), sd_ref[...])

    PIECE, WIN = MOE_PIECE, MOE_WIN
    tn = (((0,), (0,)), ((), ()))
    experts = [eb * n_exp + j for j in range(n_exp)]
    cnt = [[cnt_ref[(to * n_sub + s) * N_EXPERTS + e] for e in experts] for s in range(n_sub)]
    off = []
    for s in range(n_sub):
        o = [jnp.int32(0)]
        for j in range(n_exp):
            o.append(o[-1] + ((cnt[s][j] + 15) // 16) * 16)
        off.append(o)
    n_pieces = [(off[s][n_exp] + PIECE - 1) // PIECE for s in range(n_sub)]
    fits = n_pieces[0] < MOE_ROWS // PIECE
    for s in range(1, n_sub):
        fits = jnp.logical_and(fits, n_pieces[s] < MOE_ROWS // PIECE)

    def one_hot_rows(s, base, rows_n, js):
        slot = (lax.broadcasted_iota(jnp.int32, (rows_n, tm), 0) + base).astype(F32)
        pick = jnp.zeros((rows_n, tm), F32)
        wacc = jnp.zeros((rows_n, tm), F32)
        for j in js:
            hit = (slot - off[s][j].astype(F32)) == rank_ref[s, pl.ds(experts[j], 1), :]
            pick = jnp.where(hit, 1.0, pick)
            wacc = jnp.where(hit, w_ref[s, pl.ds(experts[j], 1), :], wacc)
        return pick.astype(BF16), jnp.sum(wacc, axis=1, keepdims=True)

    def mlp(xg, j):
        hid = jax.nn.silu(_dot(xg, wg_ref[j])) * _dot(xg, wu_ref[j])
        return _dot(hid.astype(BF16), wd_ref[j])

    def aligned(q, n):
        return q * n if isinstance(q, int) else pl.multiple_of(q * n, n)

    def piece(s, q):
        r = pl.ds(aligned(q, PIECE), PIECE)
        pick, wrow = one_hot_rows(s, q * PIECE, PIECE, range(n_exp))
        xc[s, r, :] = _dot(pick, h_ref[pl.ds(s * tm, tm), :]).astype(BF16)
        pc[s, :, r] = pick.T
        wc[s, r, :] = wrow
        yc[s, r, :] = jnp.zeros((PIECE, D_MODEL), BF16)

    def zero_piece(s, q):
        r = pl.ds(q * PIECE, PIECE)
        xc[s, r, :] = jnp.zeros((PIECE, D_MODEL), BF16)
        yc[s, r, :] = jnp.zeros((PIECE, D_MODEL), BF16)
        pc[s, :, r] = jnp.zeros((tm, PIECE), BF16)
        wc[s, r, :] = jnp.zeros((PIECE, 1), F32)

    def window(j, p, first):
        starts = [pl.multiple_of(jnp.minimum(off[s][j] + p * WIN, (n_pieces[s] + 1) * PIECE - WIN), 16)
                  for s in range(n_sub)]
        xg = jnp.concatenate([xc[s, pl.ds(starts[s], WIN), :] for s in range(n_sub)], axis=0)
        out = mlp(xg, j)
        for s in range(n_sub):
            r = pl.ds(starts[s], WIN)
            new = (out[s * WIN:(s + 1) * WIN] * wc[s, r, :] * gate).astype(BF16)
            if not first:
                rank = starts[s] - off[s][j] + lax.broadcasted_iota(jnp.int32, (WIN, 1), 0)
                new = jnp.where(rank < cnt[s][j], new, yc[s, r, :])
            yc[s, r, :] = new

    def scatter(s, start, n):
        r = pl.ds(start, n)
        o_ref[pl.ds(s * tm, tm), :] += _dot(pc[s, :, r], yc[s, r, :])

    @pl.when(fits)
    def _packed():
        for s in range(n_sub):
            piece(s, 0)
            piece(s, 1)
            zero_piece(s, 2)
        for s in range(n_sub):
            @pl.when(n_pieces[s] > 2)
            def _third(s=s):
                piece(s, 2)
                zero_piece(s, 3)
        for j in range(n_exp):
            window(j, 0, True)
        for j in range(n_exp):
            most = cnt[0][j]
            for s in range(1, n_sub):
                most = jnp.maximum(most, cnt[s][j])

            @pl.when(most > WIN)
            def _more(j=j, most=most):
                lax.fori_loop(1, (most + WIN - 1) // WIN, lambda p, c: (window(j, p, False), c)[1], 0)
        for s in range(n_sub):
            scatter(s, 0, 2 * PIECE)
        for s in range(n_sub):
            @pl.when(n_pieces[s] > 2)
            def _third_out(s=s):
                scatter(s, 2 * PIECE, PIECE)

    @pl.when(jnp.logical_not(fits))
    def _unpacked():
        for j in range(n_exp):
            for s in range(n_sub):
                rows = pl.ds(s * tm, tm)

                def block(bi, carry, j=j, s=s, rows=rows):
                    slot0 = off[s][j] + bi * PIECE
                    pick, wrow = one_hot_rows(s, slot0, PIECE, [j])
                    out = mlp(_dot(pick, h_ref[rows, :]).astype(BF16), j)
                    ow = (out * wrow * gate).astype(BF16)
                    o_ref[rows, :] += lax.dot_general(pick, ow, tn, preferred_element_type=F32)
                    return carry

                lax.fori_loop(0, (cnt[s][j] + PIECE - 1) // PIECE, block, 0)


def _moe(h2, x1, mod, rank_t, w_t, counts, w_gate, w_up, w_down, ws_gate, ws_up, ws_down, seq):
    T, D = h2.shape
    ns, E, tm = rank_t.shape
    tmo = min(MOE_OUTER, seq)
    n_sub = tmo // tm
    n_exp = MOE_EXPERTS_PER_STEP
    per_b = seq // tmo
    ws = [w_gate.astype(BF16), w_up.astype(BF16), w_down.astype(BF16)]
    sh = [ws_gate.astype(BF16), ws_up.astype(BF16), ws_down.astype(BF16)]
    tok = lambda: pl.BlockSpec((tmo, D), lambda t, e, c: (t, 0))
    sub = lambda: pl.BlockSpec((n_sub, E, tm), lambda t, e, c: (t, 0, 0))
    exp = lambda a: pl.BlockSpec((n_exp,) + a.shape[1:], lambda t, e, c: (e, 0, 0))
    full = lambda a: pl.BlockSpec(a.shape, lambda t, e, c: (0, 0))
    grid_spec = pltpu.PrefetchScalarGridSpec(
        num_scalar_prefetch=1,
        grid=(T // tmo, E // n_exp),
        in_specs=[tok(), tok(), pl.BlockSpec((None, N_MOD, D), lambda t, e, c: (t // per_b, 0, 0)),
                  sub(), sub(), exp(ws[0]), exp(ws[1]), exp(ws[2]), full(sh[0]), full(sh[1]), full(sh[2])],
        out_specs=tok(),
        scratch_shapes=[pltpu.VMEM((n_sub, MOE_ROWS, D), BF16), pltpu.VMEM((n_sub, MOE_ROWS, D), BF16),
                        pltpu.VMEM((n_sub, tm, MOE_ROWS), BF16), pltpu.VMEM((n_sub, MOE_ROWS, 1), F32)],
    )
    return pl.pallas_call(
        functools.partial(_moe_kernel, n_sub=n_sub, tm=tm, n_exp=n_exp),
        grid_spec=grid_spec,
        out_shape=jax.ShapeDtypeStruct((T, D), F32),
        compiler_params=_params(("arbitrary", "arbitrary")),
        name="moe",
    )(counts, h2, x1, mod, rank_t, w_t, *ws, *sh)


def kernel(x, c, w_ada, b_ada, norm_mix_gain, norm_ffn_gain, w_in, s5_lambda_re, s5_lambda_im, s5_log_dt, s5_b_re, s5_b_im, s5_c_re, s5_c_im, s5_d, s5_w_glu, s5_b_glu, q_norm_gain, k_norm_gain, cmp_pe, cmp_w1, cmp_b1, cmp_w2, cmp_b2, w_branch_a, w_branch_b, w_out, w_router, router_bias, w_gate, w_up, w_down, ws_gate, ws_up, ws_down):
    B, S, D = x.shape
    for l in range(w_ada.shape[0]):
        mod = _ada(c, w_ada[l], b_ada[l]).reshape(B, N_MOD, D)
        u, kvc, qt, ks, vst, kwn, vwt, gt, gm = _inproj(x, mod, norm_mix_gain[l:l + 1], w_in[l],
                                                        q_norm_gain[l], k_norm_gain[l])
        tables = _s5_tables(s5_lambda_re[l], s5_lambda_im[l], s5_log_dt[l], s5_b_re[l], s5_b_im[l],
                            s5_c_re[l], s5_c_im[l], s5_d[l])
        ys5 = _s5(u, tables)
        kvc_c = _compress(kvc, cmp_pe[l], cmp_w1[l], cmp_b1[l], cmp_w2[l], cmp_b2[l], k_norm_gain[l, 0])
        o_nsa = _attend(qt, ks, vst, kwn, vwt, gt, kvc_c)
        x1, h2, scores_t = _merge(x, mod, ys5, o_nsa, gm, s5_w_glu[l], s5_b_glu[l], w_branch_a[l],
                                  w_branch_b[l], w_out[l], norm_ffn_gain[l:l + 1], w_router[l])
        rank_t, w_t, cnt = _route(scores_t, router_bias[l])
        counts = cnt[:, :, 0].astype(jnp.int32).reshape(-1)
        x = _moe(h2.reshape(B * S, D), x1.reshape(B * S, D), mod, rank_t, w_t, counts,
                 w_gate[l], w_up[l], w_down[l], ws_gate[l], ws_up[l], ws_down[l], S).reshape(B, S, D)
    return x
```

```python
import functools
import math

import numpy as np
import jax
import jax.numpy as jnp
from jax import lax
from jax.experimental import pallas as pl
from jax.experimental.pallas import tpu as pltpu

F32 = jnp.float32
BF16 = jnp.bfloat16

D_MODEL = 1024
S5_WIDTH = 512
S5_GROUP = 16
S5_GROUPS = S5_WIDTH // S5_GROUP
S5_STATE = 64
N_HEADS = 8
N_KV_HEADS = 2
GQA_GROUP = N_HEADS // N_KV_HEADS
HEAD_DIM = 64
NSA_WIDTH = N_HEADS * HEAD_DIM
KV_WIDTH = 2 * N_KV_HEADS * HEAD_DIM
CMP_BLOCK = 32
CMP_STRIDE = 16
CMP_HIDDEN = 256
SEL_BLOCK = 64
SEL_TOPK = 8
WINDOW = 256
Q_BLOCK = 128
FORCE_BONUS = 1e3
N_EXPERTS = 64
TOP_K = 8
N_EXPERT_GROUPS = 8
TOPK_EXPERT_GROUPS = 4
EXPERT_HIDDEN = 256
SHARED_HIDDEN = 256
ROUTED_SCALE = 2.5
RMS_EPS = 1e-6
NEG_INF = -1e30
N_MOD = 6

LANES = 128
SUBLANES = 8
S5_CHUNK = 16
S5_CW = S5_CHUNK * S5_GROUP
ROW_TILE = 512
ATT_TILES = 2
SEL_KEY_CHUNK = 512
SEL_UNIT = 128
SEL_SLOTS = (5, 8)
POS_BASE = 64
POS_ROWS = 16
AUG = 128
ROUTE_TILE = 256
MOE_OUTER = 1024
MOE_EXPERTS_PER_STEP = 4
MOE_PIECE = 128
MOE_WIN = 48
MOE_ROWS = 512
RANK_NONE = -float(1 << 20)
VMEM_LIMIT = 56 * 1024 * 1024


def _dot(a, b):
    return jnp.dot(a, b, preferred_element_type=F32)


def _split(a):
    hi = a.astype(BF16)
    lo = (a - hi.astype(F32)).astype(BF16)
    return hi, lo


def _dot3(a, bh, bl):
    ah, al = _split(a)
    return _dot(ah, bh) + (_dot(al, bh) + _dot(ah, bl))


def _segment_transpose(x):
    assert x.shape[-2:] == (SUBLANES, LANES) and LANES // S5_GROUP == SUBLANES
    nd = x.ndim
    i = lax.broadcasted_iota(jnp.int32, x.shape, nd - 2)
    seg = lax.broadcasted_iota(jnp.int32, x.shape, nd - 1) // S5_GROUP
    out = x
    for d in range(1, SUBLANES):
        r = pltpu.roll(pltpu.roll(x, SUBLANES - d, axis=nd - 2), S5_GROUP * d, axis=nd - 1)
        out = jnp.where(seg == ((i + d) & (SUBLANES - 1)), r, out)
    return out


def _to_chunk_major(u, put):
    rows = u.shape[0]
    u3 = u.reshape(rows // SUBLANES, SUBLANES, S5_WIDTH)
    halves = S5_CHUNK // SUBLANES
    for jb in range(S5_WIDTH // LANES):
        t = _segment_transpose(u3[:, :, LANES * jb:LANES * (jb + 1)])
        t = t.reshape(rows // S5_CHUNK, halves, SUBLANES, LANES)
        for hf in range(halves):
            put(slice(SUBLANES * jb, SUBLANES * (jb + 1)), slice(LANES * hf, LANES * (hf + 1)), t[:, hf])


def _from_chunk_major(get, rows):
    halves = S5_CHUNK // SUBLANES
    cols = []
    for jb in range(S5_WIDTH // LANES):
        parts = [_segment_transpose(get(slice(SUBLANES * jb, SUBLANES * (jb + 1)), slice(LANES * hf, LANES * (hf + 1))))
                 for hf in range(halves)]
        cols.append(jnp.stack(parts, axis=1).reshape(rows, LANES))
    return jnp.concatenate(cols, axis=1)


def _params(sem):
    return pltpu.CompilerParams(dimension_semantics=sem, vmem_limit_bytes=VMEM_LIMIT)


def _ada_kernel(c_ref, w_ref, b_ref, o_ref):
    cs = jax.nn.silu(c_ref[...])
    wh, wl = _split(w_ref[...])
    o_ref[...] = _dot3(cs, wh, wl) + b_ref[...]


def _ada(c, w_ada, b_ada):
    B, D = c.shape
    return pl.pallas_call(
        _ada_kernel,
        grid=(N_MOD,),
        in_specs=[pl.BlockSpec((B, D), lambda j: (0, 0)),
                  pl.BlockSpec((D, D), lambda j: (0, j)),
                  pl.BlockSpec((1, D), lambda j: (0, j))],
        out_specs=pl.BlockSpec((B, D), lambda j: (0, j)),
        out_shape=jax.ShapeDtypeStruct((B, N_MOD * D), F32),
        compiler_params=_params(("arbitrary",)),
        name="ada",
    )(c, w_ada, b_ada.reshape(1, N_MOD * D))


def _head_norm(v, bd, gain):
    sq = v * v
    sh, sl = _split(sq)
    ms = _dot(sh, bd) + _dot(sl, bd)
    return v * lax.rsqrt(ms + RMS_EPS) * gain


def _inproj_kernel(x_ref, mod_ref, gain_ref, wm_ref, wt_ref, wgn_ref, wgm_ref, bd_ref, qg_ref, kg_ref,
                   u_ref, kvc_ref, qt_ref, ksa_ref, vst_ref, kwa_ref, vwt_ref, gt_ref, gm_ref):
    x = x_ref[...]
    shift = mod_ref[0:1, :]
    scale = mod_ref[1:2, :]
    y = x * lax.rsqrt(jnp.mean(x * x, axis=-1, keepdims=True) + RMS_EPS)
    h = (y * gain_ref[...]) * (1.0 + scale) + shift
    hb = h.astype(BF16)
    main = _dot(hb, wm_ref[...])
    tm = x.shape[0]

    def put(groups, lanes, block):
        u_ref[:, groups, lanes] = block

    _to_chunk_major(main[:, :S5_WIDTH], put)
    o = S5_WIDTH
    kvc_ref[...] = main[:, o:o + KV_WIDTH]
    o += KV_WIDTH
    kw = N_KV_HEADS * HEAD_DIM
    bd = bd_ref[...]
    ks = _head_norm(main[:, o:o + kw], bd[:kw, :kw], kg_ref[1:2, :]).astype(BF16)
    kwn = _head_norm(main[:, o + kw:o + 2 * kw], bd[:kw, :kw], kg_ref[2:3, :]).astype(BF16)
    pos = pl.program_id(1) * tm + lax.broadcasted_iota(jnp.int32, (tm, 1), 0)
    lane = lax.broadcasted_iota(jnp.int32, (tm, AUG), 1)
    digits = jnp.where(lane == 0, pos // POS_BASE, jnp.where(lane == 1, pos % POS_BASE, 0))
    digits = digits[:, :AUG - HEAD_DIM].astype(F32).astype(BF16)
    onehot = jnp.where(lane == pos // SEL_BLOCK, 1.0, 0.0).astype(BF16)
    for hh in range(N_KV_HEADS):
        ksa_ref[hh, :, 0:HEAD_DIM] = ks[:, hh * HEAD_DIM:(hh + 1) * HEAD_DIM]
        ksa_ref[hh, :, HEAD_DIM:AUG] = digits
        ksa_ref[hh, :, AUG:] = onehot
        kwa_ref[hh, :, 0:HEAD_DIM] = kwn[:, hh * HEAD_DIM:(hh + 1) * HEAD_DIM]
        kwa_ref[hh, :, HEAD_DIM:] = digits

    nt = (((1,), (1,)), ((), ()))
    tt = lax.dot_general(wt_ref[...], hb, nt, preferred_element_type=F32)
    qt = tt[:NSA_WIDTH]
    sq = qt * qt
    sh, sl = _split(sq)
    qn = (qt * lax.rsqrt(_dot(bd, sh) + _dot(bd, sl) + RMS_EPS) * qg_ref[...] * (HEAD_DIM ** -0.5)).astype(BF16)
    QW = GQA_GROUP * Q_BLOCK
    for qb in range(tm // Q_BLOCK):
        for hd in range(N_HEADS):
            hh, g = divmod(hd, GQA_GROUP)
            qt_ref[hh, :, qb * QW + g * Q_BLOCK:qb * QW + (g + 1) * Q_BLOCK] = (
                qn[hd * HEAD_DIM:(hd + 1) * HEAD_DIM, qb * Q_BLOCK:(qb + 1) * Q_BLOCK])
    vst_ref[...] = tt[NSA_WIDTH:NSA_WIDTH + kw].reshape(N_KV_HEADS, HEAD_DIM, tm).astype(BF16)
    vwt_ref[...] = tt[NSA_WIDTH + kw:].reshape(N_KV_HEADS, HEAD_DIM, tm).astype(BF16)
    gn = jax.nn.sigmoid(lax.dot_general(wgn_ref[...], hb, nt, preferred_element_type=F32))
    for qb in range(tm // Q_BLOCK):
        for br in range(3):
            for hd in range(N_HEADS):
                hh, g = divmod(hd, GQA_GROUP)
                r = br * N_HEADS + hd
                gt_ref[hh, br:br + 1, qb * QW + g * Q_BLOCK:qb * QW + (g + 1) * Q_BLOCK] = (
                    gn[r:r + 1, qb * Q_BLOCK:(qb + 1) * Q_BLOCK])
    gm_ref[...] = jax.nn.sigmoid(_dot(hb, wgm_ref[...])).astype(BF16)


def _s5_chunk_spec(tm):
    assert S5_CHUNK == 16 and S5_GROUP == 16, "the segment transposes assume 16 steps x 16 channels"
    return pl.BlockSpec((tm // S5_CHUNK, None, S5_GROUPS, S5_CW), lambda b, i: (i, b, 0, 0))


def _inproj(x, mod, gain, w_in, q_gain, k_gain):
    B, S, D = x.shape
    tm = min(ROW_TILE, S)
    assert S // SEL_BLOCK <= AUG and tm % Q_BLOCK == 0
    kw = N_KV_HEADS * HEAD_DIM
    cols = np.cumsum((0,) + (S5_WIDTH, NSA_WIDTH, KV_WIDTH, KV_WIDTH, KV_WIDTH, 3 * N_HEADS, 2 * D))
    c_u, c_q, c_kvc, c_kvs, c_kvw, c_gn, c_gm = cols[:7]
    sl = lambda a, n: w_in[:, a:a + n]
    wm = jnp.concatenate([sl(c_u, S5_WIDTH), sl(c_kvc, KV_WIDTH), sl(c_kvs, kw), sl(c_kvw, kw)], axis=1).astype(BF16)
    wt = jnp.concatenate([sl(c_q, NSA_WIDTH), sl(c_kvs + kw, kw), sl(c_kvw + kw, kw)], axis=1).T.astype(BF16)
    wgn = sl(c_gn, 3 * N_HEADS).T.astype(BF16)
    wgm = sl(c_gm, 2 * D).astype(BF16)
    seg = np.arange(NSA_WIDTH) // HEAD_DIM
    bd = jnp.asarray((seg[:, None] == seg[None, :]).astype(np.float32) / HEAD_DIM, BF16)
    qg = jnp.tile(q_gain, N_HEADS).reshape(NSA_WIDTH, 1)
    kg = jnp.tile(k_gain, (1, N_KV_HEADS))
    nq = tm // Q_BLOCK * GQA_GROUP * Q_BLOCK
    row = lambda w: pl.BlockSpec((None, tm, w), lambda b, i: (b, i, 0))
    full = lambda a: pl.BlockSpec(a.shape, lambda b, i: (0,) * a.ndim)
    rows4 = lambda w: pl.BlockSpec((None, N_KV_HEADS, tm, w), lambda b, i: (b, 0, i, 0))
    cols4 = lambda r, w: pl.BlockSpec((None, N_KV_HEADS, r, w), lambda b, i: (b, 0, 0, i))
    nqt = S // Q_BLOCK * GQA_GROUP * Q_BLOCK
    return pl.pallas_call(
        _inproj_kernel,
        grid=(B, S // tm),
        in_specs=[row(D), pl.BlockSpec((None, N_MOD, D), lambda b, i: (b, 0, 0)),
                  full(gain), full(wm), full(wt), full(wgn), full(wgm), full(bd), full(qg), full(kg)],
        out_specs=[_s5_chunk_spec(tm), row(KV_WIDTH), cols4(HEAD_DIM, nq), rows4(2 * AUG), cols4(HEAD_DIM, tm),
                   rows4(AUG), cols4(HEAD_DIM, tm), cols4(3, nq), row(2 * D)],
        out_shape=[jax.ShapeDtypeStruct((S // S5_CHUNK, B, S5_GROUPS, S5_CW), F32),
                   jax.ShapeDtypeStruct((B, S, KV_WIDTH), F32),
                   jax.ShapeDtypeStruct((B, N_KV_HEADS, HEAD_DIM, nqt), BF16),
                   jax.ShapeDtypeStruct((B, N_KV_HEADS, S, 2 * AUG), BF16),
                   jax.ShapeDtypeStruct((B, N_KV_HEADS, HEAD_DIM, S), BF16),
                   jax.ShapeDtypeStruct((B, N_KV_HEADS, S, AUG), BF16),
                   jax.ShapeDtypeStruct((B, N_KV_HEADS, HEAD_DIM, S), BF16),
                   jax.ShapeDtypeStruct((B, N_KV_HEADS, 3, nqt), F32),
                   jax.ShapeDtypeStruct((B, S, 2 * D), BF16)],
        compiler_params=_params(("arbitrary", "arbitrary")),
        name="inproj",
    )(x, mod, gain, wm, wt, wgn, wgm, bd, qg, kg)


def _s5_tables(lam_re, lam_im, log_dt, b_re, b_im, c_re, c_im, d_skip):
    L, P, N, G = S5_CHUNK, S5_GROUP, S5_STATE, S5_GROUPS
    hp = lax.Precision.HIGHEST
    lr, li = lam_re.astype(F32), lam_im.astype(F32)
    dt = jnp.exp(log_dt.astype(F32))[:, None]
    mag = jnp.exp(lr * dt)
    abar_re, abar_im = mag * jnp.cos(li * dt), mag * jnp.sin(li * dt)
    num_re, num_im = abar_re - 1.0, abar_im
    den = lr * lr + li * li
    coef_re = (num_re * lr + num_im * li) / den
    coef_im = (num_im * lr - num_re * li) / den
    br, bi = b_re.astype(F32), b_im.astype(F32)
    bbar_re = coef_re[..., None] * br - coef_im[..., None] * bi
    bbar_im = coef_re[..., None] * bi + coef_im[..., None] * br
    k = jnp.arange(L + 1, dtype=F32)[:, None, None]
    pmag = jnp.exp(lr * dt * k)
    pre, pim = pmag * jnp.cos(li * dt * k), pmag * jnp.sin(li * dt * k)
    cr, ci = c_re.astype(F32), c_im.astype(F32)
    ca_re = cr[None] * pre[:, :, None, :] - ci[None] * pim[:, :, None, :]
    ca_im = cr[None] * pim[:, :, None, :] + ci[None] * pre[:, :, None, :]
    kern = (jnp.einsum('kgpn,gnq->gkqp', ca_re[:L], bbar_re, precision=hp)
            - jnp.einsum('kgpn,gnq->gkqp', ca_im[:L], bbar_im, precision=hp))
    s_i = np.arange(L)[:, None]
    t_i = np.arange(L)[None, :]
    tau = np.clip(t_i - s_i, 0, L - 1)
    causal = jnp.asarray((t_i >= s_i).astype(np.float32))
    mt = kern[:, tau] * causal[None, :, :, None, None]
    mt = mt.transpose(0, 1, 3, 2, 4).reshape(G, L * P, L * P)
    rev = np.arange(L - 1, -1, -1)
    ab_re = pre[rev][..., None] * bbar_re[None] - pim[rev][..., None] * bbar_im[None]
    ab_im = pre[rev][..., None] * bbar_im[None] + pim[rev][..., None] * bbar_re[None]
    ws = jnp.concatenate([ab_re, ab_im], axis=2)
    ws = ws.transpose(1, 0, 3, 2).reshape(G, L * P, 2 * N)
    wo = jnp.concatenate([ca_re[1:], -ca_im[1:]], axis=3)
    wo = wo.transpose(1, 3, 0, 2).reshape(G, 2 * N, L * P)
    al = jnp.stack([pre[L], pim[L]], axis=1)
    dv = jnp.tile(d_skip.astype(F32), (1, L)).reshape(G, 1, L * P)
    return mt, ws, wo, al, dv


def _s5_kernel(u_ref, mth_ref, mtl_ref, wsh_ref, wsl_ref, woh_ref, wol_ref, al_ref, dv_ref,
               y_ref, vr_scr, vi_scr, xr_scr, xi_scr, *, n_chunks, bsz):
    N = S5_STATE
    u = u_ref[...]
    uh, ul = _split(u)
    y = _dot(uh, mth_ref[...]) + (_dot(ul, mth_ref[...]) + _dot(uh, mtl_ref[...]))
    v = _dot(uh, wsh_ref[...]) + (_dot(ul, wsh_ref[...]) + _dot(uh, wsl_ref[...]))
    vr_scr[...] = v[:, :N]
    vi_scr[...] = v[:, N:]
    a_r = al_ref[0:1, :]
    a_i = al_ref[1:2, :]

    def step(c, x):
        xr, xi = x
        r = pl.ds(pl.multiple_of(c * bsz, bsz), bsz)
        xr_scr[r, :] = xr
        xi_scr[r, :] = xi
        return (a_r * xr - a_i * xi + vr_scr[r, :], a_r * xi + a_i * xr + vi_scr[r, :])

    zero = jnp.zeros((bsz, N), F32)
    lax.fori_loop(0, n_chunks, step, (zero, zero), unroll=8)
    xp = jnp.concatenate([xr_scr[...], xi_scr[...]], axis=1)
    y = y + _dot3(xp, woh_ref[...], wol_ref[...])
    y_ref[...] = y + dv_ref[...] * u


def _s5(ug, tables):
    nc, B, G, cw = ug.shape
    N = S5_STATE
    mt, ws, wo, al, dv = tables
    mth, mtl = _split(mt)
    wsh, wsl = _split(ws)
    woh, wol = _split(wo)
    grp = lambda a: pl.BlockSpec((None,) + a.shape[1:], lambda g: (g,) + (0,) * (a.ndim - 1))
    ugt = ug.reshape(nc * B, G, cw).transpose(1, 0, 2)
    y = pl.pallas_call(
        functools.partial(_s5_kernel, n_chunks=nc, bsz=B),
        grid=(G,),
        in_specs=[grp(ugt), grp(mth), grp(mtl), grp(wsh), grp(wsl), grp(woh), grp(wol), grp(al), grp(dv)],
        out_specs=grp(ugt),
        out_shape=jax.ShapeDtypeStruct(ugt.shape, F32),
        scratch_shapes=[pltpu.VMEM((nc * B, N), F32)] * 4,
        compiler_params=_params(("arbitrary",)),
        name="s5",
    )(ugt, mth, mtl, wsh, wsl, woh, wol, al, dv)
    return y.transpose(1, 0, 2).reshape(nc, B, G, cw)


def _compress_kernel(x_ref, pe_ref, w1_ref, b1_ref, w2_ref, b2_ref, kg_ref, o_ref, *, n_rows):
    x = x_ref[...]
    half = CMP_STRIDE * HEAD_DIM
    a = _dot((x + pe_ref[0:1, :]).astype(BF16), w1_ref[:half, :])
    b = _dot((x + pe_ref[1:2, :]).astype(BF16), w1_ref[half:, :])
    hid = jax.nn.gelu(a + pltpu.roll(b, n_rows - 1, axis=0) + b1_ref[...])
    out = _dot(hid.astype(BF16), w2_ref[...]) + b2_ref[...]
    normed = out * lax.rsqrt(jnp.mean(out * out, axis=-1, keepdims=True) + RMS_EPS) * kg_ref[...]
    is_key = pl.program_id(1) < N_KV_HEADS
    o_ref[...] = jnp.where(is_key, normed, out).astype(BF16)


def _compress(kv_c, cmp_pe, cmp_w1, cmp_b1, cmp_w2, cmp_b2, k_gain0):
    B, S, _ = kv_c.shape
    nr = S // CMP_STRIDE
    half = CMP_STRIDE * HEAD_DIM
    nj = 2 * N_KV_HEADS
    xc = kv_c.reshape(B, nr, CMP_STRIDE, nj, HEAD_DIM).transpose(0, 3, 1, 2, 4).reshape(B, nj, nr, half)
    pe = cmp_pe.reshape(2, 2, half)
    w1 = cmp_w1.astype(BF16)
    w2 = cmp_w2.astype(BF16)
    b1 = cmp_b1.reshape(2, 1, CMP_HIDDEN)
    b2 = cmp_b2.reshape(2, 1, HEAD_DIM)
    kv = lambda a: pl.BlockSpec((None,) + a.shape[1:], lambda b, j: (j // N_KV_HEADS,) + (0,) * (a.ndim - 1))
    return pl.pallas_call(
        functools.partial(_compress_kernel, n_rows=nr),
        grid=(B, nj),
        in_specs=[pl.BlockSpec((None, None, nr, half), lambda b, j: (b, j, 0, 0)),
                  kv(pe), kv(w1), kv(b1), kv(w2), kv(b2),
                  pl.BlockSpec((1, HEAD_DIM), lambda b, j: (0, 0))],
        out_specs=pl.BlockSpec((None, None, nr, HEAD_DIM), lambda b, j: (b, j, 0, 0)),
        out_shape=jax.ShapeDtypeStruct((B, nj, nr, HEAD_DIM), BF16),
        compiler_params=_params(("arbitrary", "arbitrary")),
        name="compress",
    )(xc, pe, w1, b1, w2, b2, k_gain0.reshape(1, HEAD_DIM))


def _softmax_cols(s, mask):
    s = jnp.where(mask, s, NEG_INF)
    m = jnp.max(s, axis=0, keepdims=True)
    p = jnp.where(mask, jnp.exp(s - m), 0.0)
    return p / jnp.maximum(jnp.sum(p, axis=0, keepdims=True), 1e-20)


def _attend_kernel(qt_ref, kc_ref, vct_ref, ks_ref, vst_ref, kw_ref, vwt_ref, g_ref, slope_ref, ovt_ref,
                   o_ref, qa_scr, sa_scr, sb_scr, *, n_sel, n_pick, n_cmp_rows):
    QW = GQA_GROUP * Q_BLOCK
    chains = [(k, h) for k in range(ATT_TILES) for h in range(N_KV_HEADS)]
    t0s = [(pl.program_id(1) * ATT_TILES + k) * Q_BLOCK for k in range(ATT_TILES)]
    tqs = [t0 + (lax.broadcasted_iota(jnp.int32, (1, QW), 1) & (Q_BLOCK - 1)) for t0 in t0s]
    cols = lambda k: slice(k * QW, (k + 1) * QW)

    r = lax.broadcasted_iota(jnp.int32, (POS_ROWS, QW), 0)
    qa = {}
    for k, h in chains:
        slope = slope_ref[h]
        qa_scr[k, h, 0:HEAD_DIM, :] = qt_ref[h, :, cols(k)]
        qa_scr[k, h, HEAD_DIM:HEAD_DIM + POS_ROWS, :] = jnp.where(
            r == 0, slope * POS_BASE, jnp.where(r == 1, slope, 0.0)).astype(BF16)
        qa_scr[k, h, HEAD_DIM + POS_ROWS:AUG, :] = jnp.zeros((AUG - HEAD_DIM - POS_ROWS, QW), BF16)
        qa[k, h] = qa_scr[k, h, 0:AUG, :]

    def compressed(k, h, rows):
        sc = _dot(kc_ref[h, 0:rows, :], qa[k, h])
        cpos = lax.broadcasted_iota(jnp.int32, (rows, QW), 0) * CMP_STRIDE + (CMP_BLOCK - 1)
        p_c = _softmax_cols(sc, cpos <= tqs[k])
        o_c = _dot(vct_ref[h, :, 0:rows], p_c.astype(BF16))
        psum = p_c[:, 0:Q_BLOCK]
        for g in range(1, GQA_GROUP):
            psum = psum + p_c[:, g * Q_BLOCK:(g + 1) * Q_BLOCK]
        ph, pl_ = _split(psum)
        return o_c, _dot(ovt_ref[:, 0:rows], ph) + _dot(ovt_ref[:, 0:rows], pl_)

    half = n_cmp_rows // 2
    if half % LANES == 0:
        early = (t0s[-1] + Q_BLOCK - CMP_BLOCK) // CMP_STRIDE < half
        cmp_out = lax.cond(early, lambda _: [compressed(k, h, half) for k, h in chains],
                           lambda _: [compressed(k, h, n_cmp_rows) for k, h in chains], 0)
    else:
        cmp_out = [compressed(k, h, n_cmp_rows) for k, h in chains]
    cmp_out = dict(zip(chains, cmp_out))

    def select(k, h, imp):
        jb = lax.broadcasted_iota(jnp.int32, (n_sel, Q_BLOCK), 0)
        cur = (t0s[k] + lax.broadcasted_iota(jnp.int32, (1, Q_BLOCK), 1)) // SEL_BLOCK
        forced = (jb == 0) | (jb == cur) | (jb == cur - 1)
        imp = jnp.where(forced, imp + FORCE_BONUS, imp)
        imp = jnp.where(jb <= cur, imp, -1.0)
        bias = jnp.full((n_sel, Q_BLOCK), NEG_INF, F32)
        for _ in range(n_pick):
            m = jnp.max(imp, axis=0, keepdims=True)
            first = jnp.min(jnp.where(imp == m, jb, n_sel), axis=0, keepdims=True)
            hit = jb == first
            bias = jnp.where(hit, 0.0, bias)
            imp = jnp.where(hit, -jnp.inf, imp)
        per_unit = SEL_UNIT // SEL_BLOCK
        n_units = n_sel // per_unit
        picked = jnp.where(bias == 0.0, 1.0, 0.0).astype(BF16)
        in_unit = (lax.broadcasted_iota(jnp.int32, (n_units, n_sel), 1) // per_unit
                   == lax.broadcasted_iota(jnp.int32, (n_units, n_sel), 0))
        hits = _dot(jnp.where(in_unit, 1.0, 0.0).astype(BF16), picked)
        live = jnp.where(jnp.max(hits, axis=1, keepdims=True) > 0.5, 1.0, 0.0)
        live = jnp.broadcast_to(live, (n_units, LANES))
        earlier = (lax.broadcasted_iota(jnp.int32, (n_units, n_units), 1)
                   < lax.broadcasted_iota(jnp.int32, (n_units, n_units), 0))
        before = _dot(jnp.where(earlier, 1.0, 0.0).astype(BF16), live.astype(BF16))
        slot = lax.broadcasted_iota(jnp.int32, (n_units, LANES), 1).astype(F32)
        in_slot = jnp.where(before == slot, live, 0.0).astype(BF16)
        r8 = lax.broadcasted_iota(jnp.int32, (SUBLANES, n_units), 0)
        c8 = lax.broadcasted_iota(jnp.int32, (SUBLANES, n_units), 1)
        pick_rows = jnp.where(r8 == 0, c8, jnp.where(r8 == 1, 1, 0)).astype(F32).astype(BF16)
        ids = _dot(pick_rows, in_slot)
        n_live = (before[n_units - 1:n_units, 0:1] + live[n_units - 1:n_units, 0:1])[0, 0]
        bias = jnp.concatenate([bias] * GQA_GROUP, axis=1).astype(BF16)
        qa_scr[k, h, AUG:, :] = jnp.concatenate([bias, jnp.zeros((AUG - n_sel, QW), BF16)], axis=0)
        return ids, n_live

    def window(k, h):
        WK = WINDOW + Q_BLOCK
        w0 = pl.multiple_of(jnp.maximum(t0s[k] - WINDOW, 0), Q_BLOCK)
        sw = _dot(kw_ref[h, pl.ds(w0, WK), :], qa[k, h])
        dist = tqs[k] - (w0 + lax.broadcasted_iota(jnp.int32, (WK, QW), 0))
        in_win = (dist | (WINDOW - 1 - dist)) >= 0
        p_w = _softmax_cols(sw, in_win)
        return _dot(vwt_ref[h, :, pl.ds(w0, WK)], p_w.astype(BF16))

    lists = {c: select(*c, cmp_out[c][1]) for c in chains}
    o_win = {c: window(*c) for c in chains}

    def gathered(k, h, ids, n_slots):
        units = [ids[0, s].astype(jnp.int32) for s in range(n_slots)]
        starts = [pl.multiple_of(u * SEL_UNIT, SEL_UNIT) for u in units]
        keys = jnp.concatenate([ks_ref[h, pl.ds(st, SEL_UNIT), :] for st in starts], axis=0)
        s = _dot(keys, qa_scr[k, h])
        krow = lax.broadcasted_iota(jnp.int32, (SEL_UNIT, QW), 0)
        seen = jnp.concatenate(
            [st + krow <= jnp.where(ids[1, si] > 0.5, tqs[k], -1) for si, st in enumerate(starts)], axis=0)
        s = jnp.where(seen, s, NEG_INF)
        p = jnp.exp(s - jnp.max(s, axis=0, keepdims=True))
        vals = jnp.concatenate([vst_ref[h, :, pl.ds(st, SEL_UNIT)] for st in starts], axis=1)
        return _dot(vals, p.astype(BF16)) / jnp.maximum(jnp.sum(p, axis=0, keepdims=True), 1e-20)

    KC = SEL_KEY_CHUNK

    def swept(k, h):
        last = (t0s[k] + Q_BLOCK + KC - 1) // KC - 1

        def scores_to(ref, j):
            k0 = pl.multiple_of(j * KC, KC)
            ref[...] = _dot(ks_ref[h, pl.ds(k0, KC), :], qa_scr[k, h])

        def accumulate(carry, j, s, p_of):
            m, l, acc = carry
            k0 = pl.multiple_of(j * KC, KC)
            m_new = jnp.maximum(m, jnp.max(s, axis=0, keepdims=True))
            alpha = jnp.exp(m - m_new)
            p = p_of(jnp.exp(s - m_new))
            l = alpha * l + jnp.sum(p, axis=0, keepdims=True)
            acc = alpha * acc + _dot(vst_ref[h, :, pl.ds(k0, KC)], p.astype(BF16))
            return m_new, l, acc

        keep = lambda p: p

        def two_chunks(i, state):
            j = 2 * i
            scores_to(sb_scr, j + 1)
            state = accumulate(state, j, sa_scr[...], keep)
            scores_to(sa_scr, j + 2)
            return accumulate(state, j + 1, sb_scr[...], keep)

        def odd_chunk(state):
            state = accumulate(state, last - 1, sa_scr[...], keep)
            scores_to(sa_scr, last)
            return state

        scores_to(sa_scr, 0)
        init = (jnp.full((1, QW), NEG_INF, F32), jnp.zeros((1, QW), F32), jnp.zeros((HEAD_DIM, QW), F32))
        state = lax.fori_loop(0, last // 2, two_chunks, init)
        state = lax.cond(last % 2 == 1, odd_chunk, lambda st: st, state)
        visible = (last * KC + lax.broadcasted_iota(jnp.int32, (KC, QW), 0)) <= tqs[k]
        _, l_s, acc_s = accumulate(state, last, jnp.where(visible, sa_scr[...], NEG_INF),
                                   lambda p: jnp.where(visible, p, 0.0))
        return acc_s / jnp.maximum(l_s, 1e-20)

    few, some = SEL_SLOTS
    for k in range(ATT_TILES):
        outs = []
        for h in range(N_KV_HEADS):
            ids, n_live = lists[k, h]
            o_s = lax.cond(n_live <= few, lambda _: gathered(k, h, ids, few),
                           lambda _: lax.cond(n_live <= some, lambda _: gathered(k, h, ids, some),
                                              lambda _: swept(k, h), 0), 0)
            gk = lambda br: g_ref[h, br:br + 1, cols(k)]
            o = gk(0) * cmp_out[k, h][0] + gk(1) * o_s + gk(2) * o_win[k, h]
            for g in range(0, GQA_GROUP, 2):
                sq = jnp.concatenate([o[:, g * Q_BLOCK:(g + 1) * Q_BLOCK], o[:, (g + 1) * Q_BLOCK:(g + 2) * Q_BLOCK]],
                                     axis=0)
                outs.append(sq.T)
        o_ref[k * Q_BLOCK:(k + 1) * Q_BLOCK, :] = jnp.concatenate(outs, axis=1).astype(BF16)


def _attend(qt, ks, vst, kwn, vwt, gt, kvc_c):
    B, _, S, _ = ks.shape
    nqb = S // Q_BLOCK
    QW = GQA_GROUP * Q_BLOCK
    n_sel = S // SEL_BLOCK
    n_pick = min(SEL_TOPK, n_sel)
    nr = S // CMP_STRIDE
    n_cmp = (S - CMP_BLOCK) // CMP_STRIDE + 1
    assert n_sel % 16 == 0 and n_sel <= AUG

    cpos = np.arange(nr) * CMP_STRIDE + CMP_BLOCK - 1
    digits = np.zeros((nr, AUG - HEAD_DIM), np.float32)
    digits[:, 0] = cpos // POS_BASE
    digits[:, 1] = cpos % POS_BASE
    kc = kvc_c[:, :N_KV_HEADS]
    kc = jnp.concatenate([kc, jnp.broadcast_to(jnp.asarray(digits, BF16), kc.shape[:2] + digits.shape)], axis=-1)
    vct = kvc_c[:, N_KV_HEADS:].transpose(0, 1, 3, 2)
    slopes = 2.0 ** (-8.0 * np.arange(1, N_HEADS + 1) / N_HEADS)
    slope_t = jnp.asarray(np.repeat(slopes.reshape(N_KV_HEADS, GQA_GROUP), Q_BLOCK, axis=1)
                          .reshape(N_KV_HEADS, 1, QW), F32)
    cmp_start = np.arange(n_cmp) * CMP_STRIDE
    sel_start = np.arange(n_sel) * SEL_BLOCK
    ov = ((cmp_start[:, None] <= sel_start[None, :] + SEL_BLOCK - 1)
          & (cmp_start[:, None] + CMP_BLOCK - 1 >= sel_start[None, :])).astype(np.float32)
    ovt = np.zeros((n_sel, nr), np.float32)
    ovt[:, :n_cmp] = ov.T
    ovt = jnp.asarray(ovt, BF16)

    H = N_KV_HEADS
    per_bh = lambda r, c: pl.BlockSpec((None, H, r, c), lambda b, i: (b, 0, 0, 0))
    per_q = lambda r: pl.BlockSpec((None, H, r, ATT_TILES * QW), lambda b, i: (b, 0, 0, i))
    return pl.pallas_call(
        functools.partial(_attend_kernel, n_sel=n_sel, n_pick=n_pick, n_cmp_rows=nr),
        grid=(B, nqb // ATT_TILES),
        in_specs=[per_q(HEAD_DIM), per_bh(nr, AUG), per_bh(HEAD_DIM, nr),
                  per_bh(S, 2 * AUG), per_bh(HEAD_DIM, S), per_bh(S, AUG), per_bh(HEAD_DIM, S),
                  per_q(3),
                  pl.BlockSpec((H, 1, QW), lambda b, i: (0, 0, 0)),
                  pl.BlockSpec((n_sel, nr), lambda b, i: (0, 0))],
        out_specs=pl.BlockSpec((None, ATT_TILES * Q_BLOCK, NSA_WIDTH), lambda b, i: (b, i, 0)),
        out_shape=jax.ShapeDtypeStruct((B, S, NSA_WIDTH), BF16),
        scratch_shapes=[pltpu.VMEM((ATT_TILES, H, 2 * AUG, QW), BF16), pltpu.VMEM((SEL_KEY_CHUNK, QW), F32),
                        pltpu.VMEM((SEL_KEY_CHUNK, QW), F32)],
        compiler_params=_params(("arbitrary", "arbitrary")),
        name="attend",
    )(qt, kc, vct, ks, vst, kwn, vwt, gt, slope_t, ovt)


def _merge_kernel(x_ref, mod_ref, ys_ref, on_ref, gm_ref, wglu_ref, bglu_ref, wa_ref, wb_ref, wo_ref,
                  gain_ref, wrh_ref, wrl_ref, x1_ref, h2_ref, sc_ref):
    D = D_MODEL
    z = jax.nn.gelu(_from_chunk_major(lambda groups, lanes: ys_ref[:, groups, lanes], x_ref.shape[0]))
    glu = z * jax.nn.sigmoid(_dot(z.astype(BF16), wglu_ref[...]) + bglu_ref[...])
    ya = _dot(glu.astype(BF16), wa_ref[...])
    yb = _dot(on_ref[...], wb_ref[...])
    merged = gm_ref[:, :D].astype(F32) * ya + gm_ref[:, D:].astype(F32) * yb
    x1 = x_ref[...] + mod_ref[2:3, :] * _dot(merged.astype(BF16), wo_ref[...])
    x1_ref[...] = x1
    y = x1 * lax.rsqrt(jnp.mean(x1 * x1, axis=-1, keepdims=True) + RMS_EPS)
    h2 = (y * gain_ref[...]) * (1.0 + mod_ref[4:5, :]) + mod_ref[3:4, :]
    h2_ref[...] = h2.astype(BF16)
    hh, hl = _split(h2)
    nt = (((1,), (1,)), ((), ()))
    dg = lambda a, b: lax.dot_general(a, b, nt, preferred_element_type=F32)
    logits = dg(wrh_ref[...], hh) + (dg(wrh_ref[...], hl) + dg(wrl_ref[...], hh))
    sc_ref[...] = jax.nn.sigmoid(logits)


def _merge(x, mod, ys5, o_nsa, gm, w_glu, b_glu, w_a, w_b, w_out, gain_f, w_router):
    B, S, D = x.shape
    tm = min(ROW_TILE, S)
    wrh, wrl = _split(w_router.T)
    ws = [w_glu.astype(BF16), b_glu.reshape(1, -1), w_a.astype(BF16), w_b.astype(BF16), w_out.astype(BF16),
          gain_f, wrh, wrl]
    row = lambda w: pl.BlockSpec((None, tm, w), lambda b, i: (b, i, 0))
    full = lambda a: pl.BlockSpec(a.shape, lambda b, i: (0,) * a.ndim)
    return pl.pallas_call(
        _merge_kernel,
        grid=(B, S // tm),
        in_specs=[row(D), pl.BlockSpec((None, N_MOD, D), lambda b, i: (b, 0, 0)),
                  _s5_chunk_spec(tm), row(NSA_WIDTH), row(2 * D)] + [full(a) for a in ws],
        out_specs=[row(D), row(D), pl.BlockSpec((None, N_EXPERTS, tm), lambda b, i: (b, 0, i))],
        out_shape=[jax.ShapeDtypeStruct((B, S, D), F32), jax.ShapeDtypeStruct((B, S, D), BF16),
                   jax.ShapeDtypeStruct((B, N_EXPERTS, S), F32)],
        compiler_params=_params(("arbitrary", "arbitrary")),
        name="merge",
    )(x, mod, ys5, o_nsa, gm, *ws)


def _first_argmax_rows(v, idx, n):
    m = jnp.max(v, axis=0, keepdims=True)
    first = jnp.min(jnp.where(v == m, idx, n), axis=0, keepdims=True)
    return idx == first, m


def _route_kernel(sc_ref, bias_ref, tri_ref, rank_ref, w_ref, cnt_ref):
    E, NG = N_EXPERTS, N_EXPERT_GROUPS
    GS = E // NG
    sc = sc_ref[...]
    TM = sc.shape[1]
    sel = sc + bias_ref[...]
    i8 = lax.broadcasted_iota(jnp.int32, (GS, TM), 0)
    gscore = []
    for g in range(NG):
        blk = sel[g * GS:(g + 1) * GS, :]
        hit, m1 = _first_argmax_rows(blk, i8, GS)
        m2 = jnp.max(jnp.where(hit, -jnp.inf, blk), axis=0, keepdims=True)
        gscore.append(m1 + m2)
    gscore = jnp.concatenate(gscore, axis=0)
    ig = lax.broadcasted_iota(jnp.int32, (NG, TM), 0)
    gmask = jnp.zeros((NG, TM), F32)
    for _ in range(TOPK_EXPERT_GROUPS):
        hit, _m = _first_argmax_rows(gscore, ig, NG)
        gmask = jnp.where(hit, 1.0, gmask)
        gscore = jnp.where(hit, -jnp.inf, gscore)
    emask = jnp.concatenate([jnp.broadcast_to(gmask[g:g + 1, :], (GS, TM)) for g in range(NG)], axis=0)
    cand = jnp.where(emask > 0.5, sel, NEG_INF)
    ie = lax.broadcasted_iota(jnp.int32, (E, TM), 0)
    chosen = jnp.zeros((E, TM), F32)
    for _ in range(TOP_K):
        hit, _m = _first_argmax_rows(cand, ie, E)
        chosen = jnp.where(hit, 1.0, chosen)
        cand = jnp.where(hit, -jnp.inf, cand)
    w = chosen * sc
    w = w / jnp.sum(w, axis=0, keepdims=True) * ROUTED_SCALE
    sub = rank_ref.shape[-1]
    for q in range(TM // sub):
        cols = slice(q * sub, (q + 1) * sub)
        cb = chosen[:, cols].astype(BF16)
        prefix = _dot(cb, tri_ref[...])
        rank_ref[q] = jnp.where(chosen[:, cols] > 0.5, prefix, RANK_NONE)
        w_ref[q] = w[:, cols]
        cnt_ref[q] = _dot(cb, jnp.ones((sub, LANES), BF16))


def _route(scores_t, router_bias):
    B, E, S = scores_t.shape
    TM = min(ROUTE_TILE, S)
    per_step = min(MOE_OUTER, S) // TM
    nj = S // (TM * per_step)
    ns = B * nj * per_step
    tri = jnp.asarray(np.triu(np.ones((TM, TM), np.float32), k=1), BF16)
    tile = lambda w: pl.BlockSpec((per_step, E, w), lambda b, j: (b * nj + j, 0, 0))
    return pl.pallas_call(
        _route_kernel,
        grid=(B, nj),
        in_specs=[pl.BlockSpec((None, E, TM * per_step), lambda b, j: (b, 0, j)),
                  pl.BlockSpec((E, 1), lambda b, j: (0, 0)),
                  pl.BlockSpec((TM, TM), lambda b, j: (0, 0))],
        out_specs=[tile(TM), tile(TM), tile(LANES)],
        out_shape=[jax.ShapeDtypeStruct((ns, E, TM), F32), jax.ShapeDtypeStruct((ns, E, TM), F32),
                   jax.ShapeDtypeStruct((ns, E, LANES), F32)],
        compiler_params=_params(("arbitrary", "arbitrary")),
        name="route",
    )(scores_t, router_bias.reshape(E, 1), tri)


def _moe_kernel(cnt_ref, h_ref, x1_ref, mod_ref, rank_ref, w_ref, wg_ref, wu_ref, wd_ref,
                sg_ref, su_ref, sd_ref, o_ref, xc, yc, pc, wc, *, n_sub, tm, n_exp):
    to = pl.program_id(0)
    eb = pl.program_id(1)
    gate = mod_ref[5:6, :]

    @pl.when(eb == 0)
    def _shared():
        for s in range(n_sub):
            rows = pl.ds(s * tm, tm)
            hs = h_ref[rows, :]
            hid = jax.nn.silu(_dot(hs, sg_ref[...])) * _dot(hs, su_ref[...])
            o_ref[rows, :] = x1_ref[rows, :] + gate * _dot(hid.astype(BF16), sd_ref[...])

    PIECE, WIN = MOE_PIECE, MOE_WIN
    tn = (((0,), (0,)), ((), ()))
    experts = [eb * n_exp + j for j in range(n_exp)]
    cnt = [[cnt_ref[(to * n_sub + s) * N_EXPERTS + e] for e in experts] for s in range(n_sub)]
    off = []
    for s in range(n_sub):
        o = [jnp.int32(0)]
        for j in range(n_exp):
            o.append(o[-1] + ((cnt[s][j] + 15) // 16) * 16)
        off.append(o)
    n_pieces = [(off[s][n_exp] + PIECE - 1) // PIECE for s in range(n_sub)]
    fits = n_pieces[0] < MOE_ROWS // PIECE
    for s in range(1, n_sub):
        fits = jnp.logical_and(fits, n_pieces[s] < MOE_ROWS // PIECE)

    def one_hot_rows(s, base, rows_n, js):
        slot = (lax.broadcasted_iota(jnp.int32, (rows_n, tm), 0) + base).astype(F32)
        pick = jnp.zeros((rows_n, tm), F32)
        wacc = jnp.zeros((rows_n, tm), F32)
        for j in js:
            hit = (slot - off[s][j].astype(F32)) == rank_ref[s, pl.ds(experts[j], 1), :]
            pick = jnp.where(hit, 1.0, pick)
            wacc = jnp.where(hit, w_ref[s, pl.ds(experts[j], 1), :], wacc)
        return pick.astype(BF16), jnp.sum(wacc, axis=1, keepdims=True)

    def mlp(xg, j):
        hid = jax.nn.silu(_dot(xg, wg_ref[j])) * _dot(xg, wu_ref[j])
        return _dot(hid.astype(BF16), wd_ref[j])

    def aligned(q, n):
        return q * n if isinstance(q, int) else pl.multiple_of(q * n, n)

    def piece(s, q):
        r = pl.ds(aligned(q, PIECE), PIECE)
        pick, wrow = one_hot_rows(s, q * PIECE, PIECE, range(n_exp))
        xc[s, r, :] = _dot(pick, h_ref[pl.ds(s * tm, tm), :]).astype(BF16)
        pc[s, :, r] = pick.T
        wc[s, r, :] = wrow
        yc[s, r, :] = jnp.zeros((PIECE, D_MODEL), BF16)

    def zero_piece(s, q):
        r = pl.ds(q * PIECE, PIECE)
        xc[s, r, :] = jnp.zeros((PIECE, D_MODEL), BF16)
        yc[s, r, :] = jnp.zeros((PIECE, D_MODEL), BF16)
        pc[s, :, r] = jnp.zeros((tm, PIECE), BF16)
        wc[s, r, :] = jnp.zeros((PIECE, 1), F32)

    def window(j, p, first):
        starts = [pl.multiple_of(jnp.minimum(off[s][j] + p * WIN, (n_pieces[s] + 1) * PIECE - WIN), 16)
                  for s in range(n_sub)]
        xg = jnp.concatenate([xc[s, pl.ds(starts[s], WIN), :] for s in range(n_sub)], axis=0)
        out = mlp(xg, j)
        for s in range(n_sub):
            r = pl.ds(starts[s], WIN)
            new = (out[s * WIN:(s + 1) * WIN] * wc[s, r, :] * gate).astype(BF16)
            if not first:
                rank = starts[s] - off[s][j] + lax.broadcasted_iota(jnp.int32, (WIN, 1), 0)
                new = jnp.where(rank < cnt[s][j], new, yc[s, r, :])
            yc[s, r, :] = new

    def scatter(s, start, n):
        r = pl.ds(start, n)
        o_ref[pl.ds(s * tm, tm), :] += _dot(pc[s, :, r], yc[s, r, :])

    @pl.when(fits)
    def _packed():
        for s in range(n_sub):
            piece(s, 0)
            piece(s, 1)
            zero_piece(s, 2)
        for s in range(n_sub):
            @pl.when(n_pieces[s] > 2)
            def _third(s=s):
                piece(s, 2)
                zero_piece(s, 3)
        for j in range(n_exp):
            window(j, 0, True)
        for j in range(n_exp):
            most = cnt[0][j]
            for s in range(1, n_sub):
                most = jnp.maximum(most, cnt[s][j])

            @pl.when(most > WIN)
            def _more(j=j, most=most):
                lax.fori_loop(1, (most + WIN - 1) // WIN, lambda p, c: (window(j, p, False), c)[1], 0)
        for s in range(n_sub):
            scatter(s, 0, 2 * PIECE)
        for s in range(n_sub):
            @pl.when(n_pieces[s] > 2)
            def _third_out(s=s):
                scatter(s, 2 * PIECE, PIECE)

    @pl.when(jnp.logical_not(fits))
    def _unpacked():
        for j in range(n_exp):
            for s in range(n_sub):
                rows = pl.ds(s * tm, tm)

                def block(bi, carry, j=j, s=s, rows=rows):
                    slot0 = off[s][j] + bi * PIECE
                    pick, wrow = one_hot_rows(s, slot0, PIECE, [j])
                    out = mlp(_dot(pick, h_ref[rows, :]).astype(BF16), j)
                    ow = (out * wrow * gate).astype(BF16)
                    o_ref[rows, :] += lax.dot_general(pick, ow, tn, preferred_element_type=F32)
                    return carry

                lax.fori_loop(0, (cnt[s][j] + PIECE - 1) // PIECE, block, 0)


def _moe(h2, x1, mod, rank_t, w_t, counts, w_gate, w_up, w_down, ws_gate, ws_up, ws_down, seq):
    T, D = h2.shape
    ns, E, tm = rank_t.shape
    tmo = min(MOE_OUTER, seq)
    n_sub = tmo // tm
    n_exp = MOE_EXPERTS_PER_STEP
    per_b = seq // tmo
    ws = [w_gate.astype(BF16), w_up.astype(BF16), w_down.astype(BF16)]
    sh = [ws_gate.astype(BF16), ws_up.astype(BF16), ws_down.astype(BF16)]
    tok = lambda: pl.BlockSpec((tmo, D), lambda t, e, c: (t, 0))
    sub = lambda: pl.BlockSpec((n_sub, E, tm), lambda t, e, c: (t, 0, 0))
    exp = lambda a: pl.BlockSpec((n_exp,) + a.shape[1:], lambda t, e, c: (e, 0, 0))
    full = lambda a: pl.BlockSpec(a.shape, lambda t, e, c: (0, 0))
    grid_spec = pltpu.PrefetchScalarGridSpec(
        num_scalar_prefetch=1,
        grid=(T // tmo, E // n_exp),
        in_specs=[tok(), tok(), pl.BlockSpec((None, N_MOD, D), lambda t, e, c: (t // per_b, 0, 0)),
                  sub(), sub(), exp(ws[0]), exp(ws[1]), exp(ws[2]), full(sh[0]), full(sh[1]), full(sh[2])],
        out_specs=tok(),
        scratch_shapes=[pltpu.VMEM((n_sub, MOE_ROWS, D), BF16), pltpu.VMEM((n_sub, MOE_ROWS, D), BF16),
                        pltpu.VMEM((n_sub, tm, MOE_ROWS), BF16), pltpu.VMEM((n_sub, MOE_ROWS, 1), F32)],
    )
    return pl.pallas_call(
        functools.partial(_moe_kernel, n_sub=n_sub, tm=tm, n_exp=n_exp),
        grid_spec=grid_spec,
        out_shape=jax.ShapeDtypeStruct((T, D), F32),
        compiler_params=_params(("arbitrary", "arbitrary")),
        name="moe",
    )(counts, h2, x1, mod, rank_t, w_t, *ws, *sh)


def kernel(x, c, w_ada, b_ada, norm_mix_gain, norm_ffn_gain, w_in, s5_lambda_re, s5_lambda_im, s5_log_dt, s5_b_re, s5_b_im, s5_c_re, s5_c_im, s5_d, s5_w_glu, s5_b_glu, q_norm_gain, k_norm_gain, cmp_pe, cmp_w1, cmp_b1, cmp_w2, cmp_b2, w_branch_a, w_branch_b, w_out, w_router, router_bias, w_gate, w_up, w_down, ws_gate, ws_up, ws_down):
    B, S, D = x.shape
    for l in range(w_ada.shape[0]):
        mod = _ada(c, w_ada[l], b_ada[l]).reshape(B, N_MOD, D)
        u, kvc, qt, ks, vst, kwn, vwt, gt, gm = _inproj(x, mod, norm_mix_gain[l:l + 1], w_in[l],
                                                        q_norm_gain[l], k_norm_gain[l])
        tables = _s5_tables(s5_lambda_re[l], s5_lambda_im[l], s5_log_dt[l], s5_b_re[l], s5_b_im[l],
                            s5_c_re[l], s5_c_im[l], s5_d[l])
        ys5 = _s5(u, tables)
        kvc_c = _compress(kvc, cmp_pe[l], cmp_w1[l], cmp_b1[l], cmp_w2[l], cmp_b2[l], k_norm_gain[l, 0])
        o_nsa = _attend(qt, ks, vst, kwn, vwt, gt, kvc_c)
        x1, h2, scores_t = _merge(x, mod, ys5, o_nsa, gm, s5_w_glu[l], s5_b_glu[l], w_branch_a[l],
                                  w_branch_b[l], w_out[l], norm_ffn_gain[l:l + 1], w_router[l])
        rank_t, w_t, cnt = _route(scores_t, router_bias[l])
        counts = cnt[:, :, 0].astype(jnp.int32).reshape(-1)
        x = _moe(h2.reshape(B * S, D), x1.reshape(B * S, D), mod, rank_t, w_t, counts,
                 w_gate[l], w_up[l], w_down[l], ws_gate[l], ws_up[l], ws_down[l], S).reshape(B, S, D)
    return x
```

```python
import functools
import math

import numpy as np
import jax
import jax.numpy as jnp
from jax import lax
from jax.experimental import pallas as pl
from jax.experimental.pallas import tpu as pltpu

F32 = jnp.float32
BF16 = jnp.bfloat16

D_MODEL = 1024
S5_WIDTH = 512
S5_GROUP = 16
S5_GROUPS = S5_WIDTH // S5_GROUP
S5_STATE = 64
N_HEADS = 8
N_KV_HEADS = 2
GQA_GROUP = N_HEADS // N_KV_HEADS
HEAD_DIM = 64
NSA_WIDTH = N_HEADS * HEAD_DIM
KV_WIDTH = 2 * N_KV_HEADS * HEAD_DIM
CMP_BLOCK = 32
CMP_STRIDE = 16
CMP_HIDDEN = 256
SEL_BLOCK = 64
SEL_TOPK = 8
WINDOW = 256
Q_BLOCK = 128
FORCE_BONUS = 1e3
N_EXPERTS = 64
TOP_K = 8
N_EXPERT_GROUPS = 8
TOPK_EXPERT_GROUPS = 4
EXPERT_HIDDEN = 256
SHARED_HIDDEN = 256
ROUTED_SCALE = 2.5
RMS_EPS = 1e-6
NEG_INF = -1e30
N_MOD = 6

LANES = 128
SUBLANES = 8
S5_CHUNK = 16
S5_CW = S5_CHUNK * S5_GROUP
ROW_TILE = 512
ATT_TILES = 2
SEL_KEY_CHUNK = 512
SEL_UNIT = 128
SEL_SLOTS = (5, 8)
POS_BASE = 64
POS_ROWS = 16
AUG = 128
ROUTE_TILE = 256
MOE_OUTER = 1024
MOE_EXPERTS_PER_STEP = 4
MOE_PIECE = 128
MOE_WIN = 48
MOE_ROWS = 512
RANK_NONE = -float(1 << 20)
VMEM_LIMIT = 56 * 1024 * 1024


def _dot(a, b):
    return jnp.dot(a, b, preferred_element_type=F32)


def _split(a):
    hi = a.astype(BF16)
    lo = (a - hi.astype(F32)).astype(BF16)
    return hi, lo


def _dot3(a, bh, bl):
    ah, al = _split(a)
    return _dot(ah, bh) + (_dot(al, bh) + _dot(ah, bl))


def _segment_transpose(x):
    assert x.shape[-2:] == (SUBLANES, LANES) and LANES // S5_GROUP == SUBLANES
    nd = x.ndim
    i = lax.broadcasted_iota(jnp.int32, x.shape, nd - 2)
    seg = lax.broadcasted_iota(jnp.int32, x.shape, nd - 1) // S5_GROUP
    out = x
    for d in range(1, SUBLANES):
        r = pltpu.roll(pltpu.roll(x, SUBLANES - d, axis=nd - 2), S5_GROUP * d, axis=nd - 1)
        out = jnp.where(seg == ((i + d) & (SUBLANES - 1)), r, out)
    return out


def _to_chunk_major(u, put):
    rows = u.shape[0]
    u3 = u.reshape(rows // SUBLANES, SUBLANES, S5_WIDTH)
    halves = S5_CHUNK // SUBLANES
    for jb in range(S5_WIDTH // LANES):
        t = _segment_transpose(u3[:, :, LANES * jb:LANES * (jb + 1)])
        t = t.reshape(rows // S5_CHUNK, halves, SUBLANES, LANES)
        for hf in range(halves):
            put(slice(SUBLANES * jb, SUBLANES * (jb + 1)), slice(LANES * hf, LANES * (hf + 1)), t[:, hf])


def _from_chunk_major(get, rows):
    halves = S5_CHUNK // SUBLANES
    cols = []
    for jb in range(S5_WIDTH // LANES):
        parts = [_segment_transpose(get(slice(SUBLANES * jb, SUBLANES * (jb + 1)), slice(LANES * hf, LANES * (hf + 1))))
                 for hf in range(halves)]
        cols.append(jnp.stack(parts, axis=1).reshape(rows, LANES))
    return jnp.concatenate(cols, axis=1)


def _params(sem):
    return pltpu.CompilerParams(dimension_semantics=sem, vmem_limit_bytes=VMEM_LIMIT)


def _ada_kernel(c_ref, w_ref, b_ref, o_ref):
    cs = jax.nn.silu(c_ref[...])
    wh, wl = _split(w_ref[...])
    o_ref[...] = _dot3(cs, wh, wl) + b_ref[...]


def _ada(c, w_ada, b_ada):
    B, D = c.shape
    return pl.pallas_call(
        _ada_kernel,
        grid=(N_MOD,),
        in_specs=[pl.BlockSpec((B, D), lambda j: (0, 0)),
                  pl.BlockSpec((D, D), lambda j: (0, j)),
                  pl.BlockSpec((1, D), lambda j: (0, j))],
        out_specs=pl.BlockSpec((B, D), lambda j: (0, j)),
        out_shape=jax.ShapeDtypeStruct((B, N_MOD * D), F32),
        compiler_params=_params(("arbitrary",)),
        name="ada",
    )(c, w_ada, b_ada.reshape(1, N_MOD * D))


def _head_norm(v, bd, gain):
    sq = v * v
    sh, sl = _split(sq)
    ms = _dot(sh, bd) + _dot(sl, bd)
    return v * lax.rsqrt(ms + RMS_EPS) * gain


def _inproj_kernel(x_ref, mod_ref, gain_ref, wm_ref, wt_ref, wgn_ref, wgm_ref, bd_ref, qg_ref, kg_ref,
                   u_ref, kvc_ref, qt_ref, ksa_ref, vst_ref, kwa_ref, vwt_ref, gt_ref, gm_ref):
    x = x_ref[...]
    shift = mod_ref[0:1, :]
    scale = mod_ref[1:2, :]
    y = x * lax.rsqrt(jnp.mean(x * x, axis=-1, keepdims=True) + RMS_EPS)
    h = (y * gain_ref[...]) * (1.0 + scale) + shift
    hb = h.astype(BF16)
    main = _dot(hb, wm_ref[...])
    tm = x.shape[0]

    def put(groups, lanes, block):
        u_ref[:, groups, lanes] = block

    _to_chunk_major(main[:, :S5_WIDTH], put)
    o = S5_WIDTH
    kvc_ref[...] = main[:, o:o + KV_WIDTH]
    o += KV_WIDTH
    kw = N_KV_HEADS * HEAD_DIM
    bd = bd_ref[...]
    ks = _head_norm(main[:, o:o + kw], bd[:kw, :kw], kg_ref[1:2, :]).astype(BF16)
    kwn = _head_norm(main[:, o + kw:o + 2 * kw], bd[:kw, :kw], kg_ref[2:3, :]).astype(BF16)
    pos = pl.program_id(1) * tm + lax.broadcasted_iota(jnp.int32, (tm, 1), 0)
    lane = lax.broadcasted_iota(jnp.int32, (tm, AUG), 1)
    digits = jnp.where(lane == 0, pos // POS_BASE, jnp.where(lane == 1, pos % POS_BASE, 0))
    digits = digits[:, :AUG - HEAD_DIM].astype(F32).astype(BF16)
    onehot = jnp.where(lane == pos // SEL_BLOCK, 1.0, 0.0).astype(BF16)
    for hh in range(N_KV_HEADS):
        ksa_ref[hh, :, 0:HEAD_DIM] = ks[:, hh * HEAD_DIM:(hh + 1) * HEAD_DIM]
        ksa_ref[hh, :, HEAD_DIM:AUG] = digits
        ksa_ref[hh, :, AUG:] = onehot
        kwa_ref[hh, :, 0:HEAD_DIM] = kwn[:, hh * HEAD_DIM:(hh + 1) * HEAD_DIM]
        kwa_ref[hh, :, HEAD_DIM:] = digits

    nt = (((1,), (1,)), ((), ()))
    tt = lax.dot_general(wt_ref[...], hb, nt, preferred_element_type=F32)
    qt = tt[:NSA_WIDTH]
    sq = qt * qt
    sh, sl = _split(sq)
    qn = (qt * lax.rsqrt(_dot(bd, sh) + _dot(bd, sl) + RMS_EPS) * qg_ref[...] * (HEAD_DIM ** -0.5)).astype(BF16)
    QW = GQA_GROUP * Q_BLOCK
    for qb in range(tm // Q_BLOCK):
        for hd in range(N_HEADS):
            hh, g = divmod(hd, GQA_GROUP)
            qt_ref[hh, :, qb * QW + g * Q_BLOCK:qb * QW + (g + 1) * Q_BLOCK] = (
                qn[hd * HEAD_DIM:(hd + 1) * HEAD_DIM, qb * Q_BLOCK:(qb + 1) * Q_BLOCK])
    vst_ref[...] = tt[NSA_WIDTH:NSA_WIDTH + kw].reshape(N_KV_HEADS, HEAD_DIM, tm).astype(BF16)
    vwt_ref[...] = tt[NSA_WIDTH + kw:].reshape(N_KV_HEADS, HEAD_DIM, tm).astype(BF16)
    gn = jax.nn.sigmoid(lax.dot_general(wgn_ref[...], hb, nt, preferred_element_type=F32))
    for qb in range(tm // Q_BLOCK):
        for br in range(3):
            for hd in range(N_HEADS):
                hh, g = divmod(hd, GQA_GROUP)
                r = br * N_HEADS + hd
                gt_ref[hh, br:br + 1, qb * QW + g * Q_BLOCK:qb * QW + (g + 1) * Q_BLOCK] = (
                    gn[r:r + 1, qb * Q_BLOCK:(qb + 1) * Q_BLOCK])
    gm_ref[...] = jax.nn.sigmoid(_dot(hb, wgm_ref[...])).astype(BF16)


def _s5_chunk_spec(tm):
    assert S5_CHUNK == 16 and S5_GROUP == 16, "the segment transposes assume 16 steps x 16 channels"
    return pl.BlockSpec((tm // S5_CHUNK, None, S5_GROUPS, S5_CW), lambda b, i: (i, b, 0, 0))


def _inproj(x, mod, gain, w_in, q_gain, k_gain):
    B, S, D = x.shape
    tm = min(ROW_TILE, S)
    assert S // SEL_BLOCK <= AUG and tm % Q_BLOCK == 0
    kw = N_KV_HEADS * HEAD_DIM
    cols = np.cumsum((0,) + (S5_WIDTH, NSA_WIDTH, KV_WIDTH, KV_WIDTH, KV_WIDTH, 3 * N_HEADS, 2 * D))
    c_u, c_q, c_kvc, c_kvs, c_kvw, c_gn, c_gm = cols[:7]
    sl = lambda a, n: w_in[:, a:a + n]
    wm = jnp.concatenate([sl(c_u, S5_WIDTH), sl(c_kvc, KV_WIDTH), sl(c_kvs, kw), sl(c_kvw, kw)], axis=1).astype(BF16)
    wt = jnp.concatenate([sl(c_q, NSA_WIDTH), sl(c_kvs + kw, kw), sl(c_kvw + kw, kw)], axis=1).T.astype(BF16)
    wgn = sl(c_gn, 3 * N_HEADS).T.astype(BF16)
    wgm = sl(c_gm, 2 * D).astype(BF16)
    seg = np.arange(NSA_WIDTH) // HEAD_DIM
    bd = jnp.asarray((seg[:, None] == seg[None, :]).astype(np.float32) / HEAD_DIM, BF16)
    qg = jnp.tile(q_gain, N_HEADS).reshape(NSA_WIDTH, 1)
    kg = jnp.tile(k_gain, (1, N_KV_HEADS))
    nq = tm // Q_BLOCK * GQA_GROUP * Q_BLOCK
    row = lambda w: pl.BlockSpec((None, tm, w), lambda b, i: (b, i, 0))
    full = lambda a: pl.BlockSpec(a.shape, lambda b, i: (0,) * a.ndim)
    rows4 = lambda w: pl.BlockSpec((None, N_KV_HEADS, tm, w), lambda b, i: (b, 0, i, 0))
    cols4 = lambda r, w: pl.BlockSpec((None, N_KV_HEADS, r, w), lambda b, i: (b, 0, 0, i))
    nqt = S // Q_BLOCK * GQA_GROUP * Q_BLOCK
    return pl.pallas_call(
        _inproj_kernel,
        grid=(B, S // tm),
        in_specs=[row(D), pl.BlockSpec((None, N_MOD, D), lambda b, i: (b, 0, 0)),
                  full(gain), full(wm), full(wt), full(wgn), full(wgm), full(bd), full(qg), full(kg)],
        out_specs=[_s5_chunk_spec(tm), row(KV_WIDTH), cols4(HEAD_DIM, nq), rows4(2 * AUG), cols4(HEAD_DIM, tm),
                   rows4(AUG), cols4(HEAD_DIM, tm), cols4(3, nq), row(2 * D)],
        out_shape=[jax.ShapeDtypeStruct((S // S5_CHUNK, B, S5_GROUPS, S5_CW), F32),
                   jax.ShapeDtypeStruct((B, S, KV_WIDTH), F32),
                   jax.ShapeDtypeStruct((B, N_KV_HEADS, HEAD_DIM, nqt), BF16),
                   jax.ShapeDtypeStruct((B, N_KV_HEADS, S, 2 * AUG), BF16),
                   jax.ShapeDtypeStruct((B, N_KV_HEADS, HEAD_DIM, S), BF16),
                   jax.ShapeDtypeStruct((B, N_KV_HEADS, S, AUG), BF16),
                   jax.ShapeDtypeStruct((B, N_KV_HEADS, HEAD_DIM, S), BF16),
                   jax.ShapeDtypeStruct((B, N_KV_HEADS, 3, nqt), F32),
                   jax.ShapeDtypeStruct((B, S, 2 * D), BF16)],
        compiler_params=_params(("arbitrary", "arbitrary")),
        name="inproj",
    )(x, mod, gain, wm, wt, wgn, wgm, bd, qg, kg)


def _s5_tables(lam_re, lam_im, log_dt, b_re, b_im, c_re, c_im, d_skip):
    L, P, N, G = S5_CHUNK, S5_GROUP, S5_STATE, S5_GROUPS
    hp = lax.Precision.HIGHEST
    lr, li = lam_re.astype(F32), lam_im.astype(F32)
    dt = jnp.exp(log_dt.astype(F32))[:, None]
    mag = jnp.exp(lr * dt)
    abar_re, abar_im = mag * jnp.cos(li * dt), mag * jnp.sin(li * dt)
    num_re, num_im = abar_re - 1.0, abar_im
    den = lr * lr + li * li
    coef_re = (num_re * lr + num_im * li) / den
    coef_im = (num_im * lr - num_re * li) / den
    br, bi = b_re.astype(F32), b_im.astype(F32)
    bbar_re = coef_re[..., None] * br - coef_im[..., None] * bi
    bbar_im = coef_re[..., None] * bi + coef_im[..., None] * br
    k = jnp.arange(L + 1, dtype=F32)[:, None, None]
    pmag = jnp.exp(lr * dt * k)
    pre, pim = pmag * jnp.cos(li * dt * k), pmag * jnp.sin(li * dt * k)
    cr, ci = c_re.astype(F32), c_im.astype(F32)
    ca_re = cr[None] * pre[:, :, None, :] - ci[None] * pim[:, :, None, :]
    ca_im = cr[None] * pim[:, :, None, :] + ci[None] * pre[:, :, None, :]
    kern = (jnp.einsum('kgpn,gnq->gkqp', ca_re[:L], bbar_re, precision=hp)
            - jnp.einsum('kgpn,gnq->gkqp', ca_im[:L], bbar_im, precision=hp))
    s_i = np.arange(L)[:, None]
    t_i = np.arange(L)[None, :]
    tau = np.clip(t_i - s_i, 0, L - 1)
    causal = jnp.asarray((t_i >= s_i).astype(np.float32))
    mt = kern[:, tau] * causal[None, :, :, None, None]
    mt = mt.transpose(0, 1, 3, 2, 4).reshape(G, L * P, L * P)
    rev = np.arange(L - 1, -1, -1)
    ab_re = pre[rev][..., None] * bbar_re[None] - pim[rev][..., None] * bbar_im[None]
    ab_im = pre[rev][..., None] * bbar_im[None] + pim[rev][..., None] * bbar_re[None]
    ws = jnp.concatenate([ab_re, ab_im], axis=2)
    ws = ws.transpose(1, 0, 3, 2).reshape(G, L * P, 2 * N)
    wo = jnp.concatenate([ca_re[1:], -ca_im[1:]], axis=3)
    wo = wo.transpose(1, 3, 0, 2).reshape(G, 2 * N, L * P)
    al = jnp.stack([pre[L], pim[L]], axis=1)
    dv = jnp.tile(d_skip.astype(F32), (1, L)).reshape(G, 1, L * P)
    return mt, ws, wo, al, dv


def _s5_kernel(u_ref, mth_ref, mtl_ref, wsh_ref, wsl_ref, woh_ref, wol_ref, al_ref, dv_ref,
               y_ref, vr_scr, vi_scr, xr_scr, xi_scr, *, n_chunks, bsz):
    N = S5_STATE
    u = u_ref[...]
    uh, ul = _split(u)
    y = _dot(uh, mth_ref[...]) + (_dot(ul, mth_ref[...]) + _dot(uh, mtl_ref[...]))
    v = _dot(uh, wsh_ref[...]) + (_dot(ul, wsh_ref[...]) + _dot(uh, wsl_ref[...]))
    vr_scr[...] = v[:, :N]
    vi_scr[...] = v[:, N:]
    a_r = al_ref[0:1, :]
    a_i = al_ref[1:2, :]

    def step(c, x):
        xr, xi = x
        r = pl.ds(pl.multiple_of(c * bsz, bsz), bsz)
        xr_scr[r, :] = xr
        xi_scr[r, :] = xi
        return (a_r * xr - a_i * xi + vr_scr[r, :], a_r * xi + a_i * xr + vi_scr[r, :])

    zero = jnp.zeros((bsz, N), F32)
    lax.fori_loop(0, n_chunks, step, (zero, zero), unroll=8)
    xp = jnp.concatenate([xr_scr[...], xi_scr[...]], axis=1)
    y = y + _dot3(xp, woh_ref[...], wol_ref[...])
    y_ref[...] = y + dv_ref[...] * u


def _s5(ug, tables):
    nc, B, G, cw = ug.shape
    N = S5_STATE
    mt, ws, wo, al, dv = tables
    mth, mtl = _split(mt)
    wsh, wsl = _split(ws)
    woh, wol = _split(wo)
    grp = lambda a: pl.BlockSpec((None,) + a.shape[1:], lambda g: (g,) + (0,) * (a.ndim - 1))
    ugt = ug.reshape(nc * B, G, cw).transpose(1, 0, 2)
    y = pl.pallas_call(
        functools.partial(_s5_kernel, n_chunks=nc, bsz=B),
        grid=(G,),
        in_specs=[grp(ugt), grp(mth), grp(mtl), grp(wsh), grp(wsl), grp(woh), grp(wol), grp(al), grp(dv)],
        out_specs=grp(ugt),
        out_shape=jax.ShapeDtypeStruct(ugt.shape, F32),
        scratch_shapes=[pltpu.VMEM((nc * B, N), F32)] * 4,
        compiler_params=_params(("arbitrary",)),
        name="s5",
    )(ugt, mth, mtl, wsh, wsl, woh, wol, al, dv)
    return y.transpose(1, 0, 2).reshape(nc, B, G, cw)


def _compress_kernel(x_ref, pe_ref, w1_ref, b1_ref, w2_ref, b2_ref, kg_ref, o_ref, *, n_rows):
    x = x_ref[...]
    half = CMP_STRIDE * HEAD_DIM
    a = _dot((x + pe_ref[0:1, :]).astype(BF16), w1_ref[:half, :])
    b = _dot((x + pe_ref[1:2, :]).astype(BF16), w1_ref[half:, :])
    hid = jax.nn.gelu(a + pltpu.roll(b, n_rows - 1, axis=0) + b1_ref[...])
    out = _dot(hid.astype(BF16), w2_ref[...]) + b2_ref[...]
    normed = out * lax.rsqrt(jnp.mean(out * out, axis=-1, keepdims=True) + RMS_EPS) * kg_ref[...]
    is_key = pl.program_id(1) < N_KV_HEADS
    o_ref[...] = jnp.where(is_key, normed, out).astype(BF16)


def _compress(kv_c, cmp_pe, cmp_w1, cmp_b1, cmp_w2, cmp_b2, k_gain0):
    B, S, _ = kv_c.shape
    nr = S // CMP_STRIDE
    half = CMP_STRIDE * HEAD_DIM
    nj = 2 * N_KV_HEADS
    xc = kv_c.reshape(B, nr, CMP_STRIDE, nj, HEAD_DIM).transpose(0, 3, 1, 2, 4).reshape(B, nj, nr, half)
    pe = cmp_pe.reshape(2, 2, half)
    w1 = cmp_w1.astype(BF16)
    w2 = cmp_w2.astype(BF16)
    b1 = cmp_b1.reshape(2, 1, CMP_HIDDEN)
    b2 = cmp_b2.reshape(2, 1, HEAD_DIM)
    kv = lambda a: pl.BlockSpec((None,) + a.shape[1:], lambda b, j: (j // N_KV_HEADS,) + (0,) * (a.ndim - 1))
    return pl.pallas_call(
        functools.partial(_compress_kernel, n_rows=nr),
        grid=(B, nj),
        in_specs=[pl.BlockSpec((None, None, nr, half), lambda b, j: (b, j, 0, 0)),
                  kv(pe), kv(w1), kv(b1), kv(w2), kv(b2),
                  pl.BlockSpec((1, HEAD_DIM), lambda b, j: (0, 0))],
        out_specs=pl.BlockSpec((None, None, nr, HEAD_DIM), lambda b, j: (b, j, 0, 0)),
        out_shape=jax.ShapeDtypeStruct((B, nj, nr, HEAD_DIM), BF16),
        compiler_params=_params(("arbitrary", "arbitrary")),
        name="compress",
    )(xc, pe, w1, b1, w2, b2, k_gain0.reshape(1, HEAD_DIM))


def _softmax_cols(s, mask):
    s = jnp.where(mask, s, NEG_INF)
    m = jnp.max(s, axis=0, keepdims=True)
    p = jnp.where(mask, jnp.exp(s - m), 0.0)
    return p / jnp.maximum(jnp.sum(p, axis=0, keepdims=True), 1e-20)


def _attend_kernel(qt_ref, kc_ref, vct_ref, ks_ref, vst_ref, kw_ref, vwt_ref, g_ref, slope_ref, ovt_ref,
                   o_ref, qa_scr, sa_scr, sb_scr, *, n_sel, n_pick, n_cmp_rows):
    QW = GQA_GROUP * Q_BLOCK
    chains = [(k, h) for k in range(ATT_TILES) for h in range(N_KV_HEADS)]
    t0s = [(pl.program_id(1) * ATT_TILES + k) * Q_BLOCK for k in range(ATT_TILES)]
    tqs = [t0 + (lax.broadcasted_iota(jnp.int32, (1, QW), 1) & (Q_BLOCK - 1)) for t0 in t0s]
    cols = lambda k: slice(k * QW, (k + 1) * QW)

    r = lax.broadcasted_iota(jnp.int32, (POS_ROWS, QW), 0)
    qa = {}
    for k, h in chains:
        slope = slope_ref[h]
        qa_scr[k, h, 0:HEAD_DIM, :] = qt_ref[h, :, cols(k)]
        qa_scr[k, h, HEAD_DIM:HEAD_DIM + POS_ROWS, :] = jnp.where(
            r == 0, slope * POS_BASE, jnp.where(r == 1, slope, 0.0)).astype(BF16)
        qa_scr[k, h, HEAD_DIM + POS_ROWS:AUG, :] = jnp.zeros((AUG - HEAD_DIM - POS_ROWS, QW), BF16)
        qa[k, h] = qa_scr[k, h, 0:AUG, :]

    def compressed(k, h, rows):
        sc = _dot(kc_ref[h, 0:rows, :], qa[k, h])
        cpos = lax.broadcasted_iota(jnp.int32, (rows, QW), 0) * CMP_STRIDE + (CMP_BLOCK - 1)
        p_c = _softmax_cols(sc, cpos <= tqs[k])
        o_c = _dot(vct_ref[h, :, 0:rows], p_c.astype(BF16))
        psum = p_c[:, 0:Q_BLOCK]
        for g in range(1, GQA_GROUP):
            psum = psum + p_c[:, g * Q_BLOCK:(g + 1) * Q_BLOCK]
        ph, pl_ = _split(psum)
        return o_c, _dot(ovt_ref[:, 0:rows], ph) + _dot(ovt_ref[:, 0:rows], pl_)

    half = n_cmp_rows // 2
    if half % LANES == 0:
        early = (t0s[-1] + Q_BLOCK - CMP_BLOCK) // CMP_STRIDE < half
        cmp_out = lax.cond(early, lambda _: [compressed(k, h, half) for k, h in chains],
                           lambda _: [compressed(k, h, n_cmp_rows) for k, h in chains], 0)
    else:
        cmp_out = [compressed(k, h, n_cmp_rows) for k, h in chains]
    cmp_out = dict(zip(chains, cmp_out))

    def select(k, h, imp):
        jb = lax.broadcasted_iota(jnp.int32, (n_sel, Q_BLOCK), 0)
        cur = (t0s[k] + lax.broadcasted_iota(jnp.int32, (1, Q_BLOCK), 1)) // SEL_BLOCK
        forced = (jb == 0) | (jb == cur) | (jb == cur - 1)
        imp = jnp.where(forced, imp + FORCE_BONUS, imp)
        imp = jnp.where(jb <= cur, imp, -1.0)
        bias = jnp.full((n_sel, Q_BLOCK), NEG_INF, F32)
        for _ in range(n_pick):
            m = jnp.max(imp, axis=0, keepdims=True)
            first = jnp.min(jnp.where(imp == m, jb, n_sel), axis=0, keepdims=True)
            hit = jb == first
            bias = jnp.where(hit, 0.0, bias)
            imp = jnp.where(hit, -jnp.inf, imp)
        per_unit = SEL_UNIT // SEL_BLOCK
        n_units = n_sel // per_unit
        picked = jnp.where(bias == 0.0, 1.0, 0.0).astype(BF16)
        in_unit = (lax.broadcasted_iota(jnp.int32, (n_units, n_sel), 1) // per_unit
                   == lax.broadcasted_iota(jnp.int32, (n_units, n_sel), 0))
        hits = _dot(jnp.where(in_unit, 1.0, 0.0).astype(BF16), picked)
        live = jnp.where(jnp.max(hits, axis=1, keepdims=True) > 0.5, 1.0, 0.0)
        live = jnp.broadcast_to(live, (n_units, LANES))
        earlier = (lax.broadcasted_iota(jnp.int32, (n_units, n_units), 1)
                   < lax.broadcasted_iota(jnp.int32, (n_units, n_units), 0))
        before = _dot(jnp.where(earlier, 1.0, 0.0).astype(BF16), live.astype(BF16))
        slot = lax.broadcasted_iota(jnp.int32, (n_units, LANES), 1).astype(F32)
        in_slot = jnp.where(before == slot, live, 0.0).astype(BF16)
        r8 = lax.broadcasted_iota(jnp.int32, (SUBLANES, n_units), 0)
        c8 = lax.broadcasted_iota(jnp.int32, (SUBLANES, n_units), 1)
        pick_rows = jnp.where(r8 == 0, c8, jnp.where(r8 == 1, 1, 0)).astype(F32).astype(BF16)
        ids = _dot(pick_rows, in_slot)
        n_live = (before[n_units - 1:n_units, 0:1] + live[n_units - 1:n_units, 0:1])[0, 0]
        bias = jnp.concatenate([bias] * GQA_GROUP, axis=1).astype(BF16)
        qa_scr[k, h, AUG:, :] = jnp.concatenate([bias, jnp.zeros((AUG - n_sel, QW), BF16)], axis=0)
        return ids, n_live

    def window(k, h):
        WK = WINDOW + Q_BLOCK
        w0 = pl.multiple_of(jnp.maximum(t0s[k] - WINDOW, 0), Q_BLOCK)
        sw = _dot(kw_ref[h, pl.ds(w0, WK), :], qa[k, h])
        dist = tqs[k] - (w0 + lax.broadcasted_iota(jnp.int32, (WK, QW), 0))
        in_win = (dist | (WINDOW - 1 - dist)) >= 0
        p_w = _softmax_cols(sw, in_win)
        return _dot(vwt_ref[h, :, pl.ds(w0, WK)], p_w.astype(BF16))

    lists = {c: select(*c, cmp_out[c][1]) for c in chains}
    o_win = {c: window(*c) for c in chains}

    def gathered(k, h, ids, n_slots):
        units = [ids[0, s].astype(jnp.int32) for s in range(n_slots)]
        starts = [pl.multiple_of(u * SEL_UNIT, SEL_UNIT) for u in units]
        keys = jnp.concatenate([ks_ref[h, pl.ds(st, SEL_UNIT), :] for st in starts], axis=0)
        s = _dot(keys, qa_scr[k, h])
        krow = lax.broadcasted_iota(jnp.int32, (SEL_UNIT, QW), 0)
        seen = jnp.concatenate(
            [st + krow <= jnp.where(ids[1, si] > 0.5, tqs[k], -1) for si, st in enumerate(starts)], axis=0)
        s = jnp.where(seen, s, NEG_INF)
        p = jnp.exp(s - jnp.max(s, axis=0, keepdims=True))
        vals = jnp.concatenate([vst_ref[h, :, pl.ds(st, SEL_UNIT)] for st in starts], axis=1)
        return _dot(vals, p.astype(BF16)) / jnp.maximum(jnp.sum(p, axis=0, keepdims=True), 1e-20)

    KC = SEL_KEY_CHUNK

    def swept(k, h):
        last = (t0s[k] + Q_BLOCK + KC - 1) // KC - 1

        def scores_to(ref, j):
            k0 = pl.multiple_of(j * KC, KC)
            ref[...] = _dot(ks_ref[h, pl.ds(k0, KC), :], qa_scr[k, h])

        def accumulate(carry, j, s, p_of):
            m, l, acc = carry
            k0 = pl.multiple_of(j * KC, KC)
            m_new = jnp.maximum(m, jnp.max(s, axis=0, keepdims=True))
            alpha = jnp.exp(m - m_new)
            p = p_of(jnp.exp(s - m_new))
            l = alpha * l + jnp.sum(p, axis=0, keepdims=True)
            acc = alpha * acc + _dot(vst_ref[h, :, pl.ds(k0, KC)], p.astype(BF16))
            return m_new, l, acc

        keep = lambda p: p

        def two_chunks(i, state):
            j = 2 * i
            scores_to(sb_scr, j + 1)
            state = accumulate(state, j, sa_scr[...], keep)
            scores_to(sa_scr, j + 2)
            return accumulate(state, j + 1, sb_scr[...], keep)

        def odd_chunk(state):
            state = accumulate(state, last - 1, sa_scr[...], keep)
            scores_to(sa_scr, last)
            return state

        scores_to(sa_scr, 0)
        init = (jnp.full((1, QW), NEG_INF, F32), jnp.zeros((1, QW), F32), jnp.zeros((HEAD_DIM, QW), F32))
        state = lax.fori_loop(0, last // 2, two_chunks, init)
        state = lax.cond(last % 2 == 1, odd_chunk, lambda st: st, state)
        visible = (last * KC + lax.broadcasted_iota(jnp.int32, (KC, QW), 0)) <= tqs[k]
        _, l_s, acc_s = accumulate(state, last, jnp.where(visible, sa_scr[...], NEG_INF),
                                   lambda p: jnp.where(visible, p, 0.0))
        return acc_s / jnp.maximum(l_s, 1e-20)

    few, some = SEL_SLOTS
    for k in range(ATT_TILES):
        outs = []
        for h in range(N_KV_HEADS):
            ids, n_live = lists[k, h]
            o_s = lax.cond(n_live <= few, lambda _: gathered(k, h, ids, few),
                           lambda _: lax.cond(n_live <= some, lambda _: gathered(k, h, ids, some),
                                              lambda _: swept(k, h), 0), 0)
            gk = lambda br: g_ref[h, br:br + 1, cols(k)]
            o = gk(0) * cmp_out[k, h][0] + gk(1) * o_s + gk(2) * o_win[k, h]
            for g in range(0, GQA_GROUP, 2):
                sq = jnp.concatenate([o[:, g * Q_BLOCK:(g + 1) * Q_BLOCK], o[:, (g + 1) * Q_BLOCK:(g + 2) * Q_BLOCK]],
                                     axis=0)
                outs.append(sq.T)
        o_ref[k * Q_BLOCK:(k + 1) * Q_BLOCK, :] = jnp.concatenate(outs, axis=1).astype(BF16)


def _attend(qt, ks, vst, kwn, vwt, gt, kvc_c):
    B, _, S, _ = ks.shape
    nqb = S // Q_BLOCK
    QW = GQA_GROUP * Q_BLOCK
    n_sel = S // SEL_BLOCK
    n_pick = min(SEL_TOPK, n_sel)
    nr = S // CMP_STRIDE
    n_cmp = (S - CMP_BLOCK) // CMP_STRIDE + 1
    assert n_sel % 16 == 0 and n_sel <= AUG

    cpos = np.arange(nr) * CMP_STRIDE + CMP_BLOCK - 1
    digits = np.zeros((nr, AUG - HEAD_DIM), np.float32)
    digits[:, 0] = cpos // POS_BASE
    digits[:, 1] = cpos % POS_BASE
    kc = kvc_c[:, :N_KV_HEADS]
    kc = jnp.concatenate([kc, jnp.broadcast_to(jnp.asarray(digits, BF16), kc.shape[:2] + digits.shape)], axis=-1)
    vct = kvc_c[:, N_KV_HEADS:].transpose(0, 1, 3, 2)
    slopes = 2.0 ** (-8.0 * np.arange(1, N_HEADS + 1) / N_HEADS)
    slope_t = jnp.asarray(np.repeat(slopes.reshape(N_KV_HEADS, GQA_GROUP), Q_BLOCK, axis=1)
                          .reshape(N_KV_HEADS, 1, QW), F32)
    cmp_start = np.arange(n_cmp) * CMP_STRIDE
    sel_start = np.arange(n_sel) * SEL_BLOCK
    ov = ((cmp_start[:, None] <= sel_start[None, :] + SEL_BLOCK - 1)
          & (cmp_start[:, None] + CMP_BLOCK - 1 >= sel_start[None, :])).astype(np.float32)
    ovt = np.zeros((n_sel, nr), np.float32)
    ovt[:, :n_cmp] = ov.T
    ovt = jnp.asarray(ovt, BF16)

    H = N_KV_HEADS
    per_bh = lambda r, c: pl.BlockSpec((None, H, r, c), lambda b, i: (b, 0, 0, 0))
    per_q = lambda r: pl.BlockSpec((None, H, r, ATT_TILES * QW), lambda b, i: (b, 0, 0, i))
    return pl.pallas_call(
        functools.partial(_attend_kernel, n_sel=n_sel, n_pick=n_pick, n_cmp_rows=nr),
        grid=(B, nqb // ATT_TILES),
        in_specs=[per_q(HEAD_DIM), per_bh(nr, AUG), per_bh(HEAD_DIM, nr),
                  per_bh(S, 2 * AUG), per_bh(HEAD_DIM, S), per_bh(S, AUG), per_bh(HEAD_DIM, S),
                  per_q(3),
                  pl.BlockSpec((H, 1, QW), lambda b, i: (0, 0, 0)),
                  pl.BlockSpec((n_sel, nr), lambda b, i: (0, 0))],
        out_specs=pl.BlockSpec((None, ATT_TILES * Q_BLOCK, NSA_WIDTH), lambda b, i: (b, i, 0)),
        out_shape=jax.ShapeDtypeStruct((B, S, NSA_WIDTH), BF16),
        scratch_shapes=[pltpu.VMEM((ATT_TILES, H, 2 * AUG, QW), BF16), pltpu.VMEM((SEL_KEY_CHUNK, QW), F32),
                        pltpu.VMEM((SEL_KEY_CHUNK, QW), F32)],
        compiler_params=_params(("arbitrary", "arbitrary")),
        name="attend",
    )(qt, kc, vct, ks, vst, kwn, vwt, gt, slope_t, ovt)


def _merge_kernel(x_ref, mod_ref, ys_ref, on_ref, gm_ref, wglu_ref, bglu_ref, wa_ref, wb_ref, wo_ref,
                  gain_ref, wrh_ref, wrl_ref, x1_ref, h2_ref, sc_ref):
    D = D_MODEL
    z = jax.nn.gelu(_from_chunk_major(lambda groups, lanes: ys_ref[:, groups, lanes], x_ref.shape[0]))
    glu = z * jax.nn.sigmoid(_dot(z.astype(BF16), wglu_ref[...]) + bglu_ref[...])
    ya = _dot(glu.astype(BF16), wa_ref[...])
    yb = _dot(on_ref[...], wb_ref[...])
    merged = gm_ref[:, :D].astype(F32) * ya + gm_ref[:, D:].astype(F32) * yb
    x1 = x_ref[...] + mod_ref[2:3, :] * _dot(merged.astype(BF16), wo_ref[...])
    x1_ref[...] = x1
    y = x1 * lax.rsqrt(jnp.mean(x1 * x1, axis=-1, keepdims=True) + RMS_EPS)
    h2 = (y * gain_ref[...]) * (1.0 + mod_ref[4:5, :]) + mod_ref[3:4, :]
    h2_ref[...] = h2.astype(BF16)
    hh, hl = _split(h2)
    nt = (((1,), (1,)), ((), ()))
    dg = lambda a, b: lax.dot_general(a, b, nt, preferred_element_type=F32)
    logits = dg(wrh_ref[...], hh) + (dg(wrh_ref[...], hl) + dg(wrl_ref[...], hh))
    sc_ref[...] = jax.nn.sigmoid(logits)


def _merge(x, mod, ys5, o_nsa, gm, w_glu, b_glu, w_a, w_b, w_out, gain_f, w_router):
    B, S, D = x.shape
    tm = min(ROW_TILE, S)
    wrh, wrl = _split(w_router.T)
    ws = [w_glu.astype(BF16), b_glu.reshape(1, -1), w_a.astype(BF16), w_b.astype(BF16), w_out.astype(BF16),
          gain_f, wrh, wrl]
    row = lambda w: pl.BlockSpec((None, tm, w), lambda b, i: (b, i, 0))
    full = lambda a: pl.BlockSpec(a.shape, lambda b, i: (0,) * a.ndim)
    return pl.pallas_call(
        _merge_kernel,
        grid=(B, S // tm),
        in_specs=[row(D), pl.BlockSpec((None, N_MOD, D), lambda b, i: (b, 0, 0)),
                  _s5_chunk_spec(tm), row(NSA_WIDTH), row(2 * D)] + [full(a) for a in ws],
        out_specs=[row(D), row(D), pl.BlockSpec((None, N_EXPERTS, tm), lambda b, i: (b, 0, i))],
        out_shape=[jax.ShapeDtypeStruct((B, S, D), F32), jax.ShapeDtypeStruct((B, S, D), BF16),
                   jax.ShapeDtypeStruct((B, N_EXPERTS, S), F32)],
        compiler_params=_params(("arbitrary", "arbitrary")),
        name="merge",
    )(x, mod, ys5, o_nsa, gm, *ws)


def _first_argmax_rows(v, idx, n):
    m = jnp.max(v, axis=0, keepdims=True)
    first = jnp.min(jnp.where(v == m, idx, n), axis=0, keepdims=True)
    return idx == first, m


def _route_kernel(sc_ref, bias_ref, tri_ref, rank_ref, w_ref, cnt_ref):
    E, NG = N_EXPERTS, N_EXPERT_GROUPS
    GS = E // NG
    sc = sc_ref[...]
    TM = sc.shape[1]
    sel = sc + bias_ref[...]
    i8 = lax.broadcasted_iota(jnp.int32, (GS, TM), 0)
    gscore = []
    for g in range(NG):
        blk = sel[g * GS:(g + 1) * GS, :]
        hit, m1 = _first_argmax_rows(blk, i8, GS)
        m2 = jnp.max(jnp.where(hit, -jnp.inf, blk), axis=0, keepdims=True)
        gscore.append(m1 + m2)
    gscore = jnp.concatenate(gscore, axis=0)
    ig = lax.broadcasted_iota(jnp.int32, (NG, TM), 0)
    gmask = jnp.zeros((NG, TM), F32)
    for _ in range(TOPK_EXPERT_GROUPS):
        hit, _m = _first_argmax_rows(gscore, ig, NG)
        gmask = jnp.where(hit, 1.0, gmask)
        gscore = jnp.where(hit, -jnp.inf, gscore)
    emask = jnp.concatenate([jnp.broadcast_to(gmask[g:g + 1, :], (GS, TM)) for g in range(NG)], axis=0)
    cand = jnp.where(emask > 0.5, sel, NEG_INF)
    ie = lax.broadcasted_iota(jnp.int32, (E, TM), 0)
    chosen = jnp.zeros((E, TM), F32)
    for _ in range(TOP_K):
        hit, _m = _first_argmax_rows(cand, ie, E)
        chosen = jnp.where(hit, 1.0, chosen)
        cand = jnp.where(hit, -jnp.inf, cand)
    w = chosen * sc
    w = w / jnp.sum(w, axis=0, keepdims=True) * ROUTED_SCALE
    sub = rank_ref.shape[-1]
    for q in range(TM // sub):
        cols = slice(q * sub, (q + 1) * sub)
        cb = chosen[:, cols].astype(BF16)
        prefix = _dot(cb, tri_ref[...])
        rank_ref[q] = jnp.where(chosen[:, cols] > 0.5, prefix, RANK_NONE)
        w_ref[q] = w[:, cols]
        cnt_ref[q] = _dot(cb, jnp.ones((sub, LANES), BF16))


def _route(scores_t, router_bias):
    B, E, S = scores_t.shape
    TM = min(ROUTE_TILE, S)
    per_step = min(MOE_OUTER, S) // TM
    nj = S // (TM * per_step)
    ns = B * nj * per_step
    tri = jnp.asarray(np.triu(np.ones((TM, TM), np.float32), k=1), BF16)
    tile = lambda w: pl.BlockSpec((per_step, E, w), lambda b, j: (b * nj + j, 0, 0))
    return pl.pallas_call(
        _route_kernel,
        grid=(B, nj),
        in_specs=[pl.BlockSpec((None, E, TM * per_step), lambda b, j: (b, 0, j)),
                  pl.BlockSpec((E, 1), lambda b, j: (0, 0)),
                  pl.BlockSpec((TM, TM), lambda b, j: (0, 0))],
        out_specs=[tile(TM), tile(TM), tile(LANES)],
        out_shape=[jax.ShapeDtypeStruct((ns, E, TM), F32), jax.ShapeDtypeStruct((ns, E, TM), F32),
                   jax.ShapeDtypeStruct((ns, E, LANES), F32)],
        compiler_params=_params(("arbitrary", "arbitrary")),
        name="route",
    )(scores_t, router_bias.reshape(E, 1), tri)


def _moe_kernel(cnt_ref, h_ref, x1_ref, mod_ref, rank_ref, w_ref, wg_ref, wu_ref, wd_ref,
                sg_ref, su_ref, sd_ref, o_ref, xc, yc, pc, wc, *, n_sub, tm, n_exp):
    to = pl.program_id(0)
    eb = pl.program_id(1)
    gate = mod_ref[5:6, :]

    @pl.when(eb == 0)
    def _shared():
        for s in range(n_sub):
            rows = pl.ds(s * tm, tm)
            hs = h_ref[rows, :]
            hid = jax.nn.silu(_dot(hs, sg_ref[...])) * _dot(hs, su_ref[...])
            o_ref[rows, :] = x1_ref[rows, :] + gate * _dot(hid.astype(BF16), sd_ref[...])
            xc[s] = jnp.zeros(xc.shape[1:], BF16)
            yc[s] = jnp.zeros(yc.shape[1:], BF16)
            wc[s] = jnp.zeros(wc.shape[1:], F32)

    PIECE, WIN = MOE_PIECE, MOE_WIN
    tn = (((0,), (0,)), ((), ()))
    experts = [eb * n_exp + j for j in range(n_exp)]
    cnt = [[cnt_ref[(to * n_sub + s) * N_EXPERTS + e] for e in experts] for s in range(n_sub)]
    off = []
    for s in range(n_sub):
        o = [jnp.int32(0)]
        for j in range(n_exp):
            o.append(o[-1] + ((cnt[s][j] + 15) // 16) * 16)
        off.append(o)
    n_pieces = [(off[s][n_exp] + PIECE - 1) // PIECE for s in range(n_sub)]
    fits = n_pieces[0] < MOE_ROWS // PIECE
    for s in range(1, n_sub):
        fits = jnp.logical_and(fits, n_pieces[s] < MOE_ROWS // PIECE)

    def one_hot_rows(s, base, rows_n, js):
        slot = (lax.broadcasted_iota(jnp.int32, (rows_n, tm), 0) + base).astype(F32)
        pick = jnp.zeros((rows_n, tm), F32)
        wacc = jnp.zeros((rows_n, tm), F32)
        for j in js:
            hit = (slot - off[s][j].astype(F32)) == rank_ref[s, pl.ds(experts[j], 1), :]
            pick = jnp.where(hit, 1.0, pick)
            wacc = jnp.where(hit, w_ref[s, pl.ds(experts[j], 1), :], wacc)
        return pick.astype(BF16), jnp.sum(wacc, axis=1, keepdims=True)

    def mlp(xg, j):
        hid = jax.nn.silu(_dot(xg, wg_ref[j])) * _dot(xg, wu_ref[j])
        return _dot(hid.astype(BF16), wd_ref[j])

    def aligned(q, n):
        return q * n if isinstance(q, int) else pl.multiple_of(q * n, n)

    def piece(s, q):
        r = pl.ds(aligned(q, PIECE), PIECE)
        pick, wrow = one_hot_rows(s, q * PIECE, PIECE, range(n_exp))
        xc[s, r, :] = _dot(pick, h_ref[pl.ds(s * tm, tm), :]).astype(BF16)
        pc[s, :, r] = pick.T
        wc[s, r, :] = wrow

    def window(j, p, first):
        starts = [pl.multiple_of(jnp.minimum(off[s][j] + p * WIN, (n_pieces[s] + 1) * PIECE - WIN), 16)
                  for s in range(n_sub)]
        xg = jnp.concatenate([xc[s, pl.ds(starts[s], WIN), :] for s in range(n_sub)], axis=0)
        out = mlp(xg, j)
        for s in range(n_sub):
            r = pl.ds(starts[s], WIN)
            new = (out[s * WIN:(s + 1) * WIN] * wc[s, r, :] * gate).astype(BF16)
            if not first:
                rank = starts[s] - off[s][j] + lax.broadcasted_iota(jnp.int32, (WIN, 1), 0)
                new = jnp.where(rank < cnt[s][j], new, yc[s, r, :])
            yc[s, r, :] = new

    def scatter(s, start, n):
        r = pl.ds(start, n)
        o_ref[pl.ds(s * tm, tm), :] += _dot(pc[s, :, r], yc[s, r, :])

    @pl.when(fits)
    def _packed():
        for s in range(n_sub):
            piece(s, 0)
            piece(s, 1)
        for s in range(n_sub):
            @pl.when(n_pieces[s] > 2)
            def _third(s=s):
                piece(s, 2)
        for j in range(n_exp):
            window(j, 0, True)
        for j in range(n_exp):
            most = cnt[0][j]
            for s in range(1, n_sub):
                most = jnp.maximum(most, cnt[s][j])

            @pl.when(most > WIN)
            def _more(j=j, most=most):
                lax.fori_loop(1, (most + WIN - 1) // WIN, lambda p, c: (window(j, p, False), c)[1], 0)
        for s in range(n_sub):
            scatter(s, 0, 2 * PIECE)
        for s in range(n_sub):
            @pl.when(n_pieces[s] > 2)
            def _third_out(s=s):
                scatter(s, 2 * PIECE, PIECE)

    @pl.when(jnp.logical_not(fits))
    def _unpacked():
        for j in range(n_exp):
            for s in range(n_sub):
                rows = pl.ds(s * tm, tm)

                def block(bi, carry, j=j, s=s, rows=rows):
                    slot0 = off[s][j] + bi * PIECE
                    pick, wrow = one_hot_rows(s, slot0, PIECE, [j])
                    out = mlp(_dot(pick, h_ref[rows, :]).astype(BF16), j)
                    ow = (out * wrow * gate).astype(BF16)
                    o_ref[rows, :] += lax.dot_general(pick, ow, tn, preferred_element_type=F32)
                    return carry

                lax.fori_loop(0, (cnt[s][j] + PIECE - 1) // PIECE, block, 0)


def _moe(h2, x1, mod, rank_t, w_t, counts, w_gate, w_up, w_down, ws_gate, ws_up, ws_down, seq):
    T, D = h2.shape
    ns, E, tm = rank_t.shape
    tmo = min(MOE_OUTER, seq)
    n_sub = tmo // tm
    n_exp = MOE_EXPERTS_PER_STEP
    per_b = seq // tmo
    ws = [w_gate.astype(BF16), w_up.astype(BF16), w_down.astype(BF16)]
    sh = [ws_gate.astype(BF16), ws_up.astype(BF16), ws_down.astype(BF16)]
    tok = lambda: pl.BlockSpec((tmo, D), lambda t, e, c: (t, 0))
    sub = lambda: pl.BlockSpec((n_sub, E, tm), lambda t, e, c: (t, 0, 0))
    exp = lambda a: pl.BlockSpec((n_exp,) + a.shape[1:], lambda t, e, c: (e, 0, 0))
    full = lambda a: pl.BlockSpec(a.shape, lambda t, e, c: (0, 0))
    grid_spec = pltpu.PrefetchScalarGridSpec(
        num_scalar_prefetch=1,
        grid=(T // tmo, E // n_exp),
        in_specs=[tok(), tok(), pl.BlockSpec((None, N_MOD, D), lambda t, e, c: (t // per_b, 0, 0)),
                  sub(), sub(), exp(ws[0]), exp(ws[1]), exp(ws[2]), full(sh[0]), full(sh[1]), full(sh[2])],
        out_specs=tok(),
        scratch_shapes=[pltpu.VMEM((n_sub, MOE_ROWS, D), BF16), pltpu.VMEM((n_sub, MOE_ROWS, D), BF16),
                        pltpu.VMEM((n_sub, tm, MOE_ROWS), BF16), pltpu.VMEM((n_sub, MOE_ROWS, 1), F32)],
    )
    return pl.pallas_call(
        functools.partial(_moe_kernel, n_sub=n_sub, tm=tm, n_exp=n_exp),
        grid_spec=grid_spec,
        out_shape=jax.ShapeDtypeStruct((T, D), F32),
        compiler_params=_params(("arbitrary", "arbitrary")),
        name="moe",
    )(counts, h2, x1, mod, rank_t, w_t, *ws, *sh)


def kernel(x, c, w_ada, b_ada, norm_mix_gain, norm_ffn_gain, w_in, s5_lambda_re, s5_lambda_im, s5_log_dt, s5_b_re, s5_b_im, s5_c_re, s5_c_im, s5_d, s5_w_glu, s5_b_glu, q_norm_gain, k_norm_gain, cmp_pe, cmp_w1, cmp_b1, cmp_w2, cmp_b2, w_branch_a, w_branch_b, w_out, w_router, router_bias, w_gate, w_up, w_down, ws_gate, ws_up, ws_down):
    B, S, D = x.shape
    for l in range(w_ada.shape[0]):
        mod = _ada(c, w_ada[l], b_ada[l]).reshape(B, N_MOD, D)
        u, kvc, qt, ks, vst, kwn, vwt, gt, gm = _inproj(x, mod, norm_mix_gain[l:l + 1], w_in[l],
                                                        q_norm_gain[l], k_norm_gain[l])
        tables = _s5_tables(s5_lambda_re[l], s5_lambda_im[l], s5_log_dt[l], s5_b_re[l], s5_b_im[l],
                            s5_c_re[l], s5_c_im[l], s5_d[l])
        ys5 = _s5(u, tables)
        kvc_c = _compress(kvc, cmp_pe[l], cmp_w1[l], cmp_b1[l], cmp_w2[l], cmp_b2[l], k_norm_gain[l, 0])
        o_nsa = _attend(qt, ks, vst, kwn, vwt, gt, kvc_c)
        x1, h2, scores_t = _merge(x, mod, ys5, o_nsa, gm, s5_w_glu[l], s5_b_glu[l], w_branch_a[l],
                                  w_branch_b[l], w_out[l], norm_ffn_gain[l:l + 1], w_router[l])
        rank_t, w_t, cnt = _route(scores_t, router_bias[l])
        counts = cnt[:, :, 0].astype(jnp.int32).reshape(-1)
        x = _moe(h2.reshape(B * S, D), x1.reshape(B * S, D), mod, rank_t, w_t, counts,
                 w_gate[l], w_up[l], w_down[l], ws_gate[l], ws_up[l], ws_down[l], S).reshape(B, S, D)
    return x
```

```python
import functools

import numpy as np
import jax
import jax.numpy as jnp
from jax import lax
from jax.experimental import pallas as pl
from jax.experimental.pallas import tpu as pltpu

F32 = jnp.float32
BF16 = jnp.bfloat16

D_MODEL = 1024
S5_WIDTH = 512
S5_GROUP = 16
S5_GROUPS = S5_WIDTH // S5_GROUP
S5_STATE = 64
N_HEADS = 8
N_KV_HEADS = 2
GQA_GROUP = N_HEADS // N_KV_HEADS
HEAD_DIM = 64
NSA_WIDTH = N_HEADS * HEAD_DIM
KV_WIDTH = 2 * N_KV_HEADS * HEAD_DIM
CMP_BLOCK = 32
CMP_STRIDE = 16
CMP_HIDDEN = 256
SEL_BLOCK = 64
SEL_TOPK = 8
WINDOW = 256
Q_BLOCK = 128
FORCE_BONUS = 1e3
N_EXPERTS = 64
TOP_K = 8
N_EXPERT_GROUPS = 8
TOPK_EXPERT_GROUPS = 4
ROUTED_SCALE = 2.5
RMS_EPS = 1e-6
NEG_INF = -1e30
N_MOD = 6

LANES = 128
SUBLANES = 8
S5_CHUNK = 16
S5_CW = S5_CHUNK * S5_GROUP
ROW_TILE = 512
ATT_TILES = 2
SEL_KEY_CHUNK = 512
SEL_UNIT = 128
SEL_SLOTS = (5, 8)
POS_BASE = 64
POS_ROWS = 16
AUG = 128
ROUTE_TILE = 256
MOE_OUTER = 1024
MOE_EXPERTS_PER_STEP = 4
MOE_PIECE = 128
MOE_WIN = 48
MOE_ROWS = 512
RANK_NONE = -float(1 << 20)
VMEM_LIMIT = 56 * 1024 * 1024


def _dot(a, b):
    return jnp.dot(a, b, preferred_element_type=F32)


def _split(a):
    hi = a.astype(BF16)
    lo = (a - hi.astype(F32)).astype(BF16)
    return hi, lo


def _dot3(a, bh, bl):
    ah, al = _split(a)
    return _dot(ah, bh) + (_dot(al, bh) + _dot(ah, bl))


def _segment_transpose(x):
    assert x.shape[-2:] == (SUBLANES, LANES) and LANES // S5_GROUP == SUBLANES
    nd = x.ndim
    i = lax.broadcasted_iota(jnp.int32, x.shape, nd - 2)
    seg = lax.broadcasted_iota(jnp.int32, x.shape, nd - 1) // S5_GROUP
    out = x
    for d in range(1, SUBLANES):
        r = pltpu.roll(pltpu.roll(x, SUBLANES - d, axis=nd - 2), S5_GROUP * d, axis=nd - 1)
        out = jnp.where(seg == ((i + d) & (SUBLANES - 1)), r, out)
    return out


def _to_chunk_major(u, put):
    rows = u.shape[0]
    u3 = u.reshape(rows // SUBLANES, SUBLANES, S5_WIDTH)
    halves = S5_CHUNK // SUBLANES
    for jb in range(S5_WIDTH // LANES):
        t = _segment_transpose(u3[:, :, LANES * jb:LANES * (jb + 1)])
        t = t.reshape(rows // S5_CHUNK, halves, SUBLANES, LANES)
        for hf in range(halves):
            put(slice(SUBLANES * jb, SUBLANES * (jb + 1)), slice(LANES * hf, LANES * (hf + 1)), t[:, hf])


def _from_chunk_major(get, rows):
    halves = S5_CHUNK // SUBLANES
    cols = []
    for jb in range(S5_WIDTH // LANES):
        parts = [_segment_transpose(get(slice(SUBLANES * jb, SUBLANES * (jb + 1)), slice(LANES * hf, LANES * (hf + 1))))
                 for hf in range(halves)]
        cols.append(jnp.stack(parts, axis=1).reshape(rows, LANES))
    return jnp.concatenate(cols, axis=1)


def _params(sem):
    return pltpu.CompilerParams(dimension_semantics=sem, vmem_limit_bytes=VMEM_LIMIT)


def _ada_kernel(c_ref, w_ref, b_ref, o_ref):
    cs = jax.nn.silu(c_ref[...])
    wh, wl = _split(w_ref[...])
    o_ref[...] = _dot3(cs, wh, wl) + b_ref[...]


def _ada(c, w_ada, b_ada):
    B, D = c.shape
    return pl.pallas_call(
        _ada_kernel,
        grid=(N_MOD,),
        in_specs=[pl.BlockSpec((B, D), lambda j: (0, 0)),
                  pl.BlockSpec((D, D), lambda j: (0, j)),
                  pl.BlockSpec((1, D), lambda j: (0, j))],
        out_specs=pl.BlockSpec((B, D), lambda j: (0, j)),
        out_shape=jax.ShapeDtypeStruct((B, N_MOD * D), F32),
        compiler_params=_params(("arbitrary",)),
        name="ada",
    )(c, w_ada, b_ada.reshape(1, N_MOD * D))


def _head_norm(v, bd, gain):
    sq = v * v
    sh, sl = _split(sq)
    ms = _dot(sh, bd) + _dot(sl, bd)
    return v * lax.rsqrt(ms + RMS_EPS) * gain


def _inproj_kernel(x_ref, mod_ref, gain_ref, wm_ref, wt_ref, wgn_ref, wgm_ref, bd_ref, qg_ref, kg_ref,
                   u_ref, kvc_ref, qt_ref, ksa_ref, vst_ref, kwa_ref, vwt_ref, gt_ref, gm_ref):
    x = x_ref[...]
    shift = mod_ref[0:1, :]
    scale = mod_ref[1:2, :]
    y = x * lax.rsqrt(jnp.mean(x * x, axis=-1, keepdims=True) + RMS_EPS)
    h = (y * gain_ref[...]) * (1.0 + scale) + shift
    hb = h.astype(BF16)
    main = _dot(hb, wm_ref[...])
    tm = x.shape[0]

    def put(groups, lanes, block):
        u_ref[:, groups, lanes] = block

    _to_chunk_major(main[:, :S5_WIDTH], put)
    o = S5_WIDTH
    kvc_ref[...] = main[:, o:o + KV_WIDTH]
    o += KV_WIDTH
    kw = N_KV_HEADS * HEAD_DIM
    bd = bd_ref[...]
    ks = _head_norm(main[:, o:o + kw], bd[:kw, :kw], kg_ref[1:2, :]).astype(BF16)
    kwn = _head_norm(main[:, o + kw:o + 2 * kw], bd[:kw, :kw], kg_ref[2:3, :]).astype(BF16)
    pos = pl.program_id(1) * tm + lax.broadcasted_iota(jnp.int32, (tm, 1), 0)
    lane = lax.broadcasted_iota(jnp.int32, (tm, AUG), 1)
    digits = jnp.where(lane == 0, pos // POS_BASE, jnp.where(lane == 1, pos % POS_BASE, 0))
    digits = digits[:, :AUG - HEAD_DIM].astype(F32).astype(BF16)
    onehot = jnp.where(lane == pos // SEL_BLOCK, 1.0, 0.0).astype(BF16)
    for hh in range(N_KV_HEADS):
        ksa_ref[hh, :, 0:HEAD_DIM] = ks[:, hh * HEAD_DIM:(hh + 1) * HEAD_DIM]
        ksa_ref[hh, :, HEAD_DIM:AUG] = digits
        ksa_ref[hh, :, AUG:] = onehot
        kwa_ref[hh, :, 0:HEAD_DIM] = kwn[:, hh * HEAD_DIM:(hh + 1) * HEAD_DIM]
        kwa_ref[hh, :, HEAD_DIM:] = digits

    nt = (((1,), (1,)), ((), ()))
    tt = lax.dot_general(wt_ref[...], hb, nt, preferred_element_type=F32)
    qt = tt[:NSA_WIDTH]
    sq = qt * qt
    sh, sl = _split(sq)
    qn = (qt * lax.rsqrt(_dot(bd, sh) + _dot(bd, sl) + RMS_EPS) * qg_ref[...] * (HEAD_DIM ** -0.5)).astype(BF16)
    QW = GQA_GROUP * Q_BLOCK
    for qb in range(tm // Q_BLOCK):
        for hd in range(N_HEADS):
            hh, g = divmod(hd, GQA_GROUP)
            qt_ref[hh, :, qb * QW + g * Q_BLOCK:qb * QW + (g + 1) * Q_BLOCK] = (
                qn[hd * HEAD_DIM:(hd + 1) * HEAD_DIM, qb * Q_BLOCK:(qb + 1) * Q_BLOCK])
    vst_ref[...] = tt[NSA_WIDTH:NSA_WIDTH + kw].reshape(N_KV_HEADS, HEAD_DIM, tm).astype(BF16)
    vwt_ref[...] = tt[NSA_WIDTH + kw:].reshape(N_KV_HEADS, HEAD_DIM, tm).astype(BF16)
    gn = jax.nn.sigmoid(lax.dot_general(wgn_ref[...], hb, nt, preferred_element_type=F32))
    for qb in range(tm // Q_BLOCK):
        for br in range(3):
            for hd in range(N_HEADS):
                hh, g = divmod(hd, GQA_GROUP)
                r = br * N_HEADS + hd
                gt_ref[hh, br:br + 1, qb * QW + g * Q_BLOCK:qb * QW + (g + 1) * Q_BLOCK] = (
                    gn[r:r + 1, qb * Q_BLOCK:(qb + 1) * Q_BLOCK])
    gm_ref[...] = jax.nn.sigmoid(_dot(hb, wgm_ref[...])).astype(BF16)


def _s5_chunk_spec(tm):
    assert S5_CHUNK == 16 and S5_GROUP == 16, "the segment transposes assume 16 steps x 16 channels"
    return pl.BlockSpec((tm // S5_CHUNK, None, S5_GROUPS, S5_CW), lambda b, i: (i, b, 0, 0))


def _inproj(x, mod, gain, w_in, q_gain, k_gain):
    B, S, D = x.shape
    tm = min(ROW_TILE, S)
    assert S // SEL_BLOCK <= AUG and tm % Q_BLOCK == 0
    kw = N_KV_HEADS * HEAD_DIM
    cols = np.cumsum((0,) + (S5_WIDTH, NSA_WIDTH, KV_WIDTH, KV_WIDTH, KV_WIDTH, 3 * N_HEADS, 2 * D))
    c_u, c_q, c_kvc, c_kvs, c_kvw, c_gn, c_gm = cols[:7]
    sl = lambda a, n: w_in[:, a:a + n]
    wm = jnp.concatenate([sl(c_u, S5_WIDTH), sl(c_kvc, KV_WIDTH), sl(c_kvs, kw), sl(c_kvw, kw)], axis=1).astype(BF16)
    wt = jnp.concatenate([sl(c_q, NSA_WIDTH), sl(c_kvs + kw, kw), sl(c_kvw + kw, kw)], axis=1).T.astype(BF16)
    wgn = sl(c_gn, 3 * N_HEADS).T.astype(BF16)
    wgm = sl(c_gm, 2 * D).astype(BF16)
    seg = np.arange(NSA_WIDTH) // HEAD_DIM
    bd = jnp.asarray((seg[:, None] == seg[None, :]).astype(np.float32) / HEAD_DIM, BF16)
    qg = jnp.tile(q_gain, N_HEADS).reshape(NSA_WIDTH, 1)
    kg = jnp.tile(k_gain, (1, N_KV_HEADS))
    nq = tm // Q_BLOCK * GQA_GROUP * Q_BLOCK
    row = lambda w: pl.BlockSpec((None, tm, w), lambda b, i: (b, i, 0))
    full = lambda a: pl.BlockSpec(a.shape, lambda b, i: (0,) * a.ndim)
    rows4 = lambda w: pl.BlockSpec((None, N_KV_HEADS, tm, w), lambda b, i: (b, 0, i, 0))
    cols4 = lambda r, w: pl.BlockSpec((None, N_KV_HEADS, r, w), lambda b, i: (b, 0, 0, i))
    nqt = S // Q_BLOCK * GQA_GROUP * Q_BLOCK
    return pl.pallas_call(
        _inproj_kernel,
        grid=(B, S // tm),
        in_specs=[row(D), pl.BlockSpec((None, N_MOD, D), lambda b, i: (b, 0, 0)),
                  full(gain), full(wm), full(wt), full(wgn), full(wgm), full(bd), full(qg), full(kg)],
        out_specs=[_s5_chunk_spec(tm), row(KV_WIDTH), cols4(HEAD_DIM, nq), rows4(2 * AUG), cols4(HEAD_DIM, tm),
                   rows4(AUG), cols4(HEAD_DIM, tm), cols4(3, nq), row(2 * D)],
        out_shape=[jax.ShapeDtypeStruct((S // S5_CHUNK, B, S5_GROUPS, S5_CW), F32),
                   jax.ShapeDtypeStruct((B, S, KV_WIDTH), F32),
                   jax.ShapeDtypeStruct((B, N_KV_HEADS, HEAD_DIM, nqt), BF16),
                   jax.ShapeDtypeStruct((B, N_KV_HEADS, S, 2 * AUG), BF16),
                   jax.ShapeDtypeStruct((B, N_KV_HEADS, HEAD_DIM, S), BF16),
                   jax.ShapeDtypeStruct((B, N_KV_HEADS, S, AUG), BF16),
                   jax.ShapeDtypeStruct((B, N_KV_HEADS, HEAD_DIM, S), BF16),
                   jax.ShapeDtypeStruct((B, N_KV_HEADS, 3, nqt), F32),
                   jax.ShapeDtypeStruct((B, S, 2 * D), BF16)],
        compiler_params=_params(("arbitrary", "arbitrary")),
        name="inproj",
    )(x, mod, gain, wm, wt, wgn, wgm, bd, qg, kg)


def _s5_tables(lam_re, lam_im, log_dt, b_re, b_im, c_re, c_im, d_skip):
    L, P, N, G = S5_CHUNK, S5_GROUP, S5_STATE, S5_GROUPS
    hp = lax.Precision.HIGHEST
    lr, li = lam_re.astype(F32), lam_im.astype(F32)
    dt = jnp.exp(log_dt.astype(F32))[:, None]
    mag = jnp.exp(lr * dt)
    abar_re, abar_im = mag * jnp.cos(li * dt), mag * jnp.sin(li * dt)
    num_re, num_im = abar_re - 1.0, abar_im
    den = lr * lr + li * li
    coef_re = (num_re * lr + num_im * li) / den
    coef_im = (num_im * lr - num_re * li) / den
    br, bi = b_re.astype(F32), b_im.astype(F32)
    bbar_re = coef_re[..., None] * br - coef_im[..., None] * bi
    bbar_im = coef_re[..., None] * bi + coef_im[..., None] * br
    k = jnp.arange(L + 1, dtype=F32)[:, None, None]
    pmag = jnp.exp(lr * dt * k)
    pre, pim = pmag * jnp.cos(li * dt * k), pmag * jnp.sin(li * dt * k)
    cr, ci = c_re.astype(F32), c_im.astype(F32)
    ca_re = cr[None] * pre[:, :, None, :] - ci[None] * pim[:, :, None, :]
    ca_im = cr[None] * pim[:, :, None, :] + ci[None] * pre[:, :, None, :]
    kern = (jnp.einsum('kgpn,gnq->gkqp', ca_re[:L], bbar_re, precision=hp)
            - jnp.einsum('kgpn,gnq->gkqp', ca_im[:L], bbar_im, precision=hp))
    s_i = np.arange(L)[:, None]
    t_i = np.arange(L)[None, :]
    tau = np.clip(t_i - s_i, 0, L - 1)
    causal = jnp.asarray((t_i >= s_i).astype(np.float32))
    mt = kern[:, tau] * causal[None, :, :, None, None]
    mt = mt.transpose(0, 1, 3, 2, 4).reshape(G, L * P, L * P)
    rev = np.arange(L - 1, -1, -1)
    ab_re = pre[rev][..., None] * bbar_re[None] - pim[rev][..., None] * bbar_im[None]
    ab_im = pre[rev][..., None] * bbar_im[None] + pim[rev][..., None] * bbar_re[None]
    ws = jnp.concatenate([ab_re, ab_im], axis=2)
    ws = ws.transpose(1, 0, 3, 2).reshape(G, L * P, 2 * N)
    wo = jnp.concatenate([ca_re[1:], -ca_im[1:]], axis=3)
    wo = wo.transpose(1, 3, 0, 2).reshape(G, 2 * N, L * P)
    al = jnp.stack([pre[L], pim[L]], axis=1)
    dv = jnp.tile(d_skip.astype(F32), (1, L)).reshape(G, 1, L * P)
    return mt, ws, wo, al, dv


def _s5_kernel(u_ref, mth_ref, mtl_ref, wsh_ref, wsl_ref, woh_ref, wol_ref, al_ref, dv_ref,
               y_ref, vr_scr, vi_scr, xr_scr, xi_scr, *, n_chunks, bsz):
    N = S5_STATE
    u = u_ref[...]
    uh, ul = _split(u)
    y = _dot(uh, mth_ref[...]) + (_dot(ul, mth_ref[...]) + _dot(uh, mtl_ref[...]))
    v = _dot(uh, wsh_ref[...]) + (_dot(ul, wsh_ref[...]) + _dot(uh, wsl_ref[...]))
    vr_scr[...] = v[:, :N]
    vi_scr[...] = v[:, N:]
    a_r = al_ref[0:1, :]
    a_i = al_ref[1:2, :]

    def step(c, x):
        xr, xi = x
        r = pl.ds(pl.multiple_of(c * bsz, bsz), bsz)
        xr_scr[r, :] = xr
        xi_scr[r, :] = xi
        return (a_r * xr - a_i * xi + vr_scr[r, :], a_r * xi + a_i * xr + vi_scr[r, :])

    zero = jnp.zeros((bsz, N), F32)
    lax.fori_loop(0, n_chunks, step, (zero, zero), unroll=8)
    xp = jnp.concatenate([xr_scr[...], xi_scr[...]], axis=1)
    y = y + _dot3(xp, woh_ref[...], wol_ref[...])
    y_ref[...] = y + dv_ref[...] * u


def _s5(ug, tables):
    nc, B, G, cw = ug.shape
    N = S5_STATE
    mt, ws, wo, al, dv = tables
    mth, mtl = _split(mt)
    wsh, wsl = _split(ws)
    woh, wol = _split(wo)
    grp = lambda a: pl.BlockSpec((None,) + a.shape[1:], lambda g: (g,) + (0,) * (a.ndim - 1))
    ugt = ug.reshape(nc * B, G, cw).transpose(1, 0, 2)
    y = pl.pallas_call(
        functools.partial(_s5_kernel, n_chunks=nc, bsz=B),
        grid=(G,),
        in_specs=[grp(ugt), grp(mth), grp(mtl), grp(wsh), grp(wsl), grp(woh), grp(wol), grp(al), grp(dv)],
        out_specs=grp(ugt),
        out_shape=jax.ShapeDtypeStruct(ugt.shape, F32),
        scratch_shapes=[pltpu.VMEM((nc * B, N), F32)] * 4,
        compiler_params=_params(("arbitrary",)),
        name="s5",
    )(ugt, mth, mtl, wsh, wsl, woh, wol, al, dv)
    return y.transpose(1, 0, 2).reshape(nc, B, G, cw)


def _compress_kernel(x_ref, pe_ref, w1_ref, b1_ref, w2_ref, b2_ref, kg_ref, o_ref, *, n_rows):
    half = CMP_STRIDE * HEAD_DIM
    per_token = KV_WIDTH // LANES
    for j in range(2 * N_KV_HEADS):
        kv = j // N_KV_HEADS
        lanes = slice(j * HEAD_DIM % LANES, j * HEAD_DIM % LANES + HEAD_DIM)
        x = jnp.concatenate(
            [x_ref[pl.ds(t * per_token + j * HEAD_DIM // LANES, n_rows, stride=CMP_STRIDE * per_token), :][:, lanes]
             for t in range(CMP_STRIDE)], axis=1)
        a = _dot((x + pe_ref[kv, 0:1, :]).astype(BF16), w1_ref[kv, :half, :])
        b = _dot((x + pe_ref[kv, 1:2, :]).astype(BF16), w1_ref[kv, half:, :])
        hid = jax.nn.gelu(a + pltpu.roll(b, n_rows - 1, axis=0) + b1_ref[kv])
        out = _dot(hid.astype(BF16), w2_ref[kv]) + b2_ref[kv]
        if kv == 0:
            out = out * lax.rsqrt(jnp.mean(out * out, axis=-1, keepdims=True) + RMS_EPS) * kg_ref[...]
        o_ref[j] = out.astype(BF16)


def _compress(kv_c, cmp_pe, cmp_w1, cmp_b1, cmp_w2, cmp_b2, k_gain0):
    B, S, _ = kv_c.shape
    nr = S // CMP_STRIDE
    half = CMP_STRIDE * HEAD_DIM
    nj = 2 * N_KV_HEADS
    ws = [cmp_pe.reshape(2, 2, half), cmp_w1.astype(BF16), cmp_b1.reshape(2, 1, CMP_HIDDEN), cmp_w2.astype(BF16),
          cmp_b2.reshape(2, 1, HEAD_DIM), k_gain0.reshape(1, HEAD_DIM)]
    full = lambda a: pl.BlockSpec(a.shape, lambda b: (0,) * a.ndim)
    return pl.pallas_call(
        functools.partial(_compress_kernel, n_rows=nr),
        grid=(B,),
        in_specs=[pl.BlockSpec((None, S * KV_WIDTH // LANES, LANES), lambda b: (b, 0, 0))] + [full(a) for a in ws],
        out_specs=pl.BlockSpec((None, nj, nr, HEAD_DIM), lambda b: (b, 0, 0, 0)),
        out_shape=jax.ShapeDtypeStruct((B, nj, nr, HEAD_DIM), BF16),
        compiler_params=_params(("arbitrary",)),
        name="compress",
    )(kv_c.reshape(B, S * KV_WIDTH // LANES, LANES), *ws)


def _softmax_cols(s, mask):
    s = jnp.where(mask, s, NEG_INF)
    m = jnp.max(s, axis=0, keepdims=True)
    p = jnp.where(mask, jnp.exp(s - m), 0.0)
    return p / jnp.maximum(jnp.sum(p, axis=0, keepdims=True), 1e-20)


def _attend_kernel(qt_ref, kc_ref, vct_ref, ks_ref, vst_ref, kw_ref, vwt_ref, g_ref, slope_ref, ovt_ref,
                   o_ref, qa_scr, sa_scr, sb_scr, *, n_sel, n_pick, n_cmp_rows):
    QW = GQA_GROUP * Q_BLOCK
    chains = [(k, h) for k in range(ATT_TILES) for h in range(N_KV_HEADS)]
    t0s = [(pl.program_id(1) * ATT_TILES + k) * Q_BLOCK for k in range(ATT_TILES)]
    tqs = [t0 + (lax.broadcasted_iota(jnp.int32, (1, QW), 1) & (Q_BLOCK - 1)) for t0 in t0s]
    cols = lambda k: slice(k * QW, (k + 1) * QW)

    r = lax.broadcasted_iota(jnp.int32, (POS_ROWS, QW), 0)
    qa = {}
    for k, h in chains:
        slope = slope_ref[h]
        qa_scr[k, h, 0:HEAD_DIM, :] = qt_ref[h, :, cols(k)]
        qa_scr[k, h, HEAD_DIM:HEAD_DIM + POS_ROWS, :] = jnp.where(
            r == 0, slope * POS_BASE, jnp.where(r == 1, slope, 0.0)).astype(BF16)
        qa_scr[k, h, HEAD_DIM + POS_ROWS:AUG, :] = jnp.zeros((AUG - HEAD_DIM - POS_ROWS, QW), BF16)
        qa[k, h] = qa_scr[k, h, 0:AUG, :]

    def compressed(k, h, rows):
        sc = _dot(kc_ref[h, 0:rows, :], qa[k, h])
        cpos = lax.broadcasted_iota(jnp.int32, (rows, QW), 0) * CMP_STRIDE + (CMP_BLOCK - 1)
        p_c = _softmax_cols(sc, cpos <= tqs[k])
        o_c = _dot(vct_ref[h, :, 0:rows], p_c.astype(BF16))
        psum = p_c[:, 0:Q_BLOCK]
        for g in range(1, GQA_GROUP):
            psum = psum + p_c[:, g * Q_BLOCK:(g + 1) * Q_BLOCK]
        ph, pl_ = _split(psum)
        return o_c, _dot(ovt_ref[:, 0:rows], ph) + _dot(ovt_ref[:, 0:rows], pl_)

    half = n_cmp_rows // 2
    if half % LANES == 0:
        early = (t0s[-1] + Q_BLOCK - CMP_BLOCK) // CMP_STRIDE < half
        cmp_out = lax.cond(early, lambda _: [compressed(k, h, half) for k, h in chains],
                           lambda _: [compressed(k, h, n_cmp_rows) for k, h in chains], 0)
    else:
        cmp_out = [compressed(k, h, n_cmp_rows) for k, h in chains]
    cmp_out = dict(zip(chains, cmp_out))

    def select(k, h, imp):
        jb = lax.broadcasted_iota(jnp.int32, (n_sel, Q_BLOCK), 0)
        cur = (t0s[k] + lax.broadcasted_iota(jnp.int32, (1, Q_BLOCK), 1)) // SEL_BLOCK
        forced = (jb == 0) | (jb == cur) | (jb == cur - 1)
        imp = jnp.where(forced, imp + FORCE_BONUS, imp)
        imp = jnp.where(jb <= cur, imp, -1.0)
        bias = jnp.full((n_sel, Q_BLOCK), NEG_INF, F32)
        for _ in range(n_pick):
            m = jnp.max(imp, axis=0, keepdims=True)
            first = jnp.min(jnp.where(imp == m, jb, n_sel), axis=0, keepdims=True)
            hit = jb == first
            bias = jnp.where(hit, 0.0, bias)
            imp = jnp.where(hit, -jnp.inf, imp)
        per_unit = SEL_UNIT // SEL_BLOCK
        n_units = n_sel // per_unit
        picked = jnp.where(bias == 0.0, 1.0, 0.0).astype(BF16)
        in_unit = (lax.broadcasted_iota(jnp.int32, (n_units, n_sel), 1) // per_unit
                   == lax.broadcasted_iota(jnp.int32, (n_units, n_sel), 0))
        hits = _dot(jnp.where(in_unit, 1.0, 0.0).astype(BF16), picked)
        live = jnp.where(jnp.max(hits, axis=1, keepdims=True) > 0.5, 1.0, 0.0)
        live = jnp.broadcast_to(live, (n_units, LANES))
        earlier = (lax.broadcasted_iota(jnp.int32, (n_units, n_units), 1)
                   < lax.broadcasted_iota(jnp.int32, (n_units, n_units), 0))
        before = _dot(jnp.where(earlier, 1.0, 0.0).astype(BF16), live.astype(BF16))
        slot = lax.broadcasted_iota(jnp.int32, (n_units, LANES), 1).astype(F32)
        in_slot = jnp.where(before == slot, live, 0.0).astype(BF16)
        r8 = lax.broadcasted_iota(jnp.int32, (SUBLANES, n_units), 0)
        c8 = lax.broadcasted_iota(jnp.int32, (SUBLANES, n_units), 1)
        pick_rows = jnp.where(r8 == 0, c8, jnp.where(r8 == 1, 1, 0)).astype(F32).astype(BF16)
        ids = _dot(pick_rows, in_slot)
        n_live = (before[n_units - 1:n_units, 0:1] + live[n_units - 1:n_units, 0:1])[0, 0]
        bias = jnp.concatenate([bias] * GQA_GROUP, axis=1).astype(BF16)
        qa_scr[k, h, AUG:, :] = jnp.concatenate([bias, jnp.zeros((AUG - n_sel, QW), BF16)], axis=0)
        return ids, n_live

    def window(k, h):
        WK = WINDOW + Q_BLOCK
        w0 = pl.multiple_of(jnp.maximum(t0s[k] - WINDOW, 0), Q_BLOCK)
        sw = _dot(kw_ref[h, pl.ds(w0, WK), :], qa[k, h])
        dist = tqs[k] - (w0 + lax.broadcasted_iota(jnp.int32, (WK, QW), 0))
        in_win = (dist | (WINDOW - 1 - dist)) >= 0
        p_w = _softmax_cols(sw, in_win)
        return _dot(vwt_ref[h, :, pl.ds(w0, WK)], p_w.astype(BF16))

    lists = {c: select(*c, cmp_out[c][1]) for c in chains}
    o_win = {c: window(*c) for c in chains}

    def gathered(k, h, ids, n_slots):
        units = [ids[0, s].astype(jnp.int32) for s in range(n_slots)]
        starts = [pl.multiple_of(u * SEL_UNIT, SEL_UNIT) for u in units]
        keys = jnp.concatenate([ks_ref[h, pl.ds(st, SEL_UNIT), :] for st in starts], axis=0)
        s = _dot(keys, qa_scr[k, h])
        krow = lax.broadcasted_iota(jnp.int32, (SEL_UNIT, QW), 0)
        seen = jnp.concatenate(
            [st + krow <= jnp.where(ids[1, si] > 0.5, tqs[k], -1) for si, st in enumerate(starts)], axis=0)
        s = jnp.where(seen, s, NEG_INF)
        p = jnp.exp(s - jnp.max(s, axis=0, keepdims=True))
        vals = jnp.concatenate([vst_ref[h, :, pl.ds(st, SEL_UNIT)] for st in starts], axis=1)
        return _dot(vals, p.astype(BF16)) / jnp.maximum(jnp.sum(p, axis=0, keepdims=True), 1e-20)

    KC = SEL_KEY_CHUNK

    def swept(k, h):
        last = (t0s[k] + Q_BLOCK + KC - 1) // KC - 1

        def scores_to(ref, j):
            k0 = pl.multiple_of(j * KC, KC)
            ref[...] = _dot(ks_ref[h, pl.ds(k0, KC), :], qa_scr[k, h])

        def accumulate(carry, j, s, p_of):
            m, l, acc = carry
            k0 = pl.multiple_of(j * KC, KC)
            m_new = jnp.maximum(m, jnp.max(s, axis=0, keepdims=True))
            alpha = jnp.exp(m - m_new)
            p = p_of(jnp.exp(s - m_new))
            l = alpha * l + jnp.sum(p, axis=0, keepdims=True)
            acc = alpha * acc + _dot(vst_ref[h, :, pl.ds(k0, KC)], p.astype(BF16))
            return m_new, l, acc

        keep = lambda p: p

        def two_chunks(i, state):
            j = 2 * i
            scores_to(sb_scr, j + 1)
            state = accumulate(state, j, sa_scr[...], keep)
            scores_to(sa_scr, j + 2)
            return accumulate(state, j + 1, sb_scr[...], keep)

        def odd_chunk(state):
            state = accumulate(state, last - 1, sa_scr[...], keep)
            scores_to(sa_scr, last)
            return state

        scores_to(sa_scr, 0)
        init = (jnp.full((1, QW), NEG_INF, F32), jnp.zeros((1, QW), F32), jnp.zeros((HEAD_DIM, QW), F32))
        state = lax.fori_loop(0, last // 2, two_chunks, init)
        state = lax.cond(last % 2 == 1, odd_chunk, lambda st: st, state)
        visible = (last * KC + lax.broadcasted_iota(jnp.int32, (KC, QW), 0)) <= tqs[k]
        _, l_s, acc_s = accumulate(state, last, jnp.where(visible, sa_scr[...], NEG_INF),
                                   lambda p: jnp.where(visible, p, 0.0))
        return acc_s / jnp.maximum(l_s, 1e-20)

    few, some = SEL_SLOTS
    for k in range(ATT_TILES):
        outs = []
        for h in range(N_KV_HEADS):
            ids, n_live = lists[k, h]
            o_s = lax.cond(n_live <= few, lambda _: gathered(k, h, ids, few),
                           lambda _: lax.cond(n_live <= some, lambda _: gathered(k, h, ids, some),
                                              lambda _: swept(k, h), 0), 0)
            gk = lambda br: g_ref[h, br:br + 1, cols(k)]
            o = gk(0) * cmp_out[k, h][0] + gk(1) * o_s + gk(2) * o_win[k, h]
            for g in range(0, GQA_GROUP, 2):
                sq = jnp.concatenate([o[:, g * Q_BLOCK:(g + 1) * Q_BLOCK], o[:, (g + 1) * Q_BLOCK:(g + 2) * Q_BLOCK]],
                                     axis=0)
                outs.append(sq.T)
        o_ref[k * Q_BLOCK:(k + 1) * Q_BLOCK, :] = jnp.concatenate(outs, axis=1).astype(BF16)


def _attend(qt, ks, vst, kwn, vwt, gt, kvc_c):
    B, _, S, _ = ks.shape
    nqb = S // Q_BLOCK
    QW = GQA_GROUP * Q_BLOCK
    n_sel = S // SEL_BLOCK
    n_pick = min(SEL_TOPK, n_sel)
    nr = S // CMP_STRIDE
    n_cmp = (S - CMP_BLOCK) // CMP_STRIDE + 1
    assert n_sel % 16 == 0 and n_sel <= AUG

    cpos = np.arange(nr) * CMP_STRIDE + CMP_BLOCK - 1
    digits = np.zeros((nr, AUG - HEAD_DIM), np.float32)
    digits[:, 0] = cpos // POS_BASE
    digits[:, 1] = cpos % POS_BASE
    kc = kvc_c[:, :N_KV_HEADS]
    kc = jnp.concatenate([kc, jnp.broadcast_to(jnp.asarray(digits, BF16), kc.shape[:2] + digits.shape)], axis=-1)
    vct = kvc_c[:, N_KV_HEADS:].transpose(0, 1, 3, 2)
    slopes = 2.0 ** (-8.0 * np.arange(1, N_HEADS + 1) / N_HEADS)
    slope_t = jnp.asarray(np.repeat(slopes.reshape(N_KV_HEADS, GQA_GROUP), Q_BLOCK, axis=1)
                          .reshape(N_KV_HEADS, 1, QW), F32)
    cmp_start = np.arange(n_cmp) * CMP_STRIDE
    sel_start = np.arange(n_sel) * SEL_BLOCK
    ov = ((cmp_start[:, None] <= sel_start[None, :] + SEL_BLOCK - 1)
          & (cmp_start[:, None] + CMP_BLOCK - 1 >= sel_start[None, :])).astype(np.float32)
    ovt = np.zeros((n_sel, nr), np.float32)
    ovt[:, :n_cmp] = ov.T
    ovt = jnp.asarray(ovt, BF16)

    H = N_KV_HEADS
    per_bh = lambda r, c: pl.BlockSpec((None, H, r, c), lambda b, i: (b, 0, 0, 0))
    per_q = lambda r: pl.BlockSpec((None, H, r, ATT_TILES * QW), lambda b, i: (b, 0, 0, i))
    return pl.pallas_call(
        functools.partial(_attend_kernel, n_sel=n_sel, n_pick=n_pick, n_cmp_rows=nr),
        grid=(B, nqb // ATT_TILES),
        in_specs=[per_q(HEAD_DIM), per_bh(nr, AUG), per_bh(HEAD_DIM, nr),
                  per_bh(S, 2 * AUG), per_bh(HEAD_DIM, S), per_bh(S, AUG), per_bh(HEAD_DIM, S),
                  per_q(3),
                  pl.BlockSpec((H, 1, QW), lambda b, i: (0, 0, 0)),
                  pl.BlockSpec((n_sel, nr), lambda b, i: (0, 0))],
        out_specs=pl.BlockSpec((None, ATT_TILES * Q_BLOCK, NSA_WIDTH), lambda b, i: (b, i, 0)),
        out_shape=jax.ShapeDtypeStruct((B, S, NSA_WIDTH), BF16),
        scratch_shapes=[pltpu.VMEM((ATT_TILES, H, 2 * AUG, QW), BF16), pltpu.VMEM((SEL_KEY_CHUNK, QW), F32),
                        pltpu.VMEM((SEL_KEY_CHUNK, QW), F32)],
        compiler_params=_params(("arbitrary", "arbitrary")),
        name="attend",
    )(qt, kc, vct, ks, vst, kwn, vwt, gt, slope_t, ovt)


def _merge_kernel(x_ref, mod_ref, ys_ref, on_ref, gm_ref, wglu_ref, bglu_ref, wa_ref, wb_ref, wo_ref,
                  gain_ref, wrh_ref, wrl_ref, x1_ref, h2_ref, sc_ref):
    D = D_MODEL
    z = jax.nn.gelu(_from_chunk_major(lambda groups, lanes: ys_ref[:, groups, lanes], x_ref.shape[0]))
    glu = z * jax.nn.sigmoid(_dot(z.astype(BF16), wglu_ref[...]) + bglu_ref[...])
    ya = _dot(glu.astype(BF16), wa_ref[...])
    yb = _dot(on_ref[...], wb_ref[...])
    merged = gm_ref[:, :D].astype(F32) * ya + gm_ref[:, D:].astype(F32) * yb
    x1 = x_ref[...] + mod_ref[2:3, :] * _dot(merged.astype(BF16), wo_ref[...])
    x1_ref[...] = x1
    y = x1 * lax.rsqrt(jnp.mean(x1 * x1, axis=-1, keepdims=True) + RMS_EPS)
    h2 = (y * gain_ref[...]) * (1.0 + mod_ref[4:5, :]) + mod_ref[3:4, :]
    h2_ref[...] = h2.astype(BF16)
    hh, hl = _split(h2)
    nt = (((1,), (1,)), ((), ()))
    dg = lambda a, b: lax.dot_general(a, b, nt, preferred_element_type=F32)
    logits = dg(wrh_ref[...], hh) + (dg(wrh_ref[...], hl) + dg(wrl_ref[...], hh))
    sc_ref[...] = jax.nn.sigmoid(logits)


def _merge(x, mod, ys5, o_nsa, gm, w_glu, b_glu, w_a, w_b, w_out, gain_f, w_router):
    B, S, D = x.shape
    tm = min(ROW_TILE, S)
    wrh, wrl = _split(w_router.T)
    ws = [w_glu.astype(BF16), b_glu.reshape(1, -1), w_a.astype(BF16), w_b.astype(BF16), w_out.astype(BF16),
          gain_f, wrh, wrl]
    row = lambda w: pl.BlockSpec((None, tm, w), lambda b, i: (b, i, 0))
    full = lambda a: pl.BlockSpec(a.shape, lambda b, i: (0,) * a.ndim)
    return pl.pallas_call(
        _merge_kernel,
        grid=(B, S // tm),
        in_specs=[row(D), pl.BlockSpec((None, N_MOD, D), lambda b, i: (b, 0, 0)),
                  _s5_chunk_spec(tm), row(NSA_WIDTH), row(2 * D)] + [full(a) for a in ws],
        out_specs=[row(D), row(D), pl.BlockSpec((None, N_EXPERTS, tm), lambda b, i: (b, 0, i))],
        out_shape=[jax.ShapeDtypeStruct((B, S, D), F32), jax.ShapeDtypeStruct((B, S, D), BF16),
                   jax.ShapeDtypeStruct((B, N_EXPERTS, S), F32)],
        compiler_params=_params(("arbitrary", "arbitrary")),
        name="merge",
    )(x, mod, ys5, o_nsa, gm, *ws)


def _first_argmax_rows(v, idx, n):
    m = jnp.max(v, axis=0, keepdims=True)
    first = jnp.min(jnp.where(v == m, idx, n), axis=0, keepdims=True)
    return idx == first, m


def _route_kernel(sc_ref, bias_ref, tri_ref, rank_ref, w_ref, cnt_ref):
    E, NG = N_EXPERTS, N_EXPERT_GROUPS
    GS = E // NG
    sc = sc_ref[...]
    TM = sc.shape[1]
    sel = sc + bias_ref[...]
    i8 = lax.broadcasted_iota(jnp.int32, (GS, TM), 0)
    gscore = []
    for g in range(NG):
        blk = sel[g * GS:(g + 1) * GS, :]
        hit, m1 = _first_argmax_rows(blk, i8, GS)
        m2 = jnp.max(jnp.where(hit, -jnp.inf, blk), axis=0, keepdims=True)
        gscore.append(m1 + m2)
    gscore = jnp.concatenate(gscore, axis=0)
    ig = lax.broadcasted_iota(jnp.int32, (NG, TM), 0)
    gmask = jnp.zeros((NG, TM), F32)
    for _ in range(TOPK_EXPERT_GROUPS):
        hit, _m = _first_argmax_rows(gscore, ig, NG)
        gmask = jnp.where(hit, 1.0, gmask)
        gscore = jnp.where(hit, -jnp.inf, gscore)
    emask = jnp.concatenate([jnp.broadcast_to(gmask[g:g + 1, :], (GS, TM)) for g in range(NG)], axis=0)
    cand = jnp.where(emask > 0.5, sel, NEG_INF)
    ie = lax.broadcasted_iota(jnp.int32, (E, TM), 0)
    chosen = jnp.zeros((E, TM), F32)
    for _ in range(TOP_K):
        hit, _m = _first_argmax_rows(cand, ie, E)
        chosen = jnp.where(hit, 1.0, chosen)
        cand = jnp.where(hit, -jnp.inf, cand)
    w = chosen * sc
    w = w / jnp.sum(w, axis=0, keepdims=True) * ROUTED_SCALE
    sub = rank_ref.shape[-1]
    for q in range(TM // sub):
        cols = slice(q * sub, (q + 1) * sub)
        cb = chosen[:, cols].astype(BF16)
        prefix = _dot(cb, tri_ref[...])
        rank_ref[q] = jnp.where(chosen[:, cols] > 0.5, prefix, RANK_NONE)
        w_ref[q] = w[:, cols]
        cnt_ref[q] = _dot(cb, jnp.ones((sub, LANES), BF16))


def _route(scores_t, router_bias):
    B, E, S = scores_t.shape
    TM = min(ROUTE_TILE, S)
    per_step = min(MOE_OUTER, S) // TM
    nj = S // (TM * per_step)
    ns = B * nj * per_step
    tri = jnp.asarray(np.triu(np.ones((TM, TM), np.float32), k=1), BF16)
    tile = lambda w: pl.BlockSpec((per_step, E, w), lambda b, j: (b * nj + j, 0, 0))
    return pl.pallas_call(
        _route_kernel,
        grid=(B, nj),
        in_specs=[pl.BlockSpec((None, E, TM * per_step), lambda b, j: (b, 0, j)),
                  pl.BlockSpec((E, 1), lambda b, j: (0, 0)),
                  pl.BlockSpec((TM, TM), lambda b, j: (0, 0))],
        out_specs=[tile(TM), tile(TM), tile(LANES)],
        out_shape=[jax.ShapeDtypeStruct((ns, E, TM), F32), jax.ShapeDtypeStruct((ns, E, TM), F32),
                   jax.ShapeDtypeStruct((ns, E, LANES), F32)],
        compiler_params=_params(("arbitrary", "arbitrary")),
        name="route",
    )(scores_t, router_bias.reshape(E, 1), tri)


def _moe_kernel(cnt_ref, h_ref, x1_ref, mod_ref, rank_ref, w_ref, wg_ref, wu_ref, wd_ref,
                sg_ref, su_ref, sd_ref, o_ref, xc, yc, pc, wc, *, n_sub, tm, n_exp):
    to = pl.program_id(0)
    eb = pl.program_id(1)
    gate = mod_ref[5:6, :]

    @pl.when(eb == 0)
    def _shared():
        for s in range(n_sub):
            rows = pl.ds(s * tm, tm)
            hs = h_ref[rows, :]
            hid = jax.nn.silu(_dot(hs, sg_ref[...])) * _dot(hs, su_ref[...])
            o_ref[rows, :] = x1_ref[rows, :] + gate * _dot(hid.astype(BF16), sd_ref[...])
            xc[s] = jnp.zeros(xc.shape[1:], BF16)
            yc[s] = jnp.zeros(yc.shape[1:], BF16)
            wc[s] = jnp.zeros(wc.shape[1:], F32)

    PIECE, WIN = MOE_PIECE, MOE_WIN
    tn = (((0,), (0,)), ((), ()))
    experts = [eb * n_exp + j for j in range(n_exp)]
    cnt = [[cnt_ref[(to * n_sub + s) * N_EXPERTS + e] for e in experts] for s in range(n_sub)]
    off = []
    for s in range(n_sub):
        o = [jnp.int32(0)]
        for j in range(n_exp):
            o.append(o[-1] + ((cnt[s][j] + 15) // 16) * 16)
        off.append(o)
    n_pieces = [(off[s][n_exp] + PIECE - 1) // PIECE for s in range(n_sub)]
    fits = n_pieces[0] < MOE_ROWS // PIECE
    for s in range(1, n_sub):
        fits = jnp.logical_and(fits, n_pieces[s] < MOE_ROWS // PIECE)

    def one_hot_rows(s, base, rows_n, js):
        slot = (lax.broadcasted_iota(jnp.int32, (rows_n, tm), 0) + base).astype(F32)
        pick = jnp.zeros((rows_n, tm), F32)
        wacc = jnp.zeros((rows_n, tm), F32)
        for j in js:
            hit = (slot - off[s][j].astype(F32)) == rank_ref[s, pl.ds(experts[j], 1), :]
            pick = jnp.where(hit, 1.0, pick)
            wacc = jnp.where(hit, w_ref[s, pl.ds(experts[j], 1), :], wacc)
        return pick.astype(BF16), jnp.sum(wacc, axis=1, keepdims=True)

    def mlp(xg, j):
        hid = jax.nn.silu(_dot(xg, wg_ref[j])) * _dot(xg, wu_ref[j])
        return _dot(hid.astype(BF16), wd_ref[j])

    def piece(s, q):
        r = pl.ds(q * PIECE, PIECE)
        pick, wrow = one_hot_rows(s, q * PIECE, PIECE, range(n_exp))
        xc[s, r, :] = _dot(pick, h_ref[pl.ds(s * tm, tm), :]).astype(BF16)
        pc[s, :, r] = pick.T
        wc[s, r, :] = wrow

    def window(j, p, first):
        starts = [pl.multiple_of(jnp.minimum(off[s][j] + p * WIN, (n_pieces[s] + 1) * PIECE - WIN), 16)
                  for s in range(n_sub)]
        xg = jnp.concatenate([xc[s, pl.ds(starts[s], WIN), :] for s in range(n_sub)], axis=0)
        out = mlp(xg, j)
        for s in range(n_sub):
            r = pl.ds(starts[s], WIN)
            new = (out[s * WIN:(s + 1) * WIN] * wc[s, r, :] * gate).astype(BF16)
            if not first:
                rank = starts[s] - off[s][j] + lax.broadcasted_iota(jnp.int32, (WIN, 1), 0)
                new = jnp.where(rank < cnt[s][j], new, yc[s, r, :])
            yc[s, r, :] = new

    def scatter(s, start, n):
        r = pl.ds(start, n)
        o_ref[pl.ds(s * tm, tm), :] += _dot(pc[s, :, r], yc[s, r, :])

    @pl.when(fits)
    def _packed():
        for s in range(n_sub):
            piece(s, 0)
            piece(s, 1)
        for s in range(n_sub):
            @pl.when(n_pieces[s] > 2)
            def _third(s=s):
                piece(s, 2)
        for j in range(n_exp):
            window(j, 0, True)
        for j in range(n_exp):
            most = cnt[0][j]
            for s in range(1, n_sub):
                most = jnp.maximum(most, cnt[s][j])

            @pl.when(most > WIN)
            def _more(j=j, most=most):
                lax.fori_loop(1, (most + WIN - 1) // WIN, lambda p, c: (window(j, p, False), c)[1], 0)
        for s in range(n_sub):
            scatter(s, 0, 2 * PIECE)
        for s in range(n_sub):
            @pl.when(n_pieces[s] > 2)
            def _third_out(s=s):
                scatter(s, 2 * PIECE, PIECE)

    @pl.when(jnp.logical_not(fits))
    def _unpacked():
        for j in range(n_exp):
            for s in range(n_sub):
                rows = pl.ds(s * tm, tm)

                def block(bi, carry, j=j, s=s, rows=rows):
                    slot0 = off[s][j] + bi * PIECE
                    pick, wrow = one_hot_rows(s, slot0, PIECE, [j])
                    out = mlp(_dot(pick, h_ref[rows, :]).astype(BF16), j)
                    ow = (out * wrow * gate).astype(BF16)
                    o_ref[rows, :] += lax.dot_general(pick, ow, tn, preferred_element_type=F32)
                    return carry

                lax.fori_loop(0, (cnt[s][j] + PIECE - 1) // PIECE, block, 0)


def _moe(h2, x1, mod, rank_t, w_t, counts, w_gate, w_up, w_down, ws_gate, ws_up, ws_down, seq):
    T, D = h2.shape
    ns, E, tm = rank_t.shape
    tmo = min(MOE_OUTER, seq)
    n_sub = tmo // tm
    n_exp = MOE_EXPERTS_PER_STEP
    per_b = seq // tmo
    ws = [w_gate.astype(BF16), w_up.astype(BF16), w_down.astype(BF16)]
    sh = [ws_gate.astype(BF16), ws_up.astype(BF16), ws_down.astype(BF16)]
    tok = lambda: pl.BlockSpec((tmo, D), lambda t, e, c: (t, 0))
    sub = lambda: pl.BlockSpec((n_sub, E, tm), lambda t, e, c: (t, 0, 0))
    exp = lambda a: pl.BlockSpec((n_exp,) + a.shape[1:], lambda t, e, c: (e, 0, 0))
    full = lambda a: pl.BlockSpec(a.shape, lambda t, e, c: (0, 0))
    grid_spec = pltpu.PrefetchScalarGridSpec(
        num_scalar_prefetch=1,
        grid=(T // tmo, E // n_exp),
        in_specs=[tok(), tok(), pl.BlockSpec((None, N_MOD, D), lambda t, e, c: (t // per_b, 0, 0)),
                  sub(), sub(), exp(ws[0]), exp(ws[1]), exp(ws[2]), full(sh[0]), full(sh[1]), full(sh[2])],
        out_specs=tok(),
        scratch_shapes=[pltpu.VMEM((n_sub, MOE_ROWS, D), BF16), pltpu.VMEM((n_sub, MOE_ROWS, D), BF16),
                        pltpu.VMEM((n_sub, tm, MOE_ROWS), BF16), pltpu.VMEM((n_sub, MOE_ROWS, 1), F32)],
    )
    return pl.pallas_call(
        functools.partial(_moe_kernel, n_sub=n_sub, tm=tm, n_exp=n_exp),
        grid_spec=grid_spec,
        out_shape=jax.ShapeDtypeStruct((T, D), F32),
        compiler_params=_params(("arbitrary", "arbitrary")),
        name="moe",
    )(counts, h2, x1, mod, rank_t, w_t, *ws, *sh)


def kernel(x, c, w_ada, b_ada, norm_mix_gain, norm_ffn_gain, w_in, s5_lambda_re, s5_lambda_im, s5_log_dt, s5_b_re, s5_b_im, s5_c_re, s5_c_im, s5_d, s5_w_glu, s5_b_glu, q_norm_gain, k_norm_gain, cmp_pe, cmp_w1, cmp_b1, cmp_w2, cmp_b2, w_branch_a, w_branch_b, w_out, w_router, router_bias, w_gate, w_up, w_down, ws_gate, ws_up, ws_down):
    B, S, D = x.shape
    for l in range(w_ada.shape[0]):
        mod = _ada(c, w_ada[l], b_ada[l]).reshape(B, N_MOD, D)
        u, kvc, qt, ks, vst, kwn, vwt, gt, gm = _inproj(x, mod, norm_mix_gain[l:l + 1], w_in[l],
                                                        q_norm_gain[l], k_norm_gain[l])
        tables = _s5_tables(s5_lambda_re[l], s5_lambda_im[l], s5_log_dt[l], s5_b_re[l], s5_b_im[l],
                            s5_c_re[l], s5_c_im[l], s5_d[l])
        ys5 = _s5(u, tables)
        kvc_c = _compress(kvc, cmp_pe[l], cmp_w1[l], cmp_b1[l], cmp_w2[l], cmp_b2[l], k_norm_gain[l, 0])
        o_nsa = _attend(qt, ks, vst, kwn, vwt, gt, kvc_c)
        x1, h2, scores_t = _merge(x, mod, ys5, o_nsa, gm, s5_w_glu[l], s5_b_glu[l], w_branch_a[l],
                                  w_branch_b[l], w_out[l], norm_ffn_gain[l:l + 1], w_router[l])
        rank_t, w_t, cnt = _route(scores_t, router_bias[l])
        counts = cnt[:, :, 0].astype(jnp.int32).reshape(-1)
        x = _moe(h2.reshape(B * S, D), x1.reshape(B * S, D), mod, rank_t, w_t, counts,
                 w_gate[l], w_up[l], w_down[l], ws_gate[l], ws_up[l], ws_down[l], S).reshape(B, S, D)
    return x
```

```python
import functools

import numpy as np
import jax
import jax.numpy as jnp
from jax import lax
from jax.experimental import pallas as pl
from jax.experimental.pallas import tpu as pltpu

F32 = jnp.float32
BF16 = jnp.bfloat16

D_MODEL = 1024
S5_WIDTH = 512
S5_GROUP = 16
S5_GROUPS = S5_WIDTH // S5_GROUP
S5_STATE = 64
N_HEADS = 8
N_KV_HEADS = 2
GQA_GROUP = N_HEADS // N_KV_HEADS
HEAD_DIM = 64
NSA_WIDTH = N_HEADS * HEAD_DIM
KV_WIDTH = 2 * N_KV_HEADS * HEAD_DIM
CMP_BLOCK = 32
CMP_STRIDE = 16
CMP_HIDDEN = 256
SEL_BLOCK = 64
SEL_TOPK = 8
WINDOW = 256
Q_BLOCK = 128
FORCE_BONUS = 1e3
N_EXPERTS = 64
TOP_K = 8
N_EXPERT_GROUPS = 8
TOPK_EXPERT_GROUPS = 4
ROUTED_SCALE = 2.5
RMS_EPS = 1e-6
NEG_INF = -1e30
N_MOD = 6

LANES = 128
SUBLANES = 8
S5_CHUNK = 16
S5_CW = S5_CHUNK * S5_GROUP
ROW_TILE = 512
ATT_TILES = 2
SEL_KEY_CHUNK = 512
SEL_UNIT = 128
SEL_SLOTS = (5, 8)
POS_BASE = 64
POS_ROWS = 16
AUG = 128
ROUTE_TILE = 256
MOE_OUTER = 1024
MOE_EXPERTS_PER_STEP = 4
MOE_PIECE = 128
MOE_WIN = 48
MOE_ROWS = 512
RANK_NONE = -float(1 << 20)
VMEM_LIMIT = 56 * 1024 * 1024


def _dot(a, b):
    return jnp.dot(a, b, preferred_element_type=F32)


def _split(a):
    hi = a.astype(BF16)
    lo = (a - hi.astype(F32)).astype(BF16)
    return hi, lo


def _dot3(a, bh, bl):
    ah, al = _split(a)
    return _dot(ah, bh) + (_dot(al, bh) + _dot(ah, bl))


def _segment_transpose(x):
    assert x.shape[-2:] == (SUBLANES, LANES) and LANES // S5_GROUP == SUBLANES
    nd = x.ndim
    i = lax.broadcasted_iota(jnp.int32, x.shape, nd - 2)
    seg = lax.broadcasted_iota(jnp.int32, x.shape, nd - 1) // S5_GROUP
    out = x
    for d in range(1, SUBLANES):
        r = pltpu.roll(pltpu.roll(x, SUBLANES - d, axis=nd - 2), S5_GROUP * d, axis=nd - 1)
        out = jnp.where(seg == ((i + d) & (SUBLANES - 1)), r, out)
    return out


def _to_chunk_major(u, put):
    rows = u.shape[0]
    u3 = u.reshape(rows // SUBLANES, SUBLANES, S5_WIDTH)
    halves = S5_CHUNK // SUBLANES
    for jb in range(S5_WIDTH // LANES):
        t = _segment_transpose(u3[:, :, LANES * jb:LANES * (jb + 1)])
        t = t.reshape(rows // S5_CHUNK, halves, SUBLANES, LANES)
        for hf in range(halves):
            put(slice(SUBLANES * jb, SUBLANES * (jb + 1)), slice(LANES * hf, LANES * (hf + 1)), t[:, hf])


def _from_chunk_major(get, rows):
    halves = S5_CHUNK // SUBLANES
    cols = []
    for jb in range(S5_WIDTH // LANES):
        parts = [_segment_transpose(get(slice(SUBLANES * jb, SUBLANES * (jb + 1)), slice(LANES * hf, LANES * (hf + 1))))
                 for hf in range(halves)]
        cols.append(jnp.stack(parts, axis=1).reshape(rows, LANES))
    return jnp.concatenate(cols, axis=1)


def _params(sem):
    return pltpu.CompilerParams(dimension_semantics=sem, vmem_limit_bytes=VMEM_LIMIT)


def _ada_kernel(c_ref, w_ref, b_ref, o_ref):
    cs = jax.nn.silu(c_ref[...])
    wh, wl = _split(w_ref[...])
    o_ref[...] = _dot3(cs, wh, wl) + b_ref[...]


def _ada(c, w_ada, b_ada):
    B, D = c.shape
    return pl.pallas_call(
        _ada_kernel,
        grid=(N_MOD,),
        in_specs=[pl.BlockSpec((B, D), lambda j: (0, 0)),
                  pl.BlockSpec((D, D), lambda j: (0, j)),
                  pl.BlockSpec((1, D), lambda j: (0, j))],
        out_specs=pl.BlockSpec((B, D), lambda j: (0, j)),
        out_shape=jax.ShapeDtypeStruct((B, N_MOD * D), F32),
        compiler_params=_params(("arbitrary",)),
        name="ada",
    )(c, w_ada, b_ada.reshape(1, N_MOD * D))


def _head_norm(v, bd, gain):
    sq = v * v
    sh, sl = _split(sq)
    ms = _dot(sh, bd) + _dot(sl, bd)
    return v * lax.rsqrt(ms + RMS_EPS) * gain


def _inproj_kernel(x_ref, mod_ref, gain_ref, wm_ref, wt_ref, wgn_ref, wgm_ref, bd_ref, qg_ref, kg_ref,
                   u_ref, kvc_ref, qt_ref, ksa_ref, vst_ref, kwa_ref, vwt_ref, gt_ref, gm_ref):
    x = x_ref[...]
    shift = mod_ref[0:1, :]
    scale = mod_ref[1:2, :]
    y = x * lax.rsqrt(jnp.mean(x * x, axis=-1, keepdims=True) + RMS_EPS)
    h = (y * gain_ref[...]) * (1.0 + scale) + shift
    hb = h.astype(BF16)
    main = _dot(hb, wm_ref[...])
    tm = x.shape[0]

    def put(groups, lanes, block):
        u_ref[:, groups, lanes] = block

    _to_chunk_major(main[:, :S5_WIDTH], put)
    o = S5_WIDTH
    for part in range(KV_WIDTH // LANES):
        kvc_ref[part] = main[:, o + part * LANES:o + (part + 1) * LANES]
    o += KV_WIDTH
    kw = N_KV_HEADS * HEAD_DIM
    bd = bd_ref[...]
    ks = _head_norm(main[:, o:o + kw], bd[:kw, :kw], kg_ref[1:2, :]).astype(BF16)
    kwn = _head_norm(main[:, o + kw:o + 2 * kw], bd[:kw, :kw], kg_ref[2:3, :]).astype(BF16)
    pos = pl.program_id(1) * tm + lax.broadcasted_iota(jnp.int32, (tm, 1), 0)
    lane = lax.broadcasted_iota(jnp.int32, (tm, AUG), 1)
    digits = jnp.where(lane == 0, pos // POS_BASE, jnp.where(lane == 1, pos % POS_BASE, 0))
    digits = digits[:, :AUG - HEAD_DIM].astype(F32).astype(BF16)
    onehot = jnp.where(lane == pos // SEL_BLOCK, 1.0, 0.0).astype(BF16)
    for hh in range(N_KV_HEADS):
        ksa_ref[hh, :, 0:HEAD_DIM] = ks[:, hh * HEAD_DIM:(hh + 1) * HEAD_DIM]
        ksa_ref[hh, :, HEAD_DIM:AUG] = digits
        ksa_ref[hh, :, AUG:] = onehot
        kwa_ref[hh, :, 0:HEAD_DIM] = kwn[:, hh * HEAD_DIM:(hh + 1) * HEAD_DIM]
        kwa_ref[hh, :, HEAD_DIM:] = digits

    nt = (((1,), (1,)), ((), ()))
    tt = lax.dot_general(wt_ref[...], hb, nt, preferred_element_type=F32)
    qt = tt[:NSA_WIDTH]
    sq = qt * qt
    sh, sl = _split(sq)
    qn = (qt * lax.rsqrt(_dot(bd, sh) + _dot(bd, sl) + RMS_EPS) * qg_ref[...] * (HEAD_DIM ** -0.5)).astype(BF16)
    QW = GQA_GROUP * Q_BLOCK
    for qb in range(tm // Q_BLOCK):
        for hd in range(N_HEADS):
            hh, g = divmod(hd, GQA_GROUP)
            qt_ref[hh, :, qb * QW + g * Q_BLOCK:qb * QW + (g + 1) * Q_BLOCK] = (
                qn[hd * HEAD_DIM:(hd + 1) * HEAD_DIM, qb * Q_BLOCK:(qb + 1) * Q_BLOCK])
    vst_ref[...] = tt[NSA_WIDTH:NSA_WIDTH + kw].reshape(N_KV_HEADS, HEAD_DIM, tm).astype(BF16)
    vwt_ref[...] = tt[NSA_WIDTH + kw:].reshape(N_KV_HEADS, HEAD_DIM, tm).astype(BF16)
    gn = jax.nn.sigmoid(lax.dot_general(wgn_ref[...], hb, nt, preferred_element_type=F32))
    for qb in range(tm // Q_BLOCK):
        for br in range(3):
            for hd in range(N_HEADS):
                hh, g = divmod(hd, GQA_GROUP)
                r = br * N_HEADS + hd
                gt_ref[hh, br:br + 1, qb * QW + g * Q_BLOCK:qb * QW + (g + 1) * Q_BLOCK] = (
                    gn[r:r + 1, qb * Q_BLOCK:(qb + 1) * Q_BLOCK])
    gm_ref[...] = jax.nn.sigmoid(_dot(hb, wgm_ref[...])).astype(BF16)


def _s5_chunk_spec(tm):
    assert S5_CHUNK == 16 and S5_GROUP == 16, "the segment transposes assume 16 steps x 16 channels"
    return pl.BlockSpec((tm // S5_CHUNK, None, S5_GROUPS, S5_CW), lambda b, i: (i, b, 0, 0))


def _inproj(x, mod, gain, w_in, q_gain, k_gain):
    B, S, D = x.shape
    tm = min(ROW_TILE, S)
    assert S // SEL_BLOCK <= AUG and tm % Q_BLOCK == 0
    kw = N_KV_HEADS * HEAD_DIM
    cols = np.cumsum((0,) + (S5_WIDTH, NSA_WIDTH, KV_WIDTH, KV_WIDTH, KV_WIDTH, 3 * N_HEADS, 2 * D))
    c_u, c_q, c_kvc, c_kvs, c_kvw, c_gn, c_gm = cols[:7]
    sl = lambda a, n: w_in[:, a:a + n]
    wm = jnp.concatenate([sl(c_u, S5_WIDTH), sl(c_kvc, KV_WIDTH), sl(c_kvs, kw), sl(c_kvw, kw)], axis=1).astype(BF16)
    wt = jnp.concatenate([sl(c_q, NSA_WIDTH), sl(c_kvs + kw, kw), sl(c_kvw + kw, kw)], axis=1).T.astype(BF16)
    wgn = sl(c_gn, 3 * N_HEADS).T.astype(BF16)
    wgm = sl(c_gm, 2 * D).astype(BF16)
    seg = np.arange(NSA_WIDTH) // HEAD_DIM
    bd = jnp.asarray((seg[:, None] == seg[None, :]).astype(np.float32) / HEAD_DIM, BF16)
    qg = jnp.tile(q_gain, N_HEADS).reshape(NSA_WIDTH, 1)
    kg = jnp.tile(k_gain, (1, N_KV_HEADS))
    nq = tm // Q_BLOCK * GQA_GROUP * Q_BLOCK
    row = lambda w: pl.BlockSpec((None, tm, w), lambda b, i: (b, i, 0))
    full = lambda a: pl.BlockSpec(a.shape, lambda b, i: (0,) * a.ndim)
    rows4 = lambda w: pl.BlockSpec((None, N_KV_HEADS, tm, w), lambda b, i: (b, 0, i, 0))
    cols4 = lambda r, w: pl.BlockSpec((None, N_KV_HEADS, r, w), lambda b, i: (b, 0, 0, i))
    nqt = S // Q_BLOCK * GQA_GROUP * Q_BLOCK
    return pl.pallas_call(
        _inproj_kernel,
        grid=(B, S // tm),
        in_specs=[row(D), pl.BlockSpec((None, N_MOD, D), lambda b, i: (b, 0, 0)),
                  full(gain), full(wm), full(wt), full(wgn), full(wgm), full(bd), full(qg), full(kg)],
        out_specs=[_s5_chunk_spec(tm), pl.BlockSpec((None, KV_WIDTH // LANES, tm, LANES), lambda b, i: (b, 0, i, 0)),
                   cols4(HEAD_DIM, nq), rows4(2 * AUG), cols4(HEAD_DIM, tm),
                   rows4(AUG), cols4(HEAD_DIM, tm), cols4(3, nq), row(2 * D)],
        out_shape=[jax.ShapeDtypeStruct((S // S5_CHUNK, B, S5_GROUPS, S5_CW), F32),
                   jax.ShapeDtypeStruct((B, KV_WIDTH // LANES, S, LANES), F32),
                   jax.ShapeDtypeStruct((B, N_KV_HEADS, HEAD_DIM, nqt), BF16),
                   jax.ShapeDtypeStruct((B, N_KV_HEADS, S, 2 * AUG), BF16),
                   jax.ShapeDtypeStruct((B, N_KV_HEADS, HEAD_DIM, S), BF16),
                   jax.ShapeDtypeStruct((B, N_KV_HEADS, S, AUG), BF16),
                   jax.ShapeDtypeStruct((B, N_KV_HEADS, HEAD_DIM, S), BF16),
                   jax.ShapeDtypeStruct((B, N_KV_HEADS, 3, nqt), F32),
                   jax.ShapeDtypeStruct((B, S, 2 * D), BF16)],
        compiler_params=_params(("arbitrary", "arbitrary")),
        name="inproj",
    )(x, mod, gain, wm, wt, wgn, wgm, bd, qg, kg)


def _s5_tables(lam_re, lam_im, log_dt, b_re, b_im, c_re, c_im, d_skip):
    L, P, N, G = S5_CHUNK, S5_GROUP, S5_STATE, S5_GROUPS
    hp = lax.Precision.HIGHEST
    lr, li = lam_re.astype(F32), lam_im.astype(F32)
    dt = jnp.exp(log_dt.astype(F32))[:, None]
    mag = jnp.exp(lr * dt)
    abar_re, abar_im = mag * jnp.cos(li * dt), mag * jnp.sin(li * dt)
    num_re, num_im = abar_re - 1.0, abar_im
    den = lr * lr + li * li
    coef_re = (num_re * lr + num_im * li) / den
    coef_im = (num_im * lr - num_re * li) / den
    br, bi = b_re.astype(F32), b_im.astype(F32)
    bbar_re = coef_re[..., None] * br - coef_im[..., None] * bi
    bbar_im = coef_re[..., None] * bi + coef_im[..., None] * br
    k = jnp.arange(L + 1, dtype=F32)[:, None, None]
    pmag = jnp.exp(lr * dt * k)
    pre, pim = pmag * jnp.cos(li * dt * k), pmag * jnp.sin(li * dt * k)
    cr, ci = c_re.astype(F32), c_im.astype(F32)
    ca_re = cr[None] * pre[:, :, None, :] - ci[None] * pim[:, :, None, :]
    ca_im = cr[None] * pim[:, :, None, :] + ci[None] * pre[:, :, None, :]
    kern = (jnp.einsum('kgpn,gnq->gkqp', ca_re[:L], bbar_re, precision=hp)
            - jnp.einsum('kgpn,gnq->gkqp', ca_im[:L], bbar_im, precision=hp))
    s_i = np.arange(L)[:, None]
    t_i = np.arange(L)[None, :]
    tau = np.clip(t_i - s_i, 0, L - 1)
    causal = jnp.asarray((t_i >= s_i).astype(np.float32))
    mt = kern[:, tau] * causal[None, :, :, None, None]
    mt = mt.transpose(0, 1, 3, 2, 4).reshape(G, L * P, L * P)
    rev = np.arange(L - 1, -1, -1)
    ab_re = pre[rev][..., None] * bbar_re[None] - pim[rev][..., None] * bbar_im[None]
    ab_im = pre[rev][..., None] * bbar_im[None] + pim[rev][..., None] * bbar_re[None]
    ws = jnp.concatenate([ab_re, ab_im], axis=2)
    ws = ws.transpose(1, 0, 3, 2).reshape(G, L * P, 2 * N)
    wo = jnp.concatenate([ca_re[1:], -ca_im[1:]], axis=3)
    wo = wo.transpose(1, 3, 0, 2).reshape(G, 2 * N, L * P)
    al = jnp.stack([pre[L], pim[L]], axis=1)
    dv = jnp.tile(d_skip.astype(F32), (1, L)).reshape(G, 1, L * P)
    return mt, ws, wo, al, dv


def _s5_kernel(u_ref, mth_ref, mtl_ref, wsh_ref, wsl_ref, woh_ref, wol_ref, al_ref, dv_ref,
               y_ref, vr_scr, vi_scr, xr_scr, xi_scr, *, n_chunks, bsz):
    N = S5_STATE
    u = u_ref[...]
    uh, ul = _split(u)
    y = _dot(uh, mth_ref[...]) + (_dot(ul, mth_ref[...]) + _dot(uh, mtl_ref[...]))
    v = _dot(uh, wsh_ref[...]) + (_dot(ul, wsh_ref[...]) + _dot(uh, wsl_ref[...]))
    vr_scr[...] = v[:, :N]
    vi_scr[...] = v[:, N:]
    a_r = al_ref[0:1, :]
    a_i = al_ref[1:2, :]

    def step(c, x):
        xr, xi = x
        r = pl.ds(pl.multiple_of(c * bsz, bsz), bsz)
        xr_scr[r, :] = xr
        xi_scr[r, :] = xi
        return (a_r * xr - a_i * xi + vr_scr[r, :], a_r * xi + a_i * xr + vi_scr[r, :])

    zero = jnp.zeros((bsz, N), F32)
    lax.fori_loop(0, n_chunks, step, (zero, zero), unroll=8)
    xp = jnp.concatenate([xr_scr[...], xi_scr[...]], axis=1)
    y = y + _dot3(xp, woh_ref[...], wol_ref[...])
    y_ref[...] = y + dv_ref[...] * u


def _s5(ug, tables):
    nc, B, G, cw = ug.shape
    N = S5_STATE
    mt, ws, wo, al, dv = tables
    mth, mtl = _split(mt)
    wsh, wsl = _split(ws)
    woh, wol = _split(wo)
    grp = lambda a: pl.BlockSpec((None,) + a.shape[1:], lambda g: (g,) + (0,) * (a.ndim - 1))
    ugt = ug.reshape(nc * B, G, cw).transpose(1, 0, 2)
    y = pl.pallas_call(
        functools.partial(_s5_kernel, n_chunks=nc, bsz=B),
        grid=(G,),
        in_specs=[grp(ugt), grp(mth), grp(mtl), grp(wsh), grp(wsl), grp(woh), grp(wol), grp(al), grp(dv)],
        out_specs=grp(ugt),
        out_shape=jax.ShapeDtypeStruct(ugt.shape, F32),
        scratch_shapes=[pltpu.VMEM((nc * B, N), F32)] * 4,
        compiler_params=_params(("arbitrary",)),
        name="s5",
    )(ugt, mth, mtl, wsh, wsl, woh, wol, al, dv)
    return y.transpose(1, 0, 2).reshape(nc, B, G, cw)


def _compress_kernel(x_ref, pe_ref, w1_ref, b1_ref, w2_ref, b2_ref, kg_ref, o_ref, *, n_rows):
    half = CMP_STRIDE * HEAD_DIM
    for j in range(2 * N_KV_HEADS):
        kv, hd = divmod(j, N_KV_HEADS)
        x = jnp.concatenate(
            [x_ref[kv, pl.ds(t, n_rows, stride=CMP_STRIDE), :][:, hd * HEAD_DIM:(hd + 1) * HEAD_DIM]
             for t in range(CMP_STRIDE)], axis=1)
        a = _dot((x + pe_ref[kv, 0:1, :]).astype(BF16), w1_ref[kv, :half, :])
        b = _dot((x + pe_ref[kv, 1:2, :]).astype(BF16), w1_ref[kv, half:, :])
        hid = jax.nn.gelu(a + pltpu.roll(b, n_rows - 1, axis=0) + b1_ref[kv])
        out = _dot(hid.astype(BF16), w2_ref[kv]) + b2_ref[kv]
        if kv == 0:
            out = out * lax.rsqrt(jnp.mean(out * out, axis=-1, keepdims=True) + RMS_EPS) * kg_ref[...]
        o_ref[j] = out.astype(BF16)


def _compress(kv_c, cmp_pe, cmp_w1, cmp_b1, cmp_w2, cmp_b2, k_gain0):
    B, _, S, _ = kv_c.shape
    nr = S // CMP_STRIDE
    half = CMP_STRIDE * HEAD_DIM
    nj = 2 * N_KV_HEADS
    ws = [cmp_pe.reshape(2, 2, half), cmp_w1.astype(BF16), cmp_b1.reshape(2, 1, CMP_HIDDEN), cmp_w2.astype(BF16),
          cmp_b2.reshape(2, 1, HEAD_DIM), k_gain0.reshape(1, HEAD_DIM)]
    full = lambda a: pl.BlockSpec(a.shape, lambda b: (0,) * a.ndim)
    return pl.pallas_call(
        functools.partial(_compress_kernel, n_rows=nr),
        grid=(B,),
        in_specs=[pl.BlockSpec((None,) + kv_c.shape[1:], lambda b: (b, 0, 0, 0))] + [full(a) for a in ws],
        out_specs=pl.BlockSpec((None, nj, nr, HEAD_DIM), lambda b: (b, 0, 0, 0)),
        out_shape=jax.ShapeDtypeStruct((B, nj, nr, HEAD_DIM), BF16),
        compiler_params=_params(("arbitrary",)),
        name="compress",
    )(kv_c, *ws)


def _softmax_cols(s, mask):
    s = jnp.where(mask, s, NEG_INF)
    m = jnp.max(s, axis=0, keepdims=True)
    p = jnp.where(mask, jnp.exp(s - m), 0.0)
    return p / jnp.maximum(jnp.sum(p, axis=0, keepdims=True), 1e-20)


def _attend_kernel(qt_ref, kc_ref, vct_ref, ks_ref, vst_ref, kw_ref, vwt_ref, g_ref, slope_ref, ovt_ref,
                   o_ref, qa_scr, sa_scr, sb_scr, *, n_sel, n_pick, n_cmp_rows):
    QW = GQA_GROUP * Q_BLOCK
    chains = [(k, h) for k in range(ATT_TILES) for h in range(N_KV_HEADS)]
    t0s = [(pl.program_id(1) * ATT_TILES + k) * Q_BLOCK for k in range(ATT_TILES)]
    tqs = [t0 + (lax.broadcasted_iota(jnp.int32, (1, QW), 1) & (Q_BLOCK - 1)) for t0 in t0s]
    cols = lambda k: slice(k * QW, (k + 1) * QW)

    r = lax.broadcasted_iota(jnp.int32, (POS_ROWS, QW), 0)
    qa = {}
    for k, h in chains:
        slope = slope_ref[h]
        qa_scr[k, h, 0:HEAD_DIM, :] = qt_ref[h, :, cols(k)]
        qa_scr[k, h, HEAD_DIM:HEAD_DIM + POS_ROWS, :] = jnp.where(
            r == 0, slope * POS_BASE, jnp.where(r == 1, slope, 0.0)).astype(BF16)
        qa_scr[k, h, HEAD_DIM + POS_ROWS:AUG, :] = jnp.zeros((AUG - HEAD_DIM - POS_ROWS, QW), BF16)
        qa[k, h] = qa_scr[k, h, 0:AUG, :]

    def compressed(k, h, rows):
        sc = _dot(kc_ref[h, 0:rows, :], qa[k, h])
        cpos = lax.broadcasted_iota(jnp.int32, (rows, QW), 0) * CMP_STRIDE + (CMP_BLOCK - 1)
        p_c = _softmax_cols(sc, cpos <= tqs[k])
        o_c = _dot(vct_ref[h, :, 0:rows], p_c.astype(BF16))
        psum = p_c[:, 0:Q_BLOCK]
        for g in range(1, GQA_GROUP):
            psum = psum + p_c[:, g * Q_BLOCK:(g + 1) * Q_BLOCK]
        ph, pl_ = _split(psum)
        return o_c, _dot(ovt_ref[:, 0:rows], ph) + _dot(ovt_ref[:, 0:rows], pl_)

    half = n_cmp_rows // 2
    if half % LANES == 0:
        early = (t0s[-1] + Q_BLOCK - CMP_BLOCK) // CMP_STRIDE < half
        cmp_out = lax.cond(early, lambda _: [compressed(k, h, half) for k, h in chains],
                           lambda _: [compressed(k, h, n_cmp_rows) for k, h in chains], 0)
    else:
        cmp_out = [compressed(k, h, n_cmp_rows) for k, h in chains]
    cmp_out = dict(zip(chains, cmp_out))

    def select(k, h, imp):
        jb = lax.broadcasted_iota(jnp.int32, (n_sel, Q_BLOCK), 0)
        cur = (t0s[k] + lax.broadcasted_iota(jnp.int32, (1, Q_BLOCK), 1)) // SEL_BLOCK
        forced = (jb == 0) | (jb == cur) | (jb == cur - 1)
        imp = jnp.where(forced, imp + FORCE_BONUS, imp)
        imp = jnp.where(jb <= cur, imp, -1.0)
        bias = jnp.full((n_sel, Q_BLOCK), NEG_INF, F32)
        for _ in range(n_pick):
            m = jnp.max(imp, axis=0, keepdims=True)
            first = jnp.min(jnp.where(imp == m, jb, n_sel), axis=0, keepdims=True)
            hit = jb == first
            bias = jnp.where(hit, 0.0, bias)
            imp = jnp.where(hit, -jnp.inf, imp)
        per_unit = SEL_UNIT // SEL_BLOCK
        n_units = n_sel // per_unit
        picked = jnp.where(bias == 0.0, 1.0, 0.0).astype(BF16)
        in_unit = (lax.broadcasted_iota(jnp.int32, (n_units, n_sel), 1) // per_unit
                   == lax.broadcasted_iota(jnp.int32, (n_units, n_sel), 0))
        hits = _dot(jnp.where(in_unit, 1.0, 0.0).astype(BF16), picked)
        live = jnp.where(jnp.max(hits, axis=1, keepdims=True) > 0.5, 1.0, 0.0)
        live = jnp.broadcast_to(live, (n_units, LANES))
        earlier = (lax.broadcasted_iota(jnp.int32, (n_units, n_units), 1)
                   < lax.broadcasted_iota(jnp.int32, (n_units, n_units), 0))
        before = _dot(jnp.where(earlier, 1.0, 0.0).astype(BF16), live.astype(BF16))
        slot = lax.broadcasted_iota(jnp.int32, (n_units, LANES), 1).astype(F32)
        in_slot = jnp.where(before == slot, live, 0.0).astype(BF16)
        r8 = lax.broadcasted_iota(jnp.int32, (SUBLANES, n_units), 0)
        c8 = lax.broadcasted_iota(jnp.int32, (SUBLANES, n_units), 1)
        pick_rows = jnp.where(r8 == 0, c8, jnp.where(r8 == 1, 1, 0)).astype(F32).astype(BF16)
        ids = _dot(pick_rows, in_slot)
        n_live = (before[n_units - 1:n_units, 0:1] + live[n_units - 1:n_units, 0:1])[0, 0]
        bias = jnp.concatenate([bias] * GQA_GROUP, axis=1).astype(BF16)
        qa_scr[k, h, AUG:, :] = jnp.concatenate([bias, jnp.zeros((AUG - n_sel, QW), BF16)], axis=0)
        return ids, n_live

    def window(k, h):
        WK = WINDOW + Q_BLOCK
        w0 = pl.multiple_of(jnp.maximum(t0s[k] - WINDOW, 0), Q_BLOCK)
        sw = _dot(kw_ref[h, pl.ds(w0, WK), :], qa[k, h])
        dist = tqs[k] - (w0 + lax.broadcasted_iota(jnp.int32, (WK, QW), 0))
        in_win = (dist | (WINDOW - 1 - dist)) >= 0
        p_w = _softmax_cols(sw, in_win)
        return _dot(vwt_ref[h, :, pl.ds(w0, WK)], p_w.astype(BF16))

    lists = {c: select(*c, cmp_out[c][1]) for c in chains}
    o_win = {c: window(*c) for c in chains}

    def gathered(k, h, ids, n_slots):
        units = [ids[0, s].astype(jnp.int32) for s in range(n_slots)]
        starts = [pl.multiple_of(u * SEL_UNIT, SEL_UNIT) for u in units]
        keys = jnp.concatenate([ks_ref[h, pl.ds(st, SEL_UNIT), :] for st in starts], axis=0)
        s = _dot(keys, qa_scr[k, h])
        krow = lax.broadcasted_iota(jnp.int32, (SEL_UNIT, QW), 0)
        seen = jnp.concatenate(
            [st + krow <= jnp.where(ids[1, si] > 0.5, tqs[k], -1) for si, st in enumerate(starts)], axis=0)
        s = jnp.where(seen, s, NEG_INF)
        p = jnp.exp(s - jnp.max(s, axis=0, keepdims=True))
        vals = jnp.concatenate([vst_ref[h, :, pl.ds(st, SEL_UNIT)] for st in starts], axis=1)
        return _dot(vals, p.astype(BF16)) / jnp.maximum(jnp.sum(p, axis=0, keepdims=True), 1e-20)

    KC = SEL_KEY_CHUNK

    def swept(k, h):
        last = (t0s[k] + Q_BLOCK + KC - 1) // KC - 1

        def scores_to(ref, j):
            k0 = pl.multiple_of(j * KC, KC)
            ref[...] = _dot(ks_ref[h, pl.ds(k0, KC), :], qa_scr[k, h])

        def accumulate(carry, j, s, p_of):
            m, l, acc = carry
            k0 = pl.multiple_of(j * KC, KC)
            m_new = jnp.maximum(m, jnp.max(s, axis=0, keepdims=True))
            alpha = jnp.exp(m - m_new)
            p = p_of(jnp.exp(s - m_new))
            l = alpha * l + jnp.sum(p, axis=0, keepdims=True)
            acc = alpha * acc + _dot(vst_ref[h, :, pl.ds(k0, KC)], p.astype(BF16))
            return m_new, l, acc

        keep = lambda p: p

        def two_chunks(i, state):
            j = 2 * i
            scores_to(sb_scr, j + 1)
            state = accumulate(state, j, sa_scr[...], keep)
            scores_to(sa_scr, j + 2)
            return accumulate(state, j + 1, sb_scr[...], keep)

        def odd_chunk(state):
            state = accumulate(state, last - 1, sa_scr[...], keep)
            scores_to(sa_scr, last)
            return state

        scores_to(sa_scr, 0)
        init = (jnp.full((1, QW), NEG_INF, F32), jnp.zeros((1, QW), F32), jnp.zeros((HEAD_DIM, QW), F32))
        state = lax.fori_loop(0, last // 2, two_chunks, init)
        state = lax.cond(last % 2 == 1, odd_chunk, lambda st: st, state)
        visible = (last * KC + lax.broadcasted_iota(jnp.int32, (KC, QW), 0)) <= tqs[k]
        _, l_s, acc_s = accumulate(state, last, jnp.where(visible, sa_scr[...], NEG_INF),
                                   lambda p: jnp.where(visible, p, 0.0))
        return acc_s / jnp.maximum(l_s, 1e-20)

    few, some = SEL_SLOTS
    for k in range(ATT_TILES):
        outs = []
        for h in range(N_KV_HEADS):
            ids, n_live = lists[k, h]
            o_s = lax.cond(n_live <= few, lambda _: gathered(k, h, ids, few),
                           lambda _: lax.cond(n_live <= some, lambda _: gathered(k, h, ids, some),
                                              lambda _: swept(k, h), 0), 0)
            gk = lambda br: g_ref[h, br:br + 1, cols(k)]
            o = gk(0) * cmp_out[k, h][0] + gk(1) * o_s + gk(2) * o_win[k, h]
            for g in range(0, GQA_GROUP, 2):
                sq = jnp.concatenate([o[:, g * Q_BLOCK:(g + 1) * Q_BLOCK], o[:, (g + 1) * Q_BLOCK:(g + 2) * Q_BLOCK]],
                                     axis=0)
                outs.append(sq.T)
        o_ref[k * Q_BLOCK:(k + 1) * Q_BLOCK, :] = jnp.concatenate(outs, axis=1).astype(BF16)


def _attend(qt, ks, vst, kwn, vwt, gt, kvc_c):
    B, _, S, _ = ks.shape
    nqb = S // Q_BLOCK
    QW = GQA_GROUP * Q_BLOCK
    n_sel = S // SEL_BLOCK
    n_pick = min(SEL_TOPK, n_sel)
    nr = S // CMP_STRIDE
    n_cmp = (S - CMP_BLOCK) // CMP_STRIDE + 1
    assert n_sel % 16 == 0 and n_sel <= AUG

    cpos = np.arange(nr) * CMP_STRIDE + CMP_BLOCK - 1
    digits = np.zeros((nr, AUG - HEAD_DIM), np.float32)
    digits[:, 0] = cpos // POS_BASE
    digits[:, 1] = cpos % POS_BASE
    kc = kvc_c[:, :N_KV_HEADS]
    kc = jnp.concatenate([kc, jnp.broadcast_to(jnp.asarray(digits, BF16), kc.shape[:2] + digits.shape)], axis=-1)
    vct = kvc_c[:, N_KV_HEADS:].transpose(0, 1, 3, 2)
    slopes = 2.0 ** (-8.0 * np.arange(1, N_HEADS + 1) / N_HEADS)
    slope_t = jnp.asarray(np.repeat(slopes.reshape(N_KV_HEADS, GQA_GROUP), Q_BLOCK, axis=1)
                          .reshape(N_KV_HEADS, 1, QW), F32)
    cmp_start = np.arange(n_cmp) * CMP_STRIDE
    sel_start = np.arange(n_sel) * SEL_BLOCK
    ov = ((cmp_start[:, None] <= sel_start[None, :] + SEL_BLOCK - 1)
          & (cmp_start[:, None] + CMP_BLOCK - 1 >= sel_start[None, :])).astype(np.float32)
    ovt = np.zeros((n_sel, nr), np.float32)
    ovt[:, :n_cmp] = ov.T
    ovt = jnp.asarray(ovt, BF16)

    H = N_KV_HEADS
    per_bh = lambda r, c: pl.BlockSpec((None, H, r, c), lambda b, i: (b, 0, 0, 0))
    per_q = lambda r: pl.BlockSpec((None, H, r, ATT_TILES * QW), lambda b, i: (b, 0, 0, i))
    return pl.pallas_call(
        functools.partial(_attend_kernel, n_sel=n_sel, n_pick=n_pick, n_cmp_rows=nr),
        grid=(B, nqb // ATT_TILES),
        in_specs=[per_q(HEAD_DIM), per_bh(nr, AUG), per_bh(HEAD_DIM, nr),
                  per_bh(S, 2 * AUG), per_bh(HEAD_DIM, S), per_bh(S, AUG), per_bh(HEAD_DIM, S),
                  per_q(3),
                  pl.BlockSpec((H, 1, QW), lambda b, i: (0, 0, 0)),
                  pl.BlockSpec((n_sel, nr), lambda b, i: (0, 0))],
        out_specs=pl.BlockSpec((None, ATT_TILES * Q_BLOCK, NSA_WIDTH), lambda b, i: (b, i, 0)),
        out_shape=jax.ShapeDtypeStruct((B, S, NSA_WIDTH), BF16),
        scratch_shapes=[pltpu.VMEM((ATT_TILES, H, 2 * AUG, QW), BF16), pltpu.VMEM((SEL_KEY_CHUNK, QW), F32),
                        pltpu.VMEM((SEL_KEY_CHUNK, QW), F32)],
        compiler_params=_params(("arbitrary", "arbitrary")),
        name="attend",
    )(qt, kc, vct, ks, vst, kwn, vwt, gt, slope_t, ovt)


def _merge_kernel(x_ref, mod_ref, ys_ref, on_ref, gm_ref, wglu_ref, bglu_ref, wa_ref, wb_ref, wo_ref,
                  gain_ref, wrh_ref, wrl_ref, x1_ref, h2_ref, sc_ref):
    D = D_MODEL
    z = jax.nn.gelu(_from_chunk_major(lambda groups, lanes: ys_ref[:, groups, lanes], x_ref.shape[0]))
    glu = z * jax.nn.sigmoid(_dot(z.astype(BF16), wglu_ref[...]) + bglu_ref[...])
    ya = _dot(glu.astype(BF16), wa_ref[...])
    yb = _dot(on_ref[...], wb_ref[...])
    merged = gm_ref[:, :D].astype(F32) * ya + gm_ref[:, D:].astype(F32) * yb
    x1 = x_ref[...] + mod_ref[2:3, :] * _dot(merged.astype(BF16), wo_ref[...])
    x1_ref[...] = x1
    y = x1 * lax.rsqrt(jnp.mean(x1 * x1, axis=-1, keepdims=True) + RMS_EPS)
    h2 = (y * gain_ref[...]) * (1.0 + mod_ref[4:5, :]) + mod_ref[3:4, :]
    h2_ref[...] = h2.astype(BF16)
    hh, hl = _split(h2)
    nt = (((1,), (1,)), ((), ()))
    dg = lambda a, b: lax.dot_general(a, b, nt, preferred_element_type=F32)
    logits = dg(wrh_ref[...], hh) + (dg(wrh_ref[...], hl) + dg(wrl_ref[...], hh))
    sc_ref[...] = jax.nn.sigmoid(logits)


def _merge(x, mod, ys5, o_nsa, gm, w_glu, b_glu, w_a, w_b, w_out, gain_f, w_router):
    B, S, D = x.shape
    tm = min(ROW_TILE, S)
    wrh, wrl = _split(w_router.T)
    ws = [w_glu.astype(BF16), b_glu.reshape(1, -1), w_a.astype(BF16), w_b.astype(BF16), w_out.astype(BF16),
          gain_f, wrh, wrl]
    row = lambda w: pl.BlockSpec((None, tm, w), lambda b, i: (b, i, 0))
    full = lambda a: pl.BlockSpec(a.shape, lambda b, i: (0,) * a.ndim)
    return pl.pallas_call(
        _merge_kernel,
        grid=(B, S // tm),
        in_specs=[row(D), pl.BlockSpec((None, N_MOD, D), lambda b, i: (b, 0, 0)),
                  _s5_chunk_spec(tm), row(NSA_WIDTH), row(2 * D)] + [full(a) for a in ws],
        out_specs=[row(D), row(D), pl.BlockSpec((None, N_EXPERTS, tm), lambda b, i: (b, 0, i))],
        out_shape=[jax.ShapeDtypeStruct((B, S, D), F32), jax.ShapeDtypeStruct((B, S, D), BF16),
                   jax.ShapeDtypeStruct((B, N_EXPERTS, S), F32)],
        compiler_params=_params(("arbitrary", "arbitrary")),
        name="merge",
    )(x, mod, ys5, o_nsa, gm, *ws)


def _first_argmax_rows(v, idx, n):
    m = jnp.max(v, axis=0, keepdims=True)
    first = jnp.min(jnp.where(v == m, idx, n), axis=0, keepdims=True)
    return idx == first, m


def _route_kernel(sc_ref, bias_ref, tri_ref, rank_ref, w_ref, cnt_ref):
    E, NG = N_EXPERTS, N_EXPERT_GROUPS
    GS = E // NG
    sc = sc_ref[...]
    TM = sc.shape[1]
    sel = sc + bias_ref[...]
    i8 = lax.broadcasted_iota(jnp.int32, (GS, TM), 0)
    gscore = []
    for g in range(NG):
        blk = sel[g * GS:(g + 1) * GS, :]
        hit, m1 = _first_argmax_rows(blk, i8, GS)
        m2 = jnp.max(jnp.where(hit, -jnp.inf, blk), axis=0, keepdims=True)
        gscore.append(m1 + m2)
    gscore = jnp.concatenate(gscore, axis=0)
    ig = lax.broadcasted_iota(jnp.int32, (NG, TM), 0)
    gmask = jnp.zeros((NG, TM), F32)
    for _ in range(TOPK_EXPERT_GROUPS):
        hit, _m = _first_argmax_rows(gscore, ig, NG)
        gmask = jnp.where(hit, 1.0, gmask)
        gscore = jnp.where(hit, -jnp.inf, gscore)
    emask = jnp.concatenate([jnp.broadcast_to(gmask[g:g + 1, :], (GS, TM)) for g in range(NG)], axis=0)
    cand = jnp.where(emask > 0.5, sel, NEG_INF)
    ie = lax.broadcasted_iota(jnp.int32, (E, TM), 0)
    chosen = jnp.zeros((E, TM), F32)
    for _ in range(TOP_K):
        hit, _m = _first_argmax_rows(cand, ie, E)
        chosen = jnp.where(hit, 1.0, chosen)
        cand = jnp.where(hit, -jnp.inf, cand)
    w = chosen * sc
    w = w / jnp.sum(w, axis=0, keepdims=True) * ROUTED_SCALE
    sub = rank_ref.shape[-1]
    for q in range(TM // sub):
        cols = slice(q * sub, (q + 1) * sub)
        cb = chosen[:, cols].astype(BF16)
        prefix = _dot(cb, tri_ref[...])
        rank_ref[q] = jnp.where(chosen[:, cols] > 0.5, prefix, RANK_NONE)
        w_ref[q] = w[:, cols]
        cnt_ref[q] = _dot(cb, jnp.ones((sub, LANES), BF16))


def _route(scores_t, router_bias):
    B, E, S = scores_t.shape
    TM = min(ROUTE_TILE, S)
    per_step = min(MOE_OUTER, S) // TM
    nj = S // (TM * per_step)
    ns = B * nj * per_step
    tri = jnp.asarray(np.triu(np.ones((TM, TM), np.float32), k=1), BF16)
    tile = lambda w: pl.BlockSpec((per_step, E, w), lambda b, j: (b * nj + j, 0, 0))
    return pl.pallas_call(
        _route_kernel,
        grid=(B, nj),
        in_specs=[pl.BlockSpec((None, E, TM * per_step), lambda b, j: (b, 0, j)),
                  pl.BlockSpec((E, 1), lambda b, j: (0, 0)),
                  pl.BlockSpec((TM, TM), lambda b, j: (0, 0))],
        out_specs=[tile(TM), tile(TM), tile(LANES)],
        out_shape=[jax.ShapeDtypeStruct((ns, E, TM), F32), jax.ShapeDtypeStruct((ns, E, TM), F32),
                   jax.ShapeDtypeStruct((ns, E, LANES), F32)],
        compiler_params=_params(("arbitrary", "arbitrary")),
        name="route",
    )(scores_t, router_bias.reshape(E, 1), tri)


def _moe_kernel(cnt_ref, h_ref, x1_ref, mod_ref, rank_ref, w_ref, wg_ref, wu_ref, wd_ref,
                sg_ref, su_ref, sd_ref, o_ref, xc, yc, pc, wc, *, n_sub, tm, n_exp):
    to = pl.program_id(0)
    eb = pl.program_id(1)
    gate = mod_ref[5:6, :]

    @pl.when(eb == 0)
    def _shared():
        for s in range(n_sub):
            rows = pl.ds(s * tm, tm)
            hs = h_ref[rows, :]
            hid = jax.nn.silu(_dot(hs, sg_ref[...])) * _dot(hs, su_ref[...])
            o_ref[rows, :] = x1_ref[rows, :] + gate * _dot(hid.astype(BF16), sd_ref[...])
            xc[s] = jnp.zeros(xc.shape[1:], BF16)
            yc[s] = jnp.zeros(yc.shape[1:], BF16)
            wc[s] = jnp.zeros(wc.shape[1:], F32)

    PIECE, WIN = MOE_PIECE, MOE_WIN
    tn = (((0,), (0,)), ((), ()))
    experts = [eb * n_exp + j for j in range(n_exp)]
    cnt = [[cnt_ref[(to * n_sub + s) * N_EXPERTS + e] for e in experts] for s in range(n_sub)]
    off = []
    for s in range(n_sub):
        o = [jnp.int32(0)]
        for j in range(n_exp):
            o.append(o[-1] + ((cnt[s][j] + 15) // 16) * 16)
        off.append(o)
    n_pieces = [(off[s][n_exp] + PIECE - 1) // PIECE for s in range(n_sub)]
    fits = n_pieces[0] < MOE_ROWS // PIECE
    for s in range(1, n_sub):
        fits = jnp.logical_and(fits, n_pieces[s] < MOE_ROWS // PIECE)

    def one_hot_rows(s, base, rows_n, js):
        slot = (lax.broadcasted_iota(jnp.int32, (rows_n, tm), 0) + base).astype(F32)
        pick = jnp.zeros((rows_n, tm), F32)
        wacc = jnp.zeros((rows_n, tm), F32)
        for j in js:
            hit = (slot - off[s][j].astype(F32)) == rank_ref[s, pl.ds(experts[j], 1), :]
            pick = jnp.where(hit, 1.0, pick)
            wacc = jnp.where(hit, w_ref[s, pl.ds(experts[j], 1), :], wacc)
        return pick.astype(BF16), jnp.sum(wacc, axis=1, keepdims=True)

    def mlp(xg, j):
        hid = jax.nn.silu(_dot(xg, wg_ref[j])) * _dot(xg, wu_ref[j])
        return _dot(hid.astype(BF16), wd_ref[j])

    def piece(s, q):
        r = pl.ds(q * PIECE, PIECE)
        pick, wrow = one_hot_rows(s, q * PIECE, PIECE, range(n_exp))
        xc[s, r, :] = _dot(pick, h_ref[pl.ds(s * tm, tm), :]).astype(BF16)
        pc[s, :, r] = pick.T
        wc[s, r, :] = wrow

    def window(j, p, first):
        starts = [pl.multiple_of(jnp.minimum(off[s][j] + p * WIN, (n_pieces[s] + 1) * PIECE - WIN), 16)
                  for s in range(n_sub)]
        xg = jnp.concatenate([xc[s, pl.ds(starts[s], WIN), :] for s in range(n_sub)], axis=0)
        out = mlp(xg, j)
        for s in range(n_sub):
            r = pl.ds(starts[s], WIN)
            new = (out[s * WIN:(s + 1) * WIN] * wc[s, r, :] * gate).astype(BF16)
            if not first:
                rank = starts[s] - off[s][j] + lax.broadcasted_iota(jnp.int32, (WIN, 1), 0)
                new = jnp.where(rank < cnt[s][j], new, yc[s, r, :])
            yc[s, r, :] = new

    def scatter(s, start, n):
        r = pl.ds(start, n)
        o_ref[pl.ds(s * tm, tm), :] += _dot(pc[s, :, r], yc[s, r, :])

    @pl.when(fits)
    def _packed():
        for s in range(n_sub):
            piece(s, 0)
            piece(s, 1)
        for s in range(n_sub):
            @pl.when(n_pieces[s] > 2)
            def _third(s=s):
                piece(s, 2)
        for j in range(n_exp):
            window(j, 0, True)
        for j in range(n_exp):
            most = cnt[0][j]
            for s in range(1, n_sub):
                most = jnp.maximum(most, cnt[s][j])

            @pl.when(most > WIN)
            def _more(j=j, most=most):
                lax.fori_loop(1, (most + WIN - 1) // WIN, lambda p, c: (window(j, p, False), c)[1], 0)
        for s in range(n_sub):
            scatter(s, 0, 2 * PIECE)
        for s in range(n_sub):
            @pl.when(n_pieces[s] > 2)
            def _third_out(s=s):
                scatter(s, 2 * PIECE, PIECE)

    @pl.when(jnp.logical_not(fits))
    def _unpacked():
        for j in range(n_exp):
            for s in range(n_sub):
                rows = pl.ds(s * tm, tm)

                def block(bi, carry, j=j, s=s, rows=rows):
                    slot0 = off[s][j] + bi * PIECE
                    pick, wrow = one_hot_rows(s, slot0, PIECE, [j])
                    out = mlp(_dot(pick, h_ref[rows, :]).astype(BF16), j)
                    ow = (out * wrow * gate).astype(BF16)
                    o_ref[rows, :] += lax.dot_general(pick, ow, tn, preferred_element_type=F32)
                    return carry

                lax.fori_loop(0, (cnt[s][j] + PIECE - 1) // PIECE, block, 0)


def _moe(h2, x1, mod, rank_t, w_t, counts, w_gate, w_up, w_down, ws_gate, ws_up, ws_down, seq):
    T, D = h2.shape
    ns, E, tm = rank_t.shape
    tmo = min(MOE_OUTER, seq)
    n_sub = tmo // tm
    n_exp = MOE_EXPERTS_PER_STEP
    per_b = seq // tmo
    ws = [w_gate.astype(BF16), w_up.astype(BF16), w_down.astype(BF16)]
    sh = [ws_gate.astype(BF16), ws_up.astype(BF16), ws_down.astype(BF16)]
    tok = lambda: pl.BlockSpec((tmo, D), lambda t, e, c: (t, 0))
    sub = lambda: pl.BlockSpec((n_sub, E, tm), lambda t, e, c: (t, 0, 0))
    exp = lambda a: pl.BlockSpec((n_exp,) + a.shape[1:], lambda t, e, c: (e, 0, 0))
    full = lambda a: pl.BlockSpec(a.shape, lambda t, e, c: (0, 0))
    grid_spec = pltpu.PrefetchScalarGridSpec(
        num_scalar_prefetch=1,
        grid=(T // tmo, E // n_exp),
        in_specs=[tok(), tok(), pl.BlockSpec((None, N_MOD, D), lambda t, e, c: (t // per_b, 0, 0)),
                  sub(), sub(), exp(ws[0]), exp(ws[1]), exp(ws[2]), full(sh[0]), full(sh[1]), full(sh[2])],
        out_specs=tok(),
        scratch_shapes=[pltpu.VMEM((n_sub, MOE_ROWS, D), BF16), pltpu.VMEM((n_sub, MOE_ROWS, D), BF16),
                        pltpu.VMEM((n_sub, tm, MOE_ROWS), BF16), pltpu.VMEM((n_sub, MOE_ROWS, 1), F32)],
    )
    return pl.pallas_call(
        functools.partial(_moe_kernel, n_sub=n_sub, tm=tm, n_exp=n_exp),
        grid_spec=grid_spec,
        out_shape=jax.ShapeDtypeStruct((T, D), F32),
        compiler_params=_params(("arbitrary", "arbitrary")),
        name="moe",
    )(counts, h2, x1, mod, rank_t, w_t, *ws, *sh)


def kernel(x, c, w_ada, b_ada, norm_mix_gain, norm_ffn_gain, w_in, s5_lambda_re, s5_lambda_im, s5_log_dt, s5_b_re, s5_b_im, s5_c_re, s5_c_im, s5_d, s5_w_glu, s5_b_glu, q_norm_gain, k_norm_gain, cmp_pe, cmp_w1, cmp_b1, cmp_w2, cmp_b2, w_branch_a, w_branch_b, w_out, w_router, router_bias, w_gate, w_up, w_down, ws_gate, ws_up, ws_down):
    B, S, D = x.shape
    for l in range(w_ada.shape[0]):
        mod = _ada(c, w_ada[l], b_ada[l]).reshape(B, N_MOD, D)
        u, kvc, qt, ks, vst, kwn, vwt, gt, gm = _inproj(x, mod, norm_mix_gain[l:l + 1], w_in[l],
                                                        q_norm_gain[l], k_norm_gain[l])
        tables = _s5_tables(s5_lambda_re[l], s5_lambda_im[l], s5_log_dt[l], s5_b_re[l], s5_b_im[l],
                            s5_c_re[l], s5_c_im[l], s5_d[l])
        ys5 = _s5(u, tables)
        kvc_c = _compress(kvc, cmp_pe[l], cmp_w1[l], cmp_b1[l], cmp_w2[l], cmp_b2[l], k_norm_gain[l, 0])
        o_nsa = _attend(qt, ks, vst, kwn, vwt, gt, kvc_c)
        x1, h2, scores_t = _merge(x, mod, ys5, o_nsa, gm, s5_w_glu[l], s5_b_glu[l], w_branch_a[l],
                                  w_branch_b[l], w_out[l], norm_ffn_gain[l:l + 1], w_router[l])
        rank_t, w_t, cnt = _route(scores_t, router_bias[l])
        counts = cnt[:, :, 0].astype(jnp.int32).reshape(-1)
        x = _moe(h2.reshape(B * S, D), x1.reshape(B * S, D), mod, rank_t, w_t, counts,
                 w_gate[l], w_up[l], w_down[l], ws_gate[l], ws_up[l], ws_down[l], S).reshape(B, S, D)
    return x
```

```python
import functools

import numpy as np
import jax
import jax.numpy as jnp
from jax import lax
from jax.experimental import pallas as pl
from jax.experimental.pallas import tpu as pltpu

F32 = jnp.float32
BF16 = jnp.bfloat16

D_MODEL = 1024
S5_WIDTH = 512
S5_GROUP = 16
S5_GROUPS = S5_WIDTH // S5_GROUP
S5_STATE = 64
N_HEADS = 8
N_KV_HEADS = 2
GQA_GROUP = N_HEADS // N_KV_HEADS
HEAD_DIM = 64
NSA_WIDTH = N_HEADS * HEAD_DIM
KV_WIDTH = 2 * N_KV_HEADS * HEAD_DIM
CMP_BLOCK = 32
CMP_STRIDE = 16
CMP_HIDDEN = 256
SEL_BLOCK = 64
SEL_TOPK = 8
WINDOW = 256
Q_BLOCK = 128
FORCE_BONUS = 1e3
N_EXPERTS = 64
TOP_K = 8
N_EXPERT_GROUPS = 8
TOPK_EXPERT_GROUPS = 4
ROUTED_SCALE = 2.5
RMS_EPS = 1e-6
NEG_INF = -1e30
N_MOD = 6

LANES = 128
SUBLANES = 8
S5_CHUNK = 16
S5_CW = S5_CHUNK * S5_GROUP
ROW_TILE = 512
ATT_TILES = 2
SEL_KEY_CHUNK = 512
SEL_UNIT = 128
SEL_SLOTS = (5, 8)
POS_BASE = 64
POS_ROWS = 16
AUG = 128
ROUTE_TILE = 256
MOE_OUTER = 1024
MOE_EXPERTS_PER_STEP = 4
MOE_PIECE = 128
MOE_WIN = 48
MOE_ROWS = 512
RANK_NONE = -float(1 << 20)
VMEM_LIMIT = 56 * 1024 * 1024


def _dot(a, b):
    return jnp.dot(a, b, preferred_element_type=F32)


def _split(a):
    hi = a.astype(BF16)
    lo = (a - hi.astype(F32)).astype(BF16)
    return hi, lo


def _dot3(a, bh, bl):
    ah, al = _split(a)
    return _dot(ah, bh) + (_dot(al, bh) + _dot(ah, bl))


def _segment_transpose(x):
    assert x.shape[-2:] == (SUBLANES, LANES) and LANES // S5_GROUP == SUBLANES
    nd = x.ndim
    i = lax.broadcasted_iota(jnp.int32, x.shape, nd - 2)
    seg = lax.broadcasted_iota(jnp.int32, x.shape, nd - 1) // S5_GROUP
    out = x
    for d in range(1, SUBLANES):
        r = pltpu.roll(pltpu.roll(x, SUBLANES - d, axis=nd - 2), S5_GROUP * d, axis=nd - 1)
        out = jnp.where(seg == ((i + d) & (SUBLANES - 1)), r, out)
    return out


def _to_chunk_major(u, put):
    rows = u.shape[0]
    u3 = u.reshape(rows // SUBLANES, SUBLANES, S5_WIDTH)
    halves = S5_CHUNK // SUBLANES
    for jb in range(S5_WIDTH // LANES):
        t = _segment_transpose(u3[:, :, LANES * jb:LANES * (jb + 1)])
        t = t.reshape(rows // S5_CHUNK, halves, SUBLANES, LANES)
        for hf in range(halves):
            put(slice(SUBLANES * jb, SUBLANES * (jb + 1)), slice(LANES * hf, LANES * (hf + 1)), t[:, hf])


def _from_chunk_major(get, rows):
    halves = S5_CHUNK // SUBLANES
    cols = []
    for jb in range(S5_WIDTH // LANES):
        parts = [_segment_transpose(get(slice(SUBLANES * jb, SUBLANES * (jb + 1)), slice(LANES * hf, LANES * (hf + 1))))
                 for hf in range(halves)]
        cols.append(jnp.stack(parts, axis=1).reshape(rows, LANES))
    return jnp.concatenate(cols, axis=1)


def _params(sem):
    return pltpu.CompilerParams(dimension_semantics=sem, vmem_limit_bytes=VMEM_LIMIT)


def _ada_kernel(c_ref, w_ref, b_ref, o_ref):
    cs = jax.nn.silu(c_ref[...])
    wh, wl = _split(w_ref[...])
    o_ref[...] = _dot3(cs, wh, wl) + b_ref[...]


def _ada(c, w_ada, b_ada):
    B, D = c.shape
    return pl.pallas_call(
        _ada_kernel,
        grid=(N_MOD,),
        in_specs=[pl.BlockSpec((B, D), lambda j: (0, 0)),
                  pl.BlockSpec((D, D), lambda j: (0, j)),
                  pl.BlockSpec((1, D), lambda j: (0, j))],
        out_specs=pl.BlockSpec((B, D), lambda j: (0, j)),
        out_shape=jax.ShapeDtypeStruct((B, N_MOD * D), F32),
        compiler_params=_params(("arbitrary",)),
        name="ada",
    )(c, w_ada, b_ada.reshape(1, N_MOD * D))


def _head_norm(v, bd, gain):
    sq = v * v
    sh, sl = _split(sq)
    ms = _dot(sh, bd) + _dot(sl, bd)
    return v * lax.rsqrt(ms + RMS_EPS) * gain


def _inproj_kernel(x_ref, mod_ref, gain_ref, wm_ref, wt_ref, wgn_ref, wgm_ref, bd_ref, qg_ref, kg_ref,
                   u_ref, kvc_ref, qt_ref, ksa_ref, vst_ref, kwa_ref, vwt_ref, gt_ref, gm_ref):
    x = x_ref[...]
    shift = mod_ref[0:1, :]
    scale = mod_ref[1:2, :]
    y = x * lax.rsqrt(jnp.mean(x * x, axis=-1, keepdims=True) + RMS_EPS)
    h = (y * gain_ref[...]) * (1.0 + scale) + shift
    hb = h.astype(BF16)
    main = _dot(hb, wm_ref[...])
    tm = x.shape[0]

    def put(groups, lanes, block):
        u_ref[:, groups, lanes] = block

    _to_chunk_major(main[:, :S5_WIDTH], put)
    o = S5_WIDTH
    for part in range(KV_WIDTH // LANES):
        kvc_ref[part] = main[:, o + part * LANES:o + (part + 1) * LANES]
    o += KV_WIDTH
    kw = N_KV_HEADS * HEAD_DIM
    bd = bd_ref[...]
    ks = _head_norm(main[:, o:o + kw], bd[:kw, :kw], kg_ref[1:2, :]).astype(BF16)
    kwn = _head_norm(main[:, o + kw:o + 2 * kw], bd[:kw, :kw], kg_ref[2:3, :]).astype(BF16)
    pos = pl.program_id(1) * tm + lax.broadcasted_iota(jnp.int32, (tm, 1), 0)
    lane = lax.broadcasted_iota(jnp.int32, (tm, AUG), 1)
    digits = jnp.where(lane == 0, pos // POS_BASE, jnp.where(lane == 1, pos % POS_BASE, 0))
    digits = digits[:, :AUG - HEAD_DIM].astype(F32).astype(BF16)
    onehot = jnp.where(lane == pos // SEL_BLOCK, 1.0, 0.0).astype(BF16)
    for hh in range(N_KV_HEADS):
        ksa_ref[hh, :, 0:HEAD_DIM] = ks[:, hh * HEAD_DIM:(hh + 1) * HEAD_DIM]
        ksa_ref[hh, :, HEAD_DIM:AUG] = digits
        ksa_ref[hh, :, AUG:] = onehot
        kwa_ref[hh, :, 0:HEAD_DIM] = kwn[:, hh * HEAD_DIM:(hh + 1) * HEAD_DIM]
        kwa_ref[hh, :, HEAD_DIM:] = digits

    nt = (((1,), (1,)), ((), ()))
    tt = lax.dot_general(wt_ref[...], hb, nt, preferred_element_type=F32)
    qt = tt[:NSA_WIDTH]
    sq = qt * qt
    sh, sl = _split(sq)
    qn = (qt * lax.rsqrt(_dot(bd, sh) + _dot(bd, sl) + RMS_EPS) * qg_ref[...] * (HEAD_DIM ** -0.5)).astype(BF16)
    QW = GQA_GROUP * Q_BLOCK
    for qb in range(tm // Q_BLOCK):
        for hd in range(N_HEADS):
            hh, g = divmod(hd, GQA_GROUP)
            qt_ref[hh, :, qb * QW + g * Q_BLOCK:qb * QW + (g + 1) * Q_BLOCK] = (
                qn[hd * HEAD_DIM:(hd + 1) * HEAD_DIM, qb * Q_BLOCK:(qb + 1) * Q_BLOCK])
    vst_ref[...] = tt[NSA_WIDTH:NSA_WIDTH + kw].reshape(N_KV_HEADS, HEAD_DIM, tm).astype(BF16)
    vwt_ref[...] = tt[NSA_WIDTH + kw:].reshape(N_KV_HEADS, HEAD_DIM, tm).astype(BF16)
    gn = jax.nn.sigmoid(lax.dot_general(wgn_ref[...], hb, nt, preferred_element_type=F32))
    for qb in range(tm // Q_BLOCK):
        for br in range(3):
            for hd in range(N_HEADS):
                hh, g = divmod(hd, GQA_GROUP)
                r = br * N_HEADS + hd
                gt_ref[hh, br:br + 1, qb * QW + g * Q_BLOCK:qb * QW + (g + 1) * Q_BLOCK] = (
                    gn[r:r + 1, qb * Q_BLOCK:(qb + 1) * Q_BLOCK])
    gm_ref[...] = jax.nn.sigmoid(_dot(hb, wgm_ref[...])).astype(BF16)


def _s5_chunk_spec(tm):
    assert S5_CHUNK == 16 and S5_GROUP == 16, "the segment transposes assume 16 steps x 16 channels"
    return pl.BlockSpec((tm // S5_CHUNK, None, S5_GROUPS, S5_CW), lambda b, i: (i, b, 0, 0))


def _inproj(x, mod, gain, w_in, q_gain, k_gain):
    B, S, D = x.shape
    tm = min(ROW_TILE, S)
    assert S // SEL_BLOCK <= AUG and tm % Q_BLOCK == 0
    kw = N_KV_HEADS * HEAD_DIM
    cols = np.cumsum((0,) + (S5_WIDTH, NSA_WIDTH, KV_WIDTH, KV_WIDTH, KV_WIDTH, 3 * N_HEADS, 2 * D))
    c_u, c_q, c_kvc, c_kvs, c_kvw, c_gn, c_gm = cols[:7]
    sl = lambda a, n: w_in[:, a:a + n]
    wm = jnp.concatenate([sl(c_u, S5_WIDTH), sl(c_kvc, KV_WIDTH), sl(c_kvs, kw), sl(c_kvw, kw)], axis=1).astype(BF16)
    wt = jnp.concatenate([sl(c_q, NSA_WIDTH), sl(c_kvs + kw, kw), sl(c_kvw + kw, kw)], axis=1).T.astype(BF16)
    wgn = sl(c_gn, 3 * N_HEADS).T.astype(BF16)
    wgm = sl(c_gm, 2 * D).astype(BF16)
    seg = np.arange(NSA_WIDTH) // HEAD_DIM
    bd = jnp.asarray((seg[:, None] == seg[None, :]).astype(np.float32) / HEAD_DIM, BF16)
    qg = jnp.tile(q_gain, N_HEADS).reshape(NSA_WIDTH, 1)
    kg = jnp.tile(k_gain, (1, N_KV_HEADS))
    nq = tm // Q_BLOCK * GQA_GROUP * Q_BLOCK
    row = lambda w: pl.BlockSpec((None, tm, w), lambda b, i: (b, i, 0))
    full = lambda a: pl.BlockSpec(a.shape, lambda b, i: (0,) * a.ndim)
    rows4 = lambda w: pl.BlockSpec((None, N_KV_HEADS, tm, w), lambda b, i: (b, 0, i, 0))
    cols4 = lambda r, w: pl.BlockSpec((None, N_KV_HEADS, r, w), lambda b, i: (b, 0, 0, i))
    nqt = S // Q_BLOCK * GQA_GROUP * Q_BLOCK
    return pl.pallas_call(
        _inproj_kernel,
        grid=(B, S // tm),
        in_specs=[row(D), pl.BlockSpec((None, N_MOD, D), lambda b, i: (b, 0, 0)),
                  full(gain), full(wm), full(wt), full(wgn), full(wgm), full(bd), full(qg), full(kg)],
        out_specs=[_s5_chunk_spec(tm), pl.BlockSpec((None, KV_WIDTH // LANES, tm, LANES), lambda b, i: (b, 0, i, 0)),
                   cols4(HEAD_DIM, nq), rows4(2 * AUG), cols4(HEAD_DIM, tm),
                   rows4(AUG), cols4(HEAD_DIM, tm), cols4(3, nq), row(2 * D)],
        out_shape=[jax.ShapeDtypeStruct((S // S5_CHUNK, B, S5_GROUPS, S5_CW), F32),
                   jax.ShapeDtypeStruct((B, KV_WIDTH // LANES, S, LANES), F32),
                   jax.ShapeDtypeStruct((B, N_KV_HEADS, HEAD_DIM, nqt), BF16),
                   jax.ShapeDtypeStruct((B, N_KV_HEADS, S, 2 * AUG), BF16),
                   jax.ShapeDtypeStruct((B, N_KV_HEADS, HEAD_DIM, S), BF16),
                   jax.ShapeDtypeStruct((B, N_KV_HEADS, S, AUG), BF16),
                   jax.ShapeDtypeStruct((B, N_KV_HEADS, HEAD_DIM, S), BF16),
                   jax.ShapeDtypeStruct((B, N_KV_HEADS, 3, nqt), F32),
                   jax.ShapeDtypeStruct((B, S, 2 * D), BF16)],
        compiler_params=_params(("arbitrary", "arbitrary")),
        name="inproj",
    )(x, mod, gain, wm, wt, wgn, wgm, bd, qg, kg)


def _s5_tables(lam_re, lam_im, log_dt, b_re, b_im, c_re, c_im, d_skip):
    L, G = S5_CHUNK, S5_GROUPS
    lr, li = lam_re.astype(F32), lam_im.astype(F32)
    dt = jnp.exp(log_dt.astype(F32))[:, None]
    mag = jnp.exp(lr * dt)
    abar_re, abar_im = mag * jnp.cos(li * dt), mag * jnp.sin(li * dt)
    num_re, num_im = abar_re - 1.0, abar_im
    den = lr * lr + li * li
    coef_re = (num_re * lr + num_im * li) / den
    coef_im = (num_im * lr - num_re * li) / den
    br, bi = b_re.astype(F32), b_im.astype(F32)
    bbar_re = coef_re[..., None] * br - coef_im[..., None] * bi
    bbar_im = coef_re[..., None] * bi + coef_im[..., None] * br
    k = jnp.arange(L + 1, dtype=F32)[:, None]
    pmag = jnp.exp((lr * dt)[:, None, :] * k)
    ang = (li * dt)[:, None, :] * k
    pw = jnp.stack([jnp.tile(pmag * jnp.cos(ang), (1, 1, 2)), jnp.tile(pmag * jnp.sin(ang), (1, 1, 2))], axis=1)
    brt, bit = bbar_re.transpose(0, 2, 1), bbar_im.transpose(0, 2, 1)
    ba = jnp.concatenate([brt, bit], axis=2)
    bb = jnp.concatenate([-bit, brt], axis=2)
    cr, ci = c_re.astype(F32), c_im.astype(F32)
    ca = jnp.concatenate([cr, -ci], axis=2)
    cb = jnp.concatenate([-ci, -cr], axis=2)
    dv = jnp.tile(d_skip.astype(F32), (1, L)).reshape(G, 1, S5_CW)
    return pw, ba, bb, ca, cb, dv


def _s5_kernel(u_ref, pw_ref, ba_ref, bb_ref, ca_ref, cb_ref, dv_ref,
               y_ref, vr_scr, vi_scr, xr_scr, xi_scr, *, n_chunks, bsz):
    L, P, N = S5_CHUNK, S5_GROUP, S5_STATE
    nt = (((1,), (1,)), ((), ()))
    dot_nt = lambda a, b: lax.dot_general(a, b, nt, preferred_element_type=F32)
    mix = lambda a, b, k: a * pw_ref[0, k:k + 1, :] + b * pw_ref[1, k:k + 1, :]
    ba, bb, ca, cb = ba_ref[...], bb_ref[...], ca_ref[...], cb_ref[...]
    ws = jnp.concatenate([mix(ba, bb, L - 1 - s) for s in range(L)], axis=0)
    wot = jnp.concatenate([mix(ca, cb, k) for k in range(L + 1)], axis=0)
    wsh, wsl = _split(ws)
    wth, wtl = _split(wot)
    bah, bal = _split(ba)
    kern = dot_nt(bah, wth[:L * P]) + (dot_nt(bal, wth[:L * P]) + dot_nt(bah, wtl[:L * P]))
    lane = lax.broadcasted_iota(jnp.int32, kern.shape, 1)
    mt = jnp.concatenate([kern] + [jnp.where(lane >= P * s, pltpu.roll(kern, P * s, axis=1), 0.0)
                                   for s in range(1, L)], axis=0)
    mth, mtl = _split(mt)

    u = u_ref[...]
    uh, ul = _split(u)
    y = _dot(uh, mth) + (_dot(ul, mth) + _dot(uh, mtl))
    v = _dot(uh, wsh) + (_dot(ul, wsh) + _dot(uh, wsl))
    vr_scr[...] = v[:, :N]
    vi_scr[...] = v[:, N:]
    a_r = pw_ref[0, L:L + 1, :N]
    a_i = pw_ref[1, L:L + 1, :N]

    def step(c, x):
        xr, xi = x
        r = pl.ds(pl.multiple_of(c * bsz, bsz), bsz)
        xr_scr[r, :] = xr
        xi_scr[r, :] = xi
        return (a_r * xr - a_i * xi + vr_scr[r, :], a_r * xi + a_i * xr + vi_scr[r, :])

    zero = jnp.zeros((bsz, N), F32)
    lax.fori_loop(0, n_chunks, step, (zero, zero), unroll=8)
    xh, xl = _split(jnp.concatenate([xr_scr[...], xi_scr[...]], axis=1))
    y = y + (dot_nt(xh, wth[P:]) + (dot_nt(xl, wth[P:]) + dot_nt(xh, wtl[P:])))
    y_ref[...] = y + dv_ref[...] * u


def _s5(ug, tables):
    nc, B, G, cw = ug.shape
    N = S5_STATE
    grp = lambda a: pl.BlockSpec((None,) + a.shape[1:], lambda g: (g,) + (0,) * (a.ndim - 1))
    ugt = ug.reshape(nc * B, G, cw).transpose(1, 0, 2)
    y = pl.pallas_call(
        functools.partial(_s5_kernel, n_chunks=nc, bsz=B),
        grid=(G,),
        in_specs=[grp(ugt)] + [grp(t) for t in tables],
        out_specs=grp(ugt),
        out_shape=jax.ShapeDtypeStruct(ugt.shape, F32),
        scratch_shapes=[pltpu.VMEM((nc * B, N), F32)] * 4,
        compiler_params=_params(("arbitrary",)),
        name="s5",
    )(ugt, *tables)
    return y.transpose(1, 0, 2).reshape(nc, B, G, cw)


def _compress_kernel(x_ref, pe_ref, w1_ref, b1_ref, w2_ref, b2_ref, kg_ref, o_ref, *, n_rows):
    half = CMP_STRIDE * HEAD_DIM
    for j in range(2 * N_KV_HEADS):
        kv, hd = divmod(j, N_KV_HEADS)
        x = jnp.concatenate(
            [x_ref[kv, pl.ds(t, n_rows, stride=CMP_STRIDE), :][:, hd * HEAD_DIM:(hd + 1) * HEAD_DIM]
             for t in range(CMP_STRIDE)], axis=1)
        a = _dot((x + pe_ref[kv, 0:1, :]).astype(BF16), w1_ref[kv, :half, :])
        b = _dot((x + pe_ref[kv, 1:2, :]).astype(BF16), w1_ref[kv, half:, :])
        hid = jax.nn.gelu(a + pltpu.roll(b, n_rows - 1, axis=0) + b1_ref[kv])
        out = _dot(hid.astype(BF16), w2_ref[kv]) + b2_ref[kv]
        if kv == 0:
            out = out * lax.rsqrt(jnp.mean(out * out, axis=-1, keepdims=True) + RMS_EPS) * kg_ref[...]
        o_ref[j] = out.astype(BF16)


def _compress(kv_c, cmp_pe, cmp_w1, cmp_b1, cmp_w2, cmp_b2, k_gain0):
    B, _, S, _ = kv_c.shape
    nr = S // CMP_STRIDE
    half = CMP_STRIDE * HEAD_DIM
    nj = 2 * N_KV_HEADS
    ws = [cmp_pe.reshape(2, 2, half), cmp_w1.astype(BF16), cmp_b1.reshape(2, 1, CMP_HIDDEN), cmp_w2.astype(BF16),
          cmp_b2.reshape(2, 1, HEAD_DIM), k_gain0.reshape(1, HEAD_DIM)]
    full = lambda a: pl.BlockSpec(a.shape, lambda b: (0,) * a.ndim)
    return pl.pallas_call(
        functools.partial(_compress_kernel, n_rows=nr),
        grid=(B,),
        in_specs=[pl.BlockSpec((None,) + kv_c.shape[1:], lambda b: (b, 0, 0, 0))] + [full(a) for a in ws],
        out_specs=pl.BlockSpec((None, nj, nr, HEAD_DIM), lambda b: (b, 0, 0, 0)),
        out_shape=jax.ShapeDtypeStruct((B, nj, nr, HEAD_DIM), BF16),
        compiler_params=_params(("arbitrary",)),
        name="compress",
    )(kv_c, *ws)


def _softmax_cols(s, mask):
    s = jnp.where(mask, s, NEG_INF)
    m = jnp.max(s, axis=0, keepdims=True)
    p = jnp.where(mask, jnp.exp(s - m), 0.0)
    return p / jnp.maximum(jnp.sum(p, axis=0, keepdims=True), 1e-20)


def _attend_kernel(qt_ref, kc_ref, vct_ref, ks_ref, vst_ref, kw_ref, vwt_ref, g_ref, slope_ref, ovt_ref,
                   o_ref, qa_scr, sa_scr, sb_scr, *, n_sel, n_pick, n_cmp_rows):
    QW = GQA_GROUP * Q_BLOCK
    chains = [(k, h) for k in range(ATT_TILES) for h in range(N_KV_HEADS)]
    t0s = [(pl.program_id(1) * ATT_TILES + k) * Q_BLOCK for k in range(ATT_TILES)]
    tqs = [t0 + (lax.broadcasted_iota(jnp.int32, (1, QW), 1) & (Q_BLOCK - 1)) for t0 in t0s]
    cols = lambda k: slice(k * QW, (k + 1) * QW)

    r = lax.broadcasted_iota(jnp.int32, (POS_ROWS, QW), 0)
    qa = {}
    for k, h in chains:
        slope = slope_ref[h]
        qa_scr[k, h, 0:HEAD_DIM, :] = qt_ref[h, :, cols(k)]
        qa_scr[k, h, HEAD_DIM:HEAD_DIM + POS_ROWS, :] = jnp.where(
            r == 0, slope * POS_BASE, jnp.where(r == 1, slope, 0.0)).astype(BF16)
        qa_scr[k, h, HEAD_DIM + POS_ROWS:AUG, :] = jnp.zeros((AUG - HEAD_DIM - POS_ROWS, QW), BF16)
        qa[k, h] = qa_scr[k, h, 0:AUG, :]

    def compressed(k, h, rows):
        sc = _dot(kc_ref[h, 0:rows, :], qa[k, h])
        cpos = lax.broadcasted_iota(jnp.int32, (rows, QW), 0) * CMP_STRIDE + (CMP_BLOCK - 1)
        p_c = _softmax_cols(sc, cpos <= tqs[k])
        o_c = _dot(vct_ref[h, :, 0:rows], p_c.astype(BF16))
        psum = p_c[:, 0:Q_BLOCK]
        for g in range(1, GQA_GROUP):
            psum = psum + p_c[:, g * Q_BLOCK:(g + 1) * Q_BLOCK]
        ph, pl_ = _split(psum)
        return o_c, _dot(ovt_ref[:, 0:rows], ph) + _dot(ovt_ref[:, 0:rows], pl_)

    half = n_cmp_rows // 2
    if half % LANES == 0:
        early = (t0s[-1] + Q_BLOCK - CMP_BLOCK) // CMP_STRIDE < half
        cmp_out = lax.cond(early, lambda _: [compressed(k, h, half) for k, h in chains],
                           lambda _: [compressed(k, h, n_cmp_rows) for k, h in chains], 0)
    else:
        cmp_out = [compressed(k, h, n_cmp_rows) for k, h in chains]
    cmp_out = dict(zip(chains, cmp_out))

    def select(k, h, imp):
        jb = lax.broadcasted_iota(jnp.int32, (n_sel, Q_BLOCK), 0)
        cur = (t0s[k] + lax.broadcasted_iota(jnp.int32, (1, Q_BLOCK), 1)) // SEL_BLOCK
        forced = (jb == 0) | (jb == cur) | (jb == cur - 1)
        imp = jnp.where(forced, imp + FORCE_BONUS, imp)
        imp = jnp.where(jb <= cur, imp, -1.0)
        bias = jnp.full((n_sel, Q_BLOCK), NEG_INF, F32)
        for _ in range(n_pick):
            m = jnp.max(imp, axis=0, keepdims=True)
            first = jnp.min(jnp.where(imp == m, jb, n_sel), axis=0, keepdims=True)
            hit = jb == first
            bias = jnp.where(hit, 0.0, bias)
            imp = jnp.where(hit, -jnp.inf, imp)
        per_unit = SEL_UNIT // SEL_BLOCK
        n_units = n_sel // per_unit
        picked = jnp.where(bias == 0.0, 1.0, 0.0).astype(BF16)
        in_unit = (lax.broadcasted_iota(jnp.int32, (n_units, n_sel), 1) // per_unit
                   == lax.broadcasted_iota(jnp.int32, (n_units, n_sel), 0))
        hits = _dot(jnp.where(in_unit, 1.0, 0.0).astype(BF16), picked)
        live = jnp.where(jnp.max(hits, axis=1, keepdims=True) > 0.5, 1.0, 0.0)
        live = jnp.broadcast_to(live, (n_units, LANES))
        earlier = (lax.broadcasted_iota(jnp.int32, (n_units, n_units), 1)
                   < lax.broadcasted_iota(jnp.int32, (n_units, n_units), 0))
        before = _dot(jnp.where(earlier, 1.0, 0.0).astype(BF16), live.astype(BF16))
        slot = lax.broadcasted_iota(jnp.int32, (n_units, LANES), 1).astype(F32)
        in_slot = jnp.where(before == slot, live, 0.0).astype(BF16)
        r8 = lax.broadcasted_iota(jnp.int32, (SUBLANES, n_units), 0)
        c8 = lax.broadcasted_iota(jnp.int32, (SUBLANES, n_units), 1)
        pick_rows = jnp.where(r8 == 0, c8, jnp.where(r8 == 1, 1, 0)).astype(F32).astype(BF16)
        ids = _dot(pick_rows, in_slot)
        n_live = (before[n_units - 1:n_units, 0:1] + live[n_units - 1:n_units, 0:1])[0, 0]
        bias = jnp.concatenate([bias] * GQA_GROUP, axis=1).astype(BF16)
        qa_scr[k, h, AUG:, :] = jnp.concatenate([bias, jnp.zeros((AUG - n_sel, QW), BF16)], axis=0)
        return ids, n_live

    def window(k, h):
        WK = WINDOW + Q_BLOCK
        w0 = pl.multiple_of(jnp.maximum(t0s[k] - WINDOW, 0), Q_BLOCK)
        sw = _dot(kw_ref[h, pl.ds(w0, WK), :], qa[k, h])
        dist = tqs[k] - (w0 + lax.broadcasted_iota(jnp.int32, (WK, QW), 0))
        in_win = (dist | (WINDOW - 1 - dist)) >= 0
        p_w = _softmax_cols(sw, in_win)
        return _dot(vwt_ref[h, :, pl.ds(w0, WK)], p_w.astype(BF16))

    lists = {c: select(*c, cmp_out[c][1]) for c in chains}
    o_win = {c: window(*c) for c in chains}

    def gathered(k, h, ids, n_slots):
        units = [ids[0, s].astype(jnp.int32) for s in range(n_slots)]
        starts = [pl.multiple_of(u * SEL_UNIT, SEL_UNIT) for u in units]
        keys = jnp.concatenate([ks_ref[h, pl.ds(st, SEL_UNIT), :] for st in starts], axis=0)
        s = _dot(keys, qa_scr[k, h])
        krow = lax.broadcasted_iota(jnp.int32, (SEL_UNIT, QW), 0)
        seen = jnp.concatenate(
            [st + krow <= jnp.where(ids[1, si] > 0.5, tqs[k], -1) for si, st in enumerate(starts)], axis=0)
        s = jnp.where(seen, s, NEG_INF)
        p = jnp.exp(s - jnp.max(s, axis=0, keepdims=True))
        vals = jnp.concatenate([vst_ref[h, :, pl.ds(st, SEL_UNIT)] for st in starts], axis=1)
        return _dot(vals, p.astype(BF16)) / jnp.maximum(jnp.sum(p, axis=0, keepdims=True), 1e-20)

    KC = SEL_KEY_CHUNK

    def swept(k, h):
        last = (t0s[k] + Q_BLOCK + KC - 1) // KC - 1

        def scores_to(ref, j):
            k0 = pl.multiple_of(j * KC, KC)
            ref[...] = _dot(ks_ref[h, pl.ds(k0, KC), :], qa_scr[k, h])

        def accumulate(carry, j, s, p_of):
            m, l, acc = carry
            k0 = pl.multiple_of(j * KC, KC)
            m_new = jnp.maximum(m, jnp.max(s, axis=0, keepdims=True))
            alpha = jnp.exp(m - m_new)
            p = p_of(jnp.exp(s - m_new))
            l = alpha * l + jnp.sum(p, axis=0, keepdims=True)
            acc = alpha * acc + _dot(vst_ref[h, :, pl.ds(k0, KC)], p.astype(BF16))
            return m_new, l, acc

        keep = lambda p: p

        def two_chunks(i, state):
            j = 2 * i
            scores_to(sb_scr, j + 1)
            state = accumulate(state, j, sa_scr[...], keep)
            scores_to(sa_scr, j + 2)
            return accumulate(state, j + 1, sb_scr[...], keep)

        def odd_chunk(state):
            state = accumulate(state, last - 1, sa_scr[...], keep)
            scores_to(sa_scr, last)
            return state

        scores_to(sa_scr, 0)
        init = (jnp.full((1, QW), NEG_INF, F32), jnp.zeros((1, QW), F32), jnp.zeros((HEAD_DIM, QW), F32))
        state = lax.fori_loop(0, last // 2, two_chunks, init)
        state = lax.cond(last % 2 == 1, odd_chunk, lambda st: st, state)
        visible = (last * KC + lax.broadcasted_iota(jnp.int32, (KC, QW), 0)) <= tqs[k]
        _, l_s, acc_s = accumulate(state, last, jnp.where(visible, sa_scr[...], NEG_INF),
                                   lambda p: jnp.where(visible, p, 0.0))
        return acc_s / jnp.maximum(l_s, 1e-20)

    few, some = SEL_SLOTS
    for k in range(ATT_TILES):
        outs = []
        for h in range(N_KV_HEADS):
            ids, n_live = lists[k, h]
            o_s = lax.cond(n_live <= few, lambda _: gathered(k, h, ids, few),
                           lambda _: lax.cond(n_live <= some, lambda _: gathered(k, h, ids, some),
                                              lambda _: swept(k, h), 0), 0)
            gk = lambda br: g_ref[h, br:br + 1, cols(k)]
            o = gk(0) * cmp_out[k, h][0] + gk(1) * o_s + gk(2) * o_win[k, h]
            for g in range(0, GQA_GROUP, 2):
                sq = jnp.concatenate([o[:, g * Q_BLOCK:(g + 1) * Q_BLOCK], o[:, (g + 1) * Q_BLOCK:(g + 2) * Q_BLOCK]],
                                     axis=0)
                outs.append(sq.T)
        o_ref[k * Q_BLOCK:(k + 1) * Q_BLOCK, :] = jnp.concatenate(outs, axis=1).astype(BF16)


def _attend(qt, ks, vst, kwn, vwt, gt, kvc_c):
    B, _, S, _ = ks.shape
    nqb = S // Q_BLOCK
    QW = GQA_GROUP * Q_BLOCK
    n_sel = S // SEL_BLOCK
    n_pick = min(SEL_TOPK, n_sel)
    nr = S // CMP_STRIDE
    n_cmp = (S - CMP_BLOCK) // CMP_STRIDE + 1
    assert n_sel % 16 == 0 and n_sel <= AUG

    cpos = np.arange(nr) * CMP_STRIDE + CMP_BLOCK - 1
    digits = np.zeros((nr, AUG - HEAD_DIM), np.float32)
    digits[:, 0] = cpos // POS_BASE
    digits[:, 1] = cpos % POS_BASE
    kc = kvc_c[:, :N_KV_HEADS]
    kc = jnp.concatenate([kc, jnp.broadcast_to(jnp.asarray(digits, BF16), kc.shape[:2] + digits.shape)], axis=-1)
    vct = kvc_c[:, N_KV_HEADS:].transpose(0, 1, 3, 2)
    slopes = 2.0 ** (-8.0 * np.arange(1, N_HEADS + 1) / N_HEADS)
    slope_t = jnp.asarray(np.repeat(slopes.reshape(N_KV_HEADS, GQA_GROUP), Q_BLOCK, axis=1)
                          .reshape(N_KV_HEADS, 1, QW), F32)
    cmp_start = np.arange(n_cmp) * CMP_STRIDE
    sel_start = np.arange(n_sel) * SEL_BLOCK
    ov = ((cmp_start[:, None] <= sel_start[None, :] + SEL_BLOCK - 1)
          & (cmp_start[:, None] + CMP_BLOCK - 1 >= sel_start[None, :])).astype(np.float32)
    ovt = np.zeros((n_sel, nr), np.float32)
    ovt[:, :n_cmp] = ov.T
    ovt = jnp.asarray(ovt, BF16)

    H = N_KV_HEADS
    per_bh = lambda r, c: pl.BlockSpec((None, H, r, c), lambda b, i: (b, 0, 0, 0))
    per_q = lambda r: pl.BlockSpec((None, H, r, ATT_TILES * QW), lambda b, i: (b, 0, 0, i))
    return pl.pallas_call(
        functools.partial(_attend_kernel, n_sel=n_sel, n_pick=n_pick, n_cmp_rows=nr),
        grid=(B, nqb // ATT_TILES),
        in_specs=[per_q(HEAD_DIM), per_bh(nr, AUG), per_bh(HEAD_DIM, nr),
                  per_bh(S, 2 * AUG), per_bh(HEAD_DIM, S), per_bh(S, AUG), per_bh(HEAD_DIM, S),
                  per_q(3),
                  pl.BlockSpec((H, 1, QW), lambda b, i: (0, 0, 0)),
                  pl.BlockSpec((n_sel, nr), lambda b, i: (0, 0))],
        out_specs=pl.BlockSpec((None, ATT_TILES * Q_BLOCK, NSA_WIDTH), lambda b, i: (b, i, 0)),
        out_shape=jax.ShapeDtypeStruct((B, S, NSA_WIDTH), BF16),
        scratch_shapes=[pltpu.VMEM((ATT_TILES, H, 2 * AUG, QW), BF16), pltpu.VMEM((SEL_KEY_CHUNK, QW), F32),
                        pltpu.VMEM((SEL_KEY_CHUNK, QW), F32)],
        compiler_params=_params(("arbitrary", "arbitrary")),
        name="attend",
    )(qt, kc, vct, ks, vst, kwn, vwt, gt, slope_t, ovt)


def _merge_kernel(x_ref, mod_ref, ys_ref, on_ref, gm_ref, wglu_ref, bglu_ref, wa_ref, wb_ref, wo_ref,
                  gain_ref, wrh_ref, wrl_ref, x1_ref, h2_ref, sc_ref):
    D = D_MODEL
    z = jax.nn.gelu(_from_chunk_major(lambda groups, lanes: ys_ref[:, groups, lanes], x_ref.shape[0]))
    glu = z * jax.nn.sigmoid(_dot(z.astype(BF16), wglu_ref[...]) + bglu_ref[...])
    ya = _dot(glu.astype(BF16), wa_ref[...])
    yb = _dot(on_ref[...], wb_ref[...])
    merged = gm_ref[:, :D].astype(F32) * ya + gm_ref[:, D:].astype(F32) * yb
    x1 = x_ref[...] + mod_ref[2:3, :] * _dot(merged.astype(BF16), wo_ref[...])
    x1_ref[...] = x1
    y = x1 * lax.rsqrt(jnp.mean(x1 * x1, axis=-1, keepdims=True) + RMS_EPS)
    h2 = (y * gain_ref[...]) * (1.0 + mod_ref[4:5, :]) + mod_ref[3:4, :]
    h2_ref[...] = h2.astype(BF16)
    hh, hl = _split(h2)
    nt = (((1,), (1,)), ((), ()))
    dg = lambda a, b: lax.dot_general(a, b, nt, preferred_element_type=F32)
    logits = dg(wrh_ref[...], hh) + (dg(wrh_ref[...], hl) + dg(wrl_ref[...], hh))
    sc_ref[...] = jax.nn.sigmoid(logits)


def _merge(x, mod, ys5, o_nsa, gm, w_glu, b_glu, w_a, w_b, w_out, gain_f, w_router):
    B, S, D = x.shape
    tm = min(ROW_TILE, S)
    wrh, wrl = _split(w_router.T)
    ws = [w_glu.astype(BF16), b_glu.reshape(1, -1), w_a.astype(BF16), w_b.astype(BF16), w_out.astype(BF16),
          gain_f, wrh, wrl]
    row = lambda w: pl.BlockSpec((None, tm, w), lambda b, i: (b, i, 0))
    full = lambda a: pl.BlockSpec(a.shape, lambda b, i: (0,) * a.ndim)
    return pl.pallas_call(
        _merge_kernel,
        grid=(B, S // tm),
        in_specs=[row(D), pl.BlockSpec((None, N_MOD, D), lambda b, i: (b, 0, 0)),
                  _s5_chunk_spec(tm), row(NSA_WIDTH), row(2 * D)] + [full(a) for a in ws],
        out_specs=[row(D), row(D), pl.BlockSpec((None, N_EXPERTS, tm), lambda b, i: (b, 0, i))],
        out_shape=[jax.ShapeDtypeStruct((B, S, D), F32), jax.ShapeDtypeStruct((B, S, D), BF16),
                   jax.ShapeDtypeStruct((B, N_EXPERTS, S), F32)],
        compiler_params=_params(("arbitrary", "arbitrary")),
        name="merge",
    )(x, mod, ys5, o_nsa, gm, *ws)


def _first_argmax_rows(v, idx, n):
    m = jnp.max(v, axis=0, keepdims=True)
    first = jnp.min(jnp.where(v == m, idx, n), axis=0, keepdims=True)
    return idx == first, m


def _route_kernel(sc_ref, bias_ref, tri_ref, rank_ref, w_ref, cnt_ref):
    E, NG = N_EXPERTS, N_EXPERT_GROUPS
    GS = E // NG
    sc = sc_ref[...]
    TM = sc.shape[1]
    sel = sc + bias_ref[...]
    i8 = lax.broadcasted_iota(jnp.int32, (GS, TM), 0)
    gscore = []
    for g in range(NG):
        blk = sel[g * GS:(g + 1) * GS, :]
        hit, m1 = _first_argmax_rows(blk, i8, GS)
        m2 = jnp.max(jnp.where(hit, -jnp.inf, blk), axis=0, keepdims=True)
        gscore.append(m1 + m2)
    gscore = jnp.concatenate(gscore, axis=0)
    ig = lax.broadcasted_iota(jnp.int32, (NG, TM), 0)
    gmask = jnp.zeros((NG, TM), F32)
    for _ in range(TOPK_EXPERT_GROUPS):
        hit, _m = _first_argmax_rows(gscore, ig, NG)
        gmask = jnp.where(hit, 1.0, gmask)
        gscore = jnp.where(hit, -jnp.inf, gscore)
    emask = jnp.concatenate([jnp.broadcast_to(gmask[g:g + 1, :], (GS, TM)) for g in range(NG)], axis=0)
    cand = jnp.where(emask > 0.5, sel, NEG_INF)
    ie = lax.broadcasted_iota(jnp.int32, (E, TM), 0)
    chosen = jnp.zeros((E, TM), F32)
    for _ in range(TOP_K):
        hit, _m = _first_argmax_rows(cand, ie, E)
        chosen = jnp.where(hit, 1.0, chosen)
        cand = jnp.where(hit, -jnp.inf, cand)
    w = chosen * sc
    w = w / jnp.sum(w, axis=0, keepdims=True) * ROUTED_SCALE
    sub = rank_ref.shape[-1]
    for q in range(TM // sub):
        cols = slice(q * sub, (q + 1) * sub)
        cb = chosen[:, cols].astype(BF16)
        prefix = _dot(cb, tri_ref[...])
        rank_ref[q] = jnp.where(chosen[:, cols] > 0.5, prefix, RANK_NONE)
        w_ref[q] = w[:, cols]
        cnt_ref[q] = _dot(cb, jnp.ones((sub, LANES), BF16))


def _route(scores_t, router_bias):
    B, E, S = scores_t.shape
    TM = min(ROUTE_TILE, S)
    per_step = min(MOE_OUTER, S) // TM
    nj = S // (TM * per_step)
    ns = B * nj * per_step
    tri = jnp.asarray(np.triu(np.ones((TM, TM), np.float32), k=1), BF16)
    tile = lambda w: pl.BlockSpec((per_step, E, w), lambda b, j: (b * nj + j, 0, 0))
    return pl.pallas_call(
        _route_kernel,
        grid=(B, nj),
        in_specs=[pl.BlockSpec((None, E, TM * per_step), lambda b, j: (b, 0, j)),
                  pl.BlockSpec((E, 1), lambda b, j: (0, 0)),
                  pl.BlockSpec((TM, TM), lambda b, j: (0, 0))],
        out_specs=[tile(TM), tile(TM), tile(LANES)],
        out_shape=[jax.ShapeDtypeStruct((ns, E, TM), F32), jax.ShapeDtypeStruct((ns, E, TM), F32),
                   jax.ShapeDtypeStruct((ns, E, LANES), F32)],
        compiler_params=_params(("arbitrary", "arbitrary")),
        name="route",
    )(scores_t, router_bias.reshape(E, 1), tri)


def _moe_kernel(cnt_ref, h_ref, x1_ref, mod_ref, rank_ref, w_ref, wg_ref, wu_ref, wd_ref,
                sg_ref, su_ref, sd_ref, o_ref, xc, yc, pc, wc, *, n_sub, tm, n_exp):
    to = pl.program_id(0)
    eb = pl.program_id(1)
    gate = mod_ref[5:6, :]

    @pl.when(eb == 0)
    def _shared():
        for s in range(n_sub):
            rows = pl.ds(s * tm, tm)
            hs = h_ref[rows, :]
            hid = jax.nn.silu(_dot(hs, sg_ref[...])) * _dot(hs, su_ref[...])
            o_ref[rows, :] = x1_ref[rows, :] + gate * _dot(hid.astype(BF16), sd_ref[...])
            xc[s] = jnp.zeros(xc.shape[1:], BF16)
            yc[s] = jnp.zeros(yc.shape[1:], BF16)
            wc[s] = jnp.zeros(wc.shape[1:], F32)

    PIECE, WIN = MOE_PIECE, MOE_WIN
    tn = (((0,), (0,)), ((), ()))
    experts = [eb * n_exp + j for j in range(n_exp)]
    cnt = [[cnt_ref[(to * n_sub + s) * N_EXPERTS + e] for e in experts] for s in range(n_sub)]
    off = []
    for s in range(n_sub):
        o = [jnp.int32(0)]
        for j in range(n_exp):
            o.append(o[-1] + ((cnt[s][j] + 15) // 16) * 16)
        off.append(o)
    n_pieces = [(off[s][n_exp] + PIECE - 1) // PIECE for s in range(n_sub)]
    fits = n_pieces[0] < MOE_ROWS // PIECE
    for s in range(1, n_sub):
        fits = jnp.logical_and(fits, n_pieces[s] < MOE_ROWS // PIECE)

    def one_hot_rows(s, base, rows_n, js):
        slot = (lax.broadcasted_iota(jnp.int32, (rows_n, tm), 0) + base).astype(F32)
        pick = jnp.zeros((rows_n, tm), F32)
        wacc = jnp.zeros((rows_n, tm), F32)
        for j in js:
            hit = (slot - off[s][j].astype(F32)) == rank_ref[s, pl.ds(experts[j], 1), :]
            pick = jnp.where(hit, 1.0, pick)
            wacc = jnp.where(hit, w_ref[s, pl.ds(experts[j], 1), :], wacc)
        return pick.astype(BF16), jnp.sum(wacc, axis=1, keepdims=True)

    def mlp(xg, j):
        hid = jax.nn.silu(_dot(xg, wg_ref[j])) * _dot(xg, wu_ref[j])
        return _dot(hid.astype(BF16), wd_ref[j])

    def piece(s, q):
        r = pl.ds(q * PIECE, PIECE)
        pick, wrow = one_hot_rows(s, q * PIECE, PIECE, range(n_exp))
        xc[s, r, :] = _dot(pick, h_ref[pl.ds(s * tm, tm), :]).astype(BF16)
        pc[s, :, r] = pick.T
        wc[s, r, :] = wrow

    def window(j, p, first):
        starts = [pl.multiple_of(jnp.minimum(off[s][j] + p * WIN, (n_pieces[s] + 1) * PIECE - WIN), 16)
                  for s in range(n_sub)]
        xg = jnp.concatenate([xc[s, pl.ds(starts[s], WIN), :] for s in range(n_sub)], axis=0)
        out = mlp(xg, j)
        for s in range(n_sub):
            r = pl.ds(starts[s], WIN)
            new = (out[s * WIN:(s + 1) * WIN] * wc[s, r, :] * gate).astype(BF16)
            if not first:
                rank = starts[s] - off[s][j] + lax.broadcasted_iota(jnp.int32, (WIN, 1), 0)
                new = jnp.where(rank < cnt[s][j], new, yc[s, r, :])
            yc[s, r, :] = new

    def scatter(s, start, n):
        r = pl.ds(start, n)
        o_ref[pl.ds(s * tm, tm), :] += _dot(pc[s, :, r], yc[s, r, :])

    @pl.when(fits)
    def _packed():
        for s in range(n_sub):
            piece(s, 0)
            piece(s, 1)
        for s in range(n_sub):
            @pl.when(n_pieces[s] > 2)
            def _third(s=s):
                piece(s, 2)
        for j in range(n_exp):
            window(j, 0, True)
        for j in range(n_exp):
            most = cnt[0][j]
            for s in range(1, n_sub):
                most = jnp.maximum(most, cnt[s][j])

            @pl.when(most > WIN)
            def _more(j=j, most=most):
                lax.fori_loop(1, (most + WIN - 1) // WIN, lambda p, c: (window(j, p, False), c)[1], 0)
        for s in range(n_sub):
            scatter(s, 0, 2 * PIECE)
        for s in range(n_sub):
            @pl.when(n_pieces[s] > 2)
            def _third_out(s=s):
                scatter(s, 2 * PIECE, PIECE)

    @pl.when(jnp.logical_not(fits))
    def _unpacked():
        for j in range(n_exp):
            for s in range(n_sub):
                rows = pl.ds(s * tm, tm)

                def block(bi, carry, j=j, s=s, rows=rows):
                    slot0 = off[s][j] + bi * PIECE
                    pick, wrow = one_hot_rows(s, slot0, PIECE, [j])
                    out = mlp(_dot(pick, h_ref[rows, :]).astype(BF16), j)
                    ow = (out * wrow * gate).astype(BF16)
                    o_ref[rows, :] += lax.dot_general(pick, ow, tn, preferred_element_type=F32)
                    return carry

                lax.fori_loop(0, (cnt[s][j] + PIECE - 1) // PIECE, block, 0)


def _moe(h2, x1, mod, rank_t, w_t, counts, w_gate, w_up, w_down, ws_gate, ws_up, ws_down, seq):
    T, D = h2.shape
    ns, E, tm = rank_t.shape
    tmo = min(MOE_OUTER, seq)
    n_sub = tmo // tm
    n_exp = MOE_EXPERTS_PER_STEP
    per_b = seq // tmo
    ws = [w_gate.astype(BF16), w_up.astype(BF16), w_down.astype(BF16)]
    sh = [ws_gate.astype(BF16), ws_up.astype(BF16), ws_down.astype(BF16)]
    tok = lambda: pl.BlockSpec((tmo, D), lambda t, e, c: (t, 0))
    sub = lambda: pl.BlockSpec((n_sub, E, tm), lambda t, e, c: (t, 0, 0))
    exp = lambda a: pl.BlockSpec((n_exp,) + a.shape[1:], lambda t, e, c: (e, 0, 0))
    full = lambda a: pl.BlockSpec(a.shape, lambda t, e, c: (0, 0))
    grid_spec = pltpu.PrefetchScalarGridSpec(
        num_scalar_prefetch=1,
        grid=(T // tmo, E // n_exp),
        in_specs=[tok(), tok(), pl.BlockSpec((None, N_MOD, D), lambda t, e, c: (t // per_b, 0, 0)),
                  sub(), sub(), exp(ws[0]), exp(ws[1]), exp(ws[2]), full(sh[0]), full(sh[1]), full(sh[2])],
        out_specs=tok(),
        scratch_shapes=[pltpu.VMEM((n_sub, MOE_ROWS, D), BF16), pltpu.VMEM((n_sub, MOE_ROWS, D), BF16),
                        pltpu.VMEM((n_sub, tm, MOE_ROWS), BF16), pltpu.VMEM((n_sub, MOE_ROWS, 1), F32)],
    )
    return pl.pallas_call(
        functools.partial(_moe_kernel, n_sub=n_sub, tm=tm, n_exp=n_exp),
        grid_spec=grid_spec,
        out_shape=jax.ShapeDtypeStruct((T, D), F32),
        compiler_params=_params(("arbitrary", "arbitrary")),
        name="moe",
    )(counts, h2, x1, mod, rank_t, w_t, *ws, *sh)


def kernel(x, c, w_ada, b_ada, norm_mix_gain, norm_ffn_gain, w_in, s5_lambda_re, s5_lambda_im, s5_log_dt, s5_b_re, s5_b_im, s5_c_re, s5_c_im, s5_d, s5_w_glu, s5_b_glu, q_norm_gain, k_norm_gain, cmp_pe, cmp_w1, cmp_b1, cmp_w2, cmp_b2, w_branch_a, w_branch_b, w_out, w_router, router_bias, w_gate, w_up, w_down, ws_gate, ws_up, ws_down):
    B, S, D = x.shape
    for l in range(w_ada.shape[0]):
        mod = _ada(c, w_ada[l], b_ada[l]).reshape(B, N_MOD, D)
        u, kvc, qt, ks, vst, kwn, vwt, gt, gm = _inproj(x, mod, norm_mix_gain[l:l + 1], w_in[l],
                                                        q_norm_gain[l], k_norm_gain[l])
        tables = _s5_tables(s5_lambda_re[l], s5_lambda_im[l], s5_log_dt[l], s5_b_re[l], s5_b_im[l],
                            s5_c_re[l], s5_c_im[l], s5_d[l])
        ys5 = _s5(u, tables)
        kvc_c = _compress(kvc, cmp_pe[l], cmp_w1[l], cmp_b1[l], cmp_w2[l], cmp_b2[l], k_norm_gain[l, 0])
        o_nsa = _attend(qt, ks, vst, kwn, vwt, gt, kvc_c)
        x1, h2, scores_t = _merge(x, mod, ys5, o_nsa, gm, s5_w_glu[l], s5_b_glu[l], w_branch_a[l],
                                  w_branch_b[l], w_out[l], norm_ffn_gain[l:l + 1], w_router[l])
        rank_t, w_t, cnt = _route(scores_t, router_bias[l])
        counts = cnt[:, :, 0].astype(jnp.int32).reshape(-1)
        x = _moe(h2.reshape(B * S, D), x1.reshape(B * S, D), mod, rank_t, w_t, counts,
                 w_gate[l], w_up[l], w_down[l], ws_gate[l], ws_up[l], ws_down[l], S).reshape(B, S, D)
    return x
```

```python
import functools

import numpy as np
import jax
import jax.numpy as jnp
from jax import lax
from jax.experimental import pallas as pl
from jax.experimental.pallas import tpu as pltpu

F32 = jnp.float32
BF16 = jnp.bfloat16

D_MODEL = 1024
S5_WIDTH = 512
S5_GROUP = 16
S5_GROUPS = S5_WIDTH // S5_GROUP
S5_STATE = 64
N_HEADS = 8
N_KV_HEADS = 2
GQA_GROUP = N_HEADS // N_KV_HEADS
HEAD_DIM = 64
NSA_WIDTH = N_HEADS * HEAD_DIM
KV_WIDTH = 2 * N_KV_HEADS * HEAD_DIM
CMP_BLOCK = 32
CMP_STRIDE = 16
CMP_HIDDEN = 256
SEL_BLOCK = 64
SEL_TOPK = 8
WINDOW = 256
Q_BLOCK = 128
FORCE_BONUS = 1e3
N_EXPERTS = 64
TOP_K = 8
N_EXPERT_GROUPS = 8
TOPK_EXPERT_GROUPS = 4
ROUTED_SCALE = 2.5
RMS_EPS = 1e-6
NEG_INF = -1e30
N_MOD = 6

LANES = 128
SUBLANES = 8
S5_CHUNK = 16
S5_CW = S5_CHUNK * S5_GROUP
ROW_TILE = 512
ATT_TILES = 2
SEL_KEY_CHUNK = 512
SEL_UNIT = 128
SEL_SLOTS = (5, 8)
POS_BASE = 64
POS_ROWS = 16
AUG = 128
ROUTE_TILE = 256
MOE_OUTER = 1024
MOE_EXPERTS_PER_STEP = 4
MOE_PIECE = 128
MOE_WIN = 48
MOE_ROWS = 512
RANK_NONE = -float(1 << 20)
VMEM_LIMIT = 56 * 1024 * 1024


def _dot(a, b):
    return jnp.dot(a, b, preferred_element_type=F32)


def _split(a):
    hi = a.astype(BF16)
    lo = (a - hi.astype(F32)).astype(BF16)
    return hi, lo


def _dot3(a, bh, bl):
    ah, al = _split(a)
    return _dot(ah, bh) + (_dot(al, bh) + _dot(ah, bl))


def _segment_transpose(x):
    assert x.shape[-2:] == (SUBLANES, LANES) and LANES // S5_GROUP == SUBLANES
    nd = x.ndim
    i = lax.broadcasted_iota(jnp.int32, x.shape, nd - 2)
    seg = lax.broadcasted_iota(jnp.int32, x.shape, nd - 1) // S5_GROUP
    out = x
    for d in range(1, SUBLANES):
        r = pltpu.roll(pltpu.roll(x, SUBLANES - d, axis=nd - 2), S5_GROUP * d, axis=nd - 1)
        out = jnp.where(seg == ((i + d) & (SUBLANES - 1)), r, out)
    return out


def _to_chunk_major(u, put):
    rows = u.shape[0]
    u3 = u.reshape(rows // SUBLANES, SUBLANES, S5_WIDTH)
    halves = S5_CHUNK // SUBLANES
    for jb in range(S5_WIDTH // LANES):
        t = _segment_transpose(u3[:, :, LANES * jb:LANES * (jb + 1)])
        t = t.reshape(rows // S5_CHUNK, halves, SUBLANES, LANES)
        for hf in range(halves):
            put(slice(SUBLANES * jb, SUBLANES * (jb + 1)), slice(LANES * hf, LANES * (hf + 1)), t[:, hf])


def _from_chunk_major(get, rows):
    halves = S5_CHUNK // SUBLANES
    cols = []
    for jb in range(S5_WIDTH // LANES):
        parts = [_segment_transpose(get(slice(SUBLANES * jb, SUBLANES * (jb + 1)), slice(LANES * hf, LANES * (hf + 1))))
                 for hf in range(halves)]
        cols.append(jnp.stack(parts, axis=1).reshape(rows, LANES))
    return jnp.concatenate(cols, axis=1)


def _params(sem):
    return pltpu.CompilerParams(dimension_semantics=sem, vmem_limit_bytes=VMEM_LIMIT)


def _ada_kernel(c_ref, w_ref, b_ref, o_ref):
    cs = jax.nn.silu(c_ref[...])
    wh, wl = _split(w_ref[...])
    o_ref[...] = _dot3(cs, wh, wl) + b_ref[...]


def _ada(c, w_ada, b_ada):
    B, D = c.shape
    return pl.pallas_call(
        _ada_kernel,
        grid=(N_MOD,),
        in_specs=[pl.BlockSpec((B, D), lambda j: (0, 0)),
                  pl.BlockSpec((D, D), lambda j: (0, j)),
                  pl.BlockSpec((1, D), lambda j: (0, j))],
        out_specs=pl.BlockSpec((B, D), lambda j: (0, j)),
        out_shape=jax.ShapeDtypeStruct((B, N_MOD * D), F32),
        compiler_params=_params(("arbitrary",)),
        name="ada",
    )(c, w_ada, b_ada.reshape(1, N_MOD * D))


def _head_norm(v, bd, gain):
    sq = v * v
    sh, sl = _split(sq)
    ms = _dot(sh, bd) + _dot(sl, bd)
    return v * lax.rsqrt(ms + RMS_EPS) * gain


def _inproj_kernel(x_ref, mod_ref, gain_ref, wm_ref, wt_ref, wgn_ref, wgm_ref, bd_ref, qg_ref, kg_ref,
                   u_ref, kvc_ref, qt_ref, ksa_ref, vst_ref, kwa_ref, vwt_ref, gt_ref, gm_ref):
    x = x_ref[...]
    shift = mod_ref[0:1, :]
    scale = mod_ref[1:2, :]
    y = x * lax.rsqrt(jnp.mean(x * x, axis=-1, keepdims=True) + RMS_EPS)
    h = (y * gain_ref[...]) * (1.0 + scale) + shift
    hb = h.astype(BF16)
    main = _dot(hb, wm_ref[...])
    tm = x.shape[0]

    def put(groups, lanes, block):
        u_ref[:, groups, lanes] = block

    _to_chunk_major(main[:, :S5_WIDTH], put)
    o = S5_WIDTH
    for part in range(KV_WIDTH // LANES):
        kvc_ref[part] = main[:, o + part * LANES:o + (part + 1) * LANES]
    o += KV_WIDTH
    kw = N_KV_HEADS * HEAD_DIM
    bd = bd_ref[...]
    ks = _head_norm(main[:, o:o + kw], bd[:kw, :kw], kg_ref[1:2, :]).astype(BF16)
    kwn = _head_norm(main[:, o + kw:o + 2 * kw], bd[:kw, :kw], kg_ref[2:3, :]).astype(BF16)
    pos = pl.program_id(1) * tm + lax.broadcasted_iota(jnp.int32, (tm, 1), 0)
    lane = lax.broadcasted_iota(jnp.int32, (tm, AUG), 1)
    digits = jnp.where(lane == 0, pos // POS_BASE, jnp.where(lane == 1, pos % POS_BASE, 0))
    digits = digits[:, :AUG - HEAD_DIM].astype(F32).astype(BF16)
    onehot = jnp.where(lane == pos // SEL_BLOCK, 1.0, 0.0).astype(BF16)
    for hh in range(N_KV_HEADS):
        ksa_ref[hh, :, 0:HEAD_DIM] = ks[:, hh * HEAD_DIM:(hh + 1) * HEAD_DIM]
        ksa_ref[hh, :, HEAD_DIM:AUG] = digits
        ksa_ref[hh, :, AUG:] = onehot
        kwa_ref[hh, :, 0:HEAD_DIM] = kwn[:, hh * HEAD_DIM:(hh + 1) * HEAD_DIM]
        kwa_ref[hh, :, HEAD_DIM:] = digits

    nt = (((1,), (1,)), ((), ()))
    tt = lax.dot_general(wt_ref[...], hb, nt, preferred_element_type=F32)
    qt = tt[:NSA_WIDTH]
    sq = qt * qt
    sh, sl = _split(sq)
    qn = (qt * lax.rsqrt(_dot(bd, sh) + _dot(bd, sl) + RMS_EPS) * qg_ref[...] * (HEAD_DIM ** -0.5)).astype(BF16)
    QW = GQA_GROUP * Q_BLOCK
    for qb in range(tm // Q_BLOCK):
        for hd in range(N_HEADS):
            hh, g = divmod(hd, GQA_GROUP)
            qt_ref[hh, :, qb * QW + g * Q_BLOCK:qb * QW + (g + 1) * Q_BLOCK] = (
                qn[hd * HEAD_DIM:(hd + 1) * HEAD_DIM, qb * Q_BLOCK:(qb + 1) * Q_BLOCK])
    vst_ref[...] = tt[NSA_WIDTH:NSA_WIDTH + kw].reshape(N_KV_HEADS, HEAD_DIM, tm).astype(BF16)
    vwt_ref[...] = tt[NSA_WIDTH + kw:].reshape(N_KV_HEADS, HEAD_DIM, tm).astype(BF16)
    gn = jax.nn.sigmoid(lax.dot_general(wgn_ref[...], hb, nt, preferred_element_type=F32))
    for qb in range(tm // Q_BLOCK):
        for br in range(3):
            for hd in range(N_HEADS):
                hh, g = divmod(hd, GQA_GROUP)
                r = br * N_HEADS + hd
                gt_ref[hh, br:br + 1, qb * QW + g * Q_BLOCK:qb * QW + (g + 1) * Q_BLOCK] = (
                    gn[r:r + 1, qb * Q_BLOCK:(qb + 1) * Q_BLOCK])
    gm_ref[...] = jax.nn.sigmoid(_dot(hb, wgm_ref[...])).astype(BF16)


def _s5_chunk_spec(tm):
    assert S5_CHUNK == 16 and S5_GROUP == 16, "the segment transposes assume 16 steps x 16 channels"
    return pl.BlockSpec((tm // S5_CHUNK, None, S5_GROUPS, S5_CW), lambda b, i: (i, b, 0, 0))


def _inproj(x, mod, gain, w_in, q_gain, k_gain):
    B, S, D = x.shape
    tm = min(ROW_TILE, S)
    assert S // SEL_BLOCK <= AUG and tm % Q_BLOCK == 0
    kw = N_KV_HEADS * HEAD_DIM
    cols = np.cumsum((0,) + (S5_WIDTH, NSA_WIDTH, KV_WIDTH, KV_WIDTH, KV_WIDTH, 3 * N_HEADS, 2 * D))
    c_u, c_q, c_kvc, c_kvs, c_kvw, c_gn, c_gm = cols[:7]
    sl = lambda a, n: w_in[:, a:a + n]
    wm = jnp.concatenate([sl(c_u, S5_WIDTH), sl(c_kvc, KV_WIDTH), sl(c_kvs, kw), sl(c_kvw, kw)], axis=1).astype(BF16)
    wt = jnp.concatenate([sl(c_q, NSA_WIDTH), sl(c_kvs + kw, kw), sl(c_kvw + kw, kw)], axis=1).T.astype(BF16)
    wgn = sl(c_gn, 3 * N_HEADS).T.astype(BF16)
    wgm = sl(c_gm, 2 * D).astype(BF16)
    seg = np.arange(NSA_WIDTH) // HEAD_DIM
    bd = jnp.asarray((seg[:, None] == seg[None, :]).astype(np.float32) / HEAD_DIM, BF16)
    qg = jnp.tile(q_gain, N_HEADS).reshape(NSA_WIDTH, 1)
    kg = jnp.tile(k_gain, (1, N_KV_HEADS))
    nq = tm // Q_BLOCK * GQA_GROUP * Q_BLOCK
    row = lambda w: pl.BlockSpec((None, tm, w), lambda b, i: (b, i, 0))
    full = lambda a: pl.BlockSpec(a.shape, lambda b, i: (0,) * a.ndim)
    rows4 = lambda w: pl.BlockSpec((None, N_KV_HEADS, tm, w), lambda b, i: (b, 0, i, 0))
    cols4 = lambda r, w: pl.BlockSpec((None, N_KV_HEADS, r, w), lambda b, i: (b, 0, 0, i))
    nqt = S // Q_BLOCK * GQA_GROUP * Q_BLOCK
    return pl.pallas_call(
        _inproj_kernel,
        grid=(B, S // tm),
        in_specs=[row(D), pl.BlockSpec((None, N_MOD, D), lambda b, i: (b, 0, 0)),
                  full(gain), full(wm), full(wt), full(wgn), full(wgm), full(bd), full(qg), full(kg)],
        out_specs=[_s5_chunk_spec(tm), pl.BlockSpec((None, KV_WIDTH // LANES, tm, LANES), lambda b, i: (b, 0, i, 0)),
                   cols4(HEAD_DIM, nq), rows4(2 * AUG), cols4(HEAD_DIM, tm),
                   rows4(AUG), cols4(HEAD_DIM, tm), cols4(3, nq), row(2 * D)],
        out_shape=[jax.ShapeDtypeStruct((S // S5_CHUNK, B, S5_GROUPS, S5_CW), F32),
                   jax.ShapeDtypeStruct((B, KV_WIDTH // LANES, S, LANES), F32),
                   jax.ShapeDtypeStruct((B, N_KV_HEADS, HEAD_DIM, nqt), BF16),
                   jax.ShapeDtypeStruct((B, N_KV_HEADS, S, 2 * AUG), BF16),
                   jax.ShapeDtypeStruct((B, N_KV_HEADS, HEAD_DIM, S), BF16),
                   jax.ShapeDtypeStruct((B, N_KV_HEADS, S, AUG), BF16),
                   jax.ShapeDtypeStruct((B, N_KV_HEADS, HEAD_DIM, S), BF16),
                   jax.ShapeDtypeStruct((B, N_KV_HEADS, 3, nqt), F32),
                   jax.ShapeDtypeStruct((B, S, 2 * D), BF16)],
        compiler_params=_params(("arbitrary", "arbitrary")),
        name="inproj",
    )(x, mod, gain, wm, wt, wgn, wgm, bd, qg, kg)


def _s5_tables(lam_re, lam_im, log_dt, b_re, b_im, c_re, c_im, d_skip):
    L, G = S5_CHUNK, S5_GROUPS
    lr, li = lam_re.astype(F32), lam_im.astype(F32)
    dt = jnp.exp(log_dt.astype(F32))[:, None]
    mag = jnp.exp(lr * dt)
    abar_re, abar_im = mag * jnp.cos(li * dt), mag * jnp.sin(li * dt)
    num_re, num_im = abar_re - 1.0, abar_im
    den = lr * lr + li * li
    coef_re = (num_re * lr + num_im * li) / den
    coef_im = (num_im * lr - num_re * li) / den
    br, bi = b_re.astype(F32), b_im.astype(F32)
    bbar_re = coef_re[..., None] * br - coef_im[..., None] * bi
    bbar_im = coef_re[..., None] * bi + coef_im[..., None] * br
    k = jnp.arange(L + 1, dtype=F32)[:, None]
    pmag = jnp.exp((lr * dt)[:, None, :] * k)
    ang = (li * dt)[:, None, :] * k
    pw = jnp.stack([jnp.tile(pmag * jnp.cos(ang), (1, 1, 2)), jnp.tile(pmag * jnp.sin(ang), (1, 1, 2))], axis=1)
    brt, bit = bbar_re.transpose(0, 2, 1), bbar_im.transpose(0, 2, 1)
    ba = jnp.concatenate([brt, bit], axis=2)
    bb = jnp.concatenate([-bit, brt], axis=2)
    cr, ci = c_re.astype(F32), c_im.astype(F32)
    ca = jnp.concatenate([cr, -ci], axis=2)
    cb = jnp.concatenate([-ci, -cr], axis=2)
    dv = jnp.tile(d_skip.astype(F32), (1, L)).reshape(G, 1, S5_CW)
    return pw, ba, bb, ca, cb, dv


def _s5_kernel(u_ref, pw_ref, ba_ref, bb_ref, ca_ref, cb_ref, dv_ref,
               y_ref, vr_scr, vi_scr, xr_scr, xi_scr, *, n_chunks, bsz):
    L, P, N = S5_CHUNK, S5_GROUP, S5_STATE
    nt = (((1,), (1,)), ((), ()))
    dot_nt = lambda a, b: lax.dot_general(a, b, nt, preferred_element_type=F32)
    mix = lambda a, b, k: a * pw_ref[0, k:k + 1, :] + b * pw_ref[1, k:k + 1, :]
    ba, bb, ca, cb = ba_ref[...], bb_ref[...], ca_ref[...], cb_ref[...]
    ws = jnp.concatenate([mix(ba, bb, L - 1 - s) for s in range(L)], axis=0)
    wot = jnp.concatenate([mix(ca, cb, k) for k in range(L + 1)], axis=0)
    wsh, wsl = _split(ws)
    wth, wtl = _split(wot)
    bah, bal = _split(ba)
    kern = dot_nt(bah, wth[:L * P]) + (dot_nt(bal, wth[:L * P]) + dot_nt(bah, wtl[:L * P]))
    lane = lax.broadcasted_iota(jnp.int32, kern.shape, 1)
    mt = jnp.concatenate([kern] + [jnp.where(lane >= P * s, pltpu.roll(kern, P * s, axis=1), 0.0)
                                   for s in range(1, L)], axis=0)
    mth, mtl = _split(mt)

    u = u_ref[...]
    uh, ul = _split(u)
    y = _dot(uh, mth) + (_dot(ul, mth) + _dot(uh, mtl))
    v = _dot(uh, wsh) + (_dot(ul, wsh) + _dot(uh, wsl))
    vr_scr[...] = v[:, :N]
    vi_scr[...] = v[:, N:]
    a_r = pw_ref[0, L:L + 1, :N]
    a_i = pw_ref[1, L:L + 1, :N]

    def step(c, x):
        xr, xi = x
        r = pl.ds(pl.multiple_of(c * bsz, bsz), bsz)
        xr_scr[r, :] = xr
        xi_scr[r, :] = xi
        return (a_r * xr - a_i * xi + vr_scr[r, :], a_r * xi + a_i * xr + vi_scr[r, :])

    zero = jnp.zeros((bsz, N), F32)
    lax.fori_loop(0, n_chunks, step, (zero, zero), unroll=8)
    xh, xl = _split(jnp.concatenate([xr_scr[...], xi_scr[...]], axis=1))
    y = y + (dot_nt(xh, wth[P:]) + (dot_nt(xl, wth[P:]) + dot_nt(xh, wtl[P:])))
    y_ref[...] = y + dv_ref[...] * u


def _s5(ug, tables):
    nc, B, G, cw = ug.shape
    N = S5_STATE
    grp = lambda a: pl.BlockSpec((None,) + a.shape[1:], lambda g: (g,) + (0,) * (a.ndim - 1))
    rows = pl.BlockSpec((nc * B, cw), lambda g: (0, g))
    y = pl.pallas_call(
        functools.partial(_s5_kernel, n_chunks=nc, bsz=B),
        grid=(G,),
        in_specs=[rows] + [grp(t) for t in tables],
        out_specs=rows,
        out_shape=jax.ShapeDtypeStruct((nc * B, G * cw), F32),
        scratch_shapes=[pltpu.VMEM((nc * B, N), F32)] * 4,
        compiler_params=_params(("arbitrary",)),
        name="s5",
    )(ug.reshape(nc * B, G * cw), *tables)
    return y.reshape(nc, B, G, cw)


def _compress_kernel(x_ref, pe_ref, w1_ref, b1_ref, w2_ref, b2_ref, kg_ref, o_ref, *, n_rows):
    half = CMP_STRIDE * HEAD_DIM
    for j in range(2 * N_KV_HEADS):
        kv, hd = divmod(j, N_KV_HEADS)
        x = jnp.concatenate(
            [x_ref[kv, pl.ds(t, n_rows, stride=CMP_STRIDE), :][:, hd * HEAD_DIM:(hd + 1) * HEAD_DIM]
             for t in range(CMP_STRIDE)], axis=1)
        a = _dot((x + pe_ref[kv, 0:1, :]).astype(BF16), w1_ref[kv, :half, :])
        b = _dot((x + pe_ref[kv, 1:2, :]).astype(BF16), w1_ref[kv, half:, :])
        hid = jax.nn.gelu(a + pltpu.roll(b, n_rows - 1, axis=0) + b1_ref[kv])
        out = _dot(hid.astype(BF16), w2_ref[kv]) + b2_ref[kv]
        if kv == 0:
            out = out * lax.rsqrt(jnp.mean(out * out, axis=-1, keepdims=True) + RMS_EPS) * kg_ref[...]
        o_ref[j] = out.astype(BF16)


def _compress(kv_c, cmp_pe, cmp_w1, cmp_b1, cmp_w2, cmp_b2, k_gain0):
    B, _, S, _ = kv_c.shape
    nr = S // CMP_STRIDE
    half = CMP_STRIDE * HEAD_DIM
    nj = 2 * N_KV_HEADS
    ws = [cmp_pe.reshape(2, 2, half), cmp_w1.astype(BF16), cmp_b1.reshape(2, 1, CMP_HIDDEN), cmp_w2.astype(BF16),
          cmp_b2.reshape(2, 1, HEAD_DIM), k_gain0.reshape(1, HEAD_DIM)]
    full = lambda a: pl.BlockSpec(a.shape, lambda b: (0,) * a.ndim)
    return pl.pallas_call(
        functools.partial(_compress_kernel, n_rows=nr),
        grid=(B,),
        in_specs=[pl.BlockSpec((None,) + kv_c.shape[1:], lambda b: (b, 0, 0, 0))] + [full(a) for a in ws],
        out_specs=pl.BlockSpec((None, nj, nr, HEAD_DIM), lambda b: (b, 0, 0, 0)),
        out_shape=jax.ShapeDtypeStruct((B, nj, nr, HEAD_DIM), BF16),
        compiler_params=_params(("arbitrary",)),
        name="compress",
    )(kv_c, *ws)


def _softmax_cols(s, mask):
    s = jnp.where(mask, s, NEG_INF)
    m = jnp.max(s, axis=0, keepdims=True)
    p = jnp.where(mask, jnp.exp(s - m), 0.0)
    return p / jnp.maximum(jnp.sum(p, axis=0, keepdims=True), 1e-20)


def _attend_kernel(qt_ref, kc_ref, vct_ref, ks_ref, vst_ref, kw_ref, vwt_ref, g_ref, slope_ref, ovt_ref,
                   o_ref, qa_scr, sa_scr, sb_scr, *, n_sel, n_pick, n_cmp_rows):
    QW = GQA_GROUP * Q_BLOCK
    chains = [(k, h) for k in range(ATT_TILES) for h in range(N_KV_HEADS)]
    t0s = [(pl.program_id(1) * ATT_TILES + k) * Q_BLOCK for k in range(ATT_TILES)]
    tqs = [t0 + (lax.broadcasted_iota(jnp.int32, (1, QW), 1) & (Q_BLOCK - 1)) for t0 in t0s]
    cols = lambda k: slice(k * QW, (k + 1) * QW)

    r = lax.broadcasted_iota(jnp.int32, (POS_ROWS, QW), 0)
    qa = {}
    for k, h in chains:
        slope = slope_ref[h]
        qa_scr[k, h, 0:HEAD_DIM, :] = qt_ref[h, :, cols(k)]
        qa_scr[k, h, HEAD_DIM:HEAD_DIM + POS_ROWS, :] = jnp.where(
            r == 0, slope * POS_BASE, jnp.where(r == 1, slope, 0.0)).astype(BF16)
        qa_scr[k, h, HEAD_DIM + POS_ROWS:AUG, :] = jnp.zeros((AUG - HEAD_DIM - POS_ROWS, QW), BF16)
        qa[k, h] = qa_scr[k, h, 0:AUG, :]

    def compressed(k, h, rows):
        sc = _dot(kc_ref[h, 0:rows, :], qa[k, h])
        cpos = lax.broadcasted_iota(jnp.int32, (rows, QW), 0) * CMP_STRIDE + (CMP_BLOCK - 1)
        p_c = _softmax_cols(sc, cpos <= tqs[k])
        o_c = _dot(vct_ref[h, :, 0:rows], p_c.astype(BF16))
        psum = p_c[:, 0:Q_BLOCK]
        for g in range(1, GQA_GROUP):
            psum = psum + p_c[:, g * Q_BLOCK:(g + 1) * Q_BLOCK]
        ph, pl_ = _split(psum)
        return o_c, _dot(ovt_ref[:, 0:rows], ph) + _dot(ovt_ref[:, 0:rows], pl_)

    half = n_cmp_rows // 2
    if half % LANES == 0:
        early = (t0s[-1] + Q_BLOCK - CMP_BLOCK) // CMP_STRIDE < half
        cmp_out = lax.cond(early, lambda _: [compressed(k, h, half) for k, h in chains],
                           lambda _: [compressed(k, h, n_cmp_rows) for k, h in chains], 0)
    else:
        cmp_out = [compressed(k, h, n_cmp_rows) for k, h in chains]
    cmp_out = dict(zip(chains, cmp_out))

    def select(k, h, imp):
        jb = lax.broadcasted_iota(jnp.int32, (n_sel, Q_BLOCK), 0)
        cur = (t0s[k] + lax.broadcasted_iota(jnp.int32, (1, Q_BLOCK), 1)) // SEL_BLOCK
        forced = (jb == 0) | (jb == cur) | (jb == cur - 1)
        imp = jnp.where(forced, imp + FORCE_BONUS, imp)
        imp = jnp.where(jb <= cur, imp, -1.0)
        bias = jnp.full((n_sel, Q_BLOCK), NEG_INF, F32)
        for _ in range(n_pick):
            m = jnp.max(imp, axis=0, keepdims=True)
            first = jnp.min(jnp.where(imp == m, jb, n_sel), axis=0, keepdims=True)
            hit = jb == first
            bias = jnp.where(hit, 0.0, bias)
            imp = jnp.where(hit, -jnp.inf, imp)
        per_unit = SEL_UNIT // SEL_BLOCK
        n_units = n_sel // per_unit
        picked = jnp.where(bias == 0.0, 1.0, 0.0).astype(BF16)
        in_unit = (lax.broadcasted_iota(jnp.int32, (n_units, n_sel), 1) // per_unit
                   == lax.broadcasted_iota(jnp.int32, (n_units, n_sel), 0))
        hits = _dot(jnp.where(in_unit, 1.0, 0.0).astype(BF16), picked)
        live = jnp.where(jnp.max(hits, axis=1, keepdims=True) > 0.5, 1.0, 0.0)
        live = jnp.broadcast_to(live, (n_units, LANES))
        earlier = (lax.broadcasted_iota(jnp.int32, (n_units, n_units), 1)
                   < lax.broadcasted_iota(jnp.int32, (n_units, n_units), 0))
        before = _dot(jnp.where(earlier, 1.0, 0.0).astype(BF16), live.astype(BF16))
        slot = lax.broadcasted_iota(jnp.int32, (n_units, LANES), 1).astype(F32)
        in_slot = jnp.where(before == slot, live, 0.0).astype(BF16)
        r8 = lax.broadcasted_iota(jnp.int32, (SUBLANES, n_units), 0)
        c8 = lax.broadcasted_iota(jnp.int32, (SUBLANES, n_units), 1)
        pick_rows = jnp.where(r8 == 0, c8, jnp.where(r8 == 1, 1, 0)).astype(F32).astype(BF16)
        ids = _dot(pick_rows, in_slot)
        n_live = (before[n_units - 1:n_units, 0:1] + live[n_units - 1:n_units, 0:1])[0, 0]
        bias = jnp.concatenate([bias] * GQA_GROUP, axis=1).astype(BF16)
        qa_scr[k, h, AUG:, :] = jnp.concatenate([bias, jnp.zeros((AUG - n_sel, QW), BF16)], axis=0)
        return ids, n_live

    def window(k, h):
        WK = WINDOW + Q_BLOCK
        w0 = pl.multiple_of(jnp.maximum(t0s[k] - WINDOW, 0), Q_BLOCK)
        sw = _dot(kw_ref[h, pl.ds(w0, WK), :], qa[k, h])
        dist = tqs[k] - (w0 + lax.broadcasted_iota(jnp.int32, (WK, QW), 0))
        in_win = (dist | (WINDOW - 1 - dist)) >= 0
        p_w = _softmax_cols(sw, in_win)
        return _dot(vwt_ref[h, :, pl.ds(w0, WK)], p_w.astype(BF16))

    lists = {c: select(*c, cmp_out[c][1]) for c in chains}
    o_win = {c: window(*c) for c in chains}

    def gathered(k, h, ids, n_slots):
        units = [ids[0, s].astype(jnp.int32) for s in range(n_slots)]
        starts = [pl.multiple_of(u * SEL_UNIT, SEL_UNIT) for u in units]
        keys = jnp.concatenate([ks_ref[h, pl.ds(st, SEL_UNIT), :] for st in starts], axis=0)
        s = _dot(keys, qa_scr[k, h])
        krow = lax.broadcasted_iota(jnp.int32, (SEL_UNIT, QW), 0)
        seen = jnp.concatenate(
            [st + krow <= jnp.where(ids[1, si] > 0.5, tqs[k], -1) for si, st in enumerate(starts)], axis=0)
        s = jnp.where(seen, s, NEG_INF)
        p = jnp.exp(s - jnp.max(s, axis=0, keepdims=True))
        vals = jnp.concatenate([vst_ref[h, :, pl.ds(st, SEL_UNIT)] for st in starts], axis=1)
        return _dot(vals, p.astype(BF16)) / jnp.maximum(jnp.sum(p, axis=0, keepdims=True), 1e-20)

    KC = SEL_KEY_CHUNK

    def swept(k, h):
        last = (t0s[k] + Q_BLOCK + KC - 1) // KC - 1

        def scores_to(ref, j):
            k0 = pl.multiple_of(j * KC, KC)
            ref[...] = _dot(ks_ref[h, pl.ds(k0, KC), :], qa_scr[k, h])

        def accumulate(carry, j, s, p_of):
            m, l, acc = carry
            k0 = pl.multiple_of(j * KC, KC)
            m_new = jnp.maximum(m, jnp.max(s, axis=0, keepdims=True))
            alpha = jnp.exp(m - m_new)
            p = p_of(jnp.exp(s - m_new))
            l = alpha * l + jnp.sum(p, axis=0, keepdims=True)
            acc = alpha * acc + _dot(vst_ref[h, :, pl.ds(k0, KC)], p.astype(BF16))
            return m_new, l, acc

        keep = lambda p: p

        def two_chunks(i, state):
            j = 2 * i
            scores_to(sb_scr, j + 1)
            state = accumulate(state, j, sa_scr[...], keep)
            scores_to(sa_scr, j + 2)
            return accumulate(state, j + 1, sb_scr[...], keep)

        def odd_chunk(state):
            state = accumulate(state, last - 1, sa_scr[...], keep)
            scores_to(sa_scr, last)
            return state

        scores_to(sa_scr, 0)
        init = (jnp.full((1, QW), NEG_INF, F32), jnp.zeros((1, QW), F32), jnp.zeros((HEAD_DIM, QW), F32))
        state = lax.fori_loop(0, last // 2, two_chunks, init)
        state = lax.cond(last % 2 == 1, odd_chunk, lambda st: st, state)
        visible = (last * KC + lax.broadcasted_iota(jnp.int32, (KC, QW), 0)) <= tqs[k]
        _, l_s, acc_s = accumulate(state, last, jnp.where(visible, sa_scr[...], NEG_INF),
                                   lambda p: jnp.where(visible, p, 0.0))
        return acc_s / jnp.maximum(l_s, 1e-20)

    few, some = SEL_SLOTS
    for k in range(ATT_TILES):
        outs = []
        for h in range(N_KV_HEADS):
            ids, n_live = lists[k, h]
            o_s = lax.cond(n_live <= few, lambda _: gathered(k, h, ids, few),
                           lambda _: lax.cond(n_live <= some, lambda _: gathered(k, h, ids, some),
                                              lambda _: swept(k, h), 0), 0)
            gk = lambda br: g_ref[h, br:br + 1, cols(k)]
            o = gk(0) * cmp_out[k, h][0] + gk(1) * o_s + gk(2) * o_win[k, h]
            for g in range(0, GQA_GROUP, 2):
                sq = jnp.concatenate([o[:, g * Q_BLOCK:(g + 1) * Q_BLOCK], o[:, (g + 1) * Q_BLOCK:(g + 2) * Q_BLOCK]],
                                     axis=0)
                outs.append(sq.T)
        o_ref[k * Q_BLOCK:(k + 1) * Q_BLOCK, :] = jnp.concatenate(outs, axis=1).astype(BF16)


def _attend(qt, ks, vst, kwn, vwt, gt, kvc_c):
    B, _, S, _ = ks.shape
    nqb = S // Q_BLOCK
    QW = GQA_GROUP * Q_BLOCK
    n_sel = S // SEL_BLOCK
    n_pick = min(SEL_TOPK, n_sel)
    nr = S // CMP_STRIDE
    n_cmp = (S - CMP_BLOCK) // CMP_STRIDE + 1
    assert n_sel % 16 == 0 and n_sel <= AUG

    cpos = np.arange(nr) * CMP_STRIDE + CMP_BLOCK - 1
    digits = np.zeros((nr, AUG - HEAD_DIM), np.float32)
    digits[:, 0] = cpos // POS_BASE
    digits[:, 1] = cpos % POS_BASE
    kc = kvc_c[:, :N_KV_HEADS]
    kc = jnp.concatenate([kc, jnp.broadcast_to(jnp.asarray(digits, BF16), kc.shape[:2] + digits.shape)], axis=-1)
    vct = kvc_c[:, N_KV_HEADS:].transpose(0, 1, 3, 2)
    slopes = 2.0 ** (-8.0 * np.arange(1, N_HEADS + 1) / N_HEADS)
    slope_t = jnp.asarray(np.repeat(slopes.reshape(N_KV_HEADS, GQA_GROUP), Q_BLOCK, axis=1)
                          .reshape(N_KV_HEADS, 1, QW), F32)
    cmp_start = np.arange(n_cmp) * CMP_STRIDE
    sel_start = np.arange(n_sel) * SEL_BLOCK
    ov = ((cmp_start[:, None] <= sel_start[None, :] + SEL_BLOCK - 1)
          & (cmp_start[:, None] + CMP_BLOCK - 1 >= sel_start[None, :])).astype(np.float32)
    ovt = np.zeros((n_sel, nr), np.float32)
    ovt[:, :n_cmp] = ov.T
    ovt = jnp.asarray(ovt, BF16)

    H = N_KV_HEADS
    per_bh = lambda r, c: pl.BlockSpec((None, H, r, c), lambda b, i: (b, 0, 0, 0))
    per_q = lambda r: pl.BlockSpec((None, H, r, ATT_TILES * QW), lambda b, i: (b, 0, 0, i))
    return pl.pallas_call(
        functools.partial(_attend_kernel, n_sel=n_sel, n_pick=n_pick, n_cmp_rows=nr),
        grid=(B, nqb // ATT_TILES),
        in_specs=[per_q(HEAD_DIM), per_bh(nr, AUG), per_bh(HEAD_DIM, nr),
                  per_bh(S, 2 * AUG), per_bh(HEAD_DIM, S), per_bh(S, AUG), per_bh(HEAD_DIM, S),
                  per_q(3),
                  pl.BlockSpec((H, 1, QW), lambda b, i: (0, 0, 0)),
                  pl.BlockSpec((n_sel, nr), lambda b, i: (0, 0))],
        out_specs=pl.BlockSpec((None, ATT_TILES * Q_BLOCK, NSA_WIDTH), lambda b, i: (b, i, 0)),
        out_shape=jax.ShapeDtypeStruct((B, S, NSA_WIDTH), BF16),
        scratch_shapes=[pltpu.VMEM((ATT_TILES, H, 2 * AUG, QW), BF16), pltpu.VMEM((SEL_KEY_CHUNK, QW), F32),
                        pltpu.VMEM((SEL_KEY_CHUNK, QW), F32)],
        compiler_params=_params(("arbitrary", "arbitrary")),
        name="attend",
    )(qt, kc, vct, ks, vst, kwn, vwt, gt, slope_t, ovt)


def _merge_kernel(x_ref, mod_ref, ys_ref, on_ref, gm_ref, wglu_ref, bglu_ref, wa_ref, wb_ref, wo_ref,
                  gain_ref, wrh_ref, wrl_ref, x1_ref, h2_ref, sc_ref):
    D = D_MODEL
    z = jax.nn.gelu(_from_chunk_major(lambda groups, lanes: ys_ref[:, groups, lanes], x_ref.shape[0]))
    glu = z * jax.nn.sigmoid(_dot(z.astype(BF16), wglu_ref[...]) + bglu_ref[...])
    ya = _dot(glu.astype(BF16), wa_ref[...])
    yb = _dot(on_ref[...], wb_ref[...])
    merged = gm_ref[:, :D].astype(F32) * ya + gm_ref[:, D:].astype(F32) * yb
    x1 = x_ref[...] + mod_ref[2:3, :] * _dot(merged.astype(BF16), wo_ref[...])
    x1_ref[...] = x1
    y = x1 * lax.rsqrt(jnp.mean(x1 * x1, axis=-1, keepdims=True) + RMS_EPS)
    h2 = (y * gain_ref[...]) * (1.0 + mod_ref[4:5, :]) + mod_ref[3:4, :]
    h2_ref[...] = h2.astype(BF16)
    hh, hl = _split(h2)
    nt = (((1,), (1,)), ((), ()))
    dg = lambda a, b: lax.dot_general(a, b, nt, preferred_element_type=F32)
    logits = dg(wrh_ref[...], hh) + (dg(wrh_ref[...], hl) + dg(wrl_ref[...], hh))
    sc_ref[...] = jax.nn.sigmoid(logits)


def _merge(x, mod, ys5, o_nsa, gm, w_glu, b_glu, w_a, w_b, w_out, gain_f, w_router):
    B, S, D = x.shape
    tm = min(ROW_TILE, S)
    wrh, wrl = _split(w_router.T)
    ws = [w_glu.astype(BF16), b_glu.reshape(1, -1), w_a.astype(BF16), w_b.astype(BF16), w_out.astype(BF16),
          gain_f, wrh, wrl]
    row = lambda w: pl.BlockSpec((None, tm, w), lambda b, i: (b, i, 0))
    full = lambda a: pl.BlockSpec(a.shape, lambda b, i: (0,) * a.ndim)
    return pl.pallas_call(
        _merge_kernel,
        grid=(B, S // tm),
        in_specs=[row(D), pl.BlockSpec((None, N_MOD, D), lambda b, i: (b, 0, 0)),
                  _s5_chunk_spec(tm), row(NSA_WIDTH), row(2 * D)] + [full(a) for a in ws],
        out_specs=[row(D), row(D), pl.BlockSpec((None, N_EXPERTS, tm), lambda b, i: (b, 0, i))],
        out_shape=[jax.ShapeDtypeStruct((B, S, D), F32), jax.ShapeDtypeStruct((B, S, D), BF16),
                   jax.ShapeDtypeStruct((B, N_EXPERTS, S), F32)],
        compiler_params=_params(("arbitrary", "arbitrary")),
        name="merge",
    )(x, mod, ys5, o_nsa, gm, *ws)


def _first_argmax_rows(v, idx, n):
    m = jnp.max(v, axis=0, keepdims=True)
    first = jnp.min(jnp.where(v == m, idx, n), axis=0, keepdims=True)
    return idx == first, m


def _route_kernel(sc_ref, bias_ref, tri_ref, rank_ref, w_ref, cnt_ref):
    E, NG = N_EXPERTS, N_EXPERT_GROUPS
    GS = E // NG
    sc = sc_ref[...]
    TM = sc.shape[1]
    sel = sc + bias_ref[...]
    i8 = lax.broadcasted_iota(jnp.int32, (GS, TM), 0)
    gscore = []
    for g in range(NG):
        blk = sel[g * GS:(g + 1) * GS, :]
        hit, m1 = _first_argmax_rows(blk, i8, GS)
        m2 = jnp.max(jnp.where(hit, -jnp.inf, blk), axis=0, keepdims=True)
        gscore.append(m1 + m2)
    gscore = jnp.concatenate(gscore, axis=0)
    ig = lax.broadcasted_iota(jnp.int32, (NG, TM), 0)
    gmask = jnp.zeros((NG, TM), F32)
    for _ in range(TOPK_EXPERT_GROUPS):
        hit, _m = _first_argmax_rows(gscore, ig, NG)
        gmask = jnp.where(hit, 1.0, gmask)
        gscore = jnp.where(hit, -jnp.inf, gscore)
    emask = jnp.concatenate([jnp.broadcast_to(gmask[g:g + 1, :], (GS, TM)) for g in range(NG)], axis=0)
    cand = jnp.where(emask > 0.5, sel, NEG_INF)
    ie = lax.broadcasted_iota(jnp.int32, (E, TM), 0)
    chosen = jnp.zeros((E, TM), F32)
    for _ in range(TOP_K):
        hit, _m = _first_argmax_rows(cand, ie, E)
        chosen = jnp.where(hit, 1.0, chosen)
        cand = jnp.where(hit, -jnp.inf, cand)
    w = chosen * sc
    w = w / jnp.sum(w, axis=0, keepdims=True) * ROUTED_SCALE
    sub = rank_ref.shape[-1]
    for q in range(TM // sub):
        cols = slice(q * sub, (q + 1) * sub)
        cb = chosen[:, cols].astype(BF16)
        prefix = _dot(cb, tri_ref[...])
        rank_ref[q] = jnp.where(chosen[:, cols] > 0.5, prefix, RANK_NONE)
        w_ref[q] = w[:, cols]
        cnt_ref[q] = _dot(cb, jnp.ones((sub, LANES), BF16))


def _route(scores_t, router_bias):
    B, E, S = scores_t.shape
    TM = min(ROUTE_TILE, S)
    per_step = min(MOE_OUTER, S) // TM
    nj = S // (TM * per_step)
    ns = B * nj * per_step
    tri = jnp.asarray(np.triu(np.ones((TM, TM), np.float32), k=1), BF16)
    tile = lambda w: pl.BlockSpec((per_step, E, w), lambda b, j: (b * nj + j, 0, 0))
    return pl.pallas_call(
        _route_kernel,
        grid=(B, nj),
        in_specs=[pl.BlockSpec((None, E, TM * per_step), lambda b, j: (b, 0, j)),
                  pl.BlockSpec((E, 1), lambda b, j: (0, 0)),
                  pl.BlockSpec((TM, TM), lambda b, j: (0, 0))],
        out_specs=[tile(TM), tile(TM), tile(LANES)],
        out_shape=[jax.ShapeDtypeStruct((ns, E, TM), F32), jax.ShapeDtypeStruct((ns, E, TM), F32),
                   jax.ShapeDtypeStruct((ns, E, LANES), F32)],
        compiler_params=_params(("arbitrary", "arbitrary")),
        name="route",
    )(scores_t, router_bias.reshape(E, 1), tri)


def _moe_kernel(cnt_ref, h_ref, x1_ref, mod_ref, rank_ref, w_ref, wg_ref, wu_ref, wd_ref,
                sg_ref, su_ref, sd_ref, o_ref, xc, yc, pc, wc, *, n_sub, tm, n_exp):
    to = pl.program_id(0)
    eb = pl.program_id(1)
    gate = mod_ref[5:6, :]

    @pl.when(eb == 0)
    def _shared():
        for s in range(n_sub):
            rows = pl.ds(s * tm, tm)
            hs = h_ref[rows, :]
            hid = jax.nn.silu(_dot(hs, sg_ref[...])) * _dot(hs, su_ref[...])
            o_ref[rows, :] = x1_ref[rows, :] + gate * _dot(hid.astype(BF16), sd_ref[...])
            xc[s] = jnp.zeros(xc.shape[1:], BF16)
            yc[s] = jnp.zeros(yc.shape[1:], BF16)
            wc[s] = jnp.zeros(wc.shape[1:], F32)

    PIECE, WIN = MOE_PIECE, MOE_WIN
    tn = (((0,), (0,)), ((), ()))
    experts = [eb * n_exp + j for j in range(n_exp)]
    cnt = [[cnt_ref[(to * n_sub + s) * N_EXPERTS + e] for e in experts] for s in range(n_sub)]
    off = []
    for s in range(n_sub):
        o = [jnp.int32(0)]
        for j in range(n_exp):
            o.append(o[-1] + ((cnt[s][j] + 15) // 16) * 16)
        off.append(o)
    n_pieces = [(off[s][n_exp] + PIECE - 1) // PIECE for s in range(n_sub)]
    fits = n_pieces[0] < MOE_ROWS // PIECE
    for s in range(1, n_sub):
        fits = jnp.logical_and(fits, n_pieces[s] < MOE_ROWS // PIECE)

    def one_hot_rows(s, base, rows_n, js):
        slot = (lax.broadcasted_iota(jnp.int32, (rows_n, tm), 0) + base).astype(F32)
        pick = jnp.zeros((rows_n, tm), F32)
        wacc = jnp.zeros((rows_n, tm), F32)
        for j in js:
            hit = (slot - off[s][j].astype(F32)) == rank_ref[s, pl.ds(experts[j], 1), :]
            pick = jnp.where(hit, 1.0, pick)
            wacc = jnp.where(hit, w_ref[s, pl.ds(experts[j], 1), :], wacc)
        return pick.astype(BF16), jnp.sum(wacc, axis=1, keepdims=True)

    def mlp(xg, j):
        hid = jax.nn.silu(_dot(xg, wg_ref[j])) * _dot(xg, wu_ref[j])
        return _dot(hid.astype(BF16), wd_ref[j])

    def piece(s, q):
        r = pl.ds(q * PIECE, PIECE)
        pick, wrow = one_hot_rows(s, q * PIECE, PIECE, range(n_exp))
        xc[s, r, :] = _dot(pick, h_ref[pl.ds(s * tm, tm), :]).astype(BF16)
        pc[s, :, r] = pick.T
        wc[s, r, :] = wrow

    def window(j, p, first):
        starts = [pl.multiple_of(jnp.minimum(off[s][j] + p * WIN, (n_pieces[s] + 1) * PIECE - WIN), 16)
                  for s in range(n_sub)]
        xg = jnp.concatenate([xc[s, pl.ds(starts[s], WIN), :] for s in range(n_sub)], axis=0)
        out = mlp(xg, j)
        for s in range(n_sub):
            r = pl.ds(starts[s], WIN)
            new = (out[s * WIN:(s + 1) * WIN] * wc[s, r, :] * gate).astype(BF16)
            if not first:
                rank = starts[s] - off[s][j] + lax.broadcasted_iota(jnp.int32, (WIN, 1), 0)
                new = jnp.where(rank < cnt[s][j], new, yc[s, r, :])
            yc[s, r, :] = new

    def scatter(s, start, n):
        r = pl.ds(start, n)
        o_ref[pl.ds(s * tm, tm), :] += _dot(pc[s, :, r], yc[s, r, :])

    @pl.when(fits)
    def _packed():
        for s in range(n_sub):
            piece(s, 0)
            piece(s, 1)
        for s in range(n_sub):
            @pl.when(n_pieces[s] > 2)
            def _third(s=s):
                piece(s, 2)
        for j in range(n_exp):
            window(j, 0, True)
        for j in range(n_exp):
            most = cnt[0][j]
            for s in range(1, n_sub):
                most = jnp.maximum(most, cnt[s][j])

            @pl.when(most > WIN)
            def _more(j=j, most=most):
                lax.fori_loop(1, (most + WIN - 1) // WIN, lambda p, c: (window(j, p, False), c)[1], 0)
        for s in range(n_sub):
            scatter(s, 0, 2 * PIECE)
        for s in range(n_sub):
            @pl.when(n_pieces[s] > 2)
            def _third_out(s=s):
                scatter(s, 2 * PIECE, PIECE)

    @pl.when(jnp.logical_not(fits))
    def _unpacked():
        for j in range(n_exp):
            for s in range(n_sub):
                rows = pl.ds(s * tm, tm)

                def block(bi, carry, j=j, s=s, rows=rows):
                    slot0 = off[s][j] + bi * PIECE
                    pick, wrow = one_hot_rows(s, slot0, PIECE, [j])
                    out = mlp(_dot(pick, h_ref[rows, :]).astype(BF16), j)
                    ow = (out * wrow * gate).astype(BF16)
                    o_ref[rows, :] += lax.dot_general(pick, ow, tn, preferred_element_type=F32)
                    return carry

                lax.fori_loop(0, (cnt[s][j] + PIECE - 1) // PIECE, block, 0)


def _moe(h2, x1, mod, rank_t, w_t, counts, w_gate, w_up, w_down, ws_gate, ws_up, ws_down, seq):
    T, D = h2.shape
    ns, E, tm = rank_t.shape
    tmo = min(MOE_OUTER, seq)
    n_sub = tmo // tm
    n_exp = MOE_EXPERTS_PER_STEP
    per_b = seq // tmo
    ws = [w_gate.astype(BF16), w_up.astype(BF16), w_down.astype(BF16)]
    sh = [ws_gate.astype(BF16), ws_up.astype(BF16), ws_down.astype(BF16)]
    tok = lambda: pl.BlockSpec((tmo, D), lambda t, e, c: (t, 0))
    sub = lambda: pl.BlockSpec((n_sub, E, tm), lambda t, e, c: (t, 0, 0))
    exp = lambda a: pl.BlockSpec((n_exp,) + a.shape[1:], lambda t, e, c: (e, 0, 0))
    full = lambda a: pl.BlockSpec(a.shape, lambda t, e, c: (0, 0))
    grid_spec = pltpu.PrefetchScalarGridSpec(
        num_scalar_prefetch=1,
        grid=(T // tmo, E // n_exp),
        in_specs=[tok(), tok(), pl.BlockSpec((None, N_MOD, D), lambda t, e, c: (t // per_b, 0, 0)),
                  sub(), sub(), exp(ws[0]), exp(ws[1]), exp(ws[2]), full(sh[0]), full(sh[1]), full(sh[2])],
        out_specs=tok(),
        scratch_shapes=[pltpu.VMEM((n_sub, MOE_ROWS, D), BF16), pltpu.VMEM((n_sub, MOE_ROWS, D), BF16),
                        pltpu.VMEM((n_sub, tm, MOE_ROWS), BF16), pltpu.VMEM((n_sub, MOE_ROWS, 1), F32)],
    )
    return pl.pallas_call(
        functools.partial(_moe_kernel, n_sub=n_sub, tm=tm, n_exp=n_exp),
        grid_spec=grid_spec,
        out_shape=jax.ShapeDtypeStruct((T, D), F32),
        compiler_params=_params(("arbitrary", "arbitrary")),
        name="moe",
    )(counts, h2, x1, mod, rank_t, w_t, *ws, *sh)


def kernel(x, c, w_ada, b_ada, norm_mix_gain, norm_ffn_gain, w_in, s5_lambda_re, s5_lambda_im, s5_log_dt, s5_b_re, s5_b_im, s5_c_re, s5_c_im, s5_d, s5_w_glu, s5_b_glu, q_norm_gain, k_norm_gain, cmp_pe, cmp_w1, cmp_b1, cmp_w2, cmp_b2, w_branch_a, w_branch_b, w_out, w_router, router_bias, w_gate, w_up, w_down, ws_gate, ws_up, ws_down):
    B, S, D = x.shape
    for l in range(w_ada.shape[0]):
        mod = _ada(c, w_ada[l], b_ada[l]).reshape(B, N_MOD, D)
        u, kvc, qt, ks, vst, kwn, vwt, gt, gm = _inproj(x, mod, norm_mix_gain[l:l + 1], w_in[l],
                                                        q_norm_gain[l], k_norm_gain[l])
        tables = _s5_tables(s5_lambda_re[l], s5_lambda_im[l], s5_log_dt[l], s5_b_re[l], s5_b_im[l],
                            s5_c_re[l], s5_c_im[l], s5_d[l])
        ys5 = _s5(u, tables)
        kvc_c = _compress(kvc, cmp_pe[l], cmp_w1[l], cmp_b1[l], cmp_w2[l], cmp_b2[l], k_norm_gain[l, 0])
        o_nsa = _attend(qt, ks, vst, kwn, vwt, gt, kvc_c)
        x1, h2, scores_t = _merge(x, mod, ys5, o_nsa, gm, s5_w_glu[l], s5_b_glu[l], w_branch_a[l],
                                  w_branch_b[l], w_out[l], norm_ffn_gain[l:l + 1], w_router[l])
        rank_t, w_t, cnt = _route(scores_t, router_bias[l])
        counts = cnt[:, :, 0].astype(jnp.int32).reshape(-1)
        x = _moe(h2.reshape(B * S, D), x1.reshape(B * S, D), mod, rank_t, w_t, counts,
                 w_gate[l], w_up[l], w_down[l], ws_gate[l], ws_up[l], ws_down[l], S).reshape(B, S, D)
    return x
```

```python
import functools

import numpy as np
import jax
import jax.numpy as jnp
from jax import lax
from jax.experimental import pallas as pl
from jax.experimental.pallas import tpu as pltpu

F32 = jnp.float32
BF16 = jnp.bfloat16

D_MODEL = 1024
S5_WIDTH = 512
S5_GROUP = 16
S5_GROUPS = S5_WIDTH // S5_GROUP
S5_STATE = 64
N_HEADS = 8
N_KV_HEADS = 2
GQA_GROUP = N_HEADS // N_KV_HEADS
HEAD_DIM = 64
NSA_WIDTH = N_HEADS * HEAD_DIM
KV_WIDTH = 2 * N_KV_HEADS * HEAD_DIM
CMP_BLOCK = 32
CMP_STRIDE = 16
CMP_HIDDEN = 256
SEL_BLOCK = 64
SEL_TOPK = 8
WINDOW = 256
Q_BLOCK = 128
FORCE_BONUS = 1e3
N_EXPERTS = 64
TOP_K = 8
N_EXPERT_GROUPS = 8
TOPK_EXPERT_GROUPS = 4
ROUTED_SCALE = 2.5
RMS_EPS = 1e-6
NEG_INF = -1e30
N_MOD = 6

LANES = 128
SUBLANES = 8
S5_CHUNK = 16
S5_CW = S5_CHUNK * S5_GROUP
ROW_TILE = 512
ATT_TILES = 2
SEL_KEY_CHUNK = 512
SEL_UNIT = 128
SEL_SLOTS = (5, 8)
POS_BASE = 64
POS_ROWS = 16
AUG = 128
ROUTE_TILE = 256
MOE_OUTER = 1024
MOE_EXPERTS_PER_STEP = 8
MOE_PIECE = 128
MOE_BASE_PIECES = 3
MOE_WIN = 48
MOE_ROWS = 640
RANK_NONE = -float(1 << 20)
VMEM_LIMIT = 60 * 1024 * 1024


def _dot(a, b):
    return jnp.dot(a, b, preferred_element_type=F32)


def _split(a):
    hi = a.astype(BF16)
    lo = (a - hi.astype(F32)).astype(BF16)
    return hi, lo


def _dot3(a, bh, bl):
    ah, al = _split(a)
    return _dot(ah, bh) + (_dot(al, bh) + _dot(ah, bl))


def _segment_transpose(x):
    assert x.shape[-2:] == (SUBLANES, LANES) and LANES // S5_GROUP == SUBLANES
    nd = x.ndim
    i = lax.broadcasted_iota(jnp.int32, x.shape, nd - 2)
    seg = lax.broadcasted_iota(jnp.int32, x.shape, nd - 1) // S5_GROUP
    out = x
    for d in range(1, SUBLANES):
        r = pltpu.roll(pltpu.roll(x, SUBLANES - d, axis=nd - 2), S5_GROUP * d, axis=nd - 1)
        out = jnp.where(seg == ((i + d) & (SUBLANES - 1)), r, out)
    return out


def _to_chunk_major(u, put):
    rows = u.shape[0]
    u3 = u.reshape(rows // SUBLANES, SUBLANES, S5_WIDTH)
    halves = S5_CHUNK // SUBLANES
    for jb in range(S5_WIDTH // LANES):
        t = _segment_transpose(u3[:, :, LANES * jb:LANES * (jb + 1)])
        t = t.reshape(rows // S5_CHUNK, halves, SUBLANES, LANES)
        for hf in range(halves):
            put(slice(SUBLANES * jb, SUBLANES * (jb + 1)), slice(LANES * hf, LANES * (hf + 1)), t[:, hf])


def _from_chunk_major(get, rows):
    halves = S5_CHUNK // SUBLANES
    cols = []
    for jb in range(S5_WIDTH // LANES):
        parts = [_segment_transpose(get(slice(SUBLANES * jb, SUBLANES * (jb + 1)), slice(LANES * hf, LANES * (hf + 1))))
                 for hf in range(halves)]
        cols.append(jnp.stack(parts, axis=1).reshape(rows, LANES))
    return jnp.concatenate(cols, axis=1)


def _params(sem):
    return pltpu.CompilerParams(dimension_semantics=sem, vmem_limit_bytes=VMEM_LIMIT)


def _ada_kernel(c_ref, w_ref, b_ref, o_ref):
    cs = jax.nn.silu(c_ref[...])
    wh, wl = _split(w_ref[...])
    o_ref[...] = _dot3(cs, wh, wl) + b_ref[...]


def _ada(c, w_ada, b_ada):
    B, D = c.shape
    return pl.pallas_call(
        _ada_kernel,
        grid=(N_MOD,),
        in_specs=[pl.BlockSpec((B, D), lambda j: (0, 0)),
                  pl.BlockSpec((D, D), lambda j: (0, j)),
                  pl.BlockSpec((1, D), lambda j: (0, j))],
        out_specs=pl.BlockSpec((B, D), lambda j: (0, j)),
        out_shape=jax.ShapeDtypeStruct((B, N_MOD * D), F32),
        compiler_params=_params(("arbitrary",)),
        name="ada",
    )(c, w_ada, b_ada.reshape(1, N_MOD * D))


def _head_norm(v, bd, gain):
    sq = v * v
    sh, sl = _split(sq)
    ms = _dot(sh, bd) + _dot(sl, bd)
    return v * lax.rsqrt(ms + RMS_EPS) * gain


def _inproj_kernel(x_ref, mod_ref, gain_ref, wm_ref, wt_ref, wgn_ref, wgm_ref, bd_ref, qg_ref, kg_ref,
                   u_ref, kvc_ref, qt_ref, ksa_ref, vst_ref, kwa_ref, vwt_ref, gt_ref, gm_ref):
    x = x_ref[...]
    shift = mod_ref[0:1, :]
    scale = mod_ref[1:2, :]
    y = x * lax.rsqrt(jnp.mean(x * x, axis=-1, keepdims=True) + RMS_EPS)
    h = (y * gain_ref[...]) * (1.0 + scale) + shift
    hb = h.astype(BF16)
    main = _dot(hb, wm_ref[...])
    tm = x.shape[0]

    def put(groups, lanes, block):
        u_ref[:, groups, lanes] = block

    _to_chunk_major(main[:, :S5_WIDTH], put)
    o = S5_WIDTH
    for part in range(KV_WIDTH // LANES):
        kvc_ref[part] = main[:, o + part * LANES:o + (part + 1) * LANES]
    o += KV_WIDTH
    kw = N_KV_HEADS * HEAD_DIM
    bd = bd_ref[...]
    ks = _head_norm(main[:, o:o + kw], bd[:kw, :kw], kg_ref[1:2, :]).astype(BF16)
    kwn = _head_norm(main[:, o + kw:o + 2 * kw], bd[:kw, :kw], kg_ref[2:3, :]).astype(BF16)
    pos = pl.program_id(1) * tm + lax.broadcasted_iota(jnp.int32, (tm, 1), 0)
    lane = lax.broadcasted_iota(jnp.int32, (tm, AUG), 1)
    digits = jnp.where(lane == 0, pos // POS_BASE, jnp.where(lane == 1, pos % POS_BASE, 0))
    digits = digits[:, :AUG - HEAD_DIM].astype(F32).astype(BF16)
    onehot = jnp.where(lane == pos // SEL_BLOCK, 1.0, 0.0).astype(BF16)
    for hh in range(N_KV_HEADS):
        ksa_ref[hh, :, 0:HEAD_DIM] = ks[:, hh * HEAD_DIM:(hh + 1) * HEAD_DIM]
        ksa_ref[hh, :, HEAD_DIM:AUG] = digits
        ksa_ref[hh, :, AUG:] = onehot
        kwa_ref[hh, :, 0:HEAD_DIM] = kwn[:, hh * HEAD_DIM:(hh + 1) * HEAD_DIM]
        kwa_ref[hh, :, HEAD_DIM:] = digits

    nt = (((1,), (1,)), ((), ()))
    tt = lax.dot_general(wt_ref[...], hb, nt, preferred_element_type=F32)
    qt = tt[:NSA_WIDTH]
    sq = qt * qt
    sh, sl = _split(sq)
    qn = (qt * lax.rsqrt(_dot(bd, sh) + _dot(bd, sl) + RMS_EPS) * qg_ref[...] * (HEAD_DIM ** -0.5)).astype(BF16)
    QW = GQA_GROUP * Q_BLOCK
    for qb in range(tm // Q_BLOCK):
        for hd in range(N_HEADS):
            hh, g = divmod(hd, GQA_GROUP)
            qt_ref[hh, :, qb * QW + g * Q_BLOCK:qb * QW + (g + 1) * Q_BLOCK] = (
                qn[hd * HEAD_DIM:(hd + 1) * HEAD_DIM, qb * Q_BLOCK:(qb + 1) * Q_BLOCK])
    vst_ref[...] = tt[NSA_WIDTH:NSA_WIDTH + kw].reshape(N_KV_HEADS, HEAD_DIM, tm).astype(BF16)
    vwt_ref[...] = tt[NSA_WIDTH + kw:].reshape(N_KV_HEADS, HEAD_DIM, tm).astype(BF16)
    gn = jax.nn.sigmoid(lax.dot_general(wgn_ref[...], hb, nt, preferred_element_type=F32))
    for qb in range(tm // Q_BLOCK):
        for br in range(3):
            for hd in range(N_HEADS):
                hh, g = divmod(hd, GQA_GROUP)
                r = br * N_HEADS + hd
                gt_ref[hh, br:br + 1, qb * QW + g * Q_BLOCK:qb * QW + (g + 1) * Q_BLOCK] = (
                    gn[r:r + 1, qb * Q_BLOCK:(qb + 1) * Q_BLOCK])
    gm_ref[...] = jax.nn.sigmoid(_dot(hb, wgm_ref[...])).astype(BF16)


def _s5_chunk_spec(tm):
    assert S5_CHUNK == 16 and S5_GROUP == 16, "the segment transposes assume 16 steps x 16 channels"
    return pl.BlockSpec((tm // S5_CHUNK, None, S5_GROUPS, S5_CW), lambda b, i: (i, b, 0, 0))


def _inproj(x, mod, gain, w_in, q_gain, k_gain):
    B, S, D = x.shape
    tm = min(ROW_TILE, S)
    assert S // SEL_BLOCK <= AUG and tm % Q_BLOCK == 0
    kw = N_KV_HEADS * HEAD_DIM
    cols = np.cumsum((0,) + (S5_WIDTH, NSA_WIDTH, KV_WIDTH, KV_WIDTH, KV_WIDTH, 3 * N_HEADS, 2 * D))
    c_u, c_q, c_kvc, c_kvs, c_kvw, c_gn, c_gm = cols[:7]
    sl = lambda a, n: w_in[:, a:a + n]
    wm = jnp.concatenate([sl(c_u, S5_WIDTH), sl(c_kvc, KV_WIDTH), sl(c_kvs, kw), sl(c_kvw, kw)], axis=1).astype(BF16)
    wt = jnp.concatenate([sl(c_q, NSA_WIDTH), sl(c_kvs + kw, kw), sl(c_kvw + kw, kw)], axis=1).T.astype(BF16)
    wgn = sl(c_gn, 3 * N_HEADS).T.astype(BF16)
    wgm = sl(c_gm, 2 * D).astype(BF16)
    seg = np.arange(NSA_WIDTH) // HEAD_DIM
    bd = jnp.asarray((seg[:, None] == seg[None, :]).astype(np.float32) / HEAD_DIM, BF16)
    qg = jnp.tile(q_gain, N_HEADS).reshape(NSA_WIDTH, 1)
    kg = jnp.tile(k_gain, (1, N_KV_HEADS))
    nq = tm // Q_BLOCK * GQA_GROUP * Q_BLOCK
    row = lambda w: pl.BlockSpec((None, tm, w), lambda b, i: (b, i, 0))
    full = lambda a: pl.BlockSpec(a.shape, lambda b, i: (0,) * a.ndim)
    rows4 = lambda w: pl.BlockSpec((None, N_KV_HEADS, tm, w), lambda b, i: (b, 0, i, 0))
    cols4 = lambda r, w: pl.BlockSpec((None, N_KV_HEADS, r, w), lambda b, i: (b, 0, 0, i))
    nqt = S // Q_BLOCK * GQA_GROUP * Q_BLOCK
    return pl.pallas_call(
        _inproj_kernel,
        grid=(B, S // tm),
        in_specs=[row(D), pl.BlockSpec((None, N_MOD, D), lambda b, i: (b, 0, 0)),
                  full(gain), full(wm), full(wt), full(wgn), full(wgm), full(bd), full(qg), full(kg)],
        out_specs=[_s5_chunk_spec(tm), pl.BlockSpec((None, KV_WIDTH // LANES, tm, LANES), lambda b, i: (b, 0, i, 0)),
                   cols4(HEAD_DIM, nq), rows4(2 * AUG), cols4(HEAD_DIM, tm),
                   rows4(AUG), cols4(HEAD_DIM, tm), cols4(3, nq), row(2 * D)],
        out_shape=[jax.ShapeDtypeStruct((S // S5_CHUNK, B, S5_GROUPS, S5_CW), F32),
                   jax.ShapeDtypeStruct((B, KV_WIDTH // LANES, S, LANES), F32),
                   jax.ShapeDtypeStruct((B, N_KV_HEADS, HEAD_DIM, nqt), BF16),
                   jax.ShapeDtypeStruct((B, N_KV_HEADS, S, 2 * AUG), BF16),
                   jax.ShapeDtypeStruct((B, N_KV_HEADS, HEAD_DIM, S), BF16),
                   jax.ShapeDtypeStruct((B, N_KV_HEADS, S, AUG), BF16),
                   jax.ShapeDtypeStruct((B, N_KV_HEADS, HEAD_DIM, S), BF16),
                   jax.ShapeDtypeStruct((B, N_KV_HEADS, 3, nqt), F32),
                   jax.ShapeDtypeStruct((B, S, 2 * D), BF16)],
        compiler_params=_params(("arbitrary", "arbitrary")),
        name="inproj",
    )(x, mod, gain, wm, wt, wgn, wgm, bd, qg, kg)


def _s5_tables(lam_re, lam_im, log_dt, b_re, b_im, c_re, c_im, d_skip):
    L, G = S5_CHUNK, S5_GROUPS
    lr, li = lam_re.astype(F32), lam_im.astype(F32)
    dt = jnp.exp(log_dt.astype(F32))[:, None]
    mag = jnp.exp(lr * dt)
    abar_re, abar_im = mag * jnp.cos(li * dt), mag * jnp.sin(li * dt)
    num_re, num_im = abar_re - 1.0, abar_im
    den = lr * lr + li * li
    coef_re = (num_re * lr + num_im * li) / den
    coef_im = (num_im * lr - num_re * li) / den
    br, bi = b_re.astype(F32), b_im.astype(F32)
    bbar_re = coef_re[..., None] * br - coef_im[..., None] * bi
    bbar_im = coef_re[..., None] * bi + coef_im[..., None] * br
    k = jnp.arange(L + 1, dtype=F32)[:, None]
    pmag = jnp.exp((lr * dt)[:, None, :] * k)
    ang = (li * dt)[:, None, :] * k
    pw = jnp.stack([jnp.tile(pmag * jnp.cos(ang), (1, 1, 2)), jnp.tile(pmag * jnp.sin(ang), (1, 1, 2))], axis=1)
    brt, bit = bbar_re.transpose(0, 2, 1), bbar_im.transpose(0, 2, 1)
    ba = jnp.concatenate([brt, bit], axis=2)
    bb = jnp.concatenate([-bit, brt], axis=2)
    cr, ci = c_re.astype(F32), c_im.astype(F32)
    ca = jnp.concatenate([cr, -ci], axis=2)
    cb = jnp.concatenate([-ci, -cr], axis=2)
    dv = jnp.tile(d_skip.astype(F32), (1, L)).reshape(G, 1, S5_CW)
    return pw, ba, bb, ca, cb, dv


def _s5_kernel(u_ref, pw_ref, ba_ref, bb_ref, ca_ref, cb_ref, dv_ref,
               y_ref, vr_scr, vi_scr, xr_scr, xi_scr, *, n_chunks, bsz):
    L, P, N = S5_CHUNK, S5_GROUP, S5_STATE
    nt = (((1,), (1,)), ((), ()))
    dot_nt = lambda a, b: lax.dot_general(a, b, nt, preferred_element_type=F32)
    mix = lambda a, b, k: a * pw_ref[0, k:k + 1, :] + b * pw_ref[1, k:k + 1, :]
    ba, bb, ca, cb = ba_ref[...], bb_ref[...], ca_ref[...], cb_ref[...]
    ws = jnp.concatenate([mix(ba, bb, L - 1 - s) for s in range(L)], axis=0)
    wot = jnp.concatenate([mix(ca, cb, k) for k in range(L + 1)], axis=0)
    wsh, wsl = _split(ws)
    wth, wtl = _split(wot)
    bah, bal = _split(ba)
    kern = dot_nt(bah, wth[:L * P]) + (dot_nt(bal, wth[:L * P]) + dot_nt(bah, wtl[:L * P]))
    lane = lax.broadcasted_iota(jnp.int32, kern.shape, 1)
    mt = jnp.concatenate([kern] + [jnp.where(lane >= P * s, pltpu.roll(kern, P * s, axis=1), 0.0)
                                   for s in range(1, L)], axis=0)
    mth, mtl = _split(mt)

    u = u_ref[...]
    uh, ul = _split(u)
    y = _dot(uh, mth) + (_dot(ul, mth) + _dot(uh, mtl))
    v = _dot(uh, wsh) + (_dot(ul, wsh) + _dot(uh, wsl))
    vr_scr[...] = v[:, :N]
    vi_scr[...] = v[:, N:]
    a_r = pw_ref[0, L:L + 1, :N]
    a_i = pw_ref[1, L:L + 1, :N]

    def step(c, x):
        xr, xi = x
        r = pl.ds(pl.multiple_of(c * bsz, bsz), bsz)
        xr_scr[r, :] = xr
        xi_scr[r, :] = xi
        return (a_r * xr - a_i * xi + vr_scr[r, :], a_r * xi + a_i * xr + vi_scr[r, :])

    zero = jnp.zeros((bsz, N), F32)
    lax.fori_loop(0, n_chunks, step, (zero, zero), unroll=8)
    xh, xl = _split(jnp.concatenate([xr_scr[...], xi_scr[...]], axis=1))
    y = y + (dot_nt(xh, wth[P:]) + (dot_nt(xl, wth[P:]) + dot_nt(xh, wtl[P:])))
    y_ref[...] = y + dv_ref[...] * u


def _s5(ug, tables):
    nc, B, G, cw = ug.shape
    N = S5_STATE
    grp = lambda a: pl.BlockSpec((None,) + a.shape[1:], lambda g: (g,) + (0,) * (a.ndim - 1))
    rows = pl.BlockSpec((nc * B, cw), lambda g: (0, g))
    y = pl.pallas_call(
        functools.partial(_s5_kernel, n_chunks=nc, bsz=B),
        grid=(G,),
        in_specs=[rows] + [grp(t) for t in tables],
        out_specs=rows,
        out_shape=jax.ShapeDtypeStruct((nc * B, G * cw), F32),
        scratch_shapes=[pltpu.VMEM((nc * B, N), F32)] * 4,
        compiler_params=_params(("arbitrary",)),
        name="s5",
    )(ug.reshape(nc * B, G * cw), *tables)
    return y.reshape(nc, B, G, cw)


def _compress_kernel(x_ref, pe_ref, w1_ref, b1_ref, w2_ref, b2_ref, kg_ref, o_ref, *, n_rows):
    half = CMP_STRIDE * HEAD_DIM
    for j in range(2 * N_KV_HEADS):
        kv, hd = divmod(j, N_KV_HEADS)
        x = jnp.concatenate(
            [x_ref[kv, pl.ds(t, n_rows, stride=CMP_STRIDE), :][:, hd * HEAD_DIM:(hd + 1) * HEAD_DIM]
             for t in range(CMP_STRIDE)], axis=1)
        a = _dot((x + pe_ref[kv, 0:1, :]).astype(BF16), w1_ref[kv, :half, :])
        b = _dot((x + pe_ref[kv, 1:2, :]).astype(BF16), w1_ref[kv, half:, :])
        hid = jax.nn.gelu(a + pltpu.roll(b, n_rows - 1, axis=0) + b1_ref[kv])
        out = _dot(hid.astype(BF16), w2_ref[kv]) + b2_ref[kv]
        if kv == 0:
            out = out * lax.rsqrt(jnp.mean(out * out, axis=-1, keepdims=True) + RMS_EPS) * kg_ref[...]
        o_ref[j] = out.astype(BF16)


def _compress(kv_c, cmp_pe, cmp_w1, cmp_b1, cmp_w2, cmp_b2, k_gain0):
    B, _, S, _ = kv_c.shape
    nr = S // CMP_STRIDE
    half = CMP_STRIDE * HEAD_DIM
    nj = 2 * N_KV_HEADS
    ws = [cmp_pe.reshape(2, 2, half), cmp_w1.astype(BF16), cmp_b1.reshape(2, 1, CMP_HIDDEN), cmp_w2.astype(BF16),
          cmp_b2.reshape(2, 1, HEAD_DIM), k_gain0.reshape(1, HEAD_DIM)]
    full = lambda a: pl.BlockSpec(a.shape, lambda b: (0,) * a.ndim)
    return pl.pallas_call(
        functools.partial(_compress_kernel, n_rows=nr),
        grid=(B,),
        in_specs=[pl.BlockSpec((None,) + kv_c.shape[1:], lambda b: (b, 0, 0, 0))] + [full(a) for a in ws],
        out_specs=pl.BlockSpec((None, nj, nr, HEAD_DIM), lambda b: (b, 0, 0, 0)),
        out_shape=jax.ShapeDtypeStruct((B, nj, nr, HEAD_DIM), BF16),
        compiler_params=_params(("arbitrary",)),
        name="compress",
    )(kv_c, *ws)


def _softmax_cols(s, mask):
    s = jnp.where(mask, s, NEG_INF)
    m = jnp.max(s, axis=0, keepdims=True)
    p = jnp.where(mask, jnp.exp(s - m), 0.0)
    return p / jnp.maximum(jnp.sum(p, axis=0, keepdims=True), 1e-20)


def _attend_kernel(qt_ref, kc_ref, vct_ref, ks_ref, vst_ref, kw_ref, vwt_ref, g_ref, slope_ref, ovt_ref,
                   o_ref, qa_scr, sa_scr, sb_scr, *, n_sel, n_pick, n_cmp_rows):
    QW = GQA_GROUP * Q_BLOCK
    chains = [(k, h) for k in range(ATT_TILES) for h in range(N_KV_HEADS)]
    t0s = [(pl.program_id(1) * ATT_TILES + k) * Q_BLOCK for k in range(ATT_TILES)]
    tqs = [t0 + (lax.broadcasted_iota(jnp.int32, (1, QW), 1) & (Q_BLOCK - 1)) for t0 in t0s]
    cols = lambda k: slice(k * QW, (k + 1) * QW)

    r = lax.broadcasted_iota(jnp.int32, (POS_ROWS, QW), 0)
    qa = {}
    for k, h in chains:
        slope = slope_ref[h]
        qa_scr[k, h, 0:HEAD_DIM, :] = qt_ref[h, :, cols(k)]
        qa_scr[k, h, HEAD_DIM:HEAD_DIM + POS_ROWS, :] = jnp.where(
            r == 0, slope * POS_BASE, jnp.where(r == 1, slope, 0.0)).astype(BF16)
        qa_scr[k, h, HEAD_DIM + POS_ROWS:AUG, :] = jnp.zeros((AUG - HEAD_DIM - POS_ROWS, QW), BF16)
        qa[k, h] = qa_scr[k, h, 0:AUG, :]

    def compressed(k, h, rows):
        sc = _dot(kc_ref[h, 0:rows, :], qa[k, h])
        cpos = lax.broadcasted_iota(jnp.int32, (rows, QW), 0) * CMP_STRIDE + (CMP_BLOCK - 1)
        p_c = _softmax_cols(sc, cpos <= tqs[k])
        o_c = _dot(vct_ref[h, :, 0:rows], p_c.astype(BF16))
        psum = p_c[:, 0:Q_BLOCK]
        for g in range(1, GQA_GROUP):
            psum = psum + p_c[:, g * Q_BLOCK:(g + 1) * Q_BLOCK]
        ph, pl_ = _split(psum)
        return o_c, _dot(ovt_ref[:, 0:rows], ph) + _dot(ovt_ref[:, 0:rows], pl_)

    half = n_cmp_rows // 2
    if half % LANES == 0:
        early = (t0s[-1] + Q_BLOCK - CMP_BLOCK) // CMP_STRIDE < half
        cmp_out = lax.cond(early, lambda _: [compressed(k, h, half) for k, h in chains],
                           lambda _: [compressed(k, h, n_cmp_rows) for k, h in chains], 0)
    else:
        cmp_out = [compressed(k, h, n_cmp_rows) for k, h in chains]
    cmp_out = dict(zip(chains, cmp_out))

    def select(k, h, imp):
        jb = lax.broadcasted_iota(jnp.int32, (n_sel, Q_BLOCK), 0)
        cur = (t0s[k] + lax.broadcasted_iota(jnp.int32, (1, Q_BLOCK), 1)) // SEL_BLOCK
        forced = (jb == 0) | (jb == cur) | (jb == cur - 1)
        imp = jnp.where(forced, imp + FORCE_BONUS, imp)
        imp = jnp.where(jb <= cur, imp, -1.0)
        bias = jnp.full((n_sel, Q_BLOCK), NEG_INF, F32)
        for _ in range(n_pick):
            m = jnp.max(imp, axis=0, keepdims=True)
            first = jnp.min(jnp.where(imp == m, jb, n_sel), axis=0, keepdims=True)
            hit = jb == first
            bias = jnp.where(hit, 0.0, bias)
            imp = jnp.where(hit, -jnp.inf, imp)
        per_unit = SEL_UNIT // SEL_BLOCK
        n_units = n_sel // per_unit
        picked = jnp.where(bias == 0.0, 1.0, 0.0).astype(BF16)
        in_unit = (lax.broadcasted_iota(jnp.int32, (n_units, n_sel), 1) // per_unit
                   == lax.broadcasted_iota(jnp.int32, (n_units, n_sel), 0))
        hits = _dot(jnp.where(in_unit, 1.0, 0.0).astype(BF16), picked)
        live = jnp.where(jnp.max(hits, axis=1, keepdims=True) > 0.5, 1.0, 0.0)
        live = jnp.broadcast_to(live, (n_units, LANES))
        earlier = (lax.broadcasted_iota(jnp.int32, (n_units, n_units), 1)
                   < lax.broadcasted_iota(jnp.int32, (n_units, n_units), 0))
        before = _dot(jnp.where(earlier, 1.0, 0.0).astype(BF16), live.astype(BF16))
        slot = lax.broadcasted_iota(jnp.int32, (n_units, LANES), 1).astype(F32)
        in_slot = jnp.where(before == slot, live, 0.0).astype(BF16)
        r8 = lax.broadcasted_iota(jnp.int32, (SUBLANES, n_units), 0)
        c8 = lax.broadcasted_iota(jnp.int32, (SUBLANES, n_units), 1)
        pick_rows = jnp.where(r8 == 0, c8, jnp.where(r8 == 1, 1, 0)).astype(F32).astype(BF16)
        ids = _dot(pick_rows, in_slot)
        n_live = (before[n_units - 1:n_units, 0:1] + live[n_units - 1:n_units, 0:1])[0, 0]
        bias = jnp.concatenate([bias] * GQA_GROUP, axis=1).astype(BF16)
        qa_scr[k, h, AUG:, :] = jnp.concatenate([bias, jnp.zeros((AUG - n_sel, QW), BF16)], axis=0)
        return ids, n_live

    def window(k, h):
        WK = WINDOW + Q_BLOCK
        w0 = pl.multiple_of(jnp.maximum(t0s[k] - WINDOW, 0), Q_BLOCK)
        sw = _dot(kw_ref[h, pl.ds(w0, WK), :], qa[k, h])
        dist = tqs[k] - (w0 + lax.broadcasted_iota(jnp.int32, (WK, QW), 0))
        in_win = (dist | (WINDOW - 1 - dist)) >= 0
        p_w = _softmax_cols(sw, in_win)
        return _dot(vwt_ref[h, :, pl.ds(w0, WK)], p_w.astype(BF16))

    lists = {c: select(*c, cmp_out[c][1]) for c in chains}
    o_win = {c: window(*c) for c in chains}

    def gathered(k, h, ids, n_slots):
        units = [ids[0, s].astype(jnp.int32) for s in range(n_slots)]
        starts = [pl.multiple_of(u * SEL_UNIT, SEL_UNIT) for u in units]
        keys = jnp.concatenate([ks_ref[h, pl.ds(st, SEL_UNIT), :] for st in starts], axis=0)
        s = _dot(keys, qa_scr[k, h])
        krow = lax.broadcasted_iota(jnp.int32, (SEL_UNIT, QW), 0)
        seen = jnp.concatenate(
            [st + krow <= jnp.where(ids[1, si] > 0.5, tqs[k], -1) for si, st in enumerate(starts)], axis=0)
        s = jnp.where(seen, s, NEG_INF)
        p = jnp.exp(s - jnp.max(s, axis=0, keepdims=True))
        vals = jnp.concatenate([vst_ref[h, :, pl.ds(st, SEL_UNIT)] for st in starts], axis=1)
        return _dot(vals, p.astype(BF16)) / jnp.maximum(jnp.sum(p, axis=0, keepdims=True), 1e-20)

    KC = SEL_KEY_CHUNK

    def swept(k, h):
        last = (t0s[k] + Q_BLOCK + KC - 1) // KC - 1

        def scores_to(ref, j):
            k0 = pl.multiple_of(j * KC, KC)
            ref[...] = _dot(ks_ref[h, pl.ds(k0, KC), :], qa_scr[k, h])

        def accumulate(carry, j, s, p_of):
            m, l, acc = carry
            k0 = pl.multiple_of(j * KC, KC)
            m_new = jnp.maximum(m, jnp.max(s, axis=0, keepdims=True))
            alpha = jnp.exp(m - m_new)
            p = p_of(jnp.exp(s - m_new))
            l = alpha * l + jnp.sum(p, axis=0, keepdims=True)
            acc = alpha * acc + _dot(vst_ref[h, :, pl.ds(k0, KC)], p.astype(BF16))
            return m_new, l, acc

        keep = lambda p: p

        def two_chunks(i, state):
            j = 2 * i
            scores_to(sb_scr, j + 1)
            state = accumulate(state, j, sa_scr[...], keep)
            scores_to(sa_scr, j + 2)
            return accumulate(state, j + 1, sb_scr[...], keep)

        def odd_chunk(state):
            state = accumulate(state, last - 1, sa_scr[...], keep)
            scores_to(sa_scr, last)
            return state

        scores_to(sa_scr, 0)
        init = (jnp.full((1, QW), NEG_INF, F32), jnp.zeros((1, QW), F32), jnp.zeros((HEAD_DIM, QW), F32))
        state = lax.fori_loop(0, last // 2, two_chunks, init)
        state = lax.cond(last % 2 == 1, odd_chunk, lambda st: st, state)
        visible = (last * KC + lax.broadcasted_iota(jnp.int32, (KC, QW), 0)) <= tqs[k]
        _, l_s, acc_s = accumulate(state, last, jnp.where(visible, sa_scr[...], NEG_INF),
                                   lambda p: jnp.where(visible, p, 0.0))
        return acc_s / jnp.maximum(l_s, 1e-20)

    few, some = SEL_SLOTS
    for k in range(ATT_TILES):
        outs = []
        for h in range(N_KV_HEADS):
            ids, n_live = lists[k, h]
            o_s = lax.cond(n_live <= few, lambda _: gathered(k, h, ids, few),
                           lambda _: lax.cond(n_live <= some, lambda _: gathered(k, h, ids, some),
                                              lambda _: swept(k, h), 0), 0)
            gk = lambda br: g_ref[h, br:br + 1, cols(k)]
            o = gk(0) * cmp_out[k, h][0] + gk(1) * o_s + gk(2) * o_win[k, h]
            for g in range(0, GQA_GROUP, 2):
                sq = jnp.concatenate([o[:, g * Q_BLOCK:(g + 1) * Q_BLOCK], o[:, (g + 1) * Q_BLOCK:(g + 2) * Q_BLOCK]],
                                     axis=0)
                outs.append(sq.T)
        o_ref[k * Q_BLOCK:(k + 1) * Q_BLOCK, :] = jnp.concatenate(outs, axis=1).astype(BF16)


def _attend(qt, ks, vst, kwn, vwt, gt, kvc_c):
    B, _, S, _ = ks.shape
    nqb = S // Q_BLOCK
    QW = GQA_GROUP * Q_BLOCK
    n_sel = S // SEL_BLOCK
    n_pick = min(SEL_TOPK, n_sel)
    nr = S // CMP_STRIDE
    n_cmp = (S - CMP_BLOCK) // CMP_STRIDE + 1
    assert n_sel % 16 == 0 and n_sel <= AUG

    cpos = np.arange(nr) * CMP_STRIDE + CMP_BLOCK - 1
    digits = np.zeros((nr, AUG - HEAD_DIM), np.float32)
    digits[:, 0] = cpos // POS_BASE
    digits[:, 1] = cpos % POS_BASE
    kc = kvc_c[:, :N_KV_HEADS]
    kc = jnp.concatenate([kc, jnp.broadcast_to(jnp.asarray(digits, BF16), kc.shape[:2] + digits.shape)], axis=-1)
    vct = kvc_c[:, N_KV_HEADS:].transpose(0, 1, 3, 2)
    slopes = 2.0 ** (-8.0 * np.arange(1, N_HEADS + 1) / N_HEADS)
    slope_t = jnp.asarray(np.repeat(slopes.reshape(N_KV_HEADS, GQA_GROUP), Q_BLOCK, axis=1)
                          .reshape(N_KV_HEADS, 1, QW), F32)
    cmp_start = np.arange(n_cmp) * CMP_STRIDE
    sel_start = np.arange(n_sel) * SEL_BLOCK
    ov = ((cmp_start[:, None] <= sel_start[None, :] + SEL_BLOCK - 1)
          & (cmp_start[:, None] + CMP_BLOCK - 1 >= sel_start[None, :])).astype(np.float32)
    ovt = np.zeros((n_sel, nr), np.float32)
    ovt[:, :n_cmp] = ov.T
    ovt = jnp.asarray(ovt, BF16)

    H = N_KV_HEADS
    per_bh = lambda r, c: pl.BlockSpec((None, H, r, c), lambda b, i: (b, 0, 0, 0))
    per_q = lambda r: pl.BlockSpec((None, H, r, ATT_TILES * QW), lambda b, i: (b, 0, 0, i))
    return pl.pallas_call(
        functools.partial(_attend_kernel, n_sel=n_sel, n_pick=n_pick, n_cmp_rows=nr),
        grid=(B, nqb // ATT_TILES),
        in_specs=[per_q(HEAD_DIM), per_bh(nr, AUG), per_bh(HEAD_DIM, nr),
                  per_bh(S, 2 * AUG), per_bh(HEAD_DIM, S), per_bh(S, AUG), per_bh(HEAD_DIM, S),
                  per_q(3),
                  pl.BlockSpec((H, 1, QW), lambda b, i: (0, 0, 0)),
                  pl.BlockSpec((n_sel, nr), lambda b, i: (0, 0))],
        out_specs=pl.BlockSpec((None, ATT_TILES * Q_BLOCK, NSA_WIDTH), lambda b, i: (b, i, 0)),
        out_shape=jax.ShapeDtypeStruct((B, S, NSA_WIDTH), BF16),
        scratch_shapes=[pltpu.VMEM((ATT_TILES, H, 2 * AUG, QW), BF16), pltpu.VMEM((SEL_KEY_CHUNK, QW), F32),
                        pltpu.VMEM((SEL_KEY_CHUNK, QW), F32)],
        compiler_params=_params(("arbitrary", "arbitrary")),
        name="attend",
    )(qt, kc, vct, ks, vst, kwn, vwt, gt, slope_t, ovt)


def _merge_kernel(x_ref, mod_ref, ys_ref, on_ref, gm_ref, wglu_ref, bglu_ref, wa_ref, wb_ref, wo_ref,
                  gain_ref, wrh_ref, wrl_ref, x1_ref, h2_ref, sc_ref):
    D = D_MODEL
    z = jax.nn.gelu(_from_chunk_major(lambda groups, lanes: ys_ref[:, groups, lanes], x_ref.shape[0]))
    glu = z * jax.nn.sigmoid(_dot(z.astype(BF16), wglu_ref[...]) + bglu_ref[...])
    ya = _dot(glu.astype(BF16), wa_ref[...])
    yb = _dot(on_ref[...], wb_ref[...])
    merged = gm_ref[:, :D].astype(F32) * ya + gm_ref[:, D:].astype(F32) * yb
    x1 = x_ref[...] + mod_ref[2:3, :] * _dot(merged.astype(BF16), wo_ref[...])
    x1_ref[...] = x1
    y = x1 * lax.rsqrt(jnp.mean(x1 * x1, axis=-1, keepdims=True) + RMS_EPS)
    h2 = (y * gain_ref[...]) * (1.0 + mod_ref[4:5, :]) + mod_ref[3:4, :]
    h2_ref[...] = h2.astype(BF16)
    hh, hl = _split(h2)
    nt = (((1,), (1,)), ((), ()))
    dg = lambda a, b: lax.dot_general(a, b, nt, preferred_element_type=F32)
    logits = dg(wrh_ref[...], hh) + (dg(wrh_ref[...], hl) + dg(wrl_ref[...], hh))
    sc_ref[...] = jax.nn.sigmoid(logits)


def _merge(x, mod, ys5, o_nsa, gm, w_glu, b_glu, w_a, w_b, w_out, gain_f, w_router):
    B, S, D = x.shape
    tm = min(ROW_TILE, S)
    wrh, wrl = _split(w_router.T)
    ws = [w_glu.astype(BF16), b_glu.reshape(1, -1), w_a.astype(BF16), w_b.astype(BF16), w_out.astype(BF16),
          gain_f, wrh, wrl]
    row = lambda w: pl.BlockSpec((None, tm, w), lambda b, i: (b, i, 0))
    full = lambda a: pl.BlockSpec(a.shape, lambda b, i: (0,) * a.ndim)
    return pl.pallas_call(
        _merge_kernel,
        grid=(B, S // tm),
        in_specs=[row(D), pl.BlockSpec((None, N_MOD, D), lambda b, i: (b, 0, 0)),
                  _s5_chunk_spec(tm), row(NSA_WIDTH), row(2 * D)] + [full(a) for a in ws],
        out_specs=[row(D), row(D), pl.BlockSpec((None, N_EXPERTS, tm), lambda b, i: (b, 0, i))],
        out_shape=[jax.ShapeDtypeStruct((B, S, D), F32), jax.ShapeDtypeStruct((B, S, D), BF16),
                   jax.ShapeDtypeStruct((B, N_EXPERTS, S), F32)],
        compiler_params=_params(("arbitrary", "arbitrary")),
        name="merge",
    )(x, mod, ys5, o_nsa, gm, *ws)


def _first_argmax_rows(v, idx, n):
    m = jnp.max(v, axis=0, keepdims=True)
    first = jnp.min(jnp.where(v == m, idx, n), axis=0, keepdims=True)
    return idx == first, m


def _route_kernel(sc_ref, bias_ref, tri_ref, rank_ref, w_ref, cnt_ref):
    E, NG = N_EXPERTS, N_EXPERT_GROUPS
    GS = E // NG
    sc = sc_ref[...]
    TM = sc.shape[1]
    sel = sc + bias_ref[...]
    i8 = lax.broadcasted_iota(jnp.int32, (GS, TM), 0)
    gscore = []
    for g in range(NG):
        blk = sel[g * GS:(g + 1) * GS, :]
        hit, m1 = _first_argmax_rows(blk, i8, GS)
        m2 = jnp.max(jnp.where(hit, -jnp.inf, blk), axis=0, keepdims=True)
        gscore.append(m1 + m2)
    gscore = jnp.concatenate(gscore, axis=0)
    ig = lax.broadcasted_iota(jnp.int32, (NG, TM), 0)
    gmask = jnp.zeros((NG, TM), F32)
    for _ in range(TOPK_EXPERT_GROUPS):
        hit, _m = _first_argmax_rows(gscore, ig, NG)
        gmask = jnp.where(hit, 1.0, gmask)
        gscore = jnp.where(hit, -jnp.inf, gscore)
    emask = jnp.concatenate([jnp.broadcast_to(gmask[g:g + 1, :], (GS, TM)) for g in range(NG)], axis=0)
    cand = jnp.where(emask > 0.5, sel, NEG_INF)
    ie = lax.broadcasted_iota(jnp.int32, (E, TM), 0)
    chosen = jnp.zeros((E, TM), F32)
    for _ in range(TOP_K):
        hit, _m = _first_argmax_rows(cand, ie, E)
        chosen = jnp.where(hit, 1.0, chosen)
        cand = jnp.where(hit, -jnp.inf, cand)
    w = chosen * sc
    w = w / jnp.sum(w, axis=0, keepdims=True) * ROUTED_SCALE
    sub = rank_ref.shape[-1]
    for q in range(TM // sub):
        cols = slice(q * sub, (q + 1) * sub)
        cb = chosen[:, cols].astype(BF16)
        prefix = _dot(cb, tri_ref[...])
        rank_ref[q] = jnp.where(chosen[:, cols] > 0.5, prefix, RANK_NONE)
        w_ref[q] = w[:, cols]
        cnt_ref[q] = _dot(cb, jnp.ones((sub, LANES), BF16))


def _route(scores_t, router_bias):
    B, E, S = scores_t.shape
    TM = min(ROUTE_TILE, S)
    per_step = min(MOE_OUTER, S) // TM
    nj = S // (TM * per_step)
    ns = B * nj * per_step
    tri = jnp.asarray(np.triu(np.ones((TM, TM), np.float32), k=1), BF16)
    tile = lambda w: pl.BlockSpec((per_step, E, w), lambda b, j: (b * nj + j, 0, 0))
    return pl.pallas_call(
        _route_kernel,
        grid=(B, nj),
        in_specs=[pl.BlockSpec((None, E, TM * per_step), lambda b, j: (b, 0, j)),
                  pl.BlockSpec((E, 1), lambda b, j: (0, 0)),
                  pl.BlockSpec((TM, TM), lambda b, j: (0, 0))],
        out_specs=[tile(TM), tile(TM), tile(LANES)],
        out_shape=[jax.ShapeDtypeStruct((ns, E, TM), F32), jax.ShapeDtypeStruct((ns, E, TM), F32),
                   jax.ShapeDtypeStruct((ns, E, LANES), F32)],
        compiler_params=_params(("arbitrary", "arbitrary")),
        name="route",
    )(scores_t, router_bias.reshape(E, 1), tri)


def _moe_kernel(cnt_ref, h_ref, x1_ref, mod_ref, rank_ref, w_ref, wg_ref, wu_ref, wd_ref,
                sg_ref, su_ref, sd_ref, o_ref, xc, yc, pc, wc, *, n_sub, tm, n_exp):
    to = pl.program_id(0)
    eb = pl.program_id(1)
    gate = mod_ref[5:6, :]

    @pl.when(eb == 0)
    def _clear():
        o_ref[...] = jnp.zeros(o_ref.shape, F32)
        xc[...] = jnp.zeros(xc.shape, BF16)
        yc[...] = jnp.zeros(yc.shape, BF16)
        wc[...] = jnp.zeros(wc.shape, F32)

    @pl.when(eb < n_sub)
    def _base():
        rows = pl.ds(pl.multiple_of(eb * tm, tm), tm)
        hs = h_ref[rows, :]
        hid = jax.nn.silu(_dot(hs, sg_ref[...])) * _dot(hs, su_ref[...])
        o_ref[rows, :] += x1_ref[...] + gate * _dot(hid.astype(BF16), sd_ref[...])

    PIECE, WIN = MOE_PIECE, MOE_WIN
    tn = (((0,), (0,)), ((), ()))
    experts = [eb * n_exp + j for j in range(n_exp)]
    cnt = [[cnt_ref[(to * n_sub + s) * N_EXPERTS + e] for e in experts] for s in range(n_sub)]
    off = []
    for s in range(n_sub):
        o = [jnp.int32(0)]
        for j in range(n_exp):
            o.append(o[-1] + ((cnt[s][j] + 15) // 16) * 16)
        off.append(o)
    n_pieces = [(off[s][n_exp] + PIECE - 1) // PIECE for s in range(n_sub)]
    fits = n_pieces[0] < MOE_ROWS // PIECE
    for s in range(1, n_sub):
        fits = jnp.logical_and(fits, n_pieces[s] < MOE_ROWS // PIECE)

    def one_hot_rows(s, base, rows_n, js):
        slot = (lax.broadcasted_iota(jnp.int32, (rows_n, tm), 0) + base).astype(F32)
        pick = jnp.zeros((rows_n, tm), F32)
        wacc = jnp.zeros((rows_n, tm), F32)
        for j in js:
            hit = (slot - off[s][j].astype(F32)) == rank_ref[s, pl.ds(experts[j], 1), :]
            pick = jnp.where(hit, 1.0, pick)
            wacc = jnp.where(hit, w_ref[s, pl.ds(experts[j], 1), :], wacc)
        return pick.astype(BF16), jnp.sum(wacc, axis=1, keepdims=True)

    def mlp(xg, j):
        hid = jax.nn.silu(_dot(xg, wg_ref[j])) * _dot(xg, wu_ref[j])
        return _dot(hid.astype(BF16), wd_ref[j])

    def piece(s, q):
        r = pl.ds(q * PIECE, PIECE)
        pick, wrow = one_hot_rows(s, q * PIECE, PIECE, range(n_exp))
        xc[s, r, :] = _dot(pick, h_ref[pl.ds(s * tm, tm), :]).astype(BF16)
        pc[s, :, r] = pick.T
        wc[s, r, :] = wrow

    def window(j, p, first):
        starts = [pl.multiple_of(jnp.minimum(off[s][j] + p * WIN, (n_pieces[s] + 1) * PIECE - WIN), 16)
                  for s in range(n_sub)]
        xg = jnp.concatenate([xc[s, pl.ds(starts[s], WIN), :] for s in range(n_sub)], axis=0)
        out = mlp(xg, j)
        for s in range(n_sub):
            r = pl.ds(starts[s], WIN)
            new = (out[s * WIN:(s + 1) * WIN] * wc[s, r, :] * gate).astype(BF16)
            if not first:
                rank = starts[s] - off[s][j] + lax.broadcasted_iota(jnp.int32, (WIN, 1), 0)
                new = jnp.where(rank < cnt[s][j], new, yc[s, r, :])
            yc[s, r, :] = new

    def scatter(s, start, n):
        r = pl.ds(start, n)
        o_ref[pl.ds(s * tm, tm), :] += _dot(pc[s, :, r], yc[s, r, :])

    @pl.when(fits)
    def _packed():
        base = MOE_BASE_PIECES
        for s in range(n_sub):
            for q in range(base):
                piece(s, q)
        for s in range(n_sub):
            @pl.when(n_pieces[s] > base)
            def _last(s=s):
                piece(s, base)
        for j in range(n_exp):
            window(j, 0, True)
        for j in range(n_exp):
            most = cnt[0][j]
            for s in range(1, n_sub):
                most = jnp.maximum(most, cnt[s][j])

            @pl.when(most > WIN)
            def _more(j=j, most=most):
                lax.fori_loop(1, (most + WIN - 1) // WIN, lambda p, c: (window(j, p, False), c)[1], 0)
        for s in range(n_sub):
            scatter(s, 0, base * PIECE)
        for s in range(n_sub):
            @pl.when(n_pieces[s] > base)
            def _last_out(s=s):
                scatter(s, base * PIECE, PIECE)

    @pl.when(jnp.logical_not(fits))
    def _unpacked():
        for j in range(n_exp):
            for s in range(n_sub):
                rows = pl.ds(s * tm, tm)

                def block(bi, carry, j=j, s=s, rows=rows):
                    slot0 = off[s][j] + bi * PIECE
                    pick, wrow = one_hot_rows(s, slot0, PIECE, [j])
                    out = mlp(_dot(pick, h_ref[rows, :]).astype(BF16), j)
                    ow = (out * wrow * gate).astype(BF16)
                    o_ref[rows, :] += lax.dot_general(pick, ow, tn, preferred_element_type=F32)
                    return carry

                lax.fori_loop(0, (cnt[s][j] + PIECE - 1) // PIECE, block, 0)


def _moe(h2, x1, mod, rank_t, w_t, counts, w_gate, w_up, w_down, ws_gate, ws_up, ws_down, seq):
    T, D = h2.shape
    ns, E, tm = rank_t.shape
    tmo = min(MOE_OUTER, seq)
    n_sub = tmo // tm
    n_exp = MOE_EXPERTS_PER_STEP
    per_b = seq // tmo
    ws = [w_gate.astype(BF16), w_up.astype(BF16), w_down.astype(BF16)]
    sh = [ws_gate.astype(BF16), ws_up.astype(BF16), ws_down.astype(BF16)]
    assert E // n_exp >= n_sub and MOE_ROWS // MOE_PIECE == MOE_BASE_PIECES + 2
    tok = lambda: pl.BlockSpec((tmo, D), lambda t, e, c: (t, 0))
    x1_sub = pl.BlockSpec((tm, D), lambda t, e, c: (t * n_sub + jnp.minimum(e, n_sub - 1), 0))
    sub = lambda: pl.BlockSpec((n_sub, E, tm), lambda t, e, c: (t, 0, 0))
    exp = lambda a: pl.BlockSpec((n_exp,) + a.shape[1:], lambda t, e, c: (e, 0, 0))
    full = lambda a: pl.BlockSpec(a.shape, lambda t, e, c: (0, 0), pipeline_mode=pl.Buffered(1))
    grid_spec = pltpu.PrefetchScalarGridSpec(
        num_scalar_prefetch=1,
        grid=(T // tmo, E // n_exp),
        in_specs=[tok(), x1_sub, pl.BlockSpec((None, N_MOD, D), lambda t, e, c: (t // per_b, 0, 0)),
                  sub(), sub(), exp(ws[0]), exp(ws[1]), exp(ws[2]), full(sh[0]), full(sh[1]), full(sh[2])],
        out_specs=tok(),
        scratch_shapes=[pltpu.VMEM((n_sub, MOE_ROWS, D), BF16), pltpu.VMEM((n_sub, MOE_ROWS, D), BF16),
                        pltpu.VMEM((n_sub, tm, MOE_ROWS), BF16), pltpu.VMEM((n_sub, MOE_ROWS, 1), F32)],
    )
    return pl.pallas_call(
        functools.partial(_moe_kernel, n_sub=n_sub, tm=tm, n_exp=n_exp),
        grid_spec=grid_spec,
        out_shape=jax.ShapeDtypeStruct((T, D), F32),
        compiler_params=_params(("arbitrary", "arbitrary")),
        name="moe",
    )(counts, h2, x1, mod, rank_t, w_t, *ws, *sh)


def kernel(x, c, w_ada, b_ada, norm_mix_gain, norm_ffn_gain, w_in, s5_lambda_re, s5_lambda_im, s5_log_dt, s5_b_re, s5_b_im, s5_c_re, s5_c_im, s5_d, s5_w_glu, s5_b_glu, q_norm_gain, k_norm_gain, cmp_pe, cmp_w1, cmp_b1, cmp_w2, cmp_b2, w_branch_a, w_branch_b, w_out, w_router, router_bias, w_gate, w_up, w_down, ws_gate, ws_up, ws_down):
    B, S, D = x.shape
    for l in range(w_ada.shape[0]):
        mod = _ada(c, w_ada[l], b_ada[l]).reshape(B, N_MOD, D)
        u, kvc, qt, ks, vst, kwn, vwt, gt, gm = _inproj(x, mod, norm_mix_gain[l:l + 1], w_in[l],
                                                        q_norm_gain[l], k_norm_gain[l])
        tables = _s5_tables(s5_lambda_re[l], s5_lambda_im[l], s5_log_dt[l], s5_b_re[l], s5_b_im[l],
                            s5_c_re[l], s5_c_im[l], s5_d[l])
        ys5 = _s5(u, tables)
        kvc_c = _compress(kvc, cmp_pe[l], cmp_w1[l], cmp_b1[l], cmp_w2[l], cmp_b2[l], k_norm_gain[l, 0])
        o_nsa = _attend(qt, ks, vst, kwn, vwt, gt, kvc_c)
        x1, h2, scores_t = _merge(x, mod, ys5, o_nsa, gm, s5_w_glu[l], s5_b_glu[l], w_branch_a[l],
                                  w_branch_b[l], w_out[l], norm_ffn_gain[l:l + 1], w_router[l])
        rank_t, w_t, cnt = _route(scores_t, router_bias[l])
        counts = cnt[:, :, 0].astype(jnp.int32).reshape(-1)
        x = _moe(h2.reshape(B * S, D), x1.reshape(B * S, D), mod, rank_t, w_t, counts,
                 w_gate[l], w_up[l], w_down[l], ws_gate[l], ws_up[l], ws_down[l], S).reshape(B, S, D)
    return x
```

```python
import functools

import numpy as np
import jax
import jax.numpy as jnp
from jax import lax
from jax.experimental import pallas as pl
from jax.experimental.pallas import tpu as pltpu

F32 = jnp.float32
BF16 = jnp.bfloat16

D_MODEL = 1024
S5_WIDTH = 512
S5_GROUP = 16
S5_GROUPS = S5_WIDTH // S5_GROUP
S5_STATE = 64
N_HEADS = 8
N_KV_HEADS = 2
GQA_GROUP = N_HEADS // N_KV_HEADS
HEAD_DIM = 64
NSA_WIDTH = N_HEADS * HEAD_DIM
KV_WIDTH = 2 * N_KV_HEADS * HEAD_DIM
CMP_BLOCK = 32
CMP_STRIDE = 16
CMP_HIDDEN = 256
SEL_BLOCK = 64
SEL_TOPK = 8
WINDOW = 256
Q_BLOCK = 128
FORCE_BONUS = 1e3
N_EXPERTS = 64
TOP_K = 8
N_EXPERT_GROUPS = 8
TOPK_EXPERT_GROUPS = 4
ROUTED_SCALE = 2.5
RMS_EPS = 1e-6
NEG_INF = -1e30
N_MOD = 6

LANES = 128
SUBLANES = 8
S5_CHUNK = 16
S5_CW = S5_CHUNK * S5_GROUP
ROW_TILE = 512
ATT_TILES = 2
SEL_KEY_CHUNK = 512
SEL_UNIT = 128
SEL_SLOTS = (5, 8)
POS_BASE = 64
POS_ROWS = 16
AUG = 128
ROUTE_TILE = 256
MOE_OUTER = 1024
MOE_EXPERTS_PER_STEP = 4
MOE_PIECE = 128
MOE_WIN = 64
MOE_ROWS = 512
RANK_NONE = -float(1 << 20)
VMEM_LIMIT = 56 * 1024 * 1024


def _dot(a, b):
    return jnp.dot(a, b, preferred_element_type=F32)


def _split(a):
    hi = a.astype(BF16)
    lo = (a - hi.astype(F32)).astype(BF16)
    return hi, lo


def _dot3(a, bh, bl):
    ah, al = _split(a)
    return _dot(ah, bh) + (_dot(al, bh) + _dot(ah, bl))


def _segment_transpose(x):
    assert x.shape[-2:] == (SUBLANES, LANES) and LANES // S5_GROUP == SUBLANES
    nd = x.ndim
    i = lax.broadcasted_iota(jnp.int32, x.shape, nd - 2)
    seg = lax.broadcasted_iota(jnp.int32, x.shape, nd - 1) // S5_GROUP
    out = x
    for d in range(1, SUBLANES):
        r = pltpu.roll(pltpu.roll(x, SUBLANES - d, axis=nd - 2), S5_GROUP * d, axis=nd - 1)
        out = jnp.where(seg == ((i + d) & (SUBLANES - 1)), r, out)
    return out


def _to_chunk_major(u, put):
    rows = u.shape[0]
    u3 = u.reshape(rows // SUBLANES, SUBLANES, S5_WIDTH)
    halves = S5_CHUNK // SUBLANES
    for jb in range(S5_WIDTH // LANES):
        t = _segment_transpose(u3[:, :, LANES * jb:LANES * (jb + 1)])
        t = t.reshape(rows // S5_CHUNK, halves, SUBLANES, LANES)
        for hf in range(halves):
            put(slice(SUBLANES * jb, SUBLANES * (jb + 1)), slice(LANES * hf, LANES * (hf + 1)), t[:, hf])


def _from_chunk_major(get, rows):
    halves = S5_CHUNK // SUBLANES
    cols = []
    for jb in range(S5_WIDTH // LANES):
        parts = [_segment_transpose(get(slice(SUBLANES * jb, SUBLANES * (jb + 1)), slice(LANES * hf, LANES * (hf + 1))))
                 for hf in range(halves)]
        cols.append(jnp.stack(parts, axis=1).reshape(rows, LANES))
    return jnp.concatenate(cols, axis=1)


def _params(sem):
    return pltpu.CompilerParams(dimension_semantics=sem, vmem_limit_bytes=VMEM_LIMIT)


def _ada_kernel(c_ref, w_ref, b_ref, o_ref):
    cs = jax.nn.silu(c_ref[...])
    wh, wl = _split(w_ref[...])
    o_ref[...] = _dot3(cs, wh, wl) + b_ref[...]


def _ada(c, w_ada, b_ada):
    B, D = c.shape
    return pl.pallas_call(
        _ada_kernel,
        grid=(N_MOD,),
        in_specs=[pl.BlockSpec((B, D), lambda j: (0, 0)),
                  pl.BlockSpec((D, D), lambda j: (0, j)),
                  pl.BlockSpec((1, D), lambda j: (0, j))],
        out_specs=pl.BlockSpec((B, D), lambda j: (0, j)),
        out_shape=jax.ShapeDtypeStruct((B, N_MOD * D), F32),
        compiler_params=_params(("arbitrary",)),
        name="ada",
    )(c, w_ada, b_ada.reshape(1, N_MOD * D))


def _head_norm(v, bd, gain):
    sq = v * v
    sh, sl = _split(sq)
    ms = _dot(sh, bd) + _dot(sl, bd)
    return v * lax.rsqrt(ms + RMS_EPS) * gain


def _inproj_kernel(x_ref, mod_ref, gain_ref, wm_ref, wt_ref, wgn_ref, wgm_ref, bd_ref, qg_ref, kg_ref,
                   u_ref, kvc_ref, qt_ref, ksa_ref, vst_ref, kwa_ref, vwt_ref, gt_ref, gm_ref):
    x = x_ref[...]
    shift = mod_ref[0:1, :]
    scale = mod_ref[1:2, :]
    y = x * lax.rsqrt(jnp.mean(x * x, axis=-1, keepdims=True) + RMS_EPS)
    h = (y * gain_ref[...]) * (1.0 + scale) + shift
    hb = h.astype(BF16)
    main = _dot(hb, wm_ref[...])
    tm = x.shape[0]

    def put(groups, lanes, block):
        u_ref[:, groups, lanes] = block

    _to_chunk_major(main[:, :S5_WIDTH], put)
    o = S5_WIDTH
    for part in range(KV_WIDTH // LANES):
        kvc_ref[part] = main[:, o + part * LANES:o + (part + 1) * LANES]
    o += KV_WIDTH
    kw = N_KV_HEADS * HEAD_DIM
    bd = bd_ref[...]
    ks = _head_norm(main[:, o:o + kw], bd[:kw, :kw], kg_ref[1:2, :]).astype(BF16)
    kwn = _head_norm(main[:, o + kw:o + 2 * kw], bd[:kw, :kw], kg_ref[2:3, :]).astype(BF16)
    pos = pl.program_id(1) * tm + lax.broadcasted_iota(jnp.int32, (tm, 1), 0)
    lane = lax.broadcasted_iota(jnp.int32, (tm, AUG), 1)
    digits = jnp.where(lane == 0, pos // POS_BASE, jnp.where(lane == 1, pos % POS_BASE, 0))
    digits = digits[:, :AUG - HEAD_DIM].astype(F32).astype(BF16)
    onehot = jnp.where(lane == pos // SEL_BLOCK, 1.0, 0.0).astype(BF16)
    for hh in range(N_KV_HEADS):
        ksa_ref[hh, :, 0:HEAD_DIM] = ks[:, hh * HEAD_DIM:(hh + 1) * HEAD_DIM]
        ksa_ref[hh, :, HEAD_DIM:AUG] = digits
        ksa_ref[hh, :, AUG:] = onehot
        kwa_ref[hh, :, 0:HEAD_DIM] = kwn[:, hh * HEAD_DIM:(hh + 1) * HEAD_DIM]
        kwa_ref[hh, :, HEAD_DIM:] = digits

    nt = (((1,), (1,)), ((), ()))
    tt = lax.dot_general(wt_ref[...], hb, nt, preferred_element_type=F32)
    qt = tt[:NSA_WIDTH]
    sq = qt * qt
    sh, sl = _split(sq)
    qn = (qt * lax.rsqrt(_dot(bd, sh) + _dot(bd, sl) + RMS_EPS) * qg_ref[...] * (HEAD_DIM ** -0.5)).astype(BF16)
    QW = GQA_GROUP * Q_BLOCK
    for qb in range(tm // Q_BLOCK):
        for hd in range(N_HEADS):
            hh, g = divmod(hd, GQA_GROUP)
            qt_ref[hh, :, qb * QW + g * Q_BLOCK:qb * QW + (g + 1) * Q_BLOCK] = (
                qn[hd * HEAD_DIM:(hd + 1) * HEAD_DIM, qb * Q_BLOCK:(qb + 1) * Q_BLOCK])
    vst_ref[...] = tt[NSA_WIDTH:NSA_WIDTH + kw].reshape(N_KV_HEADS, HEAD_DIM, tm).astype(BF16)
    vwt_ref[...] = tt[NSA_WIDTH + kw:].reshape(N_KV_HEADS, HEAD_DIM, tm).astype(BF16)
    gn = jax.nn.sigmoid(lax.dot_general(wgn_ref[...], hb, nt, preferred_element_type=F32))
    for qb in range(tm // Q_BLOCK):
        for br in range(3):
            for hd in range(N_HEADS):
                hh, g = divmod(hd, GQA_GROUP)
                r = br * N_HEADS + hd
                gt_ref[hh, br:br + 1, qb * QW + g * Q_BLOCK:qb * QW + (g + 1) * Q_BLOCK] = (
                    gn[r:r + 1, qb * Q_BLOCK:(qb + 1) * Q_BLOCK])
    gm_ref[...] = jax.nn.sigmoid(_dot(hb, wgm_ref[...])).astype(BF16)


def _s5_chunk_spec(tm):
    assert S5_CHUNK == 16 and S5_GROUP == 16, "the segment transposes assume 16 steps x 16 channels"
    return pl.BlockSpec((tm // S5_CHUNK, None, S5_GROUPS, S5_CW), lambda b, i: (i, b, 0, 0))


def _inproj(x, mod, gain, w_in, q_gain, k_gain):
    B, S, D = x.shape
    tm = min(ROW_TILE, S)
    assert S // SEL_BLOCK <= AUG and tm % Q_BLOCK == 0
    kw = N_KV_HEADS * HEAD_DIM
    cols = np.cumsum((0,) + (S5_WIDTH, NSA_WIDTH, KV_WIDTH, KV_WIDTH, KV_WIDTH, 3 * N_HEADS, 2 * D))
    c_u, c_q, c_kvc, c_kvs, c_kvw, c_gn, c_gm = cols[:7]
    sl = lambda a, n: w_in[:, a:a + n]
    wm = jnp.concatenate([sl(c_u, S5_WIDTH), sl(c_kvc, KV_WIDTH), sl(c_kvs, kw), sl(c_kvw, kw)], axis=1).astype(BF16)
    wt = jnp.concatenate([sl(c_q, NSA_WIDTH), sl(c_kvs + kw, kw), sl(c_kvw + kw, kw)], axis=1).T.astype(BF16)
    wgn = sl(c_gn, 3 * N_HEADS).T.astype(BF16)
    wgm = sl(c_gm, 2 * D).astype(BF16)
    seg = np.arange(NSA_WIDTH) // HEAD_DIM
    bd = jnp.asarray((seg[:, None] == seg[None, :]).astype(np.float32) / HEAD_DIM, BF16)
    qg = jnp.tile(q_gain, N_HEADS).reshape(NSA_WIDTH, 1)
    kg = jnp.tile(k_gain, (1, N_KV_HEADS))
    nq = tm // Q_BLOCK * GQA_GROUP * Q_BLOCK
    row = lambda w: pl.BlockSpec((None, tm, w), lambda b, i: (b, i, 0))
    full = lambda a: pl.BlockSpec(a.shape, lambda b, i: (0,) * a.ndim)
    rows4 = lambda w: pl.BlockSpec((None, N_KV_HEADS, tm, w), lambda b, i: (b, 0, i, 0))
    cols4 = lambda r, w: pl.BlockSpec((None, N_KV_HEADS, r, w), lambda b, i: (b, 0, 0, i))
    nqt = S // Q_BLOCK * GQA_GROUP * Q_BLOCK
    return pl.pallas_call(
        _inproj_kernel,
        grid=(B, S // tm),
        in_specs=[row(D), pl.BlockSpec((None, N_MOD, D), lambda b, i: (b, 0, 0)),
                  full(gain), full(wm), full(wt), full(wgn), full(wgm), full(bd), full(qg), full(kg)],
        out_specs=[_s5_chunk_spec(tm), pl.BlockSpec((None, KV_WIDTH // LANES, tm, LANES), lambda b, i: (b, 0, i, 0)),
                   cols4(HEAD_DIM, nq), rows4(2 * AUG), cols4(HEAD_DIM, tm),
                   rows4(AUG), cols4(HEAD_DIM, tm), cols4(3, nq), row(2 * D)],
        out_shape=[jax.ShapeDtypeStruct((S // S5_CHUNK, B, S5_GROUPS, S5_CW), F32),
                   jax.ShapeDtypeStruct((B, KV_WIDTH // LANES, S, LANES), F32),
                   jax.ShapeDtypeStruct((B, N_KV_HEADS, HEAD_DIM, nqt), BF16),
                   jax.ShapeDtypeStruct((B, N_KV_HEADS, S, 2 * AUG), BF16),
                   jax.ShapeDtypeStruct((B, N_KV_HEADS, HEAD_DIM, S), BF16),
                   jax.ShapeDtypeStruct((B, N_KV_HEADS, S, AUG), BF16),
                   jax.ShapeDtypeStruct((B, N_KV_HEADS, HEAD_DIM, S), BF16),
                   jax.ShapeDtypeStruct((B, N_KV_HEADS, 3, nqt), F32),
                   jax.ShapeDtypeStruct((B, S, 2 * D), BF16)],
        compiler_params=_params(("arbitrary", "arbitrary")),
        name="inproj",
    )(x, mod, gain, wm, wt, wgn, wgm, bd, qg, kg)


def _s5_tables(lam_re, lam_im, log_dt, b_re, b_im, c_re, c_im, d_skip):
    L, G = S5_CHUNK, S5_GROUPS
    lr, li = lam_re.astype(F32), lam_im.astype(F32)
    dt = jnp.exp(log_dt.astype(F32))[:, None]
    mag = jnp.exp(lr * dt)
    abar_re, abar_im = mag * jnp.cos(li * dt), mag * jnp.sin(li * dt)
    num_re, num_im = abar_re - 1.0, abar_im
    den = lr * lr + li * li
    coef_re = (num_re * lr + num_im * li) / den
    coef_im = (num_im * lr - num_re * li) / den
    br, bi = b_re.astype(F32), b_im.astype(F32)
    bbar_re = coef_re[..., None] * br - coef_im[..., None] * bi
    bbar_im = coef_re[..., None] * bi + coef_im[..., None] * br
    k = jnp.arange(L + 1, dtype=F32)[:, None]
    pmag = jnp.exp((lr * dt)[:, None, :] * k)
    ang = (li * dt)[:, None, :] * k
    pw = jnp.stack([jnp.tile(pmag * jnp.cos(ang), (1, 1, 2)), jnp.tile(pmag * jnp.sin(ang), (1, 1, 2))], axis=1)
    brt, bit = bbar_re.transpose(0, 2, 1), bbar_im.transpose(0, 2, 1)
    ba = jnp.concatenate([brt, bit], axis=2)
    bb = jnp.concatenate([-bit, brt], axis=2)
    cr, ci = c_re.astype(F32), c_im.astype(F32)
    ca = jnp.concatenate([cr, -ci], axis=2)
    cb = jnp.concatenate([-ci, -cr], axis=2)
    dv = jnp.tile(d_skip.astype(F32), (1, L)).reshape(G, 1, S5_CW)
    return pw, ba, bb, ca, cb, dv


def _s5_kernel(u_ref, pw_ref, ba_ref, bb_ref, ca_ref, cb_ref, dv_ref,
               y_ref, vr_scr, vi_scr, xr_scr, xi_scr, *, n_chunks, bsz):
    L, P, N = S5_CHUNK, S5_GROUP, S5_STATE
    nt = (((1,), (1,)), ((), ()))
    dot_nt = lambda a, b: lax.dot_general(a, b, nt, preferred_element_type=F32)
    mix = lambda a, b, k: a * pw_ref[0, k:k + 1, :] + b * pw_ref[1, k:k + 1, :]
    ba, bb, ca, cb = ba_ref[...], bb_ref[...], ca_ref[...], cb_ref[...]
    ws = jnp.concatenate([mix(ba, bb, L - 1 - s) for s in range(L)], axis=0)
    wot = jnp.concatenate([mix(ca, cb, k) for k in range(L + 1)], axis=0)
    wsh, wsl = _split(ws)
    wth, wtl = _split(wot)
    bah, bal = _split(ba)
    kern = dot_nt(bah, wth[:L * P]) + (dot_nt(bal, wth[:L * P]) + dot_nt(bah, wtl[:L * P]))
    lane = lax.broadcasted_iota(jnp.int32, kern.shape, 1)
    mt = jnp.concatenate([kern] + [jnp.where(lane >= P * s, pltpu.roll(kern, P * s, axis=1), 0.0)
                                   for s in range(1, L)], axis=0)
    mth, mtl = _split(mt)

    u = u_ref[...]
    uh, ul = _split(u)
    y = _dot(uh, mth) + (_dot(ul, mth) + _dot(uh, mtl))
    v = _dot(uh, wsh) + (_dot(ul, wsh) + _dot(uh, wsl))
    vr_scr[...] = v[:, :N]
    vi_scr[...] = v[:, N:]
    a_r = pw_ref[0, L:L + 1, :N]
    a_i = pw_ref[1, L:L + 1, :N]

    def step(c, x):
        xr, xi = x
        r = pl.ds(pl.multiple_of(c * bsz, bsz), bsz)
        xr_scr[r, :] = xr
        xi_scr[r, :] = xi
        return (a_r * xr - a_i * xi + vr_scr[r, :], a_r * xi + a_i * xr + vi_scr[r, :])

    zero = jnp.zeros((bsz, N), F32)
    lax.fori_loop(0, n_chunks, step, (zero, zero), unroll=8)
    xh, xl = _split(jnp.concatenate([xr_scr[...], xi_scr[...]], axis=1))
    y = y + (dot_nt(xh, wth[P:]) + (dot_nt(xl, wth[P:]) + dot_nt(xh, wtl[P:])))
    y_ref[...] = y + dv_ref[...] * u


def _s5(ug, tables):
    nc, B, G, cw = ug.shape
    N = S5_STATE
    grp = lambda a: pl.BlockSpec((None,) + a.shape[1:], lambda g: (g,) + (0,) * (a.ndim - 1))
    rows = pl.BlockSpec((nc * B, cw), lambda g: (0, g))
    y = pl.pallas_call(
        functools.partial(_s5_kernel, n_chunks=nc, bsz=B),
        grid=(G,),
        in_specs=[rows] + [grp(t) for t in tables],
        out_specs=rows,
        out_shape=jax.ShapeDtypeStruct((nc * B, G * cw), F32),
        scratch_shapes=[pltpu.VMEM((nc * B, N), F32)] * 4,
        compiler_params=_params(("arbitrary",)),
        name="s5",
    )(ug.reshape(nc * B, G * cw), *tables)
    return y.reshape(nc, B, G, cw)


def _compress_kernel(x_ref, pe_ref, w1_ref, b1_ref, w2_ref, b2_ref, kg_ref, o_ref, *, n_rows):
    half = CMP_STRIDE * HEAD_DIM
    for j in range(2 * N_KV_HEADS):
        kv, hd = divmod(j, N_KV_HEADS)
        x = jnp.concatenate(
            [x_ref[kv, pl.ds(t, n_rows, stride=CMP_STRIDE), :][:, hd * HEAD_DIM:(hd + 1) * HEAD_DIM]
             for t in range(CMP_STRIDE)], axis=1)
        a = _dot((x + pe_ref[kv, 0:1, :]).astype(BF16), w1_ref[kv, :half, :])
        b = _dot((x + pe_ref[kv, 1:2, :]).astype(BF16), w1_ref[kv, half:, :])
        hid = jax.nn.gelu(a + pltpu.roll(b, n_rows - 1, axis=0) + b1_ref[kv])
        out = _dot(hid.astype(BF16), w2_ref[kv]) + b2_ref[kv]
        if kv == 0:
            out = out * lax.rsqrt(jnp.mean(out * out, axis=-1, keepdims=True) + RMS_EPS) * kg_ref[...]
        o_ref[j] = out.astype(BF16)


def _compress(kv_c, cmp_pe, cmp_w1, cmp_b1, cmp_w2, cmp_b2, k_gain0):
    B, _, S, _ = kv_c.shape
    nr = S // CMP_STRIDE
    half = CMP_STRIDE * HEAD_DIM
    nj = 2 * N_KV_HEADS
    ws = [cmp_pe.reshape(2, 2, half), cmp_w1.astype(BF16), cmp_b1.reshape(2, 1, CMP_HIDDEN), cmp_w2.astype(BF16),
          cmp_b2.reshape(2, 1, HEAD_DIM), k_gain0.reshape(1, HEAD_DIM)]
    full = lambda a: pl.BlockSpec(a.shape, lambda b: (0,) * a.ndim)
    return pl.pallas_call(
        functools.partial(_compress_kernel, n_rows=nr),
        grid=(B,),
        in_specs=[pl.BlockSpec((None,) + kv_c.shape[1:], lambda b: (b, 0, 0, 0))] + [full(a) for a in ws],
        out_specs=pl.BlockSpec((None, nj, nr, HEAD_DIM), lambda b: (b, 0, 0, 0)),
        out_shape=jax.ShapeDtypeStruct((B, nj, nr, HEAD_DIM), BF16),
        compiler_params=_params(("arbitrary",)),
        name="compress",
    )(kv_c, *ws)


def _softmax_cols(s, mask):
    s = jnp.where(mask, s, NEG_INF)
    m = jnp.max(s, axis=0, keepdims=True)
    p = jnp.where(mask, jnp.exp(s - m), 0.0)
    return p / jnp.maximum(jnp.sum(p, axis=0, keepdims=True), 1e-20)


def _attend_kernel(qt_ref, kc_ref, vct_ref, ks_ref, vst_ref, kw_ref, vwt_ref, g_ref, slope_ref, ovt_ref,
                   o_ref, qa_scr, sa_scr, sb_scr, *, n_sel, n_pick, n_cmp_rows):
    QW = GQA_GROUP * Q_BLOCK
    chains = [(k, h) for k in range(ATT_TILES) for h in range(N_KV_HEADS)]
    t0s = [(pl.program_id(1) * ATT_TILES + k) * Q_BLOCK for k in range(ATT_TILES)]
    tqs = [t0 + (lax.broadcasted_iota(jnp.int32, (1, QW), 1) & (Q_BLOCK - 1)) for t0 in t0s]
    cols = lambda k: slice(k * QW, (k + 1) * QW)

    r = lax.broadcasted_iota(jnp.int32, (POS_ROWS, QW), 0)
    qa = {}
    for k, h in chains:
        slope = slope_ref[h]
        qa_scr[k, h, 0:HEAD_DIM, :] = qt_ref[h, :, cols(k)]
        qa_scr[k, h, HEAD_DIM:HEAD_DIM + POS_ROWS, :] = jnp.where(
            r == 0, slope * POS_BASE, jnp.where(r == 1, slope, 0.0)).astype(BF16)
        qa_scr[k, h, HEAD_DIM + POS_ROWS:AUG, :] = jnp.zeros((AUG - HEAD_DIM - POS_ROWS, QW), BF16)
        qa[k, h] = qa_scr[k, h, 0:AUG, :]

    def compressed(k, h, rows):
        sc = _dot(kc_ref[h, 0:rows, :], qa[k, h])
        cpos = lax.broadcasted_iota(jnp.int32, (rows, QW), 0) * CMP_STRIDE + (CMP_BLOCK - 1)
        p_c = _softmax_cols(sc, cpos <= tqs[k])
        o_c = _dot(vct_ref[h, :, 0:rows], p_c.astype(BF16))
        psum = p_c[:, 0:Q_BLOCK]
        for g in range(1, GQA_GROUP):
            psum = psum + p_c[:, g * Q_BLOCK:(g + 1) * Q_BLOCK]
        ph, pl_ = _split(psum)
        return o_c, _dot(ovt_ref[:, 0:rows], ph) + _dot(ovt_ref[:, 0:rows], pl_)

    half = n_cmp_rows // 2
    if half % LANES == 0:
        early = (t0s[-1] + Q_BLOCK - CMP_BLOCK) // CMP_STRIDE < half
        cmp_out = lax.cond(early, lambda _: [compressed(k, h, half) for k, h in chains],
                           lambda _: [compressed(k, h, n_cmp_rows) for k, h in chains], 0)
    else:
        cmp_out = [compressed(k, h, n_cmp_rows) for k, h in chains]
    cmp_out = dict(zip(chains, cmp_out))

    def select(k, h, imp):
        jb = lax.broadcasted_iota(jnp.int32, (n_sel, Q_BLOCK), 0)
        cur = (t0s[k] + lax.broadcasted_iota(jnp.int32, (1, Q_BLOCK), 1)) // SEL_BLOCK
        forced = (jb == 0) | (jb == cur) | (jb == cur - 1)
        imp = jnp.where(forced, imp + FORCE_BONUS, imp)
        imp = jnp.where(jb <= cur, imp, -1.0)
        bias = jnp.full((n_sel, Q_BLOCK), NEG_INF, F32)
        for _ in range(n_pick):
            m = jnp.max(imp, axis=0, keepdims=True)
            first = jnp.min(jnp.where(imp == m, jb, n_sel), axis=0, keepdims=True)
            hit = jb == first
            bias = jnp.where(hit, 0.0, bias)
            imp = jnp.where(hit, -jnp.inf, imp)
        per_unit = SEL_UNIT // SEL_BLOCK
        n_units = n_sel // per_unit
        picked = jnp.where(bias == 0.0, 1.0, 0.0).astype(BF16)
        in_unit = (lax.broadcasted_iota(jnp.int32, (n_units, n_sel), 1) // per_unit
                   == lax.broadcasted_iota(jnp.int32, (n_units, n_sel), 0))
        hits = _dot(jnp.where(in_unit, 1.0, 0.0).astype(BF16), picked)
        live = jnp.where(jnp.max(hits, axis=1, keepdims=True) > 0.5, 1.0, 0.0)
        live = jnp.broadcast_to(live, (n_units, LANES))
        earlier = (lax.broadcasted_iota(jnp.int32, (n_units, n_units), 1)
                   < lax.broadcasted_iota(jnp.int32, (n_units, n_units), 0))
        before = _dot(jnp.where(earlier, 1.0, 0.0).astype(BF16), live.astype(BF16))
        slot = lax.broadcasted_iota(jnp.int32, (n_units, LANES), 1).astype(F32)
        in_slot = jnp.where(before == slot, live, 0.0).astype(BF16)
        r8 = lax.broadcasted_iota(jnp.int32, (SUBLANES, n_units), 0)
        c8 = lax.broadcasted_iota(jnp.int32, (SUBLANES, n_units), 1)
        pick_rows = jnp.where(r8 == 0, c8, jnp.where(r8 == 1, 1, 0)).astype(F32).astype(BF16)
        ids = _dot(pick_rows, in_slot)
        n_live = (before[n_units - 1:n_units, 0:1] + live[n_units - 1:n_units, 0:1])[0, 0]
        bias = jnp.concatenate([bias] * GQA_GROUP, axis=1).astype(BF16)
        qa_scr[k, h, AUG:, :] = jnp.concatenate([bias, jnp.zeros((AUG - n_sel, QW), BF16)], axis=0)
        return ids, n_live

    def window(k, h):
        WK = WINDOW + Q_BLOCK
        w0 = pl.multiple_of(jnp.maximum(t0s[k] - WINDOW, 0), Q_BLOCK)
        sw = _dot(kw_ref[h, pl.ds(w0, WK), :], qa[k, h])
        dist = tqs[k] - (w0 + lax.broadcasted_iota(jnp.int32, (WK, QW), 0))
        in_win = (dist | (WINDOW - 1 - dist)) >= 0
        p_w = _softmax_cols(sw, in_win)
        return _dot(vwt_ref[h, :, pl.ds(w0, WK)], p_w.astype(BF16))

    lists = {c: select(*c, cmp_out[c][1]) for c in chains}
    o_win = {c: window(*c) for c in chains}

    def gathered(k, h, ids, n_slots):
        units = [ids[0, s].astype(jnp.int32) for s in range(n_slots)]
        starts = [pl.multiple_of(u * SEL_UNIT, SEL_UNIT) for u in units]
        keys = jnp.concatenate([ks_ref[h, pl.ds(st, SEL_UNIT), :] for st in starts], axis=0)
        s = _dot(keys, qa_scr[k, h])
        krow = lax.broadcasted_iota(jnp.int32, (SEL_UNIT, QW), 0)
        seen = jnp.concatenate(
            [st + krow <= jnp.where(ids[1, si] > 0.5, tqs[k], -1) for si, st in enumerate(starts)], axis=0)
        s = jnp.where(seen, s, NEG_INF)
        p = jnp.exp(s - jnp.max(s, axis=0, keepdims=True))
        vals = jnp.concatenate([vst_ref[h, :, pl.ds(st, SEL_UNIT)] for st in starts], axis=1)
        return _dot(vals, p.astype(BF16)) / jnp.maximum(jnp.sum(p, axis=0, keepdims=True), 1e-20)

    KC = SEL_KEY_CHUNK

    def swept(k, h):
        last = (t0s[k] + Q_BLOCK + KC - 1) // KC - 1

        def scores_to(ref, j):
            k0 = pl.multiple_of(j * KC, KC)
            ref[...] = _dot(ks_ref[h, pl.ds(k0, KC), :], qa_scr[k, h])

        def accumulate(carry, j, s, p_of):
            m, l, acc = carry
            k0 = pl.multiple_of(j * KC, KC)
            m_new = jnp.maximum(m, jnp.max(s, axis=0, keepdims=True))
            alpha = jnp.exp(m - m_new)
            p = p_of(jnp.exp(s - m_new))
            l = alpha * l + jnp.sum(p, axis=0, keepdims=True)
            acc = alpha * acc + _dot(vst_ref[h, :, pl.ds(k0, KC)], p.astype(BF16))
            return m_new, l, acc

        keep = lambda p: p

        def two_chunks(i, state):
            j = 2 * i
            scores_to(sb_scr, j + 1)
            state = accumulate(state, j, sa_scr[...], keep)
            scores_to(sa_scr, j + 2)
            return accumulate(state, j + 1, sb_scr[...], keep)

        def odd_chunk(state):
            state = accumulate(state, last - 1, sa_scr[...], keep)
            scores_to(sa_scr, last)
            return state

        scores_to(sa_scr, 0)
        init = (jnp.full((1, QW), NEG_INF, F32), jnp.zeros((1, QW), F32), jnp.zeros((HEAD_DIM, QW), F32))
        state = lax.fori_loop(0, last // 2, two_chunks, init)
        state = lax.cond(last % 2 == 1, odd_chunk, lambda st: st, state)
        visible = (last * KC + lax.broadcasted_iota(jnp.int32, (KC, QW), 0)) <= tqs[k]
        _, l_s, acc_s = accumulate(state, last, jnp.where(visible, sa_scr[...], NEG_INF),
                                   lambda p: jnp.where(visible, p, 0.0))
        return acc_s / jnp.maximum(l_s, 1e-20)

    few, some = SEL_SLOTS
    for k in range(ATT_TILES):
        outs = []
        for h in range(N_KV_HEADS):
            ids, n_live = lists[k, h]
            o_s = lax.cond(n_live <= few, lambda _: gathered(k, h, ids, few),
                           lambda _: lax.cond(n_live <= some, lambda _: gathered(k, h, ids, some),
                                              lambda _: swept(k, h), 0), 0)
            gk = lambda br: g_ref[h, br:br + 1, cols(k)]
            o = gk(0) * cmp_out[k, h][0] + gk(1) * o_s + gk(2) * o_win[k, h]
            for g in range(0, GQA_GROUP, 2):
                sq = jnp.concatenate([o[:, g * Q_BLOCK:(g + 1) * Q_BLOCK], o[:, (g + 1) * Q_BLOCK:(g + 2) * Q_BLOCK]],
                                     axis=0)
                outs.append(sq.T)
        o_ref[k * Q_BLOCK:(k + 1) * Q_BLOCK, :] = jnp.concatenate(outs, axis=1).astype(BF16)


def _attend(qt, ks, vst, kwn, vwt, gt, kvc_c):
    B, _, S, _ = ks.shape
    nqb = S // Q_BLOCK
    QW = GQA_GROUP * Q_BLOCK
    n_sel = S // SEL_BLOCK
    n_pick = min(SEL_TOPK, n_sel)
    nr = S // CMP_STRIDE
    n_cmp = (S - CMP_BLOCK) // CMP_STRIDE + 1
    assert n_sel % 16 == 0 and n_sel <= AUG

    cpos = np.arange(nr) * CMP_STRIDE + CMP_BLOCK - 1
    digits = np.zeros((nr, AUG - HEAD_DIM), np.float32)
    digits[:, 0] = cpos // POS_BASE
    digits[:, 1] = cpos % POS_BASE
    kc = kvc_c[:, :N_KV_HEADS]
    kc = jnp.concatenate([kc, jnp.broadcast_to(jnp.asarray(digits, BF16), kc.shape[:2] + digits.shape)], axis=-1)
    vct = kvc_c[:, N_KV_HEADS:].transpose(0, 1, 3, 2)
    slopes = 2.0 ** (-8.0 * np.arange(1, N_HEADS + 1) / N_HEADS)
    slope_t = jnp.asarray(np.repeat(slopes.reshape(N_KV_HEADS, GQA_GROUP), Q_BLOCK, axis=1)
                          .reshape(N_KV_HEADS, 1, QW), F32)
    cmp_start = np.arange(n_cmp) * CMP_STRIDE
    sel_start = np.arange(n_sel) * SEL_BLOCK
    ov = ((cmp_start[:, None] <= sel_start[None, :] + SEL_BLOCK - 1)
          & (cmp_start[:, None] + CMP_BLOCK - 1 >= sel_start[None, :])).astype(np.float32)
    ovt = np.zeros((n_sel, nr), np.float32)
    ovt[:, :n_cmp] = ov.T
    ovt = jnp.asarray(ovt, BF16)

    H = N_KV_HEADS
    per_bh = lambda r, c: pl.BlockSpec((None, H, r, c), lambda b, i: (b, 0, 0, 0))
    per_q = lambda r: pl.BlockSpec((None, H, r, ATT_TILES * QW), lambda b, i: (b, 0, 0, i))
    return pl.pallas_call(
        functools.partial(_attend_kernel, n_sel=n_sel, n_pick=n_pick, n_cmp_rows=nr),
        grid=(B, nqb // ATT_TILES),
        in_specs=[per_q(HEAD_DIM), per_bh(nr, AUG), per_bh(HEAD_DIM, nr),
                  per_bh(S, 2 * AUG), per_bh(HEAD_DIM, S), per_bh(S, AUG), per_bh(HEAD_DIM, S),
                  per_q(3),
                  pl.BlockSpec((H, 1, QW), lambda b, i: (0, 0, 0)),
                  pl.BlockSpec((n_sel, nr), lambda b, i: (0, 0))],
        out_specs=pl.BlockSpec((None, ATT_TILES * Q_BLOCK, NSA_WIDTH), lambda b, i: (b, i, 0)),
        out_shape=jax.ShapeDtypeStruct((B, S, NSA_WIDTH), BF16),
        scratch_shapes=[pltpu.VMEM((ATT_TILES, H, 2 * AUG, QW), BF16), pltpu.VMEM((SEL_KEY_CHUNK, QW), F32),
                        pltpu.VMEM((SEL_KEY_CHUNK, QW), F32)],
        compiler_params=_params(("arbitrary", "arbitrary")),
        name="attend",
    )(qt, kc, vct, ks, vst, kwn, vwt, gt, slope_t, ovt)


def _merge_kernel(x_ref, mod_ref, ys_ref, on_ref, gm_ref, wglu_ref, bglu_ref, wa_ref, wb_ref, wo_ref,
                  gain_ref, wrh_ref, wrl_ref, x1_ref, h2_ref, sc_ref):
    D = D_MODEL
    z = jax.nn.gelu(_from_chunk_major(lambda groups, lanes: ys_ref[:, groups, lanes], x_ref.shape[0]))
    glu = z * jax.nn.sigmoid(_dot(z.astype(BF16), wglu_ref[...]) + bglu_ref[...])
    ya = _dot(glu.astype(BF16), wa_ref[...])
    yb = _dot(on_ref[...], wb_ref[...])
    merged = gm_ref[:, :D].astype(F32) * ya + gm_ref[:, D:].astype(F32) * yb
    x1 = x_ref[...] + mod_ref[2:3, :] * _dot(merged.astype(BF16), wo_ref[...])
    x1_ref[...] = x1
    y = x1 * lax.rsqrt(jnp.mean(x1 * x1, axis=-1, keepdims=True) + RMS_EPS)
    h2 = (y * gain_ref[...]) * (1.0 + mod_ref[4:5, :]) + mod_ref[3:4, :]
    h2_ref[...] = h2.astype(BF16)
    hh, hl = _split(h2)
    nt = (((1,), (1,)), ((), ()))
    dg = lambda a, b: lax.dot_general(a, b, nt, preferred_element_type=F32)
    logits = dg(wrh_ref[...], hh) + (dg(wrh_ref[...], hl) + dg(wrl_ref[...], hh))
    sc_ref[...] = jax.nn.sigmoid(logits)


def _merge(x, mod, ys5, o_nsa, gm, w_glu, b_glu, w_a, w_b, w_out, gain_f, w_router):
    B, S, D = x.shape
    tm = min(ROW_TILE, S)
    wrh, wrl = _split(w_router.T)
    ws = [w_glu.astype(BF16), b_glu.reshape(1, -1), w_a.astype(BF16), w_b.astype(BF16), w_out.astype(BF16),
          gain_f, wrh, wrl]
    row = lambda w: pl.BlockSpec((None, tm, w), lambda b, i: (b, i, 0))
    full = lambda a: pl.BlockSpec(a.shape, lambda b, i: (0,) * a.ndim)
    return pl.pallas_call(
        _merge_kernel,
        grid=(B, S // tm),
        in_specs=[row(D), pl.BlockSpec((None, N_MOD, D), lambda b, i: (b, 0, 0)),
                  _s5_chunk_spec(tm), row(NSA_WIDTH), row(2 * D)] + [full(a) for a in ws],
        out_specs=[row(D), row(D), pl.BlockSpec((None, N_EXPERTS, tm), lambda b, i: (b, 0, i))],
        out_shape=[jax.ShapeDtypeStruct((B, S, D), F32), jax.ShapeDtypeStruct((B, S, D), BF16),
                   jax.ShapeDtypeStruct((B, N_EXPERTS, S), F32)],
        compiler_params=_params(("arbitrary", "arbitrary")),
        name="merge",
    )(x, mod, ys5, o_nsa, gm, *ws)


def _first_argmax_rows(v, idx, n):
    m = jnp.max(v, axis=0, keepdims=True)
    first = jnp.min(jnp.where(v == m, idx, n), axis=0, keepdims=True)
    return idx == first, m


def _route_kernel(sc_ref, bias_ref, tri_ref, rank_ref, w_ref, cnt_ref):
    E, NG = N_EXPERTS, N_EXPERT_GROUPS
    GS = E // NG
    sc = sc_ref[...]
    TM = sc.shape[1]
    sel = sc + bias_ref[...]
    i8 = lax.broadcasted_iota(jnp.int32, (GS, TM), 0)
    gscore = []
    for g in range(NG):
        blk = sel[g * GS:(g + 1) * GS, :]
        hit, m1 = _first_argmax_rows(blk, i8, GS)
        m2 = jnp.max(jnp.where(hit, -jnp.inf, blk), axis=0, keepdims=True)
        gscore.append(m1 + m2)
    gscore = jnp.concatenate(gscore, axis=0)
    ig = lax.broadcasted_iota(jnp.int32, (NG, TM), 0)
    gmask = jnp.zeros((NG, TM), F32)
    for _ in range(TOPK_EXPERT_GROUPS):
        hit, _m = _first_argmax_rows(gscore, ig, NG)
        gmask = jnp.where(hit, 1.0, gmask)
        gscore = jnp.where(hit, -jnp.inf, gscore)
    emask = jnp.concatenate([jnp.broadcast_to(gmask[g:g + 1, :], (GS, TM)) for g in range(NG)], axis=0)
    cand = jnp.where(emask > 0.5, sel, NEG_INF)
    ie = lax.broadcasted_iota(jnp.int32, (E, TM), 0)
    chosen = jnp.zeros((E, TM), F32)
    for _ in range(TOP_K):
        hit, _m = _first_argmax_rows(cand, ie, E)
        chosen = jnp.where(hit, 1.0, chosen)
        cand = jnp.where(hit, -jnp.inf, cand)
    w = chosen * sc
    w = w / jnp.sum(w, axis=0, keepdims=True) * ROUTED_SCALE
    sub = rank_ref.shape[-1]
    for q in range(TM // sub):
        cols = slice(q * sub, (q + 1) * sub)
        cb = chosen[:, cols].astype(BF16)
        prefix = _dot(cb, tri_ref[...])
        rank_ref[q] = jnp.where(chosen[:, cols] > 0.5, prefix, RANK_NONE)
        w_ref[q] = w[:, cols]
        cnt_ref[q] = _dot(cb, jnp.ones((sub, LANES), BF16))


def _route(scores_t, router_bias):
    B, E, S = scores_t.shape
    TM = min(ROUTE_TILE, S)
    per_step = min(MOE_OUTER, S) // TM
    nj = S // (TM * per_step)
    ns = B * nj * per_step
    tri = jnp.asarray(np.triu(np.ones((TM, TM), np.float32), k=1), BF16)
    tile = lambda w: pl.BlockSpec((per_step, E, w), lambda b, j: (b * nj + j, 0, 0))
    return pl.pallas_call(
        _route_kernel,
        grid=(B, nj),
        in_specs=[pl.BlockSpec((None, E, TM * per_step), lambda b, j: (b, 0, j)),
                  pl.BlockSpec((E, 1), lambda b, j: (0, 0)),
                  pl.BlockSpec((TM, TM), lambda b, j: (0, 0))],
        out_specs=[tile(TM), tile(TM), tile(LANES)],
        out_shape=[jax.ShapeDtypeStruct((ns, E, TM), F32), jax.ShapeDtypeStruct((ns, E, TM), F32),
                   jax.ShapeDtypeStruct((ns, E, LANES), F32)],
        compiler_params=_params(("arbitrary", "arbitrary")),
        name="route",
    )(scores_t, router_bias.reshape(E, 1), tri)


def _moe_kernel(cnt_ref, h_ref, x1_ref, mod_ref, rank_ref, w_ref, wg_ref, wu_ref, wd_ref,
                sg_ref, su_ref, sd_ref, o_ref, xc, yc, pc, wc, *, n_sub, tm, n_exp):
    to = pl.program_id(0)
    eb = pl.program_id(1)
    gate = mod_ref[5:6, :]

    @pl.when(eb == 0)
    def _shared():
        for s in range(n_sub):
            rows = pl.ds(s * tm, tm)
            hs = h_ref[rows, :]
            hid = jax.nn.silu(_dot(hs, sg_ref[...])) * _dot(hs, su_ref[...])
            o_ref[rows, :] = x1_ref[rows, :] + gate * _dot(hid.astype(BF16), sd_ref[...])
            xc[s] = jnp.zeros(xc.shape[1:], BF16)
            yc[s] = jnp.zeros(yc.shape[1:], BF16)
            wc[s] = jnp.zeros(wc.shape[1:], F32)

    PIECE, WIN = MOE_PIECE, MOE_WIN
    tn = (((0,), (0,)), ((), ()))
    experts = [eb * n_exp + j for j in range(n_exp)]
    cnt = [[cnt_ref[(to * n_sub + s) * N_EXPERTS + e] for e in experts] for s in range(n_sub)]
    off = []
    for s in range(n_sub):
        o = [jnp.int32(0)]
        for j in range(n_exp):
            o.append(o[-1] + ((cnt[s][j] + 15) // 16) * 16)
        off.append(o)
    n_pieces = [(off[s][n_exp] + PIECE - 1) // PIECE for s in range(n_sub)]
    fits = n_pieces[0] < MOE_ROWS // PIECE
    for s in range(1, n_sub):
        fits = jnp.logical_and(fits, n_pieces[s] < MOE_ROWS // PIECE)

    def one_hot_rows(s, base, rows_n, js):
        slot = (lax.broadcasted_iota(jnp.int32, (rows_n, tm), 0) + base).astype(F32)
        pick = jnp.zeros((rows_n, tm), F32)
        wacc = jnp.zeros((rows_n, tm), F32)
        for j in js:
            hit = (slot - off[s][j].astype(F32)) == rank_ref[s, pl.ds(experts[j], 1), :]
            pick = jnp.where(hit, 1.0, pick)
            wacc = jnp.where(hit, w_ref[s, pl.ds(experts[j], 1), :], wacc)
        return pick.astype(BF16), jnp.sum(wacc, axis=1, keepdims=True)

    def mlp(xg, j):
        hid = jax.nn.silu(_dot(xg, wg_ref[j])) * _dot(xg, wu_ref[j])
        return _dot(hid.astype(BF16), wd_ref[j])

    def piece(s, q):
        r = pl.ds(q * PIECE, PIECE)
        pick, wrow = one_hot_rows(s, q * PIECE, PIECE, range(n_exp))
        xc[s, r, :] = _dot(pick, h_ref[pl.ds(s * tm, tm), :]).astype(BF16)
        pc[s, :, r] = pick.T
        wc[s, r, :] = wrow

    def window(j, p, first):
        starts = [pl.multiple_of(jnp.minimum(off[s][j] + p * WIN, (n_pieces[s] + 1) * PIECE - WIN), 16)
                  for s in range(n_sub)]
        xg = jnp.concatenate([xc[s, pl.ds(starts[s], WIN), :] for s in range(n_sub)], axis=0)
        out = mlp(xg, j)
        for s in range(n_sub):
            r = pl.ds(starts[s], WIN)
            new = (out[s * WIN:(s + 1) * WIN] * wc[s, r, :] * gate).astype(BF16)
            if not first:
                rank = starts[s] - off[s][j] + lax.broadcasted_iota(jnp.int32, (WIN, 1), 0)
                new = jnp.where(rank < cnt[s][j], new, yc[s, r, :])
            yc[s, r, :] = new

    def scatter(s, start, n):
        r = pl.ds(start, n)
        o_ref[pl.ds(s * tm, tm), :] += _dot(pc[s, :, r], yc[s, r, :])

    @pl.when(fits)
    def _packed():
        for s in range(n_sub):
            piece(s, 0)
            piece(s, 1)
        for s in range(n_sub):
            @pl.when(n_pieces[s] > 2)
            def _third(s=s):
                piece(s, 2)
        for j in range(n_exp):
            window(j, 0, True)
        for j in range(n_exp):
            most = cnt[0][j]
            for s in range(1, n_sub):
                most = jnp.maximum(most, cnt[s][j])

            @pl.when(most > WIN)
            def _more(j=j, most=most):
                lax.fori_loop(1, (most + WIN - 1) // WIN, lambda p, c: (window(j, p, False), c)[1], 0)
        for s in range(n_sub):
            scatter(s, 0, 2 * PIECE)
        for s in range(n_sub):
            @pl.when(n_pieces[s] > 2)
            def _third_out(s=s):
                scatter(s, 2 * PIECE, PIECE)

    @pl.when(jnp.logical_not(fits))
    def _unpacked():
        for j in range(n_exp):
            for s in range(n_sub):
                rows = pl.ds(s * tm, tm)

                def block(bi, carry, j=j, s=s, rows=rows):
                    slot0 = off[s][j] + bi * PIECE
                    pick, wrow = one_hot_rows(s, slot0, PIECE, [j])
                    out = mlp(_dot(pick, h_ref[rows, :]).astype(BF16), j)
                    ow = (out * wrow * gate).astype(BF16)
                    o_ref[rows, :] += lax.dot_general(pick, ow, tn, preferred_element_type=F32)
                    return carry

                lax.fori_loop(0, (cnt[s][j] + PIECE - 1) // PIECE, block, 0)


def _moe(h2, x1, mod, rank_t, w_t, counts, w_gate, w_up, w_down, ws_gate, ws_up, ws_down, seq):
    T, D = h2.shape
    ns, E, tm = rank_t.shape
    tmo = min(MOE_OUTER, seq)
    n_sub = tmo // tm
    n_exp = MOE_EXPERTS_PER_STEP
    per_b = seq // tmo
    ws = [w_gate.astype(BF16), w_up.astype(BF16), w_down.astype(BF16)]
    sh = [ws_gate.astype(BF16), ws_up.astype(BF16), ws_down.astype(BF16)]
    tok = lambda: pl.BlockSpec((tmo, D), lambda t, e, c: (t, 0))
    sub = lambda: pl.BlockSpec((n_sub, E, tm), lambda t, e, c: (t, 0, 0))
    exp = lambda a: pl.BlockSpec((n_exp,) + a.shape[1:], lambda t, e, c: (e, 0, 0))
    full = lambda a: pl.BlockSpec(a.shape, lambda t, e, c: (0, 0))
    grid_spec = pltpu.PrefetchScalarGridSpec(
        num_scalar_prefetch=1,
        grid=(T // tmo, E // n_exp),
        in_specs=[tok(), tok(), pl.BlockSpec((None, N_MOD, D), lambda t, e, c: (t // per_b, 0, 0)),
                  sub(), sub(), exp(ws[0]), exp(ws[1]), exp(ws[2]), full(sh[0]), full(sh[1]), full(sh[2])],
        out_specs=tok(),
        scratch_shapes=[pltpu.VMEM((n_sub, MOE_ROWS, D), BF16), pltpu.VMEM((n_sub, MOE_ROWS, D), BF16),
                        pltpu.VMEM((n_sub, tm, MOE_ROWS), BF16), pltpu.VMEM((n_sub, MOE_ROWS, 1), F32)],
    )
    return pl.pallas_call(
        functools.partial(_moe_kernel, n_sub=n_sub, tm=tm, n_exp=n_exp),
        grid_spec=grid_spec,
        out_shape=jax.ShapeDtypeStruct((T, D), F32),
        compiler_params=_params(("arbitrary", "arbitrary")),
        name="moe",
    )(counts, h2, x1, mod, rank_t, w_t, *ws, *sh)


def kernel(x, c, w_ada, b_ada, norm_mix_gain, norm_ffn_gain, w_in, s5_lambda_re, s5_lambda_im, s5_log_dt, s5_b_re, s5_b_im, s5_c_re, s5_c_im, s5_d, s5_w_glu, s5_b_glu, q_norm_gain, k_norm_gain, cmp_pe, cmp_w1, cmp_b1, cmp_w2, cmp_b2, w_branch_a, w_branch_b, w_out, w_router, router_bias, w_gate, w_up, w_down, ws_gate, ws_up, ws_down):
    B, S, D = x.shape
    for l in range(w_ada.shape[0]):
        mod = _ada(c, w_ada[l], b_ada[l]).reshape(B, N_MOD, D)
        u, kvc, qt, ks, vst, kwn, vwt, gt, gm = _inproj(x, mod, norm_mix_gain[l:l + 1], w_in[l],
                                                        q_norm_gain[l], k_norm_gain[l])
        tables = _s5_tables(s5_lambda_re[l], s5_lambda_im[l], s5_log_dt[l], s5_b_re[l], s5_b_im[l],
                            s5_c_re[l], s5_c_im[l], s5_d[l])
        ys5 = _s5(u, tables)
        kvc_c = _compress(kvc, cmp_pe[l], cmp_w1[l], cmp_b1[l], cmp_w2[l], cmp_b2[l], k_norm_gain[l, 0])
        o_nsa = _attend(qt, ks, vst, kwn, vwt, gt, kvc_c)
        x1, h2, scores_t = _merge(x, mod, ys5, o_nsa, gm, s5_w_glu[l], s5_b_glu[l], w_branch_a[l],
                                  w_branch_b[l], w_out[l], norm_ffn_gain[l:l + 1], w_router[l])
        rank_t, w_t, cnt = _route(scores_t, router_bias[l])
        counts = cnt[:, :, 0].astype(jnp.int32).reshape(-1)
        x = _moe(h2.reshape(B * S, D), x1.reshape(B * S, D), mod, rank_t, w_t, counts,
                 w_gate[l], w_up[l], w_down[l], ws_gate[l], ws_up[l], ws_down[l], S).reshape(B, S, D)
    return x
```

```python
import functools

import numpy as np
import jax
import jax.numpy as jnp
from jax import lax
from jax.experimental import pallas as pl
from jax.experimental.pallas import tpu as pltpu

F32 = jnp.float32
BF16 = jnp.bfloat16

D_MODEL = 1024
S5_WIDTH = 512
S5_GROUP = 16
S5_GROUPS = S5_WIDTH // S5_GROUP
S5_STATE = 64
N_HEADS = 8
N_KV_HEADS = 2
GQA_GROUP = N_HEADS // N_KV_HEADS
HEAD_DIM = 64
NSA_WIDTH = N_HEADS * HEAD_DIM
KV_WIDTH = 2 * N_KV_HEADS * HEAD_DIM
CMP_BLOCK = 32
CMP_STRIDE = 16
CMP_HIDDEN = 256
SEL_BLOCK = 64
SEL_TOPK = 8
WINDOW = 256
Q_BLOCK = 128
FORCE_BONUS = 1e3
N_EXPERTS = 64
TOP_K = 8
N_EXPERT_GROUPS = 8
TOPK_EXPERT_GROUPS = 4
ROUTED_SCALE = 2.5
RMS_EPS = 1e-6
NEG_INF = -1e30
N_MOD = 6

LANES = 128
SUBLANES = 8
S5_CHUNK = 16
S5_CW = S5_CHUNK * S5_GROUP
ROW_TILE = 512
ATT_TILES = 2
SEL_KEY_CHUNK = 512
SEL_UNIT = 128
SEL_SLOTS = (5, 8)
POS_BASE = 64
POS_ROWS = 16
AUG = 128
ROUTE_TILE = 256
MOE_OUTER = 1024
MOE_EXPERTS_PER_STEP = 4
MOE_PIECE = 128
MOE_BASE_ROWS = 208
MOE_WIN = 64
MOE_ROWS = 512
RANK_NONE = -float(1 << 20)
VMEM_LIMIT = 56 * 1024 * 1024


def _dot(a, b):
    return jnp.dot(a, b, preferred_element_type=F32)


def _split(a):
    hi = a.astype(BF16)
    lo = (a - hi.astype(F32)).astype(BF16)
    return hi, lo


def _dot3(a, bh, bl):
    ah, al = _split(a)
    return _dot(ah, bh) + (_dot(al, bh) + _dot(ah, bl))


def _segment_transpose(x):
    assert x.shape[-2:] == (SUBLANES, LANES) and LANES // S5_GROUP == SUBLANES
    nd = x.ndim
    i = lax.broadcasted_iota(jnp.int32, x.shape, nd - 2)
    seg = lax.broadcasted_iota(jnp.int32, x.shape, nd - 1) // S5_GROUP
    out = x
    for d in range(1, SUBLANES):
        r = pltpu.roll(pltpu.roll(x, SUBLANES - d, axis=nd - 2), S5_GROUP * d, axis=nd - 1)
        out = jnp.where(seg == ((i + d) & (SUBLANES - 1)), r, out)
    return out


def _to_chunk_major(u, put):
    rows = u.shape[0]
    u3 = u.reshape(rows // SUBLANES, SUBLANES, S5_WIDTH)
    halves = S5_CHUNK // SUBLANES
    for jb in range(S5_WIDTH // LANES):
        t = _segment_transpose(u3[:, :, LANES * jb:LANES * (jb + 1)])
        t = t.reshape(rows // S5_CHUNK, halves, SUBLANES, LANES)
        for hf in range(halves):
            put(slice(SUBLANES * jb, SUBLANES * (jb + 1)), slice(LANES * hf, LANES * (hf + 1)), t[:, hf])


def _from_chunk_major(get, rows):
    halves = S5_CHUNK // SUBLANES
    cols = []
    for jb in range(S5_WIDTH // LANES):
        parts = [_segment_transpose(get(slice(SUBLANES * jb, SUBLANES * (jb + 1)), slice(LANES * hf, LANES * (hf + 1))))
                 for hf in range(halves)]
        cols.append(jnp.stack(parts, axis=1).reshape(rows, LANES))
    return jnp.concatenate(cols, axis=1)


def _params(sem):
    return pltpu.CompilerParams(dimension_semantics=sem, vmem_limit_bytes=VMEM_LIMIT)


def _ada_kernel(c_ref, w_ref, b_ref, o_ref):
    cs = jax.nn.silu(c_ref[...])
    wh, wl = _split(w_ref[...])
    o_ref[...] = _dot3(cs, wh, wl) + b_ref[...]


def _ada(c, w_ada, b_ada):
    B, D = c.shape
    return pl.pallas_call(
        _ada_kernel,
        grid=(N_MOD,),
        in_specs=[pl.BlockSpec((B, D), lambda j: (0, 0)),
                  pl.BlockSpec((D, D), lambda j: (0, j)),
                  pl.BlockSpec((1, D), lambda j: (0, j))],
        out_specs=pl.BlockSpec((B, D), lambda j: (0, j)),
        out_shape=jax.ShapeDtypeStruct((B, N_MOD * D), F32),
        compiler_params=_params(("arbitrary",)),
        name="ada",
    )(c, w_ada, b_ada.reshape(1, N_MOD * D))


def _head_norm(v, bd, gain):
    sq = v * v
    sh, sl = _split(sq)
    ms = _dot(sh, bd) + _dot(sl, bd)
    return v * lax.rsqrt(ms + RMS_EPS) * gain


def _inproj_kernel(x_ref, mod_ref, gain_ref, wm_ref, wt_ref, wgn_ref, wgm_ref, bd_ref, qg_ref, kg_ref,
                   u_ref, kvc_ref, qt_ref, ksa_ref, vst_ref, kwa_ref, vwt_ref, gt_ref, gm_ref):
    x = x_ref[...]
    shift = mod_ref[0:1, :]
    scale = mod_ref[1:2, :]
    y = x * lax.rsqrt(jnp.mean(x * x, axis=-1, keepdims=True) + RMS_EPS)
    h = (y * gain_ref[...]) * (1.0 + scale) + shift
    hb = h.astype(BF16)
    main = _dot(hb, wm_ref[...])
    tm = x.shape[0]

    def put(groups, lanes, block):
        u_ref[:, groups, lanes] = block

    _to_chunk_major(main[:, :S5_WIDTH], put)
    o = S5_WIDTH
    for part in range(KV_WIDTH // LANES):
        kvc_ref[part] = main[:, o + part * LANES:o + (part + 1) * LANES]
    o += KV_WIDTH
    kw = N_KV_HEADS * HEAD_DIM
    bd = bd_ref[...]
    ks = _head_norm(main[:, o:o + kw], bd[:kw, :kw], kg_ref[1:2, :]).astype(BF16)
    kwn = _head_norm(main[:, o + kw:o + 2 * kw], bd[:kw, :kw], kg_ref[2:3, :]).astype(BF16)
    pos = pl.program_id(1) * tm + lax.broadcasted_iota(jnp.int32, (tm, 1), 0)
    lane = lax.broadcasted_iota(jnp.int32, (tm, AUG), 1)
    digits = jnp.where(lane == 0, pos // POS_BASE, jnp.where(lane == 1, pos % POS_BASE, 0))
    digits = digits[:, :AUG - HEAD_DIM].astype(F32).astype(BF16)
    onehot = jnp.where(lane == pos // SEL_BLOCK, 1.0, 0.0).astype(BF16)
    for hh in range(N_KV_HEADS):
        ksa_ref[hh, :, 0:HEAD_DIM] = ks[:, hh * HEAD_DIM:(hh + 1) * HEAD_DIM]
        ksa_ref[hh, :, HEAD_DIM:AUG] = digits
        ksa_ref[hh, :, AUG:] = onehot
        kwa_ref[hh, :, 0:HEAD_DIM] = kwn[:, hh * HEAD_DIM:(hh + 1) * HEAD_DIM]
        kwa_ref[hh, :, HEAD_DIM:] = digits

    nt = (((1,), (1,)), ((), ()))
    tt = lax.dot_general(wt_ref[...], hb, nt, preferred_element_type=F32)
    qt = tt[:NSA_WIDTH]
    sq = qt * qt
    sh, sl = _split(sq)
    qn = (qt * lax.rsqrt(_dot(bd, sh) + _dot(bd, sl) + RMS_EPS) * qg_ref[...] * (HEAD_DIM ** -0.5)).astype(BF16)
    QW = GQA_GROUP * Q_BLOCK
    for qb in range(tm // Q_BLOCK):
        for hd in range(N_HEADS):
            hh, g = divmod(hd, GQA_GROUP)
            qt_ref[hh, :, qb * QW + g * Q_BLOCK:qb * QW + (g + 1) * Q_BLOCK] = (
                qn[hd * HEAD_DIM:(hd + 1) * HEAD_DIM, qb * Q_BLOCK:(qb + 1) * Q_BLOCK])
    vst_ref[...] = tt[NSA_WIDTH:NSA_WIDTH + kw].reshape(N_KV_HEADS, HEAD_DIM, tm).astype(BF16)
    vwt_ref[...] = tt[NSA_WIDTH + kw:].reshape(N_KV_HEADS, HEAD_DIM, tm).astype(BF16)
    gn = jax.nn.sigmoid(lax.dot_general(wgn_ref[...], hb, nt, preferred_element_type=F32))
    for qb in range(tm // Q_BLOCK):
        for br in range(3):
            for hd in range(N_HEADS):
                hh, g = divmod(hd, GQA_GROUP)
                r = br * N_HEADS + hd
                gt_ref[hh, br:br + 1, qb * QW + g * Q_BLOCK:qb * QW + (g + 1) * Q_BLOCK] = (
                    gn[r:r + 1, qb * Q_BLOCK:(qb + 1) * Q_BLOCK])
    gm_ref[...] = jax.nn.sigmoid(_dot(hb, wgm_ref[...])).astype(BF16)


def _s5_chunk_spec(tm):
    assert S5_CHUNK == 16 and S5_GROUP == 16, "the segment transposes assume 16 steps x 16 channels"
    return pl.BlockSpec((tm // S5_CHUNK, None, S5_GROUPS, S5_CW), lambda b, i: (i, b, 0, 0))


def _inproj(x, mod, gain, w_in, q_gain, k_gain):
    B, S, D = x.shape
    tm = min(ROW_TILE, S)
    assert S // SEL_BLOCK <= AUG and tm % Q_BLOCK == 0
    kw = N_KV_HEADS * HEAD_DIM
    cols = np.cumsum((0,) + (S5_WIDTH, NSA_WIDTH, KV_WIDTH, KV_WIDTH, KV_WIDTH, 3 * N_HEADS, 2 * D))
    c_u, c_q, c_kvc, c_kvs, c_kvw, c_gn, c_gm = cols[:7]
    sl = lambda a, n: w_in[:, a:a + n]
    wm = jnp.concatenate([sl(c_u, S5_WIDTH), sl(c_kvc, KV_WIDTH), sl(c_kvs, kw), sl(c_kvw, kw)], axis=1).astype(BF16)
    wt = jnp.concatenate([sl(c_q, NSA_WIDTH), sl(c_kvs + kw, kw), sl(c_kvw + kw, kw)], axis=1).T.astype(BF16)
    wgn = sl(c_gn, 3 * N_HEADS).T.astype(BF16)
    wgm = sl(c_gm, 2 * D).astype(BF16)
    seg = np.arange(NSA_WIDTH) // HEAD_DIM
    bd = jnp.asarray((seg[:, None] == seg[None, :]).astype(np.float32) / HEAD_DIM, BF16)
    qg = jnp.tile(q_gain, N_HEADS).reshape(NSA_WIDTH, 1)
    kg = jnp.tile(k_gain, (1, N_KV_HEADS))
    nq = tm // Q_BLOCK * GQA_GROUP * Q_BLOCK
    row = lambda w: pl.BlockSpec((None, tm, w), lambda b, i: (b, i, 0))
    full = lambda a: pl.BlockSpec(a.shape, lambda b, i: (0,) * a.ndim)
    rows4 = lambda w: pl.BlockSpec((None, N_KV_HEADS, tm, w), lambda b, i: (b, 0, i, 0))
    cols4 = lambda r, w: pl.BlockSpec((None, N_KV_HEADS, r, w), lambda b, i: (b, 0, 0, i))
    nqt = S // Q_BLOCK * GQA_GROUP * Q_BLOCK
    return pl.pallas_call(
        _inproj_kernel,
        grid=(B, S // tm),
        in_specs=[row(D), pl.BlockSpec((None, N_MOD, D), lambda b, i: (b, 0, 0)),
                  full(gain), full(wm), full(wt), full(wgn), full(wgm), full(bd), full(qg), full(kg)],
        out_specs=[_s5_chunk_spec(tm), pl.BlockSpec((None, KV_WIDTH // LANES, tm, LANES), lambda b, i: (b, 0, i, 0)),
                   cols4(HEAD_DIM, nq), rows4(2 * AUG), cols4(HEAD_DIM, tm),
                   rows4(AUG), cols4(HEAD_DIM, tm), cols4(3, nq), row(2 * D)],
        out_shape=[jax.ShapeDtypeStruct((S // S5_CHUNK, B, S5_GROUPS, S5_CW), F32),
                   jax.ShapeDtypeStruct((B, KV_WIDTH // LANES, S, LANES), F32),
                   jax.ShapeDtypeStruct((B, N_KV_HEADS, HEAD_DIM, nqt), BF16),
                   jax.ShapeDtypeStruct((B, N_KV_HEADS, S, 2 * AUG), BF16),
                   jax.ShapeDtypeStruct((B, N_KV_HEADS, HEAD_DIM, S), BF16),
                   jax.ShapeDtypeStruct((B, N_KV_HEADS, S, AUG), BF16),
                   jax.ShapeDtypeStruct((B, N_KV_HEADS, HEAD_DIM, S), BF16),
                   jax.ShapeDtypeStruct((B, N_KV_HEADS, 3, nqt), F32),
                   jax.ShapeDtypeStruct((B, S, 2 * D), BF16)],
        compiler_params=_params(("arbitrary", "arbitrary")),
        name="inproj",
    )(x, mod, gain, wm, wt, wgn, wgm, bd, qg, kg)


def _s5_tables(lam_re, lam_im, log_dt, b_re, b_im, c_re, c_im, d_skip):
    L, G = S5_CHUNK, S5_GROUPS
    lr, li = lam_re.astype(F32), lam_im.astype(F32)
    dt = jnp.exp(log_dt.astype(F32))[:, None]
    mag = jnp.exp(lr * dt)
    abar_re, abar_im = mag * jnp.cos(li * dt), mag * jnp.sin(li * dt)
    num_re, num_im = abar_re - 1.0, abar_im
    den = lr * lr + li * li
    coef_re = (num_re * lr + num_im * li) / den
    coef_im = (num_im * lr - num_re * li) / den
    br, bi = b_re.astype(F32), b_im.astype(F32)
    bbar_re = coef_re[..., None] * br - coef_im[..., None] * bi
    bbar_im = coef_re[..., None] * bi + coef_im[..., None] * br
    k = jnp.arange(L + 1, dtype=F32)[:, None]
    pmag = jnp.exp((lr * dt)[:, None, :] * k)
    ang = (li * dt)[:, None, :] * k
    pw = jnp.stack([jnp.tile(pmag * jnp.cos(ang), (1, 1, 2)), jnp.tile(pmag * jnp.sin(ang), (1, 1, 2))], axis=1)
    brt, bit = bbar_re.transpose(0, 2, 1), bbar_im.transpose(0, 2, 1)
    ba = jnp.concatenate([brt, bit], axis=2)
    bb = jnp.concatenate([-bit, brt], axis=2)
    cr, ci = c_re.astype(F32), c_im.astype(F32)
    ca = jnp.concatenate([cr, -ci], axis=2)
    cb = jnp.concatenate([-ci, -cr], axis=2)
    dv = jnp.tile(d_skip.astype(F32), (1, L)).reshape(G, 1, S5_CW)
    return pw, ba, bb, ca, cb, dv


def _s5_kernel(u_ref, pw_ref, ba_ref, bb_ref, ca_ref, cb_ref, dv_ref,
               y_ref, vr_scr, vi_scr, xr_scr, xi_scr, *, n_chunks, bsz):
    L, P, N = S5_CHUNK, S5_GROUP, S5_STATE
    nt = (((1,), (1,)), ((), ()))
    dot_nt = lambda a, b: lax.dot_general(a, b, nt, preferred_element_type=F32)
    mix = lambda a, b, k: a * pw_ref[0, k:k + 1, :] + b * pw_ref[1, k:k + 1, :]
    ba, bb, ca, cb = ba_ref[...], bb_ref[...], ca_ref[...], cb_ref[...]
    ws = jnp.concatenate([mix(ba, bb, L - 1 - s) for s in range(L)], axis=0)
    wot = jnp.concatenate([mix(ca, cb, k) for k in range(L + 1)], axis=0)
    wsh, wsl = _split(ws)
    wth, wtl = _split(wot)
    bah, bal = _split(ba)
    kern = dot_nt(bah, wth[:L * P]) + (dot_nt(bal, wth[:L * P]) + dot_nt(bah, wtl[:L * P]))
    lane = lax.broadcasted_iota(jnp.int32, kern.shape, 1)
    mt = jnp.concatenate([kern] + [jnp.where(lane >= P * s, pltpu.roll(kern, P * s, axis=1), 0.0)
                                   for s in range(1, L)], axis=0)
    mth, mtl = _split(mt)

    u = u_ref[...]
    uh, ul = _split(u)
    y = _dot(uh, mth) + (_dot(ul, mth) + _dot(uh, mtl))
    v = _dot(uh, wsh) + (_dot(ul, wsh) + _dot(uh, wsl))
    vr_scr[...] = v[:, :N]
    vi_scr[...] = v[:, N:]
    a_r = pw_ref[0, L:L + 1, :N]
    a_i = pw_ref[1, L:L + 1, :N]

    def step(c, x):
        xr, xi = x
        r = pl.ds(pl.multiple_of(c * bsz, bsz), bsz)
        xr_scr[r, :] = xr
        xi_scr[r, :] = xi
        return (a_r * xr - a_i * xi + vr_scr[r, :], a_r * xi + a_i * xr + vi_scr[r, :])

    zero = jnp.zeros((bsz, N), F32)
    lax.fori_loop(0, n_chunks, step, (zero, zero), unroll=8)
    xh, xl = _split(jnp.concatenate([xr_scr[...], xi_scr[...]], axis=1))
    y = y + (dot_nt(xh, wth[P:]) + (dot_nt(xl, wth[P:]) + dot_nt(xh, wtl[P:])))
    y_ref[...] = y + dv_ref[...] * u


def _s5(ug, tables):
    nc, B, G, cw = ug.shape
    N = S5_STATE
    grp = lambda a: pl.BlockSpec((None,) + a.shape[1:], lambda g: (g,) + (0,) * (a.ndim - 1))
    rows = pl.BlockSpec((nc * B, cw), lambda g: (0, g))
    y = pl.pallas_call(
        functools.partial(_s5_kernel, n_chunks=nc, bsz=B),
        grid=(G,),
        in_specs=[rows] + [grp(t) for t in tables],
        out_specs=rows,
        out_shape=jax.ShapeDtypeStruct((nc * B, G * cw), F32),
        scratch_shapes=[pltpu.VMEM((nc * B, N), F32)] * 4,
        compiler_params=_params(("arbitrary",)),
        name="s5",
    )(ug.reshape(nc * B, G * cw), *tables)
    return y.reshape(nc, B, G, cw)


def _compress_kernel(x_ref, pe_ref, w1_ref, b1_ref, w2_ref, b2_ref, kg_ref, o_ref, *, n_rows):
    half = CMP_STRIDE * HEAD_DIM
    for j in range(2 * N_KV_HEADS):
        kv, hd = divmod(j, N_KV_HEADS)
        x = jnp.concatenate(
            [x_ref[kv, pl.ds(t, n_rows, stride=CMP_STRIDE), :][:, hd * HEAD_DIM:(hd + 1) * HEAD_DIM]
             for t in range(CMP_STRIDE)], axis=1)
        a = _dot((x + pe_ref[kv, 0:1, :]).astype(BF16), w1_ref[kv, :half, :])
        b = _dot((x + pe_ref[kv, 1:2, :]).astype(BF16), w1_ref[kv, half:, :])
        hid = jax.nn.gelu(a + pltpu.roll(b, n_rows - 1, axis=0) + b1_ref[kv])
        out = _dot(hid.astype(BF16), w2_ref[kv]) + b2_ref[kv]
        if kv == 0:
            out = out * lax.rsqrt(jnp.mean(out * out, axis=-1, keepdims=True) + RMS_EPS) * kg_ref[...]
        o_ref[j] = out.astype(BF16)


def _compress(kv_c, cmp_pe, cmp_w1, cmp_b1, cmp_w2, cmp_b2, k_gain0):
    B, _, S, _ = kv_c.shape
    nr = S // CMP_STRIDE
    half = CMP_STRIDE * HEAD_DIM
    nj = 2 * N_KV_HEADS
    ws = [cmp_pe.reshape(2, 2, half), cmp_w1.astype(BF16), cmp_b1.reshape(2, 1, CMP_HIDDEN), cmp_w2.astype(BF16),
          cmp_b2.reshape(2, 1, HEAD_DIM), k_gain0.reshape(1, HEAD_DIM)]
    full = lambda a: pl.BlockSpec(a.shape, lambda b: (0,) * a.ndim)
    return pl.pallas_call(
        functools.partial(_compress_kernel, n_rows=nr),
        grid=(B,),
        in_specs=[pl.BlockSpec((None,) + kv_c.shape[1:], lambda b: (b, 0, 0, 0))] + [full(a) for a in ws],
        out_specs=pl.BlockSpec((None, nj, nr, HEAD_DIM), lambda b: (b, 0, 0, 0)),
        out_shape=jax.ShapeDtypeStruct((B, nj, nr, HEAD_DIM), BF16),
        compiler_params=_params(("arbitrary",)),
        name="compress",
    )(kv_c, *ws)


def _softmax_cols(s, mask):
    s = jnp.where(mask, s, NEG_INF)
    m = jnp.max(s, axis=0, keepdims=True)
    p = jnp.where(mask, jnp.exp(s - m), 0.0)
    return p / jnp.maximum(jnp.sum(p, axis=0, keepdims=True), 1e-20)


def _attend_kernel(qt_ref, kc_ref, vct_ref, ks_ref, vst_ref, kw_ref, vwt_ref, g_ref, slope_ref, ovt_ref,
                   o_ref, qa_scr, sa_scr, sb_scr, *, n_sel, n_pick, n_cmp_rows):
    QW = GQA_GROUP * Q_BLOCK
    chains = [(k, h) for k in range(ATT_TILES) for h in range(N_KV_HEADS)]
    t0s = [(pl.program_id(1) * ATT_TILES + k) * Q_BLOCK for k in range(ATT_TILES)]
    tqs = [t0 + (lax.broadcasted_iota(jnp.int32, (1, QW), 1) & (Q_BLOCK - 1)) for t0 in t0s]
    cols = lambda k: slice(k * QW, (k + 1) * QW)

    r = lax.broadcasted_iota(jnp.int32, (POS_ROWS, QW), 0)
    qa = {}
    for k, h in chains:
        slope = slope_ref[h]
        qa_scr[k, h, 0:HEAD_DIM, :] = qt_ref[h, :, cols(k)]
        qa_scr[k, h, HEAD_DIM:HEAD_DIM + POS_ROWS, :] = jnp.where(
            r == 0, slope * POS_BASE, jnp.where(r == 1, slope, 0.0)).astype(BF16)
        qa_scr[k, h, HEAD_DIM + POS_ROWS:AUG, :] = jnp.zeros((AUG - HEAD_DIM - POS_ROWS, QW), BF16)
        qa[k, h] = qa_scr[k, h, 0:AUG, :]

    def compressed(k, h, rows):
        sc = _dot(kc_ref[h, 0:rows, :], qa[k, h])
        cpos = lax.broadcasted_iota(jnp.int32, (rows, QW), 0) * CMP_STRIDE + (CMP_BLOCK - 1)
        p_c = _softmax_cols(sc, cpos <= tqs[k])
        o_c = _dot(vct_ref[h, :, 0:rows], p_c.astype(BF16))
        psum = p_c[:, 0:Q_BLOCK]
        for g in range(1, GQA_GROUP):
            psum = psum + p_c[:, g * Q_BLOCK:(g + 1) * Q_BLOCK]
        ph, pl_ = _split(psum)
        return o_c, _dot(ovt_ref[:, 0:rows], ph) + _dot(ovt_ref[:, 0:rows], pl_)

    half = n_cmp_rows // 2
    if half % LANES == 0:
        early = (t0s[-1] + Q_BLOCK - CMP_BLOCK) // CMP_STRIDE < half
        cmp_out = lax.cond(early, lambda _: [compressed(k, h, half) for k, h in chains],
                           lambda _: [compressed(k, h, n_cmp_rows) for k, h in chains], 0)
    else:
        cmp_out = [compressed(k, h, n_cmp_rows) for k, h in chains]
    cmp_out = dict(zip(chains, cmp_out))

    def select(k, h, imp):
        jb = lax.broadcasted_iota(jnp.int32, (n_sel, Q_BLOCK), 0)
        cur = (t0s[k] + lax.broadcasted_iota(jnp.int32, (1, Q_BLOCK), 1)) // SEL_BLOCK
        forced = (jb == 0) | (jb == cur) | (jb == cur - 1)
        imp = jnp.where(forced, imp + FORCE_BONUS, imp)
        imp = jnp.where(jb <= cur, imp, -1.0)
        bias = jnp.full((n_sel, Q_BLOCK), NEG_INF, F32)
        for _ in range(n_pick):
            m = jnp.max(imp, axis=0, keepdims=True)
            first = jnp.min(jnp.where(imp == m, jb, n_sel), axis=0, keepdims=True)
            hit = jb == first
            bias = jnp.where(hit, 0.0, bias)
            imp = jnp.where(hit, -jnp.inf, imp)
        per_unit = SEL_UNIT // SEL_BLOCK
        n_units = n_sel // per_unit
        picked = jnp.where(bias == 0.0, 1.0, 0.0).astype(BF16)
        in_unit = (lax.broadcasted_iota(jnp.int32, (n_units, n_sel), 1) // per_unit
                   == lax.broadcasted_iota(jnp.int32, (n_units, n_sel), 0))
        hits = _dot(jnp.where(in_unit, 1.0, 0.0).astype(BF16), picked)
        live = jnp.where(jnp.max(hits, axis=1, keepdims=True) > 0.5, 1.0, 0.0)
        live = jnp.broadcast_to(live, (n_units, LANES))
        earlier = (lax.broadcasted_iota(jnp.int32, (n_units, n_units), 1)
                   < lax.broadcasted_iota(jnp.int32, (n_units, n_units), 0))
        before = _dot(jnp.where(earlier, 1.0, 0.0).astype(BF16), live.astype(BF16))
        slot = lax.broadcasted_iota(jnp.int32, (n_units, LANES), 1).astype(F32)
        in_slot = jnp.where(before == slot, live, 0.0).astype(BF16)
        r8 = lax.broadcasted_iota(jnp.int32, (SUBLANES, n_units), 0)
        c8 = lax.broadcasted_iota(jnp.int32, (SUBLANES, n_units), 1)
        pick_rows = jnp.where(r8 == 0, c8, jnp.where(r8 == 1, 1, 0)).astype(F32).astype(BF16)
        ids = _dot(pick_rows, in_slot)
        n_live = (before[n_units - 1:n_units, 0:1] + live[n_units - 1:n_units, 0:1])[0, 0]
        bias = jnp.concatenate([bias] * GQA_GROUP, axis=1).astype(BF16)
        qa_scr[k, h, AUG:, :] = jnp.concatenate([bias, jnp.zeros((AUG - n_sel, QW), BF16)], axis=0)
        return ids, n_live

    def window(k, h):
        WK = WINDOW + Q_BLOCK
        w0 = pl.multiple_of(jnp.maximum(t0s[k] - WINDOW, 0), Q_BLOCK)
        sw = _dot(kw_ref[h, pl.ds(w0, WK), :], qa[k, h])
        dist = tqs[k] - (w0 + lax.broadcasted_iota(jnp.int32, (WK, QW), 0))
        in_win = (dist | (WINDOW - 1 - dist)) >= 0
        p_w = _softmax_cols(sw, in_win)
        return _dot(vwt_ref[h, :, pl.ds(w0, WK)], p_w.astype(BF16))

    lists = {c: select(*c, cmp_out[c][1]) for c in chains}
    o_win = {c: window(*c) for c in chains}

    def gathered(k, h, ids, n_slots):
        units = [ids[0, s].astype(jnp.int32) for s in range(n_slots)]
        starts = [pl.multiple_of(u * SEL_UNIT, SEL_UNIT) for u in units]
        keys = jnp.concatenate([ks_ref[h, pl.ds(st, SEL_UNIT), :] for st in starts], axis=0)
        s = _dot(keys, qa_scr[k, h])
        krow = lax.broadcasted_iota(jnp.int32, (SEL_UNIT, QW), 0)
        seen = jnp.concatenate(
            [st + krow <= jnp.where(ids[1, si] > 0.5, tqs[k], -1) for si, st in enumerate(starts)], axis=0)
        s = jnp.where(seen, s, NEG_INF)
        p = jnp.exp(s - jnp.max(s, axis=0, keepdims=True))
        vals = jnp.concatenate([vst_ref[h, :, pl.ds(st, SEL_UNIT)] for st in starts], axis=1)
        return _dot(vals, p.astype(BF16)) / jnp.maximum(jnp.sum(p, axis=0, keepdims=True), 1e-20)

    KC = SEL_KEY_CHUNK

    def swept(k, h):
        last = (t0s[k] + Q_BLOCK + KC - 1) // KC - 1

        def scores_to(ref, j):
            k0 = pl.multiple_of(j * KC, KC)
            ref[...] = _dot(ks_ref[h, pl.ds(k0, KC), :], qa_scr[k, h])

        def accumulate(carry, j, s, p_of):
            m, l, acc = carry
            k0 = pl.multiple_of(j * KC, KC)
            m_new = jnp.maximum(m, jnp.max(s, axis=0, keepdims=True))
            alpha = jnp.exp(m - m_new)
            p = p_of(jnp.exp(s - m_new))
            l = alpha * l + jnp.sum(p, axis=0, keepdims=True)
            acc = alpha * acc + _dot(vst_ref[h, :, pl.ds(k0, KC)], p.astype(BF16))
            return m_new, l, acc

        keep = lambda p: p

        def two_chunks(i, state):
            j = 2 * i
            scores_to(sb_scr, j + 1)
            state = accumulate(state, j, sa_scr[...], keep)
            scores_to(sa_scr, j + 2)
            return accumulate(state, j + 1, sb_scr[...], keep)

        def odd_chunk(state):
            state = accumulate(state, last - 1, sa_scr[...], keep)
            scores_to(sa_scr, last)
            return state

        scores_to(sa_scr, 0)
        init = (jnp.full((1, QW), NEG_INF, F32), jnp.zeros((1, QW), F32), jnp.zeros((HEAD_DIM, QW), F32))
        state = lax.fori_loop(0, last // 2, two_chunks, init)
        state = lax.cond(last % 2 == 1, odd_chunk, lambda st: st, state)
        visible = (last * KC + lax.broadcasted_iota(jnp.int32, (KC, QW), 0)) <= tqs[k]
        _, l_s, acc_s = accumulate(state, last, jnp.where(visible, sa_scr[...], NEG_INF),
                                   lambda p: jnp.where(visible, p, 0.0))
        return acc_s / jnp.maximum(l_s, 1e-20)

    few, some = SEL_SLOTS
    for k in range(ATT_TILES):
        outs = []
        for h in range(N_KV_HEADS):
            ids, n_live = lists[k, h]
            o_s = lax.cond(n_live <= few, lambda _: gathered(k, h, ids, few),
                           lambda _: lax.cond(n_live <= some, lambda _: gathered(k, h, ids, some),
                                              lambda _: swept(k, h), 0), 0)
            gk = lambda br: g_ref[h, br:br + 1, cols(k)]
            o = gk(0) * cmp_out[k, h][0] + gk(1) * o_s + gk(2) * o_win[k, h]
            for g in range(0, GQA_GROUP, 2):
                sq = jnp.concatenate([o[:, g * Q_BLOCK:(g + 1) * Q_BLOCK], o[:, (g + 1) * Q_BLOCK:(g + 2) * Q_BLOCK]],
                                     axis=0)
                outs.append(sq.T)
        o_ref[k * Q_BLOCK:(k + 1) * Q_BLOCK, :] = jnp.concatenate(outs, axis=1).astype(BF16)


def _attend(qt, ks, vst, kwn, vwt, gt, kvc_c):
    B, _, S, _ = ks.shape
    nqb = S // Q_BLOCK
    QW = GQA_GROUP * Q_BLOCK
    n_sel = S // SEL_BLOCK
    n_pick = min(SEL_TOPK, n_sel)
    nr = S // CMP_STRIDE
    n_cmp = (S - CMP_BLOCK) // CMP_STRIDE + 1
    assert n_sel % 16 == 0 and n_sel <= AUG

    cpos = np.arange(nr) * CMP_STRIDE + CMP_BLOCK - 1
    digits = np.zeros((nr, AUG - HEAD_DIM), np.float32)
    digits[:, 0] = cpos // POS_BASE
    digits[:, 1] = cpos % POS_BASE
    kc = kvc_c[:, :N_KV_HEADS]
    kc = jnp.concatenate([kc, jnp.broadcast_to(jnp.asarray(digits, BF16), kc.shape[:2] + digits.shape)], axis=-1)
    vct = kvc_c[:, N_KV_HEADS:].transpose(0, 1, 3, 2)
    slopes = 2.0 ** (-8.0 * np.arange(1, N_HEADS + 1) / N_HEADS)
    slope_t = jnp.asarray(np.repeat(slopes.reshape(N_KV_HEADS, GQA_GROUP), Q_BLOCK, axis=1)
                          .reshape(N_KV_HEADS, 1, QW), F32)
    cmp_start = np.arange(n_cmp) * CMP_STRIDE
    sel_start = np.arange(n_sel) * SEL_BLOCK
    ov = ((cmp_start[:, None] <= sel_start[None, :] + SEL_BLOCK - 1)
          & (cmp_start[:, None] + CMP_BLOCK - 1 >= sel_start[None, :])).astype(np.float32)
    ovt = np.zeros((n_sel, nr), np.float32)
    ovt[:, :n_cmp] = ov.T
    ovt = jnp.asarray(ovt, BF16)

    H = N_KV_HEADS
    per_bh = lambda r, c: pl.BlockSpec((None, H, r, c), lambda b, i: (b, 0, 0, 0))
    per_q = lambda r: pl.BlockSpec((None, H, r, ATT_TILES * QW), lambda b, i: (b, 0, 0, i))
    return pl.pallas_call(
        functools.partial(_attend_kernel, n_sel=n_sel, n_pick=n_pick, n_cmp_rows=nr),
        grid=(B, nqb // ATT_TILES),
        in_specs=[per_q(HEAD_DIM), per_bh(nr, AUG), per_bh(HEAD_DIM, nr),
                  per_bh(S, 2 * AUG), per_bh(HEAD_DIM, S), per_bh(S, AUG), per_bh(HEAD_DIM, S),
                  per_q(3),
                  pl.BlockSpec((H, 1, QW), lambda b, i: (0, 0, 0)),
                  pl.BlockSpec((n_sel, nr), lambda b, i: (0, 0))],
        out_specs=pl.BlockSpec((None, ATT_TILES * Q_BLOCK, NSA_WIDTH), lambda b, i: (b, i, 0)),
        out_shape=jax.ShapeDtypeStruct((B, S, NSA_WIDTH), BF16),
        scratch_shapes=[pltpu.VMEM((ATT_TILES, H, 2 * AUG, QW), BF16), pltpu.VMEM((SEL_KEY_CHUNK, QW), F32),
                        pltpu.VMEM((SEL_KEY_CHUNK, QW), F32)],
        compiler_params=_params(("arbitrary", "arbitrary")),
        name="attend",
    )(qt, kc, vct, ks, vst, kwn, vwt, gt, slope_t, ovt)


def _merge_kernel(x_ref, mod_ref, ys_ref, on_ref, gm_ref, wglu_ref, bglu_ref, wa_ref, wb_ref, wo_ref,
                  gain_ref, wrh_ref, wrl_ref, x1_ref, h2_ref, sc_ref):
    D = D_MODEL
    z = jax.nn.gelu(_from_chunk_major(lambda groups, lanes: ys_ref[:, groups, lanes], x_ref.shape[0]))
    glu = z * jax.nn.sigmoid(_dot(z.astype(BF16), wglu_ref[...]) + bglu_ref[...])
    ya = _dot(glu.astype(BF16), wa_ref[...])
    yb = _dot(on_ref[...], wb_ref[...])
    merged = gm_ref[:, :D].astype(F32) * ya + gm_ref[:, D:].astype(F32) * yb
    x1 = x_ref[...] + mod_ref[2:3, :] * _dot(merged.astype(BF16), wo_ref[...])
    x1_ref[...] = x1
    y = x1 * lax.rsqrt(jnp.mean(x1 * x1, axis=-1, keepdims=True) + RMS_EPS)
    h2 = (y * gain_ref[...]) * (1.0 + mod_ref[4:5, :]) + mod_ref[3:4, :]
    h2_ref[...] = h2.astype(BF16)
    hh, hl = _split(h2)
    nt = (((1,), (1,)), ((), ()))
    dg = lambda a, b: lax.dot_general(a, b, nt, preferred_element_type=F32)
    logits = dg(wrh_ref[...], hh) + (dg(wrh_ref[...], hl) + dg(wrl_ref[...], hh))
    sc_ref[...] = jax.nn.sigmoid(logits)


def _merge(x, mod, ys5, o_nsa, gm, w_glu, b_glu, w_a, w_b, w_out, gain_f, w_router):
    B, S, D = x.shape
    tm = min(ROW_TILE, S)
    wrh, wrl = _split(w_router.T)
    ws = [w_glu.astype(BF16), b_glu.reshape(1, -1), w_a.astype(BF16), w_b.astype(BF16), w_out.astype(BF16),
          gain_f, wrh, wrl]
    row = lambda w: pl.BlockSpec((None, tm, w), lambda b, i: (b, i, 0))
    full = lambda a: pl.BlockSpec(a.shape, lambda b, i: (0,) * a.ndim)
    return pl.pallas_call(
        _merge_kernel,
        grid=(B, S // tm),
        in_specs=[row(D), pl.BlockSpec((None, N_MOD, D), lambda b, i: (b, 0, 0)),
                  _s5_chunk_spec(tm), row(NSA_WIDTH), row(2 * D)] + [full(a) for a in ws],
        out_specs=[row(D), row(D), pl.BlockSpec((None, N_EXPERTS, tm), lambda b, i: (b, 0, i))],
        out_shape=[jax.ShapeDtypeStruct((B, S, D), F32), jax.ShapeDtypeStruct((B, S, D), BF16),
                   jax.ShapeDtypeStruct((B, N_EXPERTS, S), F32)],
        compiler_params=_params(("arbitrary", "arbitrary")),
        name="merge",
    )(x, mod, ys5, o_nsa, gm, *ws)


def _first_argmax_rows(v, idx, n):
    m = jnp.max(v, axis=0, keepdims=True)
    first = jnp.min(jnp.where(v == m, idx, n), axis=0, keepdims=True)
    return idx == first, m


def _route_kernel(sc_ref, bias_ref, tri_ref, rank_ref, w_ref, cnt_ref):
    E, NG = N_EXPERTS, N_EXPERT_GROUPS
    GS = E // NG
    sc = sc_ref[...]
    TM = sc.shape[1]
    sel = sc + bias_ref[...]
    i8 = lax.broadcasted_iota(jnp.int32, (GS, TM), 0)
    gscore = []
    for g in range(NG):
        blk = sel[g * GS:(g + 1) * GS, :]
        hit, m1 = _first_argmax_rows(blk, i8, GS)
        m2 = jnp.max(jnp.where(hit, -jnp.inf, blk), axis=0, keepdims=True)
        gscore.append(m1 + m2)
    gscore = jnp.concatenate(gscore, axis=0)
    ig = lax.broadcasted_iota(jnp.int32, (NG, TM), 0)
    gmask = jnp.zeros((NG, TM), F32)
    for _ in range(TOPK_EXPERT_GROUPS):
        hit, _m = _first_argmax_rows(gscore, ig, NG)
        gmask = jnp.where(hit, 1.0, gmask)
        gscore = jnp.where(hit, -jnp.inf, gscore)
    emask = jnp.concatenate([jnp.broadcast_to(gmask[g:g + 1, :], (GS, TM)) for g in range(NG)], axis=0)
    cand = jnp.where(emask > 0.5, sel, NEG_INF)
    ie = lax.broadcasted_iota(jnp.int32, (E, TM), 0)
    chosen = jnp.zeros((E, TM), F32)
    for _ in range(TOP_K):
        hit, _m = _first_argmax_rows(cand, ie, E)
        chosen = jnp.where(hit, 1.0, chosen)
        cand = jnp.where(hit, -jnp.inf, cand)
    w = chosen * sc
    w = w / jnp.sum(w, axis=0, keepdims=True) * ROUTED_SCALE
    sub = rank_ref.shape[-1]
    for q in range(TM // sub):
        cols = slice(q * sub, (q + 1) * sub)
        cb = chosen[:, cols].astype(BF16)
        prefix = _dot(cb, tri_ref[...])
        rank_ref[q] = jnp.where(chosen[:, cols] > 0.5, prefix, RANK_NONE)
        w_ref[q] = w[:, cols]
        cnt_ref[q] = _dot(cb, jnp.ones((sub, LANES), BF16))


def _route(scores_t, router_bias):
    B, E, S = scores_t.shape
    TM = min(ROUTE_TILE, S)
    per_step = min(MOE_OUTER, S) // TM
    nj = S // (TM * per_step)
    ns = B * nj * per_step
    tri = jnp.asarray(np.triu(np.ones((TM, TM), np.float32), k=1), BF16)
    tile = lambda w: pl.BlockSpec((per_step, E, w), lambda b, j: (b * nj + j, 0, 0))
    return pl.pallas_call(
        _route_kernel,
        grid=(B, nj),
        in_specs=[pl.BlockSpec((None, E, TM * per_step), lambda b, j: (b, 0, j)),
                  pl.BlockSpec((E, 1), lambda b, j: (0, 0)),
                  pl.BlockSpec((TM, TM), lambda b, j: (0, 0))],
        out_specs=[tile(TM), tile(TM), tile(LANES)],
        out_shape=[jax.ShapeDtypeStruct((ns, E, TM), F32), jax.ShapeDtypeStruct((ns, E, TM), F32),
                   jax.ShapeDtypeStruct((ns, E, LANES), F32)],
        compiler_params=_params(("arbitrary", "arbitrary")),
        name="route",
    )(scores_t, router_bias.reshape(E, 1), tri)


def _moe_kernel(cnt_ref, h_ref, x1_ref, mod_ref, rank_ref, w_ref, wg_ref, wu_ref, wd_ref,
                sg_ref, su_ref, sd_ref, o_ref, xc, yc, pc, wc, *, n_sub, tm, n_exp):
    to = pl.program_id(0)
    eb = pl.program_id(1)
    gate = mod_ref[5:6, :]

    @pl.when(eb == 0)
    def _shared():
        for s in range(n_sub):
            rows = pl.ds(s * tm, tm)
            hs = h_ref[rows, :]
            hid = jax.nn.silu(_dot(hs, sg_ref[...])) * _dot(hs, su_ref[...])
            o_ref[rows, :] = x1_ref[rows, :] + gate * _dot(hid.astype(BF16), sd_ref[...])
            xc[s] = jnp.zeros(xc.shape[1:], BF16)
            yc[s] = jnp.zeros(yc.shape[1:], BF16)
            wc[s] = jnp.zeros(wc.shape[1:], F32)

    PIECE, WIN = MOE_PIECE, MOE_WIN
    tn = (((0,), (0,)), ((), ()))
    experts = [eb * n_exp + j for j in range(n_exp)]
    cnt = [[cnt_ref[(to * n_sub + s) * N_EXPERTS + e] for e in experts] for s in range(n_sub)]
    off = []
    for s in range(n_sub):
        o = [jnp.int32(0)]
        for j in range(n_exp):
            o.append(o[-1] + ((cnt[s][j] + 15) // 16) * 16)
        off.append(o)
    n_pieces = [(off[s][n_exp] + PIECE - 1) // PIECE for s in range(n_sub)]
    fits = n_pieces[0] < MOE_ROWS // PIECE
    for s in range(1, n_sub):
        fits = jnp.logical_and(fits, n_pieces[s] < MOE_ROWS // PIECE)

    def one_hot_rows(s, base, rows_n, js):
        slot = (lax.broadcasted_iota(jnp.int32, (rows_n, tm), 0) + base).astype(F32)
        pick = jnp.zeros((rows_n, tm), F32)
        wacc = jnp.zeros((rows_n, tm), F32)
        for j in js:
            hit = (slot - off[s][j].astype(F32)) == rank_ref[s, pl.ds(experts[j], 1), :]
            pick = jnp.where(hit, 1.0, pick)
            wacc = jnp.where(hit, w_ref[s, pl.ds(experts[j], 1), :], wacc)
        return pick.astype(BF16), jnp.sum(wacc, axis=1, keepdims=True)

    def mlp(xg, j):
        hid = jax.nn.silu(_dot(xg, wg_ref[j])) * _dot(xg, wu_ref[j])
        return _dot(hid.astype(BF16), wd_ref[j])

    def gather(s, start, n):
        picks = []
        for c in range(0, n, PIECE // 2):
            m = min(PIECE // 2, n - c)
            pick, wrow = one_hot_rows(s, start + c, m, range(n_exp))
            wc[s, pl.ds(start + c, m), :] = wrow
            picks.append(pick)
        if n % PIECE:
            picks.append(jnp.zeros((PIECE - n % PIECE, tm), BF16))
        pick = jnp.concatenate(picks, axis=0)
        xc[s, pl.ds(start, n), :] = _dot(pick[:n], h_ref[pl.ds(s * tm, tm), :]).astype(BF16)
        for q in range(0, pick.shape[0], PIECE):
            pc[s, :, pl.ds(start + q, PIECE)] = pick[q:q + PIECE].T

    def window(j, p, first):
        starts = [pl.multiple_of(jnp.minimum(off[s][j] + p * WIN, (n_pieces[s] + 1) * PIECE - WIN), 16)
                  for s in range(n_sub)]
        xg = jnp.concatenate([xc[s, pl.ds(starts[s], WIN), :] for s in range(n_sub)], axis=0)
        out = mlp(xg, j)
        for s in range(n_sub):
            r = pl.ds(starts[s], WIN)
            new = (out[s * WIN:(s + 1) * WIN] * wc[s, r, :] * gate).astype(BF16)
            if not first:
                rank = starts[s] - off[s][j] + lax.broadcasted_iota(jnp.int32, (WIN, 1), 0)
                new = jnp.where(rank < cnt[s][j], new, yc[s, r, :])
            yc[s, r, :] = new

    def scatter(s, start, n):
        r = pl.ds(start, n)
        o_ref[pl.ds(s * tm, tm), :] += _dot(pc[s, :, r], yc[s, r, :])

    @pl.when(fits)
    def _packed():
        for s in range(n_sub):
            gather(s, 0, MOE_BASE_ROWS)
        for s in range(n_sub):
            @pl.when(off[s][n_exp] > MOE_BASE_ROWS)
            def _second(s=s):
                gather(s, PIECE, PIECE)

            @pl.when(n_pieces[s] > 2)
            def _third(s=s):
                gather(s, 2 * PIECE, PIECE)
        for j in range(n_exp):
            window(j, 0, True)
        for j in range(n_exp):
            most = cnt[0][j]
            for s in range(1, n_sub):
                most = jnp.maximum(most, cnt[s][j])

            @pl.when(most > WIN)
            def _more(j=j, most=most):
                lax.fori_loop(1, (most + WIN - 1) // WIN, lambda p, c: (window(j, p, False), c)[1], 0)
        for s in range(n_sub):
            scatter(s, 0, 2 * PIECE)
        for s in range(n_sub):
            @pl.when(n_pieces[s] > 2)
            def _third_out(s=s):
                scatter(s, 2 * PIECE, PIECE)

    @pl.when(jnp.logical_not(fits))
    def _unpacked():
        for j in range(n_exp):
            for s in range(n_sub):
                rows = pl.ds(s * tm, tm)

                def block(bi, carry, j=j, s=s, rows=rows):
                    slot0 = off[s][j] + bi * PIECE
                    pick, wrow = one_hot_rows(s, slot0, PIECE, [j])
                    out = mlp(_dot(pick, h_ref[rows, :]).astype(BF16), j)
                    ow = (out * wrow * gate).astype(BF16)
                    o_ref[rows, :] += lax.dot_general(pick, ow, tn, preferred_element_type=F32)
                    return carry

                lax.fori_loop(0, (cnt[s][j] + PIECE - 1) // PIECE, block, 0)


def _moe(h2, x1, mod, rank_t, w_t, counts, w_gate, w_up, w_down, ws_gate, ws_up, ws_down, seq):
    T, D = h2.shape
    ns, E, tm = rank_t.shape
    tmo = min(MOE_OUTER, seq)
    n_sub = tmo // tm
    n_exp = MOE_EXPERTS_PER_STEP
    per_b = seq // tmo
    ws = [w_gate.astype(BF16), w_up.astype(BF16), w_down.astype(BF16)]
    sh = [ws_gate.astype(BF16), ws_up.astype(BF16), ws_down.astype(BF16)]
    tok = lambda: pl.BlockSpec((tmo, D), lambda t, e, c: (t, 0))
    sub = lambda: pl.BlockSpec((n_sub, E, tm), lambda t, e, c: (t, 0, 0))
    exp = lambda a: pl.BlockSpec((n_exp,) + a.shape[1:], lambda t, e, c: (e, 0, 0))
    full = lambda a: pl.BlockSpec(a.shape, lambda t, e, c: (0, 0))
    grid_spec = pltpu.PrefetchScalarGridSpec(
        num_scalar_prefetch=1,
        grid=(T // tmo, E // n_exp),
        in_specs=[tok(), tok(), pl.BlockSpec((None, N_MOD, D), lambda t, e, c: (t // per_b, 0, 0)),
                  sub(), sub(), exp(ws[0]), exp(ws[1]), exp(ws[2]), full(sh[0]), full(sh[1]), full(sh[2])],
        out_specs=tok(),
        scratch_shapes=[pltpu.VMEM((n_sub, MOE_ROWS, D), BF16), pltpu.VMEM((n_sub, MOE_ROWS, D), BF16),
                        pltpu.VMEM((n_sub, tm, MOE_ROWS), BF16), pltpu.VMEM((n_sub, MOE_ROWS, 1), F32)],
    )
    return pl.pallas_call(
        functools.partial(_moe_kernel, n_sub=n_sub, tm=tm, n_exp=n_exp),
        grid_spec=grid_spec,
        out_shape=jax.ShapeDtypeStruct((T, D), F32),
        compiler_params=_params(("arbitrary", "arbitrary")),
        name="moe",
    )(counts, h2, x1, mod, rank_t, w_t, *ws, *sh)


def kernel(x, c, w_ada, b_ada, norm_mix_gain, norm_ffn_gain, w_in, s5_lambda_re, s5_lambda_im, s5_log_dt, s5_b_re, s5_b_im, s5_c_re, s5_c_im, s5_d, s5_w_glu, s5_b_glu, q_norm_gain, k_norm_gain, cmp_pe, cmp_w1, cmp_b1, cmp_w2, cmp_b2, w_branch_a, w_branch_b, w_out, w_router, router_bias, w_gate, w_up, w_down, ws_gate, ws_up, ws_down):
    B, S, D = x.shape
    for l in range(w_ada.shape[0]):
        mod = _ada(c, w_ada[l], b_ada[l]).reshape(B, N_MOD, D)
        u, kvc, qt, ks, vst, kwn, vwt, gt, gm = _inproj(x, mod, norm_mix_gain[l:l + 1], w_in[l],
                                                        q_norm_gain[l], k_norm_gain[l])
        tables = _s5_tables(s5_lambda_re[l], s5_lambda_im[l], s5_log_dt[l], s5_b_re[l], s5_b_im[l],
                            s5_c_re[l], s5_c_im[l], s5_d[l])
        ys5 = _s5(u, tables)
        kvc_c = _compress(kvc, cmp_pe[l], cmp_w1[l], cmp_b1[l], cmp_w2[l], cmp_b2[l], k_norm_gain[l, 0])
        o_nsa = _attend(qt, ks, vst, kwn, vwt, gt, kvc_c)
        x1, h2, scores_t = _merge(x, mod, ys5, o_nsa, gm, s5_w_glu[l], s5_b_glu[l], w_branch_a[l],
                                  w_branch_b[l], w_out[l], norm_ffn_gain[l:l + 1], w_router[l])
        rank_t, w_t, cnt = _route(scores_t, router_bias[l])
        counts = cnt[:, :, 0].astype(jnp.int32).reshape(-1)
        x = _moe(h2.reshape(B * S, D), x1.reshape(B * S, D), mod, rank_t, w_t, counts,
                 w_gate[l], w_up[l], w_down[l], ws_gate[l], ws_up[l], ws_down[l], S).reshape(B, S, D)
    return x
```

```python
import functools

import numpy as np
import jax
import jax.numpy as jnp
from jax import lax
from jax.experimental import pallas as pl
from jax.experimental.pallas import tpu as pltpu

F32 = jnp.float32
BF16 = jnp.bfloat16

D_MODEL = 1024
S5_WIDTH = 512
S5_GROUP = 16
S5_GROUPS = S5_WIDTH // S5_GROUP
S5_STATE = 64
N_HEADS = 8
N_KV_HEADS = 2
GQA_GROUP = N_HEADS // N_KV_HEADS
HEAD_DIM = 64
NSA_WIDTH = N_HEADS * HEAD_DIM
KV_WIDTH = 2 * N_KV_HEADS * HEAD_DIM
CMP_BLOCK = 32
CMP_STRIDE = 16
CMP_HIDDEN = 256
SEL_BLOCK = 64
SEL_TOPK = 8
WINDOW = 256
Q_BLOCK = 128
FORCE_BONUS = 1e3
N_EXPERTS = 64
TOP_K = 8
N_EXPERT_GROUPS = 8
TOPK_EXPERT_GROUPS = 4
ROUTED_SCALE = 2.5
RMS_EPS = 1e-6
NEG_INF = -1e30
N_MOD = 6

LANES = 128
SUBLANES = 8
S5_CHUNK = 16
S5_CW = S5_CHUNK * S5_GROUP
ROW_TILE = 512
ATT_TILES = 2
SEL_KEY_CHUNK = 512
SEL_UNIT = 128
SEL_SLOTS = (5, 6, 7, 8)
POS_BASE = 64
POS_ROWS = 16
AUG = 128
ROUTE_TILE = 256
MOE_OUTER = 1024
MOE_EXPERTS_PER_STEP = 4
MOE_PIECE = 128
MOE_BASE_ROWS = 208
MOE_WIN = 64
MOE_ROWS = 512
RANK_NONE = -float(1 << 20)
VMEM_LIMIT = 56 * 1024 * 1024


def _dot(a, b):
    return jnp.dot(a, b, preferred_element_type=F32)


def _split(a):
    hi = a.astype(BF16)
    lo = (a - hi.astype(F32)).astype(BF16)
    return hi, lo


def _dot3(a, bh, bl):
    ah, al = _split(a)
    return _dot(ah, bh) + (_dot(al, bh) + _dot(ah, bl))


def _segment_transpose(x):
    assert x.shape[-2:] == (SUBLANES, LANES) and LANES // S5_GROUP == SUBLANES
    nd = x.ndim
    i = lax.broadcasted_iota(jnp.int32, x.shape, nd - 2)
    seg = lax.broadcasted_iota(jnp.int32, x.shape, nd - 1) // S5_GROUP
    out = x
    for d in range(1, SUBLANES):
        r = pltpu.roll(pltpu.roll(x, SUBLANES - d, axis=nd - 2), S5_GROUP * d, axis=nd - 1)
        out = jnp.where(seg == ((i + d) & (SUBLANES - 1)), r, out)
    return out


def _to_chunk_major(u, put):
    rows = u.shape[0]
    u3 = u.reshape(rows // SUBLANES, SUBLANES, S5_WIDTH)
    halves = S5_CHUNK // SUBLANES
    for jb in range(S5_WIDTH // LANES):
        t = _segment_transpose(u3[:, :, LANES * jb:LANES * (jb + 1)])
        t = t.reshape(rows // S5_CHUNK, halves, SUBLANES, LANES)
        for hf in range(halves):
            put(slice(SUBLANES * jb, SUBLANES * (jb + 1)), slice(LANES * hf, LANES * (hf + 1)), t[:, hf])


def _from_chunk_major(get, rows):
    halves = S5_CHUNK // SUBLANES
    cols = []
    for jb in range(S5_WIDTH // LANES):
        parts = [_segment_transpose(get(slice(SUBLANES * jb, SUBLANES * (jb + 1)), slice(LANES * hf, LANES * (hf + 1))))
                 for hf in range(halves)]
        cols.append(jnp.stack(parts, axis=1).reshape(rows, LANES))
    return jnp.concatenate(cols, axis=1)


def _params(sem):
    return pltpu.CompilerParams(dimension_semantics=sem, vmem_limit_bytes=VMEM_LIMIT)


def _ada_kernel(c_ref, w_ref, b_ref, o_ref):
    cs = jax.nn.silu(c_ref[...])
    wh, wl = _split(w_ref[...])
    o_ref[...] = _dot3(cs, wh, wl) + b_ref[...]


def _ada(c, w_ada, b_ada):
    B, D = c.shape
    return pl.pallas_call(
        _ada_kernel,
        grid=(N_MOD,),
        in_specs=[pl.BlockSpec((B, D), lambda j: (0, 0)),
                  pl.BlockSpec((D, D), lambda j: (0, j)),
                  pl.BlockSpec((1, D), lambda j: (0, j))],
        out_specs=pl.BlockSpec((B, D), lambda j: (0, j)),
        out_shape=jax.ShapeDtypeStruct((B, N_MOD * D), F32),
        compiler_params=_params(("arbitrary",)),
        name="ada",
    )(c, w_ada, b_ada.reshape(1, N_MOD * D))


def _head_norm(v, bd, gain):
    sq = v * v
    sh, sl = _split(sq)
    ms = _dot(sh, bd) + _dot(sl, bd)
    return v * lax.rsqrt(ms + RMS_EPS) * gain


def _inproj_kernel(x_ref, mod_ref, gain_ref, wm_ref, wt_ref, wgn_ref, wgm_ref, bd_ref, qg_ref, kg_ref,
                   u_ref, kvc_ref, qt_ref, ksa_ref, vst_ref, kwa_ref, vwt_ref, gt_ref, gm_ref):
    x = x_ref[...]
    shift = mod_ref[0:1, :]
    scale = mod_ref[1:2, :]
    y = x * lax.rsqrt(jnp.mean(x * x, axis=-1, keepdims=True) + RMS_EPS)
    h = (y * gain_ref[...]) * (1.0 + scale) + shift
    hb = h.astype(BF16)
    main = _dot(hb, wm_ref[...])
    tm = x.shape[0]

    def put(groups, lanes, block):
        u_ref[:, groups, lanes] = block

    _to_chunk_major(main[:, :S5_WIDTH], put)
    o = S5_WIDTH
    for part in range(KV_WIDTH // LANES):
        kvc_ref[part] = main[:, o + part * LANES:o + (part + 1) * LANES]
    o += KV_WIDTH
    kw = N_KV_HEADS * HEAD_DIM
    bd = bd_ref[...]
    ks = _head_norm(main[:, o:o + kw], bd[:kw, :kw], kg_ref[1:2, :]).astype(BF16)
    kwn = _head_norm(main[:, o + kw:o + 2 * kw], bd[:kw, :kw], kg_ref[2:3, :]).astype(BF16)
    pos = pl.program_id(1) * tm + lax.broadcasted_iota(jnp.int32, (tm, 1), 0)
    lane = lax.broadcasted_iota(jnp.int32, (tm, AUG), 1)
    digits = jnp.where(lane == 0, pos // POS_BASE, jnp.where(lane == 1, pos % POS_BASE, 0))
    digits = digits[:, :AUG - HEAD_DIM].astype(F32).astype(BF16)
    onehot = jnp.where(lane == pos // SEL_BLOCK, 1.0, 0.0).astype(BF16)
    for hh in range(N_KV_HEADS):
        ksa_ref[hh, :, 0:HEAD_DIM] = ks[:, hh * HEAD_DIM:(hh + 1) * HEAD_DIM]
        ksa_ref[hh, :, HEAD_DIM:AUG] = digits
        ksa_ref[hh, :, AUG:] = onehot
        kwa_ref[hh, :, 0:HEAD_DIM] = kwn[:, hh * HEAD_DIM:(hh + 1) * HEAD_DIM]
        kwa_ref[hh, :, HEAD_DIM:] = digits

    nt = (((1,), (1,)), ((), ()))
    tt = lax.dot_general(wt_ref[...], hb, nt, preferred_element_type=F32)
    qt = tt[:NSA_WIDTH]
    sq = qt * qt
    sh, sl = _split(sq)
    qn = (qt * lax.rsqrt(_dot(bd, sh) + _dot(bd, sl) + RMS_EPS) * qg_ref[...] * (HEAD_DIM ** -0.5)).astype(BF16)
    QW = GQA_GROUP * Q_BLOCK
    for qb in range(tm // Q_BLOCK):
        for hd in range(N_HEADS):
            hh, g = divmod(hd, GQA_GROUP)
            qt_ref[hh, :, qb * QW + g * Q_BLOCK:qb * QW + (g + 1) * Q_BLOCK] = (
                qn[hd * HEAD_DIM:(hd + 1) * HEAD_DIM, qb * Q_BLOCK:(qb + 1) * Q_BLOCK])
    vst_ref[...] = tt[NSA_WIDTH:NSA_WIDTH + kw].reshape(N_KV_HEADS, HEAD_DIM, tm).astype(BF16)
    vwt_ref[...] = tt[NSA_WIDTH + kw:].reshape(N_KV_HEADS, HEAD_DIM, tm).astype(BF16)
    gn = jax.nn.sigmoid(lax.dot_general(wgn_ref[...], hb, nt, preferred_element_type=F32))
    for qb in range(tm // Q_BLOCK):
        for br in range(3):
            for hd in range(N_HEADS):
                hh, g = divmod(hd, GQA_GROUP)
                r = br * N_HEADS + hd
                gt_ref[hh, br:br + 1, qb * QW + g * Q_BLOCK:qb * QW + (g + 1) * Q_BLOCK] = (
                    gn[r:r + 1, qb * Q_BLOCK:(qb + 1) * Q_BLOCK])
    gm_ref[...] = jax.nn.sigmoid(_dot(hb, wgm_ref[...])).astype(BF16)


def _s5_chunk_spec(tm):
    assert S5_CHUNK == 16 and S5_GROUP == 16, "the segment transposes assume 16 steps x 16 channels"
    return pl.BlockSpec((tm // S5_CHUNK, None, S5_GROUPS, S5_CW), lambda b, i: (i, b, 0, 0))


def _inproj(x, mod, gain, w_in, q_gain, k_gain):
    B, S, D = x.shape
    tm = min(ROW_TILE, S)
    assert S // SEL_BLOCK <= AUG and tm % Q_BLOCK == 0
    kw = N_KV_HEADS * HEAD_DIM
    cols = np.cumsum((0,) + (S5_WIDTH, NSA_WIDTH, KV_WIDTH, KV_WIDTH, KV_WIDTH, 3 * N_HEADS, 2 * D))
    c_u, c_q, c_kvc, c_kvs, c_kvw, c_gn, c_gm = cols[:7]
    sl = lambda a, n: w_in[:, a:a + n]
    wm = jnp.concatenate([sl(c_u, S5_WIDTH), sl(c_kvc, KV_WIDTH), sl(c_kvs, kw), sl(c_kvw, kw)], axis=1).astype(BF16)
    wt = jnp.concatenate([sl(c_q, NSA_WIDTH), sl(c_kvs + kw, kw), sl(c_kvw + kw, kw)], axis=1).T.astype(BF16)
    wgn = sl(c_gn, 3 * N_HEADS).T.astype(BF16)
    wgm = sl(c_gm, 2 * D).astype(BF16)
    seg = np.arange(NSA_WIDTH) // HEAD_DIM
    bd = jnp.asarray((seg[:, None] == seg[None, :]).astype(np.float32) / HEAD_DIM, BF16)
    qg = jnp.tile(q_gain, N_HEADS).reshape(NSA_WIDTH, 1)
    kg = jnp.tile(k_gain, (1, N_KV_HEADS))
    nq = tm // Q_BLOCK * GQA_GROUP * Q_BLOCK
    row = lambda w: pl.BlockSpec((None, tm, w), lambda b, i: (b, i, 0))
    full = lambda a: pl.BlockSpec(a.shape, lambda b, i: (0,) * a.ndim)
    rows4 = lambda w: pl.BlockSpec((None, N_KV_HEADS, tm, w), lambda b, i: (b, 0, i, 0))
    cols4 = lambda r, w: pl.BlockSpec((None, N_KV_HEADS, r, w), lambda b, i: (b, 0, 0, i))
    nqt = S // Q_BLOCK * GQA_GROUP * Q_BLOCK
    return pl.pallas_call(
        _inproj_kernel,
        grid=(B, S // tm),
        in_specs=[row(D), pl.BlockSpec((None, N_MOD, D), lambda b, i: (b, 0, 0)),
                  full(gain), full(wm), full(wt), full(wgn), full(wgm), full(bd), full(qg), full(kg)],
        out_specs=[_s5_chunk_spec(tm), pl.BlockSpec((None, KV_WIDTH // LANES, tm, LANES), lambda b, i: (b, 0, i, 0)),
                   cols4(HEAD_DIM, nq), rows4(2 * AUG), cols4(HEAD_DIM, tm),
                   rows4(AUG), cols4(HEAD_DIM, tm), cols4(3, nq), row(2 * D)],
        out_shape=[jax.ShapeDtypeStruct((S // S5_CHUNK, B, S5_GROUPS, S5_CW), F32),
                   jax.ShapeDtypeStruct((B, KV_WIDTH // LANES, S, LANES), F32),
                   jax.ShapeDtypeStruct((B, N_KV_HEADS, HEAD_DIM, nqt), BF16),
                   jax.ShapeDtypeStruct((B, N_KV_HEADS, S, 2 * AUG), BF16),
                   jax.ShapeDtypeStruct((B, N_KV_HEADS, HEAD_DIM, S), BF16),
                   jax.ShapeDtypeStruct((B, N_KV_HEADS, S, AUG), BF16),
                   jax.ShapeDtypeStruct((B, N_KV_HEADS, HEAD_DIM, S), BF16),
                   jax.ShapeDtypeStruct((B, N_KV_HEADS, 3, nqt), F32),
                   jax.ShapeDtypeStruct((B, S, 2 * D), BF16)],
        compiler_params=_params(("arbitrary", "arbitrary")),
        name="inproj",
    )(x, mod, gain, wm, wt, wgn, wgm, bd, qg, kg)


def _s5_tables(lam_re, lam_im, log_dt, b_re, b_im, c_re, c_im, d_skip):
    L, G = S5_CHUNK, S5_GROUPS
    lr, li = lam_re.astype(F32), lam_im.astype(F32)
    dt = jnp.exp(log_dt.astype(F32))[:, None]
    mag = jnp.exp(lr * dt)
    abar_re, abar_im = mag * jnp.cos(li * dt), mag * jnp.sin(li * dt)
    num_re, num_im = abar_re - 1.0, abar_im
    den = lr * lr + li * li
    coef_re = (num_re * lr + num_im * li) / den
    coef_im = (num_im * lr - num_re * li) / den
    br, bi = b_re.astype(F32), b_im.astype(F32)
    bbar_re = coef_re[..., None] * br - coef_im[..., None] * bi
    bbar_im = coef_re[..., None] * bi + coef_im[..., None] * br
    k = jnp.arange(L + 1, dtype=F32)[:, None]
    pmag = jnp.exp((lr * dt)[:, None, :] * k)
    ang = (li * dt)[:, None, :] * k
    pw = jnp.stack([jnp.tile(pmag * jnp.cos(ang), (1, 1, 2)), jnp.tile(pmag * jnp.sin(ang), (1, 1, 2))], axis=1)
    brt, bit = bbar_re.transpose(0, 2, 1), bbar_im.transpose(0, 2, 1)
    ba = jnp.concatenate([brt, bit], axis=2)
    bb = jnp.concatenate([-bit, brt], axis=2)
    cr, ci = c_re.astype(F32), c_im.astype(F32)
    ca = jnp.concatenate([cr, -ci], axis=2)
    cb = jnp.concatenate([-ci, -cr], axis=2)
    dv = jnp.tile(d_skip.astype(F32), (1, L)).reshape(G, 1, S5_CW)
    return pw, ba, bb, ca, cb, dv


def _s5_kernel(u_ref, pw_ref, ba_ref, bb_ref, ca_ref, cb_ref, dv_ref,
               y_ref, vr_scr, vi_scr, xr_scr, xi_scr, *, n_chunks, bsz):
    L, P, N = S5_CHUNK, S5_GROUP, S5_STATE
    nt = (((1,), (1,)), ((), ()))
    dot_nt = lambda a, b: lax.dot_general(a, b, nt, preferred_element_type=F32)
    mix = lambda a, b, k: a * pw_ref[0, k:k + 1, :] + b * pw_ref[1, k:k + 1, :]
    ba, bb, ca, cb = ba_ref[...], bb_ref[...], ca_ref[...], cb_ref[...]
    ws = jnp.concatenate([mix(ba, bb, L - 1 - s) for s in range(L)], axis=0)
    wot = jnp.concatenate([mix(ca, cb, k) for k in range(L + 1)], axis=0)
    wsh, wsl = _split(ws)
    wth, wtl = _split(wot)
    bah, bal = _split(ba)
    kern = dot_nt(bah, wth[:L * P]) + (dot_nt(bal, wth[:L * P]) + dot_nt(bah, wtl[:L * P]))
    lane = lax.broadcasted_iota(jnp.int32, kern.shape, 1)
    mt = jnp.concatenate([kern] + [jnp.where(lane >= P * s, pltpu.roll(kern, P * s, axis=1), 0.0)
                                   for s in range(1, L)], axis=0)
    mth, mtl = _split(mt)

    u = u_ref[...]
    uh, ul = _split(u)
    y = _dot(uh, mth) + (_dot(ul, mth) + _dot(uh, mtl))
    v = _dot(uh, wsh) + (_dot(ul, wsh) + _dot(uh, wsl))
    vr_scr[...] = v[:, :N]
    vi_scr[...] = v[:, N:]
    a_r = pw_ref[0, L:L + 1, :N]
    a_i = pw_ref[1, L:L + 1, :N]

    def step(c, x):
        xr, xi = x
        r = pl.ds(pl.multiple_of(c * bsz, bsz), bsz)
        xr_scr[r, :] = xr
        xi_scr[r, :] = xi
        return (a_r * xr - a_i * xi + vr_scr[r, :], a_r * xi + a_i * xr + vi_scr[r, :])

    zero = jnp.zeros((bsz, N), F32)
    lax.fori_loop(0, n_chunks, step, (zero, zero), unroll=8)
    xh, xl = _split(jnp.concatenate([xr_scr[...], xi_scr[...]], axis=1))
    y = y + (dot_nt(xh, wth[P:]) + (dot_nt(xl, wth[P:]) + dot_nt(xh, wtl[P:])))
    y_ref[...] = y + dv_ref[...] * u


def _s5(ug, tables):
    nc, B, G, cw = ug.shape
    N = S5_STATE
    grp = lambda a: pl.BlockSpec((None,) + a.shape[1:], lambda g: (g,) + (0,) * (a.ndim - 1))
    rows = pl.BlockSpec((nc * B, cw), lambda g: (0, g))
    y = pl.pallas_call(
        functools.partial(_s5_kernel, n_chunks=nc, bsz=B),
        grid=(G,),
        in_specs=[rows] + [grp(t) for t in tables],
        out_specs=rows,
        out_shape=jax.ShapeDtypeStruct((nc * B, G * cw), F32),
        scratch_shapes=[pltpu.VMEM((nc * B, N), F32)] * 4,
        compiler_params=_params(("arbitrary",)),
        name="s5",
    )(ug.reshape(nc * B, G * cw), *tables)
    return y.reshape(nc, B, G, cw)


def _compress_kernel(x_ref, pe_ref, w1_ref, b1_ref, w2_ref, b2_ref, kg_ref, o_ref, *, n_rows):
    half = CMP_STRIDE * HEAD_DIM
    for j in range(2 * N_KV_HEADS):
        kv, hd = divmod(j, N_KV_HEADS)
        x = jnp.concatenate(
            [x_ref[kv, pl.ds(t, n_rows, stride=CMP_STRIDE), :][:, hd * HEAD_DIM:(hd + 1) * HEAD_DIM]
             for t in range(CMP_STRIDE)], axis=1)
        a = _dot((x + pe_ref[kv, 0:1, :]).astype(BF16), w1_ref[kv, :half, :])
        b = _dot((x + pe_ref[kv, 1:2, :]).astype(BF16), w1_ref[kv, half:, :])
        hid = jax.nn.gelu(a + pltpu.roll(b, n_rows - 1, axis=0) + b1_ref[kv])
        out = _dot(hid.astype(BF16), w2_ref[kv]) + b2_ref[kv]
        if kv == 0:
            out = out * lax.rsqrt(jnp.mean(out * out, axis=-1, keepdims=True) + RMS_EPS) * kg_ref[...]
        o_ref[j] = out.astype(BF16)


def _compress(kv_c, cmp_pe, cmp_w1, cmp_b1, cmp_w2, cmp_b2, k_gain0):
    B, _, S, _ = kv_c.shape
    nr = S // CMP_STRIDE
    half = CMP_STRIDE * HEAD_DIM
    nj = 2 * N_KV_HEADS
    ws = [cmp_pe.reshape(2, 2, half), cmp_w1.astype(BF16), cmp_b1.reshape(2, 1, CMP_HIDDEN), cmp_w2.astype(BF16),
          cmp_b2.reshape(2, 1, HEAD_DIM), k_gain0.reshape(1, HEAD_DIM)]
    full = lambda a: pl.BlockSpec(a.shape, lambda b: (0,) * a.ndim)
    return pl.pallas_call(
        functools.partial(_compress_kernel, n_rows=nr),
        grid=(B,),
        in_specs=[pl.BlockSpec((None,) + kv_c.shape[1:], lambda b: (b, 0, 0, 0))] + [full(a) for a in ws],
        out_specs=pl.BlockSpec((None, nj, nr, HEAD_DIM), lambda b: (b, 0, 0, 0)),
        out_shape=jax.ShapeDtypeStruct((B, nj, nr, HEAD_DIM), BF16),
        compiler_params=_params(("arbitrary",)),
        name="compress",
    )(kv_c, *ws)


def _softmax_cols(s, mask):
    s = jnp.where(mask, s, NEG_INF)
    m = jnp.max(s, axis=0, keepdims=True)
    p = jnp.where(mask, jnp.exp(s - m), 0.0)
    return p / jnp.maximum(jnp.sum(p, axis=0, keepdims=True), 1e-20)


def _attend_kernel(qt_ref, kc_ref, vct_ref, ks_ref, vst_ref, kw_ref, vwt_ref, g_ref, slope_ref, ovt_ref,
                   o_ref, qa_scr, sa_scr, sb_scr, *, n_sel, n_pick, n_cmp_rows):
    QW = GQA_GROUP * Q_BLOCK
    chains = [(k, h) for k in range(ATT_TILES) for h in range(N_KV_HEADS)]
    t0s = [(pl.program_id(1) * ATT_TILES + k) * Q_BLOCK for k in range(ATT_TILES)]
    tqs = [t0 + (lax.broadcasted_iota(jnp.int32, (1, QW), 1) & (Q_BLOCK - 1)) for t0 in t0s]
    cols = lambda k: slice(k * QW, (k + 1) * QW)

    r = lax.broadcasted_iota(jnp.int32, (POS_ROWS, QW), 0)
    qa = {}
    for k, h in chains:
        slope = slope_ref[h]
        qa_scr[k, h, 0:HEAD_DIM, :] = qt_ref[h, :, cols(k)]
        qa_scr[k, h, HEAD_DIM:HEAD_DIM + POS_ROWS, :] = jnp.where(
            r == 0, slope * POS_BASE, jnp.where(r == 1, slope, 0.0)).astype(BF16)
        qa_scr[k, h, HEAD_DIM + POS_ROWS:AUG, :] = jnp.zeros((AUG - HEAD_DIM - POS_ROWS, QW), BF16)
        qa[k, h] = qa_scr[k, h, 0:AUG, :]

    def compressed(k, h, rows):
        sc = _dot(kc_ref[h, 0:rows, :], qa[k, h])
        cpos = lax.broadcasted_iota(jnp.int32, (rows, QW), 0) * CMP_STRIDE + (CMP_BLOCK - 1)
        p_c = _softmax_cols(sc, cpos <= tqs[k])
        o_c = _dot(vct_ref[h, :, 0:rows], p_c.astype(BF16))
        psum = p_c[:, 0:Q_BLOCK]
        for g in range(1, GQA_GROUP):
            psum = psum + p_c[:, g * Q_BLOCK:(g + 1) * Q_BLOCK]
        ph, pl_ = _split(psum)
        return o_c, _dot(ovt_ref[:, 0:rows], ph) + _dot(ovt_ref[:, 0:rows], pl_)

    half = n_cmp_rows // 2
    if half % LANES == 0:
        early = (t0s[-1] + Q_BLOCK - CMP_BLOCK) // CMP_STRIDE < half
        cmp_out = lax.cond(early, lambda _: [compressed(k, h, half) for k, h in chains],
                           lambda _: [compressed(k, h, n_cmp_rows) for k, h in chains], 0)
    else:
        cmp_out = [compressed(k, h, n_cmp_rows) for k, h in chains]
    cmp_out = dict(zip(chains, cmp_out))

    def select(k, h, imp):
        jb = lax.broadcasted_iota(jnp.int32, (n_sel, Q_BLOCK), 0)
        cur = (t0s[k] + lax.broadcasted_iota(jnp.int32, (1, Q_BLOCK), 1)) // SEL_BLOCK
        forced = (jb == 0) | (jb == cur) | (jb == cur - 1)
        imp = jnp.where(forced, imp + FORCE_BONUS, imp)
        imp = jnp.where(jb <= cur, imp, -1.0)
        bias = jnp.full((n_sel, Q_BLOCK), NEG_INF, F32)
        for _ in range(n_pick):
            m = jnp.max(imp, axis=0, keepdims=True)
            first = jnp.min(jnp.where(imp == m, jb, n_sel), axis=0, keepdims=True)
            hit = jb == first
            bias = jnp.where(hit, 0.0, bias)
            imp = jnp.where(hit, -jnp.inf, imp)
        per_unit = SEL_UNIT // SEL_BLOCK
        n_units = n_sel // per_unit
        picked = jnp.where(bias == 0.0, 1.0, 0.0).astype(BF16)
        in_unit = (lax.broadcasted_iota(jnp.int32, (n_units, n_sel), 1) // per_unit
                   == lax.broadcasted_iota(jnp.int32, (n_units, n_sel), 0))
        hits = _dot(jnp.where(in_unit, 1.0, 0.0).astype(BF16), picked)
        live = jnp.where(jnp.max(hits, axis=1, keepdims=True) > 0.5, 1.0, 0.0)
        live = jnp.broadcast_to(live, (n_units, LANES))
        earlier = (lax.broadcasted_iota(jnp.int32, (n_units, n_units), 1)
                   < lax.broadcasted_iota(jnp.int32, (n_units, n_units), 0))
        before = _dot(jnp.where(earlier, 1.0, 0.0).astype(BF16), live.astype(BF16))
        slot = lax.broadcasted_iota(jnp.int32, (n_units, LANES), 1).astype(F32)
        in_slot = jnp.where(before == slot, live, 0.0).astype(BF16)
        r8 = lax.broadcasted_iota(jnp.int32, (SUBLANES, n_units), 0)
        c8 = lax.broadcasted_iota(jnp.int32, (SUBLANES, n_units), 1)
        pick_rows = jnp.where(r8 == 0, c8, jnp.where(r8 == 1, 1, 0)).astype(F32).astype(BF16)
        ids = _dot(pick_rows, in_slot)
        n_live = (before[n_units - 1:n_units, 0:1] + live[n_units - 1:n_units, 0:1])[0, 0]
        bias = jnp.concatenate([bias] * GQA_GROUP, axis=1).astype(BF16)
        qa_scr[k, h, AUG:, :] = jnp.concatenate([bias, jnp.zeros((AUG - n_sel, QW), BF16)], axis=0)
        return ids, n_live

    def window(k, h):
        WK = WINDOW + Q_BLOCK
        w0 = pl.multiple_of(jnp.maximum(t0s[k] - WINDOW, 0), Q_BLOCK)
        sw = _dot(kw_ref[h, pl.ds(w0, WK), :], qa[k, h])
        dist = tqs[k] - (w0 + lax.broadcasted_iota(jnp.int32, (WK, QW), 0))
        in_win = (dist | (WINDOW - 1 - dist)) >= 0
        p_w = _softmax_cols(sw, in_win)
        return _dot(vwt_ref[h, :, pl.ds(w0, WK)], p_w.astype(BF16))

    lists = {c: select(*c, cmp_out[c][1]) for c in chains}
    o_win = {c: window(*c) for c in chains}

    def gathered(k, h, ids, n_slots):
        units = [ids[0, s].astype(jnp.int32) for s in range(n_slots)]
        starts = [pl.multiple_of(u * SEL_UNIT, SEL_UNIT) for u in units]
        keys = jnp.concatenate([ks_ref[h, pl.ds(st, SEL_UNIT), :] for st in starts], axis=0)
        s = _dot(keys, qa_scr[k, h])
        krow = lax.broadcasted_iota(jnp.int32, (SEL_UNIT, QW), 0)
        seen = jnp.concatenate(
            [st + krow <= jnp.where(ids[1, si] > 0.5, tqs[k], -1) for si, st in enumerate(starts)], axis=0)
        s = jnp.where(seen, s, NEG_INF)
        p = jnp.exp(s - jnp.max(s, axis=0, keepdims=True))
        vals = jnp.concatenate([vst_ref[h, :, pl.ds(st, SEL_UNIT)] for st in starts], axis=1)
        return _dot(vals, p.astype(BF16)) / jnp.maximum(jnp.sum(p, axis=0, keepdims=True), 1e-20)

    KC = SEL_KEY_CHUNK

    def swept(k, h):
        last = (t0s[k] + Q_BLOCK + KC - 1) // KC - 1

        def scores_to(ref, j):
            k0 = pl.multiple_of(j * KC, KC)
            ref[...] = _dot(ks_ref[h, pl.ds(k0, KC), :], qa_scr[k, h])

        def accumulate(carry, j, s, p_of):
            m, l, acc = carry
            k0 = pl.multiple_of(j * KC, KC)
            m_new = jnp.maximum(m, jnp.max(s, axis=0, keepdims=True))
            alpha = jnp.exp(m - m_new)
            p = p_of(jnp.exp(s - m_new))
            l = alpha * l + jnp.sum(p, axis=0, keepdims=True)
            acc = alpha * acc + _dot(vst_ref[h, :, pl.ds(k0, KC)], p.astype(BF16))
            return m_new, l, acc

        keep = lambda p: p

        def two_chunks(i, state):
            j = 2 * i
            scores_to(sb_scr, j + 1)
            state = accumulate(state, j, sa_scr[...], keep)
            scores_to(sa_scr, j + 2)
            return accumulate(state, j + 1, sb_scr[...], keep)

        def odd_chunk(state):
            state = accumulate(state, last - 1, sa_scr[...], keep)
            scores_to(sa_scr, last)
            return state

        scores_to(sa_scr, 0)
        init = (jnp.full((1, QW), NEG_INF, F32), jnp.zeros((1, QW), F32), jnp.zeros((HEAD_DIM, QW), F32))
        state = lax.fori_loop(0, last // 2, two_chunks, init)
        state = lax.cond(last % 2 == 1, odd_chunk, lambda st: st, state)
        visible = (last * KC + lax.broadcasted_iota(jnp.int32, (KC, QW), 0)) <= tqs[k]
        _, l_s, acc_s = accumulate(state, last, jnp.where(visible, sa_scr[...], NEG_INF),
                                   lambda p: jnp.where(visible, p, 0.0))
        return acc_s / jnp.maximum(l_s, 1e-20)

    def selected(k, h, ids, n_live, sizes):
        if not sizes:
            return swept(k, h)
        return lax.cond(n_live <= sizes[0], lambda _: gathered(k, h, ids, sizes[0]),
                        lambda _: selected(k, h, ids, n_live, sizes[1:]), 0)

    for k in range(ATT_TILES):
        outs = []
        for h in range(N_KV_HEADS):
            o_s = selected(k, h, *lists[k, h], SEL_SLOTS)
            gk = lambda br: g_ref[h, br:br + 1, cols(k)]
            o = gk(0) * cmp_out[k, h][0] + gk(1) * o_s + gk(2) * o_win[k, h]
            for g in range(0, GQA_GROUP, 2):
                sq = jnp.concatenate([o[:, g * Q_BLOCK:(g + 1) * Q_BLOCK], o[:, (g + 1) * Q_BLOCK:(g + 2) * Q_BLOCK]],
                                     axis=0)
                outs.append(sq.T)
        o_ref[k * Q_BLOCK:(k + 1) * Q_BLOCK, :] = jnp.concatenate(outs, axis=1).astype(BF16)


def _attend(qt, ks, vst, kwn, vwt, gt, kvc_c):
    B, _, S, _ = ks.shape
    nqb = S // Q_BLOCK
    QW = GQA_GROUP * Q_BLOCK
    n_sel = S // SEL_BLOCK
    n_pick = min(SEL_TOPK, n_sel)
    nr = S // CMP_STRIDE
    n_cmp = (S - CMP_BLOCK) // CMP_STRIDE + 1
    assert n_sel % 16 == 0 and n_sel <= AUG

    cpos = np.arange(nr) * CMP_STRIDE + CMP_BLOCK - 1
    digits = np.zeros((nr, AUG - HEAD_DIM), np.float32)
    digits[:, 0] = cpos // POS_BASE
    digits[:, 1] = cpos % POS_BASE
    kc = kvc_c[:, :N_KV_HEADS]
    kc = jnp.concatenate([kc, jnp.broadcast_to(jnp.asarray(digits, BF16), kc.shape[:2] + digits.shape)], axis=-1)
    vct = kvc_c[:, N_KV_HEADS:].transpose(0, 1, 3, 2)
    slopes = 2.0 ** (-8.0 * np.arange(1, N_HEADS + 1) / N_HEADS)
    slope_t = jnp.asarray(np.repeat(slopes.reshape(N_KV_HEADS, GQA_GROUP), Q_BLOCK, axis=1)
                          .reshape(N_KV_HEADS, 1, QW), F32)
    cmp_start = np.arange(n_cmp) * CMP_STRIDE
    sel_start = np.arange(n_sel) * SEL_BLOCK
    ov = ((cmp_start[:, None] <= sel_start[None, :] + SEL_BLOCK - 1)
          & (cmp_start[:, None] + CMP_BLOCK - 1 >= sel_start[None, :])).astype(np.float32)
    ovt = np.zeros((n_sel, nr), np.float32)
    ovt[:, :n_cmp] = ov.T
    ovt = jnp.asarray(ovt, BF16)

    H = N_KV_HEADS
    per_bh = lambda r, c: pl.BlockSpec((None, H, r, c), lambda b, i: (b, 0, 0, 0))
    per_q = lambda r: pl.BlockSpec((None, H, r, ATT_TILES * QW), lambda b, i: (b, 0, 0, i))
    return pl.pallas_call(
        functools.partial(_attend_kernel, n_sel=n_sel, n_pick=n_pick, n_cmp_rows=nr),
        grid=(B, nqb // ATT_TILES),
        in_specs=[per_q(HEAD_DIM), per_bh(nr, AUG), per_bh(HEAD_DIM, nr),
                  per_bh(S, 2 * AUG), per_bh(HEAD_DIM, S), per_bh(S, AUG), per_bh(HEAD_DIM, S),
                  per_q(3),
                  pl.BlockSpec((H, 1, QW), lambda b, i: (0, 0, 0)),
                  pl.BlockSpec((n_sel, nr), lambda b, i: (0, 0))],
        out_specs=pl.BlockSpec((None, ATT_TILES * Q_BLOCK, NSA_WIDTH), lambda b, i: (b, i, 0)),
        out_shape=jax.ShapeDtypeStruct((B, S, NSA_WIDTH), BF16),
        scratch_shapes=[pltpu.VMEM((ATT_TILES, H, 2 * AUG, QW), BF16), pltpu.VMEM((SEL_KEY_CHUNK, QW), F32),
                        pltpu.VMEM((SEL_KEY_CHUNK, QW), F32)],
        compiler_params=_params(("arbitrary", "arbitrary")),
        name="attend",
    )(qt, kc, vct, ks, vst, kwn, vwt, gt, slope_t, ovt)


def _merge_kernel(x_ref, mod_ref, ys_ref, on_ref, gm_ref, wglu_ref, bglu_ref, wa_ref, wb_ref, wo_ref,
                  gain_ref, wrh_ref, wrl_ref, x1_ref, h2_ref, sc_ref):
    D = D_MODEL
    z = jax.nn.gelu(_from_chunk_major(lambda groups, lanes: ys_ref[:, groups, lanes], x_ref.shape[0]))
    glu = z * jax.nn.sigmoid(_dot(z.astype(BF16), wglu_ref[...]) + bglu_ref[...])
    ya = _dot(glu.astype(BF16), wa_ref[...])
    yb = _dot(on_ref[...], wb_ref[...])
    merged = gm_ref[:, :D].astype(F32) * ya + gm_ref[:, D:].astype(F32) * yb
    x1 = x_ref[...] + mod_ref[2:3, :] * _dot(merged.astype(BF16), wo_ref[...])
    x1_ref[...] = x1
    y = x1 * lax.rsqrt(jnp.mean(x1 * x1, axis=-1, keepdims=True) + RMS_EPS)
    h2 = (y * gain_ref[...]) * (1.0 + mod_ref[4:5, :]) + mod_ref[3:4, :]
    h2_ref[...] = h2.astype(BF16)
    hh, hl = _split(h2)
    nt = (((1,), (1,)), ((), ()))
    dg = lambda a, b: lax.dot_general(a, b, nt, preferred_element_type=F32)
    logits = dg(wrh_ref[...], hh) + (dg(wrh_ref[...], hl) + dg(wrl_ref[...], hh))
    sc_ref[...] = jax.nn.sigmoid(logits)


def _merge(x, mod, ys5, o_nsa, gm, w_glu, b_glu, w_a, w_b, w_out, gain_f, w_router):
    B, S, D = x.shape
    tm = min(ROW_TILE, S)
    wrh, wrl = _split(w_router.T)
    ws = [w_glu.astype(BF16), b_glu.reshape(1, -1), w_a.astype(BF16), w_b.astype(BF16), w_out.astype(BF16),
          gain_f, wrh, wrl]
    row = lambda w: pl.BlockSpec((None, tm, w), lambda b, i: (b, i, 0))
    full = lambda a: pl.BlockSpec(a.shape, lambda b, i: (0,) * a.ndim)
    return pl.pallas_call(
        _merge_kernel,
        grid=(B, S // tm),
        in_specs=[row(D), pl.BlockSpec((None, N_MOD, D), lambda b, i: (b, 0, 0)),
                  _s5_chunk_spec(tm), row(NSA_WIDTH), row(2 * D)] + [full(a) for a in ws],
        out_specs=[row(D), row(D), pl.BlockSpec((None, N_EXPERTS, tm), lambda b, i: (b, 0, i))],
        out_shape=[jax.ShapeDtypeStruct((B, S, D), F32), jax.ShapeDtypeStruct((B, S, D), BF16),
                   jax.ShapeDtypeStruct((B, N_EXPERTS, S), F32)],
        compiler_params=_params(("arbitrary", "arbitrary")),
        name="merge",
    )(x, mod, ys5, o_nsa, gm, *ws)


def _first_argmax_rows(v, idx, n):
    m = jnp.max(v, axis=0, keepdims=True)
    first = jnp.min(jnp.where(v == m, idx, n), axis=0, keepdims=True)
    return idx == first, m


def _route_kernel(sc_ref, bias_ref, tri_ref, rank_ref, w_ref, cnt_ref):
    E, NG = N_EXPERTS, N_EXPERT_GROUPS
    GS = E // NG
    sc = sc_ref[...]
    TM = sc.shape[1]
    sel = sc + bias_ref[...]
    i8 = lax.broadcasted_iota(jnp.int32, (GS, TM), 0)
    gscore = []
    for g in range(NG):
        blk = sel[g * GS:(g + 1) * GS, :]
        hit, m1 = _first_argmax_rows(blk, i8, GS)
        m2 = jnp.max(jnp.where(hit, -jnp.inf, blk), axis=0, keepdims=True)
        gscore.append(m1 + m2)
    gscore = jnp.concatenate(gscore, axis=0)
    ig = lax.broadcasted_iota(jnp.int32, (NG, TM), 0)
    gmask = jnp.zeros((NG, TM), F32)
    for _ in range(TOPK_EXPERT_GROUPS):
        hit, _m = _first_argmax_rows(gscore, ig, NG)
        gmask = jnp.where(hit, 1.0, gmask)
        gscore = jnp.where(hit, -jnp.inf, gscore)
    emask = jnp.concatenate([jnp.broadcast_to(gmask[g:g + 1, :], (GS, TM)) for g in range(NG)], axis=0)
    cand = jnp.where(emask > 0.5, sel, NEG_INF)
    ie = lax.broadcasted_iota(jnp.int32, (E, TM), 0)
    chosen = jnp.zeros((E, TM), F32)
    for _ in range(TOP_K):
        hit, _m = _first_argmax_rows(cand, ie, E)
        chosen = jnp.where(hit, 1.0, chosen)
        cand = jnp.where(hit, -jnp.inf, cand)
    w = chosen * sc
    w = w / jnp.sum(w, axis=0, keepdims=True) * ROUTED_SCALE
    sub = rank_ref.shape[-1]
    for q in range(TM // sub):
        cols = slice(q * sub, (q + 1) * sub)
        cb = chosen[:, cols].astype(BF16)
        prefix = _dot(cb, tri_ref[...])
        rank_ref[q] = jnp.where(chosen[:, cols] > 0.5, prefix, RANK_NONE)
        w_ref[q] = w[:, cols]
        cnt_ref[q] = _dot(cb, jnp.ones((sub, LANES), BF16))


def _route(scores_t, router_bias):
    B, E, S = scores_t.shape
    TM = min(ROUTE_TILE, S)
    per_step = min(MOE_OUTER, S) // TM
    nj = S // (TM * per_step)
    ns = B * nj * per_step
    tri = jnp.asarray(np.triu(np.ones((TM, TM), np.float32), k=1), BF16)
    tile = lambda w: pl.BlockSpec((per_step, E, w), lambda b, j: (b * nj + j, 0, 0))
    return pl.pallas_call(
        _route_kernel,
        grid=(B, nj),
        in_specs=[pl.BlockSpec((None, E, TM * per_step), lambda b, j: (b, 0, j)),
                  pl.BlockSpec((E, 1), lambda b, j: (0, 0)),
                  pl.BlockSpec((TM, TM), lambda b, j: (0, 0))],
        out_specs=[tile(TM), tile(TM), tile(LANES)],
        out_shape=[jax.ShapeDtypeStruct((ns, E, TM), F32), jax.ShapeDtypeStruct((ns, E, TM), F32),
                   jax.ShapeDtypeStruct((ns, E, LANES), F32)],
        compiler_params=_params(("arbitrary", "arbitrary")),
        name="route",
    )(scores_t, router_bias.reshape(E, 1), tri)


def _moe_kernel(cnt_ref, h_ref, x1_ref, mod_ref, rank_ref, w_ref, wg_ref, wu_ref, wd_ref,
                sg_ref, su_ref, sd_ref, o_ref, xc, yc, pc, wc, *, n_sub, tm, n_exp):
    to = pl.program_id(0)
    eb = pl.program_id(1)
    gate = mod_ref[5:6, :]

    @pl.when(eb == 0)
    def _shared():
        for s in range(n_sub):
            rows = pl.ds(s * tm, tm)
            hs = h_ref[rows, :]
            hid = jax.nn.silu(_dot(hs, sg_ref[...])) * _dot(hs, su_ref[...])
            o_ref[rows, :] = x1_ref[rows, :] + gate * _dot(hid.astype(BF16), sd_ref[...])
            xc[s] = jnp.zeros(xc.shape[1:], BF16)
            yc[s] = jnp.zeros(yc.shape[1:], BF16)
            wc[s] = jnp.zeros(wc.shape[1:], F32)

    PIECE, WIN = MOE_PIECE, MOE_WIN
    tn = (((0,), (0,)), ((), ()))
    experts = [eb * n_exp + j for j in range(n_exp)]
    cnt = [[cnt_ref[(to * n_sub + s) * N_EXPERTS + e] for e in experts] for s in range(n_sub)]
    off = []
    for s in range(n_sub):
        o = [jnp.int32(0)]
        for j in range(n_exp):
            o.append(o[-1] + ((cnt[s][j] + 15) // 16) * 16)
        off.append(o)
    n_pieces = [(off[s][n_exp] + PIECE - 1) // PIECE for s in range(n_sub)]
    fits = n_pieces[0] < MOE_ROWS // PIECE
    for s in range(1, n_sub):
        fits = jnp.logical_and(fits, n_pieces[s] < MOE_ROWS // PIECE)

    def one_hot_rows(s, base, rows_n, js):
        slot = (lax.broadcasted_iota(jnp.int32, (rows_n, tm), 0) + base).astype(F32)
        pick = jnp.zeros((rows_n, tm), F32)
        wacc = jnp.zeros((rows_n, tm), F32)
        for j in js:
            hit = (slot - off[s][j].astype(F32)) == rank_ref[s, pl.ds(experts[j], 1), :]
            pick = jnp.where(hit, 1.0, pick)
            wacc = jnp.where(hit, w_ref[s, pl.ds(experts[j], 1), :], wacc)
        return pick.astype(BF16), jnp.sum(wacc, axis=1, keepdims=True)

    def mlp(xg, j):
        hid = jax.nn.silu(_dot(xg, wg_ref[j])) * _dot(xg, wu_ref[j])
        return _dot(hid.astype(BF16), wd_ref[j])

    def gather(s, start, n):
        picks = []
        for c in range(0, n, PIECE // 2):
            m = min(PIECE // 2, n - c)
            pick, wrow = one_hot_rows(s, start + c, m, range(n_exp))
            wc[s, pl.ds(start + c, m), :] = wrow
            picks.append(pick)
        if n % PIECE:
            picks.append(jnp.zeros((PIECE - n % PIECE, tm), BF16))
        pick = jnp.concatenate(picks, axis=0)
        xc[s, pl.ds(start, n), :] = _dot(pick[:n], h_ref[pl.ds(s * tm, tm), :]).astype(BF16)
        for q in range(0, pick.shape[0], PIECE):
            pc[s, :, pl.ds(start + q, PIECE)] = pick[q:q + PIECE].T

    def window(j, p, first):
        starts = [pl.multiple_of(jnp.minimum(off[s][j] + p * WIN, (n_pieces[s] + 1) * PIECE - WIN), 16)
                  for s in range(n_sub)]
        xg = jnp.concatenate([xc[s, pl.ds(starts[s], WIN), :] for s in range(n_sub)], axis=0)
        out = mlp(xg, j)
        for s in range(n_sub):
            r = pl.ds(starts[s], WIN)
            new = (out[s * WIN:(s + 1) * WIN] * wc[s, r, :] * gate).astype(BF16)
            if not first:
                rank = starts[s] - off[s][j] + lax.broadcasted_iota(jnp.int32, (WIN, 1), 0)
                new = jnp.where(rank < cnt[s][j], new, yc[s, r, :])
            yc[s, r, :] = new

    def scatter(s, start, n):
        r = pl.ds(start, n)
        o_ref[pl.ds(s * tm, tm), :] += _dot(pc[s, :, r], yc[s, r, :])

    @pl.when(fits)
    def _packed():
        for s in range(n_sub):
            gather(s, 0, MOE_BASE_ROWS)
        for s in range(n_sub):
            @pl.when(off[s][n_exp] > MOE_BASE_ROWS)
            def _second(s=s):
                gather(s, PIECE, PIECE)

            @pl.when(n_pieces[s] > 2)
            def _third(s=s):
                gather(s, 2 * PIECE, PIECE)
        for j in range(n_exp):
            window(j, 0, True)
        for j in range(n_exp):
            most = cnt[0][j]
            for s in range(1, n_sub):
                most = jnp.maximum(most, cnt[s][j])

            @pl.when(most > WIN)
            def _more(j=j, most=most):
                lax.fori_loop(1, (most + WIN - 1) // WIN, lambda p, c: (window(j, p, False), c)[1], 0)
        for s in range(n_sub):
            scatter(s, 0, 2 * PIECE)
        for s in range(n_sub):
            @pl.when(n_pieces[s] > 2)
            def _third_out(s=s):
                scatter(s, 2 * PIECE, PIECE)

    @pl.when(jnp.logical_not(fits))
    def _unpacked():
        for j in range(n_exp):
            for s in range(n_sub):
                rows = pl.ds(s * tm, tm)

                def block(bi, carry, j=j, s=s, rows=rows):
                    slot0 = off[s][j] + bi * PIECE
                    pick, wrow = one_hot_rows(s, slot0, PIECE, [j])
                    out = mlp(_dot(pick, h_ref[rows, :]).astype(BF16), j)
                    ow = (out * wrow * gate).astype(BF16)
                    o_ref[rows, :] += lax.dot_general(pick, ow, tn, preferred_element_type=F32)
                    return carry

                lax.fori_loop(0, (cnt[s][j] + PIECE - 1) // PIECE, block, 0)


def _moe(h2, x1, mod, rank_t, w_t, counts, w_gate, w_up, w_down, ws_gate, ws_up, ws_down, seq):
    T, D = h2.shape
    ns, E, tm = rank_t.shape
    tmo = min(MOE_OUTER, seq)
    n_sub = tmo // tm
    n_exp = MOE_EXPERTS_PER_STEP
    per_b = seq // tmo
    ws = [w_gate.astype(BF16), w_up.astype(BF16), w_down.astype(BF16)]
    sh = [ws_gate.astype(BF16), ws_up.astype(BF16), ws_down.astype(BF16)]
    tok = lambda: pl.BlockSpec((tmo, D), lambda t, e, c: (t, 0))
    sub = lambda: pl.BlockSpec((n_sub, E, tm), lambda t, e, c: (t, 0, 0))
    exp = lambda a: pl.BlockSpec((n_exp,) + a.shape[1:], lambda t, e, c: (e, 0, 0))
    full = lambda a: pl.BlockSpec(a.shape, lambda t, e, c: (0, 0))
    grid_spec = pltpu.PrefetchScalarGridSpec(
        num_scalar_prefetch=1,
        grid=(T // tmo, E // n_exp),
        in_specs=[tok(), tok(), pl.BlockSpec((None, N_MOD, D), lambda t, e, c: (t // per_b, 0, 0)),
                  sub(), sub(), exp(ws[0]), exp(ws[1]), exp(ws[2]), full(sh[0]), full(sh[1]), full(sh[2])],
        out_specs=tok(),
        scratch_shapes=[pltpu.VMEM((n_sub, MOE_ROWS, D), BF16), pltpu.VMEM((n_sub, MOE_ROWS, D), BF16),
                        pltpu.VMEM((n_sub, tm, MOE_ROWS), BF16), pltpu.VMEM((n_sub, MOE_ROWS, 1), F32)],
    )
    return pl.pallas_call(
        functools.partial(_moe_kernel, n_sub=n_sub, tm=tm, n_exp=n_exp),
        grid_spec=grid_spec,
        out_shape=jax.ShapeDtypeStruct((T, D), F32),
        compiler_params=_params(("arbitrary", "arbitrary")),
        name="moe",
    )(counts, h2, x1, mod, rank_t, w_t, *ws, *sh)


def kernel(x, c, w_ada, b_ada, norm_mix_gain, norm_ffn_gain, w_in, s5_lambda_re, s5_lambda_im, s5_log_dt, s5_b_re, s5_b_im, s5_c_re, s5_c_im, s5_d, s5_w_glu, s5_b_glu, q_norm_gain, k_norm_gain, cmp_pe, cmp_w1, cmp_b1, cmp_w2, cmp_b2, w_branch_a, w_branch_b, w_out, w_router, router_bias, w_gate, w_up, w_down, ws_gate, ws_up, ws_down):
    B, S, D = x.shape
    for l in range(w_ada.shape[0]):
        mod = _ada(c, w_ada[l], b_ada[l]).reshape(B, N_MOD, D)
        u, kvc, qt, ks, vst, kwn, vwt, gt, gm = _inproj(x, mod, norm_mix_gain[l:l + 1], w_in[l],
                                                        q_norm_gain[l], k_norm_gain[l])
        tables = _s5_tables(s5_lambda_re[l], s5_lambda_im[l], s5_log_dt[l], s5_b_re[l], s5_b_im[l],
                            s5_c_re[l], s5_c_im[l], s5_d[l])
        ys5 = _s5(u, tables)
        kvc_c = _compress(kvc, cmp_pe[l], cmp_w1[l], cmp_b1[l], cmp_w2[l], cmp_b2[l], k_norm_gain[l, 0])
        o_nsa = _attend(qt, ks, vst, kwn, vwt, gt, kvc_c)
        x1, h2, scores_t = _merge(x, mod, ys5, o_nsa, gm, s5_w_glu[l], s5_b_glu[l], w_branch_a[l],
                                  w_branch_b[l], w_out[l], norm_ffn_gain[l:l + 1], w_router[l])
        rank_t, w_t, cnt = _route(scores_t, router_bias[l])
        counts = cnt[:, :, 0].astype(jnp.int32).reshape(-1)
        x = _moe(h2.reshape(B * S, D), x1.reshape(B * S, D), mod, rank_t, w_t, counts,
                 w_gate[l], w_up[l], w_down[l], ws_gate[l], ws_up[l], ws_down[l], S).reshape(B, S, D)
    return x
```

```python
import functools

import numpy as np
import jax
import jax.numpy as jnp
from jax import lax
from jax.experimental import pallas as pl
from jax.experimental.pallas import tpu as pltpu

F32 = jnp.float32
BF16 = jnp.bfloat16

D_MODEL = 1024
S5_WIDTH = 512
S5_GROUP = 16
S5_GROUPS = S5_WIDTH // S5_GROUP
S5_STATE = 64
N_HEADS = 8
N_KV_HEADS = 2
GQA_GROUP = N_HEADS // N_KV_HEADS
HEAD_DIM = 64
NSA_WIDTH = N_HEADS * HEAD_DIM
KV_WIDTH = 2 * N_KV_HEADS * HEAD_DIM
CMP_BLOCK = 32
CMP_STRIDE = 16
CMP_HIDDEN = 256
SEL_BLOCK = 64
SEL_TOPK = 8
WINDOW = 256
Q_BLOCK = 128
FORCE_BONUS = 1e3
N_EXPERTS = 64
TOP_K = 8
N_EXPERT_GROUPS = 8
TOPK_EXPERT_GROUPS = 4
ROUTED_SCALE = 2.5
RMS_EPS = 1e-6
NEG_INF = -1e30
N_MOD = 6

LANES = 128
SUBLANES = 8
S5_CHUNK = 16
S5_CW = S5_CHUNK * S5_GROUP
ROW_TILE = 1024
ATT_TILES = 2
SEL_KEY_CHUNK = 512
SEL_UNIT = 128
SEL_SLOTS = (5, 6, 7, 8)
POS_BASE = 64
POS_ROWS = 16
AUG = 128
ROUTE_TILE = 256
MOE_OUTER = 1024
MOE_EXPERTS_PER_STEP = 4
MOE_PIECE = 128
MOE_BASE_ROWS = 208
MOE_WIN = 64
MOE_ROWS = 512
RANK_NONE = -float(1 << 20)
VMEM_LIMIT = 56 * 1024 * 1024


def _dot(a, b):
    return jnp.dot(a, b, preferred_element_type=F32)


def _split(a):
    hi = a.astype(BF16)
    lo = (a - hi.astype(F32)).astype(BF16)
    return hi, lo


def _dot3(a, bh, bl):
    ah, al = _split(a)
    return _dot(ah, bh) + (_dot(al, bh) + _dot(ah, bl))


def _segment_transpose(x):
    assert x.shape[-2:] == (SUBLANES, LANES) and LANES // S5_GROUP == SUBLANES
    nd = x.ndim
    i = lax.broadcasted_iota(jnp.int32, x.shape, nd - 2)
    seg = lax.broadcasted_iota(jnp.int32, x.shape, nd - 1) // S5_GROUP
    out = x
    for d in range(1, SUBLANES):
        r = pltpu.roll(pltpu.roll(x, SUBLANES - d, axis=nd - 2), S5_GROUP * d, axis=nd - 1)
        out = jnp.where(seg == ((i + d) & (SUBLANES - 1)), r, out)
    return out


def _to_chunk_major(u, put):
    rows = u.shape[0]
    u3 = u.reshape(rows // SUBLANES, SUBLANES, S5_WIDTH)
    halves = S5_CHUNK // SUBLANES
    for jb in range(S5_WIDTH // LANES):
        t = _segment_transpose(u3[:, :, LANES * jb:LANES * (jb + 1)])
        t = t.reshape(rows // S5_CHUNK, halves, SUBLANES, LANES)
        for hf in range(halves):
            put(slice(SUBLANES * jb, SUBLANES * (jb + 1)), slice(LANES * hf, LANES * (hf + 1)), t[:, hf])


def _from_chunk_major(get, rows):
    halves = S5_CHUNK // SUBLANES
    cols = []
    for jb in range(S5_WIDTH // LANES):
        parts = [_segment_transpose(get(slice(SUBLANES * jb, SUBLANES * (jb + 1)), slice(LANES * hf, LANES * (hf + 1))))
                 for hf in range(halves)]
        cols.append(jnp.stack(parts, axis=1).reshape(rows, LANES))
    return jnp.concatenate(cols, axis=1)


def _params(sem):
    return pltpu.CompilerParams(dimension_semantics=sem, vmem_limit_bytes=VMEM_LIMIT)


def _ada_kernel(c_ref, w_ref, b_ref, o_ref):
    cs = jax.nn.silu(c_ref[...])
    wh, wl = _split(w_ref[...])
    o_ref[...] = _dot3(cs, wh, wl) + b_ref[...]


def _ada(c, w_ada, b_ada):
    B, D = c.shape
    return pl.pallas_call(
        _ada_kernel,
        grid=(N_MOD,),
        in_specs=[pl.BlockSpec((B, D), lambda j: (0, 0)),
                  pl.BlockSpec((D, D), lambda j: (0, j)),
                  pl.BlockSpec((1, D), lambda j: (0, j))],
        out_specs=pl.BlockSpec((B, D), lambda j: (0, j)),
        out_shape=jax.ShapeDtypeStruct((B, N_MOD * D), F32),
        compiler_params=_params(("arbitrary",)),
        name="ada",
    )(c, w_ada, b_ada.reshape(1, N_MOD * D))


def _head_norm(v, bd, gain):
    sq = v * v
    sh, sl = _split(sq)
    ms = _dot(sh, bd) + _dot(sl, bd)
    return v * lax.rsqrt(ms + RMS_EPS) * gain


def _inproj_kernel(x_ref, mod_ref, gain_ref, wm_ref, wt_ref, wgn_ref, wgm_ref, bd_ref, qg_ref, kg_ref,
                   u_ref, kvc_ref, qt_ref, ksa_ref, vst_ref, kwa_ref, vwt_ref, gt_ref, gm_ref):
    x = x_ref[...]
    shift = mod_ref[0:1, :]
    scale = mod_ref[1:2, :]
    y = x * lax.rsqrt(jnp.mean(x * x, axis=-1, keepdims=True) + RMS_EPS)
    h = (y * gain_ref[...]) * (1.0 + scale) + shift
    hb = h.astype(BF16)
    main = _dot(hb, wm_ref[...])
    tm = x.shape[0]

    def put(groups, lanes, block):
        u_ref[:, groups, lanes] = block

    _to_chunk_major(main[:, :S5_WIDTH], put)
    o = S5_WIDTH
    for part in range(KV_WIDTH // LANES):
        kvc_ref[part] = main[:, o + part * LANES:o + (part + 1) * LANES]
    o += KV_WIDTH
    kw = N_KV_HEADS * HEAD_DIM
    bd = bd_ref[...]
    ks = _head_norm(main[:, o:o + kw], bd[:kw, :kw], kg_ref[1:2, :]).astype(BF16)
    kwn = _head_norm(main[:, o + kw:o + 2 * kw], bd[:kw, :kw], kg_ref[2:3, :]).astype(BF16)
    pos = pl.program_id(1) * tm + lax.broadcasted_iota(jnp.int32, (tm, 1), 0)
    lane = lax.broadcasted_iota(jnp.int32, (tm, AUG), 1)
    digits = jnp.where(lane == 0, pos // POS_BASE, jnp.where(lane == 1, pos % POS_BASE, 0))
    digits = digits[:, :AUG - HEAD_DIM].astype(F32).astype(BF16)
    onehot = jnp.where(lane == pos // SEL_BLOCK, 1.0, 0.0).astype(BF16)
    for hh in range(N_KV_HEADS):
        ksa_ref[hh, :, 0:HEAD_DIM] = ks[:, hh * HEAD_DIM:(hh + 1) * HEAD_DIM]
        ksa_ref[hh, :, HEAD_DIM:AUG] = digits
        ksa_ref[hh, :, AUG:] = onehot
        kwa_ref[hh, :, 0:HEAD_DIM] = kwn[:, hh * HEAD_DIM:(hh + 1) * HEAD_DIM]
        kwa_ref[hh, :, HEAD_DIM:] = digits

    nt = (((1,), (1,)), ((), ()))
    tt = lax.dot_general(wt_ref[...], hb, nt, preferred_element_type=F32)
    qt = tt[:NSA_WIDTH]
    sq = qt * qt
    sh, sl = _split(sq)
    qn = (qt * lax.rsqrt(_dot(bd, sh) + _dot(bd, sl) + RMS_EPS) * qg_ref[...] * (HEAD_DIM ** -0.5)).astype(BF16)
    QW = GQA_GROUP * Q_BLOCK
    for qb in range(tm // Q_BLOCK):
        for hd in range(N_HEADS):
            hh, g = divmod(hd, GQA_GROUP)
            qt_ref[hh, :, qb * QW + g * Q_BLOCK:qb * QW + (g + 1) * Q_BLOCK] = (
                qn[hd * HEAD_DIM:(hd + 1) * HEAD_DIM, qb * Q_BLOCK:(qb + 1) * Q_BLOCK])
    vst_ref[...] = tt[NSA_WIDTH:NSA_WIDTH + kw].reshape(N_KV_HEADS, HEAD_DIM, tm).astype(BF16)
    vwt_ref[...] = tt[NSA_WIDTH + kw:].reshape(N_KV_HEADS, HEAD_DIM, tm).astype(BF16)
    gn = jax.nn.sigmoid(lax.dot_general(wgn_ref[...], hb, nt, preferred_element_type=F32))
    for qb in range(tm // Q_BLOCK):
        for br in range(3):
            for hd in range(N_HEADS):
                hh, g = divmod(hd, GQA_GROUP)
                r = br * N_HEADS + hd
                gt_ref[hh, br:br + 1, qb * QW + g * Q_BLOCK:qb * QW + (g + 1) * Q_BLOCK] = (
                    gn[r:r + 1, qb * Q_BLOCK:(qb + 1) * Q_BLOCK])
    gm_ref[...] = jax.nn.sigmoid(_dot(hb, wgm_ref[...])).astype(BF16)


def _s5_chunk_spec(tm):
    assert S5_CHUNK == 16 and S5_GROUP == 16, "the segment transposes assume 16 steps x 16 channels"
    return pl.BlockSpec((tm // S5_CHUNK, None, S5_GROUPS, S5_CW), lambda b, i: (i, b, 0, 0))


def _inproj(x, mod, gain, w_in, q_gain, k_gain):
    B, S, D = x.shape
    tm = min(ROW_TILE, S)
    assert S // SEL_BLOCK <= AUG and tm % Q_BLOCK == 0
    kw = N_KV_HEADS * HEAD_DIM
    cols = np.cumsum((0,) + (S5_WIDTH, NSA_WIDTH, KV_WIDTH, KV_WIDTH, KV_WIDTH, 3 * N_HEADS, 2 * D))
    c_u, c_q, c_kvc, c_kvs, c_kvw, c_gn, c_gm = cols[:7]
    sl = lambda a, n: w_in[:, a:a + n]
    wm = jnp.concatenate([sl(c_u, S5_WIDTH), sl(c_kvc, KV_WIDTH), sl(c_kvs, kw), sl(c_kvw, kw)], axis=1).astype(BF16)
    wt = jnp.concatenate([sl(c_q, NSA_WIDTH), sl(c_kvs + kw, kw), sl(c_kvw + kw, kw)], axis=1).T.astype(BF16)
    wgn = sl(c_gn, 3 * N_HEADS).T.astype(BF16)
    wgm = sl(c_gm, 2 * D).astype(BF16)
    seg = np.arange(NSA_WIDTH) // HEAD_DIM
    bd = jnp.asarray((seg[:, None] == seg[None, :]).astype(np.float32) / HEAD_DIM, BF16)
    qg = jnp.tile(q_gain, N_HEADS).reshape(NSA_WIDTH, 1)
    kg = jnp.tile(k_gain, (1, N_KV_HEADS))
    nq = tm // Q_BLOCK * GQA_GROUP * Q_BLOCK
    row = lambda w: pl.BlockSpec((None, tm, w), lambda b, i: (b, i, 0))
    full = lambda a: pl.BlockSpec(a.shape, lambda b, i: (0,) * a.ndim)
    rows4 = lambda w: pl.BlockSpec((None, N_KV_HEADS, tm, w), lambda b, i: (b, 0, i, 0))
    cols4 = lambda r, w: pl.BlockSpec((None, N_KV_HEADS, r, w), lambda b, i: (b, 0, 0, i))
    nqt = S // Q_BLOCK * GQA_GROUP * Q_BLOCK
    return pl.pallas_call(
        _inproj_kernel,
        grid=(B, S // tm),
        in_specs=[row(D), pl.BlockSpec((None, N_MOD, D), lambda b, i: (b, 0, 0)),
                  full(gain), full(wm), full(wt), full(wgn), full(wgm), full(bd), full(qg), full(kg)],
        out_specs=[_s5_chunk_spec(tm), pl.BlockSpec((None, KV_WIDTH // LANES, tm, LANES), lambda b, i: (b, 0, i, 0)),
                   cols4(HEAD_DIM, nq), rows4(2 * AUG), cols4(HEAD_DIM, tm),
                   rows4(AUG), cols4(HEAD_DIM, tm), cols4(3, nq), row(2 * D)],
        out_shape=[jax.ShapeDtypeStruct((S // S5_CHUNK, B, S5_GROUPS, S5_CW), F32),
                   jax.ShapeDtypeStruct((B, KV_WIDTH // LANES, S, LANES), F32),
                   jax.ShapeDtypeStruct((B, N_KV_HEADS, HEAD_DIM, nqt), BF16),
                   jax.ShapeDtypeStruct((B, N_KV_HEADS, S, 2 * AUG), BF16),
                   jax.ShapeDtypeStruct((B, N_KV_HEADS, HEAD_DIM, S), BF16),
                   jax.ShapeDtypeStruct((B, N_KV_HEADS, S, AUG), BF16),
                   jax.ShapeDtypeStruct((B, N_KV_HEADS, HEAD_DIM, S), BF16),
                   jax.ShapeDtypeStruct((B, N_KV_HEADS, 3, nqt), F32),
                   jax.ShapeDtypeStruct((B, S, 2 * D), BF16)],
        compiler_params=_params(("arbitrary", "arbitrary")),
        name="inproj",
    )(x, mod, gain, wm, wt, wgn, wgm, bd, qg, kg)


def _s5_tables(lam_re, lam_im, log_dt, b_re, b_im, c_re, c_im, d_skip):
    L, G = S5_CHUNK, S5_GROUPS
    lr, li = lam_re.astype(F32), lam_im.astype(F32)
    dt = jnp.exp(log_dt.astype(F32))[:, None]
    mag = jnp.exp(lr * dt)
    abar_re, abar_im = mag * jnp.cos(li * dt), mag * jnp.sin(li * dt)
    num_re, num_im = abar_re - 1.0, abar_im
    den = lr * lr + li * li
    coef_re = (num_re * lr + num_im * li) / den
    coef_im = (num_im * lr - num_re * li) / den
    br, bi = b_re.astype(F32), b_im.astype(F32)
    bbar_re = coef_re[..., None] * br - coef_im[..., None] * bi
    bbar_im = coef_re[..., None] * bi + coef_im[..., None] * br
    k = jnp.arange(L + 1, dtype=F32)[:, None]
    pmag = jnp.exp((lr * dt)[:, None, :] * k)
    ang = (li * dt)[:, None, :] * k
    pw = jnp.stack([jnp.tile(pmag * jnp.cos(ang), (1, 1, 2)), jnp.tile(pmag * jnp.sin(ang), (1, 1, 2))], axis=1)
    brt, bit = bbar_re.transpose(0, 2, 1), bbar_im.transpose(0, 2, 1)
    ba = jnp.concatenate([brt, bit], axis=2)
    bb = jnp.concatenate([-bit, brt], axis=2)
    cr, ci = c_re.astype(F32), c_im.astype(F32)
    ca = jnp.concatenate([cr, -ci], axis=2)
    cb = jnp.concatenate([-ci, -cr], axis=2)
    dv = jnp.tile(d_skip.astype(F32), (1, L)).reshape(G, 1, S5_CW)
    return pw, ba, bb, ca, cb, dv


def _s5_kernel(u_ref, pw_ref, ba_ref, bb_ref, ca_ref, cb_ref, dv_ref,
               y_ref, vr_scr, vi_scr, xr_scr, xi_scr, *, n_chunks, bsz):
    L, P, N = S5_CHUNK, S5_GROUP, S5_STATE
    nt = (((1,), (1,)), ((), ()))
    dot_nt = lambda a, b: lax.dot_general(a, b, nt, preferred_element_type=F32)
    mix = lambda a, b, k: a * pw_ref[0, k:k + 1, :] + b * pw_ref[1, k:k + 1, :]
    ba, bb, ca, cb = ba_ref[...], bb_ref[...], ca_ref[...], cb_ref[...]
    ws = jnp.concatenate([mix(ba, bb, L - 1 - s) for s in range(L)], axis=0)
    wot = jnp.concatenate([mix(ca, cb, k) for k in range(L + 1)], axis=0)
    wsh, wsl = _split(ws)
    wth, wtl = _split(wot)
    bah, bal = _split(ba)
    kern = dot_nt(bah, wth[:L * P]) + (dot_nt(bal, wth[:L * P]) + dot_nt(bah, wtl[:L * P]))
    lane = lax.broadcasted_iota(jnp.int32, kern.shape, 1)
    mt = jnp.concatenate([kern] + [jnp.where(lane >= P * s, pltpu.roll(kern, P * s, axis=1), 0.0)
                                   for s in range(1, L)], axis=0)
    mth, mtl = _split(mt)

    u = u_ref[...]
    uh, ul = _split(u)
    y = _dot(uh, mth) + (_dot(ul, mth) + _dot(uh, mtl))
    v = _dot(uh, wsh) + (_dot(ul, wsh) + _dot(uh, wsl))
    vr_scr[...] = v[:, :N]
    vi_scr[...] = v[:, N:]
    a_r = pw_ref[0, L:L + 1, :N]
    a_i = pw_ref[1, L:L + 1, :N]

    def step(c, x):
        xr, xi = x
        r = pl.ds(pl.multiple_of(c * bsz, bsz), bsz)
        xr_scr[r, :] = xr
        xi_scr[r, :] = xi
        return (a_r * xr - a_i * xi + vr_scr[r, :], a_r * xi + a_i * xr + vi_scr[r, :])

    zero = jnp.zeros((bsz, N), F32)
    lax.fori_loop(0, n_chunks, step, (zero, zero), unroll=8)
    xh, xl = _split(jnp.concatenate([xr_scr[...], xi_scr[...]], axis=1))
    y = y + (dot_nt(xh, wth[P:]) + (dot_nt(xl, wth[P:]) + dot_nt(xh, wtl[P:])))
    y_ref[...] = y + dv_ref[...] * u


def _s5(ug, tables):
    nc, B, G, cw = ug.shape
    N = S5_STATE
    grp = lambda a: pl.BlockSpec((None,) + a.shape[1:], lambda g: (g,) + (0,) * (a.ndim - 1))
    rows = pl.BlockSpec((nc * B, cw), lambda g: (0, g))
    y = pl.pallas_call(
        functools.partial(_s5_kernel, n_chunks=nc, bsz=B),
        grid=(G,),
        in_specs=[rows] + [grp(t) for t in tables],
        out_specs=rows,
        out_shape=jax.ShapeDtypeStruct((nc * B, G * cw), F32),
        scratch_shapes=[pltpu.VMEM((nc * B, N), F32)] * 4,
        compiler_params=_params(("arbitrary",)),
        name="s5",
    )(ug.reshape(nc * B, G * cw), *tables)
    return y.reshape(nc, B, G, cw)


def _compress_kernel(x_ref, pe_ref, w1_ref, b1_ref, w2_ref, b2_ref, kg_ref, o_ref, *, n_rows):
    half = CMP_STRIDE * HEAD_DIM
    for j in range(2 * N_KV_HEADS):
        kv, hd = divmod(j, N_KV_HEADS)
        x = jnp.concatenate(
            [x_ref[kv, pl.ds(t, n_rows, stride=CMP_STRIDE), :][:, hd * HEAD_DIM:(hd + 1) * HEAD_DIM]
             for t in range(CMP_STRIDE)], axis=1)
        a = _dot((x + pe_ref[kv, 0:1, :]).astype(BF16), w1_ref[kv, :half, :])
        b = _dot((x + pe_ref[kv, 1:2, :]).astype(BF16), w1_ref[kv, half:, :])
        hid = jax.nn.gelu(a + pltpu.roll(b, n_rows - 1, axis=0) + b1_ref[kv])
        out = _dot(hid.astype(BF16), w2_ref[kv]) + b2_ref[kv]
        if kv == 0:
            out = out * lax.rsqrt(jnp.mean(out * out, axis=-1, keepdims=True) + RMS_EPS) * kg_ref[...]
        o_ref[j] = out.astype(BF16)


def _compress(kv_c, cmp_pe, cmp_w1, cmp_b1, cmp_w2, cmp_b2, k_gain0):
    B, _, S, _ = kv_c.shape
    nr = S // CMP_STRIDE
    half = CMP_STRIDE * HEAD_DIM
    nj = 2 * N_KV_HEADS
    ws = [cmp_pe.reshape(2, 2, half), cmp_w1.astype(BF16), cmp_b1.reshape(2, 1, CMP_HIDDEN), cmp_w2.astype(BF16),
          cmp_b2.reshape(2, 1, HEAD_DIM), k_gain0.reshape(1, HEAD_DIM)]
    full = lambda a: pl.BlockSpec(a.shape, lambda b: (0,) * a.ndim)
    return pl.pallas_call(
        functools.partial(_compress_kernel, n_rows=nr),
        grid=(B,),
        in_specs=[pl.BlockSpec((None,) + kv_c.shape[1:], lambda b: (b, 0, 0, 0))] + [full(a) for a in ws],
        out_specs=pl.BlockSpec((None, nj, nr, HEAD_DIM), lambda b: (b, 0, 0, 0)),
        out_shape=jax.ShapeDtypeStruct((B, nj, nr, HEAD_DIM), BF16),
        compiler_params=_params(("arbitrary",)),
        name="compress",
    )(kv_c, *ws)


def _softmax_cols(s, mask):
    s = jnp.where(mask, s, NEG_INF)
    m = jnp.max(s, axis=0, keepdims=True)
    p = jnp.where(mask, jnp.exp(s - m), 0.0)
    return p / jnp.maximum(jnp.sum(p, axis=0, keepdims=True), 1e-20)


def _attend_kernel(qt_ref, kc_ref, vct_ref, ks_ref, vst_ref, kw_ref, vwt_ref, g_ref, slope_ref, ovt_ref,
                   o_ref, qa_scr, sa_scr, sb_scr, *, n_sel, n_pick, n_cmp_rows):
    QW = GQA_GROUP * Q_BLOCK
    chains = [(k, h) for k in range(ATT_TILES) for h in range(N_KV_HEADS)]
    t0s = [(pl.program_id(1) * ATT_TILES + k) * Q_BLOCK for k in range(ATT_TILES)]
    tqs = [t0 + (lax.broadcasted_iota(jnp.int32, (1, QW), 1) & (Q_BLOCK - 1)) for t0 in t0s]
    cols = lambda k: slice(k * QW, (k + 1) * QW)

    r = lax.broadcasted_iota(jnp.int32, (POS_ROWS, QW), 0)
    qa = {}
    for k, h in chains:
        slope = slope_ref[h]
        qa_scr[k, h, 0:HEAD_DIM, :] = qt_ref[h, :, cols(k)]
        qa_scr[k, h, HEAD_DIM:HEAD_DIM + POS_ROWS, :] = jnp.where(
            r == 0, slope * POS_BASE, jnp.where(r == 1, slope, 0.0)).astype(BF16)
        qa_scr[k, h, HEAD_DIM + POS_ROWS:AUG, :] = jnp.zeros((AUG - HEAD_DIM - POS_ROWS, QW), BF16)
        qa[k, h] = qa_scr[k, h, 0:AUG, :]

    def compressed(k, h, rows):
        sc = _dot(kc_ref[h, 0:rows, :], qa[k, h])
        cpos = lax.broadcasted_iota(jnp.int32, (rows, QW), 0) * CMP_STRIDE + (CMP_BLOCK - 1)
        p_c = _softmax_cols(sc, cpos <= tqs[k])
        o_c = _dot(vct_ref[h, :, 0:rows], p_c.astype(BF16))
        psum = p_c[:, 0:Q_BLOCK]
        for g in range(1, GQA_GROUP):
            psum = psum + p_c[:, g * Q_BLOCK:(g + 1) * Q_BLOCK]
        ph, pl_ = _split(psum)
        return o_c, _dot(ovt_ref[:, 0:rows], ph) + _dot(ovt_ref[:, 0:rows], pl_)

    half = n_cmp_rows // 2
    if half % LANES == 0:
        early = (t0s[-1] + Q_BLOCK - CMP_BLOCK) // CMP_STRIDE < half
        cmp_out = lax.cond(early, lambda _: [compressed(k, h, half) for k, h in chains],
                           lambda _: [compressed(k, h, n_cmp_rows) for k, h in chains], 0)
    else:
        cmp_out = [compressed(k, h, n_cmp_rows) for k, h in chains]
    cmp_out = dict(zip(chains, cmp_out))

    def select(k, h, imp):
        jb = lax.broadcasted_iota(jnp.int32, (n_sel, Q_BLOCK), 0)
        cur = (t0s[k] + lax.broadcasted_iota(jnp.int32, (1, Q_BLOCK), 1)) // SEL_BLOCK
        forced = (jb == 0) | (jb == cur) | (jb == cur - 1)
        imp = jnp.where(forced, imp + FORCE_BONUS, imp)
        imp = jnp.where(jb <= cur, imp, -1.0)
        bias = jnp.full((n_sel, Q_BLOCK), NEG_INF, F32)
        for _ in range(n_pick):
            m = jnp.max(imp, axis=0, keepdims=True)
            first = jnp.min(jnp.where(imp == m, jb, n_sel), axis=0, keepdims=True)
            hit = jb == first
            bias = jnp.where(hit, 0.0, bias)
            imp = jnp.where(hit, -jnp.inf, imp)
        per_unit = SEL_UNIT // SEL_BLOCK
        n_units = n_sel // per_unit
        picked = jnp.where(bias == 0.0, 1.0, 0.0).astype(BF16)
        in_unit = (lax.broadcasted_iota(jnp.int32, (n_units, n_sel), 1) // per_unit
                   == lax.broadcasted_iota(jnp.int32, (n_units, n_sel), 0))
        hits = _dot(jnp.where(in_unit, 1.0, 0.0).astype(BF16), picked)
        live = jnp.where(jnp.max(hits, axis=1, keepdims=True) > 0.5, 1.0, 0.0)
        live = jnp.broadcast_to(live, (n_units, LANES))
        earlier = (lax.broadcasted_iota(jnp.int32, (n_units, n_units), 1)
                   < lax.broadcasted_iota(jnp.int32, (n_units, n_units), 0))
        before = _dot(jnp.where(earlier, 1.0, 0.0).astype(BF16), live.astype(BF16))
        slot = lax.broadcasted_iota(jnp.int32, (n_units, LANES), 1).astype(F32)
        in_slot = jnp.where(before == slot, live, 0.0).astype(BF16)
        r8 = lax.broadcasted_iota(jnp.int32, (SUBLANES, n_units), 0)
        c8 = lax.broadcasted_iota(jnp.int32, (SUBLANES, n_units), 1)
        pick_rows = jnp.where(r8 == 0, c8, jnp.where(r8 == 1, 1, 0)).astype(F32).astype(BF16)
        ids = _dot(pick_rows, in_slot)
        n_live = (before[n_units - 1:n_units, 0:1] + live[n_units - 1:n_units, 0:1])[0, 0]
        bias = jnp.concatenate([bias] * GQA_GROUP, axis=1).astype(BF16)
        qa_scr[k, h, AUG:, :] = jnp.concatenate([bias, jnp.zeros((AUG - n_sel, QW), BF16)], axis=0)
        return ids, n_live

    def window(k, h):
        WK = WINDOW + Q_BLOCK
        w0 = pl.multiple_of(jnp.maximum(t0s[k] - WINDOW, 0), Q_BLOCK)
        sw = _dot(kw_ref[h, pl.ds(w0, WK), :], qa[k, h])
        dist = tqs[k] - (w0 + lax.broadcasted_iota(jnp.int32, (WK, QW), 0))
        in_win = (dist | (WINDOW - 1 - dist)) >= 0
        p_w = _softmax_cols(sw, in_win)
        return _dot(vwt_ref[h, :, pl.ds(w0, WK)], p_w.astype(BF16))

    lists = {c: select(*c, cmp_out[c][1]) for c in chains}
    o_win = {c: window(*c) for c in chains}

    def gathered(k, h, ids, n_slots):
        units = [ids[0, s].astype(jnp.int32) for s in range(n_slots)]
        starts = [pl.multiple_of(u * SEL_UNIT, SEL_UNIT) for u in units]
        keys = jnp.concatenate([ks_ref[h, pl.ds(st, SEL_UNIT), :] for st in starts], axis=0)
        s = _dot(keys, qa_scr[k, h])
        krow = lax.broadcasted_iota(jnp.int32, (SEL_UNIT, QW), 0)
        seen = jnp.concatenate(
            [st + krow <= jnp.where(ids[1, si] > 0.5, tqs[k], -1) for si, st in enumerate(starts)], axis=0)
        s = jnp.where(seen, s, NEG_INF)
        p = jnp.exp(s - jnp.max(s, axis=0, keepdims=True))
        vals = jnp.concatenate([vst_ref[h, :, pl.ds(st, SEL_UNIT)] for st in starts], axis=1)
        return _dot(vals, p.astype(BF16)) / jnp.maximum(jnp.sum(p, axis=0, keepdims=True), 1e-20)

    KC = SEL_KEY_CHUNK

    def swept(k, h):
        last = (t0s[k] + Q_BLOCK + KC - 1) // KC - 1

        def scores_to(ref, j):
            k0 = pl.multiple_of(j * KC, KC)
            ref[...] = _dot(ks_ref[h, pl.ds(k0, KC), :], qa_scr[k, h])

        def accumulate(carry, j, s, p_of):
            m, l, acc = carry
            k0 = pl.multiple_of(j * KC, KC)
            m_new = jnp.maximum(m, jnp.max(s, axis=0, keepdims=True))
            alpha = jnp.exp(m - m_new)
            p = p_of(jnp.exp(s - m_new))
            l = alpha * l + jnp.sum(p, axis=0, keepdims=True)
            acc = alpha * acc + _dot(vst_ref[h, :, pl.ds(k0, KC)], p.astype(BF16))
            return m_new, l, acc

        keep = lambda p: p

        def two_chunks(i, state):
            j = 2 * i
            scores_to(sb_scr, j + 1)
            state = accumulate(state, j, sa_scr[...], keep)
            scores_to(sa_scr, j + 2)
            return accumulate(state, j + 1, sb_scr[...], keep)

        def odd_chunk(state):
            state = accumulate(state, last - 1, sa_scr[...], keep)
            scores_to(sa_scr, last)
            return state

        scores_to(sa_scr, 0)
        init = (jnp.full((1, QW), NEG_INF, F32), jnp.zeros((1, QW), F32), jnp.zeros((HEAD_DIM, QW), F32))
        state = lax.fori_loop(0, last // 2, two_chunks, init)
        state = lax.cond(last % 2 == 1, odd_chunk, lambda st: st, state)
        visible = (last * KC + lax.broadcasted_iota(jnp.int32, (KC, QW), 0)) <= tqs[k]
        _, l_s, acc_s = accumulate(state, last, jnp.where(visible, sa_scr[...], NEG_INF),
                                   lambda p: jnp.where(visible, p, 0.0))
        return acc_s / jnp.maximum(l_s, 1e-20)

    def selected(k, h, ids, n_live, sizes):
        if not sizes:
            return swept(k, h)
        return lax.cond(n_live <= sizes[0], lambda _: gathered(k, h, ids, sizes[0]),
                        lambda _: selected(k, h, ids, n_live, sizes[1:]), 0)

    for k in range(ATT_TILES):
        outs = []
        for h in range(N_KV_HEADS):
            o_s = selected(k, h, *lists[k, h], SEL_SLOTS)
            gk = lambda br: g_ref[h, br:br + 1, cols(k)]
            o = gk(0) * cmp_out[k, h][0] + gk(1) * o_s + gk(2) * o_win[k, h]
            for g in range(0, GQA_GROUP, 2):
                sq = jnp.concatenate([o[:, g * Q_BLOCK:(g + 1) * Q_BLOCK], o[:, (g + 1) * Q_BLOCK:(g + 2) * Q_BLOCK]],
                                     axis=0)
                outs.append(sq.T)
        o_ref[k * Q_BLOCK:(k + 1) * Q_BLOCK, :] = jnp.concatenate(outs, axis=1).astype(BF16)


def _attend(qt, ks, vst, kwn, vwt, gt, kvc_c):
    B, _, S, _ = ks.shape
    nqb = S // Q_BLOCK
    QW = GQA_GROUP * Q_BLOCK
    n_sel = S // SEL_BLOCK
    n_pick = min(SEL_TOPK, n_sel)
    nr = S // CMP_STRIDE
    n_cmp = (S - CMP_BLOCK) // CMP_STRIDE + 1
    assert n_sel % 16 == 0 and n_sel <= AUG

    cpos = np.arange(nr) * CMP_STRIDE + CMP_BLOCK - 1
    digits = np.zeros((nr, AUG - HEAD_DIM), np.float32)
    digits[:, 0] = cpos // POS_BASE
    digits[:, 1] = cpos % POS_BASE
    kc = kvc_c[:, :N_KV_HEADS]
    kc = jnp.concatenate([kc, jnp.broadcast_to(jnp.asarray(digits, BF16), kc.shape[:2] + digits.shape)], axis=-1)
    vct = kvc_c[:, N_KV_HEADS:].transpose(0, 1, 3, 2)
    slopes = 2.0 ** (-8.0 * np.arange(1, N_HEADS + 1) / N_HEADS)
    slope_t = jnp.asarray(np.repeat(slopes.reshape(N_KV_HEADS, GQA_GROUP), Q_BLOCK, axis=1)
                          .reshape(N_KV_HEADS, 1, QW), F32)
    cmp_start = np.arange(n_cmp) * CMP_STRIDE
    sel_start = np.arange(n_sel) * SEL_BLOCK
    ov = ((cmp_start[:, None] <= sel_start[None, :] + SEL_BLOCK - 1)
          & (cmp_start[:, None] + CMP_BLOCK - 1 >= sel_start[None, :])).astype(np.float32)
    ovt = np.zeros((n_sel, nr), np.float32)
    ovt[:, :n_cmp] = ov.T
    ovt = jnp.asarray(ovt, BF16)

    H = N_KV_HEADS
    per_bh = lambda r, c: pl.BlockSpec((None, H, r, c), lambda b, i: (b, 0, 0, 0))
    per_q = lambda r: pl.BlockSpec((None, H, r, ATT_TILES * QW), lambda b, i: (b, 0, 0, i))
    return pl.pallas_call(
        functools.partial(_attend_kernel, n_sel=n_sel, n_pick=n_pick, n_cmp_rows=nr),
        grid=(B, nqb // ATT_TILES),
        in_specs=[per_q(HEAD_DIM), per_bh(nr, AUG), per_bh(HEAD_DIM, nr),
                  per_bh(S, 2 * AUG), per_bh(HEAD_DIM, S), per_bh(S, AUG), per_bh(HEAD_DIM, S),
                  per_q(3),
                  pl.BlockSpec((H, 1, QW), lambda b, i: (0, 0, 0)),
                  pl.BlockSpec((n_sel, nr), lambda b, i: (0, 0))],
        out_specs=pl.BlockSpec((None, ATT_TILES * Q_BLOCK, NSA_WIDTH), lambda b, i: (b, i, 0)),
        out_shape=jax.ShapeDtypeStruct((B, S, NSA_WIDTH), BF16),
        scratch_shapes=[pltpu.VMEM((ATT_TILES, H, 2 * AUG, QW), BF16), pltpu.VMEM((SEL_KEY_CHUNK, QW), F32),
                        pltpu.VMEM((SEL_KEY_CHUNK, QW), F32)],
        compiler_params=_params(("arbitrary", "arbitrary")),
        name="attend",
    )(qt, kc, vct, ks, vst, kwn, vwt, gt, slope_t, ovt)


def _merge_kernel(x_ref, mod_ref, ys_ref, on_ref, gm_ref, wglu_ref, bglu_ref, wa_ref, wb_ref, wo_ref,
                  gain_ref, wrh_ref, wrl_ref, x1_ref, h2_ref, sc_ref):
    D = D_MODEL
    z = jax.nn.gelu(_from_chunk_major(lambda groups, lanes: ys_ref[:, groups, lanes], x_ref.shape[0]))
    glu = z * jax.nn.sigmoid(_dot(z.astype(BF16), wglu_ref[...]) + bglu_ref[...])
    ya = _dot(glu.astype(BF16), wa_ref[...])
    yb = _dot(on_ref[...], wb_ref[...])
    merged = gm_ref[:, :D].astype(F32) * ya + gm_ref[:, D:].astype(F32) * yb
    x1 = x_ref[...] + mod_ref[2:3, :] * _dot(merged.astype(BF16), wo_ref[...])
    x1_ref[...] = x1
    y = x1 * lax.rsqrt(jnp.mean(x1 * x1, axis=-1, keepdims=True) + RMS_EPS)
    h2 = (y * gain_ref[...]) * (1.0 + mod_ref[4:5, :]) + mod_ref[3:4, :]
    h2_ref[...] = h2.astype(BF16)
    hh, hl = _split(h2)
    nt = (((1,), (1,)), ((), ()))
    dg = lambda a, b: lax.dot_general(a, b, nt, preferred_element_type=F32)
    logits = dg(wrh_ref[...], hh) + (dg(wrh_ref[...], hl) + dg(wrl_ref[...], hh))
    sc_ref[...] = jax.nn.sigmoid(logits)


def _merge(x, mod, ys5, o_nsa, gm, w_glu, b_glu, w_a, w_b, w_out, gain_f, w_router):
    B, S, D = x.shape
    tm = min(ROW_TILE, S)
    wrh, wrl = _split(w_router.T)
    ws = [w_glu.astype(BF16), b_glu.reshape(1, -1), w_a.astype(BF16), w_b.astype(BF16), w_out.astype(BF16),
          gain_f, wrh, wrl]
    row = lambda w: pl.BlockSpec((None, tm, w), lambda b, i: (b, i, 0))
    full = lambda a: pl.BlockSpec(a.shape, lambda b, i: (0,) * a.ndim)
    return pl.pallas_call(
        _merge_kernel,
        grid=(B, S // tm),
        in_specs=[row(D), pl.BlockSpec((None, N_MOD, D), lambda b, i: (b, 0, 0)),
                  _s5_chunk_spec(tm), row(NSA_WIDTH), row(2 * D)] + [full(a) for a in ws],
        out_specs=[row(D), row(D), pl.BlockSpec((None, N_EXPERTS, tm), lambda b, i: (b, 0, i))],
        out_shape=[jax.ShapeDtypeStruct((B, S, D), F32), jax.ShapeDtypeStruct((B, S, D), BF16),
                   jax.ShapeDtypeStruct((B, N_EXPERTS, S), F32)],
        compiler_params=_params(("arbitrary", "arbitrary")),
        name="merge",
    )(x, mod, ys5, o_nsa, gm, *ws)


def _first_argmax_rows(v, idx, n):
    m = jnp.max(v, axis=0, keepdims=True)
    first = jnp.min(jnp.where(v == m, idx, n), axis=0, keepdims=True)
    return idx == first, m


def _route_kernel(sc_ref, bias_ref, tri_ref, rank_ref, w_ref, cnt_ref):
    E, NG = N_EXPERTS, N_EXPERT_GROUPS
    GS = E // NG
    sc = sc_ref[...]
    TM = sc.shape[1]
    sel = sc + bias_ref[...]
    i8 = lax.broadcasted_iota(jnp.int32, (GS, TM), 0)
    gscore = []
    for g in range(NG):
        blk = sel[g * GS:(g + 1) * GS, :]
        hit, m1 = _first_argmax_rows(blk, i8, GS)
        m2 = jnp.max(jnp.where(hit, -jnp.inf, blk), axis=0, keepdims=True)
        gscore.append(m1 + m2)
    gscore = jnp.concatenate(gscore, axis=0)
    ig = lax.broadcasted_iota(jnp.int32, (NG, TM), 0)
    gmask = jnp.zeros((NG, TM), F32)
    for _ in range(TOPK_EXPERT_GROUPS):
        hit, _m = _first_argmax_rows(gscore, ig, NG)
        gmask = jnp.where(hit, 1.0, gmask)
        gscore = jnp.where(hit, -jnp.inf, gscore)
    emask = jnp.concatenate([jnp.broadcast_to(gmask[g:g + 1, :], (GS, TM)) for g in range(NG)], axis=0)
    cand = jnp.where(emask > 0.5, sel, NEG_INF)
    ie = lax.broadcasted_iota(jnp.int32, (E, TM), 0)
    chosen = jnp.zeros((E, TM), F32)
    for _ in range(TOP_K):
        hit, _m = _first_argmax_rows(cand, ie, E)
        chosen = jnp.where(hit, 1.0, chosen)
        cand = jnp.where(hit, -jnp.inf, cand)
    w = chosen * sc
    w = w / jnp.sum(w, axis=0, keepdims=True) * ROUTED_SCALE
    sub = rank_ref.shape[-1]
    for q in range(TM // sub):
        cols = slice(q * sub, (q + 1) * sub)
        cb = chosen[:, cols].astype(BF16)
        prefix = _dot(cb, tri_ref[...])
        rank_ref[q] = jnp.where(chosen[:, cols] > 0.5, prefix, RANK_NONE)
        w_ref[q] = w[:, cols]
        cnt_ref[q] = _dot(cb, jnp.ones((sub, LANES), BF16))


def _route(scores_t, router_bias):
    B, E, S = scores_t.shape
    TM = min(ROUTE_TILE, S)
    per_step = min(MOE_OUTER, S) // TM
    nj = S // (TM * per_step)
    ns = B * nj * per_step
    tri = jnp.asarray(np.triu(np.ones((TM, TM), np.float32), k=1), BF16)
    tile = lambda w: pl.BlockSpec((per_step, E, w), lambda b, j: (b * nj + j, 0, 0))
    return pl.pallas_call(
        _route_kernel,
        grid=(B, nj),
        in_specs=[pl.BlockSpec((None, E, TM * per_step), lambda b, j: (b, 0, j)),
                  pl.BlockSpec((E, 1), lambda b, j: (0, 0)),
                  pl.BlockSpec((TM, TM), lambda b, j: (0, 0))],
        out_specs=[tile(TM), tile(TM), tile(LANES)],
        out_shape=[jax.ShapeDtypeStruct((ns, E, TM), F32), jax.ShapeDtypeStruct((ns, E, TM), F32),
                   jax.ShapeDtypeStruct((ns, E, LANES), F32)],
        compiler_params=_params(("arbitrary", "arbitrary")),
        name="route",
    )(scores_t, router_bias.reshape(E, 1), tri)


def _moe_kernel(cnt_ref, h_ref, x1_ref, mod_ref, rank_ref, w_ref, wg_ref, wu_ref, wd_ref,
                sg_ref, su_ref, sd_ref, o_ref, xc, yc, pc, wc, *, n_sub, tm, n_exp):
    to = pl.program_id(0)
    eb = pl.program_id(1)
    gate = mod_ref[5:6, :]

    @pl.when(eb == 0)
    def _shared():
        for s in range(n_sub):
            rows = pl.ds(s * tm, tm)
            hs = h_ref[rows, :]
            hid = jax.nn.silu(_dot(hs, sg_ref[...])) * _dot(hs, su_ref[...])
            o_ref[rows, :] = x1_ref[rows, :] + gate * _dot(hid.astype(BF16), sd_ref[...])
            xc[s] = jnp.zeros(xc.shape[1:], BF16)
            yc[s] = jnp.zeros(yc.shape[1:], BF16)
            wc[s] = jnp.zeros(wc.shape[1:], F32)

    PIECE, WIN = MOE_PIECE, MOE_WIN
    tn = (((0,), (0,)), ((), ()))
    experts = [eb * n_exp + j for j in range(n_exp)]
    cnt = [[cnt_ref[(to * n_sub + s) * N_EXPERTS + e] for e in experts] for s in range(n_sub)]
    off = []
    for s in range(n_sub):
        o = [jnp.int32(0)]
        for j in range(n_exp):
            o.append(o[-1] + ((cnt[s][j] + 15) // 16) * 16)
        off.append(o)
    n_pieces = [(off[s][n_exp] + PIECE - 1) // PIECE for s in range(n_sub)]
    fits = n_pieces[0] < MOE_ROWS // PIECE
    for s in range(1, n_sub):
        fits = jnp.logical_and(fits, n_pieces[s] < MOE_ROWS // PIECE)

    def one_hot_rows(s, base, rows_n, js):
        slot = (lax.broadcasted_iota(jnp.int32, (rows_n, tm), 0) + base).astype(F32)
        pick = jnp.zeros((rows_n, tm), F32)
        wacc = jnp.zeros((rows_n, tm), F32)
        for j in js:
            hit = (slot - off[s][j].astype(F32)) == rank_ref[s, pl.ds(experts[j], 1), :]
            pick = jnp.where(hit, 1.0, pick)
            wacc = jnp.where(hit, w_ref[s, pl.ds(experts[j], 1), :], wacc)
        return pick.astype(BF16), jnp.sum(wacc, axis=1, keepdims=True)

    def mlp(xg, j):
        hid = jax.nn.silu(_dot(xg, wg_ref[j])) * _dot(xg, wu_ref[j])
        return _dot(hid.astype(BF16), wd_ref[j])

    def gather(s, start, n):
        picks = []
        for c in range(0, n, PIECE // 2):
            m = min(PIECE // 2, n - c)
            pick, wrow = one_hot_rows(s, start + c, m, range(n_exp))
            wc[s, pl.ds(start + c, m), :] = wrow
            picks.append(pick)
        if n % PIECE:
            picks.append(jnp.zeros((PIECE - n % PIECE, tm), BF16))
        pick = jnp.concatenate(picks, axis=0)
        xc[s, pl.ds(start, n), :] = _dot(pick[:n], h_ref[pl.ds(s * tm, tm), :]).astype(BF16)
        for q in range(0, pick.shape[0], PIECE):
            pc[s, :, pl.ds(start + q, PIECE)] = pick[q:q + PIECE].T

    def window(j, p, first):
        starts = [pl.multiple_of(jnp.minimum(off[s][j] + p * WIN, (n_pieces[s] + 1) * PIECE - WIN), 16)
                  for s in range(n_sub)]
        xg = jnp.concatenate([xc[s, pl.ds(starts[s], WIN), :] for s in range(n_sub)], axis=0)
        out = mlp(xg, j)
        for s in range(n_sub):
            r = pl.ds(starts[s], WIN)
            new = (out[s * WIN:(s + 1) * WIN] * wc[s, r, :] * gate).astype(BF16)
            if not first:
                rank = starts[s] - off[s][j] + lax.broadcasted_iota(jnp.int32, (WIN, 1), 0)
                new = jnp.where(rank < cnt[s][j], new, yc[s, r, :])
            yc[s, r, :] = new

    def scatter(s, start, n):
        r = pl.ds(start, n)
        o_ref[pl.ds(s * tm, tm), :] += _dot(pc[s, :, r], yc[s, r, :])

    @pl.when(fits)
    def _packed():
        for s in range(n_sub):
            gather(s, 0, MOE_BASE_ROWS)
        for s in range(n_sub):
            @pl.when(off[s][n_exp] > MOE_BASE_ROWS)
            def _second(s=s):
                gather(s, PIECE, PIECE)

            @pl.when(n_pieces[s] > 2)
            def _third(s=s):
                gather(s, 2 * PIECE, PIECE)
        for j in range(n_exp):
            window(j, 0, True)
        for j in range(n_exp):
            most = cnt[0][j]
            for s in range(1, n_sub):
                most = jnp.maximum(most, cnt[s][j])

            @pl.when(most > WIN)
            def _more(j=j, most=most):
                lax.fori_loop(1, (most + WIN - 1) // WIN, lambda p, c: (window(j, p, False), c)[1], 0)
        for s in range(n_sub):
            scatter(s, 0, 2 * PIECE)
        for s in range(n_sub):
            @pl.when(n_pieces[s] > 2)
            def _third_out(s=s):
                scatter(s, 2 * PIECE, PIECE)

    @pl.when(jnp.logical_not(fits))
    def _unpacked():
        for j in range(n_exp):
            for s in range(n_sub):
                rows = pl.ds(s * tm, tm)

                def block(bi, carry, j=j, s=s, rows=rows):
                    slot0 = off[s][j] + bi * PIECE
                    pick, wrow = one_hot_rows(s, slot0, PIECE, [j])
                    out = mlp(_dot(pick, h_ref[rows, :]).astype(BF16), j)
                    ow = (out * wrow * gate).astype(BF16)
                    o_ref[rows, :] += lax.dot_general(pick, ow, tn, preferred_element_type=F32)
                    return carry

                lax.fori_loop(0, (cnt[s][j] + PIECE - 1) // PIECE, block, 0)


def _moe(h2, x1, mod, rank_t, w_t, counts, w_gate, w_up, w_down, ws_gate, ws_up, ws_down, seq):
    T, D = h2.shape
    ns, E, tm = rank_t.shape
    tmo = min(MOE_OUTER, seq)
    n_sub = tmo // tm
    n_exp = MOE_EXPERTS_PER_STEP
    per_b = seq // tmo
    ws = [w_gate.astype(BF16), w_up.astype(BF16), w_down.astype(BF16)]
    sh = [ws_gate.astype(BF16), ws_up.astype(BF16), ws_down.astype(BF16)]
    tok = lambda: pl.BlockSpec((tmo, D), lambda t, e, c: (t, 0))
    sub = lambda: pl.BlockSpec((n_sub, E, tm), lambda t, e, c: (t, 0, 0))
    exp = lambda a: pl.BlockSpec((n_exp,) + a.shape[1:], lambda t, e, c: (e, 0, 0))
    full = lambda a: pl.BlockSpec(a.shape, lambda t, e, c: (0, 0))
    grid_spec = pltpu.PrefetchScalarGridSpec(
        num_scalar_prefetch=1,
        grid=(T // tmo, E // n_exp),
        in_specs=[tok(), tok(), pl.BlockSpec((None, N_MOD, D), lambda t, e, c: (t // per_b, 0, 0)),
                  sub(), sub(), exp(ws[0]), exp(ws[1]), exp(ws[2]), full(sh[0]), full(sh[1]), full(sh[2])],
        out_specs=tok(),
        scratch_shapes=[pltpu.VMEM((n_sub, MOE_ROWS, D), BF16), pltpu.VMEM((n_sub, MOE_ROWS, D), BF16),
                        pltpu.VMEM((n_sub, tm, MOE_ROWS), BF16), pltpu.VMEM((n_sub, MOE_ROWS, 1), F32)],
    )
    return pl.pallas_call(
        functools.partial(_moe_kernel, n_sub=n_sub, tm=tm, n_exp=n_exp),
        grid_spec=grid_spec,
        out_shape=jax.ShapeDtypeStruct((T, D), F32),
        compiler_params=_params(("arbitrary", "arbitrary")),
        name="moe",
    )(counts, h2, x1, mod, rank_t, w_t, *ws, *sh)


def kernel(x, c, w_ada, b_ada, norm_mix_gain, norm_ffn_gain, w_in, s5_lambda_re, s5_lambda_im, s5_log_dt, s5_b_re, s5_b_im, s5_c_re, s5_c_im, s5_d, s5_w_glu, s5_b_glu, q_norm_gain, k_norm_gain, cmp_pe, cmp_w1, cmp_b1, cmp_w2, cmp_b2, w_branch_a, w_branch_b, w_out, w_router, router_bias, w_gate, w_up, w_down, ws_gate, ws_up, ws_down):
    B, S, D = x.shape
    for l in range(w_ada.shape[0]):
        mod = _ada(c, w_ada[l], b_ada[l]).reshape(B, N_MOD, D)
        u, kvc, qt, ks, vst, kwn, vwt, gt, gm = _inproj(x, mod, norm_mix_gain[l:l + 1], w_in[l],
                                                        q_norm_gain[l], k_norm_gain[l])
        tables = _s5_tables(s5_lambda_re[l], s5_lambda_im[l], s5_log_dt[l], s5_b_re[l], s5_b_im[l],
                            s5_c_re[l], s5_c_im[l], s5_d[l])
        ys5 = _s5(u, tables)
        kvc_c = _compress(kvc, cmp_pe[l], cmp_w1[l], cmp_b1[l], cmp_w2[l], cmp_b2[l], k_norm_gain[l, 0])
        o_nsa = _attend(qt, ks, vst, kwn, vwt, gt, kvc_c)
        x1, h2, scores_t = _merge(x, mod, ys5, o_nsa, gm, s5_w_glu[l], s5_b_glu[l], w_branch_a[l],
                                  w_branch_b[l], w_out[l], norm_ffn_gain[l:l + 1], w_router[l])
        rank_t, w_t, cnt = _route(scores_t, router_bias[l])
        counts = cnt[:, :, 0].astype(jnp.int32).reshape(-1)
        x = _moe(h2.reshape(B * S, D), x1.reshape(B * S, D), mod, rank_t, w_t, counts,
                 w_gate[l], w_up[l], w_down[l], ws_gate[l], ws_up[l], ws_down[l], S).reshape(B, S, D)
    return x
```
